```python
import jax, jax.numpy as jnp
from jax import lax
import numpy as np

D_MODEL = 1024
BATCH = 8
SEQ = 2048
DEPTH = 1
DEC_BATCH = 128
DEC_SEQ = 1
PAST_LEN = 16384
PAGE_SIZE = 128

RET_HEADS = 4
RET_QK_DIM = D_MODEL // 16
RET_V_DIM = D_MODEL // 8
RET_WIDTH = RET_HEADS * RET_V_DIM
CHUNK = 128
ROPE_BASE = 10000.0
POOL_WINDOWS = (2, 4, 8, 16)
POOL_GROUPS = len(POOL_WINDOWS)
POOL_WIDTH = D_MODEL - RET_WIDTH
POOL_GROUP_DIM = POOL_WIDTH // POOL_GROUPS
POOL_BUF = max(POOL_WINDOWS) - 1
Q_W = RET_HEADS * RET_QK_DIM
K_W = RET_HEADS * RET_QK_DIM
V_W = RET_WIDTH
G_W = RET_WIDTH
IN_WIDTH = Q_W + K_W + V_W + G_W + POOL_WIDTH
SPLITS = [Q_W, Q_W + K_W, Q_W + K_W + V_W, Q_W + K_W + V_W + G_W]
MIX_WIDTH = RET_WIDTH + POOL_WIDTH
N_EXPERTS = 64
TOP_K = 8
N_EXPERT_GROUPS = 8
TOP_GROUPS = 4
EXPERT_DIM = D_MODEL // 4
SHARED_DIM = EXPERT_DIM
ROUTE_SCALE = 2.5
MOE_BLOCK = 1024
EPS = 1e-6

kernel_name = 'hybrid_retention_pool_moe_step'


def rmsnorm(x, g):
    xf = x.astype(jnp.float32)
    y = xf * lax.rsqrt(jnp.mean(xf * xf, axis=-1, keepdims=True) + EPS)
    return (y * g.astype(jnp.float32)).astype(x.dtype)


def rotary(x, pos):
    half = x.shape[-1] // 2
    freqs = ROPE_BASE ** (-jnp.arange(half, dtype=jnp.float32) / half)
    ang = pos[:, None] * freqs[None, :]
    cos = jnp.cos(ang)[None, :, None, :]
    sin = jnp.sin(ang)[None, :, None, :]
    x1, x2 = x[..., :half], x[..., half:]
    return jnp.concatenate([x1 * cos - x2 * sin, x1 * sin + x2 * cos], axis=-1)


def ret_log_decay():
    return jnp.log(1.0 - 2.0 ** (-5.0 - jnp.arange(RET_HEADS, dtype=jnp.float32)))


def retention_chunk(s0, q, k, v):
    c = q.shape[1]
    lg = ret_log_decay()
    idx = jnp.arange(c, dtype=jnp.float32)
    diff = idx[:, None] - idx[None, :]
    causal = diff >= 0
    dmat = jnp.where(causal[None], jnp.exp(lg[:, None, None] * jnp.where(causal, diff, 0.0)[None]), 0.0)
    scores = jnp.einsum('bihd,bjhd->bhij', q, k) * dmat[None]
    inner = jnp.einsum('bhij,bjhe->bihe', scores, v)
    cross_decay = jnp.exp(lg[None, :] * (idx[:, None] + 1.0))
    cross = jnp.einsum('bihd,bhde->bihe', q, s0) * cross_decay[None, :, :, None]
    tail_decay = jnp.exp(lg[None, :] * (c - 1.0 - idx)[:, None])
    s_new = jnp.exp(lg * c)[None, :, None, None] * s0 + jnp.einsum('bjhd,bjhe,jh->bhde', k, v, tail_decay)
    return inner + cross, s_new


def retention(q, k, v, s0, chunk):
    b, l, h, _ = q.shape
    nc = l // chunk

    def to_chunks(t):
        return jnp.moveaxis(t.reshape(b, nc, chunk, h, t.shape[-1]), 1, 0)

    def step(s, qkv):
        o, s_new = retention_chunk(s, *qkv)
        return s_new, o

    s_fin, o = lax.scan(step, s0, (to_chunks(q), to_chunks(k), to_chunks(v)))
    o = jnp.moveaxis(o, 0, 1).reshape(b, l, h, v.shape[-1])
    return o, s_fin


def pool_mix(u, buf, start, w_pool, pool_scale):
    b, l, _ = u.shape
    ext = jnp.concatenate([buf.astype(u.dtype), u], axis=1)
    extf = ext.astype(jnp.float32)
    cs = jnp.concatenate([jnp.zeros((b, 1, POOL_WIDTH), jnp.float32), jnp.cumsum(extf, axis=1)], axis=1)
    pos = start + jnp.arange(l, dtype=jnp.float32)
    hi = cs[:, POOL_BUF + 1:]
    groups = []
    for gi, w in enumerate(POOL_WINDOWS):
        sl = slice(gi * POOL_GROUP_DIM, (gi + 1) * POOL_GROUP_DIM)
        lo = cs[:, POOL_BUF + 1 - w:POOL_BUF + 1 - w + l, sl]
        cnt = jnp.minimum(pos + 1.0, float(w))
        groups.append((hi[..., sl] - lo) / cnt[None, :, None] - extf[:, POOL_BUF:, sl])
    p = jnp.stack(groups, axis=2)
    p = jnp.einsum('blgc,gcd->blgd', p, w_pool.astype(jnp.float32)).reshape(b, l, POOL_WIDTH)
    p = p * pool_scale.astype(jnp.float32)
    return p.astype(u.dtype), ext[:, -POOL_BUF:]


def mixer(h, s_ret, s_pool, start, chunk, w_in, w_out, w_pool, pool_scale):
    b, l, _ = h.shape
    proj = jnp.einsum('bld,de->ble', h, w_in)
    q, k, v, g, u = jnp.split(proj, SPLITS, axis=-1)
    pos = start + jnp.arange(l, dtype=jnp.float32)
    q = rotary(q.reshape(b, l, RET_HEADS, RET_QK_DIM).astype(jnp.float32), pos)
    k = rotary(k.reshape(b, l, RET_HEADS, RET_QK_DIM).astype(jnp.float32), pos) * (RET_QK_DIM ** -0.5)
    v = v.reshape(b, l, RET_HEADS, RET_V_DIM).astype(jnp.float32)
    o, s_new = retention(q, k, v, s_ret.astype(jnp.float32), chunk)
    mu = jnp.mean(o, axis=-1, keepdims=True)
    var = jnp.mean(jnp.square(o - mu), axis=-1, keepdims=True)
    o = ((o - mu) * lax.rsqrt(var + EPS)).reshape(b, l, RET_WIDTH)
    o = (jax.nn.silu(g.astype(jnp.float32)) * o).astype(h.dtype)
    p, buf_new = pool_mix(u, s_pool, start, w_pool, pool_scale)
    y = jnp.einsum('ble,ed->bld', jnp.concatenate([o, p], axis=-1), w_out)
    return y, s_new.astype(s_ret.dtype), buf_new.astype(s_pool.dtype)


def moe_block(t, w_router, router_bias, w_eg, w_eu, w_ed, w_sg, w_su, w_sd):
    n = t.shape[0]
    scores = jax.nn.sigmoid(jnp.dot(t.astype(jnp.float32), w_router.astype(jnp.float32)))
    biased = scores + router_bias.astype(jnp.float32)
    grp = biased.reshape(n, N_EXPERT_GROUPS, N_EXPERTS // N_EXPERT_GROUPS)
    grp_score = lax.top_k(grp, 2)[0].sum(-1)
    _, gidx = lax.top_k(grp_score, TOP_GROUPS)
    gmask = jax.nn.one_hot(gidx, N_EXPERT_GROUPS, dtype=jnp.float32).sum(1) > 0
    emask = jnp.repeat(gmask, N_EXPERTS // N_EXPERT_GROUPS, axis=-1)
    _, eidx = lax.top_k(jnp.where(emask, biased, -jnp.inf), TOP_K)
    sel = jnp.take_along_axis(scores, eidx, axis=-1)
    wts = sel / jnp.sum(sel, axis=-1, keepdims=True) * ROUTE_SCALE
    gates = jnp.einsum('tk,tke->te', wts, jax.nn.one_hot(eidx, N_EXPERTS, dtype=jnp.float32))
    hg = jnp.einsum('td,edf->tef', t, w_eg)
    hu = jnp.einsum('td,edf->tef', t, w_eu)
    a = jax.nn.silu(hg) * hu * gates.astype(t.dtype)[:, :, None]
    routed = jnp.einsum('tef,efd->td', a, w_ed)
    shared = jnp.dot(jax.nn.silu(jnp.dot(t, w_sg)) * jnp.dot(t, w_su), w_sd)
    return routed + shared


def moe(h, w_router, router_bias, w_eg, w_eu, w_ed, w_sg, w_su, w_sd):
    b, l, d = h.shape
    n = b * l
    blk = min(MOE_BLOCK, n)
    nb = -(-n // blk)
    flat = jnp.pad(h.reshape(n, d), ((0, nb * blk - n), (0, 0)))
    out = lax.map(lambda t: moe_block(t, w_router, router_bias, w_eg, w_eu, w_ed, w_sg, w_su, w_sd),
                  flat.reshape(nb, blk, d))
    return out.reshape(nb * blk, d)[:n].reshape(b, l, d)


def trunk(x, c, s_ret, s_pool, start, chunk, norm1, norm2, norm_f, w_ada, b_ada, w_in, w_out, w_pool,
          pool_scale, w_router, router_bias, w_eg, w_eu, w_ed, w_sg, w_su, w_sd):
    cs = jax.nn.silu(c)
    new_ret, new_pool = [], []
    for li in range(DEPTH):
        mod = jnp.dot(cs, w_ada[li]) + b_ada[li]
        sh1, sc1, g1, sh2, sc2, g2 = jnp.split(mod, 6, axis=-1)
        h = rmsnorm(x, norm1[li]) * (1.0 + sc1[:, None]) + sh1[:, None]
        y, r_new, p_new = mixer(h, s_ret[li], s_pool[li], start, chunk, w_in[li], w_out[li], w_pool[li], pool_scale[li])
        x = x + g1[:, None] * y
        h = rmsnorm(x, norm2[li]) * (1.0 + sc2[:, None]) + sh2[:, None]
        x = x + g2[:, None] * moe(h, w_router[li], router_bias[li], w_eg[li], w_eu[li], w_ed[li],
                                  w_sg[li], w_su[li], w_sd[li])
        new_ret.append(r_new)
        new_pool.append(p_new)
    return rmsnorm(x, norm_f), jnp.stack(new_ret), jnp.stack(new_pool)


def setup_inputs(seed: int = 0) -> dict:
    key = jax.random.key(seed)
    ks = jax.random.split(key, 28)
    f32 = jnp.float32

    def nrm(k, shape, scale):
        return jax.random.normal(k, shape, f32) * scale

    return {
        'x_prompt': nrm(ks[0], (BATCH, SEQ, D_MODEL), 1.0),
        'x_sample': nrm(ks[1], (DEC_BATCH, DEC_SEQ, D_MODEL), 1.0),
        'c_prompt': nrm(ks[2], (BATCH, D_MODEL), 1.0),
        'c_sample': nrm(ks[3], (DEC_BATCH, D_MODEL), 1.0),
        'state_ret': nrm(ks[4], (DEPTH, DEC_BATCH, RET_HEADS, RET_QK_DIM, RET_V_DIM), 1.0),
        'state_pool': nrm(ks[5], (DEPTH, DEC_BATCH, POOL_BUF, POOL_WIDTH), 1.0),
        'norm1': 1.0 + nrm(ks[6], (DEPTH, D_MODEL), 0.02),
        'norm2': 1.0 + nrm(ks[7], (DEPTH, D_MODEL), 0.02),
        'norm_f': 1.0 + nrm(ks[8], (D_MODEL,), 0.02),
        'w_ada': nrm(ks[9], (DEPTH, D_MODEL, 6 * D_MODEL), 0.02),
        'b_ada': nrm(ks[10], (DEPTH, 6 * D_MODEL), 0.02),
        'w_in': nrm(ks[11], (DEPTH, D_MODEL, IN_WIDTH), D_MODEL ** -0.5),
        'w_out': nrm(ks[12], (DEPTH, MIX_WIDTH, D_MODEL), MIX_WIDTH ** -0.5),
        'w_pool': nrm(ks[13], (DEPTH, POOL_GROUPS, POOL_GROUP_DIM, POOL_GROUP_DIM), POOL_GROUP_DIM ** -0.5),
        'pool_scale': 1.0 + nrm(ks[14], (DEPTH, POOL_WIDTH), 0.1),
        'w_router': nrm(ks[15], (DEPTH, D_MODEL, N_EXPERTS), D_MODEL ** -0.5),
        'router_bias': nrm(ks[16], (DEPTH, N_EXPERTS), 0.01),
        'w_exp_gate': nrm(ks[17], (DEPTH, N_EXPERTS, D_MODEL, EXPERT_DIM), D_MODEL ** -0.5),
        'w_exp_up': nrm(ks[18], (DEPTH, N_EXPERTS, D_MODEL, EXPERT_DIM), D_MODEL ** -0.5),
        'w_exp_down': nrm(ks[19], (DEPTH, N_EXPERTS, EXPERT_DIM, D_MODEL), EXPERT_DIM ** -0.5),
        'w_sh_gate': nrm(ks[20], (DEPTH, D_MODEL, SHARED_DIM), D_MODEL ** -0.5),
        'w_sh_up': nrm(ks[21], (DEPTH, D_MODEL, SHARED_DIM), D_MODEL ** -0.5),
        'w_sh_down': nrm(ks[22], (DEPTH, SHARED_DIM, D_MODEL), SHARED_DIM ** -0.5),
    }


def reference(x_prompt, x_sample, c_prompt, c_sample, state_ret, state_pool, norm1, norm2, norm_f,
              w_ada, b_ada, w_in, w_out, w_pool, pool_scale, w_router, router_bias, w_exp_gate,
              w_exp_up, w_exp_down, w_sh_gate, w_sh_up, w_sh_down):
    b = x_prompt.shape[0]
    zero_ret = jnp.zeros((DEPTH, b, RET_HEADS, RET_QK_DIM, RET_V_DIM), state_ret.dtype)
    zero_pool = jnp.zeros((DEPTH, b, POOL_BUF, POOL_WIDTH), state_pool.dtype)
    y_prompt, ret_p, pool_p = trunk(x_prompt, c_prompt, zero_ret, zero_pool, 0, min(CHUNK, x_prompt.shape[1]),
                                    norm1, norm2, norm_f, w_ada, b_ada, w_in, w_out, w_pool, pool_scale,
                                    w_router, router_bias, w_exp_gate, w_exp_up, w_exp_down,
                                    w_sh_gate, w_sh_up, w_sh_down)
    y_sample, ret_s, pool_s = trunk(x_sample, c_sample, state_ret, state_pool, PAST_LEN, x_sample.shape[1],
                                    norm1, norm2, norm_f, w_ada, b_ada, w_in, w_out, w_pool, pool_scale,
                                    w_router, router_bias, w_exp_gate, w_exp_up, w_exp_down,
                                    w_sh_gate, w_sh_up, w_sh_down)
    return (y_prompt, y_sample, ret_p, pool_p, ret_s, pool_s)
```

```python
import functools

import jax
import jax.numpy as jnp
import numpy as np
from jax import lax
from jax.experimental import pallas as pl
from jax.experimental.pallas import tpu as pltpu

D_MODEL = 1024
RET_HEADS = 4
RET_QK_DIM = 64
RET_V_DIM = 128
RET_WIDTH = RET_HEADS * RET_V_DIM
QK_WIDTH = RET_HEADS * RET_QK_DIM
ROPE_BASE = 10000.0
POOL_WINDOWS = (2, 4, 8, 16)
POOL_WIDTH = 512
POOL_GROUP_DIM = 128
POOL_BUF = 15
IN_WIDTH = 2 * QK_WIDTH + 2 * RET_WIDTH + POOL_WIDTH
N_EXPERTS = 64
TOP_K = 8
N_EXPERT_GROUPS = 8
GROUP_SIZE = N_EXPERTS // N_EXPERT_GROUPS
TOP_GROUPS = 4
EXPERT_DIM = 256
ROUTE_SCALE = 2.5
EPS = 1e-6

LANES = 128
POOL_CARRY = 16
VMEM_LIMIT = 56 * 1024 * 1024

BF16 = jnp.bfloat16
F32 = jnp.float32


def _silu(x):
    return x * jax.nn.sigmoid(x)


def _dot(a, b):
    return jnp.dot(a, b, preferred_element_type=F32)


def _rms(x):
    return x * lax.rsqrt(jnp.mean(x * x, axis=-1, keepdims=True) + EPS)


def _mod(mod_ref, i):
    if len(mod_ref.shape) == 3:
        return mod_ref[0, i:i + 1, :]
    return mod_ref[:, i * D_MODEL:(i + 1) * D_MODEL]


def _split_bf16(x):
    hi = x.astype(BF16)
    lo = (x - hi.astype(F32)).astype(BF16)
    return hi, lo


def _first_max_onehot(work, idx, n):
    m = jnp.max(work, axis=0, keepdims=True)
    first = jnp.min(jnp.where(work == m, idx, float(n)), axis=0, keepdims=True)
    return idx == first


def _route(h2, wr_t_ref, bias_t_ref):
    n = h2.shape[0]
    h_hi, h_lo = _split_bf16(h2)
    w_hi, w_lo = _split_bf16(wr_t_ref[...])
    nt = (((1,), (1,)), ((), ()))
    logits = (lax.dot_general(w_hi, h_hi, nt, preferred_element_type=F32)
              + lax.dot_general(w_hi, h_lo, nt, preferred_element_type=F32)
              + lax.dot_general(w_lo, h_hi, nt, preferred_element_type=F32))
    scores = jax.nn.sigmoid(logits)
    biased = scores + bias_t_ref[:, 0:1]
    b3 = biased.reshape(N_EXPERT_GROUPS, GROUP_SIZE, n)
    i3 = lax.broadcasted_iota(jnp.int32, b3.shape, 1).astype(F32)
    m1 = jnp.max(b3, axis=1, keepdims=True)
    first = jnp.min(jnp.where(b3 == m1, i3, float(GROUP_SIZE)), axis=1, keepdims=True)
    m2 = jnp.max(jnp.where(i3 == first, -jnp.inf, b3), axis=1, keepdims=True)
    gscore = (m1 + m2).reshape(N_EXPERT_GROUPS, n)
    gidx = lax.broadcasted_iota(jnp.int32, gscore.shape, 0).astype(F32)
    gsel = jnp.zeros(gscore.shape, F32)
    work = gscore
    for _ in range(TOP_GROUPS):
        hit = _first_max_onehot(work, gidx, N_EXPERT_GROUPS)
        gsel = jnp.where(hit, 1.0, gsel)
        work = jnp.where(hit, -jnp.inf, work)
    gsel3 = jnp.broadcast_to(gsel.reshape(N_EXPERT_GROUPS, 1, n), b3.shape)
    work = jnp.where(gsel3 > 0.0, b3, -jnp.inf).reshape(N_EXPERTS, n)
    eidx = lax.broadcasted_iota(jnp.int32, work.shape, 0).astype(F32)
    sel = jnp.zeros(work.shape, F32)
    for _ in range(TOP_K):
        hit = _first_max_onehot(work, eidx, N_EXPERTS)
        sel = jnp.where(hit, 1.0, sel)
        work = jnp.where(hit, -jnp.inf, work)
    picked = jnp.where(sel > 0.0, scores, 0.0)
    gates_t = picked / jnp.sum(picked, axis=0, keepdims=True) * ROUTE_SCALE
    padded = jnp.concatenate([gates_t, jnp.zeros((LANES - N_EXPERTS, n), F32)], axis=0)
    return padded.T


def _group_norm_gate(o, g):
    parts = []
    for h in range(RET_HEADS):
        oh = o[:, h * RET_V_DIM:(h + 1) * RET_V_DIM]
        mu = jnp.mean(oh, axis=-1, keepdims=True)
        ctr = oh - mu
        var = jnp.mean(ctr * ctr, axis=-1, keepdims=True)
        parts.append(ctr * lax.rsqrt(var + EPS))
    return _silu(g) * jnp.concatenate(parts, axis=-1)


def _pool_project(pooled, w_pool_ref, pool_scale_ref):
    parts = [_dot(p.astype(BF16), w_pool_ref[gi].astype(BF16)) for gi, p in enumerate(pooled)]
    return jnp.concatenate(parts, axis=-1) * pool_scale_ref[...]


def _mix_tail(x, o_gated, p, mod_ref, w_out_ref, norm2_ref, wr_t_ref, bias_t_ref,
              x1_ref, h2_ref, gates_ref):
    mix = jnp.concatenate([o_gated, p], axis=-1).astype(BF16)
    y = _dot(mix, w_out_ref[...])
    x1 = x + _mod(mod_ref, 2) * y
    h2 = _rms(x1) * norm2_ref[...] * (1.0 + _mod(mod_ref, 4)) + _mod(mod_ref, 3)
    x1_ref[...] = x1.reshape(x1_ref.shape)
    h2_ref[...] = h2.astype(BF16).reshape(h2_ref.shape)
    gates_ref[...] = _route(h2, wr_t_ref, bias_t_ref).reshape(gates_ref.shape)


def _ada_kernel(c_ref, w_ref, b_ref, o_ref):
    cs = _silu(c_ref[...]).astype(BF16)
    o_ref[...] = _dot(cs, w_ref[...].astype(BF16)) + b_ref[...]


def _ada(c_all, w_ada, b_ada, block_n=1536):
    n, d = c_all.shape
    width = w_ada.shape[1]
    return pl.pallas_call(
        _ada_kernel,
        grid=(width // block_n,),
        in_specs=[pl.BlockSpec((n, d), lambda j: (0, 0)),
                  pl.BlockSpec((d, block_n), lambda j: (0, j)),
                  pl.BlockSpec((1, block_n), lambda j: (0, j))],
        out_specs=pl.BlockSpec((n, block_n), lambda j: (0, j)),
        out_shape=jax.ShapeDtypeStruct((n, width), F32),
        compiler_params=pltpu.CompilerParams(vmem_limit_bytes=VMEM_LIMIT),
        name="ada",
    )(c_all, w_ada, b_ada.reshape(1, width))


def _mix_prompt_kernel(x_ref, mod_ref, norm1_ref, w_in_ref, cos_ref, sin_ref, dmat_ref, cross_ref,
                       tail_ref, cdec_ref, w_pool_ref, pool_scale_ref, w_out_ref, norm2_ref,
                       wr_t_ref, bias_t_ref,
                       x1_ref, h2_ref, gates_ref, ret_ref, pool_ref,
                       state_ref, ext_ref, o_ref, *, block_l, chunk):
    li = pl.program_id(1)

    @pl.when(li == 0)
    def _():
        state_ref[...] = jnp.zeros_like(state_ref)
        ext_ref[0:POOL_CARRY, :] = jnp.zeros((POOL_CARRY, POOL_WIDTH), F32)

    x = x_ref[0]
    h = _rms(x) * norm1_ref[...] * (1.0 + _mod(mod_ref, 1)) + _mod(mod_ref, 0)
    proj = _dot(h.astype(BF16), w_in_ref[...])
    q = proj[:, 0:QK_WIDTH]
    k = proj[:, QK_WIDTH:2 * QK_WIDTH]
    v = proj[:, 2 * QK_WIDTH:2 * QK_WIDTH + RET_WIDTH]
    g = proj[:, 2 * QK_WIDTH + RET_WIDTH:2 * QK_WIDTH + 2 * RET_WIDTH]
    u = proj[:, 2 * QK_WIDTH + 2 * RET_WIDTH:]

    lane = lax.broadcasted_iota(jnp.int32, q.shape, 1)
    first_half = (lane % RET_QK_DIM) < (RET_QK_DIM // 2)
    cos_t = cos_ref[...]
    sin_t = sin_ref[...]

    def rot(t):
        partner = jnp.where(first_half, pltpu.roll(t, QK_WIDTH - RET_QK_DIM // 2, axis=1),
                            pltpu.roll(t, RET_QK_DIM // 2, axis=1))
        return t * cos_t + partner * sin_t

    q = rot(q)
    k = rot(k) * (RET_QK_DIM ** -0.5)
    k_t = k.T
    v16 = v.astype(BF16)
    head_of_lane = lax.broadcasted_iota(jnp.int32, (chunk, QK_WIDTH), 1) // RET_QK_DIM

    for c in range(block_l // chunk):
        rows = slice(c * chunk, (c + 1) * chunk)
        q_c = q[rows]
        kt_c = k_t[:, rows]
        kt16 = kt_c.astype(BF16)
        state16 = state_ref[...].astype(BF16)
        for hd in range(RET_HEADS):
            in_head = head_of_lane == hd
            q_h = jnp.where(in_head, q_c, 0.0).astype(BF16)
            v_h = v16[rows, hd * RET_V_DIM:(hd + 1) * RET_V_DIM]
            scores = _dot(q_h, kt16) * dmat_ref[hd]
            inner = _dot(scores.astype(BF16), v_h)
            cross = _dot(q_h, state16) * cross_ref[hd]
            o_ref[rows, hd * RET_V_DIM:(hd + 1) * RET_V_DIM] = inner + cross
            hrows = slice(hd * RET_QK_DIM, (hd + 1) * RET_QK_DIM)
            k_dec = (kt_c[hrows] * tail_ref[hd:hd + 1, :]).astype(BF16)
            state_ref[hrows, :] = state_ref[hrows, :] * cdec_ref[hd] + _dot(k_dec, v_h)

    o_gated = _group_norm_gate(o_ref[...], g)

    ext_ref[POOL_CARRY:POOL_CARRY + block_l, :] = u
    pos = (li * block_l + lax.broadcasted_iota(jnp.int32, (block_l, 1), 0)).astype(F32)
    pooled = []
    for gi, w in enumerate(POOL_WINDOWS):
        lanes = slice(gi * POOL_GROUP_DIM, (gi + 1) * POOL_GROUP_DIM)
        acc = ext_ref[POOL_CARRY:POOL_CARRY + block_l, lanes]
        for j in range(1, w):
            acc = acc + ext_ref[POOL_CARRY - j:POOL_CARRY - j + block_l, lanes]
        cnt = jnp.minimum(pos + 1.0, float(w))
        pooled.append(acc / cnt - u[:, lanes])
    p = _pool_project(pooled, w_pool_ref, pool_scale_ref)
    ext_ref[0:POOL_CARRY, :] = ext_ref[block_l:block_l + POOL_CARRY, :]

    _mix_tail(x, o_gated, p, mod_ref, w_out_ref, norm2_ref, wr_t_ref, bias_t_ref,
              x1_ref, h2_ref, gates_ref)

    @pl.when(li == pl.num_programs(1) - 1)
    def _():
        ret_ref[...] = state_ref[...].reshape(ret_ref.shape)
        pool_ref[...] = ext_ref[1:1 + POOL_BUF, :].reshape(pool_ref.shape)


def _decay_tables(chunk):
    lg = jnp.log(1.0 - 2.0 ** (-5.0 - jnp.arange(RET_HEADS, dtype=F32)))
    idx = jnp.arange(chunk, dtype=F32)
    diff = idx[:, None] - idx[None, :]
    causal = diff >= 0
    dmat = jnp.where(causal[None], jnp.exp(lg[:, None, None] * jnp.where(causal, diff, 0.0)[None]), 0.0)
    cross = jnp.exp(lg[:, None] * (idx[None, :] + 1.0))
    cross = jnp.broadcast_to(cross[:, :, None], (RET_HEADS, chunk, RET_V_DIM))
    tail = jnp.exp(lg[:, None] * (chunk - 1.0 - idx)[None, :])
    cdec = jnp.broadcast_to(jnp.exp(lg * chunk)[:, None, None], (RET_HEADS, RET_QK_DIM, RET_V_DIM))
    return dmat, cross, tail, cdec


def _rotary_tables(pos):
    half = RET_QK_DIM // 2
    freqs = ROPE_BASE ** (-jnp.arange(half, dtype=F32) / half)
    ang = pos[:, None] * freqs[None, :]
    cos, sin = jnp.cos(ang), jnp.sin(ang)
    cos_t = jnp.tile(jnp.concatenate([cos, cos], axis=-1), (1, RET_HEADS))
    sin_t = jnp.tile(jnp.concatenate([-sin, sin], axis=-1), (1, RET_HEADS))
    return cos_t, sin_t


def _full(shape):
    return pl.BlockSpec(shape, lambda *_: (0,) * len(shape))


def _mix_prompt(x, mod, norm1, w_in16, w_pool, pool_scale, w_out16, norm2, wr_t, bias_t,
                block_l=512, chunk=256):
    b, l, d = x.shape
    cos_t, sin_t = _rotary_tables(jnp.arange(l, dtype=F32))
    dmat, cross, tail, cdec = _decay_tables(chunk)
    kernel = functools.partial(_mix_prompt_kernel, block_l=block_l, chunk=chunk)
    tok = lambda bi, li: (bi, li, 0)
    return pl.pallas_call(
        kernel,
        grid=(b, l // block_l),
        in_specs=[pl.BlockSpec((1, block_l, d), tok),
                  pl.BlockSpec((1, 6, d), lambda bi, li: (bi, 0, 0)),
                  _full((1, d)),
                  _full((d, IN_WIDTH)),
                  pl.BlockSpec((block_l, QK_WIDTH), lambda bi, li: (li, 0)),
                  pl.BlockSpec((block_l, QK_WIDTH), lambda bi, li: (li, 0)),
                  _full(dmat.shape), _full(cross.shape), _full(tail.shape), _full(cdec.shape),
                  _full(w_pool.shape), _full((1, POOL_WIDTH)), _full((d, d)), _full((1, d)),
                  _full(wr_t.shape), _full(bias_t.shape)],
        out_specs=[pl.BlockSpec((1, block_l, d), tok),
                   pl.BlockSpec((1, block_l, d), tok),
                   pl.BlockSpec((1, block_l, LANES), tok),
                   pl.BlockSpec((1, RET_HEADS, RET_QK_DIM, RET_V_DIM), lambda bi, li: (bi, 0, 0, 0)),
                   pl.BlockSpec((1, POOL_BUF, POOL_WIDTH), lambda bi, li: (bi, 0, 0))],
        out_shape=[jax.ShapeDtypeStruct((b, l, d), F32),
                   jax.ShapeDtypeStruct((b, l, d), BF16),
                   jax.ShapeDtypeStruct((b, l, LANES), F32),
                   jax.ShapeDtypeStruct((b, RET_HEADS, RET_QK_DIM, RET_V_DIM), F32),
                   jax.ShapeDtypeStruct((b, POOL_BUF, POOL_WIDTH), F32)],
        scratch_shapes=[pltpu.VMEM((QK_WIDTH, RET_V_DIM), F32),
                        pltpu.VMEM((POOL_CARRY + block_l, POOL_WIDTH), F32),
                        pltpu.VMEM((block_l, RET_WIDTH), F32)],
        compiler_params=pltpu.CompilerParams(dimension_semantics=("arbitrary", "arbitrary"),
                                             vmem_limit_bytes=VMEM_LIMIT),
        name="mix_prompt",
    )(x, mod, norm1, w_in16, cos_t, sin_t, dmat, cross, tail, cdec, w_pool, pool_scale,
      w_out16, norm2, wr_t, bias_t)


def _mix_sample_front_kernel(x_ref, mod_ref, norm1_ref, w_in_ref, cos_ref, sin_ref,
                             qt_ref, kt_ref, v_ref, g_ref, u_ref):
    x = x_ref[...]
    h = _rms(x) * norm1_ref[...] * (1.0 + _mod(mod_ref, 1)) + _mod(mod_ref, 0)
    proj = _dot(h.astype(BF16), w_in_ref[...])
    half = RET_QK_DIM // 2
    cos_c = cos_ref[...]
    sin_c = sin_ref[...]

    def rot_t(t):
        parts = []
        for hd in range(RET_HEADS):
            t1 = t[hd * RET_QK_DIM:hd * RET_QK_DIM + half]
            t2 = t[hd * RET_QK_DIM + half:(hd + 1) * RET_QK_DIM]
            parts += [t1 * cos_c - t2 * sin_c, t1 * sin_c + t2 * cos_c]
        return jnp.concatenate(parts, axis=0)

    qt_ref[...] = rot_t(proj[:, 0:QK_WIDTH].T)
    kt_ref[...] = rot_t(proj[:, QK_WIDTH:2 * QK_WIDTH].T) * (RET_QK_DIM ** -0.5)
    v_ref[...] = proj[:, 2 * QK_WIDTH:2 * QK_WIDTH + RET_WIDTH]
    g_ref[...] = proj[:, 2 * QK_WIDTH + RET_WIDTH:2 * QK_WIDTH + 2 * RET_WIDTH]
    u_ref[...] = proj[:, 2 * QK_WIDTH + 2 * RET_WIDTH:]


def _ret_step_kernel(qt_ref, kt_ref, v_ref, s0_ref, o_ref, s1_ref, *, block_b, decays):
    i = pl.program_id(0)
    lane = lax.broadcasted_iota(jnp.int32, qt_ref.shape, 1)
    for j in range(block_b):
        bi = i * block_b + j
        here = lane == bi
        q_col = jnp.sum(jnp.where(here, qt_ref[...], 0.0), axis=1, keepdims=True)
        k_col = jnp.sum(jnp.where(here, kt_ref[...], 0.0), axis=1, keepdims=True)
        v_row = v_ref[pl.ds(bi, 1), :]
        outs = []
        for hd in range(RET_HEADS):
            hrows = slice(hd * RET_QK_DIM, (hd + 1) * RET_QK_DIM)
            s1 = decays[hd] * s0_ref[j, hd] + k_col[hrows] * v_row[:, hd * RET_V_DIM:(hd + 1) * RET_V_DIM]
            s1_ref[j, hd] = s1
            outs.append(jnp.sum(q_col[hrows] * s1, axis=0, keepdims=True))
        o_ref[pl.ds(bi, 1), :] = jnp.concatenate(outs, axis=-1)


def _mix_sample_back_kernel(x_ref, mod_ref, o_ref, g_ref, u_ref, buf_ref, w_pool_ref, pool_scale_ref,
                            w_out_ref, norm2_ref, wr_t_ref, bias_t_ref,
                            x1_ref, h2_ref, gates_ref, pool_ref):
    o_gated = _group_norm_gate(o_ref[...], g_ref[...])
    u = u_ref[...]
    pooled = []
    for gi, w in enumerate(POOL_WINDOWS):
        lanes = slice(gi * POOL_GROUP_DIM, (gi + 1) * POOL_GROUP_DIM)
        acc = u[:, lanes]
        for j in range(1, w):
            acc = acc + buf_ref[:, POOL_BUF - j, lanes]
        pooled.append(acc / float(w) - u[:, lanes])
    p = _pool_project(pooled, w_pool_ref, pool_scale_ref)
    pool_ref[:, 0:POOL_BUF - 1, :] = buf_ref[:, 1:POOL_BUF, :]
    pool_ref[:, POOL_BUF - 1, :] = u
    _mix_tail(x_ref[...], o_gated, p, mod_ref, w_out_ref, norm2_ref, wr_t_ref, bias_t_ref,
              x1_ref, h2_ref, gates_ref)


def _mix_sample(x, mod, state_ret, state_pool, start, norm1, w_in16, w_pool, pool_scale, w_out16,
                norm2, wr_t, bias_t, block_b=8):
    n, d = x.shape
    half = RET_QK_DIM // 2
    freqs = ROPE_BASE ** (-jnp.arange(half, dtype=F32) / half)
    ang = jnp.full((1,), start, F32)[:, None] * freqs[None, :]
    cos_c = jnp.broadcast_to(jnp.cos(ang).T, (half, n))
    sin_c = jnp.broadcast_to(jnp.sin(ang).T, (half, n))
    params = pltpu.CompilerParams(vmem_limit_bytes=VMEM_LIMIT)
    qt, kt, v, g, u = pl.pallas_call(
        _mix_sample_front_kernel,
        out_shape=[jax.ShapeDtypeStruct((QK_WIDTH, n), F32), jax.ShapeDtypeStruct((QK_WIDTH, n), F32),
                   jax.ShapeDtypeStruct((n, RET_WIDTH), F32), jax.ShapeDtypeStruct((n, RET_WIDTH), F32),
                   jax.ShapeDtypeStruct((n, POOL_WIDTH), F32)],
        compiler_params=params,
        name="mix_sample_front",
    )(x, mod, norm1, w_in16, cos_c, sin_c)

    lg = np.log(1.0 - 2.0 ** (-5.0 - np.arange(RET_HEADS, dtype=np.float32)), dtype=np.float32)
    decays = tuple(float(np.exp(lg[h])) for h in range(RET_HEADS))
    state_block = (block_b, RET_HEADS, RET_QK_DIM, RET_V_DIM)
    o, s1 = pl.pallas_call(
        functools.partial(_ret_step_kernel, block_b=block_b, decays=decays),
        grid=(n // block_b,),
        in_specs=[_full((QK_WIDTH, n)), _full((QK_WIDTH, n)), _full((n, RET_WIDTH)),
                  pl.BlockSpec(state_block, lambda i: (i, 0, 0, 0))],
        out_specs=[_full((n, RET_WIDTH)), pl.BlockSpec(state_block, lambda i: (i, 0, 0, 0))],
        out_shape=[jax.ShapeDtypeStruct((n, RET_WIDTH), F32),
                   jax.ShapeDtypeStruct(state_ret.shape, F32)],
        compiler_params=pltpu.CompilerParams(dimension_semantics=("arbitrary",),
                                             vmem_limit_bytes=VMEM_LIMIT),
        name="ret_step",
    )(qt, kt, v, state_ret)

    x1, h2, gates, pool = pl.pallas_call(
        _mix_sample_back_kernel,
        out_shape=[jax.ShapeDtypeStruct((n, d), F32), jax.ShapeDtypeStruct((n, d), BF16),
                   jax.ShapeDtypeStruct((n, LANES), F32), jax.ShapeDtypeStruct(state_pool.shape, F32)],
        compiler_params=params,
        name="mix_sample_back",
    )(x, mod, o, g, u, state_pool, w_pool, pool_scale, w_out16, norm2, wr_t, bias_t)
    return x1, h2, gates, s1, pool


def _moe_kernel(h_ref, gates_ref, x1_ref, mod_ref, normf_ref, wg_ref, wu_ref, wd_ref,
                wsg_ref, wsu_ref, wsd_ref, y_ref, acc_ref):
    e = pl.program_id(1)
    h = h_ref[...]

    @pl.when(e == 0)
    def _():
        a = _silu(_dot(h, wsg_ref[...])) * _dot(h, wsu_ref[...])
        acc_ref[...] = _dot(a.astype(BF16), wsd_ref[...])

    lane = lax.broadcasted_iota(jnp.int32, gates_ref.shape, 1)
    gate = jnp.sum(jnp.where(lane == e, gates_ref[...], 0.0), axis=1, keepdims=True)
    hg = _dot(h, wg_ref[0].astype(BF16))
    hu = _dot(h, wu_ref[0].astype(BF16))
    a = _silu(hg) * hu * gate
    acc_ref[...] += _dot(a.astype(BF16), wd_ref[0].astype(BF16))

    @pl.when(e == pl.num_programs(1) - 1)
    def _():
        x2 = x1_ref[...] + _mod(mod_ref, 5) * acc_ref[...]
        y_ref[...] = _rms(x2) * normf_ref[...]


def _moe(h2, gates, x1, mod, norm_f, w_eg, w_eu, w_ed, w_sg16, w_su16, w_sd16, block_t, per_seq):
    t, d = h2.shape
    tok = lambda i, e: (i, 0)
    if per_seq:
        mod_spec = pl.BlockSpec((1, 6, d), lambda i, e: (i // per_seq, 0, 0))
    else:
        mod_spec = pl.BlockSpec((block_t, 6 * d), tok)
    return pl.pallas_call(
        _moe_kernel,
        grid=(t // block_t, N_EXPERTS),
        in_specs=[pl.BlockSpec((block_t, d), tok),
                  pl.BlockSpec((block_t, LANES), tok),
                  pl.BlockSpec((block_t, d), tok),
                  mod_spec,
                  _full((1, d)),
                  pl.BlockSpec((1, d, EXPERT_DIM), lambda i, e: (e, 0, 0)),
                  pl.BlockSpec((1, d, EXPERT_DIM), lambda i, e: (e, 0, 0)),
                  pl.BlockSpec((1, EXPERT_DIM, d), lambda i, e: (e, 0, 0)),
                  _full((d, EXPERT_DIM)), _full((d, EXPERT_DIM)), _full((EXPERT_DIM, d))],
        out_specs=pl.BlockSpec((block_t, d), tok),
        out_shape=jax.ShapeDtypeStruct((t, d), F32),
        scratch_shapes=[pltpu.VMEM((block_t, d), F32)],
        compiler_params=pltpu.CompilerParams(dimension_semantics=("arbitrary", "arbitrary"),
                                             vmem_limit_bytes=VMEM_LIMIT),
        name="moe",
    )(h2, gates, x1, mod, norm_f, w_eg, w_eu, w_ed, w_sg16, w_su16, w_sd16)


def kernel(x_prompt, x_sample, c_prompt, c_sample, state_ret, state_pool, norm1, norm2, norm_f,
           w_ada, b_ada, w_in, w_out, w_pool, pool_scale, w_router, router_bias, w_exp_gate,
           w_exp_up, w_exp_down, w_sh_gate, w_sh_up, w_sh_down):
    b, l, d = x_prompt.shape
    n = x_sample.shape[0]
    past_len = 16384

    mod = _ada(jnp.concatenate([c_prompt, c_sample], axis=0), w_ada[0], b_ada[0])
    mod_p = mod[:b].reshape(b, 6, d)
    mod_s = mod[b:]

    w_in16 = w_in[0].astype(BF16)
    w_out16 = w_out[0].astype(BF16)
    wr_t = w_router[0].T
    bias_t = jnp.broadcast_to(router_bias[0][:, None], (N_EXPERTS, LANES))
    n1, n2, nf = norm1[0].reshape(1, d), norm2[0].reshape(1, d), norm_f.reshape(1, d)
    ps = pool_scale[0].reshape(1, POOL_WIDTH)
    shared = (w_sh_gate[0].astype(BF16), w_sh_up[0].astype(BF16), w_sh_down[0].astype(BF16))
    experts = (w_exp_gate[0], w_exp_up[0], w_exp_down[0])

    x1_p, h2_p, gates_p, ret_p, pool_p = _mix_prompt(
        x_prompt, mod_p, n1, w_in16, w_pool[0], ps, w_out16, n2, wr_t, bias_t)
    block_t = 1024
    y_p = _moe(h2_p.reshape(b * l, d), gates_p.reshape(b * l, LANES), x1_p.reshape(b * l, d),
               mod_p, nf, *experts, *shared, block_t=block_t, per_seq=l // block_t)

    x1_s, h2_s, gates_s, ret_s, pool_s = _mix_sample(
        x_sample.reshape(n, d), mod_s, state_ret[0], state_pool[0], float(past_len), n1, w_in16,
        w_pool[0], ps, w_out16, n2, wr_t, bias_t)
    y_s = _moe(h2_s, gates_s, x1_s, mod_s, nf, *experts, *shared, block_t=n, per_seq=0)

    return (y_p.reshape(b, l, d), y_s.reshape(n, 1, d), ret_p[None], pool_p[None],
            ret_s[None], pool_s[None])
```

```python
import functools

import jax
import jax.numpy as jnp
import numpy as np
from jax import lax
from jax.experimental import pallas as pl
from jax.experimental.pallas import tpu as pltpu
from jax.experimental.pallas import tpu_sc as plsc

D_MODEL = 1024
RET_HEADS = 4
RET_QK_DIM = 64
RET_V_DIM = 128
RET_WIDTH = RET_HEADS * RET_V_DIM
QK_WIDTH = RET_HEADS * RET_QK_DIM
ROPE_BASE = 10000.0
POOL_WINDOWS = (2, 4, 8, 16)
POOL_WIDTH = 512
POOL_GROUP_DIM = 128
POOL_BUF = 15
IN_WIDTH = 2 * QK_WIDTH + 2 * RET_WIDTH + POOL_WIDTH
N_EXPERTS = 64
TOP_K = 8
N_EXPERT_GROUPS = 8
GROUP_SIZE = N_EXPERTS // N_EXPERT_GROUPS
TOP_GROUPS = 4
EXPERT_DIM = 256
ROUTE_SCALE = 2.5
EPS = 1e-6

LANES = 128
POOL_CARRY = 16
VMEM_LIMIT = 56 * 1024 * 1024
HALF = D_MODEL // 2
ROW_TILE = 256
SC_CHUNK = 128

BF16 = jnp.bfloat16
F32 = jnp.float32
U32 = jnp.uint32
I32 = jnp.int32


def _silu(x):
    return x * jax.nn.sigmoid(x)


def _dot(a, b):
    return jnp.dot(a, b, preferred_element_type=F32)


def _rms(x):
    return x * lax.rsqrt(jnp.mean(x * x, axis=-1, keepdims=True) + EPS)


def _mod(mod_ref, i):
    if len(mod_ref.shape) == 3:
        return mod_ref[0, i:i + 1, :]
    return mod_ref[:, i * D_MODEL:(i + 1) * D_MODEL]


def _split_bf16(x):
    hi = x.astype(BF16)
    lo = (x - hi.astype(F32)).astype(BF16)
    return hi, lo


def _pack_rows(x):
    lo = lax.bitcast_convert_type(x[:, :HALF].astype(BF16).astype(F32), U32)
    hi = lax.bitcast_convert_type(x[:, HALF:].astype(BF16).astype(F32), U32)
    return (hi & jnp.uint32(0xFFFF0000)) | (lo >> jnp.uint32(16))


def _unpack_rows(w):
    lo = lax.bitcast_convert_type(w << jnp.uint32(16), F32)
    hi = lax.bitcast_convert_type(w & jnp.uint32(0xFFFF0000), F32)
    return lo, hi


def _first_max_onehot(work, idx, n):
    m = jnp.max(work, axis=0, keepdims=True)
    first = jnp.min(jnp.where(work == m, idx, float(n)), axis=0, keepdims=True)
    return idx == first


def _route(h2, wr_t_ref, bias_t_ref):
    n = h2.shape[0]
    h_hi, h_lo = _split_bf16(h2)
    w_hi, w_lo = _split_bf16(wr_t_ref[...])
    nt = (((1,), (1,)), ((), ()))
    logits = (lax.dot_general(w_hi, h_hi, nt, preferred_element_type=F32)
              + lax.dot_general(w_hi, h_lo, nt, preferred_element_type=F32)
              + lax.dot_general(w_lo, h_hi, nt, preferred_element_type=F32))
    scores = jax.nn.sigmoid(logits)
    biased = scores + bias_t_ref[:, 0:1]
    b3 = biased.reshape(N_EXPERT_GROUPS, GROUP_SIZE, n)
    i3 = lax.broadcasted_iota(I32, b3.shape, 1).astype(F32)
    m1 = jnp.max(b3, axis=1, keepdims=True)
    first = jnp.min(jnp.where(b3 == m1, i3, float(GROUP_SIZE)), axis=1, keepdims=True)
    m2 = jnp.max(jnp.where(i3 == first, -jnp.inf, b3), axis=1, keepdims=True)
    gscore = (m1 + m2).reshape(N_EXPERT_GROUPS, n)
    gidx = lax.broadcasted_iota(I32, gscore.shape, 0).astype(F32)
    gsel = jnp.zeros(gscore.shape, F32)
    work = gscore
    for _ in range(TOP_GROUPS):
        hit = _first_max_onehot(work, gidx, N_EXPERT_GROUPS)
        gsel = jnp.where(hit, 1.0, gsel)
        work = jnp.where(hit, -jnp.inf, work)
    gsel3 = jnp.broadcast_to(gsel.reshape(N_EXPERT_GROUPS, 1, n), b3.shape)
    work = jnp.where(gsel3 > 0.0, b3, -jnp.inf).reshape(N_EXPERTS, n)
    eidx = lax.broadcasted_iota(I32, work.shape, 0).astype(F32)
    sel = jnp.zeros(work.shape, F32)
    for _ in range(TOP_K):
        hit = _first_max_onehot(work, eidx, N_EXPERTS)
        sel = jnp.where(hit, 1.0, sel)
        work = jnp.where(hit, -jnp.inf, work)
    picked = jnp.where(sel > 0.0, scores, 0.0)
    gates = picked / jnp.sum(picked, axis=0, keepdims=True) * ROUTE_SCALE
    below = (lax.broadcasted_iota(I32, (N_EXPERTS, N_EXPERTS), 1)
             < lax.broadcasted_iota(I32, (N_EXPERTS, N_EXPERTS), 0))
    slot = _dot(jnp.where(below, 1.0, 0.0).astype(BF16), sel.astype(BF16))
    e_rows, w_rows = [], []
    for s in range(TOP_K):
        here = jnp.where(slot == float(s), sel, 0.0)
        e_rows.append(jnp.sum(here * eidx, axis=0, keepdims=True))
        w_rows.append(jnp.sum(here * gates, axis=0, keepdims=True))
    experts = jnp.concatenate(e_rows, axis=0).astype(I32)
    w_t = jnp.concatenate(w_rows + [jnp.zeros((LANES - TOP_K, n), F32)], axis=0)
    return experts, w_t.T


def _group_norm_gate(o, g):
    parts = []
    for h in range(RET_HEADS):
        oh = o[:, h * RET_V_DIM:(h + 1) * RET_V_DIM]
        mu = jnp.mean(oh, axis=-1, keepdims=True)
        ctr = oh - mu
        var = jnp.mean(ctr * ctr, axis=-1, keepdims=True)
        parts.append(ctr * lax.rsqrt(var + EPS))
    return _silu(g) * jnp.concatenate(parts, axis=-1)


def _pool_project(pooled, w_pool_ref, pool_scale_ref):
    parts = [_dot(p.astype(BF16), w_pool_ref[gi].astype(BF16)) for gi, p in enumerate(pooled)]
    return jnp.concatenate(parts, axis=-1) * pool_scale_ref[...]


def _mix_tail(x, o_gated, p, mod_ref, w_out_ref, norm2_ref, wr_t_ref, bias_t_ref,
              x1_ref, h2_ref, experts_ref, gatew_ref):
    mix = jnp.concatenate([o_gated, p], axis=-1).astype(BF16)
    y = _dot(mix, w_out_ref[...])
    x1 = x + _mod(mod_ref, 2) * y
    h2 = _rms(x1) * norm2_ref[...] * (1.0 + _mod(mod_ref, 4)) + _mod(mod_ref, 3)
    x1_ref[...] = x1.reshape(x1_ref.shape)
    h2_ref[...] = _pack_rows(h2)
    experts, gate_w = _route(h2, wr_t_ref, bias_t_ref)
    experts_ref[...] = experts
    gatew_ref[...] = gate_w


def _ada_kernel(c_ref, w_ref, b_ref, o_ref):
    cs = _silu(c_ref[...]).astype(BF16)
    o_ref[...] = _dot(cs, w_ref[...].astype(BF16)) + b_ref[...]


def _ada(c_all, w_ada, b_ada, block_n=1536):
    n, d = c_all.shape
    width = w_ada.shape[1]
    return pl.pallas_call(
        _ada_kernel,
        grid=(width // block_n,),
        in_specs=[pl.BlockSpec((n, d), lambda j: (0, 0)),
                  pl.BlockSpec((d, block_n), lambda j: (0, j)),
                  pl.BlockSpec((1, block_n), lambda j: (0, j))],
        out_specs=pl.BlockSpec((n, block_n), lambda j: (0, j)),
        out_shape=jax.ShapeDtypeStruct((n, width), F32),
        compiler_params=pltpu.CompilerParams(vmem_limit_bytes=VMEM_LIMIT),
        name="ada",
    )(c_all, w_ada, b_ada.reshape(1, width))


def _mix_prompt_kernel(x_ref, mod_ref, norm1_ref, w_in_ref, cos_ref, sin_ref, dmat_ref, cross_ref,
                       tail_ref, cdec_ref, w_pool_ref, pool_scale_ref, w_out_ref, norm2_ref,
                       wr_t_ref, bias_t_ref,
                       x1_ref, h2_ref, experts_ref, gatew_ref, ret_ref, pool_ref,
                       state_ref, ext_ref, o_ref, *, block_l, chunk):
    li = pl.program_id(1)

    @pl.when(li == 0)
    def _():
        state_ref[...] = jnp.zeros_like(state_ref)
        ext_ref[0:POOL_CARRY, :] = jnp.zeros((POOL_CARRY, POOL_WIDTH), F32)

    x = x_ref[0]
    h = _rms(x) * norm1_ref[...] * (1.0 + _mod(mod_ref, 1)) + _mod(mod_ref, 0)
    proj = _dot(h.astype(BF16), w_in_ref[...])
    q = proj[:, 0:QK_WIDTH]
    k = proj[:, QK_WIDTH:2 * QK_WIDTH]
    v = proj[:, 2 * QK_WIDTH:2 * QK_WIDTH + RET_WIDTH]
    g = proj[:, 2 * QK_WIDTH + RET_WIDTH:2 * QK_WIDTH + 2 * RET_WIDTH]
    u = proj[:, 2 * QK_WIDTH + 2 * RET_WIDTH:]

    lane = lax.broadcasted_iota(I32, q.shape, 1)
    first_half = (lane % RET_QK_DIM) < (RET_QK_DIM // 2)
    cos_t = cos_ref[...]
    sin_t = sin_ref[...]

    def rot(t):
        partner = jnp.where(first_half, pltpu.roll(t, QK_WIDTH - RET_QK_DIM // 2, axis=1),
                            pltpu.roll(t, RET_QK_DIM // 2, axis=1))
        return t * cos_t + partner * sin_t

    q = rot(q)
    k = rot(k) * (RET_QK_DIM ** -0.5)
    k_t = k.T
    v16 = v.astype(BF16)
    head_of_lane = lax.broadcasted_iota(I32, (chunk, QK_WIDTH), 1) // RET_QK_DIM

    for c in range(block_l // chunk):
        rows = slice(c * chunk, (c + 1) * chunk)
        q_c = q[rows]
        kt_c = k_t[:, rows]
        kt16 = kt_c.astype(BF16)
        state16 = state_ref[...].astype(BF16)
        for hd in range(RET_HEADS):
            in_head = head_of_lane == hd
            q_h = jnp.where(in_head, q_c, 0.0).astype(BF16)
            v_h = v16[rows, hd * RET_V_DIM:(hd + 1) * RET_V_DIM]
            scores = _dot(q_h, kt16) * dmat_ref[hd]
            inner = _dot(scores.astype(BF16), v_h)
            cross = _dot(q_h, state16) * cross_ref[hd]
            o_ref[rows, hd * RET_V_DIM:(hd + 1) * RET_V_DIM] = inner + cross
            hrows = slice(hd * RET_QK_DIM, (hd + 1) * RET_QK_DIM)
            k_dec = (kt_c[hrows] * tail_ref[hd:hd + 1, :]).astype(BF16)
            state_ref[hrows, :] = state_ref[hrows, :] * cdec_ref[hd] + _dot(k_dec, v_h)

    o_gated = _group_norm_gate(o_ref[...], g)

    ext_ref[POOL_CARRY:POOL_CARRY + block_l, :] = u
    pos = (li * block_l + lax.broadcasted_iota(I32, (block_l, 1), 0)).astype(F32)
    pooled = []
    for gi, w in enumerate(POOL_WINDOWS):
        lanes = slice(gi * POOL_GROUP_DIM, (gi + 1) * POOL_GROUP_DIM)
        acc = ext_ref[POOL_CARRY:POOL_CARRY + block_l, lanes]
        for j in range(1, w):
            acc = acc + ext_ref[POOL_CARRY - j:POOL_CARRY - j + block_l, lanes]
        cnt = jnp.minimum(pos + 1.0, float(w))
        pooled.append(acc / cnt - u[:, lanes])
    p = _pool_project(pooled, w_pool_ref, pool_scale_ref)
    ext_ref[0:POOL_CARRY, :] = ext_ref[block_l:block_l + POOL_CARRY, :]

    _mix_tail(x, o_gated, p, mod_ref, w_out_ref, norm2_ref, wr_t_ref, bias_t_ref,
              x1_ref, h2_ref, experts_ref, gatew_ref)

    @pl.when(li == pl.num_programs(1) - 1)
    def _():
        ret_ref[...] = state_ref[...].reshape(ret_ref.shape)
        pool_ref[...] = ext_ref[1:1 + POOL_BUF, :].reshape(pool_ref.shape)


def _decay_tables(chunk):
    lg = jnp.log(1.0 - 2.0 ** (-5.0 - jnp.arange(RET_HEADS, dtype=F32)))
    idx = jnp.arange(chunk, dtype=F32)
    diff = idx[:, None] - idx[None, :]
    causal = diff >= 0
    dmat = jnp.where(causal[None], jnp.exp(lg[:, None, None] * jnp.where(causal, diff, 0.0)[None]), 0.0)
    cross = jnp.exp(lg[:, None] * (idx[None, :] + 1.0))
    cross = jnp.broadcast_to(cross[:, :, None], (RET_HEADS, chunk, RET_V_DIM))
    tail = jnp.exp(lg[:, None] * (chunk - 1.0 - idx)[None, :])
    cdec = jnp.broadcast_to(jnp.exp(lg * chunk)[:, None, None], (RET_HEADS, RET_QK_DIM, RET_V_DIM))
    return dmat, cross, tail, cdec


def _rotary_tables(pos):
    half = RET_QK_DIM // 2
    freqs = ROPE_BASE ** (-jnp.arange(half, dtype=F32) / half)
    ang = pos[:, None] * freqs[None, :]
    cos, sin = jnp.cos(ang), jnp.sin(ang)
    cos_t = jnp.tile(jnp.concatenate([cos, cos], axis=-1), (1, RET_HEADS))
    sin_t = jnp.tile(jnp.concatenate([-sin, sin], axis=-1), (1, RET_HEADS))
    return cos_t, sin_t


def _full(shape):
    return pl.BlockSpec(shape, lambda *_: (0,) * len(shape))


def _mix_prompt(x, mod, norm1, w_in16, w_pool, pool_scale, w_out16, norm2, wr_t, bias_t,
                block_l=512, chunk=256):
    b, l, d = x.shape
    n_tokens = b * l
    nl = l // block_l
    cos_t, sin_t = _rotary_tables(jnp.arange(l, dtype=F32))
    dmat, cross, tail, cdec = _decay_tables(chunk)
    kernel = functools.partial(_mix_prompt_kernel, block_l=block_l, chunk=chunk)
    tok = lambda bi, li: (bi, li, 0)
    flat = lambda bi, li: (bi * nl + li, 0)
    return pl.pallas_call(
        kernel,
        grid=(b, nl),
        in_specs=[pl.BlockSpec((1, block_l, d), tok),
                  pl.BlockSpec((1, 6, d), lambda bi, li: (bi, 0, 0)),
                  _full((1, d)),
                  _full((d, IN_WIDTH)),
                  pl.BlockSpec((block_l, QK_WIDTH), lambda bi, li: (li, 0)),
                  pl.BlockSpec((block_l, QK_WIDTH), lambda bi, li: (li, 0)),
                  _full(dmat.shape), _full(cross.shape), _full(tail.shape), _full(cdec.shape),
                  _full(w_pool.shape), _full((1, POOL_WIDTH)), _full((d, d)), _full((1, d)),
                  _full(wr_t.shape), _full(bias_t.shape)],
        out_specs=[pl.BlockSpec((1, block_l, d), tok),
                   pl.BlockSpec((block_l, HALF), flat),
                   pl.BlockSpec((TOP_K, block_l), lambda bi, li: (0, bi * nl + li)),
                   pl.BlockSpec((block_l, LANES), flat),
                   pl.BlockSpec((1, RET_HEADS, RET_QK_DIM, RET_V_DIM), lambda bi, li: (bi, 0, 0, 0)),
                   pl.BlockSpec((1, POOL_BUF, POOL_WIDTH), lambda bi, li: (bi, 0, 0))],
        out_shape=[jax.ShapeDtypeStruct((b, l, d), F32),
                   jax.ShapeDtypeStruct((n_tokens, HALF), U32),
                   jax.ShapeDtypeStruct((TOP_K, n_tokens), I32),
                   jax.ShapeDtypeStruct((b * l, LANES), F32),
                   jax.ShapeDtypeStruct((b, RET_HEADS, RET_QK_DIM, RET_V_DIM), F32),
                   jax.ShapeDtypeStruct((b, POOL_BUF, POOL_WIDTH), F32)],
        scratch_shapes=[pltpu.VMEM((QK_WIDTH, RET_V_DIM), F32),
                        pltpu.VMEM((POOL_CARRY + block_l, POOL_WIDTH), F32),
                        pltpu.VMEM((block_l, RET_WIDTH), F32)],
        compiler_params=pltpu.CompilerParams(dimension_semantics=("arbitrary", "arbitrary"),
                                             vmem_limit_bytes=VMEM_LIMIT),
        name="mix_prompt",
    )(x, mod, norm1, w_in16, cos_t, sin_t, dmat, cross, tail, cdec, w_pool, pool_scale,
      w_out16, norm2, wr_t, bias_t)


def _mix_sample_front_kernel(x_ref, mod_ref, norm1_ref, w_in_ref, cos_ref, sin_ref,
                             qt_ref, kt_ref, v_ref, g_ref, u_ref):
    x = x_ref[...]
    h = _rms(x) * norm1_ref[...] * (1.0 + _mod(mod_ref, 1)) + _mod(mod_ref, 0)
    proj = _dot(h.astype(BF16), w_in_ref[...])
    half = RET_QK_DIM // 2
    cos_c = cos_ref[...]
    sin_c = sin_ref[...]

    def rot_t(t):
        parts = []
        for hd in range(RET_HEADS):
            t1 = t[hd * RET_QK_DIM:hd * RET_QK_DIM + half]
            t2 = t[hd * RET_QK_DIM + half:(hd + 1) * RET_QK_DIM]
            parts += [t1 * cos_c - t2 * sin_c, t1 * sin_c + t2 * cos_c]
        return jnp.concatenate(parts, axis=0)

    qt_ref[...] = rot_t(proj[:, 0:QK_WIDTH].T)
    kt_ref[...] = rot_t(proj[:, QK_WIDTH:2 * QK_WIDTH].T) * (RET_QK_DIM ** -0.5)
    v_ref[...] = proj[:, 2 * QK_WIDTH:2 * QK_WIDTH + RET_WIDTH]
    g_ref[...] = proj[:, 2 * QK_WIDTH + RET_WIDTH:2 * QK_WIDTH + 2 * RET_WIDTH]
    u_ref[...] = proj[:, 2 * QK_WIDTH + 2 * RET_WIDTH:]


def _ret_step_kernel(qt_ref, kt_ref, v_ref, s0_ref, o_ref, s1_ref, *, block_b, decays):
    i = pl.program_id(0)
    lane = lax.broadcasted_iota(I32, qt_ref.shape, 1)
    for j in range(block_b):
        bi = i * block_b + j
        here = lane == bi
        q_col = jnp.sum(jnp.where(here, qt_ref[...], 0.0), axis=1, keepdims=True)
        k_col = jnp.sum(jnp.where(here, kt_ref[...], 0.0), axis=1, keepdims=True)
        v_row = v_ref[pl.ds(bi, 1), :]
        outs = []
        for hd in range(RET_HEADS):
            hrows = slice(hd * RET_QK_DIM, (hd + 1) * RET_QK_DIM)
            s1 = decays[hd] * s0_ref[j, hd] + k_col[hrows] * v_row[:, hd * RET_V_DIM:(hd + 1) * RET_V_DIM]
            s1_ref[j, hd] = s1
            outs.append(jnp.sum(q_col[hrows] * s1, axis=0, keepdims=True))
        o_ref[pl.ds(bi, 1), :] = jnp.concatenate(outs, axis=-1)


def _mix_sample_back_kernel(x_ref, mod_ref, o_ref, g_ref, u_ref, buf_ref, w_pool_ref, pool_scale_ref,
                            w_out_ref, norm2_ref, wr_t_ref, bias_t_ref,
                            x1_ref, h2_ref, experts_ref, gatew_ref, pool_ref):
    o_gated = _group_norm_gate(o_ref[...], g_ref[...])
    u = u_ref[...]
    pooled = []
    for gi, w in enumerate(POOL_WINDOWS):
        lanes = slice(gi * POOL_GROUP_DIM, (gi + 1) * POOL_GROUP_DIM)
        acc = u[:, lanes]
        for j in range(1, w):
            acc = acc + buf_ref[:, POOL_BUF - j, lanes]
        pooled.append(acc / float(w) - u[:, lanes])
    p = _pool_project(pooled, w_pool_ref, pool_scale_ref)
    pool_ref[:, 0:POOL_BUF - 1, :] = buf_ref[:, 1:POOL_BUF, :]
    pool_ref[:, POOL_BUF - 1, :] = u
    _mix_tail(x_ref[...], o_gated, p, mod_ref, w_out_ref, norm2_ref, wr_t_ref, bias_t_ref,
              x1_ref, h2_ref, experts_ref, gatew_ref)


def _mix_sample(x, mod, state_ret, state_pool, start, norm1, w_in16, w_pool,
                pool_scale, w_out16, norm2, wr_t, bias_t, block_b=8):
    n, d = x.shape
    half = RET_QK_DIM // 2
    freqs = ROPE_BASE ** (-jnp.arange(half, dtype=F32) / half)
    ang = jnp.full((1,), start, F32)[:, None] * freqs[None, :]
    cos_c = jnp.broadcast_to(jnp.cos(ang).T, (half, n))
    sin_c = jnp.broadcast_to(jnp.sin(ang).T, (half, n))
    params = pltpu.CompilerParams(vmem_limit_bytes=VMEM_LIMIT)
    qt, kt, v, g, u = pl.pallas_call(
        _mix_sample_front_kernel,
        out_shape=[jax.ShapeDtypeStruct((QK_WIDTH, n), F32), jax.ShapeDtypeStruct((QK_WIDTH, n), F32),
                   jax.ShapeDtypeStruct((n, RET_WIDTH), F32), jax.ShapeDtypeStruct((n, RET_WIDTH), F32),
                   jax.ShapeDtypeStruct((n, POOL_WIDTH), F32)],
        compiler_params=params,
        name="mix_sample_front",
    )(x, mod, norm1, w_in16, cos_c, sin_c)

    lg = np.log(1.0 - 2.0 ** (-5.0 - np.arange(RET_HEADS, dtype=np.float32)), dtype=np.float32)
    decays = tuple(float(np.exp(lg[h])) for h in range(RET_HEADS))
    state_block = (block_b, RET_HEADS, RET_QK_DIM, RET_V_DIM)
    o, s1 = pl.pallas_call(
        functools.partial(_ret_step_kernel, block_b=block_b, decays=decays),
        grid=(n // block_b,),
        in_specs=[_full((QK_WIDTH, n)), _full((QK_WIDTH, n)), _full((n, RET_WIDTH)),
                  pl.BlockSpec(state_block, lambda i: (i, 0, 0, 0))],
        out_specs=[_full((n, RET_WIDTH)), pl.BlockSpec(state_block, lambda i: (i, 0, 0, 0))],
        out_shape=[jax.ShapeDtypeStruct((n, RET_WIDTH), F32),
                   jax.ShapeDtypeStruct(state_ret.shape, F32)],
        compiler_params=pltpu.CompilerParams(dimension_semantics=("arbitrary",),
                                             vmem_limit_bytes=VMEM_LIMIT),
        name="ret_step",
    )(qt, kt, v, state_ret)

    x1, h2, experts, gate_w, pool = pl.pallas_call(
        _mix_sample_back_kernel,
        out_shape=[jax.ShapeDtypeStruct((n, d), F32),
                   jax.ShapeDtypeStruct((n, HALF), U32),
                   jax.ShapeDtypeStruct((TOP_K, n), I32),
                   jax.ShapeDtypeStruct((n, LANES), F32),
                   jax.ShapeDtypeStruct(state_pool.shape, F32)],
        compiler_params=params,
        name="mix_sample_back",
    )(x, mod, o, g, u, state_pool, w_pool, pool_scale, w_out16, norm2, wr_t, bias_t)
    return x1, h2, experts, gate_w, s1, pool


def _plan_kernel(experts_ref, pos_ref, meta_ref, cnt_ref, carry_ref, off_ref, *, block_t, n_tiles_pad):
    phase = pl.program_id(0)
    j = pl.program_id(1)
    e_blk = experts_ref[...]
    eidx = lax.broadcasted_iota(I32, (N_EXPERTS, block_t), 0)
    member = jnp.zeros((N_EXPERTS, block_t), F32)
    for s in range(TOP_K):
        member = member + jnp.where(eidx == e_blk[s:s + 1, :], 1.0, 0.0)
    per_expert = jnp.broadcast_to(jnp.sum(member, axis=1, keepdims=True), (N_EXPERTS, LANES))

    @pl.when((phase == 0) & (j == 0))
    def _():
        cnt_ref[...] = jnp.zeros_like(cnt_ref)

    @pl.when(phase == 0)
    def _():
        cnt_ref[...] += per_expert

    @pl.when((phase == 0) & (j == pl.num_programs(1) - 1))
    def _():
        cnt = cnt_ref[...]
        n_tile = jnp.floor((cnt + (ROW_TILE - 1.0)) * (1.0 / ROW_TILE))
        upto = (lax.broadcasted_iota(I32, (N_EXPERTS, N_EXPERTS), 1)
                <= lax.broadcasted_iota(I32, (N_EXPERTS, N_EXPERTS), 0))
        tile_end = _dot(jnp.where(upto, 1.0, 0.0).astype(BF16), n_tile.astype(BF16))
        tile_start = tile_end - n_tile
        off_ref[...] = tile_start * ROW_TILE
        carry_ref[...] = jnp.zeros_like(carry_ref)
        tile = lax.broadcasted_iota(I32, (N_EXPERTS, n_tiles_pad), 1).astype(F32)
        t_end, t_start, c_col = tile_end[:, 0:1], tile_start[:, 0:1], cnt[:, 0:1]
        tile_expert = jnp.sum(jnp.where(t_end <= tile, 1.0, 0.0), axis=0, keepdims=True)
        inside = jnp.where(t_start <= tile, jnp.where(tile < t_end, 1.0, 0.0), 0.0)
        rows_left = jnp.minimum(c_col - (tile - t_start) * ROW_TILE, float(ROW_TILE))
        valid = jnp.sum(inside * rows_left, axis=0, keepdims=True)
        n_used = jnp.broadcast_to(tile_end[N_EXPERTS - 1:N_EXPERTS, 0:1], (1, n_tiles_pad))
        rows = [jnp.minimum(tile_expert, N_EXPERTS - 1.0), valid, n_used,
                jnp.zeros((5, n_tiles_pad), F32)]
        meta_ref[...] = jnp.concatenate(rows, axis=0).astype(I32)

    @pl.when(phase == 1)
    def _():
        before = (lax.broadcasted_iota(I32, (block_t, block_t), 0)
                  < lax.broadcasted_iota(I32, (block_t, block_t), 1))
        rank = _dot(member.astype(BF16), jnp.where(before, 1.0, 0.0).astype(BF16))
        row = off_ref[:, 0:1] + carry_ref[:, 0:1] + rank
        carry_ref[...] += per_expert
        out = [jnp.sum(jnp.where(eidx == e_blk[s:s + 1, :], row, 0.0), axis=0, keepdims=True)
               for s in range(TOP_K)]
        pos_ref[...] = jnp.concatenate(out, axis=0).astype(I32)


def _plan(experts_all, n_tiles_pad, block_t=384):
    n_tokens = experts_all.shape[1]
    nb = n_tokens // block_t
    return pl.pallas_call(
        functools.partial(_plan_kernel, block_t=block_t, n_tiles_pad=n_tiles_pad),
        grid=(2, nb),
        in_specs=[pl.BlockSpec((TOP_K, block_t), lambda ph, j: (0, j))],
        out_specs=[pl.BlockSpec((TOP_K, block_t), lambda ph, j: (0, j * ph)),
                   _full((8, n_tiles_pad))],
        out_shape=[jax.ShapeDtypeStruct((TOP_K, n_tokens), I32),
                   jax.ShapeDtypeStruct((8, n_tiles_pad), I32)],
        scratch_shapes=[pltpu.VMEM((N_EXPERTS, LANES), F32)] * 3,
        compiler_params=pltpu.CompilerParams(dimension_semantics=("arbitrary", "arbitrary"),
                                             vmem_limit_bytes=VMEM_LIMIT),
        name="plan",
    )(experts_all)


def _sc_workers():
    info = plsc.get_sparse_core_info()
    return info.num_cores, info.num_cores * info.num_subcores


def _sc_scatter_rows(rows_a, rows_b, pos_t, n_out):
    ta, w = rows_a.shape
    s = pos_t.shape[0]
    n_cores, n_workers = _sc_workers()
    chunks_a = ta // SC_CHUNK
    n_chunks = pos_t.shape[1] // SC_CHUNK
    iters = -(-n_chunks // n_workers)
    mesh = plsc.VectorSubcoreMesh(core_axis_name="c", subcore_axis_name="s")

    @functools.partial(
        pl.kernel, mesh=mesh, out_type=jax.ShapeDtypeStruct((n_out, w), rows_a.dtype),
        scratch_types=[pltpu.VMEM((SC_CHUNK, w), rows_a.dtype), pltpu.VMEM((s, SC_CHUNK), I32),
                       pltpu.SemaphoreType.DMA],
        name="dispatch")
    def k(a_hbm, b_hbm, pos_hbm, out_hbm, rows_v, idx_v, sem):
        wid = lax.axis_index("s") * n_cores + lax.axis_index("c")

        @pl.loop(0, iters)
        def _(it):
            c = it * n_workers + wid

            @pl.when(c < chunks_a)
            def _():
                pltpu.sync_copy(a_hbm.at[pl.ds(pl.multiple_of(c * SC_CHUNK, SC_CHUNK), SC_CHUNK)], rows_v)

            @pl.when((c >= chunks_a) & (c < n_chunks))
            def _():
                base_b = pl.multiple_of((c - chunks_a) * SC_CHUNK, SC_CHUNK)
                pltpu.sync_copy(b_hbm.at[pl.ds(base_b, SC_CHUNK)], rows_v)

            @pl.when(c < n_chunks)
            def _():
                base = pl.multiple_of(c * SC_CHUNK, SC_CHUNK)
                pltpu.sync_copy(pos_hbm.at[:, pl.ds(base, SC_CHUNK)], idx_v)
                copies = [pltpu.async_copy(rows_v, out_hbm.at[idx_v.at[j]], sem) for j in range(s)]
                for cp in copies:
                    cp.wait()

    return k(rows_a, rows_b, pos_t)


def _sc_gather_rows(table, pos_t):
    _, w = table.shape
    s, t = pos_t.shape
    n_cores, n_workers = _sc_workers()
    n_chunks = t // SC_CHUNK
    iters = -(-n_chunks // n_workers)
    mesh = plsc.VectorSubcoreMesh(core_axis_name="c", subcore_axis_name="s")

    @functools.partial(
        pl.kernel, mesh=mesh, out_type=jax.ShapeDtypeStruct((s, t, w), table.dtype),
        scratch_types=[pltpu.VMEM((SC_CHUNK, w), table.dtype), pltpu.VMEM((s, SC_CHUNK), I32),
                       pltpu.SemaphoreType.DMA],
        name="combine")
    def k(table_hbm, pos_hbm, out_hbm, rows_v, idx_v, sem):
        wid = lax.axis_index("s") * n_cores + lax.axis_index("c")

        @pl.loop(0, iters)
        def _(it):
            c = it * n_workers + wid

            @pl.when(c < n_chunks)
            def _():
                base = pl.multiple_of(c * SC_CHUNK, SC_CHUNK)
                pltpu.sync_copy(pos_hbm.at[:, pl.ds(base, SC_CHUNK)], idx_v)
                for j in range(s):
                    pltpu.async_copy(table_hbm.at[idx_v.at[j]], rows_v, sem).wait()
                    pltpu.sync_copy(rows_v, out_hbm.at[j, pl.ds(base, SC_CHUNK)])

    return k(table, pos_t)


def _experts_kernel(te_ref, valid_ref, used_ref, xs_ref, wg_ref, wu_ref, wd_ref, y_ref,
                    wg16_ref, wu16_ref, wd16_ref):
    i = pl.program_id(0)

    @pl.when(i < used_ref[0])
    def _():
        @pl.when((i == 0) | (te_ref[i] != te_ref[jnp.maximum(i - 1, 0)]))
        def _():
            wg16_ref[...] = wg_ref[0].astype(BF16)
            wu16_ref[...] = wu_ref[0].astype(BF16)
            wd16_ref[...] = wd_ref[0].astype(BF16)

        words = xs_ref[...]
        row = lax.broadcasted_iota(I32, words.shape, 0)
        words = jnp.where(row < valid_ref[i], words, jnp.uint32(0))
        lo, hi = _unpack_rows(words)
        lo, hi = lo.astype(BF16), hi.astype(BF16)
        hg = _dot(lo, wg16_ref[0:HALF, :]) + _dot(hi, wg16_ref[HALF:, :])
        hu = _dot(lo, wu16_ref[0:HALF, :]) + _dot(hi, wu16_ref[HALF:, :])
        a = (_silu(hg) * hu).astype(BF16)
        y_ref[...] = _pack_rows(_dot(a, wd16_ref[...]))


def _experts(xs, tile_expert, tile_valid, n_used, w_eg, w_eu, w_ed):
    n_rows = xs.shape[0]
    n_tiles = n_rows // ROW_TILE
    d = D_MODEL
    rows = lambda i, te, va, nu: (jnp.minimum(i, nu[0] - 1), 0)
    by_expert = lambda i, te, va, nu: (te[i], 0, 0)
    grid_spec = pltpu.PrefetchScalarGridSpec(
        num_scalar_prefetch=3,
        grid=(n_tiles,),
        in_specs=[pl.BlockSpec((ROW_TILE, HALF), rows),
                  pl.BlockSpec((1, d, EXPERT_DIM), by_expert),
                  pl.BlockSpec((1, d, EXPERT_DIM), by_expert),
                  pl.BlockSpec((1, EXPERT_DIM, d), by_expert)],
        out_specs=pl.BlockSpec((ROW_TILE, HALF), rows),
        scratch_shapes=[pltpu.VMEM((d, EXPERT_DIM), BF16), pltpu.VMEM((d, EXPERT_DIM), BF16),
                        pltpu.VMEM((EXPERT_DIM, d), BF16)])
    return pl.pallas_call(
        _experts_kernel,
        grid_spec=grid_spec,
        out_shape=jax.ShapeDtypeStruct((n_rows, HALF), U32),
        compiler_params=pltpu.CompilerParams(dimension_semantics=("arbitrary",),
                                             vmem_limit_bytes=VMEM_LIMIT),
        name="experts",
    )(tile_expert, tile_valid, n_used, xs, w_eg, w_eu, w_ed)


def _final_kernel(z_ref, gatew_ref, h2_ref, x1_ref, mod_ref, normf_ref, wsg_ref, wsu_ref, wsd_ref, y_ref):
    lo, hi = _unpack_rows(h2_ref[...])
    h = jnp.concatenate([lo, hi], axis=-1).astype(BF16)
    a = _silu(_dot(h, wsg_ref[...])) * _dot(h, wsu_ref[...])
    acc = _dot(a.astype(BF16), wsd_ref[...])
    for s in range(TOP_K):
        lo, hi = _unpack_rows(z_ref[s])
        acc = acc + gatew_ref[:, s:s + 1] * jnp.concatenate([lo, hi], axis=-1)
    x2 = x1_ref[...] + _mod(mod_ref, 5) * acc
    y_ref[...] = _rms(x2) * normf_ref[...]


def _final(z, gate_w, h2, x1, mod, norm_f, w_sg16, w_su16, w_sd16, block_t, first_block, per_seq):
    t, d = x1.shape
    tok = lambda i: (i, 0)
    if per_seq:
        mod_spec = pl.BlockSpec((1, 6, d), lambda i: (i // per_seq, 0, 0))
    else:
        mod_spec = pl.BlockSpec((block_t, 6 * d), tok)
    return pl.pallas_call(
        _final_kernel,
        grid=(t // block_t,),
        in_specs=[pl.BlockSpec((TOP_K, block_t, HALF), lambda i: (0, first_block + i, 0)),
                  pl.BlockSpec((block_t, LANES), tok),
                  pl.BlockSpec((block_t, HALF), tok),
                  pl.BlockSpec((block_t, d), tok),
                  mod_spec,
                  _full((1, d)),
                  _full((d, EXPERT_DIM)), _full((d, EXPERT_DIM)), _full((EXPERT_DIM, d))],
        out_specs=pl.BlockSpec((block_t, d), tok),
        out_shape=jax.ShapeDtypeStruct((t, d), F32),
        compiler_params=pltpu.CompilerParams(dimension_semantics=("arbitrary",),
                                             vmem_limit_bytes=VMEM_LIMIT),
        name="final",
    )(z, gate_w, h2, x1, mod, norm_f, w_sg16, w_su16, w_sd16)


def kernel(x_prompt, x_sample, c_prompt, c_sample, state_ret, state_pool, norm1, norm2, norm_f,
           w_ada, b_ada, w_in, w_out, w_pool, pool_scale, w_router, router_bias, w_exp_gate,
           w_exp_up, w_exp_down, w_sh_gate, w_sh_up, w_sh_down):
    b, l, d = x_prompt.shape
    n = x_sample.shape[0]
    n_tokens = b * l + n
    past_len = 16384

    mod = _ada(jnp.concatenate([c_prompt, c_sample], axis=0), w_ada[0], b_ada[0])
    mod_p = mod[:b].reshape(b, 6, d)
    mod_s = mod[b:]

    w_in16 = w_in[0].astype(BF16)
    w_out16 = w_out[0].astype(BF16)
    wr_t = w_router[0].T
    bias_t = jnp.broadcast_to(router_bias[0][:, None], (N_EXPERTS, LANES))
    n1, n2, nf = norm1[0].reshape(1, d), norm2[0].reshape(1, d), norm_f.reshape(1, d)
    ps = pool_scale[0].reshape(1, POOL_WIDTH)
    shared = (w_sh_gate[0].astype(BF16), w_sh_up[0].astype(BF16), w_sh_down[0].astype(BF16))

    x1_p, h2_p, experts_p, gatew_p, ret_p, pool_p = _mix_prompt(
        x_prompt, mod_p, n1, w_in16, w_pool[0], ps, w_out16, n2, wr_t, bias_t)
    x1_s, h2_s, experts_s, gatew_s, ret_s, pool_s = _mix_sample(
        x_sample.reshape(n, d), mod_s, state_ret[0], state_pool[0], float(past_len),
        n1, w_in16, w_pool[0], ps, w_out16, n2, wr_t, bias_t)

    n_tiles = n_tokens * TOP_K // ROW_TILE + N_EXPERTS
    n_tiles_pad = -(-n_tiles // LANES) * LANES
    pos_t, meta = _plan(jnp.concatenate([experts_p, experts_s], axis=1), n_tiles_pad)
    xs = _sc_scatter_rows(h2_p, h2_s, pos_t, n_tiles * ROW_TILE)
    ys = _experts(xs, meta[0], meta[1], meta[2, :1], w_exp_gate[0], w_exp_up[0], w_exp_down[0])
    z = _sc_gather_rows(ys, pos_t)

    block_t = 256
    y_p = _final(z, gatew_p, h2_p, x1_p.reshape(b * l, d), mod_p, nf, *shared,
                 block_t=block_t, first_block=0, per_seq=l // block_t)
    y_s = _final(z, gatew_s, h2_s, x1_s, mod_s, nf, *shared,
                 block_t=n, first_block=b * l // n, per_seq=0)

    return (y_p.reshape(b, l, d), y_s.reshape(n, 1, d), ret_p[None], pool_p[None],
            ret_s[None], pool_s[None])
```

```python
import functools

import jax
import jax.numpy as jnp
import numpy as np
from jax import lax
from jax.experimental import pallas as pl
from jax.experimental.pallas import tpu as pltpu
from jax.experimental.pallas import tpu_sc as plsc

D_MODEL = 1024
RET_HEADS = 4
RET_QK_DIM = 64
RET_V_DIM = 128
RET_WIDTH = RET_HEADS * RET_V_DIM
QK_WIDTH = RET_HEADS * RET_QK_DIM
ROPE_BASE = 10000.0
POOL_WINDOWS = (2, 4, 8, 16)
POOL_WIDTH = 512
POOL_GROUP_DIM = 128
POOL_BUF = 15
IN_WIDTH = 2 * QK_WIDTH + 2 * RET_WIDTH + POOL_WIDTH
N_EXPERTS = 64
TOP_K = 8
N_EXPERT_GROUPS = 8
GROUP_SIZE = N_EXPERTS // N_EXPERT_GROUPS
TOP_GROUPS = 4
EXPERT_DIM = 256
ROUTE_SCALE = 2.5
EPS = 1e-6

LANES = 128
POOL_CARRY = 16
VMEM_LIMIT = 56 * 1024 * 1024
HALF = D_MODEL // 2
ROW_TILE = 256
SC_CHUNK = 128

BF16 = jnp.bfloat16
F32 = jnp.float32
U32 = jnp.uint32
I32 = jnp.int32


def _silu(x):
    return x * jax.nn.sigmoid(x)


def _dot(a, b):
    return jnp.dot(a, b, preferred_element_type=F32)


def _rms(x):
    return x * lax.rsqrt(jnp.mean(x * x, axis=-1, keepdims=True) + EPS)


def _mod(mod_ref, i):
    if len(mod_ref.shape) == 3:
        return mod_ref[0, i:i + 1, :]
    return mod_ref[:, i * D_MODEL:(i + 1) * D_MODEL]


def _split_bf16(x):
    hi = x.astype(BF16)
    lo = (x - hi.astype(F32)).astype(BF16)
    return hi, lo


def _pack_rows(x):
    lo = lax.bitcast_convert_type(x[:, :HALF].astype(BF16).astype(F32), U32)
    hi = lax.bitcast_convert_type(x[:, HALF:].astype(BF16).astype(F32), U32)
    return (hi & jnp.uint32(0xFFFF0000)) | (lo >> jnp.uint32(16))


def _unpack_rows(w):
    lo = lax.bitcast_convert_type(w << jnp.uint32(16), F32)
    hi = lax.bitcast_convert_type(w & jnp.uint32(0xFFFF0000), F32)
    return lo, hi


def _first_max_onehot(work, idx, n):
    m = jnp.max(work, axis=0, keepdims=True)
    first = jnp.min(jnp.where(work == m, idx, float(n)), axis=0, keepdims=True)
    return idx == first


def _route(h2, wr_t_ref, bias_t_ref):
    n = h2.shape[0]
    h_hi, h_lo = _split_bf16(h2)
    w_hi, w_lo = _split_bf16(wr_t_ref[...])
    nt = (((1,), (1,)), ((), ()))
    logits = (lax.dot_general(w_hi, h_hi, nt, preferred_element_type=F32)
              + lax.dot_general(w_hi, h_lo, nt, preferred_element_type=F32)
              + lax.dot_general(w_lo, h_hi, nt, preferred_element_type=F32))
    scores = jax.nn.sigmoid(logits)
    biased = scores + bias_t_ref[:, 0:1]
    b3 = biased.reshape(N_EXPERT_GROUPS, GROUP_SIZE, n)
    i3 = lax.broadcasted_iota(I32, b3.shape, 1).astype(F32)
    m1 = jnp.max(b3, axis=1, keepdims=True)
    first = jnp.min(jnp.where(b3 == m1, i3, float(GROUP_SIZE)), axis=1, keepdims=True)
    m2 = jnp.max(jnp.where(i3 == first, -jnp.inf, b3), axis=1, keepdims=True)
    gscore = (m1 + m2).reshape(N_EXPERT_GROUPS, n)
    gidx = lax.broadcasted_iota(I32, gscore.shape, 0).astype(F32)
    gsel = jnp.zeros(gscore.shape, F32)
    work = gscore
    for _ in range(TOP_GROUPS):
        hit = _first_max_onehot(work, gidx, N_EXPERT_GROUPS)
        gsel = jnp.where(hit, 1.0, gsel)
        work = jnp.where(hit, -jnp.inf, work)
    gsel3 = jnp.broadcast_to(gsel.reshape(N_EXPERT_GROUPS, 1, n), b3.shape)
    work = jnp.where(gsel3 > 0.0, b3, -jnp.inf).reshape(N_EXPERTS, n)
    eidx = lax.broadcasted_iota(I32, work.shape, 0).astype(F32)
    sel = jnp.zeros(work.shape, F32)
    for _ in range(TOP_K):
        hit = _first_max_onehot(work, eidx, N_EXPERTS)
        sel = jnp.where(hit, 1.0, sel)
        work = jnp.where(hit, -jnp.inf, work)
    picked = jnp.where(sel > 0.0, scores, 0.0)
    gates = picked / jnp.sum(picked, axis=0, keepdims=True) * ROUTE_SCALE
    below = (lax.broadcasted_iota(I32, (N_EXPERTS, N_EXPERTS), 1)
             < lax.broadcasted_iota(I32, (N_EXPERTS, N_EXPERTS), 0))
    slot = _dot(jnp.where(below, 1.0, 0.0).astype(BF16), sel.astype(BF16))
    e_rows, w_rows = [], []
    for s in range(TOP_K):
        here = jnp.where(slot == float(s), sel, 0.0)
        e_rows.append(jnp.sum(here * eidx, axis=0, keepdims=True))
        w_rows.append(jnp.sum(here * gates, axis=0, keepdims=True))
    experts = jnp.concatenate(e_rows, axis=0).astype(I32)
    w_t = jnp.concatenate(w_rows + [jnp.zeros((LANES - TOP_K, n), F32)], axis=0)
    return experts, w_t.T


def _group_norm_gate(o, g):
    parts = []
    for h in range(RET_HEADS):
        oh = o[:, h * RET_V_DIM:(h + 1) * RET_V_DIM]
        mu = jnp.mean(oh, axis=-1, keepdims=True)
        ctr = oh - mu
        var = jnp.mean(ctr * ctr, axis=-1, keepdims=True)
        parts.append(ctr * lax.rsqrt(var + EPS))
    return _silu(g) * jnp.concatenate(parts, axis=-1)


def _pool_project(pooled, w_pool_ref, pool_scale_ref):
    parts = [_dot(p.astype(BF16), w_pool_ref[gi].astype(BF16)) for gi, p in enumerate(pooled)]
    return jnp.concatenate(parts, axis=-1) * pool_scale_ref[...]


def _mix_tail(x, o_gated, p, mod_ref, w_out_ref, norm2_ref, wr_t_ref, bias_t_ref,
              x1_ref, h2_ref, experts_ref, gatew_ref):
    mix = jnp.concatenate([o_gated, p], axis=-1).astype(BF16)
    y = _dot(mix, w_out_ref[...])
    x1 = x + _mod(mod_ref, 2) * y
    h2 = _rms(x1) * norm2_ref[...] * (1.0 + _mod(mod_ref, 4)) + _mod(mod_ref, 3)
    x1_ref[...] = x1.reshape(x1_ref.shape)
    h2_ref[...] = _pack_rows(h2)
    experts, gate_w = _route(h2, wr_t_ref, bias_t_ref)
    experts_ref[...] = experts
    gatew_ref[...] = gate_w


def _ada_kernel(c_ref, w_ref, b_ref, o_ref):
    cs = _silu(c_ref[...]).astype(BF16)
    o_ref[...] = _dot(cs, w_ref[...].astype(BF16)) + b_ref[...]


def _ada(c_all, w_ada, b_ada, block_n=1536):
    n, d = c_all.shape
    width = w_ada.shape[1]
    return pl.pallas_call(
        _ada_kernel,
        grid=(width // block_n,),
        in_specs=[pl.BlockSpec((n, d), lambda j: (0, 0)),
                  pl.BlockSpec((d, block_n), lambda j: (0, j)),
                  pl.BlockSpec((1, block_n), lambda j: (0, j))],
        out_specs=pl.BlockSpec((n, block_n), lambda j: (0, j)),
        out_shape=jax.ShapeDtypeStruct((n, width), F32),
        compiler_params=pltpu.CompilerParams(vmem_limit_bytes=VMEM_LIMIT),
        name="ada",
    )(c_all, w_ada, b_ada.reshape(1, width))


def _mix_prompt_kernel(x_ref, mod_ref, norm1_ref, w_in_ref, cos_ref, sin_ref, dmat_ref, cross_ref,
                       tail_ref, cdec_ref, w_pool_ref, pool_scale_ref, w_out_ref, norm2_ref,
                       wr_t_ref, bias_t_ref,
                       x1_ref, h2_ref, experts_ref, gatew_ref, ret_ref, pool_ref,
                       state_ref, ext_ref, o_ref, *, block_l, chunk):
    li = pl.program_id(1)

    @pl.when(li == 0)
    def _():
        state_ref[...] = jnp.zeros_like(state_ref)
        ext_ref[0:POOL_CARRY, :] = jnp.zeros((POOL_CARRY, POOL_WIDTH), F32)

    x = x_ref[0]
    h = _rms(x) * norm1_ref[...] * (1.0 + _mod(mod_ref, 1)) + _mod(mod_ref, 0)
    proj = _dot(h.astype(BF16), w_in_ref[...])
    q = proj[:, 0:QK_WIDTH]
    k = proj[:, QK_WIDTH:2 * QK_WIDTH]
    v = proj[:, 2 * QK_WIDTH:2 * QK_WIDTH + RET_WIDTH]
    g = proj[:, 2 * QK_WIDTH + RET_WIDTH:2 * QK_WIDTH + 2 * RET_WIDTH]
    u = proj[:, 2 * QK_WIDTH + 2 * RET_WIDTH:]

    lane = lax.broadcasted_iota(I32, q.shape, 1)
    first_half = (lane % RET_QK_DIM) < (RET_QK_DIM // 2)
    cos_t = cos_ref[...]
    sin_t = sin_ref[...]

    def rot(t):
        partner = jnp.where(first_half, pltpu.roll(t, QK_WIDTH - RET_QK_DIM // 2, axis=1),
                            pltpu.roll(t, RET_QK_DIM // 2, axis=1))
        return t * cos_t + partner * sin_t

    q = rot(q)
    k = rot(k) * (RET_QK_DIM ** -0.5)
    k_t = k.T
    v16 = v.astype(BF16)
    head_of_lane = lax.broadcasted_iota(I32, (chunk, QK_WIDTH), 1) // RET_QK_DIM

    for c in range(block_l // chunk):
        rows = slice(c * chunk, (c + 1) * chunk)
        q_c = q[rows]
        kt_c = k_t[:, rows]
        kt16 = kt_c.astype(BF16)
        state16 = state_ref[...].astype(BF16)
        for hd in range(RET_HEADS):
            in_head = head_of_lane == hd
            q_h = jnp.where(in_head, q_c, 0.0).astype(BF16)
            v_h = v16[rows, hd * RET_V_DIM:(hd + 1) * RET_V_DIM]
            scores = _dot(q_h, kt16) * dmat_ref[hd]
            inner = _dot(scores.astype(BF16), v_h)
            cross = _dot(q_h, state16) * cross_ref[hd]
            o_ref[rows, hd * RET_V_DIM:(hd + 1) * RET_V_DIM] = inner + cross
            hrows = slice(hd * RET_QK_DIM, (hd + 1) * RET_QK_DIM)
            k_dec = (kt_c[hrows] * tail_ref[hd:hd + 1, :]).astype(BF16)
            state_ref[hrows, :] = state_ref[hrows, :] * cdec_ref[hd] + _dot(k_dec, v_h)

    o_gated = _group_norm_gate(o_ref[...], g)

    ext_ref[POOL_CARRY:POOL_CARRY + block_l, :] = u
    pos = (li * block_l + lax.broadcasted_iota(I32, (block_l, 1), 0)).astype(F32)
    pooled = []
    for gi, w in enumerate(POOL_WINDOWS):
        lanes = slice(gi * POOL_GROUP_DIM, (gi + 1) * POOL_GROUP_DIM)
        acc = ext_ref[POOL_CARRY:POOL_CARRY + block_l, lanes]
        for j in range(1, w):
            acc = acc + ext_ref[POOL_CARRY - j:POOL_CARRY - j + block_l, lanes]
        cnt = jnp.minimum(pos + 1.0, float(w))
        pooled.append(acc / cnt - u[:, lanes])
    p = _pool_project(pooled, w_pool_ref, pool_scale_ref)
    ext_ref[0:POOL_CARRY, :] = ext_ref[block_l:block_l + POOL_CARRY, :]

    _mix_tail(x, o_gated, p, mod_ref, w_out_ref, norm2_ref, wr_t_ref, bias_t_ref,
              x1_ref, h2_ref, experts_ref, gatew_ref)

    @pl.when(li == pl.num_programs(1) - 1)
    def _():
        ret_ref[...] = state_ref[...].reshape(ret_ref.shape)
        pool_ref[...] = ext_ref[1:1 + POOL_BUF, :].reshape(pool_ref.shape)


def _decay_tables(chunk):
    lg = jnp.log(1.0 - 2.0 ** (-5.0 - jnp.arange(RET_HEADS, dtype=F32)))
    idx = jnp.arange(chunk, dtype=F32)
    diff = idx[:, None] - idx[None, :]
    causal = diff >= 0
    dmat = jnp.where(causal[None], jnp.exp(lg[:, None, None] * jnp.where(causal, diff, 0.0)[None]), 0.0)
    cross = jnp.exp(lg[:, None] * (idx[None, :] + 1.0))
    cross = jnp.broadcast_to(cross[:, :, None], (RET_HEADS, chunk, RET_V_DIM))
    tail = jnp.exp(lg[:, None] * (chunk - 1.0 - idx)[None, :])
    cdec = jnp.broadcast_to(jnp.exp(lg * chunk)[:, None, None], (RET_HEADS, RET_QK_DIM, RET_V_DIM))
    return dmat, cross, tail, cdec


def _rotary_tables(pos):
    half = RET_QK_DIM // 2
    freqs = ROPE_BASE ** (-jnp.arange(half, dtype=F32) / half)
    ang = pos[:, None] * freqs[None, :]
    cos, sin = jnp.cos(ang), jnp.sin(ang)
    cos_t = jnp.tile(jnp.concatenate([cos, cos], axis=-1), (1, RET_HEADS))
    sin_t = jnp.tile(jnp.concatenate([-sin, sin], axis=-1), (1, RET_HEADS))
    return cos_t, sin_t


def _full(shape):
    return pl.BlockSpec(shape, lambda *_: (0,) * len(shape))


def _mix_prompt(x, mod, norm1, w_in16, w_pool, pool_scale, w_out16, norm2, wr_t, bias_t,
                block_l=512, chunk=256):
    b, l, d = x.shape
    n_tokens = b * l
    nl = l // block_l
    cos_t, sin_t = _rotary_tables(jnp.arange(l, dtype=F32))
    dmat, cross, tail, cdec = _decay_tables(chunk)
    kernel = functools.partial(_mix_prompt_kernel, block_l=block_l, chunk=chunk)
    tok = lambda bi, li: (bi, li, 0)
    flat = lambda bi, li: (bi * nl + li, 0)
    return pl.pallas_call(
        kernel,
        grid=(b, nl),
        in_specs=[pl.BlockSpec((1, block_l, d), tok),
                  pl.BlockSpec((1, 6, d), lambda bi, li: (bi, 0, 0)),
                  _full((1, d)),
                  _full((d, IN_WIDTH)),
                  pl.BlockSpec((block_l, QK_WIDTH), lambda bi, li: (li, 0)),
                  pl.BlockSpec((block_l, QK_WIDTH), lambda bi, li: (li, 0)),
                  _full(dmat.shape), _full(cross.shape), _full(tail.shape), _full(cdec.shape),
                  _full(w_pool.shape), _full((1, POOL_WIDTH)), _full((d, d)), _full((1, d)),
                  _full(wr_t.shape), _full(bias_t.shape)],
        out_specs=[pl.BlockSpec((1, block_l, d), tok),
                   pl.BlockSpec((block_l, HALF), flat),
                   pl.BlockSpec((TOP_K, block_l), lambda bi, li: (0, bi * nl + li)),
                   pl.BlockSpec((block_l, LANES), flat),
                   pl.BlockSpec((1, RET_HEADS, RET_QK_DIM, RET_V_DIM), lambda bi, li: (bi, 0, 0, 0)),
                   pl.BlockSpec((1, POOL_BUF, POOL_WIDTH), lambda bi, li: (bi, 0, 0))],
        out_shape=[jax.ShapeDtypeStruct((b, l, d), F32),
                   jax.ShapeDtypeStruct((n_tokens, HALF), U32),
                   jax.ShapeDtypeStruct((TOP_K, n_tokens), I32),
                   jax.ShapeDtypeStruct((b * l, LANES), F32),
                   jax.ShapeDtypeStruct((b, RET_HEADS, RET_QK_DIM, RET_V_DIM), F32),
                   jax.ShapeDtypeStruct((b, POOL_BUF, POOL_WIDTH), F32)],
        scratch_shapes=[pltpu.VMEM((QK_WIDTH, RET_V_DIM), F32),
                        pltpu.VMEM((POOL_CARRY + block_l, POOL_WIDTH), F32),
                        pltpu.VMEM((block_l, RET_WIDTH), F32)],
        compiler_params=pltpu.CompilerParams(dimension_semantics=("arbitrary", "arbitrary"),
                                             vmem_limit_bytes=VMEM_LIMIT),
        name="mix_prompt",
    )(x, mod, norm1, w_in16, cos_t, sin_t, dmat, cross, tail, cdec, w_pool, pool_scale,
      w_out16, norm2, wr_t, bias_t)


def _mix_sample_front_kernel(x_ref, mod_ref, norm1_ref, w_in_ref, cos_ref, sin_ref,
                             qt_ref, kt_ref, v_ref, g_ref, u_ref):
    x = x_ref[...]
    h = _rms(x) * norm1_ref[...] * (1.0 + _mod(mod_ref, 1)) + _mod(mod_ref, 0)
    proj = _dot(h.astype(BF16), w_in_ref[...])
    half = RET_QK_DIM // 2
    cos_c = cos_ref[...]
    sin_c = sin_ref[...]

    def rot_t(t):
        parts = []
        for hd in range(RET_HEADS):
            t1 = t[hd * RET_QK_DIM:hd * RET_QK_DIM + half]
            t2 = t[hd * RET_QK_DIM + half:(hd + 1) * RET_QK_DIM]
            parts += [t1 * cos_c - t2 * sin_c, t1 * sin_c + t2 * cos_c]
        return jnp.concatenate(parts, axis=0)

    qt_ref[...] = rot_t(proj[:, 0:QK_WIDTH].T)
    kt_ref[...] = rot_t(proj[:, QK_WIDTH:2 * QK_WIDTH].T) * (RET_QK_DIM ** -0.5)
    v_ref[...] = proj[:, 2 * QK_WIDTH:2 * QK_WIDTH + RET_WIDTH]
    g_ref[...] = proj[:, 2 * QK_WIDTH + RET_WIDTH:2 * QK_WIDTH + 2 * RET_WIDTH]
    u_ref[...] = proj[:, 2 * QK_WIDTH + 2 * RET_WIDTH:]


def _ret_step_kernel(qt_ref, kt_ref, v_ref, s0_ref, o_ref, s1_ref, *, block_b, decays):
    i = pl.program_id(0)
    lane = lax.broadcasted_iota(I32, qt_ref.shape, 1)
    for j in range(block_b):
        bi = i * block_b + j
        here = lane == bi
        q_col = jnp.sum(jnp.where(here, qt_ref[...], 0.0), axis=1, keepdims=True)
        k_col = jnp.sum(jnp.where(here, kt_ref[...], 0.0), axis=1, keepdims=True)
        v_row = v_ref[pl.ds(bi, 1), :]
        outs = []
        for hd in range(RET_HEADS):
            hrows = slice(hd * RET_QK_DIM, (hd + 1) * RET_QK_DIM)
            s1 = decays[hd] * s0_ref[j, hd] + k_col[hrows] * v_row[:, hd * RET_V_DIM:(hd + 1) * RET_V_DIM]
            s1_ref[j, hd] = s1
            outs.append(jnp.sum(q_col[hrows] * s1, axis=0, keepdims=True))
        o_ref[pl.ds(bi, 1), :] = jnp.concatenate(outs, axis=-1)


def _mix_sample_back_kernel(x_ref, mod_ref, o_ref, g_ref, u_ref, buf_ref, w_pool_ref, pool_scale_ref,
                            w_out_ref, norm2_ref, wr_t_ref, bias_t_ref,
                            x1_ref, h2_ref, experts_ref, gatew_ref, pool_ref):
    o_gated = _group_norm_gate(o_ref[...], g_ref[...])
    u = u_ref[...]
    pooled = []
    for gi, w in enumerate(POOL_WINDOWS):
        lanes = slice(gi * POOL_GROUP_DIM, (gi + 1) * POOL_GROUP_DIM)
        acc = u[:, lanes]
        for j in range(1, w):
            acc = acc + buf_ref[:, POOL_BUF - j, lanes]
        pooled.append(acc / float(w) - u[:, lanes])
    p = _pool_project(pooled, w_pool_ref, pool_scale_ref)
    pool_ref[:, 0:POOL_BUF - 1, :] = buf_ref[:, 1:POOL_BUF, :]
    pool_ref[:, POOL_BUF - 1, :] = u
    _mix_tail(x_ref[...], o_gated, p, mod_ref, w_out_ref, norm2_ref, wr_t_ref, bias_t_ref,
              x1_ref, h2_ref, experts_ref, gatew_ref)


def _mix_sample(x, mod, state_ret, state_pool, start, norm1, w_in16, w_pool,
                pool_scale, w_out16, norm2, wr_t, bias_t, block_b=8):
    n, d = x.shape
    half = RET_QK_DIM // 2
    freqs = ROPE_BASE ** (-jnp.arange(half, dtype=F32) / half)
    ang = jnp.full((1,), start, F32)[:, None] * freqs[None, :]
    cos_c = jnp.broadcast_to(jnp.cos(ang).T, (half, n))
    sin_c = jnp.broadcast_to(jnp.sin(ang).T, (half, n))
    params = pltpu.CompilerParams(vmem_limit_bytes=VMEM_LIMIT)
    qt, kt, v, g, u = pl.pallas_call(
        _mix_sample_front_kernel,
        out_shape=[jax.ShapeDtypeStruct((QK_WIDTH, n), F32), jax.ShapeDtypeStruct((QK_WIDTH, n), F32),
                   jax.ShapeDtypeStruct((n, RET_WIDTH), F32), jax.ShapeDtypeStruct((n, RET_WIDTH), F32),
                   jax.ShapeDtypeStruct((n, POOL_WIDTH), F32)],
        compiler_params=params,
        name="mix_sample_front",
    )(x, mod, norm1, w_in16, cos_c, sin_c)

    lg = np.log(1.0 - 2.0 ** (-5.0 - np.arange(RET_HEADS, dtype=np.float32)), dtype=np.float32)
    decays = tuple(float(np.exp(lg[h])) for h in range(RET_HEADS))
    state_block = (block_b, RET_HEADS, RET_QK_DIM, RET_V_DIM)
    o, s1 = pl.pallas_call(
        functools.partial(_ret_step_kernel, block_b=block_b, decays=decays),
        grid=(n // block_b,),
        in_specs=[_full((QK_WIDTH, n)), _full((QK_WIDTH, n)), _full((n, RET_WIDTH)),
                  pl.BlockSpec(state_block, lambda i: (i, 0, 0, 0))],
        out_specs=[_full((n, RET_WIDTH)), pl.BlockSpec(state_block, lambda i: (i, 0, 0, 0))],
        out_shape=[jax.ShapeDtypeStruct((n, RET_WIDTH), F32),
                   jax.ShapeDtypeStruct(state_ret.shape, F32)],
        compiler_params=pltpu.CompilerParams(dimension_semantics=("arbitrary",),
                                             vmem_limit_bytes=VMEM_LIMIT),
        name="ret_step",
    )(qt, kt, v, state_ret)

    x1, h2, experts, gate_w, pool = pl.pallas_call(
        _mix_sample_back_kernel,
        out_shape=[jax.ShapeDtypeStruct((n, d), F32),
                   jax.ShapeDtypeStruct((n, HALF), U32),
                   jax.ShapeDtypeStruct((TOP_K, n), I32),
                   jax.ShapeDtypeStruct((n, LANES), F32),
                   jax.ShapeDtypeStruct(state_pool.shape, F32)],
        compiler_params=params,
        name="mix_sample_back",
    )(x, mod, o, g, u, state_pool, w_pool, pool_scale, w_out16, norm2, wr_t, bias_t)
    return x1, h2, experts, gate_w, s1, pool


def _plan_kernel(experts_ref, pos_ref, meta_ref, cnt_ref, carry_ref, off_ref, *, block_t):
    phase = pl.program_id(0)
    j = pl.program_id(1)
    e_blk = experts_ref[...]
    eidx = lax.broadcasted_iota(I32, (N_EXPERTS, block_t), 0)
    member = jnp.zeros((N_EXPERTS, block_t), F32)
    for s in range(TOP_K):
        member = member + jnp.where(eidx == e_blk[s:s + 1, :], 1.0, 0.0)
    per_expert = jnp.broadcast_to(jnp.sum(member, axis=1, keepdims=True), (N_EXPERTS, LANES))

    @pl.when((phase == 0) & (j == 0))
    def _():
        cnt_ref[...] = jnp.zeros_like(cnt_ref)

    @pl.when(phase == 0)
    def _():
        cnt_ref[...] += per_expert

    @pl.when((phase == 0) & (j == pl.num_programs(1) - 1))
    def _():
        cnt = cnt_ref[...]
        n_tile = jnp.floor((cnt + (ROW_TILE - 1.0)) * (1.0 / ROW_TILE))
        upto = (lax.broadcasted_iota(I32, (N_EXPERTS, N_EXPERTS), 1)
                <= lax.broadcasted_iota(I32, (N_EXPERTS, N_EXPERTS), 0))
        tile_end = _dot(jnp.where(upto, 1.0, 0.0).astype(BF16), n_tile.astype(BF16))
        tile_start = tile_end - n_tile
        off_ref[...] = tile_start * ROW_TILE
        carry_ref[...] = jnp.zeros_like(carry_ref)
        lane = lax.broadcasted_iota(I32, cnt.shape, 1)
        meta_ref[...] = jnp.where(lane == 0, tile_start, jnp.where(lane == 1, n_tile, cnt)).astype(I32)

    @pl.when(phase == 1)
    def _():
        before = (lax.broadcasted_iota(I32, (block_t, block_t), 0)
                  < lax.broadcasted_iota(I32, (block_t, block_t), 1))
        rank = _dot(member.astype(BF16), jnp.where(before, 1.0, 0.0).astype(BF16))
        row = off_ref[:, 0:1] + carry_ref[:, 0:1] + rank
        carry_ref[...] += per_expert
        out = [jnp.sum(jnp.where(eidx == e_blk[s:s + 1, :], row, 0.0), axis=0, keepdims=True)
               for s in range(TOP_K)]
        pos_ref[...] = jnp.concatenate(out, axis=0).astype(I32)


def _plan(experts_all, block_t=384):
    n_tokens = experts_all.shape[1]
    nb = n_tokens // block_t
    return pl.pallas_call(
        functools.partial(_plan_kernel, block_t=block_t),
        grid=(2, nb),
        in_specs=[pl.BlockSpec((TOP_K, block_t), lambda ph, j: (0, j))],
        out_specs=[pl.BlockSpec((TOP_K, block_t), lambda ph, j: (0, j * ph)),
                   _full((N_EXPERTS, LANES))],
        out_shape=[jax.ShapeDtypeStruct((TOP_K, n_tokens), I32),
                   jax.ShapeDtypeStruct((N_EXPERTS, LANES), I32)],
        scratch_shapes=[pltpu.VMEM((N_EXPERTS, LANES), F32)] * 3,
        compiler_params=pltpu.CompilerParams(dimension_semantics=("arbitrary", "arbitrary"),
                                             vmem_limit_bytes=VMEM_LIMIT),
        name="plan",
    )(experts_all)


def _sc_workers():
    info = plsc.get_sparse_core_info()
    return info.num_cores, info.num_cores * info.num_subcores


def _sc_scatter_rows(rows_a, rows_b, pos_t, n_out):
    ta, w = rows_a.shape
    s = pos_t.shape[0]
    n_cores, n_workers = _sc_workers()
    chunks_a = ta // SC_CHUNK
    n_chunks = pos_t.shape[1] // SC_CHUNK
    iters = -(-n_chunks // n_workers)
    mesh = plsc.VectorSubcoreMesh(core_axis_name="c", subcore_axis_name="s")

    @functools.partial(
        pl.kernel, mesh=mesh, out_type=jax.ShapeDtypeStruct((n_out, w), rows_a.dtype),
        scratch_types=[pltpu.VMEM((SC_CHUNK, w), rows_a.dtype), pltpu.VMEM((s, SC_CHUNK), I32),
                       pltpu.SemaphoreType.DMA],
        name="dispatch")
    def k(a_hbm, b_hbm, pos_hbm, out_hbm, rows_v, idx_v, sem):
        wid = lax.axis_index("s") * n_cores + lax.axis_index("c")

        @pl.loop(0, iters)
        def _(it):
            c = it * n_workers + wid

            @pl.when(c < chunks_a)
            def _():
                pltpu.sync_copy(a_hbm.at[pl.ds(pl.multiple_of(c * SC_CHUNK, SC_CHUNK), SC_CHUNK)], rows_v)

            @pl.when((c >= chunks_a) & (c < n_chunks))
            def _():
                base_b = pl.multiple_of((c - chunks_a) * SC_CHUNK, SC_CHUNK)
                pltpu.sync_copy(b_hbm.at[pl.ds(base_b, SC_CHUNK)], rows_v)

            @pl.when(c < n_chunks)
            def _():
                base = pl.multiple_of(c * SC_CHUNK, SC_CHUNK)
                pltpu.sync_copy(pos_hbm.at[:, pl.ds(base, SC_CHUNK)], idx_v)
                copies = [pltpu.async_copy(rows_v, out_hbm.at[idx_v.at[j]], sem) for j in range(s)]
                for cp in copies:
                    cp.wait()

    return k(rows_a, rows_b, pos_t)


def _sc_gather_rows(table, pos_t):
    _, w = table.shape
    s, t = pos_t.shape
    n_cores, n_workers = _sc_workers()
    n_chunks = t // SC_CHUNK
    iters = -(-n_chunks // n_workers)
    mesh = plsc.VectorSubcoreMesh(core_axis_name="c", subcore_axis_name="s")

    @functools.partial(
        pl.kernel, mesh=mesh, out_type=jax.ShapeDtypeStruct((s, t, w), table.dtype),
        scratch_types=[pltpu.VMEM((SC_CHUNK, w), table.dtype), pltpu.VMEM((s, SC_CHUNK), I32),
                       pltpu.SemaphoreType.DMA],
        name="combine")
    def k(table_hbm, pos_hbm, out_hbm, rows_v, idx_v, sem):
        wid = lax.axis_index("s") * n_cores + lax.axis_index("c")

        @pl.loop(0, iters)
        def _(it):
            c = it * n_workers + wid

            @pl.when(c < n_chunks)
            def _():
                base = pl.multiple_of(c * SC_CHUNK, SC_CHUNK)
                pltpu.sync_copy(pos_hbm.at[:, pl.ds(base, SC_CHUNK)], idx_v)
                for j in range(s):
                    pltpu.async_copy(table_hbm.at[idx_v.at[j]], rows_v, sem).wait()
                    pltpu.sync_copy(rows_v, out_hbm.at[j, pl.ds(base, SC_CHUNK)])

    return k(table, pos_t)


def _experts_kernel(first_ref, ntile_ref, cnt_ref, xs_hbm, wg_ref, wu_ref, wd_ref, ys_hbm,
                    wg16_ref, wu16_ref, wd16_ref, x_buf, y_buf, in_sem, out_sem):
    e = pl.program_id(0)
    n_used = first_ref[N_EXPERTS - 1] + ntile_ref[N_EXPERTS - 1]
    first, n_mine, count = first_ref[e], ntile_ref[e], cnt_ref[e]

    def tile_rows(g):
        return pl.ds(pl.multiple_of(g * ROW_TILE, ROW_TILE), ROW_TILE)

    def load(g, slot):
        return pltpu.make_async_copy(xs_hbm.at[tile_rows(g)], x_buf.at[slot], in_sem.at[slot])

    def store(g, slot):
        return pltpu.make_async_copy(y_buf.at[slot], ys_hbm.at[tile_rows(g)], out_sem.at[slot])

    @pl.when(e == 0)
    def _():
        load(0, 0).start()

    wg16_ref[...] = wg_ref[0].astype(BF16)
    wu16_ref[...] = wu_ref[0].astype(BF16)
    wd16_ref[...] = wd_ref[0].astype(BF16)

    def tile(j, carry):
        g = first + j
        slot = lax.rem(g, 2)
        load(g, slot).wait()

        @pl.when(g + 1 < n_used)
        def _():
            load(g + 1, 1 - slot).start()

        @pl.when(g >= 2)
        def _():
            store(g - 2, slot).wait()

        words = x_buf[slot]
        row = lax.broadcasted_iota(I32, words.shape, 0)
        words = jnp.where(row < count - j * ROW_TILE, words, jnp.uint32(0))
        lo, hi = _unpack_rows(words)
        lo, hi = lo.astype(BF16), hi.astype(BF16)
        hg = _dot(lo, wg16_ref[0:HALF, :]) + _dot(hi, wg16_ref[HALF:, :])
        hu = _dot(lo, wu16_ref[0:HALF, :]) + _dot(hi, wu16_ref[HALF:, :])
        a = (_silu(hg) * hu).astype(BF16)
        y_buf[slot] = _pack_rows(_dot(a, wd16_ref[...]))
        store(g, slot).start()
        return carry

    lax.fori_loop(0, n_mine, tile, 0)

    @pl.when(e == N_EXPERTS - 1)
    def _():
        @pl.when(n_used >= 2)
        def _():
            store(n_used - 2, lax.rem(n_used - 2, 2)).wait()

        store(n_used - 1, lax.rem(n_used - 1, 2)).wait()


def _experts(xs, first_tile, n_tile, count, w_eg, w_eu, w_ed):
    d = D_MODEL
    by_expert = lambda e, *_: (e, 0, 0)
    grid_spec = pltpu.PrefetchScalarGridSpec(
        num_scalar_prefetch=3,
        grid=(N_EXPERTS,),
        in_specs=[pl.BlockSpec(memory_space=pl.ANY),
                  pl.BlockSpec((1, d, EXPERT_DIM), by_expert),
                  pl.BlockSpec((1, d, EXPERT_DIM), by_expert),
                  pl.BlockSpec((1, EXPERT_DIM, d), by_expert)],
        out_specs=pl.BlockSpec(memory_space=pl.ANY),
        scratch_shapes=[pltpu.VMEM((d, EXPERT_DIM), BF16), pltpu.VMEM((d, EXPERT_DIM), BF16),
                        pltpu.VMEM((EXPERT_DIM, d), BF16),
                        pltpu.VMEM((2, ROW_TILE, HALF), U32), pltpu.VMEM((2, ROW_TILE, HALF), U32),
                        pltpu.SemaphoreType.DMA((2,)), pltpu.SemaphoreType.DMA((2,))])
    return pl.pallas_call(
        _experts_kernel,
        grid_spec=grid_spec,
        out_shape=jax.ShapeDtypeStruct(xs.shape, U32),
        compiler_params=pltpu.CompilerParams(dimension_semantics=("arbitrary",),
                                             vmem_limit_bytes=VMEM_LIMIT),
        name="experts",
    )(first_tile, n_tile, count, xs, w_eg, w_eu, w_ed)


def _final_kernel(z_ref, gatew_ref, h2_ref, x1_ref, mod_ref, normf_ref, wsg_ref, wsu_ref, wsd_ref, y_ref):
    lo, hi = _unpack_rows(h2_ref[...])
    h = jnp.concatenate([lo, hi], axis=-1).astype(BF16)
    a = _silu(_dot(h, wsg_ref[...])) * _dot(h, wsu_ref[...])
    acc = _dot(a.astype(BF16), wsd_ref[...])
    for s in range(TOP_K):
        lo, hi = _unpack_rows(z_ref[s])
        acc = acc + gatew_ref[:, s:s + 1] * jnp.concatenate([lo, hi], axis=-1)
    x2 = x1_ref[...] + _mod(mod_ref, 5) * acc
    y_ref[...] = _rms(x2) * normf_ref[...]


def _final(z, gate_w, h2, x1, mod, norm_f, w_sg16, w_su16, w_sd16, block_t, first_block, per_seq):
    t, d = x1.shape
    tok = lambda i: (i, 0)
    if per_seq:
        mod_spec = pl.BlockSpec((1, 6, d), lambda i: (i // per_seq, 0, 0))
    else:
        mod_spec = pl.BlockSpec((block_t, 6 * d), tok)
    return pl.pallas_call(
        _final_kernel,
        grid=(t // block_t,),
        in_specs=[pl.BlockSpec((TOP_K, block_t, HALF), lambda i: (0, first_block + i, 0)),
                  pl.BlockSpec((block_t, LANES), tok),
                  pl.BlockSpec((block_t, HALF), tok),
                  pl.BlockSpec((block_t, d), tok),
                  mod_spec,
                  _full((1, d)),
                  _full((d, EXPERT_DIM)), _full((d, EXPERT_DIM)), _full((EXPERT_DIM, d))],
        out_specs=pl.BlockSpec((block_t, d), tok),
        out_shape=jax.ShapeDtypeStruct((t, d), F32),
        compiler_params=pltpu.CompilerParams(dimension_semantics=("arbitrary",),
                                             vmem_limit_bytes=VMEM_LIMIT),
        name="final",
    )(z, gate_w, h2, x1, mod, norm_f, w_sg16, w_su16, w_sd16)


def kernel(x_prompt, x_sample, c_prompt, c_sample, state_ret, state_pool, norm1, norm2, norm_f,
           w_ada, b_ada, w_in, w_out, w_pool, pool_scale, w_router, router_bias, w_exp_gate,
           w_exp_up, w_exp_down, w_sh_gate, w_sh_up, w_sh_down):
    b, l, d = x_prompt.shape
    n = x_sample.shape[0]
    n_tokens = b * l + n
    past_len = 16384

    mod = _ada(jnp.concatenate([c_prompt, c_sample], axis=0), w_ada[0], b_ada[0])
    mod_p = mod[:b].reshape(b, 6, d)
    mod_s = mod[b:]

    w_in16 = w_in[0].astype(BF16)
    w_out16 = w_out[0].astype(BF16)
    wr_t = w_router[0].T
    bias_t = jnp.broadcast_to(router_bias[0][:, None], (N_EXPERTS, LANES))
    n1, n2, nf = norm1[0].reshape(1, d), norm2[0].reshape(1, d), norm_f.reshape(1, d)
    ps = pool_scale[0].reshape(1, POOL_WIDTH)
    shared = (w_sh_gate[0].astype(BF16), w_sh_up[0].astype(BF16), w_sh_down[0].astype(BF16))

    x1_p, h2_p, experts_p, gatew_p, ret_p, pool_p = _mix_prompt(
        x_prompt, mod_p, n1, w_in16, w_pool[0], ps, w_out16, n2, wr_t, bias_t)
    x1_s, h2_s, experts_s, gatew_s, ret_s, pool_s = _mix_sample(
        x_sample.reshape(n, d), mod_s, state_ret[0], state_pool[0], float(past_len),
        n1, w_in16, w_pool[0], ps, w_out16, n2, wr_t, bias_t)

    n_tiles = n_tokens * TOP_K // ROW_TILE + N_EXPERTS
    pos_t, meta = _plan(jnp.concatenate([experts_p, experts_s], axis=1))
    xs = _sc_scatter_rows(h2_p, h2_s, pos_t, n_tiles * ROW_TILE)
    ys = _experts(xs, meta[:, 0], meta[:, 1], meta[:, 2], w_exp_gate[0], w_exp_up[0], w_exp_down[0])
    z = _sc_gather_rows(ys, pos_t)

    block_t = 256
    y_p = _final(z, gatew_p, h2_p, x1_p.reshape(b * l, d), mod_p, nf, *shared,
                 block_t=block_t, first_block=0, per_seq=l // block_t)
    y_s = _final(z, gatew_s, h2_s, x1_s, mod_s, nf, *shared,
                 block_t=n, first_block=b * l // n, per_seq=0)

    return (y_p.reshape(b, l, d), y_s.reshape(n, 1, d), ret_p[None], pool_p[None],
            ret_s[None], pool_s[None])
```

```python
import functools

import jax
import jax.numpy as jnp
import numpy as np
from jax import lax
from jax.experimental import pallas as pl
from jax.experimental.pallas import tpu as pltpu
from jax.experimental.pallas import tpu_sc as plsc

D_MODEL = 1024
RET_HEADS = 4
RET_QK_DIM = 64
RET_V_DIM = 128
RET_WIDTH = RET_HEADS * RET_V_DIM
QK_WIDTH = RET_HEADS * RET_QK_DIM
ROPE_BASE = 10000.0
POOL_WINDOWS = (2, 4, 8, 16)
POOL_WIDTH = 512
POOL_GROUP_DIM = 128
POOL_BUF = 15
IN_WIDTH = 2 * QK_WIDTH + 2 * RET_WIDTH + POOL_WIDTH
N_EXPERTS = 64
TOP_K = 8
N_EXPERT_GROUPS = 8
GROUP_SIZE = N_EXPERTS // N_EXPERT_GROUPS
TOP_GROUPS = 4
EXPERT_DIM = 256
ROUTE_SCALE = 2.5
EPS = 1e-6

LANES = 128
POOL_CARRY = 16
VMEM_LIMIT = 56 * 1024 * 1024
HALF = D_MODEL // 2
ROW_TILE = 256
SC_CHUNK = 128
STREAM_DEPTH = 4

BF16 = jnp.bfloat16
F32 = jnp.float32
U32 = jnp.uint32
I32 = jnp.int32


def _silu(x):
    return x * jax.nn.sigmoid(x)


def _dot(a, b):
    return jnp.dot(a, b, preferred_element_type=F32)


def _rms(x):
    return x * lax.rsqrt(jnp.mean(x * x, axis=-1, keepdims=True) + EPS)


def _mod(mod_ref, i):
    if len(mod_ref.shape) == 3:
        return mod_ref[0, i:i + 1, :]
    return mod_ref[:, i * D_MODEL:(i + 1) * D_MODEL]


def _split_bf16(x):
    hi = x.astype(BF16)
    lo = (x - hi.astype(F32)).astype(BF16)
    return hi, lo


def _pack_rows(x):
    lo = lax.bitcast_convert_type(x[:, :HALF].astype(BF16).astype(F32), U32)
    hi = lax.bitcast_convert_type(x[:, HALF:].astype(BF16).astype(F32), U32)
    return (hi & jnp.uint32(0xFFFF0000)) | (lo >> jnp.uint32(16))


def _unpack_rows(w):
    lo = lax.bitcast_convert_type(w << jnp.uint32(16), F32)
    hi = lax.bitcast_convert_type(w & jnp.uint32(0xFFFF0000), F32)
    return lo, hi


def _first_max_onehot(work, idx, n):
    m = jnp.max(work, axis=0, keepdims=True)
    first = jnp.min(jnp.where(work == m, idx, float(n)), axis=0, keepdims=True)
    return idx == first


def _route(h2, wr_t_ref, bias_t_ref):
    n = h2.shape[0]
    h_hi, h_lo = _split_bf16(h2)
    w_hi, w_lo = _split_bf16(wr_t_ref[...])
    nt = (((1,), (1,)), ((), ()))
    logits = (lax.dot_general(w_hi, h_hi, nt, preferred_element_type=F32)
              + lax.dot_general(w_hi, h_lo, nt, preferred_element_type=F32)
              + lax.dot_general(w_lo, h_hi, nt, preferred_element_type=F32))
    scores = jax.nn.sigmoid(logits)
    biased = scores + bias_t_ref[:, 0:1]
    b3 = biased.reshape(N_EXPERT_GROUPS, GROUP_SIZE, n)
    i3 = lax.broadcasted_iota(I32, b3.shape, 1).astype(F32)
    m1 = jnp.max(b3, axis=1, keepdims=True)
    first = jnp.min(jnp.where(b3 == m1, i3, float(GROUP_SIZE)), axis=1, keepdims=True)
    m2 = jnp.max(jnp.where(i3 == first, -jnp.inf, b3), axis=1, keepdims=True)
    gscore = (m1 + m2).reshape(N_EXPERT_GROUPS, n)
    gidx = lax.broadcasted_iota(I32, gscore.shape, 0).astype(F32)
    gsel = jnp.zeros(gscore.shape, F32)
    work = gscore
    for _ in range(TOP_GROUPS):
        hit = _first_max_onehot(work, gidx, N_EXPERT_GROUPS)
        gsel = jnp.where(hit, 1.0, gsel)
        work = jnp.where(hit, -jnp.inf, work)
    gsel3 = jnp.broadcast_to(gsel.reshape(N_EXPERT_GROUPS, 1, n), b3.shape)
    work = jnp.where(gsel3 > 0.0, b3, -jnp.inf).reshape(N_EXPERTS, n)
    eidx = lax.broadcasted_iota(I32, work.shape, 0).astype(F32)
    sel = jnp.zeros(work.shape, F32)
    for _ in range(TOP_K):
        hit = _first_max_onehot(work, eidx, N_EXPERTS)
        sel = jnp.where(hit, 1.0, sel)
        work = jnp.where(hit, -jnp.inf, work)
    picked = jnp.where(sel > 0.0, scores, 0.0)
    gates = picked / jnp.sum(picked, axis=0, keepdims=True) * ROUTE_SCALE
    below = (lax.broadcasted_iota(I32, (N_EXPERTS, N_EXPERTS), 1)
             < lax.broadcasted_iota(I32, (N_EXPERTS, N_EXPERTS), 0))
    slot = _dot(jnp.where(below, 1.0, 0.0).astype(BF16), sel.astype(BF16))
    e_rows, w_rows = [], []
    for s in range(TOP_K):
        here = jnp.where(slot == float(s), sel, 0.0)
        e_rows.append(jnp.sum(here * eidx, axis=0, keepdims=True))
        w_rows.append(jnp.sum(here * gates, axis=0, keepdims=True))
    experts = jnp.concatenate(e_rows, axis=0).astype(I32)
    w_t = jnp.concatenate(w_rows + [jnp.zeros((LANES - TOP_K, n), F32)], axis=0)
    return experts, w_t.T


def _group_norm_gate(o, g):
    parts = []
    for h in range(RET_HEADS):
        oh = o[:, h * RET_V_DIM:(h + 1) * RET_V_DIM]
        mu = jnp.mean(oh, axis=-1, keepdims=True)
        ctr = oh - mu
        var = jnp.mean(ctr * ctr, axis=-1, keepdims=True)
        parts.append(ctr * lax.rsqrt(var + EPS))
    return _silu(g) * jnp.concatenate(parts, axis=-1)


def _pool_project(pooled, w_pool_ref, pool_scale_ref):
    parts = [_dot(p.astype(BF16), w_pool_ref[gi].astype(BF16)) for gi, p in enumerate(pooled)]
    return jnp.concatenate(parts, axis=-1) * pool_scale_ref[...]


def _mix_tail(x, o_gated, p, mod_ref, w_out_ref, norm2_ref, wr_t_ref, bias_t_ref,
              x1_ref, h2_ref, experts_ref, gatew_ref):
    mix = jnp.concatenate([o_gated, p], axis=-1).astype(BF16)
    y = _dot(mix, w_out_ref[...])
    x1 = x + _mod(mod_ref, 2) * y
    h2 = _rms(x1) * norm2_ref[...] * (1.0 + _mod(mod_ref, 4)) + _mod(mod_ref, 3)
    x1_ref[...] = x1.reshape(x1_ref.shape)
    h2_ref[...] = _pack_rows(h2)
    experts, gate_w = _route(h2, wr_t_ref, bias_t_ref)
    experts_ref[...] = experts
    gatew_ref[...] = gate_w


def _ada_kernel(c_ref, w_ref, b_ref, o_ref):
    cs = _silu(c_ref[...]).astype(BF16)
    o_ref[...] = _dot(cs, w_ref[...].astype(BF16)) + b_ref[...]


def _ada(c_all, w_ada, b_ada, block_n=1536):
    n, d = c_all.shape
    width = w_ada.shape[1]
    return pl.pallas_call(
        _ada_kernel,
        grid=(width // block_n,),
        in_specs=[pl.BlockSpec((n, d), lambda j: (0, 0)),
                  pl.BlockSpec((d, block_n), lambda j: (0, j)),
                  pl.BlockSpec((1, block_n), lambda j: (0, j))],
        out_specs=pl.BlockSpec((n, block_n), lambda j: (0, j)),
        out_shape=jax.ShapeDtypeStruct((n, width), F32),
        compiler_params=pltpu.CompilerParams(vmem_limit_bytes=VMEM_LIMIT),
        name="ada",
    )(c_all, w_ada, b_ada.reshape(1, width))


def _mix_prompt_kernel(x_ref, mod_ref, norm1_ref, w_in_ref, cos_ref, sin_ref, dmat_ref, cross_ref,
                       tail_ref, cdec_ref, w_pool_ref, pool_scale_ref, w_out_ref, norm2_ref,
                       wr_t_ref, bias_t_ref,
                       x1_ref, h2_ref, experts_ref, gatew_ref, ret_ref, pool_ref,
                       state_ref, ext_ref, o_ref, *, block_l, chunk):
    li = pl.program_id(1)

    @pl.when(li == 0)
    def _():
        state_ref[...] = jnp.zeros_like(state_ref)
        ext_ref[0:POOL_CARRY, :] = jnp.zeros((POOL_CARRY, POOL_WIDTH), F32)

    x = x_ref[0]
    h = _rms(x) * norm1_ref[...] * (1.0 + _mod(mod_ref, 1)) + _mod(mod_ref, 0)
    proj = _dot(h.astype(BF16), w_in_ref[...])
    q = proj[:, 0:QK_WIDTH]
    k = proj[:, QK_WIDTH:2 * QK_WIDTH]
    v = proj[:, 2 * QK_WIDTH:2 * QK_WIDTH + RET_WIDTH]
    g = proj[:, 2 * QK_WIDTH + RET_WIDTH:2 * QK_WIDTH + 2 * RET_WIDTH]
    u = proj[:, 2 * QK_WIDTH + 2 * RET_WIDTH:]

    lane = lax.broadcasted_iota(I32, q.shape, 1)
    first_half = (lane % RET_QK_DIM) < (RET_QK_DIM // 2)
    cos_t = cos_ref[...]
    sin_t = sin_ref[...]

    def rot(t):
        partner = jnp.where(first_half, pltpu.roll(t, QK_WIDTH - RET_QK_DIM // 2, axis=1),
                            pltpu.roll(t, RET_QK_DIM // 2, axis=1))
        return t * cos_t + partner * sin_t

    q = rot(q)
    k = rot(k) * (RET_QK_DIM ** -0.5)
    k_t = k.T
    v16 = v.astype(BF16)
    head_of_lane = lax.broadcasted_iota(I32, (chunk, QK_WIDTH), 1) // RET_QK_DIM

    for c in range(block_l // chunk):
        rows = slice(c * chunk, (c + 1) * chunk)
        q_c = q[rows]
        kt_c = k_t[:, rows]
        kt16 = kt_c.astype(BF16)
        state16 = state_ref[...].astype(BF16)
        for hd in range(RET_HEADS):
            in_head = head_of_lane == hd
            q_h = jnp.where(in_head, q_c, 0.0).astype(BF16)
            v_h = v16[rows, hd * RET_V_DIM:(hd + 1) * RET_V_DIM]
            scores = _dot(q_h, kt16) * dmat_ref[hd]
            inner = _dot(scores.astype(BF16), v_h)
            cross = _dot(q_h, state16) * cross_ref[hd]
            o_ref[rows, hd * RET_V_DIM:(hd + 1) * RET_V_DIM] = inner + cross
            hrows = slice(hd * RET_QK_DIM, (hd + 1) * RET_QK_DIM)
            k_dec = (kt_c[hrows] * tail_ref[hd:hd + 1, :]).astype(BF16)
            state_ref[hrows, :] = state_ref[hrows, :] * cdec_ref[hd] + _dot(k_dec, v_h)

    o_gated = _group_norm_gate(o_ref[...], g)

    ext_ref[POOL_CARRY:POOL_CARRY + block_l, :] = u
    pos = (li * block_l + lax.broadcasted_iota(I32, (block_l, 1), 0)).astype(F32)
    pooled = []
    for gi, w in enumerate(POOL_WINDOWS):
        lanes = slice(gi * POOL_GROUP_DIM, (gi + 1) * POOL_GROUP_DIM)
        acc = ext_ref[POOL_CARRY:POOL_CARRY + block_l, lanes]
        for j in range(1, w):
            acc = acc + ext_ref[POOL_CARRY - j:POOL_CARRY - j + block_l, lanes]
        cnt = jnp.minimum(pos + 1.0, float(w))
        pooled.append(acc / cnt - u[:, lanes])
    p = _pool_project(pooled, w_pool_ref, pool_scale_ref)
    ext_ref[0:POOL_CARRY, :] = ext_ref[block_l:block_l + POOL_CARRY, :]

    _mix_tail(x, o_gated, p, mod_ref, w_out_ref, norm2_ref, wr_t_ref, bias_t_ref,
              x1_ref, h2_ref, experts_ref, gatew_ref)

    @pl.when(li == pl.num_programs(1) - 1)
    def _():
        ret_ref[...] = state_ref[...].reshape(ret_ref.shape)
        pool_ref[...] = ext_ref[1:1 + POOL_BUF, :].reshape(pool_ref.shape)


def _decay_tables(chunk):
    lg = jnp.log(1.0 - 2.0 ** (-5.0 - jnp.arange(RET_HEADS, dtype=F32)))
    idx = jnp.arange(chunk, dtype=F32)
    diff = idx[:, None] - idx[None, :]
    causal = diff >= 0
    dmat = jnp.where(causal[None], jnp.exp(lg[:, None, None] * jnp.where(causal, diff, 0.0)[None]), 0.0)
    cross = jnp.exp(lg[:, None] * (idx[None, :] + 1.0))
    cross = jnp.broadcast_to(cross[:, :, None], (RET_HEADS, chunk, RET_V_DIM))
    tail = jnp.exp(lg[:, None] * (chunk - 1.0 - idx)[None, :])
    cdec = jnp.broadcast_to(jnp.exp(lg * chunk)[:, None, None], (RET_HEADS, RET_QK_DIM, RET_V_DIM))
    return dmat, cross, tail, cdec


def _rotary_tables(pos):
    half = RET_QK_DIM // 2
    freqs = ROPE_BASE ** (-jnp.arange(half, dtype=F32) / half)
    ang = pos[:, None] * freqs[None, :]
    cos, sin = jnp.cos(ang), jnp.sin(ang)
    cos_t = jnp.tile(jnp.concatenate([cos, cos], axis=-1), (1, RET_HEADS))
    sin_t = jnp.tile(jnp.concatenate([-sin, sin], axis=-1), (1, RET_HEADS))
    return cos_t, sin_t


def _full(shape):
    return pl.BlockSpec(shape, lambda *_: (0,) * len(shape))


def _mix_prompt(x, mod, norm1, w_in16, w_pool, pool_scale, w_out16, norm2, wr_t, bias_t,
                block_l=512, chunk=256):
    b, l, d = x.shape
    n_tokens = b * l
    nl = l // block_l
    cos_t, sin_t = _rotary_tables(jnp.arange(l, dtype=F32))
    dmat, cross, tail, cdec = _decay_tables(chunk)
    kernel = functools.partial(_mix_prompt_kernel, block_l=block_l, chunk=chunk)
    tok = lambda bi, li: (bi, li, 0)
    flat = lambda bi, li: (bi * nl + li, 0)
    return pl.pallas_call(
        kernel,
        grid=(b, nl),
        in_specs=[pl.BlockSpec((1, block_l, d), tok),
                  pl.BlockSpec((1, 6, d), lambda bi, li: (bi, 0, 0)),
                  _full((1, d)),
                  _full((d, IN_WIDTH)),
                  pl.BlockSpec((block_l, QK_WIDTH), lambda bi, li: (li, 0)),
                  pl.BlockSpec((block_l, QK_WIDTH), lambda bi, li: (li, 0)),
                  _full(dmat.shape), _full(cross.shape), _full(tail.shape), _full(cdec.shape),
                  _full(w_pool.shape), _full((1, POOL_WIDTH)), _full((d, d)), _full((1, d)),
                  _full(wr_t.shape), _full(bias_t.shape)],
        out_specs=[pl.BlockSpec((1, block_l, d), tok),
                   pl.BlockSpec((block_l, HALF), flat),
                   pl.BlockSpec((TOP_K, block_l), lambda bi, li: (0, bi * nl + li)),
                   pl.BlockSpec((block_l, LANES), flat),
                   pl.BlockSpec((1, RET_HEADS, RET_QK_DIM, RET_V_DIM), lambda bi, li: (bi, 0, 0, 0)),
                   pl.BlockSpec((1, POOL_BUF, POOL_WIDTH), lambda bi, li: (bi, 0, 0))],
        out_shape=[jax.ShapeDtypeStruct((b, l, d), F32),
                   jax.ShapeDtypeStruct((n_tokens, HALF), U32),
                   jax.ShapeDtypeStruct((TOP_K, n_tokens), I32),
                   jax.ShapeDtypeStruct((b * l, LANES), F32),
                   jax.ShapeDtypeStruct((b, RET_HEADS, RET_QK_DIM, RET_V_DIM), F32),
                   jax.ShapeDtypeStruct((b, POOL_BUF, POOL_WIDTH), F32)],
        scratch_shapes=[pltpu.VMEM((QK_WIDTH, RET_V_DIM), F32),
                        pltpu.VMEM((POOL_CARRY + block_l, POOL_WIDTH), F32),
                        pltpu.VMEM((block_l, RET_WIDTH), F32)],
        compiler_params=pltpu.CompilerParams(dimension_semantics=("arbitrary", "arbitrary"),
                                             vmem_limit_bytes=VMEM_LIMIT),
        name="mix_prompt",
    )(x, mod, norm1, w_in16, cos_t, sin_t, dmat, cross, tail, cdec, w_pool, pool_scale,
      w_out16, norm2, wr_t, bias_t)


def _mix_sample_front_kernel(x_ref, mod_ref, norm1_ref, w_in_ref, cos_ref, sin_ref,
                             qt_ref, kt_ref, v_ref, g_ref, u_ref):
    x = x_ref[...]
    h = _rms(x) * norm1_ref[...] * (1.0 + _mod(mod_ref, 1)) + _mod(mod_ref, 0)
    proj = _dot(h.astype(BF16), w_in_ref[...])
    half = RET_QK_DIM // 2
    cos_c = cos_ref[...]
    sin_c = sin_ref[...]

    def rot_t(t):
        parts = []
        for hd in range(RET_HEADS):
            t1 = t[hd * RET_QK_DIM:hd * RET_QK_DIM + half]
            t2 = t[hd * RET_QK_DIM + half:(hd + 1) * RET_QK_DIM]
            parts += [t1 * cos_c - t2 * sin_c, t1 * sin_c + t2 * cos_c]
        return jnp.concatenate(parts, axis=0)

    qt_ref[...] = rot_t(proj[:, 0:QK_WIDTH].T)
    kt_ref[...] = rot_t(proj[:, QK_WIDTH:2 * QK_WIDTH].T) * (RET_QK_DIM ** -0.5)
    v_ref[...] = proj[:, 2 * QK_WIDTH:2 * QK_WIDTH + RET_WIDTH]
    g_ref[...] = proj[:, 2 * QK_WIDTH + RET_WIDTH:2 * QK_WIDTH + 2 * RET_WIDTH]
    u_ref[...] = proj[:, 2 * QK_WIDTH + 2 * RET_WIDTH:]


def _ret_step_kernel(qt_ref, kt_ref, v_ref, s0_ref, o_ref, s1_ref, *, block_b, decays):
    i = pl.program_id(0)
    lane = lax.broadcasted_iota(I32, qt_ref.shape, 1)
    for j in range(block_b):
        bi = i * block_b + j
        here = lane == bi
        q_col = jnp.sum(jnp.where(here, qt_ref[...], 0.0), axis=1, keepdims=True)
        k_col = jnp.sum(jnp.where(here, kt_ref[...], 0.0), axis=1, keepdims=True)
        v_row = v_ref[pl.ds(bi, 1), :]
        outs = []
        for hd in range(RET_HEADS):
            hrows = slice(hd * RET_QK_DIM, (hd + 1) * RET_QK_DIM)
            s1 = decays[hd] * s0_ref[j, hd] + k_col[hrows] * v_row[:, hd * RET_V_DIM:(hd + 1) * RET_V_DIM]
            s1_ref[j, hd] = s1
            outs.append(jnp.sum(q_col[hrows] * s1, axis=0, keepdims=True))
        o_ref[pl.ds(bi, 1), :] = jnp.concatenate(outs, axis=-1)


def _mix_sample_back_kernel(x_ref, mod_ref, o_ref, g_ref, u_ref, buf_ref, w_pool_ref, pool_scale_ref,
                            w_out_ref, norm2_ref, wr_t_ref, bias_t_ref,
                            x1_ref, h2_ref, experts_ref, gatew_ref, pool_ref):
    o_gated = _group_norm_gate(o_ref[...], g_ref[...])
    u = u_ref[...]
    pooled = []
    for gi, w in enumerate(POOL_WINDOWS):
        lanes = slice(gi * POOL_GROUP_DIM, (gi + 1) * POOL_GROUP_DIM)
        acc = u[:, lanes]
        for j in range(1, w):
            acc = acc + buf_ref[:, POOL_BUF - j, lanes]
        pooled.append(acc / float(w) - u[:, lanes])
    p = _pool_project(pooled, w_pool_ref, pool_scale_ref)
    pool_ref[:, 0:POOL_BUF - 1, :] = buf_ref[:, 1:POOL_BUF, :]
    pool_ref[:, POOL_BUF - 1, :] = u
    _mix_tail(x_ref[...], o_gated, p, mod_ref, w_out_ref, norm2_ref, wr_t_ref, bias_t_ref,
              x1_ref, h2_ref, experts_ref, gatew_ref)


def _mix_sample(x, mod, state_ret, state_pool, start, norm1, w_in16, w_pool,
                pool_scale, w_out16, norm2, wr_t, bias_t, block_b=8):
    n, d = x.shape
    half = RET_QK_DIM // 2
    freqs = ROPE_BASE ** (-jnp.arange(half, dtype=F32) / half)
    ang = jnp.full((1,), start, F32)[:, None] * freqs[None, :]
    cos_c = jnp.broadcast_to(jnp.cos(ang).T, (half, n))
    sin_c = jnp.broadcast_to(jnp.sin(ang).T, (half, n))
    params = pltpu.CompilerParams(vmem_limit_bytes=VMEM_LIMIT)
    qt, kt, v, g, u = pl.pallas_call(
        _mix_sample_front_kernel,
        out_shape=[jax.ShapeDtypeStruct((QK_WIDTH, n), F32), jax.ShapeDtypeStruct((QK_WIDTH, n), F32),
                   jax.ShapeDtypeStruct((n, RET_WIDTH), F32), jax.ShapeDtypeStruct((n, RET_WIDTH), F32),
                   jax.ShapeDtypeStruct((n, POOL_WIDTH), F32)],
        compiler_params=params,
        name="mix_sample_front",
    )(x, mod, norm1, w_in16, cos_c, sin_c)

    lg = np.log(1.0 - 2.0 ** (-5.0 - np.arange(RET_HEADS, dtype=np.float32)), dtype=np.float32)
    decays = tuple(float(np.exp(lg[h])) for h in range(RET_HEADS))
    state_block = (block_b, RET_HEADS, RET_QK_DIM, RET_V_DIM)
    o, s1 = pl.pallas_call(
        functools.partial(_ret_step_kernel, block_b=block_b, decays=decays),
        grid=(n // block_b,),
        in_specs=[_full((QK_WIDTH, n)), _full((QK_WIDTH, n)), _full((n, RET_WIDTH)),
                  pl.BlockSpec(state_block, lambda i: (i, 0, 0, 0))],
        out_specs=[_full((n, RET_WIDTH)), pl.BlockSpec(state_block, lambda i: (i, 0, 0, 0))],
        out_shape=[jax.ShapeDtypeStruct((n, RET_WIDTH), F32),
                   jax.ShapeDtypeStruct(state_ret.shape, F32)],
        compiler_params=pltpu.CompilerParams(dimension_semantics=("arbitrary",),
                                             vmem_limit_bytes=VMEM_LIMIT),
        name="ret_step",
    )(qt, kt, v, state_ret)

    x1, h2, experts, gate_w, pool = pl.pallas_call(
        _mix_sample_back_kernel,
        out_shape=[jax.ShapeDtypeStruct((n, d), F32),
                   jax.ShapeDtypeStruct((n, HALF), U32),
                   jax.ShapeDtypeStruct((TOP_K, n), I32),
                   jax.ShapeDtypeStruct((n, LANES), F32),
                   jax.ShapeDtypeStruct(state_pool.shape, F32)],
        compiler_params=params,
        name="mix_sample_back",
    )(x, mod, o, g, u, state_pool, w_pool, pool_scale, w_out16, norm2, wr_t, bias_t)
    return x1, h2, experts, gate_w, s1, pool


def _plan_kernel(experts_ref, pos_ref, meta_ref, cnt_ref, carry_ref, off_ref, *, block_t):
    phase = pl.program_id(0)
    j = pl.program_id(1)
    e_blk = experts_ref[...]
    eidx = lax.broadcasted_iota(I32, (N_EXPERTS, block_t), 0)
    member = jnp.zeros((N_EXPERTS, block_t), F32)
    for s in range(TOP_K):
        member = member + jnp.where(eidx == e_blk[s:s + 1, :], 1.0, 0.0)
    per_expert = jnp.broadcast_to(jnp.sum(member, axis=1, keepdims=True), (N_EXPERTS, LANES))

    @pl.when((phase == 0) & (j == 0))
    def _():
        cnt_ref[...] = jnp.zeros_like(cnt_ref)

    @pl.when(phase == 0)
    def _():
        cnt_ref[...] += per_expert

    @pl.when((phase == 0) & (j == pl.num_programs(1) - 1))
    def _():
        cnt = cnt_ref[...]
        n_tile = jnp.floor((cnt + (ROW_TILE - 1.0)) * (1.0 / ROW_TILE))
        upto = (lax.broadcasted_iota(I32, (N_EXPERTS, N_EXPERTS), 1)
                <= lax.broadcasted_iota(I32, (N_EXPERTS, N_EXPERTS), 0))
        tile_end = _dot(jnp.where(upto, 1.0, 0.0).astype(BF16), n_tile.astype(BF16))
        tile_start = tile_end - n_tile
        off_ref[...] = tile_start * ROW_TILE
        carry_ref[...] = jnp.zeros_like(carry_ref)
        lane = lax.broadcasted_iota(I32, cnt.shape, 1)
        meta_ref[...] = jnp.where(lane == 0, tile_start, jnp.where(lane == 1, n_tile, cnt)).astype(I32)

    @pl.when(phase == 1)
    def _():
        before = (lax.broadcasted_iota(I32, (block_t, block_t), 0)
                  < lax.broadcasted_iota(I32, (block_t, block_t), 1))
        rank = _dot(member.astype(BF16), jnp.where(before, 1.0, 0.0).astype(BF16))
        row = off_ref[:, 0:1] + carry_ref[:, 0:1] + rank
        carry_ref[...] += per_expert
        out = [jnp.sum(jnp.where(eidx == e_blk[s:s + 1, :], row, 0.0), axis=0, keepdims=True)
               for s in range(TOP_K)]
        pos_ref[...] = jnp.concatenate(out, axis=0).astype(I32)


def _plan(experts_all, block_t=384):
    n_tokens = experts_all.shape[1]
    nb = n_tokens // block_t
    return pl.pallas_call(
        functools.partial(_plan_kernel, block_t=block_t),
        grid=(2, nb),
        in_specs=[pl.BlockSpec((TOP_K, block_t), lambda ph, j: (0, j))],
        out_specs=[pl.BlockSpec((TOP_K, block_t), lambda ph, j: (0, j * ph)),
                   _full((N_EXPERTS, LANES))],
        out_shape=[jax.ShapeDtypeStruct((TOP_K, n_tokens), I32),
                   jax.ShapeDtypeStruct((N_EXPERTS, LANES), I32)],
        scratch_shapes=[pltpu.VMEM((N_EXPERTS, LANES), F32)] * 3,
        compiler_params=pltpu.CompilerParams(dimension_semantics=("arbitrary", "arbitrary"),
                                             vmem_limit_bytes=VMEM_LIMIT),
        name="plan",
    )(experts_all)


def _sc_workers():
    info = plsc.get_sparse_core_info()
    return info.num_cores, info.num_cores * info.num_subcores


def _sc_scatter_rows(rows_a, rows_b, pos_t, n_out):
    ta, w = rows_a.shape
    s = pos_t.shape[0]
    n_cores, n_workers = _sc_workers()
    chunks_a = ta // SC_CHUNK
    n_chunks = pos_t.shape[1] // SC_CHUNK
    iters = -(-n_chunks // n_workers)
    mesh = plsc.VectorSubcoreMesh(core_axis_name="c", subcore_axis_name="s")

    @functools.partial(
        pl.kernel, mesh=mesh, out_type=jax.ShapeDtypeStruct((n_out, w), rows_a.dtype),
        scratch_types=[pltpu.VMEM((SC_CHUNK, w), rows_a.dtype), pltpu.VMEM((s, SC_CHUNK), I32),
                       pltpu.SemaphoreType.DMA],
        name="dispatch")
    def k(a_hbm, b_hbm, pos_hbm, out_hbm, rows_v, idx_v, sem):
        wid = lax.axis_index("s") * n_cores + lax.axis_index("c")

        @pl.loop(0, iters)
        def _(it):
            c = it * n_workers + wid

            @pl.when(c < chunks_a)
            def _():
                pltpu.sync_copy(a_hbm.at[pl.ds(pl.multiple_of(c * SC_CHUNK, SC_CHUNK), SC_CHUNK)], rows_v)

            @pl.when((c >= chunks_a) & (c < n_chunks))
            def _():
                base_b = pl.multiple_of((c - chunks_a) * SC_CHUNK, SC_CHUNK)
                pltpu.sync_copy(b_hbm.at[pl.ds(base_b, SC_CHUNK)], rows_v)

            @pl.when(c < n_chunks)
            def _():
                base = pl.multiple_of(c * SC_CHUNK, SC_CHUNK)
                pltpu.sync_copy(pos_hbm.at[:, pl.ds(base, SC_CHUNK)], idx_v)
                copies = [pltpu.async_copy(rows_v, out_hbm.at[idx_v.at[j]], sem) for j in range(s)]
                for cp in copies:
                    cp.wait()

    return k(rows_a, rows_b, pos_t)


def _sc_gather_rows(table, pos_t):
    _, w = table.shape
    s, t = pos_t.shape
    n_cores, n_workers = _sc_workers()
    n_chunks = t // SC_CHUNK
    iters = -(-n_chunks // n_workers)
    mesh = plsc.VectorSubcoreMesh(core_axis_name="c", subcore_axis_name="s")

    @functools.partial(
        pl.kernel, mesh=mesh, out_type=jax.ShapeDtypeStruct((s, t, w), table.dtype),
        scratch_types=[pltpu.VMEM((SC_CHUNK, w), table.dtype), pltpu.VMEM((s, SC_CHUNK), I32),
                       pltpu.SemaphoreType.DMA],
        name="combine")
    def k(table_hbm, pos_hbm, out_hbm, rows_v, idx_v, sem):
        wid = lax.axis_index("s") * n_cores + lax.axis_index("c")

        @pl.loop(0, iters)
        def _(it):
            c = it * n_workers + wid

            @pl.when(c < n_chunks)
            def _():
                base = pl.multiple_of(c * SC_CHUNK, SC_CHUNK)
                pltpu.sync_copy(pos_hbm.at[:, pl.ds(base, SC_CHUNK)], idx_v)
                for j in range(s):
                    pltpu.async_copy(table_hbm.at[idx_v.at[j]], rows_v, sem).wait()
                    pltpu.sync_copy(rows_v, out_hbm.at[j, pl.ds(base, SC_CHUNK)])

    return k(table, pos_t)


def _experts_kernel(first_ref, ntile_ref, cnt_ref, xs_hbm, wg_ref, wu_ref, wd_ref, ys_hbm,
                    wg16_ref, wu16_ref, wd16_ref, x_buf, y_buf, in_sem, out_sem):
    e = pl.program_id(0)
    n_used = first_ref[N_EXPERTS - 1] + ntile_ref[N_EXPERTS - 1]
    first, n_mine, count = first_ref[e], ntile_ref[e], cnt_ref[e]

    def tile_rows(g):
        return pl.ds(pl.multiple_of(g * ROW_TILE, ROW_TILE), ROW_TILE)

    def load(g):
        slot = lax.rem(g, STREAM_DEPTH)
        return pltpu.make_async_copy(xs_hbm.at[tile_rows(g)], x_buf.at[slot], in_sem.at[slot])

    def store(g):
        slot = lax.rem(g, STREAM_DEPTH)
        return pltpu.make_async_copy(y_buf.at[slot], ys_hbm.at[tile_rows(g)], out_sem.at[slot])

    @pl.when(e == 0)
    def _():
        for g0 in range(STREAM_DEPTH - 1):
            @pl.when(g0 < n_used)
            def _():
                load(g0).start()

    wg16_ref[...] = wg_ref[0].astype(BF16)
    wu16_ref[...] = wu_ref[0].astype(BF16)
    wd16_ref[...] = wd_ref[0].astype(BF16)

    def tile(j, carry):
        g = first + j
        slot = lax.rem(g, STREAM_DEPTH)
        load(g).wait()

        @pl.when(g + (STREAM_DEPTH - 1) < n_used)
        def _():
            load(g + (STREAM_DEPTH - 1)).start()

        @pl.when(g >= STREAM_DEPTH)
        def _():
            store(g - STREAM_DEPTH).wait()

        words = x_buf[slot]
        row = lax.broadcasted_iota(I32, words.shape, 0)
        words = jnp.where(row < count - j * ROW_TILE, words, jnp.uint32(0))
        lo, hi = _unpack_rows(words)
        lo, hi = lo.astype(BF16), hi.astype(BF16)
        hg = _dot(lo, wg16_ref[0:HALF, :]) + _dot(hi, wg16_ref[HALF:, :])
        hu = _dot(lo, wu16_ref[0:HALF, :]) + _dot(hi, wu16_ref[HALF:, :])
        a = (_silu(hg) * hu).astype(BF16)
        y_buf[slot] = _pack_rows(_dot(a, wd16_ref[...]))
        store(g).start()
        return carry

    lax.fori_loop(0, n_mine, tile, 0)

    @pl.when(e == N_EXPERTS - 1)
    def _():
        for back in range(STREAM_DEPTH, 0, -1):
            @pl.when(n_used >= back)
            def _():
                store(n_used - back).wait()


def _experts(xs, first_tile, n_tile, count, w_eg, w_eu, w_ed):
    d = D_MODEL
    by_expert = lambda e, *_: (e, 0, 0)
    grid_spec = pltpu.PrefetchScalarGridSpec(
        num_scalar_prefetch=3,
        grid=(N_EXPERTS,),
        in_specs=[pl.BlockSpec(memory_space=pl.ANY),
                  pl.BlockSpec((1, d, EXPERT_DIM), by_expert),
                  pl.BlockSpec((1, d, EXPERT_DIM), by_expert),
                  pl.BlockSpec((1, EXPERT_DIM, d), by_expert)],
        out_specs=pl.BlockSpec(memory_space=pl.ANY),
        scratch_shapes=[pltpu.VMEM((d, EXPERT_DIM), BF16), pltpu.VMEM((d, EXPERT_DIM), BF16),
                        pltpu.VMEM((EXPERT_DIM, d), BF16),
                        pltpu.VMEM((STREAM_DEPTH, ROW_TILE, HALF), U32),
                        pltpu.VMEM((STREAM_DEPTH, ROW_TILE, HALF), U32),
                        pltpu.SemaphoreType.DMA((STREAM_DEPTH,)), pltpu.SemaphoreType.DMA((STREAM_DEPTH,))])
    return pl.pallas_call(
        _experts_kernel,
        grid_spec=grid_spec,
        out_shape=jax.ShapeDtypeStruct(xs.shape, U32),
        compiler_params=pltpu.CompilerParams(dimension_semantics=("arbitrary",),
                                             vmem_limit_bytes=VMEM_LIMIT),
        name="experts",
    )(first_tile, n_tile, count, xs, w_eg, w_eu, w_ed)


def _final_kernel(z_ref, gatew_ref, h2_ref, x1_ref, mod_ref, normf_ref, wsg_ref, wsu_ref, wsd_ref, y_ref):
    lo, hi = _unpack_rows(h2_ref[...])
    h = jnp.concatenate([lo, hi], axis=-1).astype(BF16)
    a = _silu(_dot(h, wsg_ref[...])) * _dot(h, wsu_ref[...])
    acc = _dot(a.astype(BF16), wsd_ref[...])
    for s in range(TOP_K):
        lo, hi = _unpack_rows(z_ref[s])
        acc = acc + gatew_ref[:, s:s + 1] * jnp.concatenate([lo, hi], axis=-1)
    x2 = x1_ref[...] + _mod(mod_ref, 5) * acc
    y_ref[...] = _rms(x2) * normf_ref[...]


def _final(z, gate_w, h2, x1, mod, norm_f, w_sg16, w_su16, w_sd16, block_t, first_block, per_seq):
    t, d = x1.shape
    tok = lambda i: (i, 0)
    if per_seq:
        mod_spec = pl.BlockSpec((1, 6, d), lambda i: (i // per_seq, 0, 0))
    else:
        mod_spec = pl.BlockSpec((block_t, 6 * d), tok)
    return pl.pallas_call(
        _final_kernel,
        grid=(t // block_t,),
        in_specs=[pl.BlockSpec((TOP_K, block_t, HALF), lambda i: (0, first_block + i, 0)),
                  pl.BlockSpec((block_t, LANES), tok),
                  pl.BlockSpec((block_t, HALF), tok),
                  pl.BlockSpec((block_t, d), tok),
                  mod_spec,
                  _full((1, d)),
                  _full((d, EXPERT_DIM)), _full((d, EXPERT_DIM)), _full((EXPERT_DIM, d))],
        out_specs=pl.BlockSpec((block_t, d), tok),
        out_shape=jax.ShapeDtypeStruct((t, d), F32),
        compiler_params=pltpu.CompilerParams(dimension_semantics=("arbitrary",),
                                             vmem_limit_bytes=VMEM_LIMIT),
        name="final",
    )(z, gate_w, h2, x1, mod, norm_f, w_sg16, w_su16, w_sd16)


def kernel(x_prompt, x_sample, c_prompt, c_sample, state_ret, state_pool, norm1, norm2, norm_f,
           w_ada, b_ada, w_in, w_out, w_pool, pool_scale, w_router, router_bias, w_exp_gate,
           w_exp_up, w_exp_down, w_sh_gate, w_sh_up, w_sh_down):
    b, l, d = x_prompt.shape
    n = x_sample.shape[0]
    n_tokens = b * l + n
    past_len = 16384

    mod = _ada(jnp.concatenate([c_prompt, c_sample], axis=0), w_ada[0], b_ada[0])
    mod_p = mod[:b].reshape(b, 6, d)
    mod_s = mod[b:]

    w_in16 = w_in[0].astype(BF16)
    w_out16 = w_out[0].astype(BF16)
    wr_t = w_router[0].T
    bias_t = jnp.broadcast_to(router_bias[0][:, None], (N_EXPERTS, LANES))
    n1, n2, nf = norm1[0].reshape(1, d), norm2[0].reshape(1, d), norm_f.reshape(1, d)
    ps = pool_scale[0].reshape(1, POOL_WIDTH)
    shared = (w_sh_gate[0].astype(BF16), w_sh_up[0].astype(BF16), w_sh_down[0].astype(BF16))

    x1_p, h2_p, experts_p, gatew_p, ret_p, pool_p = _mix_prompt(
        x_prompt, mod_p, n1, w_in16, w_pool[0], ps, w_out16, n2, wr_t, bias_t)
    x1_s, h2_s, experts_s, gatew_s, ret_s, pool_s = _mix_sample(
        x_sample.reshape(n, d), mod_s, state_ret[0], state_pool[0], float(past_len),
        n1, w_in16, w_pool[0], ps, w_out16, n2, wr_t, bias_t)

    n_tiles = n_tokens * TOP_K // ROW_TILE + N_EXPERTS
    pos_t, meta = _plan(jnp.concatenate([experts_p, experts_s], axis=1))
    xs = _sc_scatter_rows(h2_p, h2_s, pos_t, n_tiles * ROW_TILE)
    ys = _experts(xs, meta[:, 0], meta[:, 1], meta[:, 2], w_exp_gate[0], w_exp_up[0], w_exp_down[0])
    z = _sc_gather_rows(ys, pos_t)

    block_t = 256
    y_p = _final(z, gatew_p, h2_p, x1_p.reshape(b * l, d), mod_p, nf, *shared,
                 block_t=block_t, first_block=0, per_seq=l // block_t)
    y_s = _final(z, gatew_s, h2_s, x1_s, mod_s, nf, *shared,
                 block_t=n, first_block=b * l // n, per_seq=0)

    return (y_p.reshape(b, l, d), y_s.reshape(n, 1, d), ret_p[None], pool_p[None],
            ret_s[None], pool_s[None])
```

```python
import functools

import jax
import jax.numpy as jnp
import numpy as np
from jax import lax
from jax.experimental import pallas as pl
from jax.experimental.pallas import tpu as pltpu
from jax.experimental.pallas import tpu_sc as plsc

D_MODEL = 1024
RET_HEADS = 4
RET_QK_DIM = 64
RET_V_DIM = 128
RET_WIDTH = RET_HEADS * RET_V_DIM
QK_WIDTH = RET_HEADS * RET_QK_DIM
ROPE_BASE = 10000.0
POOL_WINDOWS = (2, 4, 8, 16)
POOL_WIDTH = 512
POOL_GROUP_DIM = 128
POOL_BUF = 15
IN_WIDTH = 2 * QK_WIDTH + 2 * RET_WIDTH + POOL_WIDTH
N_EXPERTS = 64
TOP_K = 8
N_EXPERT_GROUPS = 8
GROUP_SIZE = N_EXPERTS // N_EXPERT_GROUPS
TOP_GROUPS = 4
EXPERT_DIM = 256
ROUTE_SCALE = 2.5
EPS = 1e-6

LANES = 128
POOL_CARRY = 16
VMEM_LIMIT = 56 * 1024 * 1024
HALF = D_MODEL // 2
ROW_TILE = 256
SC_CHUNK = 128
STREAM_DEPTH = 4

BF16 = jnp.bfloat16
F32 = jnp.float32
U32 = jnp.uint32
I32 = jnp.int32


def _silu(x):
    return x * jax.nn.sigmoid(x)


def _dot(a, b):
    return jnp.dot(a, b, preferred_element_type=F32)


def _rms(x):
    return x * lax.rsqrt(jnp.mean(x * x, axis=-1, keepdims=True) + EPS)


def _mod(mod_ref, i):
    if len(mod_ref.shape) == 3:
        return mod_ref[0, i:i + 1, :]
    return mod_ref[:, i * D_MODEL:(i + 1) * D_MODEL]


def _split_bf16(x):
    hi = x.astype(BF16)
    lo = (x - hi.astype(F32)).astype(BF16)
    return hi, lo


def _pack_rows(x):
    lo = lax.bitcast_convert_type(x[:, :HALF].astype(BF16).astype(F32), U32)
    hi = lax.bitcast_convert_type(x[:, HALF:].astype(BF16).astype(F32), U32)
    return (hi & jnp.uint32(0xFFFF0000)) | (lo >> jnp.uint32(16))


def _unpack_rows(w):
    lo = lax.bitcast_convert_type(w << jnp.uint32(16), F32)
    hi = lax.bitcast_convert_type(w & jnp.uint32(0xFFFF0000), F32)
    return lo, hi


def _first_max_onehot(work, idx, n):
    m = jnp.max(work, axis=0, keepdims=True)
    first = jnp.min(jnp.where(work == m, idx, float(n)), axis=0, keepdims=True)
    return idx == first


def _route(h2, wr_t_ref, bias_t_ref):
    n = h2.shape[0]
    h_hi, h_lo = _split_bf16(h2)
    w_hi, w_lo = _split_bf16(wr_t_ref[...])
    nt = (((1,), (1,)), ((), ()))
    logits = (lax.dot_general(w_hi, h_hi, nt, preferred_element_type=F32)
              + lax.dot_general(w_hi, h_lo, nt, preferred_element_type=F32)
              + lax.dot_general(w_lo, h_hi, nt, preferred_element_type=F32))
    scores = jax.nn.sigmoid(logits)
    biased = scores + bias_t_ref[:, 0:1]
    b3 = biased.reshape(N_EXPERT_GROUPS, GROUP_SIZE, n)
    i3 = lax.broadcasted_iota(I32, b3.shape, 1).astype(F32)
    m1 = jnp.max(b3, axis=1, keepdims=True)
    first = jnp.min(jnp.where(b3 == m1, i3, float(GROUP_SIZE)), axis=1, keepdims=True)
    m2 = jnp.max(jnp.where(i3 == first, -jnp.inf, b3), axis=1, keepdims=True)
    gscore = (m1 + m2).reshape(N_EXPERT_GROUPS, n)
    gidx = lax.broadcasted_iota(I32, gscore.shape, 0).astype(F32)
    gsel = jnp.zeros(gscore.shape, F32)
    work = gscore
    for _ in range(TOP_GROUPS):
        hit = _first_max_onehot(work, gidx, N_EXPERT_GROUPS)
        gsel = jnp.where(hit, 1.0, gsel)
        work = jnp.where(hit, -jnp.inf, work)
    gsel3 = jnp.broadcast_to(gsel.reshape(N_EXPERT_GROUPS, 1, n), b3.shape)
    work = jnp.where(gsel3 > 0.0, b3, -jnp.inf).reshape(N_EXPERTS, n)
    eidx = lax.broadcasted_iota(I32, work.shape, 0).astype(F32)
    sel = jnp.zeros(work.shape, F32)
    for _ in range(TOP_K):
        hit = _first_max_onehot(work, eidx, N_EXPERTS)
        sel = jnp.where(hit, 1.0, sel)
        work = jnp.where(hit, -jnp.inf, work)
    picked = jnp.where(sel > 0.0, scores, 0.0)
    gates = picked / jnp.sum(picked, axis=0, keepdims=True) * ROUTE_SCALE
    below = (lax.broadcasted_iota(I32, (N_EXPERTS, N_EXPERTS), 1)
             < lax.broadcasted_iota(I32, (N_EXPERTS, N_EXPERTS), 0))
    slot = _dot(jnp.where(below, 1.0, 0.0).astype(BF16), sel.astype(BF16))
    e_rows, w_rows = [], []
    for s in range(TOP_K):
        here = jnp.where(slot == float(s), sel, 0.0)
        e_rows.append(jnp.sum(here * eidx, axis=0, keepdims=True))
        w_rows.append(jnp.sum(here * gates, axis=0, keepdims=True))
    experts = jnp.concatenate(e_rows, axis=0).astype(I32)
    w_t = jnp.concatenate(w_rows + [jnp.zeros((LANES - TOP_K, n), F32)], axis=0)
    return experts, w_t.T


def _group_norm_gate(o, g):
    parts = []
    for h in range(RET_HEADS):
        oh = o[:, h * RET_V_DIM:(h + 1) * RET_V_DIM]
        mu = jnp.mean(oh, axis=-1, keepdims=True)
        ctr = oh - mu
        var = jnp.mean(ctr * ctr, axis=-1, keepdims=True)
        parts.append(ctr * lax.rsqrt(var + EPS))
    return _silu(g) * jnp.concatenate(parts, axis=-1)


def _pool_project(pooled, w_pool_ref, pool_scale_ref):
    parts = [_dot(p.astype(BF16), w_pool_ref[gi].astype(BF16)) for gi, p in enumerate(pooled)]
    return jnp.concatenate(parts, axis=-1) * pool_scale_ref[...]


def _mix_tail(x, o_gated, p, mod_ref, w_out_ref, norm2_ref, wr_t_ref, bias_t_ref,
              x1_ref, h2_ref, experts_ref, gatew_ref):
    mix = jnp.concatenate([o_gated, p], axis=-1).astype(BF16)
    y = _dot(mix, w_out_ref[...])
    x1 = x + _mod(mod_ref, 2) * y
    h2 = _rms(x1) * norm2_ref[...] * (1.0 + _mod(mod_ref, 4)) + _mod(mod_ref, 3)
    x1_ref[...] = x1.reshape(x1_ref.shape)
    h2_ref[...] = _pack_rows(h2)
    experts, gate_w = _route(h2, wr_t_ref, bias_t_ref)
    experts_ref[...] = experts
    gatew_ref[...] = gate_w


def _ada_kernel(c_ref, w_ref, b_ref, o_ref):
    cs = _silu(c_ref[...]).astype(BF16)
    o_ref[...] = _dot(cs, w_ref[...].astype(BF16)) + b_ref[...]


def _ada(c_all, w_ada, b_ada, block_n=1536):
    n, d = c_all.shape
    width = w_ada.shape[1]
    return pl.pallas_call(
        _ada_kernel,
        grid=(width // block_n,),
        in_specs=[pl.BlockSpec((n, d), lambda j: (0, 0)),
                  pl.BlockSpec((d, block_n), lambda j: (0, j)),
                  pl.BlockSpec((1, block_n), lambda j: (0, j))],
        out_specs=pl.BlockSpec((n, block_n), lambda j: (0, j)),
        out_shape=jax.ShapeDtypeStruct((n, width), F32),
        compiler_params=pltpu.CompilerParams(vmem_limit_bytes=VMEM_LIMIT),
        name="ada",
    )(c_all, w_ada, b_ada.reshape(1, width))


def _mix_prompt_kernel(x_ref, mod_ref, norm1_ref, w_in_ref, cos_ref, sin_ref, dmat_ref, cross_ref,
                       tail_ref, cdec_ref, w_pool_ref, pool_scale_ref, w_out_ref, norm2_ref,
                       wr_t_ref, bias_t_ref,
                       x1_ref, h2_ref, experts_ref, gatew_ref, ret_ref, pool_ref,
                       state_ref, ext_ref, o_ref, *, block_l, chunk):
    li = pl.program_id(1)

    @pl.when(li == 0)
    def _():
        state_ref[...] = jnp.zeros_like(state_ref)
        ext_ref[0:POOL_CARRY, :] = jnp.zeros((POOL_CARRY, POOL_WIDTH), F32)

    x = x_ref[0]
    h = _rms(x) * norm1_ref[...] * (1.0 + _mod(mod_ref, 1)) + _mod(mod_ref, 0)
    proj = _dot(h.astype(BF16), w_in_ref[...])
    q = proj[:, 0:QK_WIDTH]
    k = proj[:, QK_WIDTH:2 * QK_WIDTH]
    v = proj[:, 2 * QK_WIDTH:2 * QK_WIDTH + RET_WIDTH]
    g = proj[:, 2 * QK_WIDTH + RET_WIDTH:2 * QK_WIDTH + 2 * RET_WIDTH]
    u = proj[:, 2 * QK_WIDTH + 2 * RET_WIDTH:]

    lane = lax.broadcasted_iota(I32, q.shape, 1)
    first_half = (lane % RET_QK_DIM) < (RET_QK_DIM // 2)
    cos_t = cos_ref[...]
    sin_t = sin_ref[...]

    def rot(t):
        partner = jnp.where(first_half, pltpu.roll(t, QK_WIDTH - RET_QK_DIM // 2, axis=1),
                            pltpu.roll(t, RET_QK_DIM // 2, axis=1))
        return t * cos_t + partner * sin_t

    q = rot(q)
    k = rot(k) * (RET_QK_DIM ** -0.5)
    k_t = k.T
    v16 = v.astype(BF16)
    head_of_lane = lax.broadcasted_iota(I32, (chunk, QK_WIDTH), 1) // RET_QK_DIM

    for c in range(block_l // chunk):
        rows = slice(c * chunk, (c + 1) * chunk)
        q_c = q[rows]
        kt_c = k_t[:, rows]
        kt16 = kt_c.astype(BF16)
        state16 = state_ref[...].astype(BF16)
        for hd in range(RET_HEADS):
            in_head = head_of_lane == hd
            q_h = jnp.where(in_head, q_c, 0.0).astype(BF16)
            v_h = v16[rows, hd * RET_V_DIM:(hd + 1) * RET_V_DIM]
            scores = _dot(q_h, kt16) * dmat_ref[hd]
            inner = _dot(scores.astype(BF16), v_h)
            cross = _dot(q_h, state16) * cross_ref[hd]
            o_ref[rows, hd * RET_V_DIM:(hd + 1) * RET_V_DIM] = inner + cross
            hrows = slice(hd * RET_QK_DIM, (hd + 1) * RET_QK_DIM)
            k_dec = (kt_c[hrows] * tail_ref[hd:hd + 1, :]).astype(BF16)
            state_ref[hrows, :] = state_ref[hrows, :] * cdec_ref[hd] + _dot(k_dec, v_h)

    o_gated = _group_norm_gate(o_ref[...], g)

    ext_ref[POOL_CARRY:POOL_CARRY + block_l, :] = u
    pos = (li * block_l + lax.broadcasted_iota(I32, (block_l, 1), 0)).astype(F32)
    pooled = []
    for gi, w in enumerate(POOL_WINDOWS):
        lanes = slice(gi * POOL_GROUP_DIM, (gi + 1) * POOL_GROUP_DIM)
        acc = ext_ref[POOL_CARRY:POOL_CARRY + block_l, lanes]
        for j in range(1, w):
            acc = acc + ext_ref[POOL_CARRY - j:POOL_CARRY - j + block_l, lanes]
        cnt = jnp.minimum(pos + 1.0, float(w))
        pooled.append(acc / cnt - u[:, lanes])
    p = _pool_project(pooled, w_pool_ref, pool_scale_ref)
    ext_ref[0:POOL_CARRY, :] = ext_ref[block_l:block_l + POOL_CARRY, :]

    _mix_tail(x, o_gated, p, mod_ref, w_out_ref, norm2_ref, wr_t_ref, bias_t_ref,
              x1_ref, h2_ref, experts_ref, gatew_ref)

    @pl.when(li == pl.num_programs(1) - 1)
    def _():
        ret_ref[...] = state_ref[...].reshape(ret_ref.shape)
        pool_ref[...] = ext_ref[1:1 + POOL_BUF, :].reshape(pool_ref.shape)


def _decay_tables(chunk):
    lg = jnp.log(1.0 - 2.0 ** (-5.0 - jnp.arange(RET_HEADS, dtype=F32)))
    idx = jnp.arange(chunk, dtype=F32)
    diff = idx[:, None] - idx[None, :]
    causal = diff >= 0
    dmat = jnp.where(causal[None], jnp.exp(lg[:, None, None] * jnp.where(causal, diff, 0.0)[None]), 0.0)
    cross = jnp.exp(lg[:, None] * (idx[None, :] + 1.0))
    cross = jnp.broadcast_to(cross[:, :, None], (RET_HEADS, chunk, RET_V_DIM))
    tail = jnp.exp(lg[:, None] * (chunk - 1.0 - idx)[None, :])
    cdec = jnp.broadcast_to(jnp.exp(lg * chunk)[:, None, None], (RET_HEADS, RET_QK_DIM, RET_V_DIM))
    return dmat, cross, tail, cdec


def _rotary_tables(pos):
    half = RET_QK_DIM // 2
    freqs = ROPE_BASE ** (-jnp.arange(half, dtype=F32) / half)
    ang = pos[:, None] * freqs[None, :]
    cos, sin = jnp.cos(ang), jnp.sin(ang)
    cos_t = jnp.tile(jnp.concatenate([cos, cos], axis=-1), (1, RET_HEADS))
    sin_t = jnp.tile(jnp.concatenate([-sin, sin], axis=-1), (1, RET_HEADS))
    return cos_t, sin_t


def _full(shape):
    return pl.BlockSpec(shape, lambda *_: (0,) * len(shape))


def _mix_prompt(x, mod, b0, b, norm1, w_in16, w_pool, pool_scale, w_out16, norm2, wr_t, bias_t,
                block_l=512, chunk=256):
    _, l, d = x.shape
    n_tokens = b * l
    nl = l // block_l
    cos_t, sin_t = _rotary_tables(jnp.arange(l, dtype=F32))
    dmat, cross, tail, cdec = _decay_tables(chunk)
    kernel = functools.partial(_mix_prompt_kernel, block_l=block_l, chunk=chunk)
    tok = lambda bi, li: (bi, li, 0)
    flat = lambda bi, li: (bi * nl + li, 0)
    return pl.pallas_call(
        kernel,
        grid=(b, nl),
        in_specs=[pl.BlockSpec((1, block_l, d), lambda bi, li: (b0 + bi, li, 0)),
                  pl.BlockSpec((1, 6, d), lambda bi, li: (b0 + bi, 0, 0)),
                  _full((1, d)),
                  _full((d, IN_WIDTH)),
                  pl.BlockSpec((block_l, QK_WIDTH), lambda bi, li: (li, 0)),
                  pl.BlockSpec((block_l, QK_WIDTH), lambda bi, li: (li, 0)),
                  _full(dmat.shape), _full(cross.shape), _full(tail.shape), _full(cdec.shape),
                  _full(w_pool.shape), _full((1, POOL_WIDTH)), _full((d, d)), _full((1, d)),
                  _full(wr_t.shape), _full(bias_t.shape)],
        out_specs=[pl.BlockSpec((1, block_l, d), tok),
                   pl.BlockSpec((block_l, HALF), flat),
                   pl.BlockSpec((TOP_K, block_l), lambda bi, li: (0, bi * nl + li)),
                   pl.BlockSpec((block_l, LANES), flat),
                   pl.BlockSpec((1, RET_HEADS, RET_QK_DIM, RET_V_DIM), lambda bi, li: (bi, 0, 0, 0)),
                   pl.BlockSpec((1, POOL_BUF, POOL_WIDTH), lambda bi, li: (bi, 0, 0))],
        out_shape=[jax.ShapeDtypeStruct((b, l, d), F32),
                   jax.ShapeDtypeStruct((n_tokens, HALF), U32),
                   jax.ShapeDtypeStruct((TOP_K, n_tokens), I32),
                   jax.ShapeDtypeStruct((b * l, LANES), F32),
                   jax.ShapeDtypeStruct((b, RET_HEADS, RET_QK_DIM, RET_V_DIM), F32),
                   jax.ShapeDtypeStruct((b, POOL_BUF, POOL_WIDTH), F32)],
        scratch_shapes=[pltpu.VMEM((QK_WIDTH, RET_V_DIM), F32),
                        pltpu.VMEM((POOL_CARRY + block_l, POOL_WIDTH), F32),
                        pltpu.VMEM((block_l, RET_WIDTH), F32)],
        compiler_params=pltpu.CompilerParams(dimension_semantics=("arbitrary", "arbitrary"),
                                             vmem_limit_bytes=VMEM_LIMIT),
        name="mix_prompt",
    )(x, mod, norm1, w_in16, cos_t, sin_t, dmat, cross, tail, cdec, w_pool, pool_scale,
      w_out16, norm2, wr_t, bias_t)


def _mix_sample_front_kernel(x_ref, mod_ref, norm1_ref, w_in_ref, cos_ref, sin_ref,
                             qt_ref, kt_ref, v_ref, g_ref, u_ref):
    x = x_ref[...]
    h = _rms(x) * norm1_ref[...] * (1.0 + _mod(mod_ref, 1)) + _mod(mod_ref, 0)
    proj = _dot(h.astype(BF16), w_in_ref[...])
    half = RET_QK_DIM // 2
    cos_c = cos_ref[...]
    sin_c = sin_ref[...]

    def rot_t(t):
        parts = []
        for hd in range(RET_HEADS):
            t1 = t[hd * RET_QK_DIM:hd * RET_QK_DIM + half]
            t2 = t[hd * RET_QK_DIM + half:(hd + 1) * RET_QK_DIM]
            parts += [t1 * cos_c - t2 * sin_c, t1 * sin_c + t2 * cos_c]
        return jnp.concatenate(parts, axis=0)

    qt_ref[...] = rot_t(proj[:, 0:QK_WIDTH].T)
    kt_ref[...] = rot_t(proj[:, QK_WIDTH:2 * QK_WIDTH].T) * (RET_QK_DIM ** -0.5)
    v_ref[...] = proj[:, 2 * QK_WIDTH:2 * QK_WIDTH + RET_WIDTH]
    g_ref[...] = proj[:, 2 * QK_WIDTH + RET_WIDTH:2 * QK_WIDTH + 2 * RET_WIDTH]
    u_ref[...] = proj[:, 2 * QK_WIDTH + 2 * RET_WIDTH:]


def _ret_step_kernel(qt_ref, kt_ref, v_ref, s0_ref, o_ref, s1_ref, *, block_b, decays):
    i = pl.program_id(0)
    lane = lax.broadcasted_iota(I32, qt_ref.shape, 1)
    for j in range(block_b):
        bi = i * block_b + j
        here = lane == bi
        q_col = jnp.sum(jnp.where(here, qt_ref[...], 0.0), axis=1, keepdims=True)
        k_col = jnp.sum(jnp.where(here, kt_ref[...], 0.0), axis=1, keepdims=True)
        v_row = v_ref[pl.ds(bi, 1), :]
        outs = []
        for hd in range(RET_HEADS):
            hrows = slice(hd * RET_QK_DIM, (hd + 1) * RET_QK_DIM)
            s1 = decays[hd] * s0_ref[j, hd] + k_col[hrows] * v_row[:, hd * RET_V_DIM:(hd + 1) * RET_V_DIM]
            s1_ref[j, hd] = s1
            outs.append(jnp.sum(q_col[hrows] * s1, axis=0, keepdims=True))
        o_ref[pl.ds(bi, 1), :] = jnp.concatenate(outs, axis=-1)


def _mix_sample_back_kernel(x_ref, mod_ref, o_ref, g_ref, u_ref, buf_ref, w_pool_ref, pool_scale_ref,
                            w_out_ref, norm2_ref, wr_t_ref, bias_t_ref,
                            x1_ref, h2_ref, experts_ref, gatew_ref, pool_ref):
    o_gated = _group_norm_gate(o_ref[...], g_ref[...])
    u = u_ref[...]
    pooled = []
    for gi, w in enumerate(POOL_WINDOWS):
        lanes = slice(gi * POOL_GROUP_DIM, (gi + 1) * POOL_GROUP_DIM)
        acc = u[:, lanes]
        for j in range(1, w):
            acc = acc + buf_ref[:, POOL_BUF - j, lanes]
        pooled.append(acc / float(w) - u[:, lanes])
    p = _pool_project(pooled, w_pool_ref, pool_scale_ref)
    pool_ref[:, 0:POOL_BUF - 1, :] = buf_ref[:, 1:POOL_BUF, :]
    pool_ref[:, POOL_BUF - 1, :] = u
    _mix_tail(x_ref[...], o_gated, p, mod_ref, w_out_ref, norm2_ref, wr_t_ref, bias_t_ref,
              x1_ref, h2_ref, experts_ref, gatew_ref)


def _mix_sample(x, mod, state_ret, state_pool, start, norm1, w_in16, w_pool,
                pool_scale, w_out16, norm2, wr_t, bias_t, block_b=8):
    n, d = x.shape
    half = RET_QK_DIM // 2
    freqs = ROPE_BASE ** (-jnp.arange(half, dtype=F32) / half)
    ang = jnp.full((1,), start, F32)[:, None] * freqs[None, :]
    cos_c = jnp.broadcast_to(jnp.cos(ang).T, (half, n))
    sin_c = jnp.broadcast_to(jnp.sin(ang).T, (half, n))
    params = pltpu.CompilerParams(vmem_limit_bytes=VMEM_LIMIT)
    qt, kt, v, g, u = pl.pallas_call(
        _mix_sample_front_kernel,
        out_shape=[jax.ShapeDtypeStruct((QK_WIDTH, n), F32), jax.ShapeDtypeStruct((QK_WIDTH, n), F32),
                   jax.ShapeDtypeStruct((n, RET_WIDTH), F32), jax.ShapeDtypeStruct((n, RET_WIDTH), F32),
                   jax.ShapeDtypeStruct((n, POOL_WIDTH), F32)],
        compiler_params=params,
        name="mix_sample_front",
    )(x, mod, norm1, w_in16, cos_c, sin_c)

    lg = np.log(1.0 - 2.0 ** (-5.0 - np.arange(RET_HEADS, dtype=np.float32)), dtype=np.float32)
    decays = tuple(float(np.exp(lg[h])) for h in range(RET_HEADS))
    state_block = (block_b, RET_HEADS, RET_QK_DIM, RET_V_DIM)
    o, s1 = pl.pallas_call(
        functools.partial(_ret_step_kernel, block_b=block_b, decays=decays),
        grid=(n // block_b,),
        in_specs=[_full((QK_WIDTH, n)), _full((QK_WIDTH, n)), _full((n, RET_WIDTH)),
                  pl.BlockSpec(state_block, lambda i: (i, 0, 0, 0))],
        out_specs=[_full((n, RET_WIDTH)), pl.BlockSpec(state_block, lambda i: (i, 0, 0, 0))],
        out_shape=[jax.ShapeDtypeStruct((n, RET_WIDTH), F32),
                   jax.ShapeDtypeStruct(state_ret.shape, F32)],
        compiler_params=pltpu.CompilerParams(dimension_semantics=("arbitrary",),
                                             vmem_limit_bytes=VMEM_LIMIT),
        name="ret_step",
    )(qt, kt, v, state_ret)

    x1, h2, experts, gate_w, pool = pl.pallas_call(
        _mix_sample_back_kernel,
        out_shape=[jax.ShapeDtypeStruct((n, d), F32),
                   jax.ShapeDtypeStruct((n, HALF), U32),
                   jax.ShapeDtypeStruct((TOP_K, n), I32),
                   jax.ShapeDtypeStruct((n, LANES), F32),
                   jax.ShapeDtypeStruct(state_pool.shape, F32)],
        compiler_params=params,
        name="mix_sample_back",
    )(x, mod, o, g, u, state_pool, w_pool, pool_scale, w_out16, norm2, wr_t, bias_t)
    return x1, h2, experts, gate_w, s1, pool


def _plan_kernel(experts_ref, pos_ref, meta_ref, cnt_ref, carry_ref, off_ref, *, block_t):
    phase = pl.program_id(0)
    j = pl.program_id(1)
    e_blk = experts_ref[...]
    eidx = lax.broadcasted_iota(I32, (N_EXPERTS, block_t), 0)
    member = jnp.zeros((N_EXPERTS, block_t), F32)
    for s in range(TOP_K):
        member = member + jnp.where(eidx == e_blk[s:s + 1, :], 1.0, 0.0)
    per_expert = jnp.broadcast_to(jnp.sum(member, axis=1, keepdims=True), (N_EXPERTS, LANES))

    @pl.when((phase == 0) & (j == 0))
    def _():
        cnt_ref[...] = jnp.zeros_like(cnt_ref)

    @pl.when(phase == 0)
    def _():
        cnt_ref[...] += per_expert

    @pl.when((phase == 0) & (j == pl.num_programs(1) - 1))
    def _():
        cnt = cnt_ref[...]
        n_tile = jnp.floor((cnt + (ROW_TILE - 1.0)) * (1.0 / ROW_TILE))
        upto = (lax.broadcasted_iota(I32, (N_EXPERTS, N_EXPERTS), 1)
                <= lax.broadcasted_iota(I32, (N_EXPERTS, N_EXPERTS), 0))
        tile_end = _dot(jnp.where(upto, 1.0, 0.0).astype(BF16), n_tile.astype(BF16))
        tile_start = tile_end - n_tile
        off_ref[...] = tile_start * ROW_TILE
        carry_ref[...] = jnp.zeros_like(carry_ref)
        lane = lax.broadcasted_iota(I32, cnt.shape, 1)
        meta_ref[...] = jnp.where(lane == 0, tile_start, jnp.where(lane == 1, n_tile, cnt)).astype(I32)

    @pl.when(phase == 1)
    def _():
        before = (lax.broadcasted_iota(I32, (block_t, block_t), 0)
                  < lax.broadcasted_iota(I32, (block_t, block_t), 1))
        rank = _dot(member.astype(BF16), jnp.where(before, 1.0, 0.0).astype(BF16))
        row = off_ref[:, 0:1] + carry_ref[:, 0:1] + rank
        carry_ref[...] += per_expert
        out = [jnp.sum(jnp.where(eidx == e_blk[s:s + 1, :], row, 0.0), axis=0, keepdims=True)
               for s in range(TOP_K)]
        pos_ref[...] = jnp.concatenate(out, axis=0).astype(I32)


def _plan(experts_all, max_block=1024):
    n_tokens = experts_all.shape[1]
    block_t = max(k for k in range(LANES, max_block + 1, LANES) if n_tokens % k == 0)
    nb = n_tokens // block_t
    return pl.pallas_call(
        functools.partial(_plan_kernel, block_t=block_t),
        grid=(2, nb),
        in_specs=[pl.BlockSpec((TOP_K, block_t), lambda ph, j: (0, j))],
        out_specs=[pl.BlockSpec((TOP_K, block_t), lambda ph, j: (0, j * ph)),
                   _full((N_EXPERTS, LANES))],
        out_shape=[jax.ShapeDtypeStruct((TOP_K, n_tokens), I32),
                   jax.ShapeDtypeStruct((N_EXPERTS, LANES), I32)],
        scratch_shapes=[pltpu.VMEM((N_EXPERTS, LANES), F32)] * 3,
        compiler_params=pltpu.CompilerParams(dimension_semantics=("arbitrary", "arbitrary"),
                                             vmem_limit_bytes=VMEM_LIMIT),
        name="plan",
    )(experts_all)


def _sc_workers():
    info = plsc.get_sparse_core_info()
    return info.num_cores, info.num_cores * info.num_subcores


def _sc_scatter_rows(sources, pos_t, n_out):
    w = sources[0].shape[1]
    s = pos_t.shape[0]
    n_cores, n_workers = _sc_workers()
    bounds = np.cumsum([0] + [src.shape[0] // SC_CHUNK for src in sources])
    n_chunks = int(bounds[-1])
    iters = -(-n_chunks // n_workers)
    mesh = plsc.VectorSubcoreMesh(core_axis_name="c", subcore_axis_name="s")

    @functools.partial(
        pl.kernel, mesh=mesh, out_type=jax.ShapeDtypeStruct((n_out, w), sources[0].dtype),
        scratch_types=[pltpu.VMEM((SC_CHUNK, w), sources[0].dtype), pltpu.VMEM((s, SC_CHUNK), I32),
                       pltpu.SemaphoreType.DMA],
        name="dispatch")
    def k(*refs):
        src_hbm, (pos_hbm, out_hbm, rows_v, idx_v, sem) = refs[:len(sources)], refs[len(sources):]
        wid = lax.axis_index("s") * n_cores + lax.axis_index("c")

        @pl.loop(0, iters)
        def _(it):
            c = it * n_workers + wid
            for src, lo, hi in zip(src_hbm, bounds[:-1], bounds[1:]):
                @pl.when((c >= int(lo)) & (c < int(hi)))
                def _():
                    base = pl.multiple_of((c - int(lo)) * SC_CHUNK, SC_CHUNK)
                    pltpu.sync_copy(src.at[pl.ds(base, SC_CHUNK)], rows_v)

            @pl.when(c < n_chunks)
            def _():
                base = pl.multiple_of(c * SC_CHUNK, SC_CHUNK)
                pltpu.sync_copy(pos_hbm.at[:, pl.ds(base, SC_CHUNK)], idx_v)
                copies = [pltpu.async_copy(rows_v, out_hbm.at[idx_v.at[j]], sem) for j in range(s)]
                for cp in copies:
                    cp.wait()

    return k(*sources, pos_t)


def _sc_gather_rows(table, pos_t):
    _, w = table.shape
    s, t = pos_t.shape
    n_cores, n_workers = _sc_workers()
    n_chunks = t // SC_CHUNK
    iters = -(-n_chunks // n_workers)
    mesh = plsc.VectorSubcoreMesh(core_axis_name="c", subcore_axis_name="s")

    @functools.partial(
        pl.kernel, mesh=mesh, out_type=jax.ShapeDtypeStruct((s, t, w), table.dtype),
        scratch_types=[pltpu.VMEM((SC_CHUNK, w), table.dtype), pltpu.VMEM((s, SC_CHUNK), I32),
                       pltpu.SemaphoreType.DMA],
        name="combine")
    def k(table_hbm, pos_hbm, out_hbm, rows_v, idx_v, sem):
        wid = lax.axis_index("s") * n_cores + lax.axis_index("c")

        @pl.loop(0, iters)
        def _(it):
            c = it * n_workers + wid

            @pl.when(c < n_chunks)
            def _():
                base = pl.multiple_of(c * SC_CHUNK, SC_CHUNK)
                pltpu.sync_copy(pos_hbm.at[:, pl.ds(base, SC_CHUNK)], idx_v)
                for j in range(s):
                    pltpu.async_copy(table_hbm.at[idx_v.at[j]], rows_v, sem).wait()
                    pltpu.sync_copy(rows_v, out_hbm.at[j, pl.ds(base, SC_CHUNK)])

    return k(table, pos_t)


def _experts_kernel(first_ref, ntile_ref, cnt_ref, xs_hbm, wg_ref, wu_ref, wd_ref, ys_hbm,
                    wg16_ref, wu16_ref, wd16_ref, x_buf, y_buf, in_sem, out_sem):
    e = pl.program_id(0)
    n_used = first_ref[N_EXPERTS - 1] + ntile_ref[N_EXPERTS - 1]
    first, n_mine, count = first_ref[e], ntile_ref[e], cnt_ref[e]

    def tile_rows(g):
        return pl.ds(pl.multiple_of(g * ROW_TILE, ROW_TILE), ROW_TILE)

    def load(g):
        slot = lax.rem(g, STREAM_DEPTH)
        return pltpu.make_async_copy(xs_hbm.at[tile_rows(g)], x_buf.at[slot], in_sem.at[slot])

    def store(g):
        slot = lax.rem(g, STREAM_DEPTH)
        return pltpu.make_async_copy(y_buf.at[slot], ys_hbm.at[tile_rows(g)], out_sem.at[slot])

    @pl.when(e == 0)
    def _():
        for g0 in range(STREAM_DEPTH - 1):
            @pl.when(g0 < n_used)
            def _():
                load(g0).start()

    wg16_ref[...] = wg_ref[0].astype(BF16)
    wu16_ref[...] = wu_ref[0].astype(BF16)
    wd16_ref[...] = wd_ref[0].astype(BF16)

    def tile(j, carry):
        g = first + j
        slot = lax.rem(g, STREAM_DEPTH)
        load(g).wait()

        @pl.when(g + (STREAM_DEPTH - 1) < n_used)
        def _():
            load(g + (STREAM_DEPTH - 1)).start()

        @pl.when(g >= STREAM_DEPTH)
        def _():
            store(g - STREAM_DEPTH).wait()

        words = x_buf[slot]
        row = lax.broadcasted_iota(I32, words.shape, 0)
        words = jnp.where(row < count - j * ROW_TILE, words, jnp.uint32(0))
        lo, hi = _unpack_rows(words)
        lo, hi = lo.astype(BF16), hi.astype(BF16)
        hg = _dot(lo, wg16_ref[0:HALF, :]) + _dot(hi, wg16_ref[HALF:, :])
        hu = _dot(lo, wu16_ref[0:HALF, :]) + _dot(hi, wu16_ref[HALF:, :])
        a = (_silu(hg) * hu).astype(BF16)
        y_buf[slot] = _pack_rows(_dot(a, wd16_ref[...]))
        store(g).start()
        return carry

    lax.fori_loop(0, n_mine, tile, 0)

    @pl.when(e == N_EXPERTS - 1)
    def _():
        for back in range(STREAM_DEPTH, 0, -1):
            @pl.when(n_used >= back)
            def _():
                store(n_used - back).wait()


def _experts(xs, first_tile, n_tile, count, w_eg, w_eu, w_ed):
    d = D_MODEL
    by_expert = lambda e, *_: (e, 0, 0)
    grid_spec = pltpu.PrefetchScalarGridSpec(
        num_scalar_prefetch=3,
        grid=(N_EXPERTS,),
        in_specs=[pl.BlockSpec(memory_space=pl.ANY),
                  pl.BlockSpec((1, d, EXPERT_DIM), by_expert),
                  pl.BlockSpec((1, d, EXPERT_DIM), by_expert),
                  pl.BlockSpec((1, EXPERT_DIM, d), by_expert)],
        out_specs=pl.BlockSpec(memory_space=pl.ANY),
        scratch_shapes=[pltpu.VMEM((d, EXPERT_DIM), BF16), pltpu.VMEM((d, EXPERT_DIM), BF16),
                        pltpu.VMEM((EXPERT_DIM, d), BF16),
                        pltpu.VMEM((STREAM_DEPTH, ROW_TILE, HALF), U32),
                        pltpu.VMEM((STREAM_DEPTH, ROW_TILE, HALF), U32),
                        pltpu.SemaphoreType.DMA((STREAM_DEPTH,)), pltpu.SemaphoreType.DMA((STREAM_DEPTH,))])
    return pl.pallas_call(
        _experts_kernel,
        grid_spec=grid_spec,
        out_shape=jax.ShapeDtypeStruct(xs.shape, U32),
        compiler_params=pltpu.CompilerParams(dimension_semantics=("arbitrary",),
                                             vmem_limit_bytes=VMEM_LIMIT),
        name="experts",
    )(first_tile, n_tile, count, xs, w_eg, w_eu, w_ed)


def _final_kernel(z_ref, gatew_ref, h2_ref, x1_ref, mod_ref, normf_ref, wsg_ref, wsu_ref, wsd_ref, *rest):
    y_ref = rest[-1]
    lo, hi = _unpack_rows(h2_ref[...])
    h = jnp.concatenate([lo, hi], axis=-1).astype(BF16)
    a = _silu(_dot(h, wsg_ref[...])) * _dot(h, wsu_ref[...])
    acc = _dot(a.astype(BF16), wsd_ref[...])
    for s in range(TOP_K):
        lo, hi = _unpack_rows(z_ref[s])
        acc = acc + gatew_ref[:, s:s + 1] * jnp.concatenate([lo, hi], axis=-1)
    x2 = x1_ref[...] + _mod(mod_ref, 5) * acc
    y_ref[...] = _rms(x2) * normf_ref[...]


def _final(z, gate_w, h2, x1, mod, norm_f, w_sg16, w_su16, w_sd16, block_t, first_block, per_seq,
           seq0=0, out_rows=None, y_prev=None):
    t, d = x1.shape
    out_rows = t if out_rows is None else out_rows
    out_first = seq0 * per_seq
    tok = lambda i: (i, 0)
    if per_seq:
        mod_spec = pl.BlockSpec((1, 6, d), lambda i: (seq0 + i // per_seq, 0, 0))
    else:
        mod_spec = pl.BlockSpec((block_t, 6 * d), tok)
    operands = [z, gate_w, h2, x1, mod, norm_f, w_sg16, w_su16, w_sd16]
    in_specs = [pl.BlockSpec((TOP_K, block_t, HALF), lambda i: (0, first_block + i, 0)),
                pl.BlockSpec((block_t, LANES), tok),
                pl.BlockSpec((block_t, HALF), tok),
                pl.BlockSpec((block_t, d), tok),
                mod_spec,
                _full((1, d)),
                _full((d, EXPERT_DIM)), _full((d, EXPERT_DIM)), _full((EXPERT_DIM, d))]
    aliases = {}
    if y_prev is not None:
        aliases = {len(operands): 0}
        operands.append(y_prev)
        in_specs.append(pl.BlockSpec(memory_space=pl.ANY))
    return pl.pallas_call(
        _final_kernel,
        grid=(t // block_t,),
        in_specs=in_specs,
        out_specs=pl.BlockSpec((block_t, d), lambda i: (out_first + i, 0)),
        out_shape=jax.ShapeDtypeStruct((out_rows, d), F32),
        input_output_aliases=aliases,
        compiler_params=pltpu.CompilerParams(dimension_semantics=("arbitrary",),
                                             vmem_limit_bytes=VMEM_LIMIT),
        name="final",
    )(*operands)


def kernel(x_prompt, x_sample, c_prompt, c_sample, state_ret, state_pool, norm1, norm2, norm_f,
           w_ada, b_ada, w_in, w_out, w_pool, pool_scale, w_router, router_bias, w_exp_gate,
           w_exp_up, w_exp_down, w_sh_gate, w_sh_up, w_sh_down):
    b, l, d = x_prompt.shape
    n = x_sample.shape[0]
    past_len = 16384

    mod = _ada(jnp.concatenate([c_prompt, c_sample], axis=0), w_ada[0], b_ada[0])
    mod_p = mod[:b].reshape(b, 6, d)
    mod_s = mod[b:]

    w_in16 = w_in[0].astype(BF16)
    w_out16 = w_out[0].astype(BF16)
    wr_t = w_router[0].T
    bias_t = jnp.broadcast_to(router_bias[0][:, None], (N_EXPERTS, LANES))
    n1, n2, nf = norm1[0].reshape(1, d), norm2[0].reshape(1, d), norm_f.reshape(1, d)
    ps = pool_scale[0].reshape(1, POOL_WIDTH)
    shared = (w_sh_gate[0].astype(BF16), w_sh_up[0].astype(BF16), w_sh_down[0].astype(BF16))

    def routed(sources, experts):
        n_tiles = experts.shape[1] * TOP_K // ROW_TILE + N_EXPERTS
        pos_t, meta = _plan(experts)
        xs = _sc_scatter_rows(sources, pos_t, n_tiles * ROW_TILE)
        ys = _experts(xs, meta[:, 0], meta[:, 1], meta[:, 2], w_exp_gate[0], w_exp_up[0], w_exp_down[0])
        return _sc_gather_rows(ys, pos_t)

    ba = b // 2
    bb = b - ba
    mix_args = (n1, w_in16, w_pool[0], ps, w_out16, n2, wr_t, bias_t)
    x1_a, h2_a, experts_a, gatew_a, ret_a, pool_a = _mix_prompt(x_prompt, mod_p, 0, ba, *mix_args)
    z_a = routed((h2_a,), experts_a)
    x1_b, h2_b, experts_b, gatew_b, ret_b, pool_b = _mix_prompt(x_prompt, mod_p, ba, bb, *mix_args)
    x1_s, h2_s, experts_s, gatew_s, ret_s, pool_s = _mix_sample(
        x_sample.reshape(n, d), mod_s, state_ret[0], state_pool[0], float(past_len), *mix_args)
    z_b = routed((h2_b, h2_s), jnp.concatenate([experts_b, experts_s], axis=1))

    block_t = 256
    per_seq = l // block_t
    y_p = _final(z_a, gatew_a, h2_a, x1_a.reshape(ba * l, d), mod_p, nf, *shared,
                 block_t=block_t, first_block=0, per_seq=per_seq, seq0=0, out_rows=b * l)
    y_p = _final(z_b, gatew_b, h2_b, x1_b.reshape(bb * l, d), mod_p, nf, *shared,
                 block_t=block_t, first_block=0, per_seq=per_seq, seq0=ba, out_rows=b * l, y_prev=y_p)
    y_s = _final(z_b, gatew_s, h2_s, x1_s, mod_s, nf, *shared,
                 block_t=n, first_block=bb * l // n, per_seq=0)

    ret_p = jnp.concatenate([ret_a, ret_b], axis=0)
    pool_p = jnp.concatenate([pool_a, pool_b], axis=0)
    return (y_p.reshape(b, l, d), y_s.reshape(n, 1, d), ret_p[None], pool_p[None],
            ret_s[None], pool_s[None])
```

```python
import functools

import jax
import jax.numpy as jnp
import numpy as np
from jax import lax
from jax.experimental import pallas as pl
from jax.experimental.pallas import tpu as pltpu
from jax.experimental.pallas import tpu_sc as plsc

D_MODEL = 1024
RET_HEADS = 4
RET_QK_DIM = 64
RET_V_DIM = 128
RET_WIDTH = RET_HEADS * RET_V_DIM
QK_WIDTH = RET_HEADS * RET_QK_DIM
ROPE_BASE = 10000.0
POOL_WINDOWS = (2, 4, 8, 16)
POOL_WIDTH = 512
POOL_GROUP_DIM = 128
POOL_BUF = 15
IN_WIDTH = 2 * QK_WIDTH + 2 * RET_WIDTH + POOL_WIDTH
N_EXPERTS = 64
TOP_K = 8
N_EXPERT_GROUPS = 8
GROUP_SIZE = N_EXPERTS // N_EXPERT_GROUPS
TOP_GROUPS = 4
EXPERT_DIM = 256
ROUTE_SCALE = 2.5
EPS = 1e-6

LANES = 128
POOL_CARRY = 16
VMEM_LIMIT = 56 * 1024 * 1024
HALF = D_MODEL // 2
ROW_TILE = 256
SC_CHUNK = 128
STREAM_DEPTH = 4

BF16 = jnp.bfloat16
F32 = jnp.float32
U32 = jnp.uint32
I32 = jnp.int32


def _silu(x):
    return x * jax.nn.sigmoid(x)


def _dot(a, b):
    return jnp.dot(a, b, preferred_element_type=F32)


def _rms(x):
    return x * lax.rsqrt(jnp.mean(x * x, axis=-1, keepdims=True) + EPS)


def _mod(mod_ref, i):
    if len(mod_ref.shape) == 3:
        return mod_ref[0, i:i + 1, :]
    return mod_ref[:, i * D_MODEL:(i + 1) * D_MODEL]


def _split_bf16(x):
    hi = x.astype(BF16)
    lo = (x - hi.astype(F32)).astype(BF16)
    return hi, lo


def _pack_rows(x):
    lo = lax.bitcast_convert_type(x[:, :HALF].astype(BF16).astype(F32), U32)
    hi = lax.bitcast_convert_type(x[:, HALF:].astype(BF16).astype(F32), U32)
    return (hi & jnp.uint32(0xFFFF0000)) | (lo >> jnp.uint32(16))


def _unpack_rows(w):
    lo = lax.bitcast_convert_type(w << jnp.uint32(16), F32)
    hi = lax.bitcast_convert_type(w & jnp.uint32(0xFFFF0000), F32)
    return lo, hi


def _first_max_onehot(work, idx, n):
    m = jnp.max(work, axis=0, keepdims=True)
    first = jnp.min(jnp.where(work == m, idx, float(n)), axis=0, keepdims=True)
    return idx == first


def _route(h2, wr_t_ref, bias_t_ref):
    n = h2.shape[0]
    h_hi, h_lo = _split_bf16(h2)
    w_hi, w_lo = _split_bf16(wr_t_ref[...])
    nt = (((1,), (1,)), ((), ()))
    logits = (lax.dot_general(w_hi, h_hi, nt, preferred_element_type=F32)
              + lax.dot_general(w_hi, h_lo, nt, preferred_element_type=F32)
              + lax.dot_general(w_lo, h_hi, nt, preferred_element_type=F32))
    scores = jax.nn.sigmoid(logits)
    biased = scores + bias_t_ref[:, 0:1]
    b3 = biased.reshape(N_EXPERT_GROUPS, GROUP_SIZE, n)
    i3 = lax.broadcasted_iota(I32, b3.shape, 1).astype(F32)
    m1 = jnp.max(b3, axis=1, keepdims=True)
    first = jnp.min(jnp.where(b3 == m1, i3, float(GROUP_SIZE)), axis=1, keepdims=True)
    m2 = jnp.max(jnp.where(i3 == first, -jnp.inf, b3), axis=1, keepdims=True)
    gscore = (m1 + m2).reshape(N_EXPERT_GROUPS, n)
    gidx = lax.broadcasted_iota(I32, gscore.shape, 0).astype(F32)
    gsel = jnp.zeros(gscore.shape, F32)
    work = gscore
    for _ in range(TOP_GROUPS):
        hit = _first_max_onehot(work, gidx, N_EXPERT_GROUPS)
        gsel = jnp.where(hit, 1.0, gsel)
        work = jnp.where(hit, -jnp.inf, work)
    gsel3 = jnp.broadcast_to(gsel.reshape(N_EXPERT_GROUPS, 1, n), b3.shape)
    work = jnp.where(gsel3 > 0.0, b3, -jnp.inf).reshape(N_EXPERTS, n)
    eidx = lax.broadcasted_iota(I32, work.shape, 0).astype(F32)
    sel = jnp.zeros(work.shape, F32)
    for _ in range(TOP_K):
        hit = _first_max_onehot(work, eidx, N_EXPERTS)
        sel = jnp.where(hit, 1.0, sel)
        work = jnp.where(hit, -jnp.inf, work)
    picked = jnp.where(sel > 0.0, scores, 0.0)
    gates = picked / jnp.sum(picked, axis=0, keepdims=True) * ROUTE_SCALE
    below = (lax.broadcasted_iota(I32, (N_EXPERTS, N_EXPERTS), 1)
             < lax.broadcasted_iota(I32, (N_EXPERTS, N_EXPERTS), 0))
    slot = _dot(jnp.where(below, 1.0, 0.0).astype(BF16), sel.astype(BF16))
    e_rows, w_rows = [], []
    for s in range(TOP_K):
        here = jnp.where(slot == float(s), sel, 0.0)
        e_rows.append(jnp.sum(here * eidx, axis=0, keepdims=True))
        w_rows.append(jnp.sum(here * gates, axis=0, keepdims=True))
    experts = jnp.concatenate(e_rows, axis=0).astype(I32)
    w_t = jnp.concatenate(w_rows + [jnp.zeros((LANES - TOP_K, n), F32)], axis=0)
    return experts, w_t.T


def _group_norm_gate(o, g):
    parts = []
    for h in range(RET_HEADS):
        oh = o[:, h * RET_V_DIM:(h + 1) * RET_V_DIM]
        mu = jnp.mean(oh, axis=-1, keepdims=True)
        ctr = oh - mu
        var = jnp.mean(ctr * ctr, axis=-1, keepdims=True)
        parts.append(ctr * lax.rsqrt(var + EPS))
    return _silu(g) * jnp.concatenate(parts, axis=-1)


def _pool_project(pooled, w_pool_ref, pool_scale_ref):
    parts = [_dot(p.astype(BF16), w_pool_ref[gi].astype(BF16)) for gi, p in enumerate(pooled)]
    return jnp.concatenate(parts, axis=-1) * pool_scale_ref[...]


def _mix_tail(x, o_gated, p, mod_ref, w_out_ref, norm2_ref, wr_t_ref, bias_t_ref,
              x1_ref, h2_ref, experts_ref, gatew_ref):
    mix = jnp.concatenate([o_gated, p], axis=-1).astype(BF16)
    y = _dot(mix, w_out_ref[...])
    x1 = x + _mod(mod_ref, 2) * y
    h2 = _rms(x1) * norm2_ref[...] * (1.0 + _mod(mod_ref, 4)) + _mod(mod_ref, 3)
    x1_ref[...] = x1.reshape(x1_ref.shape)
    h2_ref[...] = _pack_rows(h2)
    experts, gate_w = _route(h2, wr_t_ref, bias_t_ref)
    experts_ref[...] = experts
    gatew_ref[...] = gate_w


def _ada_kernel(c_ref, w_ref, b_ref, o_ref):
    cs = _silu(c_ref[...]).astype(BF16)
    o_ref[...] = _dot(cs, w_ref[...].astype(BF16)) + b_ref[...]


def _ada(c_all, w_ada, b_ada, block_n=1536):
    n, d = c_all.shape
    width = w_ada.shape[1]
    return pl.pallas_call(
        _ada_kernel,
        grid=(width // block_n,),
        in_specs=[pl.BlockSpec((n, d), lambda j: (0, 0)),
                  pl.BlockSpec((d, block_n), lambda j: (0, j)),
                  pl.BlockSpec((1, block_n), lambda j: (0, j))],
        out_specs=pl.BlockSpec((n, block_n), lambda j: (0, j)),
        out_shape=jax.ShapeDtypeStruct((n, width), F32),
        compiler_params=pltpu.CompilerParams(vmem_limit_bytes=VMEM_LIMIT),
        name="ada",
    )(c_all, w_ada, b_ada.reshape(1, width))


def _mix_prompt_kernel(x_ref, mod_ref, norm1_ref, w_in_ref, cos_ref, sin_ref, dmat_ref, cross_ref,
                       tail_ref, cdec_ref, w_pool_ref, pool_scale_ref, w_out_ref, norm2_ref,
                       wr_t_ref, bias_t_ref,
                       x1_ref, h2_ref, experts_ref, gatew_ref, ret_ref, pool_ref,
                       state_ref, ext_ref, o_ref, *, block_l, chunk):
    li = pl.program_id(1)

    @pl.when(li == 0)
    def _():
        state_ref[...] = jnp.zeros_like(state_ref)
        ext_ref[0:POOL_CARRY, :] = jnp.zeros((POOL_CARRY, POOL_WIDTH), F32)

    x = x_ref[0]
    h = _rms(x) * norm1_ref[...] * (1.0 + _mod(mod_ref, 1)) + _mod(mod_ref, 0)
    proj = _dot(h.astype(BF16), w_in_ref[...])
    q = proj[:, 0:QK_WIDTH]
    k = proj[:, QK_WIDTH:2 * QK_WIDTH]
    v = proj[:, 2 * QK_WIDTH:2 * QK_WIDTH + RET_WIDTH]
    g = proj[:, 2 * QK_WIDTH + RET_WIDTH:2 * QK_WIDTH + 2 * RET_WIDTH]
    u = proj[:, 2 * QK_WIDTH + 2 * RET_WIDTH:]

    lane = lax.broadcasted_iota(I32, q.shape, 1)
    first_half = (lane % RET_QK_DIM) < (RET_QK_DIM // 2)
    cos_t = cos_ref[...]
    sin_t = sin_ref[...]

    def rot(t):
        partner = jnp.where(first_half, pltpu.roll(t, QK_WIDTH - RET_QK_DIM // 2, axis=1),
                            pltpu.roll(t, RET_QK_DIM // 2, axis=1))
        return t * cos_t + partner * sin_t

    q = rot(q)
    k = rot(k) * (RET_QK_DIM ** -0.5)
    k_t = k.T
    v16 = v.astype(BF16)
    head_of_lane = lax.broadcasted_iota(I32, (chunk, QK_WIDTH), 1) // RET_QK_DIM

    for c in range(block_l // chunk):
        rows = slice(c * chunk, (c + 1) * chunk)
        q_c = q[rows]
        kt_c = k_t[:, rows]
        kt16 = kt_c.astype(BF16)
        state16 = state_ref[...].astype(BF16)
        for hd in range(RET_HEADS):
            in_head = head_of_lane == hd
            q_h = jnp.where(in_head, q_c, 0.0).astype(BF16)
            v_h = v16[rows, hd * RET_V_DIM:(hd + 1) * RET_V_DIM]
            scores = _dot(q_h, kt16) * dmat_ref[hd]
            inner = _dot(scores.astype(BF16), v_h)
            cross = _dot(q_h, state16) * cross_ref[hd]
            o_ref[rows, hd * RET_V_DIM:(hd + 1) * RET_V_DIM] = inner + cross
            hrows = slice(hd * RET_QK_DIM, (hd + 1) * RET_QK_DIM)
            k_dec = (kt_c[hrows] * tail_ref[hd:hd + 1, :]).astype(BF16)
            state_ref[hrows, :] = state_ref[hrows, :] * cdec_ref[hd] + _dot(k_dec, v_h)

    o_gated = _group_norm_gate(o_ref[...], g)

    ext_ref[POOL_CARRY:POOL_CARRY + block_l, :] = u
    pos = (li * block_l + lax.broadcasted_iota(I32, (block_l, 1), 0)).astype(F32)
    pooled = []
    for gi, w in enumerate(POOL_WINDOWS):
        lanes = slice(gi * POOL_GROUP_DIM, (gi + 1) * POOL_GROUP_DIM)
        acc = ext_ref[POOL_CARRY:POOL_CARRY + block_l, lanes]
        for j in range(1, w):
            acc = acc + ext_ref[POOL_CARRY - j:POOL_CARRY - j + block_l, lanes]
        cnt = jnp.minimum(pos + 1.0, float(w))
        pooled.append(acc / cnt - u[:, lanes])
    p = _pool_project(pooled, w_pool_ref, pool_scale_ref)
    ext_ref[0:POOL_CARRY, :] = ext_ref[block_l:block_l + POOL_CARRY, :]

    _mix_tail(x, o_gated, p, mod_ref, w_out_ref, norm2_ref, wr_t_ref, bias_t_ref,
              x1_ref, h2_ref, experts_ref, gatew_ref)

    @pl.when(li == pl.num_programs(1) - 1)
    def _():
        ret_ref[...] = state_ref[...].reshape(ret_ref.shape)
        pool_ref[...] = ext_ref[1:1 + POOL_BUF, :].reshape(pool_ref.shape)


def _decay_tables(chunk):
    lg = jnp.log(1.0 - 2.0 ** (-5.0 - jnp.arange(RET_HEADS, dtype=F32)))
    idx = jnp.arange(chunk, dtype=F32)
    diff = idx[:, None] - idx[None, :]
    causal = diff >= 0
    dmat = jnp.where(causal[None], jnp.exp(lg[:, None, None] * jnp.where(causal, diff, 0.0)[None]), 0.0)
    cross = jnp.exp(lg[:, None] * (idx[None, :] + 1.0))
    cross = jnp.broadcast_to(cross[:, :, None], (RET_HEADS, chunk, RET_V_DIM))
    tail = jnp.exp(lg[:, None] * (chunk - 1.0 - idx)[None, :])
    cdec = jnp.broadcast_to(jnp.exp(lg * chunk)[:, None, None], (RET_HEADS, RET_QK_DIM, RET_V_DIM))
    return dmat, cross, tail, cdec


def _rotary_tables(pos):
    half = RET_QK_DIM // 2
    freqs = ROPE_BASE ** (-jnp.arange(half, dtype=F32) / half)
    ang = pos[:, None] * freqs[None, :]
    cos, sin = jnp.cos(ang), jnp.sin(ang)
    cos_t = jnp.tile(jnp.concatenate([cos, cos], axis=-1), (1, RET_HEADS))
    sin_t = jnp.tile(jnp.concatenate([-sin, sin], axis=-1), (1, RET_HEADS))
    return cos_t, sin_t


def _full(shape):
    return pl.BlockSpec(shape, lambda *_: (0,) * len(shape))


def _mix_prompt(x, mod, b0, b, norm1, w_in16, w_pool, pool_scale, w_out16, norm2, wr_t, bias_t,
                block_l=512, chunk=256):
    _, l, d = x.shape
    n_tokens = b * l
    nl = l // block_l
    cos_t, sin_t = _rotary_tables(jnp.arange(l, dtype=F32))
    dmat, cross, tail, cdec = _decay_tables(chunk)
    kernel = functools.partial(_mix_prompt_kernel, block_l=block_l, chunk=chunk)
    tok = lambda bi, li: (bi, li, 0)
    flat = lambda bi, li: (bi * nl + li, 0)
    return pl.pallas_call(
        kernel,
        grid=(b, nl),
        in_specs=[pl.BlockSpec((1, block_l, d), lambda bi, li: (b0 + bi, li, 0)),
                  pl.BlockSpec((1, 6, d), lambda bi, li: (b0 + bi, 0, 0)),
                  _full((1, d)),
                  _full((d, IN_WIDTH)),
                  pl.BlockSpec((block_l, QK_WIDTH), lambda bi, li: (li, 0)),
                  pl.BlockSpec((block_l, QK_WIDTH), lambda bi, li: (li, 0)),
                  _full(dmat.shape), _full(cross.shape), _full(tail.shape), _full(cdec.shape),
                  _full(w_pool.shape), _full((1, POOL_WIDTH)), _full((d, d)), _full((1, d)),
                  _full(wr_t.shape), _full(bias_t.shape)],
        out_specs=[pl.BlockSpec((1, block_l, d), tok),
                   pl.BlockSpec((block_l, HALF), flat),
                   pl.BlockSpec((TOP_K, block_l), lambda bi, li: (0, bi * nl + li)),
                   pl.BlockSpec((block_l, LANES), flat),
                   pl.BlockSpec((1, RET_HEADS, RET_QK_DIM, RET_V_DIM), lambda bi, li: (bi, 0, 0, 0)),
                   pl.BlockSpec((1, POOL_BUF, POOL_WIDTH), lambda bi, li: (bi, 0, 0))],
        out_shape=[jax.ShapeDtypeStruct((b, l, d), F32),
                   jax.ShapeDtypeStruct((n_tokens, HALF), U32),
                   jax.ShapeDtypeStruct((TOP_K, n_tokens), I32),
                   jax.ShapeDtypeStruct((b * l, LANES), F32),
                   jax.ShapeDtypeStruct((b, RET_HEADS, RET_QK_DIM, RET_V_DIM), F32),
                   jax.ShapeDtypeStruct((b, POOL_BUF, POOL_WIDTH), F32)],
        scratch_shapes=[pltpu.VMEM((QK_WIDTH, RET_V_DIM), F32),
                        pltpu.VMEM((POOL_CARRY + block_l, POOL_WIDTH), F32),
                        pltpu.VMEM((block_l, RET_WIDTH), F32)],
        compiler_params=pltpu.CompilerParams(dimension_semantics=("arbitrary", "arbitrary"),
                                             vmem_limit_bytes=VMEM_LIMIT),
        name="mix_prompt",
    )(x, mod, norm1, w_in16, cos_t, sin_t, dmat, cross, tail, cdec, w_pool, pool_scale,
      w_out16, norm2, wr_t, bias_t)


def _mix_sample_front_kernel(x_ref, mod_ref, norm1_ref, w_in_ref, cos_ref, sin_ref,
                             qt_ref, kt_ref, v_ref, g_ref, u_ref):
    x = x_ref[...]
    h = _rms(x) * norm1_ref[...] * (1.0 + _mod(mod_ref, 1)) + _mod(mod_ref, 0)
    proj = _dot(h.astype(BF16), w_in_ref[...])
    half = RET_QK_DIM // 2
    cos_c = cos_ref[...]
    sin_c = sin_ref[...]

    def rot_t(t):
        parts = []
        for hd in range(RET_HEADS):
            t1 = t[hd * RET_QK_DIM:hd * RET_QK_DIM + half]
            t2 = t[hd * RET_QK_DIM + half:(hd + 1) * RET_QK_DIM]
            parts += [t1 * cos_c - t2 * sin_c, t1 * sin_c + t2 * cos_c]
        return jnp.concatenate(parts, axis=0)

    qt_ref[...] = rot_t(proj[:, 0:QK_WIDTH].T)
    kt_ref[...] = rot_t(proj[:, QK_WIDTH:2 * QK_WIDTH].T) * (RET_QK_DIM ** -0.5)
    v_ref[...] = proj[:, 2 * QK_WIDTH:2 * QK_WIDTH + RET_WIDTH]
    g_ref[...] = proj[:, 2 * QK_WIDTH + RET_WIDTH:2 * QK_WIDTH + 2 * RET_WIDTH]
    u_ref[...] = proj[:, 2 * QK_WIDTH + 2 * RET_WIDTH:]


def _ret_step_kernel(qt_ref, kt_ref, v_ref, s0_ref, o_ref, s1_ref, *, block_b, decays):
    i = pl.program_id(0)
    lane = lax.broadcasted_iota(I32, qt_ref.shape, 1)
    for j in range(block_b):
        bi = i * block_b + j
        here = lane == bi
        q_col = jnp.sum(jnp.where(here, qt_ref[...], 0.0), axis=1, keepdims=True)
        k_col = jnp.sum(jnp.where(here, kt_ref[...], 0.0), axis=1, keepdims=True)
        v_row = v_ref[pl.ds(bi, 1), :]
        outs = []
        for hd in range(RET_HEADS):
            hrows = slice(hd * RET_QK_DIM, (hd + 1) * RET_QK_DIM)
            s1 = decays[hd] * s0_ref[j, hd] + k_col[hrows] * v_row[:, hd * RET_V_DIM:(hd + 1) * RET_V_DIM]
            s1_ref[j, hd] = s1
            outs.append(jnp.sum(q_col[hrows] * s1, axis=0, keepdims=True))
        o_ref[pl.ds(bi, 1), :] = jnp.concatenate(outs, axis=-1)


def _mix_sample_back_kernel(x_ref, mod_ref, o_ref, g_ref, u_ref, buf_ref, w_pool_ref, pool_scale_ref,
                            w_out_ref, norm2_ref, wr_t_ref, bias_t_ref,
                            x1_ref, h2_ref, experts_ref, gatew_ref, pool_ref):
    o_gated = _group_norm_gate(o_ref[...], g_ref[...])
    u = u_ref[...]
    pooled = []
    for gi, w in enumerate(POOL_WINDOWS):
        lanes = slice(gi * POOL_GROUP_DIM, (gi + 1) * POOL_GROUP_DIM)
        acc = u[:, lanes]
        for j in range(1, w):
            acc = acc + buf_ref[:, POOL_BUF - j, lanes]
        pooled.append(acc / float(w) - u[:, lanes])
    p = _pool_project(pooled, w_pool_ref, pool_scale_ref)
    pool_ref[:, 0:POOL_BUF - 1, :] = buf_ref[:, 1:POOL_BUF, :]
    pool_ref[:, POOL_BUF - 1, :] = u
    _mix_tail(x_ref[...], o_gated, p, mod_ref, w_out_ref, norm2_ref, wr_t_ref, bias_t_ref,
              x1_ref, h2_ref, experts_ref, gatew_ref)


def _mix_sample(x, mod, state_ret, state_pool, start, norm1, w_in16, w_pool,
                pool_scale, w_out16, norm2, wr_t, bias_t, block_b=8):
    n, d = x.shape
    half = RET_QK_DIM // 2
    freqs = ROPE_BASE ** (-jnp.arange(half, dtype=F32) / half)
    ang = jnp.full((1,), start, F32)[:, None] * freqs[None, :]
    cos_c = jnp.broadcast_to(jnp.cos(ang).T, (half, n))
    sin_c = jnp.broadcast_to(jnp.sin(ang).T, (half, n))
    params = pltpu.CompilerParams(vmem_limit_bytes=VMEM_LIMIT)
    qt, kt, v, g, u = pl.pallas_call(
        _mix_sample_front_kernel,
        out_shape=[jax.ShapeDtypeStruct((QK_WIDTH, n), F32), jax.ShapeDtypeStruct((QK_WIDTH, n), F32),
                   jax.ShapeDtypeStruct((n, RET_WIDTH), F32), jax.ShapeDtypeStruct((n, RET_WIDTH), F32),
                   jax.ShapeDtypeStruct((n, POOL_WIDTH), F32)],
        compiler_params=params,
        name="mix_sample_front",
    )(x, mod, norm1, w_in16, cos_c, sin_c)

    lg = np.log(1.0 - 2.0 ** (-5.0 - np.arange(RET_HEADS, dtype=np.float32)), dtype=np.float32)
    decays = tuple(float(np.exp(lg[h])) for h in range(RET_HEADS))
    state_block = (block_b, RET_HEADS, RET_QK_DIM, RET_V_DIM)
    o, s1 = pl.pallas_call(
        functools.partial(_ret_step_kernel, block_b=block_b, decays=decays),
        grid=(n // block_b,),
        in_specs=[_full((QK_WIDTH, n)), _full((QK_WIDTH, n)), _full((n, RET_WIDTH)),
                  pl.BlockSpec(state_block, lambda i: (i, 0, 0, 0))],
        out_specs=[_full((n, RET_WIDTH)), pl.BlockSpec(state_block, lambda i: (i, 0, 0, 0))],
        out_shape=[jax.ShapeDtypeStruct((n, RET_WIDTH), F32),
                   jax.ShapeDtypeStruct(state_ret.shape, F32)],
        compiler_params=pltpu.CompilerParams(dimension_semantics=("arbitrary",),
                                             vmem_limit_bytes=VMEM_LIMIT),
        name="ret_step",
    )(qt, kt, v, state_ret)

    x1, h2, experts, gate_w, pool = pl.pallas_call(
        _mix_sample_back_kernel,
        out_shape=[jax.ShapeDtypeStruct((n, d), F32),
                   jax.ShapeDtypeStruct((n, HALF), U32),
                   jax.ShapeDtypeStruct((TOP_K, n), I32),
                   jax.ShapeDtypeStruct((n, LANES), F32),
                   jax.ShapeDtypeStruct(state_pool.shape, F32)],
        compiler_params=params,
        name="mix_sample_back",
    )(x, mod, o, g, u, state_pool, w_pool, pool_scale, w_out16, norm2, wr_t, bias_t)
    return x1, h2, experts, gate_w, s1, pool


def _plan_kernel(experts_ref, pos_ref, meta_ref, cnt_ref, carry_ref, off_ref, *, block_t):
    phase = pl.program_id(0)
    j = pl.program_id(1)
    e_blk = experts_ref[...]
    eidx = lax.broadcasted_iota(I32, (N_EXPERTS, block_t), 0)
    member = jnp.zeros((N_EXPERTS, block_t), F32)
    for s in range(TOP_K):
        member = member + jnp.where(eidx == e_blk[s:s + 1, :], 1.0, 0.0)
    per_expert = jnp.broadcast_to(jnp.sum(member, axis=1, keepdims=True), (N_EXPERTS, LANES))

    @pl.when((phase == 0) & (j == 0))
    def _():
        cnt_ref[...] = jnp.zeros_like(cnt_ref)

    @pl.when(phase == 0)
    def _():
        cnt_ref[...] += per_expert

    @pl.when((phase == 0) & (j == pl.num_programs(1) - 1))
    def _():
        cnt = cnt_ref[...]
        n_tile = jnp.floor((cnt + (ROW_TILE - 1.0)) * (1.0 / ROW_TILE))
        upto = (lax.broadcasted_iota(I32, (N_EXPERTS, N_EXPERTS), 1)
                <= lax.broadcasted_iota(I32, (N_EXPERTS, N_EXPERTS), 0))
        tile_end = _dot(jnp.where(upto, 1.0, 0.0).astype(BF16), n_tile.astype(BF16))
        tile_start = tile_end - n_tile
        off_ref[...] = tile_start * ROW_TILE
        carry_ref[...] = jnp.zeros_like(carry_ref)
        lane = lax.broadcasted_iota(I32, cnt.shape, 1)
        meta_ref[...] = jnp.where(lane == 0, tile_start, jnp.where(lane == 1, n_tile, cnt)).astype(I32)

    @pl.when(phase == 1)
    def _():
        before = (lax.broadcasted_iota(I32, (block_t, block_t), 0)
                  < lax.broadcasted_iota(I32, (block_t, block_t), 1))
        rank = _dot(member.astype(BF16), jnp.where(before, 1.0, 0.0).astype(BF16))
        row = off_ref[:, 0:1] + carry_ref[:, 0:1] + rank
        carry_ref[...] += per_expert
        out = [jnp.sum(jnp.where(eidx == e_blk[s:s + 1, :], row, 0.0), axis=0, keepdims=True)
               for s in range(TOP_K)]
        pos_ref[...] = jnp.concatenate(out, axis=0).astype(I32)


def _plan(experts_all, max_block=1024):
    n_tokens = experts_all.shape[1]
    block_t = max(k for k in range(LANES, max_block + 1, LANES) if n_tokens % k == 0)
    nb = n_tokens // block_t
    return pl.pallas_call(
        functools.partial(_plan_kernel, block_t=block_t),
        grid=(2, nb),
        in_specs=[pl.BlockSpec((TOP_K, block_t), lambda ph, j: (0, j))],
        out_specs=[pl.BlockSpec((TOP_K, block_t), lambda ph, j: (0, j * ph)),
                   _full((N_EXPERTS, LANES))],
        out_shape=[jax.ShapeDtypeStruct((TOP_K, n_tokens), I32),
                   jax.ShapeDtypeStruct((N_EXPERTS, LANES), I32)],
        scratch_shapes=[pltpu.VMEM((N_EXPERTS, LANES), F32)] * 3,
        compiler_params=pltpu.CompilerParams(dimension_semantics=("arbitrary", "arbitrary"),
                                             vmem_limit_bytes=VMEM_LIMIT),
        name="plan",
    )(experts_all)


def _sc_workers():
    info = plsc.get_sparse_core_info()
    return info.num_cores, info.num_cores * info.num_subcores


def _sc_scatter_rows(sources, pos_t, n_out):
    w = sources[0].shape[1]
    s = pos_t.shape[0]
    n_cores, n_workers = _sc_workers()
    bounds = np.cumsum([0] + [src.shape[0] // SC_CHUNK for src in sources])
    n_chunks = int(bounds[-1])
    iters = -(-n_chunks // n_workers)
    mesh = plsc.VectorSubcoreMesh(core_axis_name="c", subcore_axis_name="s")

    @functools.partial(
        pl.kernel, mesh=mesh, out_type=jax.ShapeDtypeStruct((n_out, w), sources[0].dtype),
        scratch_types=[pltpu.VMEM((SC_CHUNK, w), sources[0].dtype), pltpu.VMEM((s, SC_CHUNK), I32),
                       pltpu.SemaphoreType.DMA],
        name="dispatch")
    def k(*refs):
        src_hbm, (pos_hbm, out_hbm, rows_v, idx_v, sem) = refs[:len(sources)], refs[len(sources):]
        wid = lax.axis_index("s") * n_cores + lax.axis_index("c")

        @pl.loop(0, iters)
        def _(it):
            c = it * n_workers + wid
            for src, lo, hi in zip(src_hbm, bounds[:-1], bounds[1:]):
                @pl.when((c >= int(lo)) & (c < int(hi)))
                def _():
                    base = pl.multiple_of((c - int(lo)) * SC_CHUNK, SC_CHUNK)
                    pltpu.sync_copy(src.at[pl.ds(base, SC_CHUNK)], rows_v)

            @pl.when(c < n_chunks)
            def _():
                base = pl.multiple_of(c * SC_CHUNK, SC_CHUNK)
                pltpu.sync_copy(pos_hbm.at[:, pl.ds(base, SC_CHUNK)], idx_v)
                copies = [pltpu.async_copy(rows_v, out_hbm.at[idx_v.at[j]], sem) for j in range(s)]
                for cp in copies:
                    cp.wait()

    return k(*sources, pos_t)


def _sc_gather_rows(table, pos_t):
    _, w = table.shape
    s, t = pos_t.shape
    n_cores, n_workers = _sc_workers()
    n_chunks = t // SC_CHUNK
    iters = -(-n_chunks // n_workers)
    mesh = plsc.VectorSubcoreMesh(core_axis_name="c", subcore_axis_name="s")

    @functools.partial(
        pl.kernel, mesh=mesh, out_type=jax.ShapeDtypeStruct((s, t, w), table.dtype),
        scratch_types=[pltpu.VMEM((SC_CHUNK, w), table.dtype), pltpu.VMEM((s, SC_CHUNK), I32),
                       pltpu.SemaphoreType.DMA],
        name="combine")
    def k(table_hbm, pos_hbm, out_hbm, rows_v, idx_v, sem):
        wid = lax.axis_index("s") * n_cores + lax.axis_index("c")

        @pl.loop(0, iters)
        def _(it):
            c = it * n_workers + wid

            @pl.when(c < n_chunks)
            def _():
                base = pl.multiple_of(c * SC_CHUNK, SC_CHUNK)
                pltpu.sync_copy(pos_hbm.at[:, pl.ds(base, SC_CHUNK)], idx_v)
                for j in range(s):
                    pltpu.async_copy(table_hbm.at[idx_v.at[j]], rows_v, sem).wait()
                    pltpu.sync_copy(rows_v, out_hbm.at[j, pl.ds(base, SC_CHUNK)])

    return k(table, pos_t)


def _experts_kernel(first_ref, ntile_ref, cnt_ref, xs_hbm, wg_ref, wu_ref, wd_ref, *rest):
    ys_hbm, wg16_ref, wu16_ref, wd16_ref, x_buf, y_buf, in_sem, out_sem = rest[-8:]
    e = pl.program_id(0)
    n_used = first_ref[N_EXPERTS - 1] + ntile_ref[N_EXPERTS - 1]
    first, n_mine, count = first_ref[e], ntile_ref[e], cnt_ref[e]

    def tile_rows(g):
        return pl.ds(pl.multiple_of(g * ROW_TILE, ROW_TILE), ROW_TILE)

    def load(g):
        slot = lax.rem(g, STREAM_DEPTH)
        return pltpu.make_async_copy(xs_hbm.at[tile_rows(g)], x_buf.at[slot], in_sem.at[slot])

    def store(g):
        slot = lax.rem(g, STREAM_DEPTH)
        return pltpu.make_async_copy(y_buf.at[slot], ys_hbm.at[tile_rows(g)], out_sem.at[slot])

    @pl.when(e == 0)
    def _():
        for g0 in range(STREAM_DEPTH - 1):
            @pl.when(g0 < n_used)
            def _():
                load(g0).start()

    wg16_ref[...] = wg_ref[0].astype(BF16)
    wu16_ref[...] = wu_ref[0].astype(BF16)
    wd16_ref[...] = wd_ref[0].astype(BF16)

    def tile(j, carry):
        g = first + j
        slot = lax.rem(g, STREAM_DEPTH)
        load(g).wait()

        @pl.when(g + (STREAM_DEPTH - 1) < n_used)
        def _():
            load(g + (STREAM_DEPTH - 1)).start()

        @pl.when(g >= STREAM_DEPTH)
        def _():
            store(g - STREAM_DEPTH).wait()

        words = x_buf[slot]
        row = lax.broadcasted_iota(I32, words.shape, 0)
        words = jnp.where(row < count - j * ROW_TILE, words, jnp.uint32(0))
        lo, hi = _unpack_rows(words)
        lo, hi = lo.astype(BF16), hi.astype(BF16)
        hg = _dot(lo, wg16_ref[0:HALF, :]) + _dot(hi, wg16_ref[HALF:, :])
        hu = _dot(lo, wu16_ref[0:HALF, :]) + _dot(hi, wu16_ref[HALF:, :])
        a = (_silu(hg) * hu).astype(BF16)
        y_buf[slot] = _pack_rows(_dot(a, wd16_ref[...]))
        store(g).start()
        return carry

    lax.fori_loop(0, n_mine, tile, 0)

    @pl.when(e == N_EXPERTS - 1)
    def _():
        for back in range(STREAM_DEPTH, 0, -1):
            @pl.when(n_used >= back)
            def _():
                store(n_used - back).wait()


def _experts(xs, first_tile, n_tile, count, w_eg, w_eu, w_ed, after=None):
    d = D_MODEL
    by_expert = lambda e, *_: (e, 0, 0)
    operands = [xs, w_eg, w_eu, w_ed] + ([] if after is None else [after])
    grid_spec = pltpu.PrefetchScalarGridSpec(
        num_scalar_prefetch=3,
        grid=(N_EXPERTS,),
        in_specs=[pl.BlockSpec(memory_space=pl.ANY),
                  pl.BlockSpec((1, d, EXPERT_DIM), by_expert),
                  pl.BlockSpec((1, d, EXPERT_DIM), by_expert),
                  pl.BlockSpec((1, EXPERT_DIM, d), by_expert)]
                 + [pl.BlockSpec(memory_space=pl.ANY)] * (after is not None),
        out_specs=pl.BlockSpec(memory_space=pl.ANY),
        scratch_shapes=[pltpu.VMEM((d, EXPERT_DIM), BF16), pltpu.VMEM((d, EXPERT_DIM), BF16),
                        pltpu.VMEM((EXPERT_DIM, d), BF16),
                        pltpu.VMEM((STREAM_DEPTH, ROW_TILE, HALF), U32),
                        pltpu.VMEM((STREAM_DEPTH, ROW_TILE, HALF), U32),
                        pltpu.SemaphoreType.DMA((STREAM_DEPTH,)), pltpu.SemaphoreType.DMA((STREAM_DEPTH,))])
    return pl.pallas_call(
        _experts_kernel,
        grid_spec=grid_spec,
        out_shape=jax.ShapeDtypeStruct(xs.shape, U32),
        compiler_params=pltpu.CompilerParams(dimension_semantics=("arbitrary",),
                                             vmem_limit_bytes=VMEM_LIMIT),
        name="experts",
    )(first_tile, n_tile, count, *operands)


def _final_kernel(z_ref, gatew_ref, h2_ref, x1_ref, mod_ref, normf_ref, wsg_ref, wsu_ref, wsd_ref, *rest):
    y_ref = rest[-1]
    lo, hi = _unpack_rows(h2_ref[...])
    h = jnp.concatenate([lo, hi], axis=-1).astype(BF16)
    a = _silu(_dot(h, wsg_ref[...])) * _dot(h, wsu_ref[...])
    acc = _dot(a.astype(BF16), wsd_ref[...])
    for s in range(TOP_K):
        lo, hi = _unpack_rows(z_ref[s])
        acc = acc + gatew_ref[:, s:s + 1] * jnp.concatenate([lo, hi], axis=-1)
    x2 = x1_ref[...] + _mod(mod_ref, 5) * acc
    y_ref[...] = _rms(x2) * normf_ref[...]


def _final(z, gate_w, h2, x1, mod, norm_f, w_sg16, w_su16, w_sd16, block_t, first_block, per_seq,
           seq0=0, out_rows=None, y_prev=None):
    t, d = x1.shape
    out_rows = t if out_rows is None else out_rows
    out_first = seq0 * per_seq
    tok = lambda i: (i, 0)
    if per_seq:
        mod_spec = pl.BlockSpec((1, 6, d), lambda i: (seq0 + i // per_seq, 0, 0))
    else:
        mod_spec = pl.BlockSpec((block_t, 6 * d), tok)
    operands = [z, gate_w, h2, x1, mod, norm_f, w_sg16, w_su16, w_sd16]
    in_specs = [pl.BlockSpec((TOP_K, block_t, HALF), lambda i: (0, first_block + i, 0)),
                pl.BlockSpec((block_t, LANES), tok),
                pl.BlockSpec((block_t, HALF), tok),
                pl.BlockSpec((block_t, d), tok),
                mod_spec,
                _full((1, d)),
                _full((d, EXPERT_DIM)), _full((d, EXPERT_DIM)), _full((EXPERT_DIM, d))]
    aliases = {}
    if y_prev is not None:
        aliases = {len(operands): 0}
        operands.append(y_prev)
        in_specs.append(pl.BlockSpec(memory_space=pl.ANY))
    return pl.pallas_call(
        _final_kernel,
        grid=(t // block_t,),
        in_specs=in_specs,
        out_specs=pl.BlockSpec((block_t, d), lambda i: (out_first + i, 0)),
        out_shape=jax.ShapeDtypeStruct((out_rows, d), F32),
        input_output_aliases=aliases,
        compiler_params=pltpu.CompilerParams(dimension_semantics=("arbitrary",),
                                             vmem_limit_bytes=VMEM_LIMIT),
        name="final",
    )(*operands)


def kernel(x_prompt, x_sample, c_prompt, c_sample, state_ret, state_pool, norm1, norm2, norm_f,
           w_ada, b_ada, w_in, w_out, w_pool, pool_scale, w_router, router_bias, w_exp_gate,
           w_exp_up, w_exp_down, w_sh_gate, w_sh_up, w_sh_down):
    b, l, d = x_prompt.shape
    n = x_sample.shape[0]
    past_len = 16384

    mod = _ada(jnp.concatenate([c_prompt, c_sample], axis=0), w_ada[0], b_ada[0])
    mod_p = mod[:b].reshape(b, 6, d)
    mod_s = mod[b:]

    w_in16 = w_in[0].astype(BF16)
    w_out16 = w_out[0].astype(BF16)
    wr_t = w_router[0].T
    bias_t = jnp.broadcast_to(router_bias[0][:, None], (N_EXPERTS, LANES))
    n1, n2, nf = norm1[0].reshape(1, d), norm2[0].reshape(1, d), norm_f.reshape(1, d)
    ps = pool_scale[0].reshape(1, POOL_WIDTH)
    shared = (w_sh_gate[0].astype(BF16), w_sh_up[0].astype(BF16), w_sh_down[0].astype(BF16))

    def routed(sources, experts, after=None):
        n_tiles = experts.shape[1] * TOP_K // ROW_TILE + N_EXPERTS
        pos_t, meta = _plan(experts)
        xs = _sc_scatter_rows(sources, pos_t, n_tiles * ROW_TILE)
        ys = _experts(xs, meta[:, 0], meta[:, 1], meta[:, 2], w_exp_gate[0], w_exp_up[0], w_exp_down[0],
                      after=after)
        return _sc_gather_rows(ys, pos_t), ys

    ba = b // 2
    bb = b - ba
    mix_args = (n1, w_in16, w_pool[0], ps, w_out16, n2, wr_t, bias_t)
    x1_a, h2_a, experts_a, gatew_a, ret_a, pool_a = _mix_prompt(x_prompt, mod_p, 0, ba, *mix_args)
    z_a, ys_a = routed((h2_a,), experts_a)
    x1_b, h2_b, experts_b, gatew_b, ret_b, pool_b = _mix_prompt(x_prompt, mod_p, ba, bb, *mix_args)
    x1_s, h2_s, experts_s, gatew_s, ret_s, pool_s = _mix_sample(
        x_sample.reshape(n, d), mod_s, state_ret[0], state_pool[0], float(past_len), *mix_args)
    z_b, _ = routed((h2_b, h2_s), jnp.concatenate([experts_b, experts_s], axis=1), after=ys_a)

    block_t = 256
    per_seq = l // block_t
    y_p = _final(z_a, gatew_a, h2_a, x1_a.reshape(ba * l, d), mod_p, nf, *shared,
                 block_t=block_t, first_block=0, per_seq=per_seq, seq0=0, out_rows=b * l)
    y_p = _final(z_b, gatew_b, h2_b, x1_b.reshape(bb * l, d), mod_p, nf, *shared,
                 block_t=block_t, first_block=0, per_seq=per_seq, seq0=ba, out_rows=b * l, y_prev=y_p)
    y_s = _final(z_b, gatew_s, h2_s, x1_s, mod_s, nf, *shared,
                 block_t=n, first_block=bb * l // n, per_seq=0)

    ret_p = jnp.concatenate([ret_a, ret_b], axis=0)
    pool_p = jnp.concatenate([pool_a, pool_b], axis=0)
    return (y_p.reshape(b, l, d), y_s.reshape(n, 1, d), ret_p[None], pool_p[None],
            ret_s[None], pool_s[None])
```

```python
import functools

import jax
import jax.numpy as jnp
import numpy as np
from jax import lax
from jax.experimental import pallas as pl
from jax.experimental.pallas import tpu as pltpu
from jax.experimental.pallas import tpu_sc as plsc

D_MODEL = 1024
RET_HEADS = 4
RET_QK_DIM = 64
RET_V_DIM = 128
RET_WIDTH = RET_HEADS * RET_V_DIM
QK_WIDTH = RET_HEADS * RET_QK_DIM
ROPE_BASE = 10000.0
POOL_WINDOWS = (2, 4, 8, 16)
POOL_WIDTH = 512
POOL_GROUP_DIM = 128
POOL_BUF = 15
IN_WIDTH = 2 * QK_WIDTH + 2 * RET_WIDTH + POOL_WIDTH
N_EXPERTS = 64
TOP_K = 8
N_EXPERT_GROUPS = 8
GROUP_SIZE = N_EXPERTS // N_EXPERT_GROUPS
TOP_GROUPS = 4
EXPERT_DIM = 256
ROUTE_SCALE = 2.5
EPS = 1e-6

LANES = 128
POOL_CARRY = 16
VMEM_LIMIT = 56 * 1024 * 1024
HALF = D_MODEL // 2
ROW_TILE = 256
SC_CHUNK = 128
STREAM_DEPTH = 4

BF16 = jnp.bfloat16
F32 = jnp.float32
U32 = jnp.uint32
I32 = jnp.int32


def _silu(x):
    return x * jax.nn.sigmoid(x)


def _dot(a, b):
    return jnp.dot(a, b, preferred_element_type=F32)


def _rms(x):
    return x * lax.rsqrt(jnp.mean(x * x, axis=-1, keepdims=True) + EPS)


def _mod(mod_ref, i):
    if len(mod_ref.shape) == 3:
        return mod_ref[0, i:i + 1, :]
    return mod_ref[:, i * D_MODEL:(i + 1) * D_MODEL]


def _split_bf16(x):
    hi = x.astype(BF16)
    lo = (x - hi.astype(F32)).astype(BF16)
    return hi, lo


def _pack_rows(x):
    lo = lax.bitcast_convert_type(x[:, :HALF].astype(BF16).astype(F32), U32)
    hi = lax.bitcast_convert_type(x[:, HALF:].astype(BF16).astype(F32), U32)
    return (hi & jnp.uint32(0xFFFF0000)) | (lo >> jnp.uint32(16))


def _unpack_rows(w):
    lo = lax.bitcast_convert_type(w << jnp.uint32(16), F32)
    hi = lax.bitcast_convert_type(w & jnp.uint32(0xFFFF0000), F32)
    return lo, hi


def _first_max_onehot(work, idx, n):
    m = jnp.max(work, axis=0, keepdims=True)
    first = jnp.min(jnp.where(work == m, idx, float(n)), axis=0, keepdims=True)
    return idx == first


def _route(h2, wr_t_ref, bias_t_ref):
    n = h2.shape[0]
    h_hi, h_lo = _split_bf16(h2)
    w_hi, w_lo = _split_bf16(wr_t_ref[...])
    nt = (((1,), (1,)), ((), ()))
    logits = (lax.dot_general(w_hi, h_hi, nt, preferred_element_type=F32)
              + lax.dot_general(w_hi, h_lo, nt, preferred_element_type=F32)
              + lax.dot_general(w_lo, h_hi, nt, preferred_element_type=F32))
    scores = jax.nn.sigmoid(logits)
    biased = scores + bias_t_ref[:, 0:1]
    b3 = biased.reshape(N_EXPERT_GROUPS, GROUP_SIZE, n)
    i3 = lax.broadcasted_iota(I32, b3.shape, 1).astype(F32)
    m1 = jnp.max(b3, axis=1, keepdims=True)
    first = jnp.min(jnp.where(b3 == m1, i3, float(GROUP_SIZE)), axis=1, keepdims=True)
    m2 = jnp.max(jnp.where(i3 == first, -jnp.inf, b3), axis=1, keepdims=True)
    gscore = (m1 + m2).reshape(N_EXPERT_GROUPS, n)
    gidx = lax.broadcasted_iota(I32, gscore.shape, 0).astype(F32)
    gsel = jnp.zeros(gscore.shape, F32)
    work = gscore
    for _ in range(TOP_GROUPS):
        hit = _first_max_onehot(work, gidx, N_EXPERT_GROUPS)
        gsel = jnp.where(hit, 1.0, gsel)
        work = jnp.where(hit, -jnp.inf, work)
    gsel3 = jnp.broadcast_to(gsel.reshape(N_EXPERT_GROUPS, 1, n), b3.shape)
    work = jnp.where(gsel3 > 0.0, b3, -jnp.inf).reshape(N_EXPERTS, n)
    eidx = lax.broadcasted_iota(I32, work.shape, 0).astype(F32)
    sel = jnp.zeros(work.shape, F32)
    for _ in range(TOP_K):
        hit = _first_max_onehot(work, eidx, N_EXPERTS)
        sel = jnp.where(hit, 1.0, sel)
        work = jnp.where(hit, -jnp.inf, work)
    picked = jnp.where(sel > 0.0, scores, 0.0)
    gates = picked / jnp.sum(picked, axis=0, keepdims=True) * ROUTE_SCALE
    below = (lax.broadcasted_iota(I32, (N_EXPERTS, N_EXPERTS), 1)
             < lax.broadcasted_iota(I32, (N_EXPERTS, N_EXPERTS), 0))
    slot = _dot(jnp.where(below, 1.0, 0.0).astype(BF16), sel.astype(BF16))
    e_rows, w_rows = [], []
    for s in range(TOP_K):
        here = jnp.where(slot == float(s), sel, 0.0)
        e_rows.append(jnp.sum(here * eidx, axis=0, keepdims=True))
        w_rows.append(jnp.sum(here * gates, axis=0, keepdims=True))
    experts = jnp.concatenate(e_rows, axis=0).astype(I32)
    w_t = jnp.concatenate(w_rows + [jnp.zeros((LANES - TOP_K, n), F32)], axis=0)
    return experts, w_t.T


def _group_norm_gate(o, g):
    parts = []
    for h in range(RET_HEADS):
        oh = o[:, h * RET_V_DIM:(h + 1) * RET_V_DIM]
        mu = jnp.mean(oh, axis=-1, keepdims=True)
        ctr = oh - mu
        var = jnp.mean(ctr * ctr, axis=-1, keepdims=True)
        parts.append(ctr * lax.rsqrt(var + EPS))
    return _silu(g) * jnp.concatenate(parts, axis=-1)


def _pool_project(pooled, w_pool_ref, pool_scale_ref):
    parts = [_dot(p.astype(BF16), w_pool_ref[gi].astype(BF16)) for gi, p in enumerate(pooled)]
    return jnp.concatenate(parts, axis=-1) * pool_scale_ref[...]


def _mix_tail(x, o_gated, p, mod_ref, w_out_ref, norm2_ref, wr_t_ref, bias_t_ref,
              x1_ref, h2_ref, experts_ref, gatew_ref):
    mix = jnp.concatenate([o_gated, p], axis=-1).astype(BF16)
    y = _dot(mix, w_out_ref[...])
    x1 = x + _mod(mod_ref, 2) * y
    h2 = _rms(x1) * norm2_ref[...] * (1.0 + _mod(mod_ref, 4)) + _mod(mod_ref, 3)
    x1_ref[...] = x1.reshape(x1_ref.shape)
    h2_ref[...] = _pack_rows(h2)
    experts, gate_w = _route(h2, wr_t_ref, bias_t_ref)
    experts_ref[...] = experts
    gatew_ref[...] = gate_w


def _ada_kernel(c_ref, w_ref, b_ref, o_ref):
    cs = _silu(c_ref[...]).astype(BF16)
    o_ref[...] = _dot(cs, w_ref[...].astype(BF16)) + b_ref[...]


def _ada(c_all, w_ada, b_ada, block_n=1536):
    n, d = c_all.shape
    width = w_ada.shape[1]
    return pl.pallas_call(
        _ada_kernel,
        grid=(width // block_n,),
        in_specs=[pl.BlockSpec((n, d), lambda j: (0, 0)),
                  pl.BlockSpec((d, block_n), lambda j: (0, j)),
                  pl.BlockSpec((1, block_n), lambda j: (0, j))],
        out_specs=pl.BlockSpec((n, block_n), lambda j: (0, j)),
        out_shape=jax.ShapeDtypeStruct((n, width), F32),
        compiler_params=pltpu.CompilerParams(vmem_limit_bytes=VMEM_LIMIT),
        name="ada",
    )(c_all, w_ada, b_ada.reshape(1, width))


def _mix_prompt_kernel(x_ref, mod_ref, norm1_ref, w_in_ref, cos_ref, sin_ref, dmat_ref, cross_ref,
                       tail_ref, cdec_ref, w_pool_ref, pool_scale_ref, w_out_ref, norm2_ref,
                       wr_t_ref, bias_t_ref,
                       x1_ref, h2_ref, experts_ref, gatew_ref, ret_ref, pool_ref,
                       state_ref, ext_ref, o_ref, *, block_l, chunk):
    li = pl.program_id(1)

    @pl.when(li == 0)
    def _():
        state_ref[...] = jnp.zeros_like(state_ref)
        ext_ref[0:POOL_CARRY, :] = jnp.zeros((POOL_CARRY, POOL_WIDTH), F32)

    x = x_ref[0]
    h = _rms(x) * norm1_ref[...] * (1.0 + _mod(mod_ref, 1)) + _mod(mod_ref, 0)
    proj = _dot(h.astype(BF16), w_in_ref[...])
    q = proj[:, 0:QK_WIDTH]
    k = proj[:, QK_WIDTH:2 * QK_WIDTH]
    v = proj[:, 2 * QK_WIDTH:2 * QK_WIDTH + RET_WIDTH]
    g = proj[:, 2 * QK_WIDTH + RET_WIDTH:2 * QK_WIDTH + 2 * RET_WIDTH]
    u = proj[:, 2 * QK_WIDTH + 2 * RET_WIDTH:]

    lane = lax.broadcasted_iota(I32, q.shape, 1)
    first_half = (lane % RET_QK_DIM) < (RET_QK_DIM // 2)
    cos_t = cos_ref[...]
    sin_t = sin_ref[...]

    def rot(t):
        partner = jnp.where(first_half, pltpu.roll(t, QK_WIDTH - RET_QK_DIM // 2, axis=1),
                            pltpu.roll(t, RET_QK_DIM // 2, axis=1))
        return t * cos_t + partner * sin_t

    q = rot(q)
    k = rot(k) * (RET_QK_DIM ** -0.5)
    k_t = k.T
    v16 = v.astype(BF16)
    head_of_lane = lax.broadcasted_iota(I32, (chunk, QK_WIDTH), 1) // RET_QK_DIM

    for c in range(block_l // chunk):
        rows = slice(c * chunk, (c + 1) * chunk)
        q_c = q[rows]
        kt_c = k_t[:, rows]
        kt16 = kt_c.astype(BF16)
        state16 = state_ref[...].astype(BF16)
        for hd in range(RET_HEADS):
            in_head = head_of_lane == hd
            q_h = jnp.where(in_head, q_c, 0.0).astype(BF16)
            v_h = v16[rows, hd * RET_V_DIM:(hd + 1) * RET_V_DIM]
            scores = _dot(q_h, kt16) * dmat_ref[hd]
            inner = _dot(scores.astype(BF16), v_h)
            cross = _dot(q_h, state16) * cross_ref[hd]
            o_ref[rows, hd * RET_V_DIM:(hd + 1) * RET_V_DIM] = inner + cross
            hrows = slice(hd * RET_QK_DIM, (hd + 1) * RET_QK_DIM)
            k_dec = (kt_c[hrows] * tail_ref[hd:hd + 1, :]).astype(BF16)
            state_ref[hrows, :] = state_ref[hrows, :] * cdec_ref[hd] + _dot(k_dec, v_h)

    o_gated = _group_norm_gate(o_ref[...], g)

    ext_ref[POOL_CARRY:POOL_CARRY + block_l, :] = u
    pos = (li * block_l + lax.broadcasted_iota(I32, (block_l, 1), 0)).astype(F32)
    pooled = []
    for gi, w in enumerate(POOL_WINDOWS):
        lanes = slice(gi * POOL_GROUP_DIM, (gi + 1) * POOL_GROUP_DIM)
        acc = ext_ref[POOL_CARRY:POOL_CARRY + block_l, lanes]
        for j in range(1, w):
            acc = acc + ext_ref[POOL_CARRY - j:POOL_CARRY - j + block_l, lanes]
        cnt = jnp.minimum(pos + 1.0, float(w))
        pooled.append(acc / cnt - u[:, lanes])
    p = _pool_project(pooled, w_pool_ref, pool_scale_ref)
    ext_ref[0:POOL_CARRY, :] = ext_ref[block_l:block_l + POOL_CARRY, :]

    _mix_tail(x, o_gated, p, mod_ref, w_out_ref, norm2_ref, wr_t_ref, bias_t_ref,
              x1_ref, h2_ref, experts_ref, gatew_ref)

    @pl.when(li == pl.num_programs(1) - 1)
    def _():
        ret_ref[...] = state_ref[...].reshape(ret_ref.shape)
        pool_ref[...] = ext_ref[1:1 + POOL_BUF, :].reshape(pool_ref.shape)


def _decay_tables(chunk):
    lg = jnp.log(1.0 - 2.0 ** (-5.0 - jnp.arange(RET_HEADS, dtype=F32)))
    idx = jnp.arange(chunk, dtype=F32)
    diff = idx[:, None] - idx[None, :]
    causal = diff >= 0
    dmat = jnp.where(causal[None], jnp.exp(lg[:, None, None] * jnp.where(causal, diff, 0.0)[None]), 0.0)
    cross = jnp.exp(lg[:, None] * (idx[None, :] + 1.0))
    cross = jnp.broadcast_to(cross[:, :, None], (RET_HEADS, chunk, RET_V_DIM))
    tail = jnp.exp(lg[:, None] * (chunk - 1.0 - idx)[None, :])
    cdec = jnp.broadcast_to(jnp.exp(lg * chunk)[:, None, None], (RET_HEADS, RET_QK_DIM, RET_V_DIM))
    return dmat, cross, tail, cdec


def _rotary_tables(pos):
    half = RET_QK_DIM // 2
    freqs = ROPE_BASE ** (-jnp.arange(half, dtype=F32) / half)
    ang = pos[:, None] * freqs[None, :]
    cos, sin = jnp.cos(ang), jnp.sin(ang)
    cos_t = jnp.tile(jnp.concatenate([cos, cos], axis=-1), (1, RET_HEADS))
    sin_t = jnp.tile(jnp.concatenate([-sin, sin], axis=-1), (1, RET_HEADS))
    return cos_t, sin_t


def _full(shape):
    return pl.BlockSpec(shape, lambda *_: (0,) * len(shape))


def _mix_prompt(x, mod, b0, b, norm1, w_in16, w_pool, pool_scale, w_out16, norm2, wr_t, bias_t,
                block_l=512, chunk=256):
    _, l, d = x.shape
    n_tokens = b * l
    nl = l // block_l
    cos_t, sin_t = _rotary_tables(jnp.arange(l, dtype=F32))
    dmat, cross, tail, cdec = _decay_tables(chunk)
    kernel = functools.partial(_mix_prompt_kernel, block_l=block_l, chunk=chunk)
    tok = lambda bi, li: (bi, li, 0)
    flat = lambda bi, li: (bi * nl + li, 0)
    return pl.pallas_call(
        kernel,
        grid=(b, nl),
        in_specs=[pl.BlockSpec((1, block_l, d), lambda bi, li: (b0 + bi, li, 0)),
                  pl.BlockSpec((1, 6, d), lambda bi, li: (b0 + bi, 0, 0)),
                  _full((1, d)),
                  _full((d, IN_WIDTH)),
                  pl.BlockSpec((block_l, QK_WIDTH), lambda bi, li: (li, 0)),
                  pl.BlockSpec((block_l, QK_WIDTH), lambda bi, li: (li, 0)),
                  _full(dmat.shape), _full(cross.shape), _full(tail.shape), _full(cdec.shape),
                  _full(w_pool.shape), _full((1, POOL_WIDTH)), _full((d, d)), _full((1, d)),
                  _full(wr_t.shape), _full(bias_t.shape)],
        out_specs=[pl.BlockSpec((1, block_l, d), tok),
                   pl.BlockSpec((block_l, HALF), flat),
                   pl.BlockSpec((TOP_K, block_l), lambda bi, li: (0, bi * nl + li)),
                   pl.BlockSpec((block_l, LANES), flat),
                   pl.BlockSpec((1, RET_HEADS, RET_QK_DIM, RET_V_DIM), lambda bi, li: (bi, 0, 0, 0)),
                   pl.BlockSpec((1, POOL_BUF, POOL_WIDTH), lambda bi, li: (bi, 0, 0))],
        out_shape=[jax.ShapeDtypeStruct((b, l, d), F32),
                   jax.ShapeDtypeStruct((n_tokens, HALF), U32),
                   jax.ShapeDtypeStruct((TOP_K, n_tokens), I32),
                   jax.ShapeDtypeStruct((b * l, LANES), F32),
                   jax.ShapeDtypeStruct((b, RET_HEADS, RET_QK_DIM, RET_V_DIM), F32),
                   jax.ShapeDtypeStruct((b, POOL_BUF, POOL_WIDTH), F32)],
        scratch_shapes=[pltpu.VMEM((QK_WIDTH, RET_V_DIM), F32),
                        pltpu.VMEM((POOL_CARRY + block_l, POOL_WIDTH), F32),
                        pltpu.VMEM((block_l, RET_WIDTH), F32)],
        compiler_params=pltpu.CompilerParams(dimension_semantics=("arbitrary", "arbitrary"),
                                             vmem_limit_bytes=VMEM_LIMIT),
        name="mix_prompt",
    )(x, mod, norm1, w_in16, cos_t, sin_t, dmat, cross, tail, cdec, w_pool, pool_scale,
      w_out16, norm2, wr_t, bias_t)


def _mix_sample_front_kernel(x_ref, mod_ref, norm1_ref, w_in_ref, cos_ref, sin_ref,
                             qt_ref, kt_ref, v_ref, g_ref, u_ref):
    x = x_ref[...]
    h = _rms(x) * norm1_ref[...] * (1.0 + _mod(mod_ref, 1)) + _mod(mod_ref, 0)
    proj = _dot(h.astype(BF16), w_in_ref[...])
    half = RET_QK_DIM // 2
    cos_c = cos_ref[...]
    sin_c = sin_ref[...]

    def rot_t(t):
        parts = []
        for hd in range(RET_HEADS):
            t1 = t[hd * RET_QK_DIM:hd * RET_QK_DIM + half]
            t2 = t[hd * RET_QK_DIM + half:(hd + 1) * RET_QK_DIM]
            parts += [t1 * cos_c - t2 * sin_c, t1 * sin_c + t2 * cos_c]
        return jnp.concatenate(parts, axis=0)

    qt_ref[...] = rot_t(proj[:, 0:QK_WIDTH].T)
    kt_ref[...] = rot_t(proj[:, QK_WIDTH:2 * QK_WIDTH].T) * (RET_QK_DIM ** -0.5)
    v_ref[...] = proj[:, 2 * QK_WIDTH:2 * QK_WIDTH + RET_WIDTH]
    g_ref[...] = proj[:, 2 * QK_WIDTH + RET_WIDTH:2 * QK_WIDTH + 2 * RET_WIDTH]
    u_ref[...] = proj[:, 2 * QK_WIDTH + 2 * RET_WIDTH:]


def _ret_step_kernel(qt_ref, kt_ref, v_ref, s0_ref, o_ref, s1_ref, *, block_b, decays):
    i = pl.program_id(0)
    lane = lax.broadcasted_iota(I32, qt_ref.shape, 1)
    for j in range(block_b):
        bi = i * block_b + j
        here = lane == bi
        q_col = jnp.sum(jnp.where(here, qt_ref[...], 0.0), axis=1, keepdims=True)
        k_col = jnp.sum(jnp.where(here, kt_ref[...], 0.0), axis=1, keepdims=True)
        v_row = v_ref[pl.ds(bi, 1), :]
        outs = []
        for hd in range(RET_HEADS):
            hrows = slice(hd * RET_QK_DIM, (hd + 1) * RET_QK_DIM)
            s1 = decays[hd] * s0_ref[j, hd] + k_col[hrows] * v_row[:, hd * RET_V_DIM:(hd + 1) * RET_V_DIM]
            s1_ref[j, hd] = s1
            outs.append(jnp.sum(q_col[hrows] * s1, axis=0, keepdims=True))
        o_ref[pl.ds(bi, 1), :] = jnp.concatenate(outs, axis=-1)


def _mix_sample_back_kernel(x_ref, mod_ref, o_ref, g_ref, u_ref, buf_ref, w_pool_ref, pool_scale_ref,
                            w_out_ref, norm2_ref, wr_t_ref, bias_t_ref,
                            x1_ref, h2_ref, experts_ref, gatew_ref, pool_ref):
    o_gated = _group_norm_gate(o_ref[...], g_ref[...])
    u = u_ref[...]
    pooled = []
    for gi, w in enumerate(POOL_WINDOWS):
        lanes = slice(gi * POOL_GROUP_DIM, (gi + 1) * POOL_GROUP_DIM)
        acc = u[:, lanes]
        for j in range(1, w):
            acc = acc + buf_ref[:, POOL_BUF - j, lanes]
        pooled.append(acc / float(w) - u[:, lanes])
    p = _pool_project(pooled, w_pool_ref, pool_scale_ref)
    pool_ref[:, 0:POOL_BUF - 1, :] = buf_ref[:, 1:POOL_BUF, :]
    pool_ref[:, POOL_BUF - 1, :] = u
    _mix_tail(x_ref[...], o_gated, p, mod_ref, w_out_ref, norm2_ref, wr_t_ref, bias_t_ref,
              x1_ref, h2_ref, experts_ref, gatew_ref)


def _mix_sample(x, mod, state_ret, state_pool, start, norm1, w_in16, w_pool,
                pool_scale, w_out16, norm2, wr_t, bias_t, block_b=8):
    n, d = x.shape
    half = RET_QK_DIM // 2
    freqs = ROPE_BASE ** (-jnp.arange(half, dtype=F32) / half)
    ang = jnp.full((1,), start, F32)[:, None] * freqs[None, :]
    cos_c = jnp.broadcast_to(jnp.cos(ang).T, (half, n))
    sin_c = jnp.broadcast_to(jnp.sin(ang).T, (half, n))
    params = pltpu.CompilerParams(vmem_limit_bytes=VMEM_LIMIT)
    qt, kt, v, g, u = pl.pallas_call(
        _mix_sample_front_kernel,
        out_shape=[jax.ShapeDtypeStruct((QK_WIDTH, n), F32), jax.ShapeDtypeStruct((QK_WIDTH, n), F32),
                   jax.ShapeDtypeStruct((n, RET_WIDTH), F32), jax.ShapeDtypeStruct((n, RET_WIDTH), F32),
                   jax.ShapeDtypeStruct((n, POOL_WIDTH), F32)],
        compiler_params=params,
        name="mix_sample_front",
    )(x, mod, norm1, w_in16, cos_c, sin_c)

    lg = np.log(1.0 - 2.0 ** (-5.0 - np.arange(RET_HEADS, dtype=np.float32)), dtype=np.float32)
    decays = tuple(float(np.exp(lg[h])) for h in range(RET_HEADS))
    state_block = (block_b, RET_HEADS, RET_QK_DIM, RET_V_DIM)
    o, s1 = pl.pallas_call(
        functools.partial(_ret_step_kernel, block_b=block_b, decays=decays),
        grid=(n // block_b,),
        in_specs=[_full((QK_WIDTH, n)), _full((QK_WIDTH, n)), _full((n, RET_WIDTH)),
                  pl.BlockSpec(state_block, lambda i: (i, 0, 0, 0))],
        out_specs=[_full((n, RET_WIDTH)), pl.BlockSpec(state_block, lambda i: (i, 0, 0, 0))],
        out_shape=[jax.ShapeDtypeStruct((n, RET_WIDTH), F32),
                   jax.ShapeDtypeStruct(state_ret.shape, F32)],
        compiler_params=pltpu.CompilerParams(dimension_semantics=("arbitrary",),
                                             vmem_limit_bytes=VMEM_LIMIT),
        name="ret_step",
    )(qt, kt, v, state_ret)

    x1, h2, experts, gate_w, pool = pl.pallas_call(
        _mix_sample_back_kernel,
        out_shape=[jax.ShapeDtypeStruct((n, d), F32),
                   jax.ShapeDtypeStruct((n, HALF), U32),
                   jax.ShapeDtypeStruct((TOP_K, n), I32),
                   jax.ShapeDtypeStruct((n, LANES), F32),
                   jax.ShapeDtypeStruct(state_pool.shape, F32)],
        compiler_params=params,
        name="mix_sample_back",
    )(x, mod, o, g, u, state_pool, w_pool, pool_scale, w_out16, norm2, wr_t, bias_t)
    return x1, h2, experts, gate_w, s1, pool


def _plan_kernel(experts_ref, pos_ref, meta_ref, cnt_ref, carry_ref, off_ref, *, block_t):
    phase = pl.program_id(0)
    j = pl.program_id(1)
    e_blk = experts_ref[...]
    eidx = lax.broadcasted_iota(I32, (N_EXPERTS, block_t), 0)
    member = jnp.zeros((N_EXPERTS, block_t), F32)
    for s in range(TOP_K):
        member = member + jnp.where(eidx == e_blk[s:s + 1, :], 1.0, 0.0)
    per_expert = jnp.broadcast_to(jnp.sum(member, axis=1, keepdims=True), (N_EXPERTS, LANES))

    @pl.when((phase == 0) & (j == 0))
    def _():
        cnt_ref[...] = jnp.zeros_like(cnt_ref)

    @pl.when(phase == 0)
    def _():
        cnt_ref[...] += per_expert

    @pl.when((phase == 0) & (j == pl.num_programs(1) - 1))
    def _():
        cnt = cnt_ref[...]
        n_tile = jnp.floor((cnt + (ROW_TILE - 1.0)) * (1.0 / ROW_TILE))
        upto = (lax.broadcasted_iota(I32, (N_EXPERTS, N_EXPERTS), 1)
                <= lax.broadcasted_iota(I32, (N_EXPERTS, N_EXPERTS), 0))
        tile_end = _dot(jnp.where(upto, 1.0, 0.0).astype(BF16), n_tile.astype(BF16))
        tile_start = tile_end - n_tile
        off_ref[...] = tile_start * ROW_TILE
        carry_ref[...] = jnp.zeros_like(carry_ref)
        lane = lax.broadcasted_iota(I32, cnt.shape, 1)
        meta_ref[...] = jnp.where(lane == 0, tile_start, jnp.where(lane == 1, n_tile, cnt)).astype(I32)

    @pl.when(phase == 1)
    def _():
        before = (lax.broadcasted_iota(I32, (block_t, block_t), 0)
                  < lax.broadcasted_iota(I32, (block_t, block_t), 1))
        rank = _dot(member.astype(BF16), jnp.where(before, 1.0, 0.0).astype(BF16))
        row = off_ref[:, 0:1] + carry_ref[:, 0:1] + rank
        carry_ref[...] += per_expert
        out = [jnp.sum(jnp.where(eidx == e_blk[s:s + 1, :], row, 0.0), axis=0, keepdims=True)
               for s in range(TOP_K)]
        pos_ref[...] = jnp.concatenate(out, axis=0).astype(I32)


def _plan(experts_all, max_block=1024):
    n_tokens = experts_all.shape[1]
    block_t = max(k for k in range(LANES, max_block + 1, LANES) if n_tokens % k == 0)
    nb = n_tokens // block_t
    return pl.pallas_call(
        functools.partial(_plan_kernel, block_t=block_t),
        grid=(2, nb),
        in_specs=[pl.BlockSpec((TOP_K, block_t), lambda ph, j: (0, j))],
        out_specs=[pl.BlockSpec((TOP_K, block_t), lambda ph, j: (0, j * ph)),
                   _full((N_EXPERTS, LANES))],
        out_shape=[jax.ShapeDtypeStruct((TOP_K, n_tokens), I32),
                   jax.ShapeDtypeStruct((N_EXPERTS, LANES), I32)],
        scratch_shapes=[pltpu.VMEM((N_EXPERTS, LANES), F32)] * 3,
        compiler_params=pltpu.CompilerParams(dimension_semantics=("arbitrary", "arbitrary"),
                                             vmem_limit_bytes=VMEM_LIMIT),
        name="plan",
    )(experts_all)


def _sc_workers():
    info = plsc.get_sparse_core_info()
    return info.num_cores, info.num_cores * info.num_subcores


def _sc_scatter_rows(sources, pos_t, n_out):
    w = sources[0].shape[1]
    s = pos_t.shape[0]
    n_cores, n_workers = _sc_workers()
    bounds = np.cumsum([0] + [src.shape[0] // SC_CHUNK for src in sources])
    n_chunks = int(bounds[-1])
    iters = -(-n_chunks // n_workers)
    mesh = plsc.VectorSubcoreMesh(core_axis_name="c", subcore_axis_name="s")

    @functools.partial(
        pl.kernel, mesh=mesh, out_type=jax.ShapeDtypeStruct((n_out, w), sources[0].dtype),
        scratch_types=[pltpu.VMEM((SC_CHUNK, w), sources[0].dtype), pltpu.VMEM((s, SC_CHUNK), I32),
                       pltpu.SemaphoreType.DMA],
        name="dispatch")
    def k(*refs):
        src_hbm, (pos_hbm, out_hbm, rows_v, idx_v, sem) = refs[:len(sources)], refs[len(sources):]
        wid = lax.axis_index("s") * n_cores + lax.axis_index("c")

        @pl.loop(0, iters)
        def _(it):
            c = it * n_workers + wid
            for src, lo, hi in zip(src_hbm, bounds[:-1], bounds[1:]):
                @pl.when((c >= int(lo)) & (c < int(hi)))
                def _():
                    base = pl.multiple_of((c - int(lo)) * SC_CHUNK, SC_CHUNK)
                    pltpu.sync_copy(src.at[pl.ds(base, SC_CHUNK)], rows_v)

            @pl.when(c < n_chunks)
            def _():
                base = pl.multiple_of(c * SC_CHUNK, SC_CHUNK)
                pltpu.sync_copy(pos_hbm.at[:, pl.ds(base, SC_CHUNK)], idx_v)
                copies = [pltpu.async_copy(rows_v, out_hbm.at[idx_v.at[j]], sem) for j in range(s)]
                for cp in copies:
                    cp.wait()

    return k(*sources, pos_t)


def _sc_gather_rows(table, pos_t):
    _, w = table.shape
    s, t = pos_t.shape
    n_cores, n_workers = _sc_workers()
    n_chunks = t // SC_CHUNK
    iters = -(-n_chunks // n_workers)
    mesh = plsc.VectorSubcoreMesh(core_axis_name="c", subcore_axis_name="s")

    @functools.partial(
        pl.kernel, mesh=mesh, out_type=jax.ShapeDtypeStruct((s, t, w), table.dtype),
        scratch_types=[pltpu.VMEM((SC_CHUNK, w), table.dtype), pltpu.VMEM((s, SC_CHUNK), I32),
                       pltpu.SemaphoreType.DMA],
        name="combine")
    def k(table_hbm, pos_hbm, out_hbm, rows_v, idx_v, sem):
        wid = lax.axis_index("s") * n_cores + lax.axis_index("c")

        @pl.loop(0, iters)
        def _(it):
            c = it * n_workers + wid

            @pl.when(c < n_chunks)
            def _():
                base = pl.multiple_of(c * SC_CHUNK, SC_CHUNK)
                pltpu.sync_copy(pos_hbm.at[:, pl.ds(base, SC_CHUNK)], idx_v)
                for j in range(s):
                    pltpu.async_copy(table_hbm.at[idx_v.at[j]], rows_v, sem).wait()
                    pltpu.sync_copy(rows_v, out_hbm.at[j, pl.ds(base, SC_CHUNK)])

    return k(table, pos_t)


def _experts_kernel(first_ref, ntile_ref, cnt_ref, xs_hbm, wg_ref, wu_ref, wd_ref, ys_hbm,
                    wg16_ref, wu16_ref, wd16_ref, x_buf, y_buf, in_sem, out_sem):
    e = pl.program_id(0)
    n_used = first_ref[N_EXPERTS - 1] + ntile_ref[N_EXPERTS - 1]
    first, n_mine, count = first_ref[e], ntile_ref[e], cnt_ref[e]

    def tile_rows(g):
        return pl.ds(pl.multiple_of(g * ROW_TILE, ROW_TILE), ROW_TILE)

    def load(g):
        slot = lax.rem(g, STREAM_DEPTH)
        return pltpu.make_async_copy(xs_hbm.at[tile_rows(g)], x_buf.at[slot], in_sem.at[slot])

    def store(g):
        slot = lax.rem(g, STREAM_DEPTH)
        return pltpu.make_async_copy(y_buf.at[slot], ys_hbm.at[tile_rows(g)], out_sem.at[slot])

    @pl.when(e == 0)
    def _():
        for g0 in range(STREAM_DEPTH - 1):
            @pl.when(g0 < n_used)
            def _():
                load(g0).start()

    wg16_ref[...] = wg_ref[0].astype(BF16)
    wu16_ref[...] = wu_ref[0].astype(BF16)
    wd16_ref[...] = wd_ref[0].astype(BF16)

    def tile(j, carry):
        g = first + j
        slot = lax.rem(g, STREAM_DEPTH)
        load(g).wait()

        @pl.when(g + (STREAM_DEPTH - 1) < n_used)
        def _():
            load(g + (STREAM_DEPTH - 1)).start()

        @pl.when(g >= STREAM_DEPTH)
        def _():
            store(g - STREAM_DEPTH).wait()

        words = x_buf[slot]
        row = lax.broadcasted_iota(I32, words.shape, 0)
        words = jnp.where(row < count - j * ROW_TILE, words, jnp.uint32(0))
        lo, hi = _unpack_rows(words)
        lo, hi = lo.astype(BF16), hi.astype(BF16)
        hg = _dot(lo, wg16_ref[0:HALF, :]) + _dot(hi, wg16_ref[HALF:, :])
        hu = _dot(lo, wu16_ref[0:HALF, :]) + _dot(hi, wu16_ref[HALF:, :])
        a = (_silu(hg) * hu).astype(BF16)
        y_buf[slot] = _pack_rows(_dot(a, wd16_ref[...]))
        store(g).start()
        return carry

    lax.fori_loop(0, n_mine, tile, 0)

    @pl.when(e == N_EXPERTS - 1)
    def _():
        for back in range(STREAM_DEPTH, 0, -1):
            @pl.when(n_used >= back)
            def _():
                store(n_used - back).wait()


def _experts(xs, first_tile, n_tile, count, w_eg, w_eu, w_ed):
    d = D_MODEL
    by_expert = lambda e, *_: (e, 0, 0)
    grid_spec = pltpu.PrefetchScalarGridSpec(
        num_scalar_prefetch=3,
        grid=(N_EXPERTS,),
        in_specs=[pl.BlockSpec(memory_space=pl.ANY),
                  pl.BlockSpec((1, d, EXPERT_DIM), by_expert),
                  pl.BlockSpec((1, d, EXPERT_DIM), by_expert),
                  pl.BlockSpec((1, EXPERT_DIM, d), by_expert)],
        out_specs=pl.BlockSpec(memory_space=pl.ANY),
        scratch_shapes=[pltpu.VMEM((d, EXPERT_DIM), BF16), pltpu.VMEM((d, EXPERT_DIM), BF16),
                        pltpu.VMEM((EXPERT_DIM, d), BF16),
                        pltpu.VMEM((STREAM_DEPTH, ROW_TILE, HALF), U32),
                        pltpu.VMEM((STREAM_DEPTH, ROW_TILE, HALF), U32),
                        pltpu.SemaphoreType.DMA((STREAM_DEPTH,)), pltpu.SemaphoreType.DMA((STREAM_DEPTH,))])
    return pl.pallas_call(
        _experts_kernel,
        grid_spec=grid_spec,
        out_shape=jax.ShapeDtypeStruct(xs.shape, U32),
        compiler_params=pltpu.CompilerParams(dimension_semantics=("arbitrary",),
                                             vmem_limit_bytes=VMEM_LIMIT),
        name="experts",
    )(first_tile, n_tile, count, xs, w_eg, w_eu, w_ed)


def _final_kernel(z_ref, gatew_ref, h2_ref, x1_ref, mod_ref, normf_ref, wsg_ref, wsu_ref, wsd_ref, *rest):
    y_ref = rest[-1]
    lo, hi = _unpack_rows(h2_ref[...])
    h = jnp.concatenate([lo, hi], axis=-1).astype(BF16)
    a = _silu(_dot(h, wsg_ref[...])) * _dot(h, wsu_ref[...])
    acc = _dot(a.astype(BF16), wsd_ref[...])
    for s in range(TOP_K):
        lo, hi = _unpack_rows(z_ref[s])
        acc = acc + gatew_ref[:, s:s + 1] * jnp.concatenate([lo, hi], axis=-1)
    x2 = x1_ref[...] + _mod(mod_ref, 5) * acc
    y_ref[...] = _rms(x2) * normf_ref[...]


def _final(z, gate_w, h2, x1, mod, norm_f, w_sg16, w_su16, w_sd16, block_t, first_block, per_seq,
           seq0=0, out_rows=None, y_prev=None):
    t, d = x1.shape
    out_rows = t if out_rows is None else out_rows
    out_first = seq0 * per_seq
    tok = lambda i: (i, 0)
    if per_seq:
        mod_spec = pl.BlockSpec((1, 6, d), lambda i: (seq0 + i // per_seq, 0, 0))
    else:
        mod_spec = pl.BlockSpec((block_t, 6 * d), tok)
    operands = [z, gate_w, h2, x1, mod, norm_f, w_sg16, w_su16, w_sd16]
    in_specs = [pl.BlockSpec((TOP_K, block_t, HALF), lambda i: (0, first_block + i, 0)),
                pl.BlockSpec((block_t, LANES), tok),
                pl.BlockSpec((block_t, HALF), tok),
                pl.BlockSpec((block_t, d), tok),
                mod_spec,
                _full((1, d)),
                _full((d, EXPERT_DIM)), _full((d, EXPERT_DIM)), _full((EXPERT_DIM, d))]
    aliases = {}
    if y_prev is not None:
        aliases = {len(operands): 0}
        operands.append(y_prev)
        in_specs.append(pl.BlockSpec(memory_space=pl.ANY))
    return pl.pallas_call(
        _final_kernel,
        grid=(t // block_t,),
        in_specs=in_specs,
        out_specs=pl.BlockSpec((block_t, d), lambda i: (out_first + i, 0)),
        out_shape=jax.ShapeDtypeStruct((out_rows, d), F32),
        input_output_aliases=aliases,
        compiler_params=pltpu.CompilerParams(dimension_semantics=("arbitrary",),
                                             vmem_limit_bytes=VMEM_LIMIT),
        name="final",
    )(*operands)


def kernel(x_prompt, x_sample, c_prompt, c_sample, state_ret, state_pool, norm1, norm2, norm_f,
           w_ada, b_ada, w_in, w_out, w_pool, pool_scale, w_router, router_bias, w_exp_gate,
           w_exp_up, w_exp_down, w_sh_gate, w_sh_up, w_sh_down):
    b, l, d = x_prompt.shape
    n = x_sample.shape[0]
    past_len = 16384

    mod = _ada(jnp.concatenate([c_prompt, c_sample], axis=0), w_ada[0], b_ada[0])
    mod_p = mod[:b].reshape(b, 6, d)
    mod_s = mod[b:]

    w_in16 = w_in[0].astype(BF16)
    w_out16 = w_out[0].astype(BF16)
    wr_t = w_router[0].T
    bias_t = jnp.broadcast_to(router_bias[0][:, None], (N_EXPERTS, LANES))
    n1, n2, nf = norm1[0].reshape(1, d), norm2[0].reshape(1, d), norm_f.reshape(1, d)
    ps = pool_scale[0].reshape(1, POOL_WIDTH)
    shared = (w_sh_gate[0].astype(BF16), w_sh_up[0].astype(BF16), w_sh_down[0].astype(BF16))

    def routed(sources, experts):
        n_tiles = experts.shape[1] * TOP_K // ROW_TILE + N_EXPERTS
        pos_t, meta = _plan(experts)
        xs = _sc_scatter_rows(sources, pos_t, n_tiles * ROW_TILE)
        ys = _experts(xs, meta[:, 0], meta[:, 1], meta[:, 2], w_exp_gate[0], w_exp_up[0], w_exp_down[0])
        return _sc_gather_rows(ys, pos_t)

    ba = b // 2
    bb = b - ba
    mix_args = (n1, w_in16, w_pool[0], ps, w_out16, n2, wr_t, bias_t)
    x1_a, h2_a, experts_a, gatew_a, ret_a, pool_a = _mix_prompt(x_prompt, mod_p, 0, ba, *mix_args)
    z_a = routed((h2_a,), experts_a)
    x1_b, h2_b, experts_b, gatew_b, ret_b, pool_b = _mix_prompt(x_prompt, mod_p, ba, bb, *mix_args)
    x1_s, h2_s, experts_s, gatew_s, ret_s, pool_s = _mix_sample(
        x_sample.reshape(n, d), mod_s, state_ret[0], state_pool[0], float(past_len), *mix_args)
    z_b = routed((h2_b, h2_s), jnp.concatenate([experts_b, experts_s], axis=1))

    block_t = 256
    per_seq = l // block_t
    y_s = _final(z_b, gatew_s, h2_s, x1_s, mod_s, nf, *shared,
                 block_t=n, first_block=bb * l // n, per_seq=0)
    y_p = _final(z_b, gatew_b, h2_b, x1_b.reshape(bb * l, d), mod_p, nf, *shared,
                 block_t=block_t, first_block=0, per_seq=per_seq, seq0=ba, out_rows=b * l)
    y_p = _final(z_a, gatew_a, h2_a, x1_a.reshape(ba * l, d), mod_p, nf, *shared,
                 block_t=block_t, first_block=0, per_seq=per_seq, seq0=0, out_rows=b * l, y_prev=y_p)

    ret_p = jnp.concatenate([ret_a, ret_b], axis=0)
    pool_p = jnp.concatenate([pool_a, pool_b], axis=0)
    return (y_p.reshape(b, l, d), y_s.reshape(n, 1, d), ret_p[None], pool_p[None],
            ret_s[None], pool_s[None])
```

```python
import functools

import jax
import jax.numpy as jnp
import numpy as np
from jax import lax
from jax.experimental import pallas as pl
from jax.experimental.pallas import tpu as pltpu
from jax.experimental.pallas import tpu_sc as plsc

D_MODEL = 1024
RET_HEADS = 4
RET_QK_DIM = 64
RET_V_DIM = 128
RET_WIDTH = RET_HEADS * RET_V_DIM
QK_WIDTH = RET_HEADS * RET_QK_DIM
ROPE_BASE = 10000.0
POOL_WINDOWS = (2, 4, 8, 16)
POOL_WIDTH = 512
POOL_GROUP_DIM = 128
POOL_BUF = 15
IN_WIDTH = 2 * QK_WIDTH + 2 * RET_WIDTH + POOL_WIDTH
N_EXPERTS = 64
TOP_K = 8
N_EXPERT_GROUPS = 8
GROUP_SIZE = N_EXPERTS // N_EXPERT_GROUPS
TOP_GROUPS = 4
EXPERT_DIM = 256
ROUTE_SCALE = 2.5
EPS = 1e-6

LANES = 128
POOL_CARRY = 16
VMEM_LIMIT = 56 * 1024 * 1024
HALF = D_MODEL // 2
ROW_TILE = 256
SC_CHUNK = 128
STREAM_DEPTH = 8

BF16 = jnp.bfloat16
F32 = jnp.float32
U32 = jnp.uint32
I32 = jnp.int32


def _silu(x):
    return x * jax.nn.sigmoid(x)


def _dot(a, b):
    return jnp.dot(a, b, preferred_element_type=F32)


def _rms(x):
    return x * lax.rsqrt(jnp.mean(x * x, axis=-1, keepdims=True) + EPS)


def _mod(mod_ref, i):
    if len(mod_ref.shape) == 3:
        return mod_ref[0, i:i + 1, :]
    return mod_ref[:, i * D_MODEL:(i + 1) * D_MODEL]


def _split_bf16(x):
    hi = x.astype(BF16)
    lo = (x - hi.astype(F32)).astype(BF16)
    return hi, lo


def _pack_rows(x):
    lo = lax.bitcast_convert_type(x[:, :HALF].astype(BF16).astype(F32), U32)
    hi = lax.bitcast_convert_type(x[:, HALF:].astype(BF16).astype(F32), U32)
    return (hi & jnp.uint32(0xFFFF0000)) | (lo >> jnp.uint32(16))


def _unpack_rows(w):
    lo = lax.bitcast_convert_type(w << jnp.uint32(16), F32)
    hi = lax.bitcast_convert_type(w & jnp.uint32(0xFFFF0000), F32)
    return lo, hi


def _first_max_onehot(work, idx, n):
    m = jnp.max(work, axis=0, keepdims=True)
    first = jnp.min(jnp.where(work == m, idx, float(n)), axis=0, keepdims=True)
    return idx == first


def _route(h2, wr_t_ref, bias_t_ref):
    n = h2.shape[0]
    h_hi, h_lo = _split_bf16(h2)
    w_hi, w_lo = _split_bf16(wr_t_ref[...])
    nt = (((1,), (1,)), ((), ()))
    logits = (lax.dot_general(w_hi, h_hi, nt, preferred_element_type=F32)
              + lax.dot_general(w_hi, h_lo, nt, preferred_element_type=F32)
              + lax.dot_general(w_lo, h_hi, nt, preferred_element_type=F32))
    scores = jax.nn.sigmoid(logits)
    biased = scores + bias_t_ref[:, 0:1]
    b3 = biased.reshape(N_EXPERT_GROUPS, GROUP_SIZE, n)
    i3 = lax.broadcasted_iota(I32, b3.shape, 1).astype(F32)
    m1 = jnp.max(b3, axis=1, keepdims=True)
    first = jnp.min(jnp.where(b3 == m1, i3, float(GROUP_SIZE)), axis=1, keepdims=True)
    m2 = jnp.max(jnp.where(i3 == first, -jnp.inf, b3), axis=1, keepdims=True)
    gscore = (m1 + m2).reshape(N_EXPERT_GROUPS, n)
    gidx = lax.broadcasted_iota(I32, gscore.shape, 0).astype(F32)
    gsel = jnp.zeros(gscore.shape, F32)
    work = gscore
    for _ in range(TOP_GROUPS):
        hit = _first_max_onehot(work, gidx, N_EXPERT_GROUPS)
        gsel = jnp.where(hit, 1.0, gsel)
        work = jnp.where(hit, -jnp.inf, work)
    gsel3 = jnp.broadcast_to(gsel.reshape(N_EXPERT_GROUPS, 1, n), b3.shape)
    work = jnp.where(gsel3 > 0.0, b3, -jnp.inf).reshape(N_EXPERTS, n)
    eidx = lax.broadcasted_iota(I32, work.shape, 0).astype(F32)
    sel = jnp.zeros(work.shape, F32)
    for _ in range(TOP_K):
        hit = _first_max_onehot(work, eidx, N_EXPERTS)
        sel = jnp.where(hit, 1.0, sel)
        work = jnp.where(hit, -jnp.inf, work)
    picked = jnp.where(sel > 0.0, scores, 0.0)
    gates = picked / jnp.sum(picked, axis=0, keepdims=True) * ROUTE_SCALE
    below = (lax.broadcasted_iota(I32, (N_EXPERTS, N_EXPERTS), 1)
             < lax.broadcasted_iota(I32, (N_EXPERTS, N_EXPERTS), 0))
    slot = _dot(jnp.where(below, 1.0, 0.0).astype(BF16), sel.astype(BF16))
    e_rows, w_rows = [], []
    for s in range(TOP_K):
        here = jnp.where(slot == float(s), sel, 0.0)
        e_rows.append(jnp.sum(here * eidx, axis=0, keepdims=True))
        w_rows.append(jnp.sum(here * gates, axis=0, keepdims=True))
    experts = jnp.concatenate(e_rows, axis=0).astype(I32)
    w_t = jnp.concatenate(w_rows + [jnp.zeros((LANES - TOP_K, n), F32)], axis=0)
    return experts, w_t.T


def _group_norm_gate(o, g):
    parts = []
    for h in range(RET_HEADS):
        oh = o[:, h * RET_V_DIM:(h + 1) * RET_V_DIM]
        mu = jnp.mean(oh, axis=-1, keepdims=True)
        ctr = oh - mu
        var = jnp.mean(ctr * ctr, axis=-1, keepdims=True)
        parts.append(ctr * lax.rsqrt(var + EPS))
    return _silu(g) * jnp.concatenate(parts, axis=-1)


def _pool_project(pooled, w_pool_ref, pool_scale_ref):
    parts = [_dot(p.astype(BF16), w_pool_ref[gi].astype(BF16)) for gi, p in enumerate(pooled)]
    return jnp.concatenate(parts, axis=-1) * pool_scale_ref[...]


def _mix_tail(x, o_gated, p, mod_ref, w_out_ref, norm2_ref, wr_t_ref, bias_t_ref,
              x1_ref, h2_ref, experts_ref, gatew_ref):
    mix = jnp.concatenate([o_gated, p], axis=-1).astype(BF16)
    y = _dot(mix, w_out_ref[...])
    x1 = x + _mod(mod_ref, 2) * y
    h2 = _rms(x1) * norm2_ref[...] * (1.0 + _mod(mod_ref, 4)) + _mod(mod_ref, 3)
    x1_ref[...] = x1.reshape(x1_ref.shape)
    h2_ref[...] = _pack_rows(h2)
    experts, gate_w = _route(h2, wr_t_ref, bias_t_ref)
    experts_ref[...] = experts
    gatew_ref[...] = gate_w


def _ada_kernel(c_ref, w_ref, b_ref, o_ref):
    cs = _silu(c_ref[...]).astype(BF16)
    o_ref[...] = _dot(cs, w_ref[...].astype(BF16)) + b_ref[...]


def _ada(c_all, w_ada, b_ada, block_n=1536):
    n, d = c_all.shape
    width = w_ada.shape[1]
    return pl.pallas_call(
        _ada_kernel,
        grid=(width // block_n,),
        in_specs=[pl.BlockSpec((n, d), lambda j: (0, 0)),
                  pl.BlockSpec((d, block_n), lambda j: (0, j)),
                  pl.BlockSpec((1, block_n), lambda j: (0, j))],
        out_specs=pl.BlockSpec((n, block_n), lambda j: (0, j)),
        out_shape=jax.ShapeDtypeStruct((n, width), F32),
        compiler_params=pltpu.CompilerParams(vmem_limit_bytes=VMEM_LIMIT),
        name="ada",
    )(c_all, w_ada, b_ada.reshape(1, width))


def _mix_prompt_kernel(x_ref, mod_ref, norm1_ref, w_in_ref, cos_ref, sin_ref, dmat_ref, cross_ref,
                       tail_ref, cdec_ref, w_pool_ref, pool_scale_ref, w_out_ref, norm2_ref,
                       wr_t_ref, bias_t_ref,
                       x1_ref, h2_ref, experts_ref, gatew_ref, ret_ref, pool_ref,
                       state_ref, ext_ref, o_ref, *, block_l, chunk):
    li = pl.program_id(1)

    @pl.when(li == 0)
    def _():
        state_ref[...] = jnp.zeros_like(state_ref)
        ext_ref[0:POOL_CARRY, :] = jnp.zeros((POOL_CARRY, POOL_WIDTH), F32)

    x = x_ref[0]
    h = _rms(x) * norm1_ref[...] * (1.0 + _mod(mod_ref, 1)) + _mod(mod_ref, 0)
    proj = _dot(h.astype(BF16), w_in_ref[...])
    q = proj[:, 0:QK_WIDTH]
    k = proj[:, QK_WIDTH:2 * QK_WIDTH]
    v = proj[:, 2 * QK_WIDTH:2 * QK_WIDTH + RET_WIDTH]
    g = proj[:, 2 * QK_WIDTH + RET_WIDTH:2 * QK_WIDTH + 2 * RET_WIDTH]
    u = proj[:, 2 * QK_WIDTH + 2 * RET_WIDTH:]

    lane = lax.broadcasted_iota(I32, q.shape, 1)
    first_half = (lane % RET_QK_DIM) < (RET_QK_DIM // 2)
    cos_t = cos_ref[...]
    sin_t = sin_ref[...]

    def rot(t):
        partner = jnp.where(first_half, pltpu.roll(t, QK_WIDTH - RET_QK_DIM // 2, axis=1),
                            pltpu.roll(t, RET_QK_DIM // 2, axis=1))
        return t * cos_t + partner * sin_t

    q = rot(q)
    k = rot(k) * (RET_QK_DIM ** -0.5)
    k_t = k.T
    v16 = v.astype(BF16)
    head_of_lane = lax.broadcasted_iota(I32, (chunk, QK_WIDTH), 1) // RET_QK_DIM

    for c in range(block_l // chunk):
        rows = slice(c * chunk, (c + 1) * chunk)
        q_c = q[rows]
        kt_c = k_t[:, rows]
        kt16 = kt_c.astype(BF16)
        state16 = state_ref[...].astype(BF16)
        for hd in range(RET_HEADS):
            in_head = head_of_lane == hd
            q_h = jnp.where(in_head, q_c, 0.0).astype(BF16)
            v_h = v16[rows, hd * RET_V_DIM:(hd + 1) * RET_V_DIM]
            scores = _dot(q_h, kt16) * dmat_ref[hd]
            inner = _dot(scores.astype(BF16), v_h)
            cross = _dot(q_h, state16) * cross_ref[hd]
            o_ref[rows, hd * RET_V_DIM:(hd + 1) * RET_V_DIM] = inner + cross
            hrows = slice(hd * RET_QK_DIM, (hd + 1) * RET_QK_DIM)
            k_dec = (kt_c[hrows] * tail_ref[hd:hd + 1, :]).astype(BF16)
            state_ref[hrows, :] = state_ref[hrows, :] * cdec_ref[hd] + _dot(k_dec, v_h)

    o_gated = _group_norm_gate(o_ref[...], g)

    ext_ref[POOL_CARRY:POOL_CARRY + block_l, :] = u
    pos = (li * block_l + lax.broadcasted_iota(I32, (block_l, 1), 0)).astype(F32)
    pooled = []
    for gi, w in enumerate(POOL_WINDOWS):
        lanes = slice(gi * POOL_GROUP_DIM, (gi + 1) * POOL_GROUP_DIM)
        acc = ext_ref[POOL_CARRY:POOL_CARRY + block_l, lanes]
        for j in range(1, w):
            acc = acc + ext_ref[POOL_CARRY - j:POOL_CARRY - j + block_l, lanes]
        cnt = jnp.minimum(pos + 1.0, float(w))
        pooled.append(acc / cnt - u[:, lanes])
    p = _pool_project(pooled, w_pool_ref, pool_scale_ref)
    ext_ref[0:POOL_CARRY, :] = ext_ref[block_l:block_l + POOL_CARRY, :]

    _mix_tail(x, o_gated, p, mod_ref, w_out_ref, norm2_ref, wr_t_ref, bias_t_ref,
              x1_ref, h2_ref, experts_ref, gatew_ref)

    @pl.when(li == pl.num_programs(1) - 1)
    def _():
        ret_ref[...] = state_ref[...].reshape(ret_ref.shape)
        pool_ref[...] = ext_ref[1:1 + POOL_BUF, :].reshape(pool_ref.shape)


def _decay_tables(chunk):
    lg = jnp.log(1.0 - 2.0 ** (-5.0 - jnp.arange(RET_HEADS, dtype=F32)))
    idx = jnp.arange(chunk, dtype=F32)
    diff = idx[:, None] - idx[None, :]
    causal = diff >= 0
    dmat = jnp.where(causal[None], jnp.exp(lg[:, None, None] * jnp.where(causal, diff, 0.0)[None]), 0.0)
    cross = jnp.exp(lg[:, None] * (idx[None, :] + 1.0))
    cross = jnp.broadcast_to(cross[:, :, None], (RET_HEADS, chunk, RET_V_DIM))
    tail = jnp.exp(lg[:, None] * (chunk - 1.0 - idx)[None, :])
    cdec = jnp.broadcast_to(jnp.exp(lg * chunk)[:, None, None], (RET_HEADS, RET_QK_DIM, RET_V_DIM))
    return dmat, cross, tail, cdec


def _rotary_tables(pos):
    half = RET_QK_DIM // 2
    freqs = ROPE_BASE ** (-jnp.arange(half, dtype=F32) / half)
    ang = pos[:, None] * freqs[None, :]
    cos, sin = jnp.cos(ang), jnp.sin(ang)
    cos_t = jnp.tile(jnp.concatenate([cos, cos], axis=-1), (1, RET_HEADS))
    sin_t = jnp.tile(jnp.concatenate([-sin, sin], axis=-1), (1, RET_HEADS))
    return cos_t, sin_t


def _full(shape):
    return pl.BlockSpec(shape, lambda *_: (0,) * len(shape))


def _mix_prompt(x, mod, b0, b, norm1, w_in16, w_pool, pool_scale, w_out16, norm2, wr_t, bias_t,
                block_l=512, chunk=256):
    _, l, d = x.shape
    n_tokens = b * l
    nl = l // block_l
    cos_t, sin_t = _rotary_tables(jnp.arange(l, dtype=F32))
    dmat, cross, tail, cdec = _decay_tables(chunk)
    kernel = functools.partial(_mix_prompt_kernel, block_l=block_l, chunk=chunk)
    tok = lambda bi, li: (bi, li, 0)
    flat = lambda bi, li: (bi * nl + li, 0)
    return pl.pallas_call(
        kernel,
        grid=(b, nl),
        in_specs=[pl.BlockSpec((1, block_l, d), lambda bi, li: (b0 + bi, li, 0)),
                  pl.BlockSpec((1, 6, d), lambda bi, li: (b0 + bi, 0, 0)),
                  _full((1, d)),
                  _full((d, IN_WIDTH)),
                  pl.BlockSpec((block_l, QK_WIDTH), lambda bi, li: (li, 0)),
                  pl.BlockSpec((block_l, QK_WIDTH), lambda bi, li: (li, 0)),
                  _full(dmat.shape), _full(cross.shape), _full(tail.shape), _full(cdec.shape),
                  _full(w_pool.shape), _full((1, POOL_WIDTH)), _full((d, d)), _full((1, d)),
                  _full(wr_t.shape), _full(bias_t.shape)],
        out_specs=[pl.BlockSpec((1, block_l, d), tok),
                   pl.BlockSpec((block_l, HALF), flat),
                   pl.BlockSpec((TOP_K, block_l), lambda bi, li: (0, bi * nl + li)),
                   pl.BlockSpec((block_l, LANES), flat),
                   pl.BlockSpec((1, RET_HEADS, RET_QK_DIM, RET_V_DIM), lambda bi, li: (bi, 0, 0, 0)),
                   pl.BlockSpec((1, POOL_BUF, POOL_WIDTH), lambda bi, li: (bi, 0, 0))],
        out_shape=[jax.ShapeDtypeStruct((b, l, d), F32),
                   jax.ShapeDtypeStruct((n_tokens, HALF), U32),
                   jax.ShapeDtypeStruct((TOP_K, n_tokens), I32),
                   jax.ShapeDtypeStruct((b * l, LANES), F32),
                   jax.ShapeDtypeStruct((b, RET_HEADS, RET_QK_DIM, RET_V_DIM), F32),
                   jax.ShapeDtypeStruct((b, POOL_BUF, POOL_WIDTH), F32)],
        scratch_shapes=[pltpu.VMEM((QK_WIDTH, RET_V_DIM), F32),
                        pltpu.VMEM((POOL_CARRY + block_l, POOL_WIDTH), F32),
                        pltpu.VMEM((block_l, RET_WIDTH), F32)],
        compiler_params=pltpu.CompilerParams(dimension_semantics=("arbitrary", "arbitrary"),
                                             vmem_limit_bytes=VMEM_LIMIT),
        name="mix_prompt",
    )(x, mod, norm1, w_in16, cos_t, sin_t, dmat, cross, tail, cdec, w_pool, pool_scale,
      w_out16, norm2, wr_t, bias_t)


def _mix_sample_front_kernel(x_ref, mod_ref, norm1_ref, w_in_ref, cos_ref, sin_ref,
                             qt_ref, kt_ref, v_ref, g_ref, u_ref):
    x = x_ref[...]
    h = _rms(x) * norm1_ref[...] * (1.0 + _mod(mod_ref, 1)) + _mod(mod_ref, 0)
    proj = _dot(h.astype(BF16), w_in_ref[...])
    half = RET_QK_DIM // 2
    cos_c = cos_ref[...]
    sin_c = sin_ref[...]

    def rot_t(t):
        parts = []
        for hd in range(RET_HEADS):
            t1 = t[hd * RET_QK_DIM:hd * RET_QK_DIM + half]
            t2 = t[hd * RET_QK_DIM + half:(hd + 1) * RET_QK_DIM]
            parts += [t1 * cos_c - t2 * sin_c, t1 * sin_c + t2 * cos_c]
        return jnp.concatenate(parts, axis=0)

    qt_ref[...] = rot_t(proj[:, 0:QK_WIDTH].T)
    kt_ref[...] = rot_t(proj[:, QK_WIDTH:2 * QK_WIDTH].T) * (RET_QK_DIM ** -0.5)
    v_ref[...] = proj[:, 2 * QK_WIDTH:2 * QK_WIDTH + RET_WIDTH]
    g_ref[...] = proj[:, 2 * QK_WIDTH + RET_WIDTH:2 * QK_WIDTH + 2 * RET_WIDTH]
    u_ref[...] = proj[:, 2 * QK_WIDTH + 2 * RET_WIDTH:]


def _ret_step_kernel(qt_ref, kt_ref, v_ref, s0_ref, o_ref, s1_ref, *, block_b, decays):
    i = pl.program_id(0)
    lane = lax.broadcasted_iota(I32, qt_ref.shape, 1)
    for j in range(block_b):
        bi = i * block_b + j
        here = lane == bi
        q_col = jnp.sum(jnp.where(here, qt_ref[...], 0.0), axis=1, keepdims=True)
        k_col = jnp.sum(jnp.where(here, kt_ref[...], 0.0), axis=1, keepdims=True)
        v_row = v_ref[pl.ds(bi, 1), :]
        outs = []
        for hd in range(RET_HEADS):
            hrows = slice(hd * RET_QK_DIM, (hd + 1) * RET_QK_DIM)
            s1 = decays[hd] * s0_ref[j, hd] + k_col[hrows] * v_row[:, hd * RET_V_DIM:(hd + 1) * RET_V_DIM]
            s1_ref[j, hd] = s1
            outs.append(jnp.sum(q_col[hrows] * s1, axis=0, keepdims=True))
        o_ref[pl.ds(bi, 1), :] = jnp.concatenate(outs, axis=-1)


def _mix_sample_back_kernel(x_ref, mod_ref, o_ref, g_ref, u_ref, buf_ref, w_pool_ref, pool_scale_ref,
                            w_out_ref, norm2_ref, wr_t_ref, bias_t_ref,
                            x1_ref, h2_ref, experts_ref, gatew_ref, pool_ref):
    o_gated = _group_norm_gate(o_ref[...], g_ref[...])
    u = u_ref[...]
    pooled = []
    for gi, w in enumerate(POOL_WINDOWS):
        lanes = slice(gi * POOL_GROUP_DIM, (gi + 1) * POOL_GROUP_DIM)
        acc = u[:, lanes]
        for j in range(1, w):
            acc = acc + buf_ref[:, POOL_BUF - j, lanes]
        pooled.append(acc / float(w) - u[:, lanes])
    p = _pool_project(pooled, w_pool_ref, pool_scale_ref)
    pool_ref[:, 0:POOL_BUF - 1, :] = buf_ref[:, 1:POOL_BUF, :]
    pool_ref[:, POOL_BUF - 1, :] = u
    _mix_tail(x_ref[...], o_gated, p, mod_ref, w_out_ref, norm2_ref, wr_t_ref, bias_t_ref,
              x1_ref, h2_ref, experts_ref, gatew_ref)


def _mix_sample(x, mod, state_ret, state_pool, start, norm1, w_in16, w_pool,
                pool_scale, w_out16, norm2, wr_t, bias_t, block_b=32):
    n, d = x.shape
    half = RET_QK_DIM // 2
    freqs = ROPE_BASE ** (-jnp.arange(half, dtype=F32) / half)
    ang = jnp.full((1,), start, F32)[:, None] * freqs[None, :]
    cos_c = jnp.broadcast_to(jnp.cos(ang).T, (half, n))
    sin_c = jnp.broadcast_to(jnp.sin(ang).T, (half, n))
    params = pltpu.CompilerParams(vmem_limit_bytes=VMEM_LIMIT)
    qt, kt, v, g, u = pl.pallas_call(
        _mix_sample_front_kernel,
        out_shape=[jax.ShapeDtypeStruct((QK_WIDTH, n), F32), jax.ShapeDtypeStruct((QK_WIDTH, n), F32),
                   jax.ShapeDtypeStruct((n, RET_WIDTH), F32), jax.ShapeDtypeStruct((n, RET_WIDTH), F32),
                   jax.ShapeDtypeStruct((n, POOL_WIDTH), F32)],
        compiler_params=params,
        name="mix_sample_front",
    )(x, mod, norm1, w_in16, cos_c, sin_c)

    lg = np.log(1.0 - 2.0 ** (-5.0 - np.arange(RET_HEADS, dtype=np.float32)), dtype=np.float32)
    decays = tuple(float(np.exp(lg[h])) for h in range(RET_HEADS))
    state_block = (block_b, RET_HEADS, RET_QK_DIM, RET_V_DIM)
    o, s1 = pl.pallas_call(
        functools.partial(_ret_step_kernel, block_b=block_b, decays=decays),
        grid=(n // block_b,),
        in_specs=[_full((QK_WIDTH, n)), _full((QK_WIDTH, n)), _full((n, RET_WIDTH)),
                  pl.BlockSpec(state_block, lambda i: (i, 0, 0, 0))],
        out_specs=[_full((n, RET_WIDTH)), pl.BlockSpec(state_block, lambda i: (i, 0, 0, 0))],
        out_shape=[jax.ShapeDtypeStruct((n, RET_WIDTH), F32),
                   jax.ShapeDtypeStruct(state_ret.shape, F32)],
        compiler_params=pltpu.CompilerParams(dimension_semantics=("arbitrary",),
                                             vmem_limit_bytes=VMEM_LIMIT),
        name="ret_step",
    )(qt, kt, v, state_ret)

    x1, h2, experts, gate_w, pool = pl.pallas_call(
        _mix_sample_back_kernel,
        out_shape=[jax.ShapeDtypeStruct((n, d), F32),
                   jax.ShapeDtypeStruct((n, HALF), U32),
                   jax.ShapeDtypeStruct((TOP_K, n), I32),
                   jax.ShapeDtypeStruct((n, LANES), F32),
                   jax.ShapeDtypeStruct(state_pool.shape, F32)],
        compiler_params=params,
        name="mix_sample_back",
    )(x, mod, o, g, u, state_pool, w_pool, pool_scale, w_out16, norm2, wr_t, bias_t)
    return x1, h2, experts, gate_w, s1, pool


def _plan_kernel(experts_ref, pos_ref, meta_ref, cnt_ref, carry_ref, off_ref, *, block_t):
    phase = pl.program_id(0)
    j = pl.program_id(1)
    e_blk = experts_ref[...]
    eidx = lax.broadcasted_iota(I32, (N_EXPERTS, block_t), 0)
    member = jnp.zeros((N_EXPERTS, block_t), F32)
    for s in range(TOP_K):
        member = member + jnp.where(eidx == e_blk[s:s + 1, :], 1.0, 0.0)
    per_expert = jnp.broadcast_to(jnp.sum(member, axis=1, keepdims=True), (N_EXPERTS, LANES))

    @pl.when((phase == 0) & (j == 0))
    def _():
        cnt_ref[...] = jnp.zeros_like(cnt_ref)

    @pl.when(phase == 0)
    def _():
        cnt_ref[...] += per_expert

    @pl.when((phase == 0) & (j == pl.num_programs(1) - 1))
    def _():
        cnt = cnt_ref[...]
        n_tile = jnp.floor((cnt + (ROW_TILE - 1.0)) * (1.0 / ROW_TILE))
        upto = (lax.broadcasted_iota(I32, (N_EXPERTS, N_EXPERTS), 1)
                <= lax.broadcasted_iota(I32, (N_EXPERTS, N_EXPERTS), 0))
        tile_end = _dot(jnp.where(upto, 1.0, 0.0).astype(BF16), n_tile.astype(BF16))
        tile_start = tile_end - n_tile
        off_ref[...] = tile_start * ROW_TILE
        carry_ref[...] = jnp.zeros_like(carry_ref)
        lane = lax.broadcasted_iota(I32, cnt.shape, 1)
        meta_ref[...] = jnp.where(lane == 0, tile_start, jnp.where(lane == 1, n_tile, cnt)).astype(I32)

    @pl.when(phase == 1)
    def _():
        before = (lax.broadcasted_iota(I32, (block_t, block_t), 0)
                  < lax.broadcasted_iota(I32, (block_t, block_t), 1))
        rank = _dot(member.astype(BF16), jnp.where(before, 1.0, 0.0).astype(BF16))
        row = off_ref[:, 0:1] + carry_ref[:, 0:1] + rank
        carry_ref[...] += per_expert
        out = [jnp.sum(jnp.where(eidx == e_blk[s:s + 1, :], row, 0.0), axis=0, keepdims=True)
               for s in range(TOP_K)]
        pos_ref[...] = jnp.concatenate(out, axis=0).astype(I32)


def _plan(experts_all, max_block=1024):
    n_tokens = experts_all.shape[1]
    block_t = max(k for k in range(LANES, max_block + 1, LANES) if n_tokens % k == 0)
    nb = n_tokens // block_t
    return pl.pallas_call(
        functools.partial(_plan_kernel, block_t=block_t),
        grid=(2, nb),
        in_specs=[pl.BlockSpec((TOP_K, block_t), lambda ph, j: (0, j))],
        out_specs=[pl.BlockSpec((TOP_K, block_t), lambda ph, j: (0, j * ph)),
                   _full((N_EXPERTS, LANES))],
        out_shape=[jax.ShapeDtypeStruct((TOP_K, n_tokens), I32),
                   jax.ShapeDtypeStruct((N_EXPERTS, LANES), I32)],
        scratch_shapes=[pltpu.VMEM((N_EXPERTS, LANES), F32)] * 3,
        compiler_params=pltpu.CompilerParams(dimension_semantics=("arbitrary", "arbitrary"),
                                             vmem_limit_bytes=VMEM_LIMIT),
        name="plan",
    )(experts_all)


def _sc_workers():
    info = plsc.get_sparse_core_info()
    return info.num_cores, info.num_cores * info.num_subcores


def _sc_scatter_rows(sources, pos_t, n_out):
    w = sources[0].shape[1]
    s = pos_t.shape[0]
    n_cores, n_workers = _sc_workers()
    bounds = np.cumsum([0] + [src.shape[0] // SC_CHUNK for src in sources])
    n_chunks = int(bounds[-1])
    iters = -(-n_chunks // n_workers)
    mesh = plsc.VectorSubcoreMesh(core_axis_name="c", subcore_axis_name="s")

    @functools.partial(
        pl.kernel, mesh=mesh, out_type=jax.ShapeDtypeStruct((n_out, w), sources[0].dtype),
        scratch_types=[pltpu.VMEM((SC_CHUNK, w), sources[0].dtype), pltpu.VMEM((s, SC_CHUNK), I32),
                       pltpu.SemaphoreType.DMA],
        name="dispatch")
    def k(*refs):
        src_hbm, (pos_hbm, out_hbm, rows_v, idx_v, sem) = refs[:len(sources)], refs[len(sources):]
        wid = lax.axis_index("s") * n_cores + lax.axis_index("c")

        @pl.loop(0, iters)
        def _(it):
            c = it * n_workers + wid
            for src, lo, hi in zip(src_hbm, bounds[:-1], bounds[1:]):
                @pl.when((c >= int(lo)) & (c < int(hi)))
                def _():
                    base = pl.multiple_of((c - int(lo)) * SC_CHUNK, SC_CHUNK)
                    pltpu.sync_copy(src.at[pl.ds(base, SC_CHUNK)], rows_v)

            @pl.when(c < n_chunks)
            def _():
                base = pl.multiple_of(c * SC_CHUNK, SC_CHUNK)
                pltpu.sync_copy(pos_hbm.at[:, pl.ds(base, SC_CHUNK)], idx_v)
                copies = [pltpu.async_copy(rows_v, out_hbm.at[idx_v.at[j]], sem) for j in range(s)]
                for cp in copies:
                    cp.wait()

    return k(*sources, pos_t)


def _sc_gather_rows(table, pos_t):
    _, w = table.shape
    s, t = pos_t.shape
    n_cores, n_workers = _sc_workers()
    n_chunks = t // SC_CHUNK
    iters = -(-n_chunks // n_workers)
    mesh = plsc.VectorSubcoreMesh(core_axis_name="c", subcore_axis_name="s")

    @functools.partial(
        pl.kernel, mesh=mesh, out_type=jax.ShapeDtypeStruct((s, t, w), table.dtype),
        scratch_types=[pltpu.VMEM((SC_CHUNK, w), table.dtype), pltpu.VMEM((s, SC_CHUNK), I32),
                       pltpu.SemaphoreType.DMA],
        name="combine")
    def k(table_hbm, pos_hbm, out_hbm, rows_v, idx_v, sem):
        wid = lax.axis_index("s") * n_cores + lax.axis_index("c")

        @pl.loop(0, iters)
        def _(it):
            c = it * n_workers + wid

            @pl.when(c < n_chunks)
            def _():
                base = pl.multiple_of(c * SC_CHUNK, SC_CHUNK)
                pltpu.sync_copy(pos_hbm.at[:, pl.ds(base, SC_CHUNK)], idx_v)
                for j in range(s):
                    pltpu.async_copy(table_hbm.at[idx_v.at[j]], rows_v, sem).wait()
                    pltpu.sync_copy(rows_v, out_hbm.at[j, pl.ds(base, SC_CHUNK)])

    return k(table, pos_t)


def _experts_kernel(first_ref, ntile_ref, cnt_ref, xs_hbm, wg_ref, wu_ref, wd_ref, ys_hbm,
                    wg16_ref, wu16_ref, wd16_ref, x_buf, y_buf, in_sem, out_sem):
    e = pl.program_id(0)
    n_used = first_ref[N_EXPERTS - 1] + ntile_ref[N_EXPERTS - 1]
    first, n_mine, count = first_ref[e], ntile_ref[e], cnt_ref[e]

    def tile_rows(g):
        return pl.ds(pl.multiple_of(g * ROW_TILE, ROW_TILE), ROW_TILE)

    def load(g):
        slot = lax.rem(g, STREAM_DEPTH)
        return pltpu.make_async_copy(xs_hbm.at[tile_rows(g)], x_buf.at[slot], in_sem.at[slot])

    def store(g):
        slot = lax.rem(g, STREAM_DEPTH)
        return pltpu.make_async_copy(y_buf.at[slot], ys_hbm.at[tile_rows(g)], out_sem.at[slot])

    @pl.when(e == 0)
    def _():
        for g0 in range(STREAM_DEPTH - 1):
            @pl.when(g0 < n_used)
            def _():
                load(g0).start()

    wg16_ref[...] = wg_ref[0].astype(BF16)
    wu16_ref[...] = wu_ref[0].astype(BF16)
    wd16_ref[...] = wd_ref[0].astype(BF16)

    def tile(j, carry):
        g = first + j
        slot = lax.rem(g, STREAM_DEPTH)
        load(g).wait()

        @pl.when(g + (STREAM_DEPTH - 1) < n_used)
        def _():
            load(g + (STREAM_DEPTH - 1)).start()

        @pl.when(g >= STREAM_DEPTH)
        def _():
            store(g - STREAM_DEPTH).wait()

        words = x_buf[slot]
        row = lax.broadcasted_iota(I32, words.shape, 0)
        words = jnp.where(row < count - j * ROW_TILE, words, jnp.uint32(0))
        lo, hi = _unpack_rows(words)
        lo, hi = lo.astype(BF16), hi.astype(BF16)
        hg = _dot(lo, wg16_ref[0:HALF, :]) + _dot(hi, wg16_ref[HALF:, :])
        hu = _dot(lo, wu16_ref[0:HALF, :]) + _dot(hi, wu16_ref[HALF:, :])
        a = (_silu(hg) * hu).astype(BF16)
        y_buf[slot] = _pack_rows(_dot(a, wd16_ref[...]))
        store(g).start()
        return carry

    lax.fori_loop(0, n_mine, tile, 0)

    @pl.when(e == N_EXPERTS - 1)
    def _():
        for back in range(STREAM_DEPTH, 0, -1):
            @pl.when(n_used >= back)
            def _():
                store(n_used - back).wait()


def _experts(xs, first_tile, n_tile, count, w_eg, w_eu, w_ed):
    d = D_MODEL
    by_expert = lambda e, *_: (e, 0, 0)
    grid_spec = pltpu.PrefetchScalarGridSpec(
        num_scalar_prefetch=3,
        grid=(N_EXPERTS,),
        in_specs=[pl.BlockSpec(memory_space=pl.ANY),
                  pl.BlockSpec((1, d, EXPERT_DIM), by_expert),
                  pl.BlockSpec((1, d, EXPERT_DIM), by_expert),
                  pl.BlockSpec((1, EXPERT_DIM, d), by_expert)],
        out_specs=pl.BlockSpec(memory_space=pl.ANY),
        scratch_shapes=[pltpu.VMEM((d, EXPERT_DIM), BF16), pltpu.VMEM((d, EXPERT_DIM), BF16),
                        pltpu.VMEM((EXPERT_DIM, d), BF16),
                        pltpu.VMEM((STREAM_DEPTH, ROW_TILE, HALF), U32),
                        pltpu.VMEM((STREAM_DEPTH, ROW_TILE, HALF), U32),
                        pltpu.SemaphoreType.DMA((STREAM_DEPTH,)), pltpu.SemaphoreType.DMA((STREAM_DEPTH,))])
    return pl.pallas_call(
        _experts_kernel,
        grid_spec=grid_spec,
        out_shape=jax.ShapeDtypeStruct(xs.shape, U32),
        compiler_params=pltpu.CompilerParams(dimension_semantics=("arbitrary",),
                                             vmem_limit_bytes=VMEM_LIMIT),
        name="experts",
    )(first_tile, n_tile, count, xs, w_eg, w_eu, w_ed)


def _final_kernel(z_ref, gatew_ref, h2_ref, x1_ref, mod_ref, normf_ref, wsg_ref, wsu_ref, wsd_ref, *rest):
    y_ref = rest[-1]
    lo, hi = _unpack_rows(h2_ref[...])
    h = jnp.concatenate([lo, hi], axis=-1).astype(BF16)
    a = _silu(_dot(h, wsg_ref[...])) * _dot(h, wsu_ref[...])
    acc = _dot(a.astype(BF16), wsd_ref[...])
    for s in range(TOP_K):
        lo, hi = _unpack_rows(z_ref[s])
        acc = acc + gatew_ref[:, s:s + 1] * jnp.concatenate([lo, hi], axis=-1)
    x2 = x1_ref[...] + _mod(mod_ref, 5) * acc
    y_ref[...] = _rms(x2) * normf_ref[...]


def _final(z, gate_w, h2, x1, mod, norm_f, w_sg16, w_su16, w_sd16, block_t, first_block, per_seq,
           seq0=0, out_rows=None, y_prev=None):
    t, d = x1.shape
    out_rows = t if out_rows is None else out_rows
    out_first = seq0 * per_seq
    tok = lambda i: (i, 0)
    if per_seq:
        mod_spec = pl.BlockSpec((1, 6, d), lambda i: (seq0 + i // per_seq, 0, 0))
    else:
        mod_spec = pl.BlockSpec((block_t, 6 * d), tok)
    operands = [z, gate_w, h2, x1, mod, norm_f, w_sg16, w_su16, w_sd16]
    in_specs = [pl.BlockSpec((TOP_K, block_t, HALF), lambda i: (0, first_block + i, 0)),
                pl.BlockSpec((block_t, LANES), tok),
                pl.BlockSpec((block_t, HALF), tok),
                pl.BlockSpec((block_t, d), tok),
                mod_spec,
                _full((1, d)),
                _full((d, EXPERT_DIM)), _full((d, EXPERT_DIM)), _full((EXPERT_DIM, d))]
    aliases = {}
    if y_prev is not None:
        aliases = {len(operands): 0}
        operands.append(y_prev)
        in_specs.append(pl.BlockSpec(memory_space=pl.ANY))
    return pl.pallas_call(
        _final_kernel,
        grid=(t // block_t,),
        in_specs=in_specs,
        out_specs=pl.BlockSpec((block_t, d), lambda i: (out_first + i, 0)),
        out_shape=jax.ShapeDtypeStruct((out_rows, d), F32),
        input_output_aliases=aliases,
        compiler_params=pltpu.CompilerParams(dimension_semantics=("arbitrary",),
                                             vmem_limit_bytes=VMEM_LIMIT),
        name="final",
    )(*operands)


def kernel(x_prompt, x_sample, c_prompt, c_sample, state_ret, state_pool, norm1, norm2, norm_f,
           w_ada, b_ada, w_in, w_out, w_pool, pool_scale, w_router, router_bias, w_exp_gate,
           w_exp_up, w_exp_down, w_sh_gate, w_sh_up, w_sh_down):
    b, l, d = x_prompt.shape
    n = x_sample.shape[0]
    past_len = 16384

    mod = _ada(jnp.concatenate([c_prompt, c_sample], axis=0), w_ada[0], b_ada[0])
    mod_p = mod[:b].reshape(b, 6, d)
    mod_s = mod[b:]

    w_in16 = w_in[0].astype(BF16)
    w_out16 = w_out[0].astype(BF16)
    wr_t = w_router[0].T
    bias_t = jnp.broadcast_to(router_bias[0][:, None], (N_EXPERTS, LANES))
    n1, n2, nf = norm1[0].reshape(1, d), norm2[0].reshape(1, d), norm_f.reshape(1, d)
    ps = pool_scale[0].reshape(1, POOL_WIDTH)
    shared = (w_sh_gate[0].astype(BF16), w_sh_up[0].astype(BF16), w_sh_down[0].astype(BF16))

    def routed(sources, experts):
        n_tiles = experts.shape[1] * TOP_K // ROW_TILE + N_EXPERTS
        pos_t, meta = _plan(experts)
        xs = _sc_scatter_rows(sources, pos_t, n_tiles * ROW_TILE)
        ys = _experts(xs, meta[:, 0], meta[:, 1], meta[:, 2], w_exp_gate[0], w_exp_up[0], w_exp_down[0])
        return _sc_gather_rows(ys, pos_t)

    ba = b // 2
    bb = b - ba
    mix_args = (n1, w_in16, w_pool[0], ps, w_out16, n2, wr_t, bias_t)
    x1_a, h2_a, experts_a, gatew_a, ret_a, pool_a = _mix_prompt(x_prompt, mod_p, 0, ba, *mix_args)
    z_a = routed((h2_a,), experts_a)
    x1_b, h2_b, experts_b, gatew_b, ret_b, pool_b = _mix_prompt(x_prompt, mod_p, ba, bb, *mix_args)
    x1_s, h2_s, experts_s, gatew_s, ret_s, pool_s = _mix_sample(
        x_sample.reshape(n, d), mod_s, state_ret[0], state_pool[0], float(past_len), *mix_args)
    z_b = routed((h2_b, h2_s), jnp.concatenate([experts_b, experts_s], axis=1))

    block_t = 256
    per_seq = l // block_t
    y_s = _final(z_b, gatew_s, h2_s, x1_s, mod_s, nf, *shared,
                 block_t=n, first_block=bb * l // n, per_seq=0)
    y_p = _final(z_b, gatew_b, h2_b, x1_b.reshape(bb * l, d), mod_p, nf, *shared,
                 block_t=block_t, first_block=0, per_seq=per_seq, seq0=ba, out_rows=b * l)
    y_p = _final(z_a, gatew_a, h2_a, x1_a.reshape(ba * l, d), mod_p, nf, *shared,
                 block_t=block_t, first_block=0, per_seq=per_seq, seq0=0, out_rows=b * l, y_prev=y_p)

    ret_p = jnp.concatenate([ret_a, ret_b], axis=0)
    pool_p = jnp.concatenate([pool_a, pool_b], axis=0)
    return (y_p.reshape(b, l, d), y_s.reshape(n, 1, d), ret_p[None], pool_p[None],
            ret_s[None], pool_s[None])
```

```python
import functools

import jax
import jax.numpy as jnp
import numpy as np
from jax import lax
from jax.experimental import pallas as pl
from jax.experimental.pallas import tpu as pltpu
from jax.experimental.pallas import tpu_sc as plsc

D_MODEL = 1024
RET_HEADS = 4
RET_QK_DIM = 64
RET_V_DIM = 128
RET_WIDTH = RET_HEADS * RET_V_DIM
QK_WIDTH = RET_HEADS * RET_QK_DIM
ROPE_BASE = 10000.0
POOL_WINDOWS = (2, 4, 8, 16)
POOL_WIDTH = 512
POOL_GROUP_DIM = 128
POOL_BUF = 15
IN_WIDTH = 2 * QK_WIDTH + 2 * RET_WIDTH + POOL_WIDTH
N_EXPERTS = 64
TOP_K = 8
N_EXPERT_GROUPS = 8
GROUP_SIZE = N_EXPERTS // N_EXPERT_GROUPS
TOP_GROUPS = 4
EXPERT_DIM = 256
ROUTE_SCALE = 2.5
EPS = 1e-6

LANES = 128
SUBLANES = 8
POOL_CARRY = 24
VMEM_LIMIT = 56 * 1024 * 1024
HALF = D_MODEL // 2
ROW_TILE = 256
SC_CHUNK = 128
STREAM_DEPTH = 8

BF16 = jnp.bfloat16
F32 = jnp.float32
U32 = jnp.uint32
I32 = jnp.int32


def _silu(x):
    return x * jax.nn.sigmoid(x)


def _dot(a, b):
    return jnp.dot(a, b, preferred_element_type=F32)


def _rms(x):
    return x * lax.rsqrt(jnp.mean(x * x, axis=-1, keepdims=True) + EPS)


def _mod(mod_ref, i, seq=0):
    if len(mod_ref.shape) == 3:
        return mod_ref[seq, i:i + 1, :]
    return mod_ref[:, i * D_MODEL:(i + 1) * D_MODEL]


def _split_bf16(x):
    hi = x.astype(BF16)
    lo = (x - hi.astype(F32)).astype(BF16)
    return hi, lo


def _pack_rows(x):
    lo = lax.bitcast_convert_type(x[:, :HALF].astype(BF16).astype(F32), U32)
    hi = lax.bitcast_convert_type(x[:, HALF:].astype(BF16).astype(F32), U32)
    return (hi & jnp.uint32(0xFFFF0000)) | (lo >> jnp.uint32(16))


def _unpack_rows(w):
    lo = lax.bitcast_convert_type(w << jnp.uint32(16), F32)
    hi = lax.bitcast_convert_type(w & jnp.uint32(0xFFFF0000), F32)
    return lo, hi


def _first_max_onehot(work, idx, n):
    m = jnp.max(work, axis=0, keepdims=True)
    first = jnp.min(jnp.where(work == m, idx, float(n)), axis=0, keepdims=True)
    return idx == first


def _route(h2, wr_t_ref, bias_t_ref):
    n = h2.shape[0]
    h_hi, h_lo = _split_bf16(h2)
    w_hi, w_lo = _split_bf16(wr_t_ref[...])
    nt = (((1,), (1,)), ((), ()))
    logits = (lax.dot_general(w_hi, h_hi, nt, preferred_element_type=F32)
              + lax.dot_general(w_hi, h_lo, nt, preferred_element_type=F32)
              + lax.dot_general(w_lo, h_hi, nt, preferred_element_type=F32))
    scores = jax.nn.sigmoid(logits)
    biased = scores + bias_t_ref[:, 0:1]
    b3 = biased.reshape(N_EXPERT_GROUPS, GROUP_SIZE, n)
    i3 = lax.broadcasted_iota(I32, b3.shape, 1).astype(F32)
    m1 = jnp.max(b3, axis=1, keepdims=True)
    first = jnp.min(jnp.where(b3 == m1, i3, float(GROUP_SIZE)), axis=1, keepdims=True)
    m2 = jnp.max(jnp.where(i3 == first, -jnp.inf, b3), axis=1, keepdims=True)
    gscore = (m1 + m2).reshape(N_EXPERT_GROUPS, n)
    gidx = lax.broadcasted_iota(I32, gscore.shape, 0).astype(F32)
    gsel = jnp.zeros(gscore.shape, F32)
    work = gscore
    for _ in range(TOP_GROUPS):
        hit = _first_max_onehot(work, gidx, N_EXPERT_GROUPS)
        gsel = jnp.where(hit, 1.0, gsel)
        work = jnp.where(hit, -jnp.inf, work)
    gsel3 = jnp.broadcast_to(gsel.reshape(N_EXPERT_GROUPS, 1, n), b3.shape)
    work = jnp.where(gsel3 > 0.0, b3, -jnp.inf).reshape(N_EXPERTS, n)
    eidx = lax.broadcasted_iota(I32, work.shape, 0).astype(F32)
    sel = jnp.zeros(work.shape, F32)
    for _ in range(TOP_K):
        hit = _first_max_onehot(work, eidx, N_EXPERTS)
        sel = jnp.where(hit, 1.0, sel)
        work = jnp.where(hit, -jnp.inf, work)
    picked = jnp.where(sel > 0.0, scores, 0.0)
    gates = picked / jnp.sum(picked, axis=0, keepdims=True) * ROUTE_SCALE
    below = (lax.broadcasted_iota(I32, (N_EXPERTS, N_EXPERTS), 1)
             < lax.broadcasted_iota(I32, (N_EXPERTS, N_EXPERTS), 0))
    slot = _dot(jnp.where(below, 1.0, 0.0).astype(BF16), sel.astype(BF16))
    e_rows, w_rows = [], []
    for s in range(TOP_K):
        here = jnp.where(slot == float(s), sel, 0.0)
        e_rows.append(jnp.sum(here * eidx, axis=0, keepdims=True))
        w_rows.append(jnp.sum(here * gates, axis=0, keepdims=True))
    experts = jnp.concatenate(e_rows, axis=0).astype(I32)
    w_t = jnp.concatenate(w_rows + [jnp.zeros((LANES - TOP_K, n), F32)], axis=0)
    return experts, w_t.T


def _group_norm_gate(o, g):
    parts = []
    for h in range(RET_HEADS):
        oh = o[:, h * RET_V_DIM:(h + 1) * RET_V_DIM]
        mu = jnp.mean(oh, axis=-1, keepdims=True)
        ctr = oh - mu
        var = jnp.mean(ctr * ctr, axis=-1, keepdims=True)
        parts.append(ctr * lax.rsqrt(var + EPS))
    return _silu(g) * jnp.concatenate(parts, axis=-1)


def _pool_project(pooled, w_pool_ref, pool_scale_ref):
    parts = [_dot(p.astype(BF16), w_pool_ref[gi].astype(BF16)) for gi, p in enumerate(pooled)]
    return jnp.concatenate(parts, axis=-1) * pool_scale_ref[...]


def _mix_tail(x, o_gated, p, mod_ref, seq, w_out_ref, norm2_ref, wr_t_ref, bias_t_ref,
              x1_ref, h2_ref, experts_ref, gatew_ref):
    mix = jnp.concatenate([o_gated, p], axis=-1).astype(BF16)
    y = _dot(mix, w_out_ref[...])
    x1 = x + _mod(mod_ref, 2, seq) * y
    h2 = _rms(x1) * norm2_ref[...] * (1.0 + _mod(mod_ref, 4, seq)) + _mod(mod_ref, 3, seq)
    x1_ref[...] = x1
    h2_ref[...] = _pack_rows(h2)
    experts, gate_w = _route(h2, wr_t_ref, bias_t_ref)
    experts_ref[...] = experts
    gatew_ref[...] = gate_w


def _ada_kernel(c_ref, w_ref, b_ref, o_ref):
    cs = _silu(c_ref[...]).astype(BF16)
    o_ref[...] = _dot(cs, w_ref[...].astype(BF16)) + b_ref[...]


def _ada(c_all, w_ada, b_ada, block_n=1536):
    n, d = c_all.shape
    width = w_ada.shape[1]
    return pl.pallas_call(
        _ada_kernel,
        grid=(width // block_n,),
        in_specs=[pl.BlockSpec((n, d), lambda j: (0, 0)),
                  pl.BlockSpec((d, block_n), lambda j: (0, j)),
                  pl.BlockSpec((1, block_n), lambda j: (0, j))],
        out_specs=pl.BlockSpec((n, block_n), lambda j: (0, j)),
        out_shape=jax.ShapeDtypeStruct((n, width), F32),
        compiler_params=pltpu.CompilerParams(vmem_limit_bytes=VMEM_LIMIT),
        name="ada",
    )(c_all, w_ada, b_ada.reshape(1, width))


def _mix_prompt_kernel(x_ref, mod_ref, norm1_ref, w_in_ref, cos_ref, sin_ref, dmat_ref, cross_ref,
                       tail_ref, cdec_ref, w_pool_ref, pool_scale_ref, w_out_ref, norm2_ref,
                       wr_t_ref, bias_t_ref,
                       x1_ref, h2_ref, experts_ref, gatew_ref, ret_ref, pool_ref,
                       state_ref, ext_ref, win_ref, o_ref, *, block_l, chunk, seqs):
    li = pl.program_id(1)

    @pl.when(li == 0)
    def _():
        state_ref[...] = jnp.zeros_like(state_ref)
        ext_ref[:, 0:POOL_CARRY, :] = jnp.zeros((seqs, POOL_CARRY, POOL_WIDTH), F32)
        win_ref[:, 0:SUBLANES, :] = jnp.zeros((seqs, SUBLANES, POOL_WIDTH), F32)

    for seq in range(seqs):
        _mix_prompt_seq(seq, li, x_ref, mod_ref, norm1_ref, w_in_ref, cos_ref, sin_ref, dmat_ref, cross_ref,
                        tail_ref, cdec_ref, w_pool_ref, pool_scale_ref, w_out_ref, norm2_ref,
                        wr_t_ref, bias_t_ref, x1_ref, h2_ref, experts_ref, gatew_ref,
                        state_ref.at[seq], ext_ref.at[seq], win_ref.at[seq], o_ref.at[seq],
                        block_l=block_l, chunk=chunk)

    @pl.when(li == pl.num_programs(1) - 1)
    def _():
        ret_ref[...] = state_ref[...].reshape(ret_ref.shape)
        pool_ref[...] = ext_ref[:, POOL_CARRY - POOL_BUF:POOL_CARRY, :]


def _window_sums(ext_ref, win_ref, block_l):
    g = POOL_GROUP_DIM
    top = POOL_CARRY + block_l
    new = slice(POOL_CARRY - SUBLANES, None)
    s2 = ext_ref[SUBLANES:top, :] + ext_ref[SUBLANES - 1:top - 1, :]
    win_ref[SUBLANES:top, g:] = s2[:, g:]
    s4 = s2[:, g:] + win_ref[SUBLANES - 2:top - 2, g:]
    win_ref[SUBLANES:top, 2 * g:] = s4[:, g:]
    s8 = s4[:, g:] + win_ref[SUBLANES - 4:top - 4, 2 * g:]
    win_ref[SUBLANES:top, 3 * g:] = s8[:, g:]
    s16 = s8[:, g:] + win_ref[0:top - SUBLANES, 3 * g:]
    return [s2[new, 0:g], s4[new, 0:g], s8[new, 0:g], s16[new, :]]


def _mix_prompt_seq(seq, li, x_ref, mod_ref, norm1_ref, w_in_ref, cos_ref, sin_ref, dmat_ref, cross_ref,
                    tail_ref, cdec_ref, w_pool_ref, pool_scale_ref, w_out_ref, norm2_ref,
                    wr_t_ref, bias_t_ref, x1_ref, h2_ref, experts_ref, gatew_ref,
                    state_ref, ext_ref, win_ref, o_ref, *, block_l, chunk):
    x = x_ref[seq]
    h = _rms(x) * norm1_ref[...] * (1.0 + _mod(mod_ref, 1, seq)) + _mod(mod_ref, 0, seq)
    proj = _dot(h.astype(BF16), w_in_ref[...])
    q = proj[:, 0:QK_WIDTH]
    k = proj[:, QK_WIDTH:2 * QK_WIDTH]
    v = proj[:, 2 * QK_WIDTH:2 * QK_WIDTH + RET_WIDTH]
    g = proj[:, 2 * QK_WIDTH + RET_WIDTH:2 * QK_WIDTH + 2 * RET_WIDTH]
    u = proj[:, 2 * QK_WIDTH + 2 * RET_WIDTH:]

    lane = lax.broadcasted_iota(I32, q.shape, 1)
    first_half = (lane % RET_QK_DIM) < (RET_QK_DIM // 2)
    cos_t = cos_ref[...]
    sin_t = sin_ref[...]

    def rot(t):
        partner = jnp.where(first_half, pltpu.roll(t, QK_WIDTH - RET_QK_DIM // 2, axis=1),
                            pltpu.roll(t, RET_QK_DIM // 2, axis=1))
        return t * cos_t + partner * sin_t

    q = rot(q)
    k = rot(k) * (RET_QK_DIM ** -0.5)
    k_t = k.T
    v16 = v.astype(BF16)
    head_of_lane = lax.broadcasted_iota(I32, (chunk, QK_WIDTH), 1) // RET_QK_DIM

    for c in range(block_l // chunk):
        rows = slice(c * chunk, (c + 1) * chunk)
        q_c = q[rows]
        kt_c = k_t[:, rows]
        kt16 = kt_c.astype(BF16)
        state16 = state_ref[...].astype(BF16)
        for hd in range(RET_HEADS):
            in_head = head_of_lane == hd
            q_h = jnp.where(in_head, q_c, 0.0).astype(BF16)
            v_h = v16[rows, hd * RET_V_DIM:(hd + 1) * RET_V_DIM]
            scores = _dot(q_h, kt16) * dmat_ref[hd]
            inner = _dot(scores.astype(BF16), v_h)
            cross = _dot(q_h, state16) * cross_ref[hd]
            o_ref[rows, hd * RET_V_DIM:(hd + 1) * RET_V_DIM] = inner + cross
            hrows = slice(hd * RET_QK_DIM, (hd + 1) * RET_QK_DIM)
            k_dec = (kt_c[hrows] * tail_ref[hd:hd + 1, :]).astype(BF16)
            state_ref[hrows, :] = state_ref[hrows, :] * cdec_ref[hd] + _dot(k_dec, v_h)

    o_gated = _group_norm_gate(o_ref[...], g)

    ext_ref[POOL_CARRY:POOL_CARRY + block_l, :] = u
    pos = (li * block_l + lax.broadcasted_iota(I32, (block_l, 1), 0)).astype(F32)
    pooled = []
    for gi, (w, acc) in enumerate(zip(POOL_WINDOWS, _window_sums(ext_ref, win_ref, block_l))):
        cnt = jnp.minimum(pos + 1.0, float(w))
        pooled.append(acc / cnt - u[:, gi * POOL_GROUP_DIM:(gi + 1) * POOL_GROUP_DIM])
    p = _pool_project(pooled, w_pool_ref, pool_scale_ref)
    ext_ref[0:POOL_CARRY, :] = ext_ref[block_l:block_l + POOL_CARRY, :]

    _mix_tail(x, o_gated, p, mod_ref, seq, w_out_ref, norm2_ref, wr_t_ref, bias_t_ref,
              x1_ref.at[seq], h2_ref.at[seq], experts_ref.at[seq], gatew_ref.at[seq])


def _decay_tables(chunk):
    lg = jnp.log(1.0 - 2.0 ** (-5.0 - jnp.arange(RET_HEADS, dtype=F32)))
    idx = jnp.arange(chunk, dtype=F32)
    diff = idx[:, None] - idx[None, :]
    causal = diff >= 0
    dmat = jnp.where(causal[None], jnp.exp(lg[:, None, None] * jnp.where(causal, diff, 0.0)[None]), 0.0)
    cross = jnp.exp(lg[:, None] * (idx[None, :] + 1.0))
    cross = jnp.broadcast_to(cross[:, :, None], (RET_HEADS, chunk, RET_V_DIM))
    tail = jnp.exp(lg[:, None] * (chunk - 1.0 - idx)[None, :])
    cdec = jnp.broadcast_to(jnp.exp(lg * chunk)[:, None, None], (RET_HEADS, RET_QK_DIM, RET_V_DIM))
    return dmat, cross, tail, cdec


def _rotary_tables(pos):
    half = RET_QK_DIM // 2
    freqs = ROPE_BASE ** (-jnp.arange(half, dtype=F32) / half)
    ang = pos[:, None] * freqs[None, :]
    cos, sin = jnp.cos(ang), jnp.sin(ang)
    cos_t = jnp.tile(jnp.concatenate([cos, cos], axis=-1), (1, RET_HEADS))
    sin_t = jnp.tile(jnp.concatenate([-sin, sin], axis=-1), (1, RET_HEADS))
    return cos_t, sin_t


def _full(shape):
    return pl.BlockSpec(shape, lambda *_: (0,) * len(shape))


def _mix_prompt(x, mod, b0, b, norm1, w_in16, w_pool, pool_scale, w_out16, norm2, wr_t, bias_t,
                block_l=512, chunk=256, seqs=2):
    _, l, d = x.shape
    nl = l // block_l
    s0 = b0 // seqs
    cos_t, sin_t = _rotary_tables(jnp.arange(l, dtype=F32))
    dmat, cross, tail, cdec = _decay_tables(chunk)
    kernel = functools.partial(_mix_prompt_kernel, block_l=block_l, chunk=chunk, seqs=seqs)
    tok = lambda bi, li: (bi, li, 0)
    per_seq = lambda bi, li: (bi, 0, 0)
    x1, h2, experts, gate_w, ret, pool = pl.pallas_call(
        kernel,
        grid=(b // seqs, nl),
        in_specs=[pl.BlockSpec((seqs, block_l, d), lambda bi, li: (s0 + bi, li, 0)),
                  pl.BlockSpec((seqs, 6, d), lambda bi, li: (s0 + bi, 0, 0)),
                  _full((1, d)),
                  _full((d, IN_WIDTH)),
                  pl.BlockSpec((block_l, QK_WIDTH), lambda bi, li: (li, 0)),
                  pl.BlockSpec((block_l, QK_WIDTH), lambda bi, li: (li, 0)),
                  _full(dmat.shape), _full(cross.shape), _full(tail.shape), _full(cdec.shape),
                  _full(w_pool.shape), _full((1, POOL_WIDTH)), _full((d, d)), _full((1, d)),
                  _full(wr_t.shape), _full(bias_t.shape)],
        out_specs=[pl.BlockSpec((seqs, block_l, d), tok),
                   pl.BlockSpec((seqs, block_l, HALF), tok),
                   pl.BlockSpec((seqs, TOP_K, block_l), lambda bi, li: (bi, 0, li)),
                   pl.BlockSpec((seqs, block_l, LANES), tok),
                   pl.BlockSpec((seqs, RET_HEADS, RET_QK_DIM, RET_V_DIM), lambda bi, li: (bi, 0, 0, 0)),
                   pl.BlockSpec((seqs, POOL_BUF, POOL_WIDTH), per_seq)],
        out_shape=[jax.ShapeDtypeStruct((b, l, d), F32),
                   jax.ShapeDtypeStruct((b, l, HALF), U32),
                   jax.ShapeDtypeStruct((b, TOP_K, l), I32),
                   jax.ShapeDtypeStruct((b, l, LANES), F32),
                   jax.ShapeDtypeStruct((b, RET_HEADS, RET_QK_DIM, RET_V_DIM), F32),
                   jax.ShapeDtypeStruct((b, POOL_BUF, POOL_WIDTH), F32)],
        scratch_shapes=[pltpu.VMEM((seqs, QK_WIDTH, RET_V_DIM), F32),
                        pltpu.VMEM((seqs, POOL_CARRY + block_l, POOL_WIDTH), F32),
                        pltpu.VMEM((seqs, POOL_CARRY + block_l, POOL_WIDTH), F32),
                        pltpu.VMEM((seqs, block_l, RET_WIDTH), F32)],
        compiler_params=pltpu.CompilerParams(dimension_semantics=("arbitrary", "arbitrary"),
                                             vmem_limit_bytes=VMEM_LIMIT),
        name="mix_prompt",
    )(x, mod, norm1, w_in16, cos_t, sin_t, dmat, cross, tail, cdec, w_pool, pool_scale,
      w_out16, norm2, wr_t, bias_t)
    experts = jnp.transpose(experts, (1, 0, 2)).reshape(TOP_K, b * l)
    return x1, h2.reshape(b * l, HALF), experts, gate_w.reshape(b * l, LANES), ret, pool


def _mix_sample_front_kernel(x_ref, mod_ref, norm1_ref, w_in_ref, cos_ref, sin_ref,
                             qt_ref, kt_ref, v_ref, g_ref, u_ref):
    x = x_ref[...]
    h = _rms(x) * norm1_ref[...] * (1.0 + _mod(mod_ref, 1)) + _mod(mod_ref, 0)
    proj = _dot(h.astype(BF16), w_in_ref[...])
    half = RET_QK_DIM // 2
    cos_c = cos_ref[...]
    sin_c = sin_ref[...]

    def rot_t(t):
        parts = []
        for hd in range(RET_HEADS):
            t1 = t[hd * RET_QK_DIM:hd * RET_QK_DIM + half]
            t2 = t[hd * RET_QK_DIM + half:(hd + 1) * RET_QK_DIM]
            parts += [t1 * cos_c - t2 * sin_c, t1 * sin_c + t2 * cos_c]
        return jnp.concatenate(parts, axis=0)

    qt_ref[...] = rot_t(proj[:, 0:QK_WIDTH].T)
    kt_ref[...] = rot_t(proj[:, QK_WIDTH:2 * QK_WIDTH].T) * (RET_QK_DIM ** -0.5)
    v_ref[...] = proj[:, 2 * QK_WIDTH:2 * QK_WIDTH + RET_WIDTH]
    g_ref[...] = proj[:, 2 * QK_WIDTH + RET_WIDTH:2 * QK_WIDTH + 2 * RET_WIDTH]
    u_ref[...] = proj[:, 2 * QK_WIDTH + 2 * RET_WIDTH:]


def _ret_step_kernel(qt_ref, kt_ref, v_ref, s0_ref, o_ref, s1_ref, *, block_b, decays):
    i = pl.program_id(0)
    lane = lax.broadcasted_iota(I32, qt_ref.shape, 1)
    for j in range(block_b):
        bi = i * block_b + j
        here = lane == bi
        q_col = jnp.sum(jnp.where(here, qt_ref[...], 0.0), axis=1, keepdims=True)
        k_col = jnp.sum(jnp.where(here, kt_ref[...], 0.0), axis=1, keepdims=True)
        v_row = v_ref[pl.ds(bi, 1), :]
        outs = []
        for hd in range(RET_HEADS):
            hrows = slice(hd * RET_QK_DIM, (hd + 1) * RET_QK_DIM)
            s1 = decays[hd] * s0_ref[j, hd] + k_col[hrows] * v_row[:, hd * RET_V_DIM:(hd + 1) * RET_V_DIM]
            s1_ref[j, hd] = s1
            outs.append(jnp.sum(q_col[hrows] * s1, axis=0, keepdims=True))
        o_ref[pl.ds(bi, 1), :] = jnp.concatenate(outs, axis=-1)


def _mix_sample_back_kernel(x_ref, mod_ref, o_ref, g_ref, u_ref, buf_ref, w_pool_ref, pool_scale_ref,
                            w_out_ref, norm2_ref, wr_t_ref, bias_t_ref,
                            x1_ref, h2_ref, experts_ref, gatew_ref, pool_ref):
    o_gated = _group_norm_gate(o_ref[...], g_ref[...])
    u = u_ref[...]
    pooled = []
    for gi, w in enumerate(POOL_WINDOWS):
        lanes = slice(gi * POOL_GROUP_DIM, (gi + 1) * POOL_GROUP_DIM)
        acc = u[:, lanes]
        for j in range(1, w):
            acc = acc + buf_ref[:, POOL_BUF - j, lanes]
        pooled.append(acc / float(w) - u[:, lanes])
    p = _pool_project(pooled, w_pool_ref, pool_scale_ref)
    pool_ref[:, 0:POOL_BUF - 1, :] = buf_ref[:, 1:POOL_BUF, :]
    pool_ref[:, POOL_BUF - 1, :] = u
    _mix_tail(x_ref[...], o_gated, p, mod_ref, 0, w_out_ref, norm2_ref, wr_t_ref, bias_t_ref,
              x1_ref, h2_ref, experts_ref, gatew_ref)


def _mix_sample(x, mod, state_ret, state_pool, start, norm1, w_in16, w_pool,
                pool_scale, w_out16, norm2, wr_t, bias_t, block_b=32):
    n, d = x.shape
    half = RET_QK_DIM // 2
    freqs = ROPE_BASE ** (-jnp.arange(half, dtype=F32) / half)
    ang = jnp.full((1,), start, F32)[:, None] * freqs[None, :]
    cos_c = jnp.broadcast_to(jnp.cos(ang).T, (half, n))
    sin_c = jnp.broadcast_to(jnp.sin(ang).T, (half, n))
    params = pltpu.CompilerParams(vmem_limit_bytes=VMEM_LIMIT)
    qt, kt, v, g, u = pl.pallas_call(
        _mix_sample_front_kernel,
        out_shape=[jax.ShapeDtypeStruct((QK_WIDTH, n), F32), jax.ShapeDtypeStruct((QK_WIDTH, n), F32),
                   jax.ShapeDtypeStruct((n, RET_WIDTH), F32), jax.ShapeDtypeStruct((n, RET_WIDTH), F32),
                   jax.ShapeDtypeStruct((n, POOL_WIDTH), F32)],
        compiler_params=params,
        name="mix_sample_front",
    )(x, mod, norm1, w_in16, cos_c, sin_c)

    lg = np.log(1.0 - 2.0 ** (-5.0 - np.arange(RET_HEADS, dtype=np.float32)), dtype=np.float32)
    decays = tuple(float(np.exp(lg[h])) for h in range(RET_HEADS))
    state_block = (block_b, RET_HEADS, RET_QK_DIM, RET_V_DIM)
    o, s1 = pl.pallas_call(
        functools.partial(_ret_step_kernel, block_b=block_b, decays=decays),
        grid=(n // block_b,),
        in_specs=[_full((QK_WIDTH, n)), _full((QK_WIDTH, n)), _full((n, RET_WIDTH)),
                  pl.BlockSpec(state_block, lambda i: (i, 0, 0, 0))],
        out_specs=[_full((n, RET_WIDTH)), pl.BlockSpec(state_block, lambda i: (i, 0, 0, 0))],
        out_shape=[jax.ShapeDtypeStruct((n, RET_WIDTH), F32),
                   jax.ShapeDtypeStruct(state_ret.shape, F32)],
        compiler_params=pltpu.CompilerParams(dimension_semantics=("arbitrary",),
                                             vmem_limit_bytes=VMEM_LIMIT),
        name="ret_step",
    )(qt, kt, v, state_ret)

    x1, h2, experts, gate_w, pool = pl.pallas_call(
        _mix_sample_back_kernel,
        out_shape=[jax.ShapeDtypeStruct((n, d), F32),
                   jax.ShapeDtypeStruct((n, HALF), U32),
                   jax.ShapeDtypeStruct((TOP_K, n), I32),
                   jax.ShapeDtypeStruct((n, LANES), F32),
                   jax.ShapeDtypeStruct(state_pool.shape, F32)],
        compiler_params=params,
        name="mix_sample_back",
    )(x, mod, o, g, u, state_pool, w_pool, pool_scale, w_out16, norm2, wr_t, bias_t)
    return x1, h2, experts, gate_w, s1, pool


def _plan_kernel(experts_ref, pos_ref, meta_ref, cnt_ref, carry_ref, off_ref, *, block_t):
    phase = pl.program_id(0)
    j = pl.program_id(1)
    e_blk = experts_ref[...]
    eidx = lax.broadcasted_iota(I32, (N_EXPERTS, block_t), 0)
    member = jnp.zeros((N_EXPERTS, block_t), F32)
    for s in range(TOP_K):
        member = member + jnp.where(eidx == e_blk[s:s + 1, :], 1.0, 0.0)
    per_expert = jnp.broadcast_to(jnp.sum(member, axis=1, keepdims=True), (N_EXPERTS, LANES))

    @pl.when((phase == 0) & (j == 0))
    def _():
        cnt_ref[...] = jnp.zeros_like(cnt_ref)

    @pl.when(phase == 0)
    def _():
        cnt_ref[...] += per_expert

    @pl.when((phase == 0) & (j == pl.num_programs(1) - 1))
    def _():
        cnt = cnt_ref[...]
        n_tile = jnp.floor((cnt + (ROW_TILE - 1.0)) * (1.0 / ROW_TILE))
        upto = (lax.broadcasted_iota(I32, (N_EXPERTS, N_EXPERTS), 1)
                <= lax.broadcasted_iota(I32, (N_EXPERTS, N_EXPERTS), 0))
        tile_end = _dot(jnp.where(upto, 1.0, 0.0).astype(BF16), n_tile.astype(BF16))
        tile_start = tile_end - n_tile
        off_ref[...] = tile_start * ROW_TILE
        carry_ref[...] = jnp.zeros_like(carry_ref)
        lane = lax.broadcasted_iota(I32, cnt.shape, 1)
        meta_ref[...] = jnp.where(lane == 0, tile_start, jnp.where(lane == 1, n_tile, cnt)).astype(I32)

    @pl.when(phase == 1)
    def _():
        before = (lax.broadcasted_iota(I32, (block_t, block_t), 0)
                  < lax.broadcasted_iota(I32, (block_t, block_t), 1))
        rank = _dot(member.astype(BF16), jnp.where(before, 1.0, 0.0).astype(BF16))
        row = off_ref[:, 0:1] + carry_ref[:, 0:1] + rank
        carry_ref[...] += per_expert
        out = [jnp.sum(jnp.where(eidx == e_blk[s:s + 1, :], row, 0.0), axis=0, keepdims=True)
               for s in range(TOP_K)]
        pos_ref[...] = jnp.concatenate(out, axis=0).astype(I32)


def _plan(experts_all, max_block=1024):
    n_tokens = experts_all.shape[1]
    block_t = max(k for k in range(LANES, max_block + 1, LANES) if n_tokens % k == 0)
    nb = n_tokens // block_t
    return pl.pallas_call(
        functools.partial(_plan_kernel, block_t=block_t),
        grid=(2, nb),
        in_specs=[pl.BlockSpec((TOP_K, block_t), lambda ph, j: (0, j))],
        out_specs=[pl.BlockSpec((TOP_K, block_t), lambda ph, j: (0, j * ph)),
                   _full((N_EXPERTS, LANES))],
        out_shape=[jax.ShapeDtypeStruct((TOP_K, n_tokens), I32),
                   jax.ShapeDtypeStruct((N_EXPERTS, LANES), I32)],
        scratch_shapes=[pltpu.VMEM((N_EXPERTS, LANES), F32)] * 3,
        compiler_params=pltpu.CompilerParams(dimension_semantics=("arbitrary", "arbitrary"),
                                             vmem_limit_bytes=VMEM_LIMIT),
        name="plan",
    )(experts_all)


def _sc_workers():
    info = plsc.get_sparse_core_info()
    return info.num_cores, info.num_cores * info.num_subcores


def _sc_scatter_rows(sources, pos_t, n_out):
    w = sources[0].shape[1]
    s = pos_t.shape[0]
    n_cores, n_workers = _sc_workers()
    bounds = np.cumsum([0] + [src.shape[0] // SC_CHUNK for src in sources])
    n_chunks = int(bounds[-1])
    iters = -(-n_chunks // n_workers)
    mesh = plsc.VectorSubcoreMesh(core_axis_name="c", subcore_axis_name="s")

    @functools.partial(
        pl.kernel, mesh=mesh, out_type=jax.ShapeDtypeStruct((n_out, w), sources[0].dtype),
        scratch_types=[pltpu.VMEM((SC_CHUNK, w), sources[0].dtype), pltpu.VMEM((s, SC_CHUNK), I32),
                       pltpu.SemaphoreType.DMA],
        name="dispatch")
    def k(*refs):
        src_hbm, (pos_hbm, out_hbm, rows_v, idx_v, sem) = refs[:len(sources)], refs[len(sources):]
        wid = lax.axis_index("s") * n_cores + lax.axis_index("c")

        @pl.loop(0, iters)
        def _(it):
            c = it * n_workers + wid
            for src, lo, hi in zip(src_hbm, bounds[:-1], bounds[1:]):
                @pl.when((c >= int(lo)) & (c < int(hi)))
                def _():
                    base = pl.multiple_of((c - int(lo)) * SC_CHUNK, SC_CHUNK)
                    pltpu.sync_copy(src.at[pl.ds(base, SC_CHUNK)], rows_v)

            @pl.when(c < n_chunks)
            def _():
                base = pl.multiple_of(c * SC_CHUNK, SC_CHUNK)
                pltpu.sync_copy(pos_hbm.at[:, pl.ds(base, SC_CHUNK)], idx_v)
                copies = [pltpu.async_copy(rows_v, out_hbm.at[idx_v.at[j]], sem) for j in range(s)]
                for cp in copies:
                    cp.wait()

    return k(*sources, pos_t)


def _sc_gather_rows(table, pos_t):
    _, w = table.shape
    s, t = pos_t.shape
    n_cores, n_workers = _sc_workers()
    n_chunks = t // SC_CHUNK
    iters = -(-n_chunks // n_workers)
    mesh = plsc.VectorSubcoreMesh(core_axis_name="c", subcore_axis_name="s")

    @functools.partial(
        pl.kernel, mesh=mesh, out_type=jax.ShapeDtypeStruct((s, t, w), table.dtype),
        scratch_types=[pltpu.VMEM((SC_CHUNK, w), table.dtype), pltpu.VMEM((s, SC_CHUNK), I32),
                       pltpu.SemaphoreType.DMA],
        name="combine")
    def k(table_hbm, pos_hbm, out_hbm, rows_v, idx_v, sem):
        wid = lax.axis_index("s") * n_cores + lax.axis_index("c")

        @pl.loop(0, iters)
        def _(it):
            c = it * n_workers + wid

            @pl.when(c < n_chunks)
            def _():
                base = pl.multiple_of(c * SC_CHUNK, SC_CHUNK)
                pltpu.sync_copy(pos_hbm.at[:, pl.ds(base, SC_CHUNK)], idx_v)
                for j in range(s):
                    pltpu.async_copy(table_hbm.at[idx_v.at[j]], rows_v, sem).wait()
                    pltpu.sync_copy(rows_v, out_hbm.at[j, pl.ds(base, SC_CHUNK)])

    return k(table, pos_t)


def _experts_kernel(first_ref, ntile_ref, cnt_ref, xs_hbm, wg_ref, wu_ref, wd_ref, ys_hbm,
                    wg16_ref, wu16_ref, wd16_ref, x_buf, y_buf, in_sem, out_sem):
    e = pl.program_id(0)
    n_used = first_ref[N_EXPERTS - 1] + ntile_ref[N_EXPERTS - 1]
    first, n_mine, count = first_ref[e], ntile_ref[e], cnt_ref[e]

    def tile_rows(g):
        return pl.ds(pl.multiple_of(g * ROW_TILE, ROW_TILE), ROW_TILE)

    def load(g):
        slot = lax.rem(g, STREAM_DEPTH)
        return pltpu.make_async_copy(xs_hbm.at[tile_rows(g)], x_buf.at[slot], in_sem.at[slot])

    def store(g):
        slot = lax.rem(g, STREAM_DEPTH)
        return pltpu.make_async_copy(y_buf.at[slot], ys_hbm.at[tile_rows(g)], out_sem.at[slot])

    @pl.when(e == 0)
    def _():
        for g0 in range(STREAM_DEPTH - 1):
            @pl.when(g0 < n_used)
            def _():
                load(g0).start()

    wg16_ref[...] = wg_ref[0].astype(BF16)
    wu16_ref[...] = wu_ref[0].astype(BF16)
    wd16_ref[...] = wd_ref[0].astype(BF16)

    def tile(j, carry):
        g = first + j
        slot = lax.rem(g, STREAM_DEPTH)
        load(g).wait()

        @pl.when(g + (STREAM_DEPTH - 1) < n_used)
        def _():
            load(g + (STREAM_DEPTH - 1)).start()

        @pl.when(g >= STREAM_DEPTH)
        def _():
            store(g - STREAM_DEPTH).wait()

        words = x_buf[slot]
        row = lax.broadcasted_iota(I32, words.shape, 0)
        words = jnp.where(row < count - j * ROW_TILE, words, jnp.uint32(0))
        lo, hi = _unpack_rows(words)
        lo, hi = lo.astype(BF16), hi.astype(BF16)
        hg = _dot(lo, wg16_ref[0:HALF, :]) + _dot(hi, wg16_ref[HALF:, :])
        hu = _dot(lo, wu16_ref[0:HALF, :]) + _dot(hi, wu16_ref[HALF:, :])
        a = (_silu(hg) * hu).astype(BF16)
        y_buf[slot] = _pack_rows(_dot(a, wd16_ref[...]))
        store(g).start()
        return carry

    lax.fori_loop(0, n_mine, tile, 0)

    @pl.when(e == N_EXPERTS - 1)
    def _():
        for back in range(STREAM_DEPTH, 0, -1):
            @pl.when(n_used >= back)
            def _():
                store(n_used - back).wait()


def _experts(xs, first_tile, n_tile, count, w_eg, w_eu, w_ed):
    d = D_MODEL
    by_expert = lambda e, *_: (e, 0, 0)
    grid_spec = pltpu.PrefetchScalarGridSpec(
        num_scalar_prefetch=3,
        grid=(N_EXPERTS,),
        in_specs=[pl.BlockSpec(memory_space=pl.ANY),
                  pl.BlockSpec((1, d, EXPERT_DIM), by_expert),
                  pl.BlockSpec((1, d, EXPERT_DIM), by_expert),
                  pl.BlockSpec((1, EXPERT_DIM, d), by_expert)],
        out_specs=pl.BlockSpec(memory_space=pl.ANY),
        scratch_shapes=[pltpu.VMEM((d, EXPERT_DIM), BF16), pltpu.VMEM((d, EXPERT_DIM), BF16),
                        pltpu.VMEM((EXPERT_DIM, d), BF16),
                        pltpu.VMEM((STREAM_DEPTH, ROW_TILE, HALF), U32),
                        pltpu.VMEM((STREAM_DEPTH, ROW_TILE, HALF), U32),
                        pltpu.SemaphoreType.DMA((STREAM_DEPTH,)), pltpu.SemaphoreType.DMA((STREAM_DEPTH,))])
    return pl.pallas_call(
        _experts_kernel,
        grid_spec=grid_spec,
        out_shape=jax.ShapeDtypeStruct(xs.shape, U32),
        compiler_params=pltpu.CompilerParams(dimension_semantics=("arbitrary",),
                                             vmem_limit_bytes=VMEM_LIMIT),
        name="experts",
    )(first_tile, n_tile, count, xs, w_eg, w_eu, w_ed)


def _final_kernel(z_ref, gatew_ref, h2_ref, x1_ref, mod_ref, normf_ref, wsg_ref, wsu_ref, wsd_ref, *rest):
    y_ref = rest[-1]
    lo, hi = _unpack_rows(h2_ref[...])
    h = jnp.concatenate([lo, hi], axis=-1).astype(BF16)
    a = _silu(_dot(h, wsg_ref[...])) * _dot(h, wsu_ref[...])
    acc = _dot(a.astype(BF16), wsd_ref[...])
    for s in range(TOP_K):
        lo, hi = _unpack_rows(z_ref[s])
        acc = acc + gatew_ref[:, s:s + 1] * jnp.concatenate([lo, hi], axis=-1)
    x2 = x1_ref[...] + _mod(mod_ref, 5) * acc
    y_ref[...] = _rms(x2) * normf_ref[...]


def _final(z, gate_w, h2, x1, mod, norm_f, w_sg16, w_su16, w_sd16, block_t, first_block, per_seq,
           seq0=0, out_rows=None, y_prev=None):
    t, d = x1.shape
    out_rows = t if out_rows is None else out_rows
    out_first = seq0 * per_seq
    tok = lambda i: (i, 0)
    if per_seq:
        mod_spec = pl.BlockSpec((1, 6, d), lambda i: (seq0 + i // per_seq, 0, 0))
    else:
        mod_spec = pl.BlockSpec((block_t, 6 * d), tok)
    operands = [z, gate_w, h2, x1, mod, norm_f, w_sg16, w_su16, w_sd16]
    in_specs = [pl.BlockSpec((TOP_K, block_t, HALF), lambda i: (0, first_block + i, 0)),
                pl.BlockSpec((block_t, LANES), tok),
                pl.BlockSpec((block_t, HALF), tok),
                pl.BlockSpec((block_t, d), tok),
                mod_spec,
                _full((1, d)),
                _full((d, EXPERT_DIM)), _full((d, EXPERT_DIM)), _full((EXPERT_DIM, d))]
    aliases = {}
    if y_prev is not None:
        aliases = {len(operands): 0}
        operands.append(y_prev)
        in_specs.append(pl.BlockSpec(memory_space=pl.ANY))
    return pl.pallas_call(
        _final_kernel,
        grid=(t // block_t,),
        in_specs=in_specs,
        out_specs=pl.BlockSpec((block_t, d), lambda i: (out_first + i, 0)),
        out_shape=jax.ShapeDtypeStruct((out_rows, d), F32),
        input_output_aliases=aliases,
        compiler_params=pltpu.CompilerParams(dimension_semantics=("arbitrary",),
                                             vmem_limit_bytes=VMEM_LIMIT),
        name="final",
    )(*operands)


def kernel(x_prompt, x_sample, c_prompt, c_sample, state_ret, state_pool, norm1, norm2, norm_f,
           w_ada, b_ada, w_in, w_out, w_pool, pool_scale, w_router, router_bias, w_exp_gate,
           w_exp_up, w_exp_down, w_sh_gate, w_sh_up, w_sh_down):
    b, l, d = x_prompt.shape
    n = x_sample.shape[0]
    past_len = 16384

    mod = _ada(jnp.concatenate([c_prompt, c_sample], axis=0), w_ada[0], b_ada[0])
    mod_p = mod[:b].reshape(b, 6, d)
    mod_s = mod[b:]

    w_in16 = w_in[0].astype(BF16)
    w_out16 = w_out[0].astype(BF16)
    wr_t = w_router[0].T
    bias_t = jnp.broadcast_to(router_bias[0][:, None], (N_EXPERTS, LANES))
    n1, n2, nf = norm1[0].reshape(1, d), norm2[0].reshape(1, d), norm_f.reshape(1, d)
    ps = pool_scale[0].reshape(1, POOL_WIDTH)
    shared = (w_sh_gate[0].astype(BF16), w_sh_up[0].astype(BF16), w_sh_down[0].astype(BF16))

    def routed(sources, experts):
        n_tiles = experts.shape[1] * TOP_K // ROW_TILE + N_EXPERTS
        pos_t, meta = _plan(experts)
        xs = _sc_scatter_rows(sources, pos_t, n_tiles * ROW_TILE)
        ys = _experts(xs, meta[:, 0], meta[:, 1], meta[:, 2], w_exp_gate[0], w_exp_up[0], w_exp_down[0])
        return _sc_gather_rows(ys, pos_t)

    ba = b // 2
    bb = b - ba
    mix_args = (n1, w_in16, w_pool[0], ps, w_out16, n2, wr_t, bias_t)
    x1_a, h2_a, experts_a, gatew_a, ret_a, pool_a = _mix_prompt(x_prompt, mod_p, 0, ba, *mix_args)
    z_a = routed((h2_a,), experts_a)
    x1_b, h2_b, experts_b, gatew_b, ret_b, pool_b = _mix_prompt(x_prompt, mod_p, ba, bb, *mix_args)
    x1_s, h2_s, experts_s, gatew_s, ret_s, pool_s = _mix_sample(
        x_sample.reshape(n, d), mod_s, state_ret[0], state_pool[0], float(past_len), *mix_args)
    z_b = routed((h2_b, h2_s), jnp.concatenate([experts_b, experts_s], axis=1))

    block_t = 256
    per_seq = l // block_t
    y_s = _final(z_b, gatew_s, h2_s, x1_s, mod_s, nf, *shared,
                 block_t=n, first_block=bb * l // n, per_seq=0)
    y_p = _final(z_b, gatew_b, h2_b, x1_b.reshape(bb * l, d), mod_p, nf, *shared,
                 block_t=block_t, first_block=0, per_seq=per_seq, seq0=ba, out_rows=b * l)
    y_p = _final(z_a, gatew_a, h2_a, x1_a.reshape(ba * l, d), mod_p, nf, *shared,
                 block_t=block_t, first_block=0, per_seq=per_seq, seq0=0, out_rows=b * l, y_prev=y_p)

    ret_p = jnp.concatenate([ret_a, ret_b], axis=0)
    pool_p = jnp.concatenate([pool_a, pool_b], axis=0)
    return (y_p.reshape(b, l, d), y_s.reshape(n, 1, d), ret_p[None], pool_p[None],
            ret_s[None], pool_s[None])
```

```python
import functools

import jax
import jax.numpy as jnp
import numpy as np
from jax import lax
from jax.experimental import pallas as pl
from jax.experimental.pallas import tpu as pltpu
from jax.experimental.pallas import tpu_sc as plsc

D_MODEL = 1024
RET_HEADS = 4
RET_QK_DIM = 64
RET_V_DIM = 128
RET_WIDTH = RET_HEADS * RET_V_DIM
QK_WIDTH = RET_HEADS * RET_QK_DIM
ROPE_BASE = 10000.0
POOL_WINDOWS = (2, 4, 8, 16)
POOL_WIDTH = 512
POOL_GROUP_DIM = 128
POOL_BUF = 15
IN_WIDTH = 2 * QK_WIDTH + 2 * RET_WIDTH + POOL_WIDTH
N_EXPERTS = 64
TOP_K = 8
N_EXPERT_GROUPS = 8
GROUP_SIZE = N_EXPERTS // N_EXPERT_GROUPS
TOP_GROUPS = 4
EXPERT_DIM = 256
ROUTE_SCALE = 2.5
EPS = 1e-6

LANES = 128
SUBLANES = 8
POOL_CARRY = 24
VMEM_LIMIT = 56 * 1024 * 1024
HALF = D_MODEL // 2
ROW_TILE = 256
SC_CHUNK = 128
SC_LANES = 16
SC_UNROLL = 16
STREAM_DEPTH = 8

BF16 = jnp.bfloat16
F32 = jnp.float32
U32 = jnp.uint32
I32 = jnp.int32


def _silu(x):
    return x * jax.nn.sigmoid(x)


def _dot(a, b):
    return jnp.dot(a, b, preferred_element_type=F32)


def _rms(x):
    return x * lax.rsqrt(jnp.mean(x * x, axis=-1, keepdims=True) + EPS)


def _mod(mod_ref, i, seq=0):
    if len(mod_ref.shape) == 3:
        return mod_ref[seq, i:i + 1, :]
    return mod_ref[:, i * D_MODEL:(i + 1) * D_MODEL]


def _split_bf16(x):
    hi = x.astype(BF16)
    lo = (x - hi.astype(F32)).astype(BF16)
    return hi, lo


def _pack_rows(x):
    lo = lax.bitcast_convert_type(x[:, :HALF].astype(BF16).astype(F32), U32)
    hi = lax.bitcast_convert_type(x[:, HALF:].astype(BF16).astype(F32), U32)
    return (hi & jnp.uint32(0xFFFF0000)) | (lo >> jnp.uint32(16))


def _unpack_rows(w):
    lo = lax.bitcast_convert_type(w << jnp.uint32(16), F32)
    hi = lax.bitcast_convert_type(w & jnp.uint32(0xFFFF0000), F32)
    return lo, hi


def _first_max_onehot(work, idx, n):
    m = jnp.max(work, axis=0, keepdims=True)
    first = jnp.min(jnp.where(work == m, idx, float(n)), axis=0, keepdims=True)
    return idx == first


def _route(h2, wr_t_ref, bias_t_ref):
    n = h2.shape[0]
    h_hi, h_lo = _split_bf16(h2)
    w_hi, w_lo = _split_bf16(wr_t_ref[...])
    nt = (((1,), (1,)), ((), ()))
    logits = (lax.dot_general(w_hi, h_hi, nt, preferred_element_type=F32)
              + lax.dot_general(w_hi, h_lo, nt, preferred_element_type=F32)
              + lax.dot_general(w_lo, h_hi, nt, preferred_element_type=F32))
    scores = jax.nn.sigmoid(logits)
    biased = scores + bias_t_ref[:, 0:1]
    b3 = biased.reshape(N_EXPERT_GROUPS, GROUP_SIZE, n)
    i3 = lax.broadcasted_iota(I32, b3.shape, 1).astype(F32)
    m1 = jnp.max(b3, axis=1, keepdims=True)
    first = jnp.min(jnp.where(b3 == m1, i3, float(GROUP_SIZE)), axis=1, keepdims=True)
    m2 = jnp.max(jnp.where(i3 == first, -jnp.inf, b3), axis=1, keepdims=True)
    gscore = (m1 + m2).reshape(N_EXPERT_GROUPS, n)
    gidx = lax.broadcasted_iota(I32, gscore.shape, 0).astype(F32)
    gsel = jnp.zeros(gscore.shape, F32)
    work = gscore
    for _ in range(TOP_GROUPS):
        hit = _first_max_onehot(work, gidx, N_EXPERT_GROUPS)
        gsel = jnp.where(hit, 1.0, gsel)
        work = jnp.where(hit, -jnp.inf, work)
    gsel3 = jnp.broadcast_to(gsel.reshape(N_EXPERT_GROUPS, 1, n), b3.shape)
    work = jnp.where(gsel3 > 0.0, b3, -jnp.inf).reshape(N_EXPERTS, n)
    eidx = lax.broadcasted_iota(I32, work.shape, 0).astype(F32)
    sel = jnp.zeros(work.shape, F32)
    for _ in range(TOP_K):
        hit = _first_max_onehot(work, eidx, N_EXPERTS)
        sel = jnp.where(hit, 1.0, sel)
        work = jnp.where(hit, -jnp.inf, work)
    picked = jnp.where(sel > 0.0, scores, 0.0)
    gates = picked / jnp.sum(picked, axis=0, keepdims=True) * ROUTE_SCALE
    below = (lax.broadcasted_iota(I32, (N_EXPERTS, N_EXPERTS), 1)
             < lax.broadcasted_iota(I32, (N_EXPERTS, N_EXPERTS), 0))
    slot = _dot(jnp.where(below, 1.0, 0.0).astype(BF16), sel.astype(BF16))
    e_rows, w_rows = [], []
    for s in range(TOP_K):
        here = jnp.where(slot == float(s), sel, 0.0)
        e_rows.append(jnp.sum(here * eidx, axis=0, keepdims=True))
        w_rows.append(jnp.sum(here * gates, axis=0, keepdims=True))
    experts = jnp.concatenate(e_rows, axis=0).astype(I32)
    w_t = jnp.concatenate(w_rows + [jnp.zeros((LANES - TOP_K, n), F32)], axis=0)
    return experts, w_t.T


def _group_norm_gate(o, g):
    parts = []
    for h in range(RET_HEADS):
        oh = o[:, h * RET_V_DIM:(h + 1) * RET_V_DIM]
        mu = jnp.mean(oh, axis=-1, keepdims=True)
        ctr = oh - mu
        var = jnp.mean(ctr * ctr, axis=-1, keepdims=True)
        parts.append(ctr * lax.rsqrt(var + EPS))
    return _silu(g) * jnp.concatenate(parts, axis=-1)


def _pool_project(pooled, w_pool_ref, pool_scale_ref):
    parts = [_dot(p.astype(BF16), w_pool_ref[gi].astype(BF16)) for gi, p in enumerate(pooled)]
    return jnp.concatenate(parts, axis=-1) * pool_scale_ref[...]


def _mix_tail(x, o_gated, p, mod_ref, seq, w_out_ref, norm2_ref, wr_t_ref, bias_t_ref,
              x1_ref, h2_ref, experts_ref, gatew_ref):
    mix = jnp.concatenate([o_gated, p], axis=-1).astype(BF16)
    y = _dot(mix, w_out_ref[...])
    x1 = x + _mod(mod_ref, 2, seq) * y
    h2 = _rms(x1) * norm2_ref[...] * (1.0 + _mod(mod_ref, 4, seq)) + _mod(mod_ref, 3, seq)
    x1_ref[...] = x1
    h2_ref[...] = _pack_rows(h2)
    experts, gate_w = _route(h2, wr_t_ref, bias_t_ref)
    experts_ref[...] = experts
    gatew_ref[...] = gate_w


def _ada_kernel(c_ref, w_ref, b_ref, o_ref):
    cs = _silu(c_ref[...]).astype(BF16)
    o_ref[...] = _dot(cs, w_ref[...].astype(BF16)) + b_ref[...]


def _ada(c_all, w_ada, b_ada, block_n=1536):
    n, d = c_all.shape
    width = w_ada.shape[1]
    return pl.pallas_call(
        _ada_kernel,
        grid=(width // block_n,),
        in_specs=[pl.BlockSpec((n, d), lambda j: (0, 0)),
                  pl.BlockSpec((d, block_n), lambda j: (0, j)),
                  pl.BlockSpec((1, block_n), lambda j: (0, j))],
        out_specs=pl.BlockSpec((n, block_n), lambda j: (0, j)),
        out_shape=jax.ShapeDtypeStruct((n, width), F32),
        compiler_params=pltpu.CompilerParams(vmem_limit_bytes=VMEM_LIMIT),
        name="ada",
    )(c_all, w_ada, b_ada.reshape(1, width))


def _mix_prompt_kernel(x_ref, mod_ref, norm1_ref, w_in_ref, cos_ref, sin_ref, dmat_ref, cross_ref,
                       tail_ref, cdec_ref, w_pool_ref, pool_scale_ref, w_out_ref, norm2_ref,
                       wr_t_ref, bias_t_ref,
                       x1_ref, h2_ref, experts_ref, gatew_ref, ret_ref, pool_ref,
                       state_ref, ext_ref, win_ref, o_ref, *, block_l, chunk, seqs):
    li = pl.program_id(1)

    @pl.when(li == 0)
    def _():
        state_ref[...] = jnp.zeros_like(state_ref)
        ext_ref[:, 0:POOL_CARRY, :] = jnp.zeros((seqs, POOL_CARRY, POOL_WIDTH), F32)
        win_ref[:, 0:SUBLANES, :] = jnp.zeros((seqs, SUBLANES, POOL_WIDTH), F32)

    for seq in range(seqs):
        _mix_prompt_seq(seq, li, x_ref, mod_ref, norm1_ref, w_in_ref, cos_ref, sin_ref, dmat_ref, cross_ref,
                        tail_ref, cdec_ref, w_pool_ref, pool_scale_ref, w_out_ref, norm2_ref,
                        wr_t_ref, bias_t_ref, x1_ref, h2_ref, experts_ref, gatew_ref,
                        state_ref.at[seq], ext_ref.at[seq], win_ref.at[seq], o_ref.at[seq],
                        block_l=block_l, chunk=chunk)

    @pl.when(li == pl.num_programs(1) - 1)
    def _():
        ret_ref[...] = state_ref[...].reshape(ret_ref.shape)
        pool_ref[...] = ext_ref[:, POOL_CARRY - POOL_BUF:POOL_CARRY, :]


def _window_sums(ext_ref, win_ref, block_l):
    g = POOL_GROUP_DIM
    top = POOL_CARRY + block_l
    new = slice(POOL_CARRY - SUBLANES, None)
    s2 = ext_ref[SUBLANES:top, :] + ext_ref[SUBLANES - 1:top - 1, :]
    win_ref[SUBLANES:top, g:] = s2[:, g:]
    s4 = s2[:, g:] + win_ref[SUBLANES - 2:top - 2, g:]
    win_ref[SUBLANES:top, 2 * g:] = s4[:, g:]
    s8 = s4[:, g:] + win_ref[SUBLANES - 4:top - 4, 2 * g:]
    win_ref[SUBLANES:top, 3 * g:] = s8[:, g:]
    s16 = s8[:, g:] + win_ref[0:top - SUBLANES, 3 * g:]
    return [s2[new, 0:g], s4[new, 0:g], s8[new, 0:g], s16[new, :]]


def _mix_prompt_seq(seq, li, x_ref, mod_ref, norm1_ref, w_in_ref, cos_ref, sin_ref, dmat_ref, cross_ref,
                    tail_ref, cdec_ref, w_pool_ref, pool_scale_ref, w_out_ref, norm2_ref,
                    wr_t_ref, bias_t_ref, x1_ref, h2_ref, experts_ref, gatew_ref,
                    state_ref, ext_ref, win_ref, o_ref, *, block_l, chunk):
    x = x_ref[seq]
    h = _rms(x) * norm1_ref[...] * (1.0 + _mod(mod_ref, 1, seq)) + _mod(mod_ref, 0, seq)
    proj = _dot(h.astype(BF16), w_in_ref[...])
    q = proj[:, 0:QK_WIDTH]
    k = proj[:, QK_WIDTH:2 * QK_WIDTH]
    v = proj[:, 2 * QK_WIDTH:2 * QK_WIDTH + RET_WIDTH]
    g = proj[:, 2 * QK_WIDTH + RET_WIDTH:2 * QK_WIDTH + 2 * RET_WIDTH]
    u = proj[:, 2 * QK_WIDTH + 2 * RET_WIDTH:]

    lane = lax.broadcasted_iota(I32, q.shape, 1)
    first_half = (lane % RET_QK_DIM) < (RET_QK_DIM // 2)
    cos_t = cos_ref[...]
    sin_t = sin_ref[...]

    def rot(t):
        partner = jnp.where(first_half, pltpu.roll(t, QK_WIDTH - RET_QK_DIM // 2, axis=1),
                            pltpu.roll(t, RET_QK_DIM // 2, axis=1))
        return t * cos_t + partner * sin_t

    q = rot(q)
    k = rot(k) * (RET_QK_DIM ** -0.5)
    k_t = k.T
    v16 = v.astype(BF16)
    head_of_lane = lax.broadcasted_iota(I32, (chunk, QK_WIDTH), 1) // RET_QK_DIM

    for c in range(block_l // chunk):
        rows = slice(c * chunk, (c + 1) * chunk)
        q_c = q[rows]
        kt_c = k_t[:, rows]
        kt16 = kt_c.astype(BF16)
        state16 = state_ref[...].astype(BF16)
        for hd in range(RET_HEADS):
            in_head = head_of_lane == hd
            q_h = jnp.where(in_head, q_c, 0.0).astype(BF16)
            v_h = v16[rows, hd * RET_V_DIM:(hd + 1) * RET_V_DIM]
            scores = _dot(q_h, kt16) * dmat_ref[hd]
            inner = _dot(scores.astype(BF16), v_h)
            cross = _dot(q_h, state16) * cross_ref[hd]
            o_ref[rows, hd * RET_V_DIM:(hd + 1) * RET_V_DIM] = inner + cross
            hrows = slice(hd * RET_QK_DIM, (hd + 1) * RET_QK_DIM)
            k_dec = (kt_c[hrows] * tail_ref[hd:hd + 1, :]).astype(BF16)
            state_ref[hrows, :] = state_ref[hrows, :] * cdec_ref[hd] + _dot(k_dec, v_h)

    o_gated = _group_norm_gate(o_ref[...], g)

    ext_ref[POOL_CARRY:POOL_CARRY + block_l, :] = u
    pos = (li * block_l + lax.broadcasted_iota(I32, (block_l, 1), 0)).astype(F32)
    pooled = []
    for gi, (w, acc) in enumerate(zip(POOL_WINDOWS, _window_sums(ext_ref, win_ref, block_l))):
        cnt = jnp.minimum(pos + 1.0, float(w))
        pooled.append(acc / cnt - u[:, gi * POOL_GROUP_DIM:(gi + 1) * POOL_GROUP_DIM])
    p = _pool_project(pooled, w_pool_ref, pool_scale_ref)
    ext_ref[0:POOL_CARRY, :] = ext_ref[block_l:block_l + POOL_CARRY, :]

    _mix_tail(x, o_gated, p, mod_ref, seq, w_out_ref, norm2_ref, wr_t_ref, bias_t_ref,
              x1_ref.at[seq], h2_ref.at[seq], experts_ref.at[seq], gatew_ref.at[seq])


def _decay_tables(chunk):
    lg = jnp.log(1.0 - 2.0 ** (-5.0 - jnp.arange(RET_HEADS, dtype=F32)))
    idx = jnp.arange(chunk, dtype=F32)
    diff = idx[:, None] - idx[None, :]
    causal = diff >= 0
    dmat = jnp.where(causal[None], jnp.exp(lg[:, None, None] * jnp.where(causal, diff, 0.0)[None]), 0.0)
    cross = jnp.exp(lg[:, None] * (idx[None, :] + 1.0))
    cross = jnp.broadcast_to(cross[:, :, None], (RET_HEADS, chunk, RET_V_DIM))
    tail = jnp.exp(lg[:, None] * (chunk - 1.0 - idx)[None, :])
    cdec = jnp.broadcast_to(jnp.exp(lg * chunk)[:, None, None], (RET_HEADS, RET_QK_DIM, RET_V_DIM))
    return dmat, cross, tail, cdec


def _rotary_tables(pos):
    half = RET_QK_DIM // 2
    freqs = ROPE_BASE ** (-jnp.arange(half, dtype=F32) / half)
    ang = pos[:, None] * freqs[None, :]
    cos, sin = jnp.cos(ang), jnp.sin(ang)
    cos_t = jnp.tile(jnp.concatenate([cos, cos], axis=-1), (1, RET_HEADS))
    sin_t = jnp.tile(jnp.concatenate([-sin, sin], axis=-1), (1, RET_HEADS))
    return cos_t, sin_t


def _full(shape):
    return pl.BlockSpec(shape, lambda *_: (0,) * len(shape))


def _mix_prompt(x, mod, b0, b, norm1, w_in16, w_pool, pool_scale, w_out16, norm2, wr_t, bias_t,
                block_l=512, chunk=256, seqs=2):
    _, l, d = x.shape
    nl = l // block_l
    s0 = b0 // seqs
    cos_t, sin_t = _rotary_tables(jnp.arange(l, dtype=F32))
    dmat, cross, tail, cdec = _decay_tables(chunk)
    kernel = functools.partial(_mix_prompt_kernel, block_l=block_l, chunk=chunk, seqs=seqs)
    tok = lambda bi, li: (bi, li, 0)
    per_seq = lambda bi, li: (bi, 0, 0)
    x1, h2, experts, gate_w, ret, pool = pl.pallas_call(
        kernel,
        grid=(b // seqs, nl),
        in_specs=[pl.BlockSpec((seqs, block_l, d), lambda bi, li: (s0 + bi, li, 0)),
                  pl.BlockSpec((seqs, 6, d), lambda bi, li: (s0 + bi, 0, 0)),
                  _full((1, d)),
                  _full((d, IN_WIDTH)),
                  pl.BlockSpec((block_l, QK_WIDTH), lambda bi, li: (li, 0)),
                  pl.BlockSpec((block_l, QK_WIDTH), lambda bi, li: (li, 0)),
                  _full(dmat.shape), _full(cross.shape), _full(tail.shape), _full(cdec.shape),
                  _full(w_pool.shape), _full((1, POOL_WIDTH)), _full((d, d)), _full((1, d)),
                  _full(wr_t.shape), _full(bias_t.shape)],
        out_specs=[pl.BlockSpec((seqs, block_l, d), tok),
                   pl.BlockSpec((seqs, block_l, HALF), tok),
                   pl.BlockSpec((seqs, TOP_K, block_l), lambda bi, li: (bi, 0, li)),
                   pl.BlockSpec((seqs, block_l, LANES), tok),
                   pl.BlockSpec((seqs, RET_HEADS, RET_QK_DIM, RET_V_DIM), lambda bi, li: (bi, 0, 0, 0)),
                   pl.BlockSpec((seqs, POOL_BUF, POOL_WIDTH), per_seq)],
        out_shape=[jax.ShapeDtypeStruct((b, l, d), F32),
                   jax.ShapeDtypeStruct((b, l, HALF), U32),
                   jax.ShapeDtypeStruct((b, TOP_K, l), I32),
                   jax.ShapeDtypeStruct((b, l, LANES), F32),
                   jax.ShapeDtypeStruct((b, RET_HEADS, RET_QK_DIM, RET_V_DIM), F32),
                   jax.ShapeDtypeStruct((b, POOL_BUF, POOL_WIDTH), F32)],
        scratch_shapes=[pltpu.VMEM((seqs, QK_WIDTH, RET_V_DIM), F32),
                        pltpu.VMEM((seqs, POOL_CARRY + block_l, POOL_WIDTH), F32),
                        pltpu.VMEM((seqs, POOL_CARRY + block_l, POOL_WIDTH), F32),
                        pltpu.VMEM((seqs, block_l, RET_WIDTH), F32)],
        compiler_params=pltpu.CompilerParams(dimension_semantics=("arbitrary", "arbitrary"),
                                             vmem_limit_bytes=VMEM_LIMIT),
        name="mix_prompt",
    )(x, mod, norm1, w_in16, cos_t, sin_t, dmat, cross, tail, cdec, w_pool, pool_scale,
      w_out16, norm2, wr_t, bias_t)
    experts = jnp.transpose(experts, (1, 0, 2)).reshape(TOP_K, b * l)
    return x1, h2.reshape(b * l, HALF), experts, gate_w.reshape(b * l, LANES), ret, pool


def _mix_sample_front_kernel(x_ref, mod_ref, norm1_ref, w_in_ref, cos_ref, sin_ref,
                             qt_ref, kt_ref, v_ref, g_ref, u_ref):
    x = x_ref[...]
    h = _rms(x) * norm1_ref[...] * (1.0 + _mod(mod_ref, 1)) + _mod(mod_ref, 0)
    proj = _dot(h.astype(BF16), w_in_ref[...])
    half = RET_QK_DIM // 2
    cos_c = cos_ref[...]
    sin_c = sin_ref[...]

    def rot_t(t):
        parts = []
        for hd in range(RET_HEADS):
            t1 = t[hd * RET_QK_DIM:hd * RET_QK_DIM + half]
            t2 = t[hd * RET_QK_DIM + half:(hd + 1) * RET_QK_DIM]
            parts += [t1 * cos_c - t2 * sin_c, t1 * sin_c + t2 * cos_c]
        return jnp.concatenate(parts, axis=0)

    qt_ref[...] = rot_t(proj[:, 0:QK_WIDTH].T)
    kt_ref[...] = rot_t(proj[:, QK_WIDTH:2 * QK_WIDTH].T) * (RET_QK_DIM ** -0.5)
    v_ref[...] = proj[:, 2 * QK_WIDTH:2 * QK_WIDTH + RET_WIDTH]
    g_ref[...] = proj[:, 2 * QK_WIDTH + RET_WIDTH:2 * QK_WIDTH + 2 * RET_WIDTH]
    u_ref[...] = proj[:, 2 * QK_WIDTH + 2 * RET_WIDTH:]


def _ret_step_kernel(qt_ref, kt_ref, v_ref, s0_ref, o_ref, s1_ref, *, block_b, decays):
    i = pl.program_id(0)
    lane = lax.broadcasted_iota(I32, qt_ref.shape, 1)
    for j in range(block_b):
        bi = i * block_b + j
        here = lane == bi
        q_col = jnp.sum(jnp.where(here, qt_ref[...], 0.0), axis=1, keepdims=True)
        k_col = jnp.sum(jnp.where(here, kt_ref[...], 0.0), axis=1, keepdims=True)
        v_row = v_ref[pl.ds(bi, 1), :]
        outs = []
        for hd in range(RET_HEADS):
            hrows = slice(hd * RET_QK_DIM, (hd + 1) * RET_QK_DIM)
            s1 = decays[hd] * s0_ref[j, hd] + k_col[hrows] * v_row[:, hd * RET_V_DIM:(hd + 1) * RET_V_DIM]
            s1_ref[j, hd] = s1
            outs.append(jnp.sum(q_col[hrows] * s1, axis=0, keepdims=True))
        o_ref[pl.ds(bi, 1), :] = jnp.concatenate(outs, axis=-1)


def _mix_sample_back_kernel(x_ref, mod_ref, o_ref, g_ref, u_ref, buf_ref, w_pool_ref, pool_scale_ref,
                            w_out_ref, norm2_ref, wr_t_ref, bias_t_ref,
                            x1_ref, h2_ref, experts_ref, gatew_ref, pool_ref):
    o_gated = _group_norm_gate(o_ref[...], g_ref[...])
    u = u_ref[...]
    pooled = []
    for gi, w in enumerate(POOL_WINDOWS):
        lanes = slice(gi * POOL_GROUP_DIM, (gi + 1) * POOL_GROUP_DIM)
        acc = u[:, lanes]
        for j in range(1, w):
            acc = acc + buf_ref[:, POOL_BUF - j, lanes]
        pooled.append(acc / float(w) - u[:, lanes])
    p = _pool_project(pooled, w_pool_ref, pool_scale_ref)
    pool_ref[:, 0:POOL_BUF - 1, :] = buf_ref[:, 1:POOL_BUF, :]
    pool_ref[:, POOL_BUF - 1, :] = u
    _mix_tail(x_ref[...], o_gated, p, mod_ref, 0, w_out_ref, norm2_ref, wr_t_ref, bias_t_ref,
              x1_ref, h2_ref, experts_ref, gatew_ref)


def _mix_sample(x, mod, state_ret, state_pool, start, norm1, w_in16, w_pool,
                pool_scale, w_out16, norm2, wr_t, bias_t, block_b=32):
    n, d = x.shape
    half = RET_QK_DIM // 2
    freqs = ROPE_BASE ** (-jnp.arange(half, dtype=F32) / half)
    ang = jnp.full((1,), start, F32)[:, None] * freqs[None, :]
    cos_c = jnp.broadcast_to(jnp.cos(ang).T, (half, n))
    sin_c = jnp.broadcast_to(jnp.sin(ang).T, (half, n))
    params = pltpu.CompilerParams(vmem_limit_bytes=VMEM_LIMIT)
    qt, kt, v, g, u = pl.pallas_call(
        _mix_sample_front_kernel,
        out_shape=[jax.ShapeDtypeStruct((QK_WIDTH, n), F32), jax.ShapeDtypeStruct((QK_WIDTH, n), F32),
                   jax.ShapeDtypeStruct((n, RET_WIDTH), F32), jax.ShapeDtypeStruct((n, RET_WIDTH), F32),
                   jax.ShapeDtypeStruct((n, POOL_WIDTH), F32)],
        compiler_params=params,
        name="mix_sample_front",
    )(x, mod, norm1, w_in16, cos_c, sin_c)

    lg = np.log(1.0 - 2.0 ** (-5.0 - np.arange(RET_HEADS, dtype=np.float32)), dtype=np.float32)
    decays = tuple(float(np.exp(lg[h])) for h in range(RET_HEADS))
    state_block = (block_b, RET_HEADS, RET_QK_DIM, RET_V_DIM)
    o, s1 = pl.pallas_call(
        functools.partial(_ret_step_kernel, block_b=block_b, decays=decays),
        grid=(n // block_b,),
        in_specs=[_full((QK_WIDTH, n)), _full((QK_WIDTH, n)), _full((n, RET_WIDTH)),
                  pl.BlockSpec(state_block, lambda i: (i, 0, 0, 0))],
        out_specs=[_full((n, RET_WIDTH)), pl.BlockSpec(state_block, lambda i: (i, 0, 0, 0))],
        out_shape=[jax.ShapeDtypeStruct((n, RET_WIDTH), F32),
                   jax.ShapeDtypeStruct(state_ret.shape, F32)],
        compiler_params=pltpu.CompilerParams(dimension_semantics=("arbitrary",),
                                             vmem_limit_bytes=VMEM_LIMIT),
        name="ret_step",
    )(qt, kt, v, state_ret)

    x1, h2, experts, gate_w, pool = pl.pallas_call(
        _mix_sample_back_kernel,
        out_shape=[jax.ShapeDtypeStruct((n, d), F32),
                   jax.ShapeDtypeStruct((n, HALF), U32),
                   jax.ShapeDtypeStruct((TOP_K, n), I32),
                   jax.ShapeDtypeStruct((n, LANES), F32),
                   jax.ShapeDtypeStruct(state_pool.shape, F32)],
        compiler_params=params,
        name="mix_sample_back",
    )(x, mod, o, g, u, state_pool, w_pool, pool_scale, w_out16, norm2, wr_t, bias_t)
    return x1, h2, experts, gate_w, s1, pool


def _plan_kernel(experts_ref, pos_ref, meta_ref, cnt_ref, carry_ref, off_ref, *, block_t):
    phase = pl.program_id(0)
    j = pl.program_id(1)
    e_blk = experts_ref[...]
    eidx = lax.broadcasted_iota(I32, (N_EXPERTS, block_t), 0)
    member = jnp.zeros((N_EXPERTS, block_t), F32)
    for s in range(TOP_K):
        member = member + jnp.where(eidx == e_blk[s:s + 1, :], 1.0, 0.0)
    per_expert = jnp.broadcast_to(jnp.sum(member, axis=1, keepdims=True), (N_EXPERTS, LANES))

    @pl.when((phase == 0) & (j == 0))
    def _():
        cnt_ref[...] = jnp.zeros_like(cnt_ref)

    @pl.when(phase == 0)
    def _():
        cnt_ref[...] += per_expert

    @pl.when((phase == 0) & (j == pl.num_programs(1) - 1))
    def _():
        cnt = cnt_ref[...]
        n_tile = jnp.floor((cnt + (ROW_TILE - 1.0)) * (1.0 / ROW_TILE))
        upto = (lax.broadcasted_iota(I32, (N_EXPERTS, N_EXPERTS), 1)
                <= lax.broadcasted_iota(I32, (N_EXPERTS, N_EXPERTS), 0))
        tile_end = _dot(jnp.where(upto, 1.0, 0.0).astype(BF16), n_tile.astype(BF16))
        tile_start = tile_end - n_tile
        off_ref[...] = tile_start * ROW_TILE
        carry_ref[...] = jnp.zeros_like(carry_ref)
        lane = lax.broadcasted_iota(I32, cnt.shape, 1)
        meta_ref[...] = jnp.where(lane == 0, tile_start, jnp.where(lane == 1, n_tile, cnt)).astype(I32)

    @pl.when(phase == 1)
    def _():
        before = (lax.broadcasted_iota(I32, (block_t, block_t), 0)
                  < lax.broadcasted_iota(I32, (block_t, block_t), 1))
        rank = _dot(member.astype(BF16), jnp.where(before, 1.0, 0.0).astype(BF16))
        row = off_ref[:, 0:1] + carry_ref[:, 0:1] + rank
        carry_ref[...] += per_expert
        out = [jnp.sum(jnp.where(eidx == e_blk[s:s + 1, :], row, 0.0), axis=0, keepdims=True)
               for s in range(TOP_K)]
        pos_ref[...] = jnp.concatenate(out, axis=0).astype(I32)


def _plan(experts_all, max_block=1024):
    n_tokens = experts_all.shape[1]
    block_t = max(k for k in range(LANES, max_block + 1, LANES) if n_tokens % k == 0)
    nb = n_tokens // block_t
    return pl.pallas_call(
        functools.partial(_plan_kernel, block_t=block_t),
        grid=(2, nb),
        in_specs=[pl.BlockSpec((TOP_K, block_t), lambda ph, j: (0, j))],
        out_specs=[pl.BlockSpec((TOP_K, block_t), lambda ph, j: (0, j * ph)),
                   _full((N_EXPERTS, LANES))],
        out_shape=[jax.ShapeDtypeStruct((TOP_K, n_tokens), I32),
                   jax.ShapeDtypeStruct((N_EXPERTS, LANES), I32)],
        scratch_shapes=[pltpu.VMEM((N_EXPERTS, LANES), F32)] * 3,
        compiler_params=pltpu.CompilerParams(dimension_semantics=("arbitrary", "arbitrary"),
                                             vmem_limit_bytes=VMEM_LIMIT),
        name="plan",
    )(experts_all)


def _sc_workers():
    info = plsc.get_sparse_core_info()
    return info.num_cores, info.num_cores * info.num_subcores


def _sc_scatter_rows(sources, pos_t, n_out):
    w = sources[0].shape[1]
    s = pos_t.shape[0]
    n_cores, n_workers = _sc_workers()
    bounds = np.cumsum([0] + [src.shape[0] // SC_CHUNK for src in sources])
    n_chunks = int(bounds[-1])
    iters = -(-n_chunks // n_workers)
    mesh = plsc.VectorSubcoreMesh(core_axis_name="c", subcore_axis_name="s")

    @functools.partial(
        pl.kernel, mesh=mesh, out_type=jax.ShapeDtypeStruct((n_out, w), sources[0].dtype),
        scratch_types=[pltpu.VMEM((SC_CHUNK, w), sources[0].dtype), pltpu.VMEM((s, SC_CHUNK), I32),
                       pltpu.SemaphoreType.DMA],
        name="dispatch")
    def k(*refs):
        src_hbm, (pos_hbm, out_hbm, rows_v, idx_v, sem) = refs[:len(sources)], refs[len(sources):]
        wid = lax.axis_index("s") * n_cores + lax.axis_index("c")

        @pl.loop(0, iters)
        def _(it):
            c = it * n_workers + wid
            for src, lo, hi in zip(src_hbm, bounds[:-1], bounds[1:]):
                @pl.when((c >= int(lo)) & (c < int(hi)))
                def _():
                    base = pl.multiple_of((c - int(lo)) * SC_CHUNK, SC_CHUNK)
                    pltpu.sync_copy(src.at[pl.ds(base, SC_CHUNK)], rows_v)

            @pl.when(c < n_chunks)
            def _():
                base = pl.multiple_of(c * SC_CHUNK, SC_CHUNK)
                pltpu.sync_copy(pos_hbm.at[:, pl.ds(base, SC_CHUNK)], idx_v)
                copies = [pltpu.async_copy(rows_v, out_hbm.at[idx_v.at[j]], sem) for j in range(s)]
                for cp in copies:
                    cp.wait()

    return k(*sources, pos_t)


def _sc_gather_rows(table, pos_t):
    _, w = table.shape
    s, t = pos_t.shape
    n_cores, n_workers = _sc_workers()
    n_chunks = t // SC_CHUNK
    iters = -(-n_chunks // n_workers)
    mesh = plsc.VectorSubcoreMesh(core_axis_name="c", subcore_axis_name="s")

    @functools.partial(
        pl.kernel, mesh=mesh, out_type=jax.ShapeDtypeStruct((s, t, w), table.dtype),
        scratch_types=[pltpu.VMEM((SC_CHUNK, w), table.dtype), pltpu.VMEM((s, SC_CHUNK), I32),
                       pltpu.SemaphoreType.DMA],
        name="combine")
    def k(table_hbm, pos_hbm, out_hbm, rows_v, idx_v, sem):
        wid = lax.axis_index("s") * n_cores + lax.axis_index("c")

        @pl.loop(0, iters)
        def _(it):
            c = it * n_workers + wid

            @pl.when(c < n_chunks)
            def _():
                base = pl.multiple_of(c * SC_CHUNK, SC_CHUNK)
                pltpu.sync_copy(pos_hbm.at[:, pl.ds(base, SC_CHUNK)], idx_v)
                for j in range(s):
                    pltpu.async_copy(table_hbm.at[idx_v.at[j]], rows_v, sem).wait()
                    pltpu.sync_copy(rows_v, out_hbm.at[j, pl.ds(base, SC_CHUNK)])

    return k(table, pos_t)


def _sc_pack_weights(w, rows_per_item):
    e, r, c = w.shape
    half = r // 2
    rb = rows_per_item
    per_expert = half // rb
    n_cores, n_workers = _sc_workers()
    per_worker = e * per_expert // n_workers
    assert per_worker * n_workers == e * per_expert and per_worker % 2 == 0 and c % (SC_LANES * SC_UNROLL) == 0
    mesh = plsc.VectorSubcoreMesh(core_axis_name="c", subcore_axis_name="s")

    @functools.partial(
        pl.kernel, mesh=mesh, out_type=jax.ShapeDtypeStruct((e * half, c), U32),
        scratch_types=[pltpu.VMEM((2, rb, c), F32), pltpu.VMEM((2, rb, c), F32), pltpu.VMEM((2, rb, c), U32),
                       pltpu.SemaphoreType.DMA((2,)), pltpu.SemaphoreType.DMA((2,))],
        compiler_params=pltpu.CompilerParams(needs_layout_passes=False),
        name="pack_weights")
    def k(w_hbm, out_hbm, lo_v, hi_v, out_v, in_sem, out_sem):
        wid = lax.axis_index("s") * n_cores + lax.axis_index("c")
        first = wid * per_worker

        def rows(item):
            ex = item // per_expert
            j = item - ex * per_expert
            return (pl.multiple_of(ex * r + j * rb, rb), pl.multiple_of(ex * r + half + j * rb, rb),
                    pl.multiple_of(ex * half + j * rb, rb))

        def loads(item, b):
            lo_row, hi_row, _ = rows(item)
            return (pltpu.make_async_copy(w_hbm.at[pl.ds(lo_row, rb)], lo_v.at[b], in_sem.at[b]),
                    pltpu.make_async_copy(w_hbm.at[pl.ds(hi_row, rb)], hi_v.at[b], in_sem.at[b]))

        def store(item, b):
            return pltpu.make_async_copy(out_v.at[b], out_hbm.at[pl.ds(rows(item)[2], rb)], out_sem.at[b])

        for cp in loads(first, 0):
            cp.start()

        @pl.loop(0, per_worker // 2)
        def _(pair):
            for b in range(2):
                item = first + pair * 2 + b
                for cp in loads(item, b):
                    cp.wait()

                @pl.when(item + 1 < first + per_worker)
                def _():
                    for cp in loads(item + 1, 1 - b):
                        cp.start()

                @pl.when(pair > 0)
                def _():
                    store(item - 2, b).wait()

                @pl.loop(0, rb)
                def _(i):
                    @pl.loop(0, c // (SC_LANES * SC_UNROLL))
                    def _(vb):
                        for u in range(SC_UNROLL):
                            sl = pl.ds(pl.multiple_of((vb * SC_UNROLL + u) * SC_LANES, SC_LANES), SC_LANES)
                            packed = plsc.pack(lo_v[b, i, sl], hi_v[b, i, sl], format=plsc.PackFormat.INTERLEAVED)
                            out_v[b, i, sl] = plsc.bitcast(packed, U32)

                store(item, b).start()

        for b in range(2):
            store(first + per_worker - 2 + b, b).wait()

    return k(w.reshape(e * r, c)).reshape(e, half, c)


def _experts_kernel(first_ref, ntile_ref, cnt_ref, xs_hbm, wg_ref, wu_ref, wd_ref, ys_hbm,
                    wg16_ref, wu16_ref, wd16_ref, x_buf, y_buf, in_sem, out_sem):
    e = pl.program_id(0)
    n_used = first_ref[N_EXPERTS - 1] + ntile_ref[N_EXPERTS - 1]
    first, n_mine, count = first_ref[e], ntile_ref[e], cnt_ref[e]

    def tile_rows(g):
        return pl.ds(pl.multiple_of(g * ROW_TILE, ROW_TILE), ROW_TILE)

    def load(g):
        slot = lax.rem(g, STREAM_DEPTH)
        return pltpu.make_async_copy(xs_hbm.at[tile_rows(g)], x_buf.at[slot], in_sem.at[slot])

    def store(g):
        slot = lax.rem(g, STREAM_DEPTH)
        return pltpu.make_async_copy(y_buf.at[slot], ys_hbm.at[tile_rows(g)], out_sem.at[slot])

    @pl.when(e == 0)
    def _():
        for g0 in range(STREAM_DEPTH - 1):
            @pl.when(g0 < n_used)
            def _():
                load(g0).start()

    for packed_ref, w16_ref in ((wg_ref, wg16_ref), (wu_ref, wu16_ref), (wd_ref, wd16_ref)):
        rows = packed_ref.shape[1]
        lo, hi = _unpack_rows(packed_ref[0])
        w16_ref[0:rows, :] = lo.astype(BF16)
        w16_ref[rows:, :] = hi.astype(BF16)

    def tile(j, carry):
        g = first + j
        slot = lax.rem(g, STREAM_DEPTH)
        load(g).wait()

        @pl.when(g + (STREAM_DEPTH - 1) < n_used)
        def _():
            load(g + (STREAM_DEPTH - 1)).start()

        @pl.when(g >= STREAM_DEPTH)
        def _():
            store(g - STREAM_DEPTH).wait()

        words = x_buf[slot]
        row = lax.broadcasted_iota(I32, words.shape, 0)
        words = jnp.where(row < count - j * ROW_TILE, words, jnp.uint32(0))
        lo, hi = _unpack_rows(words)
        lo, hi = lo.astype(BF16), hi.astype(BF16)
        hg = _dot(lo, wg16_ref[0:HALF, :]) + _dot(hi, wg16_ref[HALF:, :])
        hu = _dot(lo, wu16_ref[0:HALF, :]) + _dot(hi, wu16_ref[HALF:, :])
        a = (_silu(hg) * hu).astype(BF16)
        y_buf[slot] = _pack_rows(_dot(a, wd16_ref[...]))
        store(g).start()
        return carry

    lax.fori_loop(0, n_mine, tile, 0)

    @pl.when(e == N_EXPERTS - 1)
    def _():
        for back in range(STREAM_DEPTH, 0, -1):
            @pl.when(n_used >= back)
            def _():
                store(n_used - back).wait()


def _experts(xs, first_tile, n_tile, count, w_eg, w_eu, w_ed):
    d = D_MODEL
    by_expert = lambda e, *_: (e, 0, 0)
    grid_spec = pltpu.PrefetchScalarGridSpec(
        num_scalar_prefetch=3,
        grid=(N_EXPERTS,),
        in_specs=[pl.BlockSpec(memory_space=pl.ANY),
                  pl.BlockSpec((1, d // 2, EXPERT_DIM), by_expert),
                  pl.BlockSpec((1, d // 2, EXPERT_DIM), by_expert),
                  pl.BlockSpec((1, EXPERT_DIM // 2, d), by_expert)],
        out_specs=pl.BlockSpec(memory_space=pl.ANY),
        scratch_shapes=[pltpu.VMEM((d, EXPERT_DIM), BF16), pltpu.VMEM((d, EXPERT_DIM), BF16),
                        pltpu.VMEM((EXPERT_DIM, d), BF16),
                        pltpu.VMEM((STREAM_DEPTH, ROW_TILE, HALF), U32),
                        pltpu.VMEM((STREAM_DEPTH, ROW_TILE, HALF), U32),
                        pltpu.SemaphoreType.DMA((STREAM_DEPTH,)), pltpu.SemaphoreType.DMA((STREAM_DEPTH,))])
    return pl.pallas_call(
        _experts_kernel,
        grid_spec=grid_spec,
        out_shape=jax.ShapeDtypeStruct(xs.shape, U32),
        compiler_params=pltpu.CompilerParams(dimension_semantics=("arbitrary",),
                                             vmem_limit_bytes=VMEM_LIMIT),
        name="experts",
    )(first_tile, n_tile, count, xs, w_eg, w_eu, w_ed)


def _final_kernel(z_ref, gatew_ref, h2_ref, x1_ref, mod_ref, normf_ref, wsg_ref, wsu_ref, wsd_ref, *rest):
    y_ref = rest[-1]
    lo, hi = _unpack_rows(h2_ref[...])
    h = jnp.concatenate([lo, hi], axis=-1).astype(BF16)
    a = _silu(_dot(h, wsg_ref[...])) * _dot(h, wsu_ref[...])
    acc = _dot(a.astype(BF16), wsd_ref[...])
    for s in range(TOP_K):
        lo, hi = _unpack_rows(z_ref[s])
        acc = acc + gatew_ref[:, s:s + 1] * jnp.concatenate([lo, hi], axis=-1)
    x2 = x1_ref[...] + _mod(mod_ref, 5) * acc
    y_ref[...] = _rms(x2) * normf_ref[...]


def _final(z, gate_w, h2, x1, mod, norm_f, w_sg16, w_su16, w_sd16, block_t, first_block, per_seq,
           seq0=0, out_rows=None, y_prev=None):
    t, d = x1.shape
    out_rows = t if out_rows is None else out_rows
    out_first = seq0 * per_seq
    tok = lambda i: (i, 0)
    if per_seq:
        mod_spec = pl.BlockSpec((1, 6, d), lambda i: (seq0 + i // per_seq, 0, 0))
    else:
        mod_spec = pl.BlockSpec((block_t, 6 * d), tok)
    operands = [z, gate_w, h2, x1, mod, norm_f, w_sg16, w_su16, w_sd16]
    in_specs = [pl.BlockSpec((TOP_K, block_t, HALF), lambda i: (0, first_block + i, 0)),
                pl.BlockSpec((block_t, LANES), tok),
                pl.BlockSpec((block_t, HALF), tok),
                pl.BlockSpec((block_t, d), tok),
                mod_spec,
                _full((1, d)),
                _full((d, EXPERT_DIM)), _full((d, EXPERT_DIM)), _full((EXPERT_DIM, d))]
    aliases = {}
    if y_prev is not None:
        aliases = {len(operands): 0}
        operands.append(y_prev)
        in_specs.append(pl.BlockSpec(memory_space=pl.ANY))
    return pl.pallas_call(
        _final_kernel,
        grid=(t // block_t,),
        in_specs=in_specs,
        out_specs=pl.BlockSpec((block_t, d), lambda i: (out_first + i, 0)),
        out_shape=jax.ShapeDtypeStruct((out_rows, d), F32),
        input_output_aliases=aliases,
        compiler_params=pltpu.CompilerParams(dimension_semantics=("arbitrary",),
                                             vmem_limit_bytes=VMEM_LIMIT),
        name="final",
    )(*operands)


def kernel(x_prompt, x_sample, c_prompt, c_sample, state_ret, state_pool, norm1, norm2, norm_f,
           w_ada, b_ada, w_in, w_out, w_pool, pool_scale, w_router, router_bias, w_exp_gate,
           w_exp_up, w_exp_down, w_sh_gate, w_sh_up, w_sh_down):
    b, l, d = x_prompt.shape
    n = x_sample.shape[0]
    past_len = 16384

    mod = _ada(jnp.concatenate([c_prompt, c_sample], axis=0), w_ada[0], b_ada[0])
    mod_p = mod[:b].reshape(b, 6, d)
    mod_s = mod[b:]

    w_in16 = w_in[0].astype(BF16)
    w_out16 = w_out[0].astype(BF16)
    wr_t = w_router[0].T
    bias_t = jnp.broadcast_to(router_bias[0][:, None], (N_EXPERTS, LANES))
    n1, n2, nf = norm1[0].reshape(1, d), norm2[0].reshape(1, d), norm_f.reshape(1, d)
    ps = pool_scale[0].reshape(1, POOL_WIDTH)
    shared = (w_sh_gate[0].astype(BF16), w_sh_up[0].astype(BF16), w_sh_down[0].astype(BF16))

    def routed(sources, experts):
        n_tiles = experts.shape[1] * TOP_K // ROW_TILE + N_EXPERTS
        pos_t, meta = _plan(experts)
        xs = _sc_scatter_rows(sources, pos_t, n_tiles * ROW_TILE)
        ys = _experts(xs, meta[:, 0], meta[:, 1], meta[:, 2], *expert_w)
        return _sc_gather_rows(ys, pos_t)

    expert_w = (_sc_pack_weights(w_exp_gate[0], 64), _sc_pack_weights(w_exp_up[0], 64),
                _sc_pack_weights(w_exp_down[0], 16))

    ba = b // 2
    bb = b - ba
    mix_args = (n1, w_in16, w_pool[0], ps, w_out16, n2, wr_t, bias_t)
    x1_a, h2_a, experts_a, gatew_a, ret_a, pool_a = _mix_prompt(x_prompt, mod_p, 0, ba, *mix_args)
    z_a = routed((h2_a,), experts_a)
    x1_b, h2_b, experts_b, gatew_b, ret_b, pool_b = _mix_prompt(x_prompt, mod_p, ba, bb, *mix_args)
    x1_s, h2_s, experts_s, gatew_s, ret_s, pool_s = _mix_sample(
        x_sample.reshape(n, d), mod_s, state_ret[0], state_pool[0], float(past_len), *mix_args)
    z_b = routed((h2_b, h2_s), jnp.concatenate([experts_b, experts_s], axis=1))

    block_t = 256
    per_seq = l // block_t
    y_s = _final(z_b, gatew_s, h2_s, x1_s, mod_s, nf, *shared,
                 block_t=n, first_block=bb * l // n, per_seq=0)
    y_p = _final(z_b, gatew_b, h2_b, x1_b.reshape(bb * l, d), mod_p, nf, *shared,
                 block_t=block_t, first_block=0, per_seq=per_seq, seq0=ba, out_rows=b * l)
    y_p = _final(z_a, gatew_a, h2_a, x1_a.reshape(ba * l, d), mod_p, nf, *shared,
                 block_t=block_t, first_block=0, per_seq=per_seq, seq0=0, out_rows=b * l, y_prev=y_p)

    ret_p = jnp.concatenate([ret_a, ret_b], axis=0)
    pool_p = jnp.concatenate([pool_a, pool_b], axis=0)
    return (y_p.reshape(b, l, d), y_s.reshape(n, 1, d), ret_p[None], pool_p[None],
            ret_s[None], pool_s[None])
```

```python
import functools

import jax
import jax.numpy as jnp
import numpy as np
from jax import lax
from jax.experimental import pallas as pl
from jax.experimental.pallas import tpu as pltpu
from jax.experimental.pallas import tpu_sc as plsc

D_MODEL = 1024
RET_HEADS = 4
RET_QK_DIM = 64
RET_V_DIM = 128
RET_WIDTH = RET_HEADS * RET_V_DIM
QK_WIDTH = RET_HEADS * RET_QK_DIM
ROPE_BASE = 10000.0
POOL_WINDOWS = (2, 4, 8, 16)
POOL_WIDTH = 512
POOL_GROUP_DIM = 128
POOL_BUF = 15
IN_WIDTH = 2 * QK_WIDTH + 2 * RET_WIDTH + POOL_WIDTH
N_EXPERTS = 64
TOP_K = 8
N_EXPERT_GROUPS = 8
GROUP_SIZE = N_EXPERTS // N_EXPERT_GROUPS
TOP_GROUPS = 4
EXPERT_DIM = 256
ROUTE_SCALE = 2.5
EPS = 1e-6

LANES = 128
SUBLANES = 8
POOL_CARRY = 24
VMEM_LIMIT = 56 * 1024 * 1024
HALF = D_MODEL // 2
ROW_TILE = 256
SC_CHUNK = 128
SC_LANES = 16
SC_UNROLL = 16
STREAM_DEPTH = 8

BF16 = jnp.bfloat16
F32 = jnp.float32
U32 = jnp.uint32
I32 = jnp.int32


def _silu(x):
    return x * jax.nn.sigmoid(x)


def _dot(a, b):
    return jnp.dot(a, b, preferred_element_type=F32)


def _rms(x):
    return x * lax.rsqrt(jnp.mean(x * x, axis=-1, keepdims=True) + EPS)


def _mod(mod_ref, i, seq=0):
    if len(mod_ref.shape) == 3:
        return mod_ref[seq, i:i + 1, :]
    return mod_ref[:, i * D_MODEL:(i + 1) * D_MODEL]


def _split_bf16(x):
    hi = x.astype(BF16)
    lo = (x - hi.astype(F32)).astype(BF16)
    return hi, lo


def _pack_rows(x):
    lo = lax.bitcast_convert_type(x[:, :HALF].astype(BF16).astype(F32), U32)
    hi = lax.bitcast_convert_type(x[:, HALF:].astype(BF16).astype(F32), U32)
    return (hi & jnp.uint32(0xFFFF0000)) | (lo >> jnp.uint32(16))


def _unpack_rows(w):
    lo = lax.bitcast_convert_type(w << jnp.uint32(16), F32)
    hi = lax.bitcast_convert_type(w & jnp.uint32(0xFFFF0000), F32)
    return lo, hi


def _first_max_onehot(work, idx, n):
    m = jnp.max(work, axis=0, keepdims=True)
    first = jnp.min(jnp.where(work == m, idx, float(n)), axis=0, keepdims=True)
    return idx == first


def _route(h2, wr_t_ref, bias_t_ref):
    n = h2.shape[0]
    h_hi, h_lo = _split_bf16(h2)
    w_hi, w_lo = _split_bf16(wr_t_ref[...])
    nt = (((1,), (1,)), ((), ()))
    logits = (lax.dot_general(w_hi, h_hi, nt, preferred_element_type=F32)
              + lax.dot_general(w_hi, h_lo, nt, preferred_element_type=F32)
              + lax.dot_general(w_lo, h_hi, nt, preferred_element_type=F32))
    scores = jax.nn.sigmoid(logits)
    biased = scores + bias_t_ref[:, 0:1]
    b3 = biased.reshape(N_EXPERT_GROUPS, GROUP_SIZE, n)
    i3 = lax.broadcasted_iota(I32, b3.shape, 1).astype(F32)
    m1 = jnp.max(b3, axis=1, keepdims=True)
    first = jnp.min(jnp.where(b3 == m1, i3, float(GROUP_SIZE)), axis=1, keepdims=True)
    m2 = jnp.max(jnp.where(i3 == first, -jnp.inf, b3), axis=1, keepdims=True)
    gscore = (m1 + m2).reshape(N_EXPERT_GROUPS, n)
    gidx = lax.broadcasted_iota(I32, gscore.shape, 0).astype(F32)
    gsel = jnp.zeros(gscore.shape, F32)
    work = gscore
    for _ in range(TOP_GROUPS):
        hit = _first_max_onehot(work, gidx, N_EXPERT_GROUPS)
        gsel = jnp.where(hit, 1.0, gsel)
        work = jnp.where(hit, -jnp.inf, work)
    gsel3 = jnp.broadcast_to(gsel.reshape(N_EXPERT_GROUPS, 1, n), b3.shape)
    work = jnp.where(gsel3 > 0.0, b3, -jnp.inf).reshape(N_EXPERTS, n)
    eidx = lax.broadcasted_iota(I32, work.shape, 0).astype(F32)
    sel = jnp.zeros(work.shape, F32)
    for _ in range(TOP_K):
        hit = _first_max_onehot(work, eidx, N_EXPERTS)
        sel = jnp.where(hit, 1.0, sel)
        work = jnp.where(hit, -jnp.inf, work)
    picked = jnp.where(sel > 0.0, scores, 0.0)
    gates = picked / jnp.sum(picked, axis=0, keepdims=True) * ROUTE_SCALE
    below = (lax.broadcasted_iota(I32, (N_EXPERTS, N_EXPERTS), 1)
             < lax.broadcasted_iota(I32, (N_EXPERTS, N_EXPERTS), 0))
    slot = _dot(jnp.where(below, 1.0, 0.0).astype(BF16), sel.astype(BF16))
    e_rows, w_rows = [], []
    for s in range(TOP_K):
        here = jnp.where(slot == float(s), sel, 0.0)
        e_rows.append(jnp.sum(here * eidx, axis=0, keepdims=True))
        w_rows.append(jnp.sum(here * gates, axis=0, keepdims=True))
    experts = jnp.concatenate(e_rows, axis=0).astype(I32)
    w_t = jnp.concatenate(w_rows + [jnp.zeros((LANES - TOP_K, n), F32)], axis=0)
    return experts, w_t.T


def _group_norm_gate(o, g):
    parts = []
    for h in range(RET_HEADS):
        oh = o[:, h * RET_V_DIM:(h + 1) * RET_V_DIM]
        mu = jnp.mean(oh, axis=-1, keepdims=True)
        ctr = oh - mu
        var = jnp.mean(ctr * ctr, axis=-1, keepdims=True)
        parts.append(ctr * lax.rsqrt(var + EPS))
    return _silu(g) * jnp.concatenate(parts, axis=-1)


def _pool_project(pooled, w_pool_ref, pool_scale_ref):
    parts = [_dot(p.astype(BF16), w_pool_ref[gi].astype(BF16)) for gi, p in enumerate(pooled)]
    return jnp.concatenate(parts, axis=-1) * pool_scale_ref[...]


def _mix_tail(x, o_gated, p, mod_ref, seq, w_out_ref, norm2_ref, wr_t_ref, bias_t_ref,
              x1_ref, h2_ref, experts_ref, gatew_ref):
    mix = jnp.concatenate([o_gated, p], axis=-1).astype(BF16)
    y = _dot(mix, w_out_ref[...])
    x1 = x + _mod(mod_ref, 2, seq) * y
    h2 = _rms(x1) * norm2_ref[...] * (1.0 + _mod(mod_ref, 4, seq)) + _mod(mod_ref, 3, seq)
    x1_ref[...] = x1
    h2_ref[...] = _pack_rows(h2)
    experts, gate_w = _route(h2, wr_t_ref, bias_t_ref)
    experts_ref[...] = experts
    gatew_ref[...] = gate_w


def _ada_kernel(c_ref, w_ref, b_ref, o_ref):
    cs = _silu(c_ref[...]).astype(BF16)
    o_ref[...] = _dot(cs, w_ref[...].astype(BF16)) + b_ref[...]


def _ada(c_all, w_ada, b_ada, block_n=1536):
    n, d = c_all.shape
    width = w_ada.shape[1]
    return pl.pallas_call(
        _ada_kernel,
        grid=(width // block_n,),
        in_specs=[pl.BlockSpec((n, d), lambda j: (0, 0)),
                  pl.BlockSpec((d, block_n), lambda j: (0, j)),
                  pl.BlockSpec((1, block_n), lambda j: (0, j))],
        out_specs=pl.BlockSpec((n, block_n), lambda j: (0, j)),
        out_shape=jax.ShapeDtypeStruct((n, width), F32),
        compiler_params=pltpu.CompilerParams(vmem_limit_bytes=VMEM_LIMIT),
        name="ada",
    )(c_all, w_ada, b_ada.reshape(1, width))


def _mix_prompt_kernel(x_ref, mod_ref, norm1_ref, w_in_ref, cos_ref, sin_ref, dmat_ref, cross_ref,
                       tail_ref, cdec_ref, w_pool_ref, pool_scale_ref, w_out_ref, norm2_ref,
                       wr_t_ref, bias_t_ref,
                       x1_ref, h2_ref, experts_ref, gatew_ref, ret_ref, pool_ref,
                       state_ref, ext_ref, win_ref, o_ref, *, block_l, chunk, seqs):
    li = pl.program_id(1)

    @pl.when(li == 0)
    def _():
        state_ref[...] = jnp.zeros_like(state_ref)
        ext_ref[:, 0:POOL_CARRY, :] = jnp.zeros((seqs, POOL_CARRY, POOL_WIDTH), F32)
        win_ref[:, 0:SUBLANES, :] = jnp.zeros((seqs, SUBLANES, POOL_WIDTH), F32)

    for seq in range(seqs):
        _mix_prompt_seq(seq, li, x_ref, mod_ref, norm1_ref, w_in_ref, cos_ref, sin_ref, dmat_ref, cross_ref,
                        tail_ref, cdec_ref, w_pool_ref, pool_scale_ref, w_out_ref, norm2_ref,
                        wr_t_ref, bias_t_ref, x1_ref, h2_ref, experts_ref, gatew_ref,
                        state_ref.at[seq], ext_ref.at[seq], win_ref.at[seq], o_ref.at[seq],
                        block_l=block_l, chunk=chunk)

    @pl.when(li == pl.num_programs(1) - 1)
    def _():
        ret_ref[...] = state_ref[...].reshape(ret_ref.shape)
        pool_ref[...] = ext_ref[:, POOL_CARRY - POOL_BUF:POOL_CARRY, :]


def _window_sums(ext_ref, win_ref, block_l):
    g = POOL_GROUP_DIM
    top = POOL_CARRY + block_l
    new = slice(POOL_CARRY - SUBLANES, None)
    s2 = ext_ref[SUBLANES:top, :] + ext_ref[SUBLANES - 1:top - 1, :]
    win_ref[SUBLANES:top, g:] = s2[:, g:]
    s4 = s2[:, g:] + win_ref[SUBLANES - 2:top - 2, g:]
    win_ref[SUBLANES:top, 2 * g:] = s4[:, g:]
    s8 = s4[:, g:] + win_ref[SUBLANES - 4:top - 4, 2 * g:]
    win_ref[SUBLANES:top, 3 * g:] = s8[:, g:]
    s16 = s8[:, g:] + win_ref[0:top - SUBLANES, 3 * g:]
    return [s2[new, 0:g], s4[new, 0:g], s8[new, 0:g], s16[new, :]]


def _mix_prompt_seq(seq, li, x_ref, mod_ref, norm1_ref, w_in_ref, cos_ref, sin_ref, dmat_ref, cross_ref,
                    tail_ref, cdec_ref, w_pool_ref, pool_scale_ref, w_out_ref, norm2_ref,
                    wr_t_ref, bias_t_ref, x1_ref, h2_ref, experts_ref, gatew_ref,
                    state_ref, ext_ref, win_ref, o_ref, *, block_l, chunk):
    x = x_ref[seq]
    h = _rms(x) * norm1_ref[...] * (1.0 + _mod(mod_ref, 1, seq)) + _mod(mod_ref, 0, seq)
    proj = _dot(h.astype(BF16), w_in_ref[...])
    q = proj[:, 0:QK_WIDTH]
    k = proj[:, QK_WIDTH:2 * QK_WIDTH]
    v = proj[:, 2 * QK_WIDTH:2 * QK_WIDTH + RET_WIDTH]
    g = proj[:, 2 * QK_WIDTH + RET_WIDTH:2 * QK_WIDTH + 2 * RET_WIDTH]
    u = proj[:, 2 * QK_WIDTH + 2 * RET_WIDTH:]

    lane = lax.broadcasted_iota(I32, q.shape, 1)
    first_half = (lane % RET_QK_DIM) < (RET_QK_DIM // 2)
    cos_t = cos_ref[...]
    sin_t = sin_ref[...]

    def rot(t):
        partner = jnp.where(first_half, pltpu.roll(t, QK_WIDTH - RET_QK_DIM // 2, axis=1),
                            pltpu.roll(t, RET_QK_DIM // 2, axis=1))
        return t * cos_t + partner * sin_t

    q = rot(q)
    k = rot(k) * (RET_QK_DIM ** -0.5)
    k_t = k.T
    v16 = v.astype(BF16)
    head_of_lane = lax.broadcasted_iota(I32, (chunk, QK_WIDTH), 1) // RET_QK_DIM

    for c in range(block_l // chunk):
        rows = slice(c * chunk, (c + 1) * chunk)
        q_c = q[rows]
        kt_c = k_t[:, rows]
        kt16 = kt_c.astype(BF16)
        state16 = state_ref[...].astype(BF16)
        for hd in range(RET_HEADS):
            in_head = head_of_lane == hd
            q_h = jnp.where(in_head, q_c, 0.0).astype(BF16)
            v_h = v16[rows, hd * RET_V_DIM:(hd + 1) * RET_V_DIM]
            scores = _dot(q_h, kt16) * dmat_ref[hd]
            inner = _dot(scores.astype(BF16), v_h)
            cross = _dot(q_h, state16) * cross_ref[hd]
            o_ref[rows, hd * RET_V_DIM:(hd + 1) * RET_V_DIM] = inner + cross
            hrows = slice(hd * RET_QK_DIM, (hd + 1) * RET_QK_DIM)
            k_dec = (kt_c[hrows] * tail_ref[hd:hd + 1, :]).astype(BF16)
            state_ref[hrows, :] = state_ref[hrows, :] * cdec_ref[hd] + _dot(k_dec, v_h)

    o_gated = _group_norm_gate(o_ref[...], g)

    ext_ref[POOL_CARRY:POOL_CARRY + block_l, :] = u
    pos = (li * block_l + lax.broadcasted_iota(I32, (block_l, 1), 0)).astype(F32)
    pooled = []
    for gi, (w, acc) in enumerate(zip(POOL_WINDOWS, _window_sums(ext_ref, win_ref, block_l))):
        cnt = jnp.minimum(pos + 1.0, float(w))
        pooled.append(acc / cnt - u[:, gi * POOL_GROUP_DIM:(gi + 1) * POOL_GROUP_DIM])
    p = _pool_project(pooled, w_pool_ref, pool_scale_ref)
    ext_ref[0:POOL_CARRY, :] = ext_ref[block_l:block_l + POOL_CARRY, :]

    _mix_tail(x, o_gated, p, mod_ref, seq, w_out_ref, norm2_ref, wr_t_ref, bias_t_ref,
              x1_ref.at[seq], h2_ref.at[seq], experts_ref.at[seq], gatew_ref.at[seq])


def _decay_tables(chunk):
    lg = jnp.log(1.0 - 2.0 ** (-5.0 - jnp.arange(RET_HEADS, dtype=F32)))
    idx = jnp.arange(chunk, dtype=F32)
    diff = idx[:, None] - idx[None, :]
    causal = diff >= 0
    dmat = jnp.where(causal[None], jnp.exp(lg[:, None, None] * jnp.where(causal, diff, 0.0)[None]), 0.0)
    cross = jnp.exp(lg[:, None] * (idx[None, :] + 1.0))
    cross = jnp.broadcast_to(cross[:, :, None], (RET_HEADS, chunk, RET_V_DIM))
    tail = jnp.exp(lg[:, None] * (chunk - 1.0 - idx)[None, :])
    cdec = jnp.broadcast_to(jnp.exp(lg * chunk)[:, None, None], (RET_HEADS, RET_QK_DIM, RET_V_DIM))
    return dmat, cross, tail, cdec


def _rotary_tables(pos):
    half = RET_QK_DIM // 2
    freqs = ROPE_BASE ** (-jnp.arange(half, dtype=F32) / half)
    ang = pos[:, None] * freqs[None, :]
    cos, sin = jnp.cos(ang), jnp.sin(ang)
    cos_t = jnp.tile(jnp.concatenate([cos, cos], axis=-1), (1, RET_HEADS))
    sin_t = jnp.tile(jnp.concatenate([-sin, sin], axis=-1), (1, RET_HEADS))
    return cos_t, sin_t


def _full(shape):
    return pl.BlockSpec(shape, lambda *_: (0,) * len(shape))


def _mix_prompt(x, mod, b0, b, norm1, w_in16, w_pool, pool_scale, w_out16, norm2, wr_t, bias_t,
                block_l=512, chunk=256, seqs=2):
    _, l, d = x.shape
    nl = l // block_l
    s0 = b0 // seqs
    cos_t, sin_t = _rotary_tables(jnp.arange(l, dtype=F32))
    dmat, cross, tail, cdec = _decay_tables(chunk)
    kernel = functools.partial(_mix_prompt_kernel, block_l=block_l, chunk=chunk, seqs=seqs)
    tok = lambda bi, li: (bi, li, 0)
    per_seq = lambda bi, li: (bi, 0, 0)
    x1, h2, experts, gate_w, ret, pool = pl.pallas_call(
        kernel,
        grid=(b // seqs, nl),
        in_specs=[pl.BlockSpec((seqs, block_l, d), lambda bi, li: (s0 + bi, li, 0)),
                  pl.BlockSpec((seqs, 6, d), lambda bi, li: (s0 + bi, 0, 0)),
                  _full((1, d)),
                  _full((d, IN_WIDTH)),
                  pl.BlockSpec((block_l, QK_WIDTH), lambda bi, li: (li, 0)),
                  pl.BlockSpec((block_l, QK_WIDTH), lambda bi, li: (li, 0)),
                  _full(dmat.shape), _full(cross.shape), _full(tail.shape), _full(cdec.shape),
                  _full(w_pool.shape), _full((1, POOL_WIDTH)), _full((d, d)), _full((1, d)),
                  _full(wr_t.shape), _full(bias_t.shape)],
        out_specs=[pl.BlockSpec((seqs, block_l, d), tok),
                   pl.BlockSpec((seqs, block_l, HALF), tok),
                   pl.BlockSpec((seqs, TOP_K, block_l), lambda bi, li: (bi, 0, li)),
                   pl.BlockSpec((seqs, block_l, LANES), tok),
                   pl.BlockSpec((seqs, RET_HEADS, RET_QK_DIM, RET_V_DIM), lambda bi, li: (bi, 0, 0, 0)),
                   pl.BlockSpec((seqs, POOL_BUF, POOL_WIDTH), per_seq)],
        out_shape=[jax.ShapeDtypeStruct((b, l, d), F32),
                   jax.ShapeDtypeStruct((b, l, HALF), U32),
                   jax.ShapeDtypeStruct((b, TOP_K, l), I32),
                   jax.ShapeDtypeStruct((b, l, LANES), F32),
                   jax.ShapeDtypeStruct((b, RET_HEADS, RET_QK_DIM, RET_V_DIM), F32),
                   jax.ShapeDtypeStruct((b, POOL_BUF, POOL_WIDTH), F32)],
        scratch_shapes=[pltpu.VMEM((seqs, QK_WIDTH, RET_V_DIM), F32),
                        pltpu.VMEM((seqs, POOL_CARRY + block_l, POOL_WIDTH), F32),
                        pltpu.VMEM((seqs, POOL_CARRY + block_l, POOL_WIDTH), F32),
                        pltpu.VMEM((seqs, block_l, RET_WIDTH), F32)],
        compiler_params=pltpu.CompilerParams(dimension_semantics=("arbitrary", "arbitrary"),
                                             vmem_limit_bytes=VMEM_LIMIT),
        name="mix_prompt",
    )(x, mod, norm1, w_in16, cos_t, sin_t, dmat, cross, tail, cdec, w_pool, pool_scale,
      w_out16, norm2, wr_t, bias_t)
    experts = jnp.transpose(experts, (1, 0, 2)).reshape(TOP_K, b * l)
    return x1, h2.reshape(b * l, HALF), experts, gate_w.reshape(b * l, LANES), ret, pool


def _mix_sample_front_kernel(x_ref, mod_ref, norm1_ref, w_in_ref, cos_ref, sin_ref,
                             qt_ref, kt_ref, v_ref, g_ref, u_ref):
    x = x_ref[...]
    h = _rms(x) * norm1_ref[...] * (1.0 + _mod(mod_ref, 1)) + _mod(mod_ref, 0)
    proj = _dot(h.astype(BF16), w_in_ref[...])
    half = RET_QK_DIM // 2
    cos_c = cos_ref[...]
    sin_c = sin_ref[...]

    def rot_t(t):
        parts = []
        for hd in range(RET_HEADS):
            t1 = t[hd * RET_QK_DIM:hd * RET_QK_DIM + half]
            t2 = t[hd * RET_QK_DIM + half:(hd + 1) * RET_QK_DIM]
            parts += [t1 * cos_c - t2 * sin_c, t1 * sin_c + t2 * cos_c]
        return jnp.concatenate(parts, axis=0)

    qt_ref[...] = rot_t(proj[:, 0:QK_WIDTH].T)
    kt_ref[...] = rot_t(proj[:, QK_WIDTH:2 * QK_WIDTH].T) * (RET_QK_DIM ** -0.5)
    v_ref[...] = proj[:, 2 * QK_WIDTH:2 * QK_WIDTH + RET_WIDTH]
    g_ref[...] = proj[:, 2 * QK_WIDTH + RET_WIDTH:2 * QK_WIDTH + 2 * RET_WIDTH]
    u_ref[...] = proj[:, 2 * QK_WIDTH + 2 * RET_WIDTH:]


def _ret_step_kernel(qt_ref, kt_ref, v_ref, s0_ref, o_ref, s1_ref, *, block_b, decays):
    i = pl.program_id(0)
    lane = lax.broadcasted_iota(I32, qt_ref.shape, 1)
    for j in range(block_b):
        bi = i * block_b + j
        here = lane == bi
        q_col = jnp.sum(jnp.where(here, qt_ref[...], 0.0), axis=1, keepdims=True)
        k_col = jnp.sum(jnp.where(here, kt_ref[...], 0.0), axis=1, keepdims=True)
        v_row = v_ref[pl.ds(bi, 1), :]
        outs = []
        for hd in range(RET_HEADS):
            hrows = slice(hd * RET_QK_DIM, (hd + 1) * RET_QK_DIM)
            s1 = decays[hd] * s0_ref[j, hd] + k_col[hrows] * v_row[:, hd * RET_V_DIM:(hd + 1) * RET_V_DIM]
            s1_ref[j, hd] = s1
            outs.append(jnp.sum(q_col[hrows] * s1, axis=0, keepdims=True))
        o_ref[pl.ds(bi, 1), :] = jnp.concatenate(outs, axis=-1)


def _mix_sample_back_kernel(x_ref, mod_ref, o_ref, g_ref, u_ref, buf_ref, w_pool_ref, pool_scale_ref,
                            w_out_ref, norm2_ref, wr_t_ref, bias_t_ref,
                            x1_ref, h2_ref, experts_ref, gatew_ref, pool_ref):
    o_gated = _group_norm_gate(o_ref[...], g_ref[...])
    u = u_ref[...]
    pooled = []
    for gi, w in enumerate(POOL_WINDOWS):
        lanes = slice(gi * POOL_GROUP_DIM, (gi + 1) * POOL_GROUP_DIM)
        acc = u[:, lanes]
        for j in range(1, w):
            acc = acc + buf_ref[:, POOL_BUF - j, lanes]
        pooled.append(acc / float(w) - u[:, lanes])
    p = _pool_project(pooled, w_pool_ref, pool_scale_ref)
    pool_ref[:, 0:POOL_BUF - 1, :] = buf_ref[:, 1:POOL_BUF, :]
    pool_ref[:, POOL_BUF - 1, :] = u
    _mix_tail(x_ref[...], o_gated, p, mod_ref, 0, w_out_ref, norm2_ref, wr_t_ref, bias_t_ref,
              x1_ref, h2_ref, experts_ref, gatew_ref)


def _mix_sample(x, mod, state_ret, state_pool, start, norm1, w_in16, w_pool,
                pool_scale, w_out16, norm2, wr_t, bias_t, block_b=32):
    n, d = x.shape
    half = RET_QK_DIM // 2
    freqs = ROPE_BASE ** (-jnp.arange(half, dtype=F32) / half)
    ang = jnp.full((1,), start, F32)[:, None] * freqs[None, :]
    cos_c = jnp.broadcast_to(jnp.cos(ang).T, (half, n))
    sin_c = jnp.broadcast_to(jnp.sin(ang).T, (half, n))
    params = pltpu.CompilerParams(vmem_limit_bytes=VMEM_LIMIT)
    qt, kt, v, g, u = pl.pallas_call(
        _mix_sample_front_kernel,
        out_shape=[jax.ShapeDtypeStruct((QK_WIDTH, n), F32), jax.ShapeDtypeStruct((QK_WIDTH, n), F32),
                   jax.ShapeDtypeStruct((n, RET_WIDTH), F32), jax.ShapeDtypeStruct((n, RET_WIDTH), F32),
                   jax.ShapeDtypeStruct((n, POOL_WIDTH), F32)],
        compiler_params=params,
        name="mix_sample_front",
    )(x, mod, norm1, w_in16, cos_c, sin_c)

    lg = np.log(1.0 - 2.0 ** (-5.0 - np.arange(RET_HEADS, dtype=np.float32)), dtype=np.float32)
    decays = tuple(float(np.exp(lg[h])) for h in range(RET_HEADS))
    state_block = (block_b, RET_HEADS, RET_QK_DIM, RET_V_DIM)
    o, s1 = pl.pallas_call(
        functools.partial(_ret_step_kernel, block_b=block_b, decays=decays),
        grid=(n // block_b,),
        in_specs=[_full((QK_WIDTH, n)), _full((QK_WIDTH, n)), _full((n, RET_WIDTH)),
                  pl.BlockSpec(state_block, lambda i: (i, 0, 0, 0))],
        out_specs=[_full((n, RET_WIDTH)), pl.BlockSpec(state_block, lambda i: (i, 0, 0, 0))],
        out_shape=[jax.ShapeDtypeStruct((n, RET_WIDTH), F32),
                   jax.ShapeDtypeStruct(state_ret.shape, F32)],
        compiler_params=pltpu.CompilerParams(dimension_semantics=("arbitrary",),
                                             vmem_limit_bytes=VMEM_LIMIT),
        name="ret_step",
    )(qt, kt, v, state_ret)

    x1, h2, experts, gate_w, pool = pl.pallas_call(
        _mix_sample_back_kernel,
        out_shape=[jax.ShapeDtypeStruct((n, d), F32),
                   jax.ShapeDtypeStruct((n, HALF), U32),
                   jax.ShapeDtypeStruct((TOP_K, n), I32),
                   jax.ShapeDtypeStruct((n, LANES), F32),
                   jax.ShapeDtypeStruct(state_pool.shape, F32)],
        compiler_params=params,
        name="mix_sample_back",
    )(x, mod, o, g, u, state_pool, w_pool, pool_scale, w_out16, norm2, wr_t, bias_t)
    return x1, h2, experts, gate_w, s1, pool


def _plan_kernel(experts_ref, pos_ref, meta_ref, cnt_ref, carry_ref, off_ref, *, block_t):
    phase = pl.program_id(0)
    j = pl.program_id(1)
    e_blk = experts_ref[...]
    eidx = lax.broadcasted_iota(I32, (N_EXPERTS, block_t), 0)
    member = jnp.zeros((N_EXPERTS, block_t), F32)
    for s in range(TOP_K):
        member = member + jnp.where(eidx == e_blk[s:s + 1, :], 1.0, 0.0)
    per_expert = jnp.broadcast_to(jnp.sum(member, axis=1, keepdims=True), (N_EXPERTS, LANES))

    @pl.when((phase == 0) & (j == 0))
    def _():
        cnt_ref[...] = jnp.zeros_like(cnt_ref)

    @pl.when(phase == 0)
    def _():
        cnt_ref[...] += per_expert

    @pl.when((phase == 0) & (j == pl.num_programs(1) - 1))
    def _():
        cnt = cnt_ref[...]
        n_tile = jnp.floor((cnt + (ROW_TILE - 1.0)) * (1.0 / ROW_TILE))
        upto = (lax.broadcasted_iota(I32, (N_EXPERTS, N_EXPERTS), 1)
                <= lax.broadcasted_iota(I32, (N_EXPERTS, N_EXPERTS), 0))
        tile_end = _dot(jnp.where(upto, 1.0, 0.0).astype(BF16), n_tile.astype(BF16))
        tile_start = tile_end - n_tile
        off_ref[...] = tile_start * ROW_TILE
        carry_ref[...] = jnp.zeros_like(carry_ref)
        lane = lax.broadcasted_iota(I32, cnt.shape, 1)
        meta_ref[...] = jnp.where(lane == 0, tile_start, jnp.where(lane == 1, n_tile, cnt)).astype(I32)

    @pl.when(phase == 1)
    def _():
        before = (lax.broadcasted_iota(I32, (block_t, block_t), 0)
                  < lax.broadcasted_iota(I32, (block_t, block_t), 1))
        rank = _dot(member.astype(BF16), jnp.where(before, 1.0, 0.0).astype(BF16))
        row = off_ref[:, 0:1] + carry_ref[:, 0:1] + rank
        carry_ref[...] += per_expert
        out = [jnp.sum(jnp.where(eidx == e_blk[s:s + 1, :], row, 0.0), axis=0, keepdims=True)
               for s in range(TOP_K)]
        pos_ref[...] = jnp.concatenate(out, axis=0).astype(I32)


def _plan(experts_all, max_block=1024):
    n_tokens = experts_all.shape[1]
    block_t = max(k for k in range(LANES, max_block + 1, LANES) if n_tokens % k == 0)
    nb = n_tokens // block_t
    return pl.pallas_call(
        functools.partial(_plan_kernel, block_t=block_t),
        grid=(2, nb),
        in_specs=[pl.BlockSpec((TOP_K, block_t), lambda ph, j: (0, j))],
        out_specs=[pl.BlockSpec((TOP_K, block_t), lambda ph, j: (0, j * ph)),
                   _full((N_EXPERTS, LANES))],
        out_shape=[jax.ShapeDtypeStruct((TOP_K, n_tokens), I32),
                   jax.ShapeDtypeStruct((N_EXPERTS, LANES), I32)],
        scratch_shapes=[pltpu.VMEM((N_EXPERTS, LANES), F32)] * 3,
        compiler_params=pltpu.CompilerParams(dimension_semantics=("arbitrary", "arbitrary"),
                                             vmem_limit_bytes=VMEM_LIMIT),
        name="plan",
    )(experts_all)


def _sc_workers():
    info = plsc.get_sparse_core_info()
    return info.num_cores, info.num_cores * info.num_subcores


def _sc_scatter_rows(sources, pos_t, n_out):
    w = sources[0].shape[1]
    s = pos_t.shape[0]
    n_cores, n_workers = _sc_workers()
    bounds = np.cumsum([0] + [src.shape[0] // SC_CHUNK for src in sources])
    n_chunks = int(bounds[-1])
    iters = -(-n_chunks // n_workers)
    mesh = plsc.VectorSubcoreMesh(core_axis_name="c", subcore_axis_name="s")

    @functools.partial(
        pl.kernel, mesh=mesh, out_type=jax.ShapeDtypeStruct((n_out, w), sources[0].dtype),
        scratch_types=[pltpu.VMEM((SC_CHUNK, w), sources[0].dtype), pltpu.VMEM((s, SC_CHUNK), I32),
                       pltpu.SemaphoreType.DMA],
        name="dispatch")
    def k(*refs):
        src_hbm, (pos_hbm, out_hbm, rows_v, idx_v, sem) = refs[:len(sources)], refs[len(sources):]
        wid = lax.axis_index("s") * n_cores + lax.axis_index("c")

        @pl.loop(0, iters)
        def _(it):
            c = it * n_workers + wid
            for src, lo, hi in zip(src_hbm, bounds[:-1], bounds[1:]):
                @pl.when((c >= int(lo)) & (c < int(hi)))
                def _():
                    base = pl.multiple_of((c - int(lo)) * SC_CHUNK, SC_CHUNK)
                    pltpu.sync_copy(src.at[pl.ds(base, SC_CHUNK)], rows_v)

            @pl.when(c < n_chunks)
            def _():
                base = pl.multiple_of(c * SC_CHUNK, SC_CHUNK)
                pltpu.sync_copy(pos_hbm.at[:, pl.ds(base, SC_CHUNK)], idx_v)
                copies = [pltpu.async_copy(rows_v, out_hbm.at[idx_v.at[j]], sem) for j in range(s)]
                for cp in copies:
                    cp.wait()

    return k(*sources, pos_t)


def _sc_gather_rows(table, pos_t):
    _, w = table.shape
    s, t = pos_t.shape
    n_cores, n_workers = _sc_workers()
    n_chunks = t // SC_CHUNK
    iters = -(-n_chunks // n_workers)
    mesh = plsc.VectorSubcoreMesh(core_axis_name="c", subcore_axis_name="s")

    @functools.partial(
        pl.kernel, mesh=mesh, out_type=jax.ShapeDtypeStruct((s, t, w), table.dtype),
        scratch_types=[pltpu.VMEM((SC_CHUNK, w), table.dtype), pltpu.VMEM((s, SC_CHUNK), I32),
                       pltpu.SemaphoreType.DMA],
        name="combine")
    def k(table_hbm, pos_hbm, out_hbm, rows_v, idx_v, sem):
        wid = lax.axis_index("s") * n_cores + lax.axis_index("c")

        @pl.loop(0, iters)
        def _(it):
            c = it * n_workers + wid

            @pl.when(c < n_chunks)
            def _():
                base = pl.multiple_of(c * SC_CHUNK, SC_CHUNK)
                pltpu.sync_copy(pos_hbm.at[:, pl.ds(base, SC_CHUNK)], idx_v)
                for j in range(s):
                    pltpu.async_copy(table_hbm.at[idx_v.at[j]], rows_v, sem).wait()
                    pltpu.sync_copy(rows_v, out_hbm.at[j, pl.ds(base, SC_CHUNK)])

    return k(table, pos_t)


def _sc_pack_weights(w, rows_per_item):
    e, r, c = w.shape
    half = r // 2
    rb = rows_per_item
    per_expert = half // rb
    n_cores, n_workers = _sc_workers()
    per_worker = e * per_expert // n_workers
    assert per_worker * n_workers == e * per_expert and per_worker % 2 == 0 and c % (SC_LANES * SC_UNROLL) == 0
    mesh = plsc.VectorSubcoreMesh(core_axis_name="c", subcore_axis_name="s")

    @functools.partial(
        pl.kernel, mesh=mesh, out_type=jax.ShapeDtypeStruct((e * half, c), U32),
        scratch_types=[pltpu.VMEM((2, rb, c), F32), pltpu.VMEM((2, rb, c), F32), pltpu.VMEM((2, rb, c), U32),
                       pltpu.SemaphoreType.DMA((2,)), pltpu.SemaphoreType.DMA((2,))],
        compiler_params=pltpu.CompilerParams(needs_layout_passes=False),
        cost_estimate=pl.CostEstimate(flops=e * r * c, transcendentals=0, bytes_accessed=6 * e * r * c),
        name="pack_weights")
    def k(w_hbm, out_hbm, lo_v, hi_v, out_v, in_sem, out_sem):
        wid = lax.axis_index("s") * n_cores + lax.axis_index("c")
        first = wid * per_worker

        def rows(item):
            ex = item // per_expert
            j = item - ex * per_expert
            return (pl.multiple_of(ex * r + j * rb, rb), pl.multiple_of(ex * r + half + j * rb, rb),
                    pl.multiple_of(ex * half + j * rb, rb))

        def loads(item, b):
            lo_row, hi_row, _ = rows(item)
            return (pltpu.make_async_copy(w_hbm.at[pl.ds(lo_row, rb)], lo_v.at[b], in_sem.at[b]),
                    pltpu.make_async_copy(w_hbm.at[pl.ds(hi_row, rb)], hi_v.at[b], in_sem.at[b]))

        def store(item, b):
            return pltpu.make_async_copy(out_v.at[b], out_hbm.at[pl.ds(rows(item)[2], rb)], out_sem.at[b])

        for cp in loads(first, 0):
            cp.start()

        @pl.loop(0, per_worker // 2)
        def _(pair):
            for b in range(2):
                item = first + pair * 2 + b
                for cp in loads(item, b):
                    cp.wait()

                @pl.when(item + 1 < first + per_worker)
                def _():
                    for cp in loads(item + 1, 1 - b):
                        cp.start()

                @pl.when(pair > 0)
                def _():
                    store(item - 2, b).wait()

                @pl.loop(0, rb)
                def _(i):
                    @pl.loop(0, c // (SC_LANES * SC_UNROLL))
                    def _(vb):
                        for u in range(SC_UNROLL):
                            sl = pl.ds(pl.multiple_of((vb * SC_UNROLL + u) * SC_LANES, SC_LANES), SC_LANES)
                            packed = plsc.pack(lo_v[b, i, sl], hi_v[b, i, sl], format=plsc.PackFormat.INTERLEAVED)
                            out_v[b, i, sl] = plsc.bitcast(packed, U32)

                store(item, b).start()

        for b in range(2):
            store(first + per_worker - 2 + b, b).wait()

    return k(w.reshape(e * r, c)).reshape(e, half, c)


def _experts_kernel(first_ref, ntile_ref, cnt_ref, xs_hbm, wg_ref, wu_ref, wd_ref, ys_hbm,
                    wg16_ref, wu16_ref, wd16_ref, x_buf, y_buf, in_sem, out_sem):
    e = pl.program_id(0)
    n_used = first_ref[N_EXPERTS - 1] + ntile_ref[N_EXPERTS - 1]
    first, n_mine, count = first_ref[e], ntile_ref[e], cnt_ref[e]

    def tile_rows(g):
        return pl.ds(pl.multiple_of(g * ROW_TILE, ROW_TILE), ROW_TILE)

    def load(g):
        slot = lax.rem(g, STREAM_DEPTH)
        return pltpu.make_async_copy(xs_hbm.at[tile_rows(g)], x_buf.at[slot], in_sem.at[slot])

    def store(g):
        slot = lax.rem(g, STREAM_DEPTH)
        return pltpu.make_async_copy(y_buf.at[slot], ys_hbm.at[tile_rows(g)], out_sem.at[slot])

    @pl.when(e == 0)
    def _():
        for g0 in range(STREAM_DEPTH - 1):
            @pl.when(g0 < n_used)
            def _():
                load(g0).start()

    for packed_ref, w16_ref in ((wg_ref, wg16_ref), (wu_ref, wu16_ref), (wd_ref, wd16_ref)):
        rows = packed_ref.shape[1]
        lo, hi = _unpack_rows(packed_ref[0])
        w16_ref[0:rows, :] = lo.astype(BF16)
        w16_ref[rows:, :] = hi.astype(BF16)

    def tile(j, carry):
        g = first + j
        slot = lax.rem(g, STREAM_DEPTH)
        load(g).wait()

        @pl.when(g + (STREAM_DEPTH - 1) < n_used)
        def _():
            load(g + (STREAM_DEPTH - 1)).start()

        @pl.when(g >= STREAM_DEPTH)
        def _():
            store(g - STREAM_DEPTH).wait()

        words = x_buf[slot]
        row = lax.broadcasted_iota(I32, words.shape, 0)
        words = jnp.where(row < count - j * ROW_TILE, words, jnp.uint32(0))
        lo, hi = _unpack_rows(words)
        lo, hi = lo.astype(BF16), hi.astype(BF16)
        hg = _dot(lo, wg16_ref[0:HALF, :]) + _dot(hi, wg16_ref[HALF:, :])
        hu = _dot(lo, wu16_ref[0:HALF, :]) + _dot(hi, wu16_ref[HALF:, :])
        a = (_silu(hg) * hu).astype(BF16)
        y_buf[slot] = _pack_rows(_dot(a, wd16_ref[...]))
        store(g).start()
        return carry

    lax.fori_loop(0, n_mine, tile, 0)

    @pl.when(e == N_EXPERTS - 1)
    def _():
        for back in range(STREAM_DEPTH, 0, -1):
            @pl.when(n_used >= back)
            def _():
                store(n_used - back).wait()


def _experts(xs, first_tile, n_tile, count, w_eg, w_eu, w_ed):
    d = D_MODEL
    by_expert = lambda e, *_: (e, 0, 0)
    grid_spec = pltpu.PrefetchScalarGridSpec(
        num_scalar_prefetch=3,
        grid=(N_EXPERTS,),
        in_specs=[pl.BlockSpec(memory_space=pl.ANY),
                  pl.BlockSpec((1, d // 2, EXPERT_DIM), by_expert),
                  pl.BlockSpec((1, d // 2, EXPERT_DIM), by_expert),
                  pl.BlockSpec((1, EXPERT_DIM // 2, d), by_expert)],
        out_specs=pl.BlockSpec(memory_space=pl.ANY),
        scratch_shapes=[pltpu.VMEM((d, EXPERT_DIM), BF16), pltpu.VMEM((d, EXPERT_DIM), BF16),
                        pltpu.VMEM((EXPERT_DIM, d), BF16),
                        pltpu.VMEM((STREAM_DEPTH, ROW_TILE, HALF), U32),
                        pltpu.VMEM((STREAM_DEPTH, ROW_TILE, HALF), U32),
                        pltpu.SemaphoreType.DMA((STREAM_DEPTH,)), pltpu.SemaphoreType.DMA((STREAM_DEPTH,))])
    return pl.pallas_call(
        _experts_kernel,
        grid_spec=grid_spec,
        out_shape=jax.ShapeDtypeStruct(xs.shape, U32),
        compiler_params=pltpu.CompilerParams(dimension_semantics=("arbitrary",),
                                             vmem_limit_bytes=VMEM_LIMIT),
        name="experts",
    )(first_tile, n_tile, count, xs, w_eg, w_eu, w_ed)


def _final_kernel(z_ref, gatew_ref, h2_ref, x1_ref, mod_ref, normf_ref, wsg_ref, wsu_ref, wsd_ref, *rest):
    y_ref = rest[-1]
    lo, hi = _unpack_rows(h2_ref[...])
    h = jnp.concatenate([lo, hi], axis=-1).astype(BF16)
    a = _silu(_dot(h, wsg_ref[...])) * _dot(h, wsu_ref[...])
    acc = _dot(a.astype(BF16), wsd_ref[...])
    for s in range(TOP_K):
        lo, hi = _unpack_rows(z_ref[s])
        acc = acc + gatew_ref[:, s:s + 1] * jnp.concatenate([lo, hi], axis=-1)
    x2 = x1_ref[...] + _mod(mod_ref, 5) * acc
    y_ref[...] = _rms(x2) * normf_ref[...]


def _final(z, gate_w, h2, x1, mod, norm_f, w_sg16, w_su16, w_sd16, block_t, first_block, per_seq,
           seq0=0, out_rows=None, y_prev=None):
    t, d = x1.shape
    out_rows = t if out_rows is None else out_rows
    out_first = seq0 * per_seq
    tok = lambda i: (i, 0)
    if per_seq:
        mod_spec = pl.BlockSpec((1, 6, d), lambda i: (seq0 + i // per_seq, 0, 0))
    else:
        mod_spec = pl.BlockSpec((block_t, 6 * d), tok)
    operands = [z, gate_w, h2, x1, mod, norm_f, w_sg16, w_su16, w_sd16]
    in_specs = [pl.BlockSpec((TOP_K, block_t, HALF), lambda i: (0, first_block + i, 0)),
                pl.BlockSpec((block_t, LANES), tok),
                pl.BlockSpec((block_t, HALF), tok),
                pl.BlockSpec((block_t, d), tok),
                mod_spec,
                _full((1, d)),
                _full((d, EXPERT_DIM)), _full((d, EXPERT_DIM)), _full((EXPERT_DIM, d))]
    aliases = {}
    if y_prev is not None:
        aliases = {len(operands): 0}
        operands.append(y_prev)
        in_specs.append(pl.BlockSpec(memory_space=pl.ANY))
    return pl.pallas_call(
        _final_kernel,
        grid=(t // block_t,),
        in_specs=in_specs,
        out_specs=pl.BlockSpec((block_t, d), lambda i: (out_first + i, 0)),
        out_shape=jax.ShapeDtypeStruct((out_rows, d), F32),
        input_output_aliases=aliases,
        compiler_params=pltpu.CompilerParams(dimension_semantics=("arbitrary",),
                                             vmem_limit_bytes=VMEM_LIMIT),
        name="final",
    )(*operands)


def kernel(x_prompt, x_sample, c_prompt, c_sample, state_ret, state_pool, norm1, norm2, norm_f,
           w_ada, b_ada, w_in, w_out, w_pool, pool_scale, w_router, router_bias, w_exp_gate,
           w_exp_up, w_exp_down, w_sh_gate, w_sh_up, w_sh_down):
    b, l, d = x_prompt.shape
    n = x_sample.shape[0]
    past_len = 16384

    mod = _ada(jnp.concatenate([c_prompt, c_sample], axis=0), w_ada[0], b_ada[0])
    mod_p = mod[:b].reshape(b, 6, d)
    mod_s = mod[b:]

    w_in16 = w_in[0].astype(BF16)
    w_out16 = w_out[0].astype(BF16)
    wr_t = w_router[0].T
    bias_t = jnp.broadcast_to(router_bias[0][:, None], (N_EXPERTS, LANES))
    n1, n2, nf = norm1[0].reshape(1, d), norm2[0].reshape(1, d), norm_f.reshape(1, d)
    ps = pool_scale[0].reshape(1, POOL_WIDTH)
    shared = (w_sh_gate[0].astype(BF16), w_sh_up[0].astype(BF16), w_sh_down[0].astype(BF16))

    def routed(sources, experts):
        n_tiles = experts.shape[1] * TOP_K // ROW_TILE + N_EXPERTS
        pos_t, meta = _plan(experts)
        xs = _sc_scatter_rows(sources, pos_t, n_tiles * ROW_TILE)
        ys = _experts(xs, meta[:, 0], meta[:, 1], meta[:, 2], *expert_w)
        return _sc_gather_rows(ys, pos_t)

    expert_w = (_sc_pack_weights(w_exp_gate[0], 64), _sc_pack_weights(w_exp_up[0], 64),
                _sc_pack_weights(w_exp_down[0], 16))

    ba = b // 2
    bb = b - ba
    mix_args = (n1, w_in16, w_pool[0], ps, w_out16, n2, wr_t, bias_t)
    x1_a, h2_a, experts_a, gatew_a, ret_a, pool_a = _mix_prompt(x_prompt, mod_p, 0, ba, *mix_args)
    z_a = routed((h2_a,), experts_a)
    x1_b, h2_b, experts_b, gatew_b, ret_b, pool_b = _mix_prompt(x_prompt, mod_p, ba, bb, *mix_args)
    x1_s, h2_s, experts_s, gatew_s, ret_s, pool_s = _mix_sample(
        x_sample.reshape(n, d), mod_s, state_ret[0], state_pool[0], float(past_len), *mix_args)
    z_b = routed((h2_b, h2_s), jnp.concatenate([experts_b, experts_s], axis=1))

    block_t = 256
    per_seq = l // block_t
    y_s = _final(z_b, gatew_s, h2_s, x1_s, mod_s, nf, *shared,
                 block_t=n, first_block=bb * l // n, per_seq=0)
    y_p = _final(z_b, gatew_b, h2_b, x1_b.reshape(bb * l, d), mod_p, nf, *shared,
                 block_t=block_t, first_block=0, per_seq=per_seq, seq0=ba, out_rows=b * l)
    y_p = _final(z_a, gatew_a, h2_a, x1_a.reshape(ba * l, d), mod_p, nf, *shared,
                 block_t=block_t, first_block=0, per_seq=per_seq, seq0=0, out_rows=b * l, y_prev=y_p)

    ret_p = jnp.concatenate([ret_a, ret_b], axis=0)
    pool_p = jnp.concatenate([pool_a, pool_b], axis=0)
    return (y_p.reshape(b, l, d), y_s.reshape(n, 1, d), ret_p[None], pool_p[None],
            ret_s[None], pool_s[None])
```

```python
import functools

import jax
import jax.numpy as jnp
import numpy as np
from jax import lax
from jax.experimental import pallas as pl
from jax.experimental.pallas import tpu as pltpu
from jax.experimental.pallas import tpu_sc as plsc

D_MODEL = 1024
RET_HEADS = 4
RET_QK_DIM = 64
RET_V_DIM = 128
RET_WIDTH = RET_HEADS * RET_V_DIM
QK_WIDTH = RET_HEADS * RET_QK_DIM
ROPE_BASE = 10000.0
POOL_WINDOWS = (2, 4, 8, 16)
POOL_WIDTH = 512
POOL_GROUP_DIM = 128
POOL_BUF = 15
IN_WIDTH = 2 * QK_WIDTH + 2 * RET_WIDTH + POOL_WIDTH
N_EXPERTS = 64
TOP_K = 8
N_EXPERT_GROUPS = 8
GROUP_SIZE = N_EXPERTS // N_EXPERT_GROUPS
TOP_GROUPS = 4
EXPERT_DIM = 256
ROUTE_SCALE = 2.5
EPS = 1e-6

LANES = 128
SUBLANES = 8
POOL_CARRY = 24
VMEM_LIMIT = 56 * 1024 * 1024
HALF = D_MODEL // 2
ROW_TILE = 256
SC_CHUNK = 128
SC_LANES = 16
SC_UNROLL = 16
STREAM_DEPTH = 8

BF16 = jnp.bfloat16
F32 = jnp.float32
U32 = jnp.uint32
I32 = jnp.int32


def _silu(x):
    return x * jax.nn.sigmoid(x)


def _dot(a, b):
    return jnp.dot(a, b, preferred_element_type=F32)


def _rms(x):
    return x * lax.rsqrt(jnp.mean(x * x, axis=-1, keepdims=True) + EPS)


def _mod(mod_ref, i, seq=0):
    if len(mod_ref.shape) == 3:
        return mod_ref[seq, i:i + 1, :]
    return mod_ref[:, i * D_MODEL:(i + 1) * D_MODEL]


def _split_bf16(x):
    hi = x.astype(BF16)
    lo = (x - hi.astype(F32)).astype(BF16)
    return hi, lo


def _pack_rows(x):
    lo = lax.bitcast_convert_type(x[:, :HALF].astype(BF16).astype(F32), U32)
    hi = lax.bitcast_convert_type(x[:, HALF:].astype(BF16).astype(F32), U32)
    return (hi & jnp.uint32(0xFFFF0000)) | (lo >> jnp.uint32(16))


def _unpack_rows(w):
    lo = lax.bitcast_convert_type(w << jnp.uint32(16), F32)
    hi = lax.bitcast_convert_type(w & jnp.uint32(0xFFFF0000), F32)
    return lo, hi


def _first_max_onehot(work, idx, n):
    m = jnp.max(work, axis=0, keepdims=True)
    first = jnp.min(jnp.where(work == m, idx, float(n)), axis=0, keepdims=True)
    return idx == first


def _route(h2, wr_t_ref, bias_t_ref):
    n = h2.shape[0]
    h_hi, h_lo = _split_bf16(h2)
    w_hi, w_lo = _split_bf16(wr_t_ref[...])
    nt = (((1,), (1,)), ((), ()))
    logits = (lax.dot_general(w_hi, h_hi, nt, preferred_element_type=F32)
              + lax.dot_general(w_hi, h_lo, nt, preferred_element_type=F32)
              + lax.dot_general(w_lo, h_hi, nt, preferred_element_type=F32))
    scores = jax.nn.sigmoid(logits)
    biased = scores + bias_t_ref[:, 0:1]
    b3 = biased.reshape(N_EXPERT_GROUPS, GROUP_SIZE, n)
    i3 = lax.broadcasted_iota(I32, b3.shape, 1).astype(F32)
    m1 = jnp.max(b3, axis=1, keepdims=True)
    first = jnp.min(jnp.where(b3 == m1, i3, float(GROUP_SIZE)), axis=1, keepdims=True)
    m2 = jnp.max(jnp.where(i3 == first, -jnp.inf, b3), axis=1, keepdims=True)
    gscore = (m1 + m2).reshape(N_EXPERT_GROUPS, n)
    gidx = lax.broadcasted_iota(I32, gscore.shape, 0).astype(F32)
    gsel = jnp.zeros(gscore.shape, F32)
    work = gscore
    for _ in range(TOP_GROUPS):
        hit = _first_max_onehot(work, gidx, N_EXPERT_GROUPS)
        gsel = jnp.where(hit, 1.0, gsel)
        work = jnp.where(hit, -jnp.inf, work)
    gsel3 = jnp.broadcast_to(gsel.reshape(N_EXPERT_GROUPS, 1, n), b3.shape)
    work = jnp.where(gsel3 > 0.0, b3, -jnp.inf).reshape(N_EXPERTS, n)
    eidx = lax.broadcasted_iota(I32, work.shape, 0).astype(F32)
    sel = jnp.zeros(work.shape, F32)
    for _ in range(TOP_K):
        hit = _first_max_onehot(work, eidx, N_EXPERTS)
        sel = jnp.where(hit, 1.0, sel)
        work = jnp.where(hit, -jnp.inf, work)
    picked = jnp.where(sel > 0.0, scores, 0.0)
    gates = picked / jnp.sum(picked, axis=0, keepdims=True) * ROUTE_SCALE
    below = (lax.broadcasted_iota(I32, (N_EXPERTS, N_EXPERTS), 1)
             < lax.broadcasted_iota(I32, (N_EXPERTS, N_EXPERTS), 0))
    slot = _dot(jnp.where(below, 1.0, 0.0).astype(BF16), sel.astype(BF16))
    e_rows, w_rows = [], []
    for s in range(TOP_K):
        here = jnp.where(slot == float(s), sel, 0.0)
        e_rows.append(jnp.sum(here * eidx, axis=0, keepdims=True))
        w_rows.append(jnp.sum(here * gates, axis=0, keepdims=True))
    experts = jnp.concatenate(e_rows, axis=0).astype(I32)
    w_t = jnp.concatenate(w_rows + [jnp.zeros((LANES - TOP_K, n), F32)], axis=0)
    return experts, w_t.T


def _group_norm_gate(o, g):
    parts = []
    for h in range(RET_HEADS):
        oh = o[:, h * RET_V_DIM:(h + 1) * RET_V_DIM]
        mu = jnp.mean(oh, axis=-1, keepdims=True)
        ctr = oh - mu
        var = jnp.mean(ctr * ctr, axis=-1, keepdims=True)
        parts.append(ctr * lax.rsqrt(var + EPS))
    return _silu(g) * jnp.concatenate(parts, axis=-1)


def _pool_project(pooled, w_pool_ref, pool_scale_ref):
    parts = [_dot(p.astype(BF16), w_pool_ref[gi].astype(BF16)) for gi, p in enumerate(pooled)]
    return jnp.concatenate(parts, axis=-1) * pool_scale_ref[...]


def _mix_tail(x, o_gated, p, mod_ref, seq, w_out_ref, norm2_ref, wr_t_ref, bias_t_ref,
              x1_ref, h2_ref, experts_ref, gatew_ref):
    mix = jnp.concatenate([o_gated, p], axis=-1).astype(BF16)
    y = _dot(mix, w_out_ref[...])
    x1 = x + _mod(mod_ref, 2, seq) * y
    h2 = _rms(x1) * norm2_ref[...] * (1.0 + _mod(mod_ref, 4, seq)) + _mod(mod_ref, 3, seq)
    x1_ref[...] = x1
    h2_ref[...] = _pack_rows(h2)
    experts, gate_w = _route(h2, wr_t_ref, bias_t_ref)
    experts_ref[...] = experts
    gatew_ref[...] = gate_w


def _ada_kernel(c_ref, w_ref, b_ref, o_ref):
    cs = _silu(c_ref[...]).astype(BF16)
    o_ref[...] = _dot(cs, w_ref[...].astype(BF16)) + b_ref[...]


def _ada(c_all, w_ada, b_ada, block_n=1536):
    n, d = c_all.shape
    width = w_ada.shape[1]
    return pl.pallas_call(
        _ada_kernel,
        grid=(width // block_n,),
        in_specs=[pl.BlockSpec((n, d), lambda j: (0, 0)),
                  pl.BlockSpec((d, block_n), lambda j: (0, j)),
                  pl.BlockSpec((1, block_n), lambda j: (0, j))],
        out_specs=pl.BlockSpec((n, block_n), lambda j: (0, j)),
        out_shape=jax.ShapeDtypeStruct((n, width), F32),
        compiler_params=pltpu.CompilerParams(vmem_limit_bytes=VMEM_LIMIT),
        name="ada",
    )(c_all, w_ada, b_ada.reshape(1, width))


def _mix_prompt_kernel(x_ref, mod_ref, norm1_ref, w_in_ref, cos_ref, sin_ref, dmat_ref, cross_ref,
                       tail_ref, cdec_ref, w_pool_ref, pool_scale_ref, w_out_ref, norm2_ref,
                       wr_t_ref, bias_t_ref,
                       x1_ref, h2_ref, experts_ref, gatew_ref, ret_ref, pool_ref,
                       state_ref, ext_ref, win_ref, o_ref, *, block_l, chunk, seqs):
    li = pl.program_id(1)

    @pl.when(li == 0)
    def _():
        state_ref[...] = jnp.zeros_like(state_ref)
        ext_ref[:, 0:POOL_CARRY, :] = jnp.zeros((seqs, POOL_CARRY, POOL_WIDTH), F32)
        win_ref[:, 0:SUBLANES, :] = jnp.zeros((seqs, SUBLANES, POOL_WIDTH), F32)

    for seq in range(seqs):
        _mix_prompt_seq(seq, li, x_ref, mod_ref, norm1_ref, w_in_ref, cos_ref, sin_ref, dmat_ref, cross_ref,
                        tail_ref, cdec_ref, w_pool_ref, pool_scale_ref, w_out_ref, norm2_ref,
                        wr_t_ref, bias_t_ref, x1_ref, h2_ref, experts_ref, gatew_ref,
                        state_ref.at[seq], ext_ref.at[seq], win_ref.at[seq], o_ref.at[seq],
                        block_l=block_l, chunk=chunk)

    @pl.when(li == pl.num_programs(1) - 1)
    def _():
        ret_ref[...] = state_ref[...].reshape(ret_ref.shape)
        pool_ref[...] = ext_ref[:, POOL_CARRY - POOL_BUF:POOL_CARRY, :]


def _window_sums(ext_ref, win_ref, block_l):
    g = POOL_GROUP_DIM
    top = POOL_CARRY + block_l
    new = slice(POOL_CARRY - SUBLANES, None)
    s2 = ext_ref[SUBLANES:top, :] + ext_ref[SUBLANES - 1:top - 1, :]
    win_ref[SUBLANES:top, g:] = s2[:, g:]
    s4 = s2[:, g:] + win_ref[SUBLANES - 2:top - 2, g:]
    win_ref[SUBLANES:top, 2 * g:] = s4[:, g:]
    s8 = s4[:, g:] + win_ref[SUBLANES - 4:top - 4, 2 * g:]
    win_ref[SUBLANES:top, 3 * g:] = s8[:, g:]
    s16 = s8[:, g:] + win_ref[0:top - SUBLANES, 3 * g:]
    return [s2[new, 0:g], s4[new, 0:g], s8[new, 0:g], s16[new, :]]


def _mix_prompt_seq(seq, li, x_ref, mod_ref, norm1_ref, w_in_ref, cos_ref, sin_ref, dmat_ref, cross_ref,
                    tail_ref, cdec_ref, w_pool_ref, pool_scale_ref, w_out_ref, norm2_ref,
                    wr_t_ref, bias_t_ref, x1_ref, h2_ref, experts_ref, gatew_ref,
                    state_ref, ext_ref, win_ref, o_ref, *, block_l, chunk):
    x = x_ref[seq]
    h = _rms(x) * norm1_ref[...] * (1.0 + _mod(mod_ref, 1, seq)) + _mod(mod_ref, 0, seq)
    proj = _dot(h.astype(BF16), w_in_ref[...])
    q = proj[:, 0:QK_WIDTH]
    k = proj[:, QK_WIDTH:2 * QK_WIDTH]
    v = proj[:, 2 * QK_WIDTH:2 * QK_WIDTH + RET_WIDTH]
    g = proj[:, 2 * QK_WIDTH + RET_WIDTH:2 * QK_WIDTH + 2 * RET_WIDTH]
    u = proj[:, 2 * QK_WIDTH + 2 * RET_WIDTH:]

    lane = lax.broadcasted_iota(I32, q.shape, 1)
    first_half = (lane % RET_QK_DIM) < (RET_QK_DIM // 2)
    cos_t = cos_ref[...]
    sin_t = sin_ref[...]

    def rot(t):
        partner = jnp.where(first_half, pltpu.roll(t, QK_WIDTH - RET_QK_DIM // 2, axis=1),
                            pltpu.roll(t, RET_QK_DIM // 2, axis=1))
        return t * cos_t + partner * sin_t

    q = rot(q)
    k = rot(k) * (RET_QK_DIM ** -0.5)
    k_t = k.T
    v16 = v.astype(BF16)
    head_of_lane = lax.broadcasted_iota(I32, (chunk, QK_WIDTH), 1) // RET_QK_DIM

    for c in range(block_l // chunk):
        rows = slice(c * chunk, (c + 1) * chunk)
        q_c = q[rows]
        kt_c = k_t[:, rows]
        kt16 = kt_c.astype(BF16)
        state16 = state_ref[...].astype(BF16)
        for hd in range(RET_HEADS):
            in_head = head_of_lane == hd
            q_h = jnp.where(in_head, q_c, 0.0).astype(BF16)
            v_h = v16[rows, hd * RET_V_DIM:(hd + 1) * RET_V_DIM]
            scores = _dot(q_h, kt16) * dmat_ref[hd]
            inner = _dot(scores.astype(BF16), v_h)
            cross = _dot(q_h, state16) * cross_ref[hd]
            o_ref[rows, hd * RET_V_DIM:(hd + 1) * RET_V_DIM] = inner + cross
            hrows = slice(hd * RET_QK_DIM, (hd + 1) * RET_QK_DIM)
            k_dec = (kt_c[hrows] * tail_ref[hd:hd + 1, :]).astype(BF16)
            state_ref[hrows, :] = state_ref[hrows, :] * cdec_ref[hd] + _dot(k_dec, v_h)

    o_gated = _group_norm_gate(o_ref[...], g)

    ext_ref[POOL_CARRY:POOL_CARRY + block_l, :] = u
    pos = (li * block_l + lax.broadcasted_iota(I32, (block_l, 1), 0)).astype(F32)
    pooled = []
    for gi, (w, acc) in enumerate(zip(POOL_WINDOWS, _window_sums(ext_ref, win_ref, block_l))):
        cnt = jnp.minimum(pos + 1.0, float(w))
        pooled.append(acc / cnt - u[:, gi * POOL_GROUP_DIM:(gi + 1) * POOL_GROUP_DIM])
    p = _pool_project(pooled, w_pool_ref, pool_scale_ref)
    ext_ref[0:POOL_CARRY, :] = ext_ref[block_l:block_l + POOL_CARRY, :]

    _mix_tail(x, o_gated, p, mod_ref, seq, w_out_ref, norm2_ref, wr_t_ref, bias_t_ref,
              x1_ref.at[seq], h2_ref.at[seq], experts_ref.at[seq], gatew_ref.at[seq])


def _decay_tables(chunk):
    lg = jnp.log(1.0 - 2.0 ** (-5.0 - jnp.arange(RET_HEADS, dtype=F32)))
    idx = jnp.arange(chunk, dtype=F32)
    diff = idx[:, None] - idx[None, :]
    causal = diff >= 0
    dmat = jnp.where(causal[None], jnp.exp(lg[:, None, None] * jnp.where(causal, diff, 0.0)[None]), 0.0)
    cross = jnp.exp(lg[:, None] * (idx[None, :] + 1.0))
    cross = jnp.broadcast_to(cross[:, :, None], (RET_HEADS, chunk, RET_V_DIM))
    tail = jnp.exp(lg[:, None] * (chunk - 1.0 - idx)[None, :])
    cdec = jnp.broadcast_to(jnp.exp(lg * chunk)[:, None, None], (RET_HEADS, RET_QK_DIM, RET_V_DIM))
    return dmat, cross, tail, cdec


def _rotary_tables(pos):
    half = RET_QK_DIM // 2
    freqs = ROPE_BASE ** (-jnp.arange(half, dtype=F32) / half)
    ang = pos[:, None] * freqs[None, :]
    cos, sin = jnp.cos(ang), jnp.sin(ang)
    cos_t = jnp.tile(jnp.concatenate([cos, cos], axis=-1), (1, RET_HEADS))
    sin_t = jnp.tile(jnp.concatenate([-sin, sin], axis=-1), (1, RET_HEADS))
    return cos_t, sin_t


def _full(shape):
    return pl.BlockSpec(shape, lambda *_: (0,) * len(shape))


def _mix_prompt(x, mod, b0, b, norm1, w_in16, w_pool, pool_scale, w_out16, norm2, wr_t, bias_t,
                block_l=512, chunk=256, seqs=2):
    _, l, d = x.shape
    nl = l // block_l
    s0 = b0 // seqs
    cos_t, sin_t = _rotary_tables(jnp.arange(l, dtype=F32))
    dmat, cross, tail, cdec = _decay_tables(chunk)
    kernel = functools.partial(_mix_prompt_kernel, block_l=block_l, chunk=chunk, seqs=seqs)
    tok = lambda bi, li: (bi, li, 0)
    per_seq = lambda bi, li: (bi, 0, 0)
    x1, h2, experts, gate_w, ret, pool = pl.pallas_call(
        kernel,
        grid=(b // seqs, nl),
        in_specs=[pl.BlockSpec((seqs, block_l, d), lambda bi, li: (s0 + bi, li, 0)),
                  pl.BlockSpec((seqs, 6, d), lambda bi, li: (s0 + bi, 0, 0)),
                  _full((1, d)),
                  _full((d, IN_WIDTH)),
                  pl.BlockSpec((block_l, QK_WIDTH), lambda bi, li: (li, 0)),
                  pl.BlockSpec((block_l, QK_WIDTH), lambda bi, li: (li, 0)),
                  _full(dmat.shape), _full(cross.shape), _full(tail.shape), _full(cdec.shape),
                  _full(w_pool.shape), _full((1, POOL_WIDTH)), _full((d, d)), _full((1, d)),
                  _full(wr_t.shape), _full(bias_t.shape)],
        out_specs=[pl.BlockSpec((seqs, block_l, d), tok),
                   pl.BlockSpec((seqs, block_l, HALF), tok),
                   pl.BlockSpec((seqs, TOP_K, block_l), lambda bi, li: (bi, 0, li)),
                   pl.BlockSpec((seqs, block_l, LANES), tok),
                   pl.BlockSpec((seqs, RET_HEADS, RET_QK_DIM, RET_V_DIM), lambda bi, li: (bi, 0, 0, 0)),
                   pl.BlockSpec((seqs, POOL_BUF, POOL_WIDTH), per_seq)],
        out_shape=[jax.ShapeDtypeStruct((b, l, d), F32),
                   jax.ShapeDtypeStruct((b, l, HALF), U32),
                   jax.ShapeDtypeStruct((b, TOP_K, l), I32),
                   jax.ShapeDtypeStruct((b, l, LANES), F32),
                   jax.ShapeDtypeStruct((b, RET_HEADS, RET_QK_DIM, RET_V_DIM), F32),
                   jax.ShapeDtypeStruct((b, POOL_BUF, POOL_WIDTH), F32)],
        scratch_shapes=[pltpu.VMEM((seqs, QK_WIDTH, RET_V_DIM), F32),
                        pltpu.VMEM((seqs, POOL_CARRY + block_l, POOL_WIDTH), F32),
                        pltpu.VMEM((seqs, POOL_CARRY + block_l, POOL_WIDTH), F32),
                        pltpu.VMEM((seqs, block_l, RET_WIDTH), F32)],
        compiler_params=pltpu.CompilerParams(dimension_semantics=("arbitrary", "arbitrary"),
                                             vmem_limit_bytes=VMEM_LIMIT),
        name="mix_prompt",
    )(x, mod, norm1, w_in16, cos_t, sin_t, dmat, cross, tail, cdec, w_pool, pool_scale,
      w_out16, norm2, wr_t, bias_t)
    experts = jnp.transpose(experts, (1, 0, 2)).reshape(TOP_K, b * l)
    return x1, h2.reshape(b * l, HALF), experts, gate_w.reshape(b * l, LANES), ret, pool


def _mix_sample_front_kernel(x_ref, mod_ref, norm1_ref, w_in_ref, cos_ref, sin_ref,
                             qt_ref, kt_ref, v_ref, g_ref, u_ref):
    x = x_ref[...]
    h = _rms(x) * norm1_ref[...] * (1.0 + _mod(mod_ref, 1)) + _mod(mod_ref, 0)
    proj = _dot(h.astype(BF16), w_in_ref[...])
    half = RET_QK_DIM // 2
    cos_c = cos_ref[...]
    sin_c = sin_ref[...]

    def rot_t(t):
        parts = []
        for hd in range(RET_HEADS):
            t1 = t[hd * RET_QK_DIM:hd * RET_QK_DIM + half]
            t2 = t[hd * RET_QK_DIM + half:(hd + 1) * RET_QK_DIM]
            parts += [t1 * cos_c - t2 * sin_c, t1 * sin_c + t2 * cos_c]
        return jnp.concatenate(parts, axis=0)

    qt_ref[...] = rot_t(proj[:, 0:QK_WIDTH].T)
    kt_ref[...] = rot_t(proj[:, QK_WIDTH:2 * QK_WIDTH].T) * (RET_QK_DIM ** -0.5)
    v_ref[...] = proj[:, 2 * QK_WIDTH:2 * QK_WIDTH + RET_WIDTH]
    g_ref[...] = proj[:, 2 * QK_WIDTH + RET_WIDTH:2 * QK_WIDTH + 2 * RET_WIDTH]
    u_ref[...] = proj[:, 2 * QK_WIDTH + 2 * RET_WIDTH:]


def _ret_step_kernel(qt_ref, kt_ref, v_ref, s0_ref, o_ref, s1_ref, *, block_b, decays):
    i = pl.program_id(0)
    lane = lax.broadcasted_iota(I32, qt_ref.shape, 1)
    for j in range(block_b):
        bi = i * block_b + j
        here = lane == bi
        q_col = jnp.sum(jnp.where(here, qt_ref[...], 0.0), axis=1, keepdims=True)
        k_col = jnp.sum(jnp.where(here, kt_ref[...], 0.0), axis=1, keepdims=True)
        v_row = v_ref[pl.ds(bi, 1), :]
        outs = []
        for hd in range(RET_HEADS):
            hrows = slice(hd * RET_QK_DIM, (hd + 1) * RET_QK_DIM)
            s1 = decays[hd] * s0_ref[j, hd] + k_col[hrows] * v_row[:, hd * RET_V_DIM:(hd + 1) * RET_V_DIM]
            s1_ref[j, hd] = s1
            outs.append(jnp.sum(q_col[hrows] * s1, axis=0, keepdims=True))
        o_ref[pl.ds(bi, 1), :] = jnp.concatenate(outs, axis=-1)


def _mix_sample_back_kernel(x_ref, mod_ref, o_ref, g_ref, u_ref, buf_ref, w_pool_ref, pool_scale_ref,
                            w_out_ref, norm2_ref, wr_t_ref, bias_t_ref,
                            x1_ref, h2_ref, experts_ref, gatew_ref, pool_ref):
    o_gated = _group_norm_gate(o_ref[...], g_ref[...])
    u = u_ref[...]
    pooled = []
    for gi, w in enumerate(POOL_WINDOWS):
        lanes = slice(gi * POOL_GROUP_DIM, (gi + 1) * POOL_GROUP_DIM)
        acc = u[:, lanes]
        for j in range(1, w):
            acc = acc + buf_ref[:, POOL_BUF - j, lanes]
        pooled.append(acc / float(w) - u[:, lanes])
    p = _pool_project(pooled, w_pool_ref, pool_scale_ref)
    pool_ref[:, 0:POOL_BUF - 1, :] = buf_ref[:, 1:POOL_BUF, :]
    pool_ref[:, POOL_BUF - 1, :] = u
    _mix_tail(x_ref[...], o_gated, p, mod_ref, 0, w_out_ref, norm2_ref, wr_t_ref, bias_t_ref,
              x1_ref, h2_ref, experts_ref, gatew_ref)


def _mix_sample(x, mod, state_ret, state_pool, start, norm1, w_in16, w_pool,
                pool_scale, w_out16, norm2, wr_t, bias_t, block_b=32):
    n, d = x.shape
    half = RET_QK_DIM // 2
    freqs = ROPE_BASE ** (-jnp.arange(half, dtype=F32) / half)
    ang = jnp.full((1,), start, F32)[:, None] * freqs[None, :]
    cos_c = jnp.broadcast_to(jnp.cos(ang).T, (half, n))
    sin_c = jnp.broadcast_to(jnp.sin(ang).T, (half, n))
    params = pltpu.CompilerParams(vmem_limit_bytes=VMEM_LIMIT)
    qt, kt, v, g, u = pl.pallas_call(
        _mix_sample_front_kernel,
        out_shape=[jax.ShapeDtypeStruct((QK_WIDTH, n), F32), jax.ShapeDtypeStruct((QK_WIDTH, n), F32),
                   jax.ShapeDtypeStruct((n, RET_WIDTH), F32), jax.ShapeDtypeStruct((n, RET_WIDTH), F32),
                   jax.ShapeDtypeStruct((n, POOL_WIDTH), F32)],
        compiler_params=params,
        name="mix_sample_front",
    )(x, mod, norm1, w_in16, cos_c, sin_c)

    lg = np.log(1.0 - 2.0 ** (-5.0 - np.arange(RET_HEADS, dtype=np.float32)), dtype=np.float32)
    decays = tuple(float(np.exp(lg[h])) for h in range(RET_HEADS))
    state_block = (block_b, RET_HEADS, RET_QK_DIM, RET_V_DIM)
    o, s1 = pl.pallas_call(
        functools.partial(_ret_step_kernel, block_b=block_b, decays=decays),
        grid=(n // block_b,),
        in_specs=[_full((QK_WIDTH, n)), _full((QK_WIDTH, n)), _full((n, RET_WIDTH)),
                  pl.BlockSpec(state_block, lambda i: (i, 0, 0, 0))],
        out_specs=[_full((n, RET_WIDTH)), pl.BlockSpec(state_block, lambda i: (i, 0, 0, 0))],
        out_shape=[jax.ShapeDtypeStruct((n, RET_WIDTH), F32),
                   jax.ShapeDtypeStruct(state_ret.shape, F32)],
        compiler_params=pltpu.CompilerParams(dimension_semantics=("arbitrary",),
                                             vmem_limit_bytes=VMEM_LIMIT),
        name="ret_step",
    )(qt, kt, v, state_ret)

    x1, h2, experts, gate_w, pool = pl.pallas_call(
        _mix_sample_back_kernel,
        out_shape=[jax.ShapeDtypeStruct((n, d), F32),
                   jax.ShapeDtypeStruct((n, HALF), U32),
                   jax.ShapeDtypeStruct((TOP_K, n), I32),
                   jax.ShapeDtypeStruct((n, LANES), F32),
                   jax.ShapeDtypeStruct(state_pool.shape, F32)],
        compiler_params=params,
        name="mix_sample_back",
    )(x, mod, o, g, u, state_pool, w_pool, pool_scale, w_out16, norm2, wr_t, bias_t)
    return x1, h2, experts, gate_w, s1, pool


def _plan_kernel(experts_ref, pos_ref, meta_ref, cnt_ref, carry_ref, off_ref, *, block_t):
    phase = pl.program_id(0)
    j = pl.program_id(1)
    e_blk = experts_ref[...]
    eidx = lax.broadcasted_iota(I32, (N_EXPERTS, block_t), 0)
    member = jnp.zeros((N_EXPERTS, block_t), F32)
    for s in range(TOP_K):
        member = member + jnp.where(eidx == e_blk[s:s + 1, :], 1.0, 0.0)
    per_expert = jnp.broadcast_to(jnp.sum(member, axis=1, keepdims=True), (N_EXPERTS, LANES))

    @pl.when((phase == 0) & (j == 0))
    def _():
        cnt_ref[...] = jnp.zeros_like(cnt_ref)

    @pl.when(phase == 0)
    def _():
        cnt_ref[...] += per_expert

    @pl.when((phase == 0) & (j == pl.num_programs(1) - 1))
    def _():
        cnt = cnt_ref[...]
        n_tile = jnp.floor((cnt + (ROW_TILE - 1.0)) * (1.0 / ROW_TILE))
        upto = (lax.broadcasted_iota(I32, (N_EXPERTS, N_EXPERTS), 1)
                <= lax.broadcasted_iota(I32, (N_EXPERTS, N_EXPERTS), 0))
        tile_end = _dot(jnp.where(upto, 1.0, 0.0).astype(BF16), n_tile.astype(BF16))
        tile_start = tile_end - n_tile
        off_ref[...] = tile_start * ROW_TILE
        carry_ref[...] = jnp.zeros_like(carry_ref)
        lane = lax.broadcasted_iota(I32, cnt.shape, 1)
        meta_ref[...] = jnp.where(lane == 0, tile_start, jnp.where(lane == 1, n_tile, cnt)).astype(I32)

    @pl.when(phase == 1)
    def _():
        before = (lax.broadcasted_iota(I32, (block_t, block_t), 0)
                  < lax.broadcasted_iota(I32, (block_t, block_t), 1))
        rank = _dot(member.astype(BF16), jnp.where(before, 1.0, 0.0).astype(BF16))
        row = off_ref[:, 0:1] + carry_ref[:, 0:1] + rank
        carry_ref[...] += per_expert
        out = [jnp.sum(jnp.where(eidx == e_blk[s:s + 1, :], row, 0.0), axis=0, keepdims=True)
               for s in range(TOP_K)]
        pos_ref[...] = jnp.concatenate(out, axis=0).astype(I32)


def _plan(experts_all, max_block=1024):
    n_tokens = experts_all.shape[1]
    block_t = max(k for k in range(LANES, max_block + 1, LANES) if n_tokens % k == 0)
    nb = n_tokens // block_t
    return pl.pallas_call(
        functools.partial(_plan_kernel, block_t=block_t),
        grid=(2, nb),
        in_specs=[pl.BlockSpec((TOP_K, block_t), lambda ph, j: (0, j))],
        out_specs=[pl.BlockSpec((TOP_K, block_t), lambda ph, j: (0, j * ph)),
                   _full((N_EXPERTS, LANES))],
        out_shape=[jax.ShapeDtypeStruct((TOP_K, n_tokens), I32),
                   jax.ShapeDtypeStruct((N_EXPERTS, LANES), I32)],
        scratch_shapes=[pltpu.VMEM((N_EXPERTS, LANES), F32)] * 3,
        compiler_params=pltpu.CompilerParams(dimension_semantics=("arbitrary", "arbitrary"),
                                             vmem_limit_bytes=VMEM_LIMIT),
        name="plan",
    )(experts_all)


def _sc_workers():
    info = plsc.get_sparse_core_info()
    return info.num_cores, info.num_cores * info.num_subcores


def _sc_scatter_rows(sources, pos_t, n_out, after=()):
    w = sources[0].shape[1]
    s = pos_t.shape[0]
    n_cores, n_workers = _sc_workers()
    bounds = np.cumsum([0] + [src.shape[0] // SC_CHUNK for src in sources])
    n_chunks = int(bounds[-1])
    iters = -(-n_chunks // n_workers)
    mesh = plsc.VectorSubcoreMesh(core_axis_name="c", subcore_axis_name="s")

    @functools.partial(
        pl.kernel, mesh=mesh, out_type=jax.ShapeDtypeStruct((n_out, w), sources[0].dtype),
        scratch_types=[pltpu.VMEM((SC_CHUNK, w), sources[0].dtype), pltpu.VMEM((s, SC_CHUNK), I32),
                       pltpu.SemaphoreType.DMA],
        name="dispatch")
    def k(*refs):
        src_hbm, pos_hbm = refs[:len(sources)], refs[len(sources)]
        out_hbm, rows_v, idx_v, sem = refs[len(sources) + 1 + len(after):]
        wid = lax.axis_index("s") * n_cores + lax.axis_index("c")

        @pl.loop(0, iters)
        def _(it):
            c = it * n_workers + wid
            for src, lo, hi in zip(src_hbm, bounds[:-1], bounds[1:]):
                @pl.when((c >= int(lo)) & (c < int(hi)))
                def _():
                    base = pl.multiple_of((c - int(lo)) * SC_CHUNK, SC_CHUNK)
                    pltpu.sync_copy(src.at[pl.ds(base, SC_CHUNK)], rows_v)

            @pl.when(c < n_chunks)
            def _():
                base = pl.multiple_of(c * SC_CHUNK, SC_CHUNK)
                pltpu.sync_copy(pos_hbm.at[:, pl.ds(base, SC_CHUNK)], idx_v)
                copies = [pltpu.async_copy(rows_v, out_hbm.at[idx_v.at[j]], sem) for j in range(s)]
                for cp in copies:
                    cp.wait()

    return k(*sources, pos_t, *after)


def _sc_gather_rows(table, pos_t):
    _, w = table.shape
    s, t = pos_t.shape
    n_cores, n_workers = _sc_workers()
    n_chunks = t // SC_CHUNK
    iters = -(-n_chunks // n_workers)
    mesh = plsc.VectorSubcoreMesh(core_axis_name="c", subcore_axis_name="s")

    @functools.partial(
        pl.kernel, mesh=mesh, out_type=jax.ShapeDtypeStruct((s, t, w), table.dtype),
        scratch_types=[pltpu.VMEM((SC_CHUNK, w), table.dtype), pltpu.VMEM((s, SC_CHUNK), I32),
                       pltpu.SemaphoreType.DMA],
        name="combine")
    def k(table_hbm, pos_hbm, out_hbm, rows_v, idx_v, sem):
        wid = lax.axis_index("s") * n_cores + lax.axis_index("c")

        @pl.loop(0, iters)
        def _(it):
            c = it * n_workers + wid

            @pl.when(c < n_chunks)
            def _():
                base = pl.multiple_of(c * SC_CHUNK, SC_CHUNK)
                pltpu.sync_copy(pos_hbm.at[:, pl.ds(base, SC_CHUNK)], idx_v)
                for j in range(s):
                    pltpu.async_copy(table_hbm.at[idx_v.at[j]], rows_v, sem).wait()
                    pltpu.sync_copy(rows_v, out_hbm.at[j, pl.ds(base, SC_CHUNK)])

    return k(table, pos_t)


def _sc_pack_weights(w, rows_per_item):
    e, r, c = w.shape
    half = r // 2
    rb = rows_per_item
    per_expert = half // rb
    n_cores, n_workers = _sc_workers()
    per_worker = e * per_expert // n_workers
    assert per_worker * n_workers == e * per_expert and per_worker % 2 == 0 and c % (SC_LANES * SC_UNROLL) == 0
    mesh = plsc.VectorSubcoreMesh(core_axis_name="c", subcore_axis_name="s")

    @functools.partial(
        pl.kernel, mesh=mesh, out_type=jax.ShapeDtypeStruct((e * half, c), U32),
        scratch_types=[pltpu.VMEM((2, rb, c), F32), pltpu.VMEM((2, rb, c), F32), pltpu.VMEM((2, rb, c), U32),
                       pltpu.SemaphoreType.DMA((2,)), pltpu.SemaphoreType.DMA((2,))],
        compiler_params=pltpu.CompilerParams(needs_layout_passes=False),
        cost_estimate=pl.CostEstimate(flops=e * r * c, transcendentals=0, bytes_accessed=6 * e * r * c),
        name="pack_weights")
    def k(w_hbm, out_hbm, lo_v, hi_v, out_v, in_sem, out_sem):
        wid = lax.axis_index("s") * n_cores + lax.axis_index("c")
        first = wid * per_worker

        def rows(item):
            ex = item // per_expert
            j = item - ex * per_expert
            return (pl.multiple_of(ex * r + j * rb, rb), pl.multiple_of(ex * r + half + j * rb, rb),
                    pl.multiple_of(ex * half + j * rb, rb))

        def loads(item, b):
            lo_row, hi_row, _ = rows(item)
            return (pltpu.make_async_copy(w_hbm.at[pl.ds(lo_row, rb)], lo_v.at[b], in_sem.at[b]),
                    pltpu.make_async_copy(w_hbm.at[pl.ds(hi_row, rb)], hi_v.at[b], in_sem.at[b]))

        def store(item, b):
            return pltpu.make_async_copy(out_v.at[b], out_hbm.at[pl.ds(rows(item)[2], rb)], out_sem.at[b])

        for cp in loads(first, 0):
            cp.start()

        @pl.loop(0, per_worker // 2)
        def _(pair):
            for b in range(2):
                item = first + pair * 2 + b
                for cp in loads(item, b):
                    cp.wait()

                @pl.when(item + 1 < first + per_worker)
                def _():
                    for cp in loads(item + 1, 1 - b):
                        cp.start()

                @pl.when(pair > 0)
                def _():
                    store(item - 2, b).wait()

                @pl.loop(0, rb)
                def _(i):
                    @pl.loop(0, c // (SC_LANES * SC_UNROLL))
                    def _(vb):
                        for u in range(SC_UNROLL):
                            sl = pl.ds(pl.multiple_of((vb * SC_UNROLL + u) * SC_LANES, SC_LANES), SC_LANES)
                            packed = plsc.pack(lo_v[b, i, sl], hi_v[b, i, sl], format=plsc.PackFormat.INTERLEAVED)
                            out_v[b, i, sl] = plsc.bitcast(packed, U32)

                store(item, b).start()

        for b in range(2):
            store(first + per_worker - 2 + b, b).wait()

    return k(w.reshape(e * r, c)).reshape(e, half, c)


def _experts_kernel(first_ref, ntile_ref, cnt_ref, xs_hbm, wg_ref, wu_ref, wd_ref, ys_hbm,
                    wg16_ref, wu16_ref, wd16_ref, x_buf, y_buf, in_sem, out_sem):
    e = pl.program_id(0)
    n_used = first_ref[N_EXPERTS - 1] + ntile_ref[N_EXPERTS - 1]
    first, n_mine, count = first_ref[e], ntile_ref[e], cnt_ref[e]

    def tile_rows(g):
        return pl.ds(pl.multiple_of(g * ROW_TILE, ROW_TILE), ROW_TILE)

    def load(g):
        slot = lax.rem(g, STREAM_DEPTH)
        return pltpu.make_async_copy(xs_hbm.at[tile_rows(g)], x_buf.at[slot], in_sem.at[slot])

    def store(g):
        slot = lax.rem(g, STREAM_DEPTH)
        return pltpu.make_async_copy(y_buf.at[slot], ys_hbm.at[tile_rows(g)], out_sem.at[slot])

    @pl.when(e == 0)
    def _():
        for g0 in range(STREAM_DEPTH - 1):
            @pl.when(g0 < n_used)
            def _():
                load(g0).start()

    for packed_ref, w16_ref in ((wg_ref, wg16_ref), (wu_ref, wu16_ref), (wd_ref, wd16_ref)):
        rows = packed_ref.shape[1]
        lo, hi = _unpack_rows(packed_ref[0])
        w16_ref[0:rows, :] = lo.astype(BF16)
        w16_ref[rows:, :] = hi.astype(BF16)

    def tile(j, carry):
        g = first + j
        slot = lax.rem(g, STREAM_DEPTH)
        load(g).wait()

        @pl.when(g + (STREAM_DEPTH - 1) < n_used)
        def _():
            load(g + (STREAM_DEPTH - 1)).start()

        @pl.when(g >= STREAM_DEPTH)
        def _():
            store(g - STREAM_DEPTH).wait()

        words = x_buf[slot]
        row = lax.broadcasted_iota(I32, words.shape, 0)
        words = jnp.where(row < count - j * ROW_TILE, words, jnp.uint32(0))
        lo, hi = _unpack_rows(words)
        lo, hi = lo.astype(BF16), hi.astype(BF16)
        hg = _dot(lo, wg16_ref[0:HALF, :]) + _dot(hi, wg16_ref[HALF:, :])
        hu = _dot(lo, wu16_ref[0:HALF, :]) + _dot(hi, wu16_ref[HALF:, :])
        a = (_silu(hg) * hu).astype(BF16)
        y_buf[slot] = _pack_rows(_dot(a, wd16_ref[...]))
        store(g).start()
        return carry

    lax.fori_loop(0, n_mine, tile, 0)

    @pl.when(e == N_EXPERTS - 1)
    def _():
        for back in range(STREAM_DEPTH, 0, -1):
            @pl.when(n_used >= back)
            def _():
                store(n_used - back).wait()


def _experts(xs, first_tile, n_tile, count, w_eg, w_eu, w_ed):
    d = D_MODEL
    by_expert = lambda e, *_: (e, 0, 0)
    grid_spec = pltpu.PrefetchScalarGridSpec(
        num_scalar_prefetch=3,
        grid=(N_EXPERTS,),
        in_specs=[pl.BlockSpec(memory_space=pl.ANY),
                  pl.BlockSpec((1, d // 2, EXPERT_DIM), by_expert),
                  pl.BlockSpec((1, d // 2, EXPERT_DIM), by_expert),
                  pl.BlockSpec((1, EXPERT_DIM // 2, d), by_expert)],
        out_specs=pl.BlockSpec(memory_space=pl.ANY),
        scratch_shapes=[pltpu.VMEM((d, EXPERT_DIM), BF16), pltpu.VMEM((d, EXPERT_DIM), BF16),
                        pltpu.VMEM((EXPERT_DIM, d), BF16),
                        pltpu.VMEM((STREAM_DEPTH, ROW_TILE, HALF), U32),
                        pltpu.VMEM((STREAM_DEPTH, ROW_TILE, HALF), U32),
                        pltpu.SemaphoreType.DMA((STREAM_DEPTH,)), pltpu.SemaphoreType.DMA((STREAM_DEPTH,))])
    return pl.pallas_call(
        _experts_kernel,
        grid_spec=grid_spec,
        out_shape=jax.ShapeDtypeStruct(xs.shape, U32),
        compiler_params=pltpu.CompilerParams(dimension_semantics=("arbitrary",),
                                             vmem_limit_bytes=VMEM_LIMIT),
        name="experts",
    )(first_tile, n_tile, count, xs, w_eg, w_eu, w_ed)


def _final_kernel(z_ref, gatew_ref, h2_ref, x1_ref, mod_ref, normf_ref, wsg_ref, wsu_ref, wsd_ref, *rest):
    y_ref = rest[-1]
    lo, hi = _unpack_rows(h2_ref[...])
    h = jnp.concatenate([lo, hi], axis=-1).astype(BF16)
    a = _silu(_dot(h, wsg_ref[...])) * _dot(h, wsu_ref[...])
    acc = _dot(a.astype(BF16), wsd_ref[...])
    for s in range(TOP_K):
        lo, hi = _unpack_rows(z_ref[s])
        acc = acc + gatew_ref[:, s:s + 1] * jnp.concatenate([lo, hi], axis=-1)
    x2 = x1_ref[...] + _mod(mod_ref, 5) * acc
    y_ref[...] = _rms(x2) * normf_ref[...]


def _final(z, gate_w, h2, x1, mod, norm_f, w_sg16, w_su16, w_sd16, block_t, first_block, per_seq,
           seq0=0, out_rows=None, y_prev=None):
    t, d = x1.shape
    out_rows = t if out_rows is None else out_rows
    out_first = seq0 * per_seq
    tok = lambda i: (i, 0)
    if per_seq:
        mod_spec = pl.BlockSpec((1, 6, d), lambda i: (seq0 + i // per_seq, 0, 0))
    else:
        mod_spec = pl.BlockSpec((block_t, 6 * d), tok)
    operands = [z, gate_w, h2, x1, mod, norm_f, w_sg16, w_su16, w_sd16]
    in_specs = [pl.BlockSpec((TOP_K, block_t, HALF), lambda i: (0, first_block + i, 0)),
                pl.BlockSpec((block_t, LANES), tok),
                pl.BlockSpec((block_t, HALF), tok),
                pl.BlockSpec((block_t, d), tok),
                mod_spec,
                _full((1, d)),
                _full((d, EXPERT_DIM)), _full((d, EXPERT_DIM)), _full((EXPERT_DIM, d))]
    aliases = {}
    if y_prev is not None:
        aliases = {len(operands): 0}
        operands.append(y_prev)
        in_specs.append(pl.BlockSpec(memory_space=pl.ANY))
    return pl.pallas_call(
        _final_kernel,
        grid=(t // block_t,),
        in_specs=in_specs,
        out_specs=pl.BlockSpec((block_t, d), lambda i: (out_first + i, 0)),
        out_shape=jax.ShapeDtypeStruct((out_rows, d), F32),
        input_output_aliases=aliases,
        compiler_params=pltpu.CompilerParams(dimension_semantics=("arbitrary",),
                                             vmem_limit_bytes=VMEM_LIMIT),
        name="final",
    )(*operands)


def kernel(x_prompt, x_sample, c_prompt, c_sample, state_ret, state_pool, norm1, norm2, norm_f,
           w_ada, b_ada, w_in, w_out, w_pool, pool_scale, w_router, router_bias, w_exp_gate,
           w_exp_up, w_exp_down, w_sh_gate, w_sh_up, w_sh_down):
    b, l, d = x_prompt.shape
    n = x_sample.shape[0]
    past_len = 16384

    mod = _ada(jnp.concatenate([c_prompt, c_sample], axis=0), w_ada[0], b_ada[0])
    mod_p = mod[:b].reshape(b, 6, d)
    mod_s = mod[b:]

    w_in16 = w_in[0].astype(BF16)
    w_out16 = w_out[0].astype(BF16)
    wr_t = w_router[0].T
    bias_t = jnp.broadcast_to(router_bias[0][:, None], (N_EXPERTS, LANES))
    n1, n2, nf = norm1[0].reshape(1, d), norm2[0].reshape(1, d), norm_f.reshape(1, d)
    ps = pool_scale[0].reshape(1, POOL_WIDTH)
    shared = (w_sh_gate[0].astype(BF16), w_sh_up[0].astype(BF16), w_sh_down[0].astype(BF16))

    def routed(sources, experts):
        n_tiles = experts.shape[1] * TOP_K // ROW_TILE + N_EXPERTS
        pos_t, meta = _plan(experts)
        xs = _sc_scatter_rows(sources, pos_t, n_tiles * ROW_TILE, after=expert_w)
        ys = _experts(xs, meta[:, 0], meta[:, 1], meta[:, 2], *expert_w)
        return _sc_gather_rows(ys, pos_t)

    expert_w = (_sc_pack_weights(w_exp_gate[0], 64), _sc_pack_weights(w_exp_up[0], 64),
                _sc_pack_weights(w_exp_down[0], 16))

    ba = b // 2
    bb = b - ba
    mix_args = (n1, w_in16, w_pool[0], ps, w_out16, n2, wr_t, bias_t)
    x1_a, h2_a, experts_a, gatew_a, ret_a, pool_a = _mix_prompt(x_prompt, mod_p, 0, ba, *mix_args)
    z_a = routed((h2_a,), experts_a)
    x1_b, h2_b, experts_b, gatew_b, ret_b, pool_b = _mix_prompt(x_prompt, mod_p, ba, bb, *mix_args)
    x1_s, h2_s, experts_s, gatew_s, ret_s, pool_s = _mix_sample(
        x_sample.reshape(n, d), mod_s, state_ret[0], state_pool[0], float(past_len), *mix_args)
    z_b = routed((h2_b, h2_s), jnp.concatenate([experts_b, experts_s], axis=1))

    block_t = 256
    per_seq = l // block_t
    y_s = _final(z_b, gatew_s, h2_s, x1_s, mod_s, nf, *shared,
                 block_t=n, first_block=bb * l // n, per_seq=0)
    y_p = _final(z_b, gatew_b, h2_b, x1_b.reshape(bb * l, d), mod_p, nf, *shared,
                 block_t=block_t, first_block=0, per_seq=per_seq, seq0=ba, out_rows=b * l)
    y_p = _final(z_a, gatew_a, h2_a, x1_a.reshape(ba * l, d), mod_p, nf, *shared,
                 block_t=block_t, first_block=0, per_seq=per_seq, seq0=0, out_rows=b * l, y_prev=y_p)

    ret_p = jnp.concatenate([ret_a, ret_b], axis=0)
    pool_p = jnp.concatenate([pool_a, pool_b], axis=0)
    return (y_p.reshape(b, l, d), y_s.reshape(n, 1, d), ret_p[None], pool_p[None],
            ret_s[None], pool_s[None])
```

```python
import functools

import jax
import jax.numpy as jnp
import numpy as np
from jax import lax
from jax.experimental import pallas as pl
from jax.experimental.pallas import tpu as pltpu
from jax.experimental.pallas import tpu_sc as plsc

D_MODEL = 1024
RET_HEADS = 4
RET_QK_DIM = 64
RET_V_DIM = 128
RET_WIDTH = RET_HEADS * RET_V_DIM
QK_WIDTH = RET_HEADS * RET_QK_DIM
ROPE_BASE = 10000.0
POOL_WINDOWS = (2, 4, 8, 16)
POOL_WIDTH = 512
POOL_GROUP_DIM = 128
POOL_BUF = 15
IN_WIDTH = 2 * QK_WIDTH + 2 * RET_WIDTH + POOL_WIDTH
N_EXPERTS = 64
TOP_K = 8
N_EXPERT_GROUPS = 8
GROUP_SIZE = N_EXPERTS // N_EXPERT_GROUPS
TOP_GROUPS = 4
EXPERT_DIM = 256
ROUTE_SCALE = 2.5
EPS = 1e-6

LANES = 128
SUBLANES = 8
POOL_CARRY = 24
VMEM_LIMIT = 56 * 1024 * 1024
HALF = D_MODEL // 2
ROW_TILE = 256
SC_CHUNK = 128
SC_LANES = 16
SC_UNROLL = 16
STREAM_DEPTH = 8

BF16 = jnp.bfloat16
F32 = jnp.float32
U32 = jnp.uint32
I32 = jnp.int32


def _silu(x):
    return x * jax.nn.sigmoid(x)


def _dot(a, b):
    return jnp.dot(a, b, preferred_element_type=F32)


def _rms(x):
    return x * lax.rsqrt(jnp.mean(x * x, axis=-1, keepdims=True) + EPS)


def _mod(mod_ref, i, seq=0):
    if len(mod_ref.shape) == 3:
        return mod_ref[seq, i:i + 1, :]
    return mod_ref[:, i * D_MODEL:(i + 1) * D_MODEL]


def _split_bf16(x):
    hi = x.astype(BF16)
    lo = (x - hi.astype(F32)).astype(BF16)
    return hi, lo


def _pack_rows(x):
    lo = lax.bitcast_convert_type(x[:, :HALF].astype(BF16).astype(F32), U32)
    hi = lax.bitcast_convert_type(x[:, HALF:].astype(BF16).astype(F32), U32)
    return (hi & jnp.uint32(0xFFFF0000)) | (lo >> jnp.uint32(16))


def _unpack_rows(w):
    lo = lax.bitcast_convert_type(w << jnp.uint32(16), F32)
    hi = lax.bitcast_convert_type(w & jnp.uint32(0xFFFF0000), F32)
    return lo, hi


def _first_max_onehot(work, idx, n):
    m = jnp.max(work, axis=0, keepdims=True)
    first = jnp.min(jnp.where(work == m, idx, float(n)), axis=0, keepdims=True)
    return idx == first


def _route(h2, wr_t_ref, bias_t_ref):
    n = h2.shape[0]
    h_hi, h_lo = _split_bf16(h2)
    w_hi, w_lo = _split_bf16(wr_t_ref[...])
    nt = (((1,), (1,)), ((), ()))
    logits = (lax.dot_general(w_hi, h_hi, nt, preferred_element_type=F32)
              + lax.dot_general(w_hi, h_lo, nt, preferred_element_type=F32)
              + lax.dot_general(w_lo, h_hi, nt, preferred_element_type=F32))
    scores = jax.nn.sigmoid(logits)
    biased = scores + bias_t_ref[:, 0:1]
    b3 = biased.reshape(N_EXPERT_GROUPS, GROUP_SIZE, n)
    i3 = lax.broadcasted_iota(I32, b3.shape, 1).astype(F32)
    m1 = jnp.max(b3, axis=1, keepdims=True)
    first = jnp.min(jnp.where(b3 == m1, i3, float(GROUP_SIZE)), axis=1, keepdims=True)
    m2 = jnp.max(jnp.where(i3 == first, -jnp.inf, b3), axis=1, keepdims=True)
    gscore = (m1 + m2).reshape(N_EXPERT_GROUPS, n)
    gidx = lax.broadcasted_iota(I32, gscore.shape, 0).astype(F32)
    gsel = jnp.zeros(gscore.shape, F32)
    work = gscore
    for _ in range(TOP_GROUPS):
        hit = _first_max_onehot(work, gidx, N_EXPERT_GROUPS)
        gsel = jnp.where(hit, 1.0, gsel)
        work = jnp.where(hit, -jnp.inf, work)
    gsel3 = jnp.broadcast_to(gsel.reshape(N_EXPERT_GROUPS, 1, n), b3.shape)
    work = jnp.where(gsel3 > 0.0, b3, -jnp.inf).reshape(N_EXPERTS, n)
    eidx = lax.broadcasted_iota(I32, work.shape, 0).astype(F32)
    sel = jnp.zeros(work.shape, F32)
    for _ in range(TOP_K):
        hit = _first_max_onehot(work, eidx, N_EXPERTS)
        sel = jnp.where(hit, 1.0, sel)
        work = jnp.where(hit, -jnp.inf, work)
    picked = jnp.where(sel > 0.0, scores, 0.0)
    gates = picked / jnp.sum(picked, axis=0, keepdims=True) * ROUTE_SCALE
    below = (lax.broadcasted_iota(I32, (N_EXPERTS, N_EXPERTS), 1)
             < lax.broadcasted_iota(I32, (N_EXPERTS, N_EXPERTS), 0))
    slot = _dot(jnp.where(below, 1.0, 0.0).astype(BF16), sel.astype(BF16))
    e_rows, w_rows = [], []
    for s in range(TOP_K):
        here = jnp.where(slot == float(s), sel, 0.0)
        e_rows.append(jnp.sum(here * eidx, axis=0, keepdims=True))
        w_rows.append(jnp.sum(here * gates, axis=0, keepdims=True))
    experts = jnp.concatenate(e_rows, axis=0).astype(I32)
    w_t = jnp.concatenate(w_rows + [jnp.zeros((LANES - TOP_K, n), F32)], axis=0)
    return experts, w_t.T


def _group_norm_gate(o, g):
    parts = []
    for h in range(RET_HEADS):
        oh = o[:, h * RET_V_DIM:(h + 1) * RET_V_DIM]
        mu = jnp.mean(oh, axis=-1, keepdims=True)
        ctr = oh - mu
        var = jnp.mean(ctr * ctr, axis=-1, keepdims=True)
        parts.append(ctr * lax.rsqrt(var + EPS))
    return _silu(g) * jnp.concatenate(parts, axis=-1)


def _pool_project(pooled, w_pool_ref, pool_scale_ref):
    parts = [_dot(p.astype(BF16), w_pool_ref[gi].astype(BF16)) for gi, p in enumerate(pooled)]
    return jnp.concatenate(parts, axis=-1) * pool_scale_ref[...]


def _mix_tail(x, o_gated, p, mod_ref, seq, w_out_ref, norm2_ref, wr_t_ref, bias_t_ref,
              x1_ref, h2_ref, experts_ref, gatew_ref):
    mix = jnp.concatenate([o_gated, p], axis=-1).astype(BF16)
    y = _dot(mix, w_out_ref[...])
    x1 = x + _mod(mod_ref, 2, seq) * y
    h2 = _rms(x1) * norm2_ref[...] * (1.0 + _mod(mod_ref, 4, seq)) + _mod(mod_ref, 3, seq)
    x1_ref[...] = x1
    h2_ref[...] = _pack_rows(h2)
    experts, gate_w = _route(h2, wr_t_ref, bias_t_ref)
    experts_ref[...] = experts
    gatew_ref[...] = gate_w


def _ada_kernel(c_ref, w_ref, b_ref, o_ref):
    cs = _silu(c_ref[...]).astype(BF16)
    o_ref[...] = _dot(cs, w_ref[...].astype(BF16)) + b_ref[...]


def _ada(c_all, w_ada, b_ada, block_n=1536):
    n, d = c_all.shape
    width = w_ada.shape[1]
    return pl.pallas_call(
        _ada_kernel,
        grid=(width // block_n,),
        in_specs=[pl.BlockSpec((n, d), lambda j: (0, 0)),
                  pl.BlockSpec((d, block_n), lambda j: (0, j)),
                  pl.BlockSpec((1, block_n), lambda j: (0, j))],
        out_specs=pl.BlockSpec((n, block_n), lambda j: (0, j)),
        out_shape=jax.ShapeDtypeStruct((n, width), F32),
        compiler_params=pltpu.CompilerParams(vmem_limit_bytes=VMEM_LIMIT),
        name="ada",
    )(c_all, w_ada, b_ada.reshape(1, width))


def _mix_prompt_kernel(x_ref, mod_ref, norm1_ref, w_in_ref, cos_ref, sin_ref, dmat_ref, cross_ref,
                       tail_ref, cdec_ref, w_pool_ref, pool_scale_ref, w_out_ref, norm2_ref,
                       wr_t_ref, bias_t_ref,
                       x1_ref, h2_ref, experts_ref, gatew_ref, ret_ref, pool_ref,
                       state_ref, ext_ref, win_ref, o_ref, *, block_l, chunk, seqs):
    li = pl.program_id(1)

    @pl.when(li == 0)
    def _():
        state_ref[...] = jnp.zeros_like(state_ref)
        ext_ref[:, 0:POOL_CARRY, :] = jnp.zeros((seqs, POOL_CARRY, POOL_WIDTH), F32)
        win_ref[:, 0:SUBLANES, :] = jnp.zeros((seqs, SUBLANES, POOL_WIDTH), F32)

    for seq in range(seqs):
        _mix_prompt_seq(seq, li, x_ref, mod_ref, norm1_ref, w_in_ref, cos_ref, sin_ref, dmat_ref, cross_ref,
                        tail_ref, cdec_ref, w_pool_ref, pool_scale_ref, w_out_ref, norm2_ref,
                        wr_t_ref, bias_t_ref, x1_ref, h2_ref, experts_ref, gatew_ref,
                        state_ref.at[seq], ext_ref.at[seq], win_ref.at[seq], o_ref.at[seq],
                        block_l=block_l, chunk=chunk)

    @pl.when(li == pl.num_programs(1) - 1)
    def _():
        ret_ref[...] = state_ref[...].reshape(ret_ref.shape)
        pool_ref[...] = ext_ref[:, POOL_CARRY - POOL_BUF:POOL_CARRY, :]


def _window_sums(ext_ref, win_ref, block_l):
    g = POOL_GROUP_DIM
    top = POOL_CARRY + block_l
    new = slice(POOL_CARRY - SUBLANES, None)
    s2 = ext_ref[SUBLANES:top, :] + ext_ref[SUBLANES - 1:top - 1, :]
    win_ref[SUBLANES:top, g:] = s2[:, g:]
    s4 = s2[:, g:] + win_ref[SUBLANES - 2:top - 2, g:]
    win_ref[SUBLANES:top, 2 * g:] = s4[:, g:]
    s8 = s4[:, g:] + win_ref[SUBLANES - 4:top - 4, 2 * g:]
    win_ref[SUBLANES:top, 3 * g:] = s8[:, g:]
    s16 = s8[:, g:] + win_ref[0:top - SUBLANES, 3 * g:]
    return [s2[new, 0:g], s4[new, 0:g], s8[new, 0:g], s16[new, :]]


def _mix_prompt_seq(seq, li, x_ref, mod_ref, norm1_ref, w_in_ref, cos_ref, sin_ref, dmat_ref, cross_ref,
                    tail_ref, cdec_ref, w_pool_ref, pool_scale_ref, w_out_ref, norm2_ref,
                    wr_t_ref, bias_t_ref, x1_ref, h2_ref, experts_ref, gatew_ref,
                    state_ref, ext_ref, win_ref, o_ref, *, block_l, chunk):
    x = x_ref[seq]
    h = _rms(x) * norm1_ref[...] * (1.0 + _mod(mod_ref, 1, seq)) + _mod(mod_ref, 0, seq)
    proj = _dot(h.astype(BF16), w_in_ref[...])
    q = proj[:, 0:QK_WIDTH]
    k = proj[:, QK_WIDTH:2 * QK_WIDTH]
    v = proj[:, 2 * QK_WIDTH:2 * QK_WIDTH + RET_WIDTH]
    g = proj[:, 2 * QK_WIDTH + RET_WIDTH:2 * QK_WIDTH + 2 * RET_WIDTH]
    u = proj[:, 2 * QK_WIDTH + 2 * RET_WIDTH:]

    lane = lax.broadcasted_iota(I32, q.shape, 1)
    first_half = (lane % RET_QK_DIM) < (RET_QK_DIM // 2)
    cos_t = cos_ref[...]
    sin_t = sin_ref[...]

    def rot(t):
        partner = jnp.where(first_half, pltpu.roll(t, QK_WIDTH - RET_QK_DIM // 2, axis=1),
                            pltpu.roll(t, RET_QK_DIM // 2, axis=1))
        return t * cos_t + partner * sin_t

    q = rot(q)
    k = rot(k) * (RET_QK_DIM ** -0.5)
    k_t = k.T
    v16 = v.astype(BF16)
    head_of_lane = lax.broadcasted_iota(I32, (chunk, QK_WIDTH), 1) // RET_QK_DIM

    for c in range(block_l // chunk):
        rows = slice(c * chunk, (c + 1) * chunk)
        q_c = q[rows]
        kt_c = k_t[:, rows]
        kt16 = kt_c.astype(BF16)
        state16 = state_ref[...].astype(BF16)
        for hd in range(RET_HEADS):
            in_head = head_of_lane == hd
            q_h = jnp.where(in_head, q_c, 0.0).astype(BF16)
            v_h = v16[rows, hd * RET_V_DIM:(hd + 1) * RET_V_DIM]
            scores = _dot(q_h, kt16) * dmat_ref[hd]
            inner = _dot(scores.astype(BF16), v_h)
            cross = _dot(q_h, state16) * cross_ref[hd]
            o_ref[rows, hd * RET_V_DIM:(hd + 1) * RET_V_DIM] = inner + cross
            hrows = slice(hd * RET_QK_DIM, (hd + 1) * RET_QK_DIM)
            k_dec = (kt_c[hrows] * tail_ref[hd:hd + 1, :]).astype(BF16)
            state_ref[hrows, :] = state_ref[hrows, :] * cdec_ref[hd] + _dot(k_dec, v_h)

    o_gated = _group_norm_gate(o_ref[...], g)

    ext_ref[POOL_CARRY:POOL_CARRY + block_l, :] = u
    pos = (li * block_l + lax.broadcasted_iota(I32, (block_l, 1), 0)).astype(F32)
    pooled = []
    for gi, (w, acc) in enumerate(zip(POOL_WINDOWS, _window_sums(ext_ref, win_ref, block_l))):
        cnt = jnp.minimum(pos + 1.0, float(w))
        pooled.append(acc / cnt - u[:, gi * POOL_GROUP_DIM:(gi + 1) * POOL_GROUP_DIM])
    p = _pool_project(pooled, w_pool_ref, pool_scale_ref)
    ext_ref[0:POOL_CARRY, :] = ext_ref[block_l:block_l + POOL_CARRY, :]

    _mix_tail(x, o_gated, p, mod_ref, seq, w_out_ref, norm2_ref, wr_t_ref, bias_t_ref,
              x1_ref.at[seq], h2_ref.at[seq], experts_ref.at[seq], gatew_ref.at[seq])


def _decay_tables(chunk):
    lg = jnp.log(1.0 - 2.0 ** (-5.0 - jnp.arange(RET_HEADS, dtype=F32)))
    idx = jnp.arange(chunk, dtype=F32)
    diff = idx[:, None] - idx[None, :]
    causal = diff >= 0
    dmat = jnp.where(causal[None], jnp.exp(lg[:, None, None] * jnp.where(causal, diff, 0.0)[None]), 0.0)
    cross = jnp.exp(lg[:, None] * (idx[None, :] + 1.0))
    cross = jnp.broadcast_to(cross[:, :, None], (RET_HEADS, chunk, RET_V_DIM))
    tail = jnp.exp(lg[:, None] * (chunk - 1.0 - idx)[None, :])
    cdec = jnp.broadcast_to(jnp.exp(lg * chunk)[:, None, None], (RET_HEADS, RET_QK_DIM, RET_V_DIM))
    return dmat, cross, tail, cdec


def _rotary_tables(pos):
    half = RET_QK_DIM // 2
    freqs = ROPE_BASE ** (-jnp.arange(half, dtype=F32) / half)
    ang = pos[:, None] * freqs[None, :]
    cos, sin = jnp.cos(ang), jnp.sin(ang)
    cos_t = jnp.tile(jnp.concatenate([cos, cos], axis=-1), (1, RET_HEADS))
    sin_t = jnp.tile(jnp.concatenate([-sin, sin], axis=-1), (1, RET_HEADS))
    return cos_t, sin_t


def _full(shape):
    return pl.BlockSpec(shape, lambda *_: (0,) * len(shape))


def _mix_prompt(x, mod, b0, b, norm1, w_in16, w_pool, pool_scale, w_out16, norm2, wr_t, bias_t,
                block_l=512, chunk=256, seqs=2):
    _, l, d = x.shape
    nl = l // block_l
    s0 = b0 // seqs
    cos_t, sin_t = _rotary_tables(jnp.arange(l, dtype=F32))
    dmat, cross, tail, cdec = _decay_tables(chunk)
    kernel = functools.partial(_mix_prompt_kernel, block_l=block_l, chunk=chunk, seqs=seqs)
    tok = lambda bi, li: (bi, li, 0)
    per_seq = lambda bi, li: (bi, 0, 0)
    x1, h2, experts, gate_w, ret, pool = pl.pallas_call(
        kernel,
        grid=(b // seqs, nl),
        in_specs=[pl.BlockSpec((seqs, block_l, d), lambda bi, li: (s0 + bi, li, 0)),
                  pl.BlockSpec((seqs, 6, d), lambda bi, li: (s0 + bi, 0, 0)),
                  _full((1, d)),
                  _full((d, IN_WIDTH)),
                  pl.BlockSpec((block_l, QK_WIDTH), lambda bi, li: (li, 0)),
                  pl.BlockSpec((block_l, QK_WIDTH), lambda bi, li: (li, 0)),
                  _full(dmat.shape), _full(cross.shape), _full(tail.shape), _full(cdec.shape),
                  _full(w_pool.shape), _full((1, POOL_WIDTH)), _full((d, d)), _full((1, d)),
                  _full(wr_t.shape), _full(bias_t.shape)],
        out_specs=[pl.BlockSpec((seqs, block_l, d), tok),
                   pl.BlockSpec((seqs, block_l, HALF), tok),
                   pl.BlockSpec((seqs, TOP_K, block_l), lambda bi, li: (bi, 0, li)),
                   pl.BlockSpec((seqs, block_l, LANES), tok),
                   pl.BlockSpec((seqs, RET_HEADS, RET_QK_DIM, RET_V_DIM), lambda bi, li: (bi, 0, 0, 0)),
                   pl.BlockSpec((seqs, POOL_BUF, POOL_WIDTH), per_seq)],
        out_shape=[jax.ShapeDtypeStruct((b, l, d), F32),
                   jax.ShapeDtypeStruct((b, l, HALF), U32),
                   jax.ShapeDtypeStruct((b, TOP_K, l), I32),
                   jax.ShapeDtypeStruct((b, l, LANES), F32),
                   jax.ShapeDtypeStruct((b, RET_HEADS, RET_QK_DIM, RET_V_DIM), F32),
                   jax.ShapeDtypeStruct((b, POOL_BUF, POOL_WIDTH), F32)],
        scratch_shapes=[pltpu.VMEM((seqs, QK_WIDTH, RET_V_DIM), F32),
                        pltpu.VMEM((seqs, POOL_CARRY + block_l, POOL_WIDTH), F32),
                        pltpu.VMEM((seqs, POOL_CARRY + block_l, POOL_WIDTH), F32),
                        pltpu.VMEM((seqs, block_l, RET_WIDTH), F32)],
        compiler_params=pltpu.CompilerParams(dimension_semantics=("arbitrary", "arbitrary"),
                                             vmem_limit_bytes=VMEM_LIMIT),
        name="mix_prompt",
    )(x, mod, norm1, w_in16, cos_t, sin_t, dmat, cross, tail, cdec, w_pool, pool_scale,
      w_out16, norm2, wr_t, bias_t)
    experts = jnp.transpose(experts, (1, 0, 2)).reshape(TOP_K, b * l)
    return x1, h2.reshape(b * l, HALF), experts, gate_w.reshape(b * l, LANES), ret, pool


def _mix_sample_front_kernel(x_ref, mod_ref, norm1_ref, w_in_ref, cos_ref, sin_ref,
                             qt_ref, kt_ref, v_ref, g_ref, u_ref):
    x = x_ref[...]
    h = _rms(x) * norm1_ref[...] * (1.0 + _mod(mod_ref, 1)) + _mod(mod_ref, 0)
    proj = _dot(h.astype(BF16), w_in_ref[...])
    half = RET_QK_DIM // 2
    cos_c = cos_ref[...]
    sin_c = sin_ref[...]

    def rot_t(t):
        parts = []
        for hd in range(RET_HEADS):
            t1 = t[hd * RET_QK_DIM:hd * RET_QK_DIM + half]
            t2 = t[hd * RET_QK_DIM + half:(hd + 1) * RET_QK_DIM]
            parts += [t1 * cos_c - t2 * sin_c, t1 * sin_c + t2 * cos_c]
        return jnp.concatenate(parts, axis=0)

    qt_ref[...] = rot_t(proj[:, 0:QK_WIDTH].T)
    kt_ref[...] = rot_t(proj[:, QK_WIDTH:2 * QK_WIDTH].T) * (RET_QK_DIM ** -0.5)
    v_ref[...] = proj[:, 2 * QK_WIDTH:2 * QK_WIDTH + RET_WIDTH]
    g_ref[...] = proj[:, 2 * QK_WIDTH + RET_WIDTH:2 * QK_WIDTH + 2 * RET_WIDTH]
    u_ref[...] = proj[:, 2 * QK_WIDTH + 2 * RET_WIDTH:]


def _ret_step_kernel(qt_ref, kt_ref, v_ref, s0_ref, o_ref, s1_ref, *, block_b, decays):
    i = pl.program_id(0)
    lane = lax.broadcasted_iota(I32, qt_ref.shape, 1)
    for j in range(block_b):
        bi = i * block_b + j
        here = lane == bi
        q_col = jnp.sum(jnp.where(here, qt_ref[...], 0.0), axis=1, keepdims=True)
        k_col = jnp.sum(jnp.where(here, kt_ref[...], 0.0), axis=1, keepdims=True)
        v_row = v_ref[pl.ds(bi, 1), :]
        outs = []
        for hd in range(RET_HEADS):
            hrows = slice(hd * RET_QK_DIM, (hd + 1) * RET_QK_DIM)
            s1 = decays[hd] * s0_ref[j, hd] + k_col[hrows] * v_row[:, hd * RET_V_DIM:(hd + 1) * RET_V_DIM]
            s1_ref[j, hd] = s1
            outs.append(jnp.sum(q_col[hrows] * s1, axis=0, keepdims=True))
        o_ref[pl.ds(bi, 1), :] = jnp.concatenate(outs, axis=-1)


def _mix_sample_back_kernel(x_ref, mod_ref, o_ref, g_ref, u_ref, buf_ref, w_pool_ref, pool_scale_ref,
                            w_out_ref, norm2_ref, wr_t_ref, bias_t_ref,
                            x1_ref, h2_ref, experts_ref, gatew_ref, pool_ref):
    o_gated = _group_norm_gate(o_ref[...], g_ref[...])
    u = u_ref[...]
    pooled = []
    for gi, w in enumerate(POOL_WINDOWS):
        lanes = slice(gi * POOL_GROUP_DIM, (gi + 1) * POOL_GROUP_DIM)
        acc = u[:, lanes]
        for j in range(1, w):
            acc = acc + buf_ref[:, POOL_BUF - j, lanes]
        pooled.append(acc / float(w) - u[:, lanes])
    p = _pool_project(pooled, w_pool_ref, pool_scale_ref)
    pool_ref[:, 0:POOL_BUF - 1, :] = buf_ref[:, 1:POOL_BUF, :]
    pool_ref[:, POOL_BUF - 1, :] = u
    _mix_tail(x_ref[...], o_gated, p, mod_ref, 0, w_out_ref, norm2_ref, wr_t_ref, bias_t_ref,
              x1_ref, h2_ref, experts_ref, gatew_ref)


def _mix_sample(x, mod, state_ret, state_pool, start, norm1, w_in16, w_pool,
                pool_scale, w_out16, norm2, wr_t, bias_t, block_b=32):
    n, d = x.shape
    half = RET_QK_DIM // 2
    freqs = ROPE_BASE ** (-jnp.arange(half, dtype=F32) / half)
    ang = jnp.full((1,), start, F32)[:, None] * freqs[None, :]
    cos_c = jnp.broadcast_to(jnp.cos(ang).T, (half, n))
    sin_c = jnp.broadcast_to(jnp.sin(ang).T, (half, n))
    params = pltpu.CompilerParams(vmem_limit_bytes=VMEM_LIMIT)
    qt, kt, v, g, u = pl.pallas_call(
        _mix_sample_front_kernel,
        out_shape=[jax.ShapeDtypeStruct((QK_WIDTH, n), F32), jax.ShapeDtypeStruct((QK_WIDTH, n), F32),
                   jax.ShapeDtypeStruct((n, RET_WIDTH), F32), jax.ShapeDtypeStruct((n, RET_WIDTH), F32),
                   jax.ShapeDtypeStruct((n, POOL_WIDTH), F32)],
        compiler_params=params,
        name="mix_sample_front",
    )(x, mod, norm1, w_in16, cos_c, sin_c)

    lg = np.log(1.0 - 2.0 ** (-5.0 - np.arange(RET_HEADS, dtype=np.float32)), dtype=np.float32)
    decays = tuple(float(np.exp(lg[h])) for h in range(RET_HEADS))
    state_block = (block_b, RET_HEADS, RET_QK_DIM, RET_V_DIM)
    o, s1 = pl.pallas_call(
        functools.partial(_ret_step_kernel, block_b=block_b, decays=decays),
        grid=(n // block_b,),
        in_specs=[_full((QK_WIDTH, n)), _full((QK_WIDTH, n)), _full((n, RET_WIDTH)),
                  pl.BlockSpec(state_block, lambda i: (i, 0, 0, 0))],
        out_specs=[_full((n, RET_WIDTH)), pl.BlockSpec(state_block, lambda i: (i, 0, 0, 0))],
        out_shape=[jax.ShapeDtypeStruct((n, RET_WIDTH), F32),
                   jax.ShapeDtypeStruct(state_ret.shape, F32)],
        compiler_params=pltpu.CompilerParams(dimension_semantics=("arbitrary",),
                                             vmem_limit_bytes=VMEM_LIMIT),
        name="ret_step",
    )(qt, kt, v, state_ret)

    x1, h2, experts, gate_w, pool = pl.pallas_call(
        _mix_sample_back_kernel,
        out_shape=[jax.ShapeDtypeStruct((n, d), F32),
                   jax.ShapeDtypeStruct((n, HALF), U32),
                   jax.ShapeDtypeStruct((TOP_K, n), I32),
                   jax.ShapeDtypeStruct((n, LANES), F32),
                   jax.ShapeDtypeStruct(state_pool.shape, F32)],
        compiler_params=params,
        name="mix_sample_back",
    )(x, mod, o, g, u, state_pool, w_pool, pool_scale, w_out16, norm2, wr_t, bias_t)
    return x1, h2, experts, gate_w, s1, pool


def _plan_kernel(experts_ref, pos_ref, meta_ref, cnt_ref, carry_ref, off_ref, *, block_t):
    phase = pl.program_id(0)
    j = pl.program_id(1)
    e_blk = experts_ref[...]
    eidx = lax.broadcasted_iota(I32, (N_EXPERTS, block_t), 0)
    member = jnp.zeros((N_EXPERTS, block_t), F32)
    for s in range(TOP_K):
        member = member + jnp.where(eidx == e_blk[s:s + 1, :], 1.0, 0.0)
    per_expert = jnp.broadcast_to(jnp.sum(member, axis=1, keepdims=True), (N_EXPERTS, LANES))

    @pl.when((phase == 0) & (j == 0))
    def _():
        cnt_ref[...] = jnp.zeros_like(cnt_ref)

    @pl.when(phase == 0)
    def _():
        cnt_ref[...] += per_expert

    @pl.when((phase == 0) & (j == pl.num_programs(1) - 1))
    def _():
        cnt = cnt_ref[...]
        n_tile = jnp.floor((cnt + (ROW_TILE - 1.0)) * (1.0 / ROW_TILE))
        upto = (lax.broadcasted_iota(I32, (N_EXPERTS, N_EXPERTS), 1)
                <= lax.broadcasted_iota(I32, (N_EXPERTS, N_EXPERTS), 0))
        tile_end = _dot(jnp.where(upto, 1.0, 0.0).astype(BF16), n_tile.astype(BF16))
        tile_start = tile_end - n_tile
        off_ref[...] = tile_start * ROW_TILE
        carry_ref[...] = jnp.zeros_like(carry_ref)
        lane = lax.broadcasted_iota(I32, cnt.shape, 1)
        meta_ref[...] = jnp.where(lane == 0, tile_start, jnp.where(lane == 1, n_tile, cnt)).astype(I32)

    @pl.when(phase == 1)
    def _():
        before = (lax.broadcasted_iota(I32, (block_t, block_t), 0)
                  < lax.broadcasted_iota(I32, (block_t, block_t), 1))
        rank = _dot(member.astype(BF16), jnp.where(before, 1.0, 0.0).astype(BF16))
        row = off_ref[:, 0:1] + carry_ref[:, 0:1] + rank
        carry_ref[...] += per_expert
        out = [jnp.sum(jnp.where(eidx == e_blk[s:s + 1, :], row, 0.0), axis=0, keepdims=True)
               for s in range(TOP_K)]
        pos_ref[...] = jnp.concatenate(out, axis=0).astype(I32)


def _plan(experts_all, max_block=1024):
    n_tokens = experts_all.shape[1]
    block_t = max(k for k in range(LANES, max_block + 1, LANES) if n_tokens % k == 0)
    nb = n_tokens // block_t
    return pl.pallas_call(
        functools.partial(_plan_kernel, block_t=block_t),
        grid=(2, nb),
        in_specs=[pl.BlockSpec((TOP_K, block_t), lambda ph, j: (0, j))],
        out_specs=[pl.BlockSpec((TOP_K, block_t), lambda ph, j: (0, j * ph)),
                   _full((N_EXPERTS, LANES))],
        out_shape=[jax.ShapeDtypeStruct((TOP_K, n_tokens), I32),
                   jax.ShapeDtypeStruct((N_EXPERTS, LANES), I32)],
        scratch_shapes=[pltpu.VMEM((N_EXPERTS, LANES), F32)] * 3,
        compiler_params=pltpu.CompilerParams(dimension_semantics=("arbitrary", "arbitrary"),
                                             vmem_limit_bytes=VMEM_LIMIT),
        name="plan",
    )(experts_all)


def _sc_workers():
    info = plsc.get_sparse_core_info()
    return info.num_cores, info.num_cores * info.num_subcores


def _sc_scatter_rows(sources, pos_t, n_out, after=()):
    w = sources[0].shape[1]
    s = pos_t.shape[0]
    n_cores, n_workers = _sc_workers()
    bounds = np.cumsum([0] + [src.shape[0] // SC_CHUNK for src in sources])
    n_chunks = int(bounds[-1])
    iters = -(-n_chunks // n_workers)
    mesh = plsc.VectorSubcoreMesh(core_axis_name="c", subcore_axis_name="s")

    @functools.partial(
        pl.kernel, mesh=mesh, out_type=jax.ShapeDtypeStruct((n_out, w), sources[0].dtype),
        scratch_types=[pltpu.VMEM((SC_CHUNK, w), sources[0].dtype), pltpu.VMEM((s, SC_CHUNK), I32),
                       pltpu.SemaphoreType.DMA],
        name="dispatch")
    def k(*refs):
        src_hbm, pos_hbm = refs[:len(sources)], refs[len(sources)]
        out_hbm, rows_v, idx_v, sem = refs[len(sources) + 1 + len(after):]
        wid = lax.axis_index("s") * n_cores + lax.axis_index("c")

        @pl.loop(0, iters)
        def _(it):
            c = it * n_workers + wid
            for src, lo, hi in zip(src_hbm, bounds[:-1], bounds[1:]):
                @pl.when((c >= int(lo)) & (c < int(hi)))
                def _():
                    base = pl.multiple_of((c - int(lo)) * SC_CHUNK, SC_CHUNK)
                    pltpu.sync_copy(src.at[pl.ds(base, SC_CHUNK)], rows_v)

            @pl.when(c < n_chunks)
            def _():
                base = pl.multiple_of(c * SC_CHUNK, SC_CHUNK)
                pltpu.sync_copy(pos_hbm.at[:, pl.ds(base, SC_CHUNK)], idx_v)
                copies = [pltpu.async_copy(rows_v, out_hbm.at[idx_v.at[j]], sem) for j in range(s)]
                for cp in copies:
                    cp.wait()

    return k(*sources, pos_t, *after)


def _sc_gather_rows(table, pos_t):
    _, w = table.shape
    s, t = pos_t.shape
    n_cores, n_workers = _sc_workers()
    n_chunks = t // SC_CHUNK
    iters = -(-n_chunks // n_workers)
    mesh = plsc.VectorSubcoreMesh(core_axis_name="c", subcore_axis_name="s")

    @functools.partial(
        pl.kernel, mesh=mesh, out_type=jax.ShapeDtypeStruct((s, t, w), table.dtype),
        scratch_types=[pltpu.VMEM((SC_CHUNK, w), table.dtype), pltpu.VMEM((s, SC_CHUNK), I32),
                       pltpu.SemaphoreType.DMA],
        name="combine")
    def k(table_hbm, pos_hbm, out_hbm, rows_v, idx_v, sem):
        wid = lax.axis_index("s") * n_cores + lax.axis_index("c")

        @pl.loop(0, iters)
        def _(it):
            c = it * n_workers + wid

            @pl.when(c < n_chunks)
            def _():
                base = pl.multiple_of(c * SC_CHUNK, SC_CHUNK)
                pltpu.sync_copy(pos_hbm.at[:, pl.ds(base, SC_CHUNK)], idx_v)
                for j in range(s):
                    pltpu.async_copy(table_hbm.at[idx_v.at[j]], rows_v, sem).wait()
                    pltpu.sync_copy(rows_v, out_hbm.at[j, pl.ds(base, SC_CHUNK)])

    return k(table, pos_t)


def _sc_pack_weights(w, rows_per_item, after=()):
    e, r, c = w.shape
    half = r // 2
    rb = rows_per_item
    per_expert = half // rb
    n_cores, n_workers = _sc_workers()
    per_worker = e * per_expert // n_workers
    assert per_worker * n_workers == e * per_expert and per_worker % 2 == 0 and c % (SC_LANES * SC_UNROLL) == 0
    mesh = plsc.VectorSubcoreMesh(core_axis_name="c", subcore_axis_name="s")

    @functools.partial(
        pl.kernel, mesh=mesh, out_type=jax.ShapeDtypeStruct((e * half, c), U32),
        scratch_types=[pltpu.VMEM((2, rb, c), F32), pltpu.VMEM((2, rb, c), F32), pltpu.VMEM((2, rb, c), U32),
                       pltpu.SemaphoreType.DMA((2,)), pltpu.SemaphoreType.DMA((2,))],
        compiler_params=pltpu.CompilerParams(needs_layout_passes=False),
        cost_estimate=pl.CostEstimate(flops=e * r * c, transcendentals=0, bytes_accessed=6 * e * r * c),
        name="pack_weights")
    def k(w_hbm, *refs):
        out_hbm, lo_v, hi_v, out_v, in_sem, out_sem = refs[len(after):]
        wid = lax.axis_index("s") * n_cores + lax.axis_index("c")
        first = wid * per_worker

        def rows(item):
            ex = item // per_expert
            j = item - ex * per_expert
            return (pl.multiple_of(ex * r + j * rb, rb), pl.multiple_of(ex * r + half + j * rb, rb),
                    pl.multiple_of(ex * half + j * rb, rb))

        def loads(item, b):
            lo_row, hi_row, _ = rows(item)
            return (pltpu.make_async_copy(w_hbm.at[pl.ds(lo_row, rb)], lo_v.at[b], in_sem.at[b]),
                    pltpu.make_async_copy(w_hbm.at[pl.ds(hi_row, rb)], hi_v.at[b], in_sem.at[b]))

        def store(item, b):
            return pltpu.make_async_copy(out_v.at[b], out_hbm.at[pl.ds(rows(item)[2], rb)], out_sem.at[b])

        for cp in loads(first, 0):
            cp.start()

        @pl.loop(0, per_worker // 2)
        def _(pair):
            for b in range(2):
                item = first + pair * 2 + b
                for cp in loads(item, b):
                    cp.wait()

                @pl.when(item + 1 < first + per_worker)
                def _():
                    for cp in loads(item + 1, 1 - b):
                        cp.start()

                @pl.when(pair > 0)
                def _():
                    store(item - 2, b).wait()

                @pl.loop(0, rb)
                def _(i):
                    @pl.loop(0, c // (SC_LANES * SC_UNROLL))
                    def _(vb):
                        for u in range(SC_UNROLL):
                            sl = pl.ds(pl.multiple_of((vb * SC_UNROLL + u) * SC_LANES, SC_LANES), SC_LANES)
                            packed = plsc.pack(lo_v[b, i, sl], hi_v[b, i, sl], format=plsc.PackFormat.INTERLEAVED)
                            out_v[b, i, sl] = plsc.bitcast(packed, U32)

                store(item, b).start()

        for b in range(2):
            store(first + per_worker - 2 + b, b).wait()

    return k(w.reshape(e * r, c), *after).reshape(e, half, c)


def _experts_kernel(first_ref, ntile_ref, cnt_ref, xs_hbm, wg_ref, wu_ref, wd_ref, ys_hbm,
                    wg16_ref, wu16_ref, wd16_ref, x_buf, y_buf, in_sem, out_sem):
    e = pl.program_id(0)
    n_used = first_ref[N_EXPERTS - 1] + ntile_ref[N_EXPERTS - 1]
    first, n_mine, count = first_ref[e], ntile_ref[e], cnt_ref[e]

    def tile_rows(g):
        return pl.ds(pl.multiple_of(g * ROW_TILE, ROW_TILE), ROW_TILE)

    def load(g):
        slot = lax.rem(g, STREAM_DEPTH)
        return pltpu.make_async_copy(xs_hbm.at[tile_rows(g)], x_buf.at[slot], in_sem.at[slot])

    def store(g):
        slot = lax.rem(g, STREAM_DEPTH)
        return pltpu.make_async_copy(y_buf.at[slot], ys_hbm.at[tile_rows(g)], out_sem.at[slot])

    @pl.when(e == 0)
    def _():
        for g0 in range(STREAM_DEPTH - 1):
            @pl.when(g0 < n_used)
            def _():
                load(g0).start()

    for packed_ref, w16_ref in ((wg_ref, wg16_ref), (wu_ref, wu16_ref), (wd_ref, wd16_ref)):
        rows = packed_ref.shape[1]
        lo, hi = _unpack_rows(packed_ref[0])
        w16_ref[0:rows, :] = lo.astype(BF16)
        w16_ref[rows:, :] = hi.astype(BF16)

    def tile(j, carry):
        g = first + j
        slot = lax.rem(g, STREAM_DEPTH)
        load(g).wait()

        @pl.when(g + (STREAM_DEPTH - 1) < n_used)
        def _():
            load(g + (STREAM_DEPTH - 1)).start()

        @pl.when(g >= STREAM_DEPTH)
        def _():
            store(g - STREAM_DEPTH).wait()

        words = x_buf[slot]
        row = lax.broadcasted_iota(I32, words.shape, 0)
        words = jnp.where(row < count - j * ROW_TILE, words, jnp.uint32(0))
        lo, hi = _unpack_rows(words)
        lo, hi = lo.astype(BF16), hi.astype(BF16)
        hg = _dot(lo, wg16_ref[0:HALF, :]) + _dot(hi, wg16_ref[HALF:, :])
        hu = _dot(lo, wu16_ref[0:HALF, :]) + _dot(hi, wu16_ref[HALF:, :])
        a = (_silu(hg) * hu).astype(BF16)
        y_buf[slot] = _pack_rows(_dot(a, wd16_ref[...]))
        store(g).start()
        return carry

    lax.fori_loop(0, n_mine, tile, 0)

    @pl.when(e == N_EXPERTS - 1)
    def _():
        for back in range(STREAM_DEPTH, 0, -1):
            @pl.when(n_used >= back)
            def _():
                store(n_used - back).wait()


def _experts(xs, first_tile, n_tile, count, w_eg, w_eu, w_ed):
    d = D_MODEL
    by_expert = lambda e, *_: (e, 0, 0)
    grid_spec = pltpu.PrefetchScalarGridSpec(
        num_scalar_prefetch=3,
        grid=(N_EXPERTS,),
        in_specs=[pl.BlockSpec(memory_space=pl.ANY),
                  pl.BlockSpec((1, d // 2, EXPERT_DIM), by_expert),
                  pl.BlockSpec((1, d // 2, EXPERT_DIM), by_expert),
                  pl.BlockSpec((1, EXPERT_DIM // 2, d), by_expert)],
        out_specs=pl.BlockSpec(memory_space=pl.ANY),
        scratch_shapes=[pltpu.VMEM((d, EXPERT_DIM), BF16), pltpu.VMEM((d, EXPERT_DIM), BF16),
                        pltpu.VMEM((EXPERT_DIM, d), BF16),
                        pltpu.VMEM((STREAM_DEPTH, ROW_TILE, HALF), U32),
                        pltpu.VMEM((STREAM_DEPTH, ROW_TILE, HALF), U32),
                        pltpu.SemaphoreType.DMA((STREAM_DEPTH,)), pltpu.SemaphoreType.DMA((STREAM_DEPTH,))])
    return pl.pallas_call(
        _experts_kernel,
        grid_spec=grid_spec,
        out_shape=jax.ShapeDtypeStruct(xs.shape, U32),
        compiler_params=pltpu.CompilerParams(dimension_semantics=("arbitrary",),
                                             vmem_limit_bytes=VMEM_LIMIT),
        name="experts",
    )(first_tile, n_tile, count, xs, w_eg, w_eu, w_ed)


def _final_kernel(z_ref, gatew_ref, h2_ref, x1_ref, mod_ref, normf_ref, wsg_ref, wsu_ref, wsd_ref, *rest):
    y_ref = rest[-1]
    lo, hi = _unpack_rows(h2_ref[...])
    h = jnp.concatenate([lo, hi], axis=-1).astype(BF16)
    a = _silu(_dot(h, wsg_ref[...])) * _dot(h, wsu_ref[...])
    acc = _dot(a.astype(BF16), wsd_ref[...])
    for s in range(TOP_K):
        lo, hi = _unpack_rows(z_ref[s])
        acc = acc + gatew_ref[:, s:s + 1] * jnp.concatenate([lo, hi], axis=-1)
    x2 = x1_ref[...] + _mod(mod_ref, 5) * acc
    y_ref[...] = _rms(x2) * normf_ref[...]


def _final(z, gate_w, h2, x1, mod, norm_f, w_sg16, w_su16, w_sd16, block_t, first_block, per_seq,
           seq0=0, out_rows=None, y_prev=None):
    t, d = x1.shape
    out_rows = t if out_rows is None else out_rows
    out_first = seq0 * per_seq
    tok = lambda i: (i, 0)
    if per_seq:
        mod_spec = pl.BlockSpec((1, 6, d), lambda i: (seq0 + i // per_seq, 0, 0))
    else:
        mod_spec = pl.BlockSpec((block_t, 6 * d), tok)
    operands = [z, gate_w, h2, x1, mod, norm_f, w_sg16, w_su16, w_sd16]
    in_specs = [pl.BlockSpec((TOP_K, block_t, HALF), lambda i: (0, first_block + i, 0)),
                pl.BlockSpec((block_t, LANES), tok),
                pl.BlockSpec((block_t, HALF), tok),
                pl.BlockSpec((block_t, d), tok),
                mod_spec,
                _full((1, d)),
                _full((d, EXPERT_DIM)), _full((d, EXPERT_DIM)), _full((EXPERT_DIM, d))]
    aliases = {}
    if y_prev is not None:
        aliases = {len(operands): 0}
        operands.append(y_prev)
        in_specs.append(pl.BlockSpec(memory_space=pl.ANY))
    return pl.pallas_call(
        _final_kernel,
        grid=(t // block_t,),
        in_specs=in_specs,
        out_specs=pl.BlockSpec((block_t, d), lambda i: (out_first + i, 0)),
        out_shape=jax.ShapeDtypeStruct((out_rows, d), F32),
        input_output_aliases=aliases,
        compiler_params=pltpu.CompilerParams(dimension_semantics=("arbitrary",),
                                             vmem_limit_bytes=VMEM_LIMIT),
        name="final",
    )(*operands)


def kernel(x_prompt, x_sample, c_prompt, c_sample, state_ret, state_pool, norm1, norm2, norm_f,
           w_ada, b_ada, w_in, w_out, w_pool, pool_scale, w_router, router_bias, w_exp_gate,
           w_exp_up, w_exp_down, w_sh_gate, w_sh_up, w_sh_down):
    b, l, d = x_prompt.shape
    n = x_sample.shape[0]
    past_len = 16384

    mod = _ada(jnp.concatenate([c_prompt, c_sample], axis=0), w_ada[0], b_ada[0])
    mod_p = mod[:b].reshape(b, 6, d)
    mod_s = mod[b:]

    w_in16 = w_in[0].astype(BF16)
    w_out16 = w_out[0].astype(BF16)
    wr_t = w_router[0].T
    bias_t = jnp.broadcast_to(router_bias[0][:, None], (N_EXPERTS, LANES))
    n1, n2, nf = norm1[0].reshape(1, d), norm2[0].reshape(1, d), norm_f.reshape(1, d)
    ps = pool_scale[0].reshape(1, POOL_WIDTH)
    shared = (w_sh_gate[0].astype(BF16), w_sh_up[0].astype(BF16), w_sh_down[0].astype(BF16))

    def routed(sources, experts):
        n_tiles = experts.shape[1] * TOP_K // ROW_TILE + N_EXPERTS
        pos_t, meta = _plan(experts)
        xs = _sc_scatter_rows(sources, pos_t, n_tiles * ROW_TILE, after=expert_w)
        ys = _experts(xs, meta[:, 0], meta[:, 1], meta[:, 2], *expert_w)
        return _sc_gather_rows(ys, pos_t)

    mix_args = (n1, w_in16, w_pool[0], ps, w_out16, n2, wr_t, bias_t)
    x1_s, h2_s, experts_s, gatew_s, ret_s, pool_s = _mix_sample(
        x_sample.reshape(n, d), mod_s, state_ret[0], state_pool[0], float(past_len), *mix_args)

    expert_w = (_sc_pack_weights(w_exp_gate[0], 64, after=(x1_s,)),
                _sc_pack_weights(w_exp_up[0], 64, after=(x1_s,)),
                _sc_pack_weights(w_exp_down[0], 16, after=(x1_s,)))

    ba = b // 2
    bb = b - ba
    x1_a, h2_a, experts_a, gatew_a, ret_a, pool_a = _mix_prompt(x_prompt, mod_p, 0, ba, *mix_args)
    z_a = routed((h2_a,), experts_a)
    x1_b, h2_b, experts_b, gatew_b, ret_b, pool_b = _mix_prompt(x_prompt, mod_p, ba, bb, *mix_args)
    z_b = routed((h2_b, h2_s), jnp.concatenate([experts_b, experts_s], axis=1))

    block_t = 256
    per_seq = l // block_t
    y_s = _final(z_b, gatew_s, h2_s, x1_s, mod_s, nf, *shared,
                 block_t=n, first_block=bb * l // n, per_seq=0)
    y_p = _final(z_b, gatew_b, h2_b, x1_b.reshape(bb * l, d), mod_p, nf, *shared,
                 block_t=block_t, first_block=0, per_seq=per_seq, seq0=ba, out_rows=b * l)
    y_p = _final(z_a, gatew_a, h2_a, x1_a.reshape(ba * l, d), mod_p, nf, *shared,
                 block_t=block_t, first_block=0, per_seq=per_seq, seq0=0, out_rows=b * l, y_prev=y_p)

    ret_p = jnp.concatenate([ret_a, ret_b], axis=0)
    pool_p = jnp.concatenate([pool_a, pool_b], axis=0)
    return (y_p.reshape(b, l, d), y_s.reshape(n, 1, d), ret_p[None], pool_p[None],
            ret_s[None], pool_s[None])
```

```python
import functools

import jax
import jax.numpy as jnp
import numpy as np
from jax import lax
from jax.experimental import pallas as pl
from jax.experimental.pallas import tpu as pltpu
from jax.experimental.pallas import tpu_sc as plsc

D_MODEL = 1024
RET_HEADS = 4
RET_QK_DIM = 64
RET_V_DIM = 128
RET_WIDTH = RET_HEADS * RET_V_DIM
QK_WIDTH = RET_HEADS * RET_QK_DIM
ROPE_BASE = 10000.0
POOL_WINDOWS = (2, 4, 8, 16)
POOL_WIDTH = 512
POOL_GROUP_DIM = 128
POOL_BUF = 15
IN_WIDTH = 2 * QK_WIDTH + 2 * RET_WIDTH + POOL_WIDTH
N_EXPERTS = 64
TOP_K = 8
N_EXPERT_GROUPS = 8
GROUP_SIZE = N_EXPERTS // N_EXPERT_GROUPS
TOP_GROUPS = 4
EXPERT_DIM = 256
ROUTE_SCALE = 2.5
EPS = 1e-6

LANES = 128
SUBLANES = 8
POOL_CARRY = 24
VMEM_LIMIT = 56 * 1024 * 1024
HALF = D_MODEL // 2
ROW_TILE = 512
SC_CHUNK = 128
SC_LANES = 16
SC_UNROLL = 16
STREAM_DEPTH = 8

BF16 = jnp.bfloat16
F32 = jnp.float32
U32 = jnp.uint32
I32 = jnp.int32


def _silu(x):
    return x * jax.nn.sigmoid(x)


def _dot(a, b):
    return jnp.dot(a, b, preferred_element_type=F32)


def _rms(x):
    return x * lax.rsqrt(jnp.mean(x * x, axis=-1, keepdims=True) + EPS)


def _mod(mod_ref, i, seq=0):
    if len(mod_ref.shape) == 3:
        return mod_ref[seq, i:i + 1, :]
    return mod_ref[:, i * D_MODEL:(i + 1) * D_MODEL]


def _split_bf16(x):
    hi = x.astype(BF16)
    lo = (x - hi.astype(F32)).astype(BF16)
    return hi, lo


def _pack_rows(x):
    lo = lax.bitcast_convert_type(x[:, :HALF].astype(BF16).astype(F32), U32)
    hi = lax.bitcast_convert_type(x[:, HALF:].astype(BF16).astype(F32), U32)
    return (hi & jnp.uint32(0xFFFF0000)) | (lo >> jnp.uint32(16))


def _unpack_rows(w):
    lo = lax.bitcast_convert_type(w << jnp.uint32(16), F32)
    hi = lax.bitcast_convert_type(w & jnp.uint32(0xFFFF0000), F32)
    return lo, hi


def _first_max_onehot(work, idx, n):
    m = jnp.max(work, axis=0, keepdims=True)
    first = jnp.min(jnp.where(work == m, idx, float(n)), axis=0, keepdims=True)
    return idx == first


def _route(h2, wr_t_ref, bias_t_ref):
    n = h2.shape[0]
    h_hi, h_lo = _split_bf16(h2)
    w_hi, w_lo = _split_bf16(wr_t_ref[...])
    nt = (((1,), (1,)), ((), ()))
    logits = (lax.dot_general(w_hi, h_hi, nt, preferred_element_type=F32)
              + lax.dot_general(w_hi, h_lo, nt, preferred_element_type=F32)
              + lax.dot_general(w_lo, h_hi, nt, preferred_element_type=F32))
    scores = jax.nn.sigmoid(logits)
    biased = scores + bias_t_ref[:, 0:1]
    b3 = biased.reshape(N_EXPERT_GROUPS, GROUP_SIZE, n)
    i3 = lax.broadcasted_iota(I32, b3.shape, 1).astype(F32)
    m1 = jnp.max(b3, axis=1, keepdims=True)
    first = jnp.min(jnp.where(b3 == m1, i3, float(GROUP_SIZE)), axis=1, keepdims=True)
    m2 = jnp.max(jnp.where(i3 == first, -jnp.inf, b3), axis=1, keepdims=True)
    gscore = (m1 + m2).reshape(N_EXPERT_GROUPS, n)
    gidx = lax.broadcasted_iota(I32, gscore.shape, 0).astype(F32)
    gsel = jnp.zeros(gscore.shape, F32)
    work = gscore
    for _ in range(TOP_GROUPS):
        hit = _first_max_onehot(work, gidx, N_EXPERT_GROUPS)
        gsel = jnp.where(hit, 1.0, gsel)
        work = jnp.where(hit, -jnp.inf, work)
    gsel3 = jnp.broadcast_to(gsel.reshape(N_EXPERT_GROUPS, 1, n), b3.shape)
    work = jnp.where(gsel3 > 0.0, b3, -jnp.inf).reshape(N_EXPERTS, n)
    eidx = lax.broadcasted_iota(I32, work.shape, 0).astype(F32)
    sel = jnp.zeros(work.shape, F32)
    for _ in range(TOP_K):
        hit = _first_max_onehot(work, eidx, N_EXPERTS)
        sel = jnp.where(hit, 1.0, sel)
        work = jnp.where(hit, -jnp.inf, work)
    picked = jnp.where(sel > 0.0, scores, 0.0)
    gates = picked / jnp.sum(picked, axis=0, keepdims=True) * ROUTE_SCALE
    below = (lax.broadcasted_iota(I32, (N_EXPERTS, N_EXPERTS), 1)
             < lax.broadcasted_iota(I32, (N_EXPERTS, N_EXPERTS), 0))
    slot = _dot(jnp.where(below, 1.0, 0.0).astype(BF16), sel.astype(BF16))
    e_rows, w_rows = [], []
    for s in range(TOP_K):
        here = jnp.where(slot == float(s), sel, 0.0)
        e_rows.append(jnp.sum(here * eidx, axis=0, keepdims=True))
        w_rows.append(jnp.sum(here * gates, axis=0, keepdims=True))
    experts = jnp.concatenate(e_rows, axis=0).astype(I32)
    w_t = jnp.concatenate(w_rows + [jnp.zeros((LANES - TOP_K, n), F32)], axis=0)
    return experts, w_t.T


def _group_norm_gate(o, g):
    parts = []
    for h in range(RET_HEADS):
        oh = o[:, h * RET_V_DIM:(h + 1) * RET_V_DIM]
        mu = jnp.mean(oh, axis=-1, keepdims=True)
        ctr = oh - mu
        var = jnp.mean(ctr * ctr, axis=-1, keepdims=True)
        parts.append(ctr * lax.rsqrt(var + EPS))
    return _silu(g) * jnp.concatenate(parts, axis=-1)


def _pool_project(pooled, w_pool_ref, pool_scale_ref):
    parts = [_dot(p.astype(BF16), w_pool_ref[gi].astype(BF16)) for gi, p in enumerate(pooled)]
    return jnp.concatenate(parts, axis=-1) * pool_scale_ref[...]


def _out_residual(x, o_gated, p, mod_ref, seq, w_out_ref):
    mix = jnp.concatenate([o_gated, p], axis=-1).astype(BF16)
    return x + _mod(mod_ref, 2, seq) * _dot(mix, w_out_ref[...])


def _norm_route(x1, mod_ref, seq, norm2_ref, wr_t_ref, bias_t_ref, h2_ref, experts_ref, gatew_ref):
    h2 = _rms(x1) * norm2_ref[...] * (1.0 + _mod(mod_ref, 4, seq)) + _mod(mod_ref, 3, seq)
    h2_ref[...] = _pack_rows(h2)
    experts, gate_w = _route(h2, wr_t_ref, bias_t_ref)
    experts_ref[...] = experts
    gatew_ref[...] = gate_w


def _ada_kernel(cp_ref, cs_ref, w_ref, b_ref, op_ref, os_ref):
    w16 = w_ref[...].astype(BF16)
    for c_ref, o_ref in ((cp_ref, op_ref), (cs_ref, os_ref)):
        o_ref[...] = _dot(_silu(c_ref[...]).astype(BF16), w16) + b_ref[...]


def _ada(c_prompt, c_sample, w_ada, b_ada, block_n=1536):
    d, width = w_ada.shape
    rows = lambda c: pl.BlockSpec((c.shape[0], d), lambda j: (0, 0))
    cols = lambda c: pl.BlockSpec((c.shape[0], block_n), lambda j: (0, j))
    return pl.pallas_call(
        _ada_kernel,
        grid=(width // block_n,),
        in_specs=[rows(c_prompt), rows(c_sample),
                  pl.BlockSpec((d, block_n), lambda j: (0, j)),
                  pl.BlockSpec((1, block_n), lambda j: (0, j))],
        out_specs=[cols(c_prompt), cols(c_sample)],
        out_shape=[jax.ShapeDtypeStruct((c.shape[0], width), F32) for c in (c_prompt, c_sample)],
        compiler_params=pltpu.CompilerParams(vmem_limit_bytes=VMEM_LIMIT),
        name="ada",
    )(c_prompt, c_sample, w_ada, b_ada.reshape(1, width))


def _mix_prompt_kernel(x_ref, mod_ref, norm1_ref, w_in_ref, cos_ref, sin_ref, dmat_ref, cross_ref,
                       tail_ref, cdec_ref, w_pool_ref, pool_scale_ref, w_out_ref, norm2_ref,
                       wr_t_ref, bias_t_ref,
                       x1_ref, h2_ref, experts_ref, gatew_ref, ret_ref, pool_ref,
                       state_ref, ext_ref, win_ref, o_ref, *, block_l, chunk, seqs):
    li = pl.program_id(1)

    @pl.when(li == 0)
    def _():
        state_ref[...] = jnp.zeros_like(state_ref)
        ext_ref[:, 0:POOL_CARRY, :] = jnp.zeros((seqs, POOL_CARRY, POOL_WIDTH), F32)
        win_ref[:, 0:SUBLANES, :] = jnp.zeros((seqs, SUBLANES, POOL_WIDTH), F32)

    for seq in range(seqs):
        _mix_prompt_seq(seq, li, x_ref, mod_ref, norm1_ref, w_in_ref, cos_ref, sin_ref, dmat_ref, cross_ref,
                        tail_ref, cdec_ref, w_pool_ref, pool_scale_ref, w_out_ref, norm2_ref,
                        wr_t_ref, bias_t_ref, x1_ref, h2_ref, experts_ref, gatew_ref,
                        state_ref.at[seq], ext_ref.at[seq], win_ref.at[seq], o_ref.at[seq],
                        block_l=block_l, chunk=chunk)

    @pl.when(li == pl.num_programs(1) - 1)
    def _():
        for seq in range(seqs):
            ret_ref[seq] = _head_order(state_ref[seq]).reshape(ret_ref.shape[1:])
        pool_ref[...] = ext_ref[:, POOL_CARRY - POOL_BUF:POOL_CARRY, :]


def _window_sums(ext_ref, win_ref, block_l):
    g = POOL_GROUP_DIM
    top = POOL_CARRY + block_l
    new = slice(POOL_CARRY - SUBLANES, None)
    s2 = ext_ref[SUBLANES:top, :] + ext_ref[SUBLANES - 1:top - 1, :]
    win_ref[SUBLANES:top, g:] = s2[:, g:]
    s4 = s2[:, g:] + win_ref[SUBLANES - 2:top - 2, g:]
    win_ref[SUBLANES:top, 2 * g:] = s4[:, g:]
    s8 = s4[:, g:] + win_ref[SUBLANES - 4:top - 4, 2 * g:]
    win_ref[SUBLANES:top, 3 * g:] = s8[:, g:]
    s16 = s8[:, g:] + win_ref[0:top - SUBLANES, 3 * g:]
    return [s2[new, 0:g], s4[new, 0:g], s8[new, 0:g], s16[new, :]]


def _mix_prompt_seq(seq, li, x_ref, mod_ref, norm1_ref, w_in_ref, cos_ref, sin_ref, dmat_ref, cross_ref,
                    tail_ref, cdec_ref, w_pool_ref, pool_scale_ref, w_out_ref, norm2_ref,
                    wr_t_ref, bias_t_ref, x1_ref, h2_ref, experts_ref, gatew_ref,
                    state_ref, ext_ref, win_ref, o_ref, *, block_l, chunk):
    x = x_ref[seq]
    h16 = (_rms(x) * norm1_ref[...] * (1.0 + _mod(mod_ref, 1, seq)) + _mod(mod_ref, 0, seq)).astype(BF16)
    proj = _dot(h16, w_in_ref[...])
    q = proj[:, 0:QK_WIDTH]
    k = proj[:, QK_WIDTH:2 * QK_WIDTH]
    v = proj[:, 2 * QK_WIDTH:2 * QK_WIDTH + RET_WIDTH]
    g = proj[:, 2 * QK_WIDTH + RET_WIDTH:2 * QK_WIDTH + 2 * RET_WIDTH]
    u = proj[:, 2 * QK_WIDTH + 2 * RET_WIDTH:]

    cos_t = cos_ref[...]
    sin_t = sin_ref[...]

    def rot(t):
        t1, t2 = t[:, :QK_WIDTH // 2], t[:, QK_WIDTH // 2:]
        return jnp.concatenate([t1 * cos_t - t2 * sin_t, t1 * sin_t + t2 * cos_t], axis=-1)

    q = rot(q)
    k = rot(k) * (RET_QK_DIM ** -0.5)
    k_t = k.T
    v16 = v.astype(BF16)
    half = RET_QK_DIM // 2
    head_of_lane = (lax.broadcasted_iota(I32, (chunk, QK_WIDTH), 1) % (QK_WIDTH // 2)) // half

    for c in range(block_l // chunk):
        rows = slice(c * chunk, (c + 1) * chunk)
        q_c = q[rows]
        kt_c = k_t[:, rows]
        kt16 = kt_c.astype(BF16)
        state16 = state_ref[...].astype(BF16)
        for hd in range(RET_HEADS):
            in_head = head_of_lane == hd
            q_h = jnp.where(in_head, q_c, 0.0).astype(BF16)
            v_h = v16[rows, hd * RET_V_DIM:(hd + 1) * RET_V_DIM]
            scores = _dot(q_h, kt16) * dmat_ref[hd]
            inner = _dot(scores.astype(BF16), v_h)
            cross = _dot(q_h, state16) * cross_ref[hd]
            o_ref[rows, hd * RET_V_DIM:(hd + 1) * RET_V_DIM] = inner + cross
            r1, r2 = _head_rows(hd)
            k_dec = jnp.concatenate([kt_c[r1], kt_c[r2]], axis=0) * tail_ref[hd:hd + 1, :]
            upd = _dot(k_dec.astype(BF16), v_h)
            state_ref[r1, :] = state_ref[r1, :] * cdec_ref[hd, 0:half] + upd[0:half]
            state_ref[r2, :] = state_ref[r2, :] * cdec_ref[hd, half:] + upd[half:]

    o_gated = _group_norm_gate(o_ref[...], g)

    ext_ref[POOL_CARRY:POOL_CARRY + block_l, :] = u
    pos = (li * block_l + lax.broadcasted_iota(I32, (block_l, 1), 0)).astype(F32)
    pooled = []
    for gi, (w, acc) in enumerate(zip(POOL_WINDOWS, _window_sums(ext_ref, win_ref, block_l))):
        cnt = jnp.minimum(pos + 1.0, float(w))
        pooled.append(acc / cnt - u[:, gi * POOL_GROUP_DIM:(gi + 1) * POOL_GROUP_DIM])
    p = _pool_project(pooled, w_pool_ref, pool_scale_ref)
    ext_ref[0:POOL_CARRY, :] = ext_ref[block_l:block_l + POOL_CARRY, :]

    x1 = _out_residual(x, o_gated, p, mod_ref, seq, w_out_ref)
    x1_ref[seq] = x1
    _norm_route(x1, mod_ref, seq, norm2_ref, wr_t_ref, bias_t_ref,
                h2_ref.at[seq], experts_ref.at[seq], gatew_ref.at[seq])


def _decay_tables(chunk):
    f32 = np.float32
    lg = np.log(f32(1.0) - f32(2.0) ** (f32(-5.0) - np.arange(RET_HEADS, dtype=f32))).astype(f32)
    idx = np.arange(chunk, dtype=f32)
    diff = idx[:, None] - idx[None, :]
    causal = diff >= 0
    dmat = np.where(causal[None], np.exp(lg[:, None, None] * np.where(causal, diff, f32(0.0))[None]), f32(0.0))
    cross = np.exp(lg[:, None] * (idx[None, :] + f32(1.0)))
    cross = np.broadcast_to(cross[:, :, None], (RET_HEADS, chunk, RET_V_DIM))
    tail = np.exp(lg[:, None] * (f32(chunk - 1.0) - idx)[None, :])
    cdec = np.broadcast_to(np.exp(lg * f32(chunk))[:, None, None], (RET_HEADS, RET_QK_DIM, RET_V_DIM))
    return tuple(jnp.asarray(np.ascontiguousarray(t, dtype=f32)) for t in (dmat, cross, tail, cdec))


def _rotary_angles(pos):
    half = RET_QK_DIM // 2
    freqs = (np.float32(ROPE_BASE) ** (-np.arange(half, dtype=np.float32) / np.float32(half))).astype(np.float32)
    return np.asarray(pos, np.float32)[:, None] * freqs[None, :]


def _rotary_tables(pos):
    ang = _rotary_angles(pos)
    return (jnp.asarray(np.tile(np.cos(ang), (1, RET_HEADS)), F32),
            jnp.asarray(np.tile(np.sin(ang), (1, RET_HEADS)), F32))


def _split_halves_columns():
    half = RET_QK_DIM // 2
    one = [hd * RET_QK_DIM + part * half + j for part in range(2) for hd in range(RET_HEADS) for j in range(half)]
    return np.asarray(one + [QK_WIDTH + c for c in one] + list(range(2 * QK_WIDTH, IN_WIDTH)), np.int32)


def _head_rows(hd):
    half = RET_QK_DIM // 2
    return (slice(hd * half, (hd + 1) * half),
            slice(QK_WIDTH // 2 + hd * half, QK_WIDTH // 2 + (hd + 1) * half))


def _head_order(t):
    return jnp.concatenate([t[r] for hd in range(RET_HEADS) for r in _head_rows(hd)], axis=0)


def _full(shape):
    return pl.BlockSpec(shape, lambda *_: (0,) * len(shape))


def _mix_prompt(x, mod, b0, b, norm1, w_in16, w_pool, pool_scale, w_out16, norm2, wr_t, bias_t,
                block_l=512, chunk=256, seqs=2):
    _, l, d = x.shape
    nl = l // block_l
    s0 = b0 // seqs
    cos_t, sin_t = _rotary_tables(np.arange(l))
    dmat, cross, tail, cdec = _decay_tables(chunk)
    kernel = functools.partial(_mix_prompt_kernel, block_l=block_l, chunk=chunk, seqs=seqs)
    tok = lambda bi, li: (bi, li, 0)
    per_seq = lambda bi, li: (bi, 0, 0)
    x1, h2, experts, gate_w, ret, pool = pl.pallas_call(
        kernel,
        grid=(b // seqs, nl),
        in_specs=[pl.BlockSpec((seqs, block_l, d), lambda bi, li: (s0 + bi, li, 0)),
                  pl.BlockSpec((seqs, 6, d), lambda bi, li: (s0 + bi, 0, 0)),
                  _full((1, d)),
                  _full((d, IN_WIDTH)),
                  pl.BlockSpec((block_l, QK_WIDTH // 2), lambda bi, li: (li, 0)),
                  pl.BlockSpec((block_l, QK_WIDTH // 2), lambda bi, li: (li, 0)),
                  _full(dmat.shape), _full(cross.shape), _full(tail.shape), _full(cdec.shape),
                  _full(w_pool.shape), _full((1, POOL_WIDTH)), _full((d, d)), _full((1, d)),
                  _full(wr_t.shape), _full(bias_t.shape)],
        out_specs=[pl.BlockSpec((seqs, block_l, d), tok),
                   pl.BlockSpec((seqs, block_l, HALF), tok),
                   pl.BlockSpec((seqs, TOP_K, block_l), lambda bi, li: (bi, 0, li)),
                   pl.BlockSpec((seqs, block_l, LANES), tok),
                   pl.BlockSpec((seqs, RET_HEADS, RET_QK_DIM, RET_V_DIM), lambda bi, li: (bi, 0, 0, 0)),
                   pl.BlockSpec((seqs, POOL_BUF, POOL_WIDTH), per_seq)],
        out_shape=[jax.ShapeDtypeStruct((b, l, d), F32),
                   jax.ShapeDtypeStruct((b, l, HALF), U32),
                   jax.ShapeDtypeStruct((b, TOP_K, l), I32),
                   jax.ShapeDtypeStruct((b, l, LANES), F32),
                   jax.ShapeDtypeStruct((b, RET_HEADS, RET_QK_DIM, RET_V_DIM), F32),
                   jax.ShapeDtypeStruct((b, POOL_BUF, POOL_WIDTH), F32)],
        scratch_shapes=[pltpu.VMEM((seqs, QK_WIDTH, RET_V_DIM), F32),
                        pltpu.VMEM((seqs, POOL_CARRY + block_l, POOL_WIDTH), F32),
                        pltpu.VMEM((seqs, POOL_CARRY + block_l, POOL_WIDTH), F32),
                        pltpu.VMEM((seqs, block_l, RET_WIDTH), F32)],
        compiler_params=pltpu.CompilerParams(dimension_semantics=("arbitrary", "arbitrary"),
                                             vmem_limit_bytes=VMEM_LIMIT),
        name="mix_prompt",
    )(x, mod, norm1, w_in16, cos_t, sin_t, dmat, cross, tail, cdec, w_pool, pool_scale,
      w_out16, norm2, wr_t, bias_t)
    experts = jnp.transpose(experts, (1, 0, 2)).reshape(TOP_K, b * l)
    return x1, h2.reshape(b * l, HALF), experts, gate_w.reshape(b * l, LANES), ret, pool


def _mix_sample_front_kernel(x_ref, mod_ref, norm1_ref, w_in_ref, cos_ref, sin_ref,
                             qt_ref, kt_ref, v_ref, g_ref, u_ref):
    x = x_ref[...]
    h = _rms(x) * norm1_ref[...] * (1.0 + _mod(mod_ref, 1)) + _mod(mod_ref, 0)
    proj = _dot(h.astype(BF16), w_in_ref[...])
    cos_c = cos_ref[...]
    sin_c = sin_ref[...]

    def rot_t(t):
        t1, t2 = t[:QK_WIDTH // 2], t[QK_WIDTH // 2:]
        return _head_order(jnp.concatenate([t1 * cos_c - t2 * sin_c, t1 * sin_c + t2 * cos_c], axis=0))

    qt_ref[...] = rot_t(proj[:, 0:QK_WIDTH].T)
    kt_ref[...] = rot_t(proj[:, QK_WIDTH:2 * QK_WIDTH].T) * (RET_QK_DIM ** -0.5)
    v_ref[...] = proj[:, 2 * QK_WIDTH:2 * QK_WIDTH + RET_WIDTH]
    g_ref[...] = proj[:, 2 * QK_WIDTH + RET_WIDTH:2 * QK_WIDTH + 2 * RET_WIDTH]
    u_ref[...] = proj[:, 2 * QK_WIDTH + 2 * RET_WIDTH:]


def _ret_step_kernel(qt_ref, kt_ref, v_ref, s0_ref, o_ref, s1_ref, *, block_b, decays):
    i = pl.program_id(0)
    lane = lax.broadcasted_iota(I32, qt_ref.shape, 1)
    for j in range(block_b):
        bi = i * block_b + j
        here = lane == bi
        q_col = jnp.sum(jnp.where(here, qt_ref[...], 0.0), axis=1, keepdims=True)
        k_col = jnp.sum(jnp.where(here, kt_ref[...], 0.0), axis=1, keepdims=True)
        v_row = v_ref[pl.ds(bi, 1), :]
        outs = []
        for hd in range(RET_HEADS):
            hrows = slice(hd * RET_QK_DIM, (hd + 1) * RET_QK_DIM)
            s1 = decays[hd] * s0_ref[j, hd] + k_col[hrows] * v_row[:, hd * RET_V_DIM:(hd + 1) * RET_V_DIM]
            s1_ref[j, hd] = s1
            outs.append(jnp.sum(q_col[hrows] * s1, axis=0, keepdims=True))
        o_ref[pl.ds(bi, 1), :] = jnp.concatenate(outs, axis=-1)


def _mix_sample_back_kernel(x_ref, mod_ref, o_ref, g_ref, u_ref, buf_ref, w_pool_ref, pool_scale_ref,
                            w_out_ref, norm2_ref, wr_t_ref, bias_t_ref,
                            x1_ref, h2_ref, experts_ref, gatew_ref, pool_ref):
    o_gated = _group_norm_gate(o_ref[...], g_ref[...])
    u = u_ref[...]
    pooled = []
    for gi, w in enumerate(POOL_WINDOWS):
        lanes = slice(gi * POOL_GROUP_DIM, (gi + 1) * POOL_GROUP_DIM)
        acc = u[:, lanes]
        for j in range(1, w):
            acc = acc + buf_ref[:, POOL_BUF - j, lanes]
        pooled.append(acc / float(w) - u[:, lanes])
    p = _pool_project(pooled, w_pool_ref, pool_scale_ref)
    pool_ref[:, 0:POOL_BUF - 1, :] = buf_ref[:, 1:POOL_BUF, :]
    pool_ref[:, POOL_BUF - 1, :] = u
    x1 = _out_residual(x_ref[...], o_gated, p, mod_ref, 0, w_out_ref)
    x1_ref[...] = x1
    _norm_route(x1, mod_ref, 0, norm2_ref, wr_t_ref, bias_t_ref, h2_ref, experts_ref, gatew_ref)


def _mix_sample(x, mod, state_ret, state_pool, start, norm1, w_in16, w_pool,
                pool_scale, w_out16, norm2, wr_t, bias_t, block_b=32):
    n, d = x.shape
    ang = np.tile(_rotary_angles([start]), (1, RET_HEADS))
    cos_c = jnp.asarray(np.broadcast_to(np.cos(ang).T, (QK_WIDTH // 2, n)), F32)
    sin_c = jnp.asarray(np.broadcast_to(np.sin(ang).T, (QK_WIDTH // 2, n)), F32)
    params = pltpu.CompilerParams(vmem_limit_bytes=VMEM_LIMIT)
    qt, kt, v, g, u = pl.pallas_call(
        _mix_sample_front_kernel,
        out_shape=[jax.ShapeDtypeStruct((QK_WIDTH, n), F32), jax.ShapeDtypeStruct((QK_WIDTH, n), F32),
                   jax.ShapeDtypeStruct((n, RET_WIDTH), F32), jax.ShapeDtypeStruct((n, RET_WIDTH), F32),
                   jax.ShapeDtypeStruct((n, POOL_WIDTH), F32)],
        compiler_params=params,
        name="mix_sample_front",
    )(x, mod, norm1, w_in16, cos_c, sin_c)

    lg = np.log(1.0 - 2.0 ** (-5.0 - np.arange(RET_HEADS, dtype=np.float32)), dtype=np.float32)
    decays = tuple(float(np.exp(lg[h])) for h in range(RET_HEADS))
    state_block = (block_b, RET_HEADS, RET_QK_DIM, RET_V_DIM)
    o, s1 = pl.pallas_call(
        functools.partial(_ret_step_kernel, block_b=block_b, decays=decays),
        grid=(n // block_b,),
        in_specs=[_full((QK_WIDTH, n)), _full((QK_WIDTH, n)), _full((n, RET_WIDTH)),
                  pl.BlockSpec(state_block, lambda i: (i, 0, 0, 0))],
        out_specs=[_full((n, RET_WIDTH)), pl.BlockSpec(state_block, lambda i: (i, 0, 0, 0))],
        out_shape=[jax.ShapeDtypeStruct((n, RET_WIDTH), F32),
                   jax.ShapeDtypeStruct(state_ret.shape, F32)],
        compiler_params=pltpu.CompilerParams(dimension_semantics=("arbitrary",),
                                             vmem_limit_bytes=VMEM_LIMIT),
        name="ret_step",
    )(qt, kt, v, state_ret)

    x1, h2, experts, gate_w, pool = pl.pallas_call(
        _mix_sample_back_kernel,
        out_shape=[jax.ShapeDtypeStruct((n, d), F32),
                   jax.ShapeDtypeStruct((n, HALF), U32),
                   jax.ShapeDtypeStruct((TOP_K, n), I32),
                   jax.ShapeDtypeStruct((n, LANES), F32),
                   jax.ShapeDtypeStruct(state_pool.shape, F32)],
        compiler_params=params,
        name="mix_sample_back",
    )(x, mod, o, g, u, state_pool, w_pool, pool_scale, w_out16, norm2, wr_t, bias_t)
    return x1, h2, experts, gate_w, s1, pool


def _plan_kernel(experts_ref, pos_ref, meta_ref, cnt_ref, carry_ref, off_ref, *, block_t):
    phase = pl.program_id(0)
    j = pl.program_id(1)
    e_blk = experts_ref[...]
    eidx = lax.broadcasted_iota(I32, (N_EXPERTS, block_t), 0)
    member = jnp.zeros((N_EXPERTS, block_t), F32)
    for s in range(TOP_K):
        member = member + jnp.where(eidx == e_blk[s:s + 1, :], 1.0, 0.0)
    per_expert = jnp.broadcast_to(jnp.sum(member, axis=1, keepdims=True), (N_EXPERTS, LANES))

    @pl.when((phase == 0) & (j == 0))
    def _():
        cnt_ref[...] = jnp.zeros_like(cnt_ref)

    @pl.when(phase == 0)
    def _():
        cnt_ref[...] += per_expert

    @pl.when((phase == 0) & (j == pl.num_programs(1) - 1))
    def _():
        cnt = cnt_ref[...]
        n_tile = jnp.floor((cnt + (ROW_TILE - 1.0)) * (1.0 / ROW_TILE))
        upto = (lax.broadcasted_iota(I32, (N_EXPERTS, N_EXPERTS), 1)
                <= lax.broadcasted_iota(I32, (N_EXPERTS, N_EXPERTS), 0))
        tile_end = _dot(jnp.where(upto, 1.0, 0.0).astype(BF16), n_tile.astype(BF16))
        tile_start = tile_end - n_tile
        off_ref[...] = tile_start * ROW_TILE
        carry_ref[...] = jnp.zeros_like(carry_ref)
        lane = lax.broadcasted_iota(I32, cnt.shape, 1)
        meta_ref[...] = jnp.where(lane == 0, tile_start, jnp.where(lane == 1, n_tile, cnt)).astype(I32)

    @pl.when(phase == 1)
    def _():
        before = (lax.broadcasted_iota(I32, (block_t, block_t), 0)
                  < lax.broadcasted_iota(I32, (block_t, block_t), 1))
        rank = _dot(member.astype(BF16), jnp.where(before, 1.0, 0.0).astype(BF16))
        row = off_ref[:, 0:1] + carry_ref[:, 0:1] + rank
        carry_ref[...] += per_expert
        out = [jnp.sum(jnp.where(eidx == e_blk[s:s + 1, :], row, 0.0), axis=0, keepdims=True)
               for s in range(TOP_K)]
        pos_ref[...] = jnp.concatenate(out, axis=0).astype(I32)


def _plan(experts_all, max_block=1024):
    n_tokens = experts_all.shape[1]
    block_t = max(k for k in range(LANES, max_block + 1, LANES) if n_tokens % k == 0)
    nb = n_tokens // block_t
    return pl.pallas_call(
        functools.partial(_plan_kernel, block_t=block_t),
        grid=(2, nb),
        in_specs=[pl.BlockSpec((TOP_K, block_t), lambda ph, j: (0, j))],
        out_specs=[pl.BlockSpec((TOP_K, block_t), lambda ph, j: (0, j * ph)),
                   _full((N_EXPERTS, LANES))],
        out_shape=[jax.ShapeDtypeStruct((TOP_K, n_tokens), I32),
                   jax.ShapeDtypeStruct((N_EXPERTS, LANES), I32)],
        scratch_shapes=[pltpu.VMEM((N_EXPERTS, LANES), F32)] * 3,
        compiler_params=pltpu.CompilerParams(dimension_semantics=("arbitrary", "arbitrary"),
                                             vmem_limit_bytes=VMEM_LIMIT),
        name="plan",
    )(experts_all)


def _sc_workers():
    info = plsc.get_sparse_core_info()
    return info.num_cores, info.num_cores * info.num_subcores


def _sc_scatter_rows(sources, pos_t, n_out, after=()):
    w = sources[0].shape[1]
    s = pos_t.shape[0]
    n_cores, n_workers = _sc_workers()
    bounds = np.cumsum([0] + [src.shape[0] // SC_CHUNK for src in sources])
    n_chunks = int(bounds[-1])
    iters = -(-n_chunks // n_workers)
    mesh = plsc.VectorSubcoreMesh(core_axis_name="c", subcore_axis_name="s")

    @functools.partial(
        pl.kernel, mesh=mesh, out_type=jax.ShapeDtypeStruct((n_out, w), sources[0].dtype),
        scratch_types=[pltpu.VMEM((SC_CHUNK, w), sources[0].dtype), pltpu.VMEM((s, SC_CHUNK), I32),
                       pltpu.SemaphoreType.DMA],
        name="dispatch")
    def k(*refs):
        src_hbm, pos_hbm = refs[:len(sources)], refs[len(sources)]
        out_hbm, rows_v, idx_v, sem = refs[len(sources) + 1 + len(after):]
        wid = lax.axis_index("s") * n_cores + lax.axis_index("c")

        @pl.loop(0, iters)
        def _(it):
            c = it * n_workers + wid
            for src, lo, hi in zip(src_hbm, bounds[:-1], bounds[1:]):
                @pl.when((c >= int(lo)) & (c < int(hi)))
                def _():
                    base = pl.multiple_of((c - int(lo)) * SC_CHUNK, SC_CHUNK)
                    pltpu.sync_copy(src.at[pl.ds(base, SC_CHUNK)], rows_v)

            @pl.when(c < n_chunks)
            def _():
                base = pl.multiple_of(c * SC_CHUNK, SC_CHUNK)
                pltpu.sync_copy(pos_hbm.at[:, pl.ds(base, SC_CHUNK)], idx_v)
                copies = [pltpu.async_copy(rows_v, out_hbm.at[idx_v.at[j]], sem) for j in range(s)]
                for cp in copies:
                    cp.wait()

    return k(*sources, pos_t, *after)


def _sc_gather_rows(table, pos_t):
    _, w = table.shape
    s, t = pos_t.shape
    n_cores, n_workers = _sc_workers()
    n_chunks = t // SC_CHUNK
    iters = -(-n_chunks // n_workers)
    mesh = plsc.VectorSubcoreMesh(core_axis_name="c", subcore_axis_name="s")

    @functools.partial(
        pl.kernel, mesh=mesh, out_type=jax.ShapeDtypeStruct((s, t, w), table.dtype),
        scratch_types=[pltpu.VMEM((SC_CHUNK, w), table.dtype), pltpu.VMEM((s, SC_CHUNK), I32),
                       pltpu.SemaphoreType.DMA],
        name="combine")
    def k(table_hbm, pos_hbm, out_hbm, rows_v, idx_v, sem):
        wid = lax.axis_index("s") * n_cores + lax.axis_index("c")

        @pl.loop(0, iters)
        def _(it):
            c = it * n_workers + wid

            @pl.when(c < n_chunks)
            def _():
                base = pl.multiple_of(c * SC_CHUNK, SC_CHUNK)
                pltpu.sync_copy(pos_hbm.at[:, pl.ds(base, SC_CHUNK)], idx_v)
                for j in range(s):
                    pltpu.async_copy(table_hbm.at[idx_v.at[j]], rows_v, sem).wait()
                    pltpu.sync_copy(rows_v, out_hbm.at[j, pl.ds(base, SC_CHUNK)])

    return k(table, pos_t)


def _sc_pack_weights(w, rows_per_item, after=()):
    e, r, c = w.shape
    half = r // 2
    rb = rows_per_item
    per_expert = half // rb
    n_cores, n_workers = _sc_workers()
    per_worker = e * per_expert // n_workers
    assert per_worker * n_workers == e * per_expert and per_worker % 2 == 0 and c % (SC_LANES * SC_UNROLL) == 0
    mesh = plsc.VectorSubcoreMesh(core_axis_name="c", subcore_axis_name="s")

    @functools.partial(
        pl.kernel, mesh=mesh, out_type=jax.ShapeDtypeStruct((e * half, c), U32),
        scratch_types=[pltpu.VMEM((2, rb, c), F32), pltpu.VMEM((2, rb, c), F32), pltpu.VMEM((2, rb, c), U32),
                       pltpu.SemaphoreType.DMA((2,)), pltpu.SemaphoreType.DMA((2,))],
        compiler_params=pltpu.CompilerParams(needs_layout_passes=False),
        cost_estimate=pl.CostEstimate(flops=e * r * c, transcendentals=0, bytes_accessed=6 * e * r * c),
        name="pack_weights")
    def k(w_hbm, *refs):
        out_hbm, lo_v, hi_v, out_v, in_sem, out_sem = refs[len(after):]
        wid = lax.axis_index("s") * n_cores + lax.axis_index("c")
        first = wid * per_worker

        def rows(item):
            ex = item // per_expert
            j = item - ex * per_expert
            return (pl.multiple_of(ex * r + j * rb, rb), pl.multiple_of(ex * r + half + j * rb, rb),
                    pl.multiple_of(ex * half + j * rb, rb))

        def loads(item, b):
            lo_row, hi_row, _ = rows(item)
            return (pltpu.make_async_copy(w_hbm.at[pl.ds(lo_row, rb)], lo_v.at[b], in_sem.at[b]),
                    pltpu.make_async_copy(w_hbm.at[pl.ds(hi_row, rb)], hi_v.at[b], in_sem.at[b]))

        def store(item, b):
            return pltpu.make_async_copy(out_v.at[b], out_hbm.at[pl.ds(rows(item)[2], rb)], out_sem.at[b])

        for cp in loads(first, 0):
            cp.start()

        @pl.loop(0, per_worker // 2)
        def _(pair):
            for b in range(2):
                item = first + pair * 2 + b
                for cp in loads(item, b):
                    cp.wait()

                @pl.when(item + 1 < first + per_worker)
                def _():
                    for cp in loads(item + 1, 1 - b):
                        cp.start()

                @pl.when(pair > 0)
                def _():
                    store(item - 2, b).wait()

                @pl.loop(0, rb)
                def _(i):
                    @pl.loop(0, c // (SC_LANES * SC_UNROLL))
                    def _(vb):
                        for u in range(SC_UNROLL):
                            sl = pl.ds(pl.multiple_of((vb * SC_UNROLL + u) * SC_LANES, SC_LANES), SC_LANES)
                            packed = plsc.pack(lo_v[b, i, sl], hi_v[b, i, sl], format=plsc.PackFormat.INTERLEAVED)
                            out_v[b, i, sl] = plsc.bitcast(packed, U32)

                store(item, b).start()

        for b in range(2):
            store(first + per_worker - 2 + b, b).wait()

    return k(w.reshape(e * r, c), *after).reshape(e, half, c)


def _experts_kernel(first_ref, ntile_ref, cnt_ref, xs_hbm, wg_ref, wu_ref, wd_ref, ys_hbm,
                    wg16_ref, wu16_ref, wd16_ref, x_buf, y_buf, in_sem, out_sem):
    e = pl.program_id(0)
    n_used = first_ref[N_EXPERTS - 1] + ntile_ref[N_EXPERTS - 1]
    first, n_mine, count = first_ref[e], ntile_ref[e], cnt_ref[e]

    def tile_rows(g):
        return pl.ds(pl.multiple_of(g * ROW_TILE, ROW_TILE), ROW_TILE)

    def load(g):
        slot = lax.rem(g, STREAM_DEPTH)
        return pltpu.make_async_copy(xs_hbm.at[tile_rows(g)], x_buf.at[slot], in_sem.at[slot])

    def store(g):
        slot = lax.rem(g, STREAM_DEPTH)
        return pltpu.make_async_copy(y_buf.at[slot], ys_hbm.at[tile_rows(g)], out_sem.at[slot])

    @pl.when(e == 0)
    def _():
        for g0 in range(STREAM_DEPTH - 1):
            @pl.when(g0 < n_used)
            def _():
                load(g0).start()

    for packed_ref, w16_ref in ((wg_ref, wg16_ref), (wu_ref, wu16_ref), (wd_ref, wd16_ref)):
        rows = packed_ref.shape[1]
        lo, hi = _unpack_rows(packed_ref[0])
        w16_ref[0:rows, :] = lo.astype(BF16)
        w16_ref[rows:, :] = hi.astype(BF16)

    def tile(j, carry):
        g = first + j
        slot = lax.rem(g, STREAM_DEPTH)
        load(g).wait()

        @pl.when(g + (STREAM_DEPTH - 1) < n_used)
        def _():
            load(g + (STREAM_DEPTH - 1)).start()

        @pl.when(g >= STREAM_DEPTH)
        def _():
            store(g - STREAM_DEPTH).wait()

        words = x_buf[slot]
        row = lax.broadcasted_iota(I32, words.shape, 0)
        words = jnp.where(row < count - j * ROW_TILE, words, jnp.uint32(0))
        lo, hi = _unpack_rows(words)
        lo, hi = lo.astype(BF16), hi.astype(BF16)
        hg = _dot(lo, wg16_ref[0:HALF, :]) + _dot(hi, wg16_ref[HALF:, :])
        hu = _dot(lo, wu16_ref[0:HALF, :]) + _dot(hi, wu16_ref[HALF:, :])
        a = (_silu(hg) * hu).astype(BF16)
        y_buf[slot] = _pack_rows(_dot(a, wd16_ref[...]))
        store(g).start()
        return carry

    lax.fori_loop(0, n_mine, tile, 0)

    @pl.when(e == N_EXPERTS - 1)
    def _():
        for back in range(STREAM_DEPTH, 0, -1):
            @pl.when(n_used >= back)
            def _():
                store(n_used - back).wait()


def _experts(xs, first_tile, n_tile, count, w_eg, w_eu, w_ed):
    d = D_MODEL
    by_expert = lambda e, *_: (e, 0, 0)
    grid_spec = pltpu.PrefetchScalarGridSpec(
        num_scalar_prefetch=3,
        grid=(N_EXPERTS,),
        in_specs=[pl.BlockSpec(memory_space=pl.ANY),
                  pl.BlockSpec((1, d // 2, EXPERT_DIM), by_expert),
                  pl.BlockSpec((1, d // 2, EXPERT_DIM), by_expert),
                  pl.BlockSpec((1, EXPERT_DIM // 2, d), by_expert)],
        out_specs=pl.BlockSpec(memory_space=pl.ANY),
        scratch_shapes=[pltpu.VMEM((d, EXPERT_DIM), BF16), pltpu.VMEM((d, EXPERT_DIM), BF16),
                        pltpu.VMEM((EXPERT_DIM, d), BF16),
                        pltpu.VMEM((STREAM_DEPTH, ROW_TILE, HALF), U32),
                        pltpu.VMEM((STREAM_DEPTH, ROW_TILE, HALF), U32),
                        pltpu.SemaphoreType.DMA((STREAM_DEPTH,)), pltpu.SemaphoreType.DMA((STREAM_DEPTH,))])
    return pl.pallas_call(
        _experts_kernel,
        grid_spec=grid_spec,
        out_shape=jax.ShapeDtypeStruct(xs.shape, U32),
        compiler_params=pltpu.CompilerParams(dimension_semantics=("arbitrary",),
                                             vmem_limit_bytes=VMEM_LIMIT),
        name="experts",
    )(first_tile, n_tile, count, xs, w_eg, w_eu, w_ed)


def _final_kernel(z_ref, gatew_ref, h2_ref, x1_ref, mod_ref, normf_ref, wsg_ref, wsu_ref, wsd_ref, *rest):
    y_ref, wsg16_ref, wsu16_ref, wsd16_ref = rest[-4:]

    @pl.when(pl.program_id(0) == 0)
    def _():
        wsg16_ref[...] = wsg_ref[...].astype(BF16)
        wsu16_ref[...] = wsu_ref[...].astype(BF16)
        wsd16_ref[...] = wsd_ref[...].astype(BF16)

    lo, hi = _unpack_rows(h2_ref[...])
    h = jnp.concatenate([lo, hi], axis=-1).astype(BF16)
    a = _silu(_dot(h, wsg16_ref[...])) * _dot(h, wsu16_ref[...])
    acc = _dot(a.astype(BF16), wsd16_ref[...])
    for s in range(TOP_K):
        lo, hi = _unpack_rows(z_ref[s])
        acc = acc + gatew_ref[:, s:s + 1] * jnp.concatenate([lo, hi], axis=-1)
    x2 = x1_ref[...] + _mod(mod_ref, 5) * acc
    y_ref[...] = _rms(x2) * normf_ref[...]


def _final(z, gate_w, h2, x1, mod, norm_f, w_sg, w_su, w_sd, block_t, first_block, per_seq,
           seq0=0, out_rows=None, y_prev=None):
    t, d = x1.shape
    out_rows = t if out_rows is None else out_rows
    out_first = seq0 * per_seq
    tok = lambda i: (i, 0)
    if per_seq:
        mod_spec = pl.BlockSpec((1, 6, d), lambda i: (seq0 + i // per_seq, 0, 0))
    else:
        mod_spec = pl.BlockSpec((block_t, 6 * d), tok)
    operands = [z, gate_w, h2, x1, mod, norm_f, w_sg, w_su, w_sd]
    in_specs = [pl.BlockSpec((TOP_K, block_t, HALF), lambda i: (0, first_block + i, 0)),
                pl.BlockSpec((block_t, LANES), tok),
                pl.BlockSpec((block_t, HALF), tok),
                pl.BlockSpec((block_t, d), tok),
                mod_spec,
                _full((1, d)),
                _full((d, EXPERT_DIM)), _full((d, EXPERT_DIM)), _full((EXPERT_DIM, d))]
    aliases = {}
    if y_prev is not None:
        aliases = {len(operands): 0}
        operands.append(y_prev)
        in_specs.append(pl.BlockSpec(memory_space=pl.ANY))
    return pl.pallas_call(
        _final_kernel,
        grid=(t // block_t,),
        in_specs=in_specs,
        out_specs=pl.BlockSpec((block_t, d), lambda i: (out_first + i, 0)),
        out_shape=jax.ShapeDtypeStruct((out_rows, d), F32),
        scratch_shapes=[pltpu.VMEM((d, EXPERT_DIM), BF16), pltpu.VMEM((d, EXPERT_DIM), BF16),
                        pltpu.VMEM((EXPERT_DIM, d), BF16)],
        input_output_aliases=aliases,
        compiler_params=pltpu.CompilerParams(dimension_semantics=("arbitrary",),
                                             vmem_limit_bytes=VMEM_LIMIT),
        name="final",
    )(*operands)


def kernel(x_prompt, x_sample, c_prompt, c_sample, state_ret, state_pool, norm1, norm2, norm_f,
           w_ada, b_ada, w_in, w_out, w_pool, pool_scale, w_router, router_bias, w_exp_gate,
           w_exp_up, w_exp_down, w_sh_gate, w_sh_up, w_sh_down):
    b, l, d = x_prompt.shape
    n = x_sample.shape[0]
    past_len = 16384

    mod_p, mod_s = _ada(c_prompt, c_sample, w_ada[0], b_ada[0])
    mod_p = mod_p.reshape(b, 6, d)

    w_in16 = jnp.take(w_in[0], _split_halves_columns(), axis=1).astype(BF16)
    w_out16 = w_out[0].astype(BF16)
    wr_t = w_router[0].T
    bias_t = jnp.broadcast_to(router_bias[0][:, None], (N_EXPERTS, LANES))
    n1, n2, nf = norm1[0].reshape(1, d), norm2[0].reshape(1, d), norm_f.reshape(1, d)
    ps = pool_scale[0].reshape(1, POOL_WIDTH)
    shared = (w_sh_gate[0], w_sh_up[0], w_sh_down[0])

    def routed(sources, experts):
        n_tiles = experts.shape[1] * TOP_K // ROW_TILE + N_EXPERTS
        pos_t, meta = _plan(experts)
        xs = _sc_scatter_rows(sources, pos_t, n_tiles * ROW_TILE, after=expert_w)
        ys = _experts(xs, meta[:, 0], meta[:, 1], meta[:, 2], *expert_w)
        return _sc_gather_rows(ys, pos_t)

    mix_args = (n1, w_in16, w_pool[0], ps, w_out16, n2, wr_t, bias_t)
    x1_s, h2_s, experts_s, gatew_s, ret_s, pool_s = _mix_sample(
        x_sample.reshape(n, d), mod_s, state_ret[0], state_pool[0], float(past_len), *mix_args)

    expert_w = (_sc_pack_weights(w_exp_gate[0], 64, after=(x1_s,)),
                _sc_pack_weights(w_exp_up[0], 64, after=(x1_s,)),
                _sc_pack_weights(w_exp_down[0], 16, after=(x1_s,)))

    ba = b // 2
    bb = b - ba
    x1_a, h2_a, experts_a, gatew_a, ret_a, pool_a = _mix_prompt(x_prompt, mod_p, 0, ba, *mix_args)
    z_a = routed((h2_a,), experts_a)
    x1_b, h2_b, experts_b, gatew_b, ret_b, pool_b = _mix_prompt(x_prompt, mod_p, ba, bb, *mix_args)
    z_b = routed((h2_b, h2_s), jnp.concatenate([experts_b, experts_s], axis=1))

    block_t = 256
    per_seq = l // block_t
    y_s = _final(z_b, gatew_s, h2_s, x1_s, mod_s, nf, *shared,
                 block_t=n, first_block=bb * l // n, per_seq=0)
    y_p = _final(z_b, gatew_b, h2_b, x1_b.reshape(bb * l, d), mod_p, nf, *shared,
                 block_t=block_t, first_block=0, per_seq=per_seq, seq0=ba, out_rows=b * l)
    y_p = _final(z_a, gatew_a, h2_a, x1_a.reshape(ba * l, d), mod_p, nf, *shared,
                 block_t=block_t, first_block=0, per_seq=per_seq, seq0=0, out_rows=b * l, y_prev=y_p)

    ret_p = jnp.concatenate([ret_a, ret_b], axis=0)
    pool_p = jnp.concatenate([pool_a, pool_b], axis=0)
    return (y_p.reshape(b, l, d), y_s.reshape(n, 1, d), ret_p[None], pool_p[None],
            ret_s[None], pool_s[None])
```

```python
import functools

import jax
import jax.numpy as jnp
import numpy as np
from jax import lax
from jax.experimental import pallas as pl
from jax.experimental.pallas import tpu as pltpu
from jax.experimental.pallas import tpu_sc as plsc

D_MODEL = 1024
RET_HEADS = 4
RET_QK_DIM = 64
RET_V_DIM = 128
RET_WIDTH = RET_HEADS * RET_V_DIM
QK_WIDTH = RET_HEADS * RET_QK_DIM
ROPE_BASE = 10000.0
POOL_WINDOWS = (2, 4, 8, 16)
POOL_WIDTH = 512
POOL_GROUP_DIM = 128
POOL_BUF = 15
IN_WIDTH = 2 * QK_WIDTH + 2 * RET_WIDTH + POOL_WIDTH
N_EXPERTS = 64
TOP_K = 8
N_EXPERT_GROUPS = 8
GROUP_SIZE = N_EXPERTS // N_EXPERT_GROUPS
TOP_GROUPS = 4
EXPERT_DIM = 256
ROUTE_SCALE = 2.5
EPS = 1e-6

LANES = 128
SUBLANES = 8
POOL_CARRY = 24
VMEM_LIMIT = 56 * 1024 * 1024
HALF = D_MODEL // 2
ROW_TILE = 512
SC_CHUNK = 128
SC_LANES = 16
SC_UNROLL = 16
STREAM_DEPTH = 8

BF16 = jnp.bfloat16
F32 = jnp.float32
U32 = jnp.uint32
I32 = jnp.int32


def _silu(x):
    return x * jax.nn.sigmoid(x)


def _dot(a, b):
    return jnp.dot(a, b, preferred_element_type=F32)


def _rms(x):
    return x * lax.rsqrt(jnp.mean(x * x, axis=-1, keepdims=True) + EPS)


def _mod(mod_ref, i, seq=0):
    if len(mod_ref.shape) == 3:
        return mod_ref[seq, i:i + 1, :]
    return mod_ref[:, i * D_MODEL:(i + 1) * D_MODEL]


def _split_bf16(x):
    hi = x.astype(BF16)
    lo = (x - hi.astype(F32)).astype(BF16)
    return hi, lo


def _pack_rows(x):
    lo = lax.bitcast_convert_type(x[:, :HALF].astype(BF16).astype(F32), U32)
    hi = lax.bitcast_convert_type(x[:, HALF:].astype(BF16).astype(F32), U32)
    return (hi & jnp.uint32(0xFFFF0000)) | (lo >> jnp.uint32(16))


def _unpack_rows(w):
    lo = lax.bitcast_convert_type(w << jnp.uint32(16), F32)
    hi = lax.bitcast_convert_type(w & jnp.uint32(0xFFFF0000), F32)
    return lo, hi


def _first_max_onehot(work, idx, n):
    m = jnp.max(work, axis=0, keepdims=True)
    first = jnp.min(jnp.where(work == m, idx, float(n)), axis=0, keepdims=True)
    return idx == first


def _route(h2, wr_t_ref, bias_t_ref):
    n = h2.shape[0]
    h_hi, h_lo = _split_bf16(h2)
    w_hi, w_lo = _split_bf16(wr_t_ref[...])
    nt = (((1,), (1,)), ((), ()))
    logits = (lax.dot_general(w_hi, h_hi, nt, preferred_element_type=F32)
              + lax.dot_general(w_hi, h_lo, nt, preferred_element_type=F32)
              + lax.dot_general(w_lo, h_hi, nt, preferred_element_type=F32))
    scores = jax.nn.sigmoid(logits)
    biased = scores + bias_t_ref[:, 0:1]
    b3 = biased.reshape(N_EXPERT_GROUPS, GROUP_SIZE, n)
    i3 = lax.broadcasted_iota(I32, b3.shape, 1).astype(F32)
    m1 = jnp.max(b3, axis=1, keepdims=True)
    first = jnp.min(jnp.where(b3 == m1, i3, float(GROUP_SIZE)), axis=1, keepdims=True)
    m2 = jnp.max(jnp.where(i3 == first, -jnp.inf, b3), axis=1, keepdims=True)
    gscore = (m1 + m2).reshape(N_EXPERT_GROUPS, n)
    gidx = lax.broadcasted_iota(I32, gscore.shape, 0).astype(F32)
    gsel = jnp.zeros(gscore.shape, F32)
    work = gscore
    for _ in range(TOP_GROUPS):
        hit = _first_max_onehot(work, gidx, N_EXPERT_GROUPS)
        gsel = jnp.where(hit, 1.0, gsel)
        work = jnp.where(hit, -jnp.inf, work)
    gsel3 = jnp.broadcast_to(gsel.reshape(N_EXPERT_GROUPS, 1, n), b3.shape)
    work = jnp.where(gsel3 > 0.0, b3, -jnp.inf).reshape(N_EXPERTS, n)
    eidx = lax.broadcasted_iota(I32, work.shape, 0).astype(F32)
    sel = jnp.zeros(work.shape, F32)
    for _ in range(TOP_K):
        hit = _first_max_onehot(work, eidx, N_EXPERTS)
        sel = jnp.where(hit, 1.0, sel)
        work = jnp.where(hit, -jnp.inf, work)
    picked = jnp.where(sel > 0.0, scores, 0.0)
    gates = picked / jnp.sum(picked, axis=0, keepdims=True) * ROUTE_SCALE
    below = (lax.broadcasted_iota(I32, (N_EXPERTS, N_EXPERTS), 1)
             < lax.broadcasted_iota(I32, (N_EXPERTS, N_EXPERTS), 0))
    slot = _dot(jnp.where(below, 1.0, 0.0).astype(BF16), sel.astype(BF16))
    e_rows, w_rows = [], []
    for s in range(TOP_K):
        here = jnp.where(slot == float(s), sel, 0.0)
        e_rows.append(jnp.sum(here * eidx, axis=0, keepdims=True))
        w_rows.append(jnp.sum(here * gates, axis=0, keepdims=True))
    experts = jnp.concatenate(e_rows, axis=0).astype(I32)
    w_t = jnp.concatenate(w_rows + [jnp.zeros((LANES - TOP_K, n), F32)], axis=0)
    return experts, w_t.T


def _group_norm_gate(o, g):
    parts = []
    for h in range(RET_HEADS):
        oh = o[:, h * RET_V_DIM:(h + 1) * RET_V_DIM]
        mu = jnp.mean(oh, axis=-1, keepdims=True)
        ctr = oh - mu
        var = jnp.mean(ctr * ctr, axis=-1, keepdims=True)
        parts.append(ctr * lax.rsqrt(var + EPS))
    return _silu(g) * jnp.concatenate(parts, axis=-1)


def _pool_project(pooled, w_pool_ref, pool_scale_ref):
    parts = [_dot(p.astype(BF16), w_pool_ref[gi].astype(BF16)) for gi, p in enumerate(pooled)]
    return jnp.concatenate(parts, axis=-1) * pool_scale_ref[...]


def _out_residual(x, o_gated, p, mod_ref, seq, w_out_ref):
    mix = jnp.concatenate([o_gated, p], axis=-1).astype(BF16)
    return x + _mod(mod_ref, 2, seq) * _dot(mix, w_out_ref[...])


def _norm_route(x1, mod_ref, seq, norm2_ref, wr_t_ref, bias_t_ref, h2_ref, experts_ref, gatew_ref):
    h2 = _rms(x1) * norm2_ref[...] * (1.0 + _mod(mod_ref, 4, seq)) + _mod(mod_ref, 3, seq)
    h2_ref[...] = _pack_rows(h2)
    experts, gate_w = _route(h2, wr_t_ref, bias_t_ref)
    experts_ref[...] = experts
    gatew_ref[...] = gate_w


def _ada_kernel(cp_ref, cs_ref, w_ref, b_ref, op_ref, os_ref):
    w16 = w_ref[...].astype(BF16)
    for c_ref, o_ref in ((cp_ref, op_ref), (cs_ref, os_ref)):
        o_ref[...] = _dot(_silu(c_ref[...]).astype(BF16), w16) + b_ref[...]


def _ada(c_prompt, c_sample, w_ada, b_ada, block_n=1536):
    d, width = w_ada.shape
    rows = lambda c: pl.BlockSpec((c.shape[0], d), lambda j: (0, 0))
    cols = lambda c: pl.BlockSpec((c.shape[0], block_n), lambda j: (0, j))
    return pl.pallas_call(
        _ada_kernel,
        grid=(width // block_n,),
        in_specs=[rows(c_prompt), rows(c_sample),
                  pl.BlockSpec((d, block_n), lambda j: (0, j)),
                  pl.BlockSpec((1, block_n), lambda j: (0, j))],
        out_specs=[cols(c_prompt), cols(c_sample)],
        out_shape=[jax.ShapeDtypeStruct((c.shape[0], width), F32) for c in (c_prompt, c_sample)],
        compiler_params=pltpu.CompilerParams(vmem_limit_bytes=VMEM_LIMIT),
        name="ada",
    )(c_prompt, c_sample, w_ada, b_ada.reshape(1, width))


def _mix_prompt_kernel(x_ref, mod_ref, norm1_ref, w_in_ref, cos_ref, sin_ref, dmat_ref, cross_ref,
                       tail_ref, cdec_ref, w_pool_ref, pool_scale_ref, w_out_ref, norm2_ref,
                       wr_t_ref, bias_t_ref,
                       x1_ref, h2_ref, experts_ref, gatew_ref, ret_ref, pool_ref,
                       state_ref, ext_ref, win_ref, o_ref, *, block_l, chunk, seqs):
    li = pl.program_id(1)

    @pl.when(li == 0)
    def _():
        state_ref[...] = jnp.zeros_like(state_ref)
        ext_ref[:, 0:POOL_CARRY, :] = jnp.zeros((seqs, POOL_CARRY, POOL_WIDTH), F32)
        win_ref[:, 0:SUBLANES, :] = jnp.zeros((seqs, SUBLANES, POOL_WIDTH), F32)

    for seq in range(seqs):
        _mix_prompt_seq(seq, li, x_ref, mod_ref, norm1_ref, w_in_ref, cos_ref, sin_ref, dmat_ref, cross_ref,
                        tail_ref, cdec_ref, w_pool_ref, pool_scale_ref, w_out_ref, norm2_ref,
                        wr_t_ref, bias_t_ref, x1_ref, h2_ref, experts_ref, gatew_ref,
                        state_ref.at[seq], ext_ref.at[seq], win_ref.at[seq], o_ref.at[seq],
                        block_l=block_l, chunk=chunk)

    @pl.when(li == pl.num_programs(1) - 1)
    def _():
        ret_ref[...] = state_ref[...].reshape(ret_ref.shape)
        pool_ref[...] = ext_ref[:, POOL_CARRY - POOL_BUF:POOL_CARRY, :]


def _window_sums(ext_ref, win_ref, block_l):
    g = POOL_GROUP_DIM
    top = POOL_CARRY + block_l
    new = slice(POOL_CARRY - SUBLANES, None)
    s2 = ext_ref[SUBLANES:top, :] + ext_ref[SUBLANES - 1:top - 1, :]
    win_ref[SUBLANES:top, g:] = s2[:, g:]
    s4 = s2[:, g:] + win_ref[SUBLANES - 2:top - 2, g:]
    win_ref[SUBLANES:top, 2 * g:] = s4[:, g:]
    s8 = s4[:, g:] + win_ref[SUBLANES - 4:top - 4, 2 * g:]
    win_ref[SUBLANES:top, 3 * g:] = s8[:, g:]
    s16 = s8[:, g:] + win_ref[0:top - SUBLANES, 3 * g:]
    return [s2[new, 0:g], s4[new, 0:g], s8[new, 0:g], s16[new, :]]


def _mix_prompt_seq(seq, li, x_ref, mod_ref, norm1_ref, w_in_ref, cos_ref, sin_ref, dmat_ref, cross_ref,
                    tail_ref, cdec_ref, w_pool_ref, pool_scale_ref, w_out_ref, norm2_ref,
                    wr_t_ref, bias_t_ref, x1_ref, h2_ref, experts_ref, gatew_ref,
                    state_ref, ext_ref, win_ref, o_ref, *, block_l, chunk):
    x = x_ref[seq]
    h16 = (_rms(x) * norm1_ref[...] * (1.0 + _mod(mod_ref, 1, seq)) + _mod(mod_ref, 0, seq)).astype(BF16)
    proj = _dot(h16, w_in_ref[...])
    q = proj[:, 0:QK_WIDTH]
    k = proj[:, QK_WIDTH:2 * QK_WIDTH]
    v = proj[:, 2 * QK_WIDTH:2 * QK_WIDTH + RET_WIDTH]
    g = proj[:, 2 * QK_WIDTH + RET_WIDTH:2 * QK_WIDTH + 2 * RET_WIDTH]
    u = proj[:, 2 * QK_WIDTH + 2 * RET_WIDTH:]

    lane = lax.broadcasted_iota(I32, q.shape, 1)
    first_half = (lane % RET_QK_DIM) < (RET_QK_DIM // 2)
    cos_t = cos_ref[...]
    sin_t = sin_ref[...]

    def rot(t):
        partner = jnp.where(first_half, pltpu.roll(t, QK_WIDTH - RET_QK_DIM // 2, axis=1),
                            pltpu.roll(t, RET_QK_DIM // 2, axis=1))
        return t * cos_t + partner * sin_t

    q = rot(q)
    k = rot(k) * (RET_QK_DIM ** -0.5)
    k_t = k.T
    v16 = v.astype(BF16)
    head_of_lane = lax.broadcasted_iota(I32, (chunk, QK_WIDTH), 1) // RET_QK_DIM

    for c in range(block_l // chunk):
        rows = slice(c * chunk, (c + 1) * chunk)
        q_c = q[rows]
        kt_c = k_t[:, rows]
        kt16 = kt_c.astype(BF16)
        state16 = state_ref[...].astype(BF16)
        for hd in range(RET_HEADS):
            in_head = head_of_lane == hd
            q_h = jnp.where(in_head, q_c, 0.0).astype(BF16)
            v_h = v16[rows, hd * RET_V_DIM:(hd + 1) * RET_V_DIM]
            scores = _dot(q_h, kt16) * dmat_ref[hd]
            inner = _dot(scores.astype(BF16), v_h)
            cross = _dot(q_h, state16) * cross_ref[hd]
            o_ref[rows, hd * RET_V_DIM:(hd + 1) * RET_V_DIM] = inner + cross
            hrows = slice(hd * RET_QK_DIM, (hd + 1) * RET_QK_DIM)
            k_dec = (kt_c[hrows] * tail_ref[hd:hd + 1, :]).astype(BF16)
            state_ref[hrows, :] = state_ref[hrows, :] * cdec_ref[hd] + _dot(k_dec, v_h)

    o_gated = _group_norm_gate(o_ref[...], g)

    ext_ref[POOL_CARRY:POOL_CARRY + block_l, :] = u
    pos = (li * block_l + lax.broadcasted_iota(I32, (block_l, 1), 0)).astype(F32)
    pooled = []
    for gi, (w, acc) in enumerate(zip(POOL_WINDOWS, _window_sums(ext_ref, win_ref, block_l))):
        cnt = jnp.minimum(pos + 1.0, float(w))
        pooled.append(acc / cnt - u[:, gi * POOL_GROUP_DIM:(gi + 1) * POOL_GROUP_DIM])
    p = _pool_project(pooled, w_pool_ref, pool_scale_ref)
    ext_ref[0:POOL_CARRY, :] = ext_ref[block_l:block_l + POOL_CARRY, :]

    x1 = _out_residual(x, o_gated, p, mod_ref, seq, w_out_ref)
    x1_ref[seq] = x1
    _norm_route(x1, mod_ref, seq, norm2_ref, wr_t_ref, bias_t_ref,
                h2_ref.at[seq], experts_ref.at[seq], gatew_ref.at[seq])


def _decay_tables(chunk):
    f32 = np.float32
    lg = np.log(f32(1.0) - f32(2.0) ** (f32(-5.0) - np.arange(RET_HEADS, dtype=f32))).astype(f32)
    idx = np.arange(chunk, dtype=f32)
    diff = idx[:, None] - idx[None, :]
    causal = diff >= 0
    dmat = np.where(causal[None], np.exp(lg[:, None, None] * np.where(causal, diff, f32(0.0))[None]), f32(0.0))
    cross = np.exp(lg[:, None] * (idx[None, :] + f32(1.0)))
    cross = np.broadcast_to(cross[:, :, None], (RET_HEADS, chunk, RET_V_DIM))
    tail = np.exp(lg[:, None] * (f32(chunk - 1.0) - idx)[None, :])
    cdec = np.broadcast_to(np.exp(lg * f32(chunk))[:, None, None], (RET_HEADS, RET_QK_DIM, RET_V_DIM))
    return tuple(jnp.asarray(np.ascontiguousarray(t, dtype=f32)) for t in (dmat, cross, tail, cdec))


def _rotary_angles(pos):
    half = RET_QK_DIM // 2
    freqs = (np.float32(ROPE_BASE) ** (-np.arange(half, dtype=np.float32) / np.float32(half))).astype(np.float32)
    return np.asarray(pos, np.float32)[:, None] * freqs[None, :]


def _rotary_tables(pos):
    ang = _rotary_angles(pos)
    cos, sin = np.cos(ang), np.sin(ang)
    cos_t = np.tile(np.concatenate([cos, cos], axis=-1), (1, RET_HEADS))
    sin_t = np.tile(np.concatenate([-sin, sin], axis=-1), (1, RET_HEADS))
    return jnp.asarray(cos_t, F32), jnp.asarray(sin_t, F32)


def _full(shape):
    return pl.BlockSpec(shape, lambda *_: (0,) * len(shape))


def _mix_prompt(x, mod, b0, b, norm1, w_in16, w_pool, pool_scale, w_out16, norm2, wr_t, bias_t,
                block_l=512, chunk=256, seqs=2):
    _, l, d = x.shape
    nl = l // block_l
    s0 = b0 // seqs
    cos_t, sin_t = _rotary_tables(np.arange(l))
    dmat, cross, tail, cdec = _decay_tables(chunk)
    kernel = functools.partial(_mix_prompt_kernel, block_l=block_l, chunk=chunk, seqs=seqs)
    tok = lambda bi, li: (bi, li, 0)
    per_seq = lambda bi, li: (bi, 0, 0)
    x1, h2, experts, gate_w, ret, pool = pl.pallas_call(
        kernel,
        grid=(b // seqs, nl),
        in_specs=[pl.BlockSpec((seqs, block_l, d), lambda bi, li: (s0 + bi, li, 0)),
                  pl.BlockSpec((seqs, 6, d), lambda bi, li: (s0 + bi, 0, 0)),
                  _full((1, d)),
                  _full((d, IN_WIDTH)),
                  pl.BlockSpec((block_l, QK_WIDTH), lambda bi, li: (li, 0)),
                  pl.BlockSpec((block_l, QK_WIDTH), lambda bi, li: (li, 0)),
                  _full(dmat.shape), _full(cross.shape), _full(tail.shape), _full(cdec.shape),
                  _full(w_pool.shape), _full((1, POOL_WIDTH)), _full((d, d)), _full((1, d)),
                  _full(wr_t.shape), _full(bias_t.shape)],
        out_specs=[pl.BlockSpec((seqs, block_l, d), tok),
                   pl.BlockSpec((seqs, block_l, HALF), tok),
                   pl.BlockSpec((seqs, TOP_K, block_l), lambda bi, li: (bi, 0, li)),
                   pl.BlockSpec((seqs, block_l, LANES), tok),
                   pl.BlockSpec((seqs, RET_HEADS, RET_QK_DIM, RET_V_DIM), lambda bi, li: (bi, 0, 0, 0)),
                   pl.BlockSpec((seqs, POOL_BUF, POOL_WIDTH), per_seq)],
        out_shape=[jax.ShapeDtypeStruct((b, l, d), F32),
                   jax.ShapeDtypeStruct((b, l, HALF), U32),
                   jax.ShapeDtypeStruct((b, TOP_K, l), I32),
                   jax.ShapeDtypeStruct((b, l, LANES), F32),
                   jax.ShapeDtypeStruct((b, RET_HEADS, RET_QK_DIM, RET_V_DIM), F32),
                   jax.ShapeDtypeStruct((b, POOL_BUF, POOL_WIDTH), F32)],
        scratch_shapes=[pltpu.VMEM((seqs, QK_WIDTH, RET_V_DIM), F32),
                        pltpu.VMEM((seqs, POOL_CARRY + block_l, POOL_WIDTH), F32),
                        pltpu.VMEM((seqs, POOL_CARRY + block_l, POOL_WIDTH), F32),
                        pltpu.VMEM((seqs, block_l, RET_WIDTH), F32)],
        compiler_params=pltpu.CompilerParams(dimension_semantics=("arbitrary", "arbitrary"),
                                             vmem_limit_bytes=VMEM_LIMIT),
        name="mix_prompt",
    )(x, mod, norm1, w_in16, cos_t, sin_t, dmat, cross, tail, cdec, w_pool, pool_scale,
      w_out16, norm2, wr_t, bias_t)
    experts = jnp.transpose(experts, (1, 0, 2)).reshape(TOP_K, b * l)
    return x1, h2.reshape(b * l, HALF), experts, gate_w.reshape(b * l, LANES), ret, pool


def _mix_sample_front_kernel(x_ref, mod_ref, norm1_ref, w_in_ref, cos_ref, sin_ref,
                             qt_ref, kt_ref, v_ref, g_ref, u_ref):
    x = x_ref[...]
    h = _rms(x) * norm1_ref[...] * (1.0 + _mod(mod_ref, 1)) + _mod(mod_ref, 0)
    proj = _dot(h.astype(BF16), w_in_ref[...])
    half = RET_QK_DIM // 2
    cos_c = cos_ref[...]
    sin_c = sin_ref[...]

    def rot_t(t):
        parts = []
        for hd in range(RET_HEADS):
            t1 = t[hd * RET_QK_DIM:hd * RET_QK_DIM + half]
            t2 = t[hd * RET_QK_DIM + half:(hd + 1) * RET_QK_DIM]
            parts += [t1 * cos_c - t2 * sin_c, t1 * sin_c + t2 * cos_c]
        return jnp.concatenate(parts, axis=0)

    qt_ref[...] = rot_t(proj[:, 0:QK_WIDTH].T)
    kt_ref[...] = rot_t(proj[:, QK_WIDTH:2 * QK_WIDTH].T) * (RET_QK_DIM ** -0.5)
    v_ref[...] = proj[:, 2 * QK_WIDTH:2 * QK_WIDTH + RET_WIDTH]
    g_ref[...] = proj[:, 2 * QK_WIDTH + RET_WIDTH:2 * QK_WIDTH + 2 * RET_WIDTH]
    u_ref[...] = proj[:, 2 * QK_WIDTH + 2 * RET_WIDTH:]


def _ret_step_kernel(qt_ref, kt_ref, v_ref, s0_ref, o_ref, s1_ref, *, block_b, decays):
    i = pl.program_id(0)
    lane = lax.broadcasted_iota(I32, qt_ref.shape, 1)
    for j in range(block_b):
        bi = i * block_b + j
        here = lane == bi
        q_col = jnp.sum(jnp.where(here, qt_ref[...], 0.0), axis=1, keepdims=True)
        k_col = jnp.sum(jnp.where(here, kt_ref[...], 0.0), axis=1, keepdims=True)
        v_row = v_ref[pl.ds(bi, 1), :]
        outs = []
        for hd in range(RET_HEADS):
            hrows = slice(hd * RET_QK_DIM, (hd + 1) * RET_QK_DIM)
            s1 = decays[hd] * s0_ref[j, hd] + k_col[hrows] * v_row[:, hd * RET_V_DIM:(hd + 1) * RET_V_DIM]
            s1_ref[j, hd] = s1
            outs.append(jnp.sum(q_col[hrows] * s1, axis=0, keepdims=True))
        o_ref[pl.ds(bi, 1), :] = jnp.concatenate(outs, axis=-1)


def _mix_sample_back_kernel(x_ref, mod_ref, o_ref, g_ref, u_ref, buf_ref, w_pool_ref, pool_scale_ref,
                            w_out_ref, norm2_ref, wr_t_ref, bias_t_ref,
                            x1_ref, h2_ref, experts_ref, gatew_ref, pool_ref):
    o_gated = _group_norm_gate(o_ref[...], g_ref[...])
    u = u_ref[...]
    pooled = []
    for gi, w in enumerate(POOL_WINDOWS):
        lanes = slice(gi * POOL_GROUP_DIM, (gi + 1) * POOL_GROUP_DIM)
        acc = u[:, lanes]
        for j in range(1, w):
            acc = acc + buf_ref[:, POOL_BUF - j, lanes]
        pooled.append(acc / float(w) - u[:, lanes])
    p = _pool_project(pooled, w_pool_ref, pool_scale_ref)
    pool_ref[:, 0:POOL_BUF - 1, :] = buf_ref[:, 1:POOL_BUF, :]
    pool_ref[:, POOL_BUF - 1, :] = u
    x1 = _out_residual(x_ref[...], o_gated, p, mod_ref, 0, w_out_ref)
    x1_ref[...] = x1
    _norm_route(x1, mod_ref, 0, norm2_ref, wr_t_ref, bias_t_ref, h2_ref, experts_ref, gatew_ref)


def _mix_sample(x, mod, state_ret, state_pool, start, norm1, w_in16, w_pool,
                pool_scale, w_out16, norm2, wr_t, bias_t, block_b=32):
    n, d = x.shape
    half = RET_QK_DIM // 2
    ang = _rotary_angles([start])
    cos_c = jnp.asarray(np.broadcast_to(np.cos(ang).T, (half, n)), F32)
    sin_c = jnp.asarray(np.broadcast_to(np.sin(ang).T, (half, n)), F32)
    params = pltpu.CompilerParams(vmem_limit_bytes=VMEM_LIMIT)
    qt, kt, v, g, u = pl.pallas_call(
        _mix_sample_front_kernel,
        out_shape=[jax.ShapeDtypeStruct((QK_WIDTH, n), F32), jax.ShapeDtypeStruct((QK_WIDTH, n), F32),
                   jax.ShapeDtypeStruct((n, RET_WIDTH), F32), jax.ShapeDtypeStruct((n, RET_WIDTH), F32),
                   jax.ShapeDtypeStruct((n, POOL_WIDTH), F32)],
        compiler_params=params,
        name="mix_sample_front",
    )(x, mod, norm1, w_in16, cos_c, sin_c)

    lg = np.log(1.0 - 2.0 ** (-5.0 - np.arange(RET_HEADS, dtype=np.float32)), dtype=np.float32)
    decays = tuple(float(np.exp(lg[h])) for h in range(RET_HEADS))
    state_block = (block_b, RET_HEADS, RET_QK_DIM, RET_V_DIM)
    o, s1 = pl.pallas_call(
        functools.partial(_ret_step_kernel, block_b=block_b, decays=decays),
        grid=(n // block_b,),
        in_specs=[_full((QK_WIDTH, n)), _full((QK_WIDTH, n)), _full((n, RET_WIDTH)),
                  pl.BlockSpec(state_block, lambda i: (i, 0, 0, 0))],
        out_specs=[_full((n, RET_WIDTH)), pl.BlockSpec(state_block, lambda i: (i, 0, 0, 0))],
        out_shape=[jax.ShapeDtypeStruct((n, RET_WIDTH), F32),
                   jax.ShapeDtypeStruct(state_ret.shape, F32)],
        compiler_params=pltpu.CompilerParams(dimension_semantics=("arbitrary",),
                                             vmem_limit_bytes=VMEM_LIMIT),
        name="ret_step",
    )(qt, kt, v, state_ret)

    x1, h2, experts, gate_w, pool = pl.pallas_call(
        _mix_sample_back_kernel,
        out_shape=[jax.ShapeDtypeStruct((n, d), F32),
                   jax.ShapeDtypeStruct((n, HALF), U32),
                   jax.ShapeDtypeStruct((TOP_K, n), I32),
                   jax.ShapeDtypeStruct((n, LANES), F32),
                   jax.ShapeDtypeStruct(state_pool.shape, F32)],
        compiler_params=params,
        name="mix_sample_back",
    )(x, mod, o, g, u, state_pool, w_pool, pool_scale, w_out16, norm2, wr_t, bias_t)
    return x1, h2, experts, gate_w, s1, pool


def _plan_kernel(experts_ref, pos_ref, meta_ref, cnt_ref, carry_ref, off_ref, *, block_t):
    phase = pl.program_id(0)
    j = pl.program_id(1)
    e_blk = experts_ref[...]
    eidx = lax.broadcasted_iota(I32, (N_EXPERTS, block_t), 0)
    member = jnp.zeros((N_EXPERTS, block_t), F32)
    for s in range(TOP_K):
        member = member + jnp.where(eidx == e_blk[s:s + 1, :], 1.0, 0.0)
    per_expert = jnp.broadcast_to(jnp.sum(member, axis=1, keepdims=True), (N_EXPERTS, LANES))

    @pl.when((phase == 0) & (j == 0))
    def _():
        cnt_ref[...] = jnp.zeros_like(cnt_ref)

    @pl.when(phase == 0)
    def _():
        cnt_ref[...] += per_expert

    @pl.when((phase == 0) & (j == pl.num_programs(1) - 1))
    def _():
        cnt = cnt_ref[...]
        n_tile = jnp.floor((cnt + (ROW_TILE - 1.0)) * (1.0 / ROW_TILE))
        upto = (lax.broadcasted_iota(I32, (N_EXPERTS, N_EXPERTS), 1)
                <= lax.broadcasted_iota(I32, (N_EXPERTS, N_EXPERTS), 0))
        tile_end = _dot(jnp.where(upto, 1.0, 0.0).astype(BF16), n_tile.astype(BF16))
        tile_start = tile_end - n_tile
        off_ref[...] = tile_start * ROW_TILE
        carry_ref[...] = jnp.zeros_like(carry_ref)
        lane = lax.broadcasted_iota(I32, cnt.shape, 1)
        meta_ref[...] = jnp.where(lane == 0, tile_start, jnp.where(lane == 1, n_tile, cnt)).astype(I32)

    @pl.when(phase == 1)
    def _():
        before = (lax.broadcasted_iota(I32, (block_t, block_t), 0)
                  < lax.broadcasted_iota(I32, (block_t, block_t), 1))
        rank = _dot(member.astype(BF16), jnp.where(before, 1.0, 0.0).astype(BF16))
        row = off_ref[:, 0:1] + carry_ref[:, 0:1] + rank
        carry_ref[...] += per_expert
        out = [jnp.sum(jnp.where(eidx == e_blk[s:s + 1, :], row, 0.0), axis=0, keepdims=True)
               for s in range(TOP_K)]
        pos_ref[...] = jnp.concatenate(out, axis=0).astype(I32)


def _plan(experts_all, max_block=1024):
    n_tokens = experts_all.shape[1]
    block_t = max(k for k in range(LANES, max_block + 1, LANES) if n_tokens % k == 0)
    nb = n_tokens // block_t
    return pl.pallas_call(
        functools.partial(_plan_kernel, block_t=block_t),
        grid=(2, nb),
        in_specs=[pl.BlockSpec((TOP_K, block_t), lambda ph, j: (0, j))],
        out_specs=[pl.BlockSpec((TOP_K, block_t), lambda ph, j: (0, j * ph)),
                   _full((N_EXPERTS, LANES))],
        out_shape=[jax.ShapeDtypeStruct((TOP_K, n_tokens), I32),
                   jax.ShapeDtypeStruct((N_EXPERTS, LANES), I32)],
        scratch_shapes=[pltpu.VMEM((N_EXPERTS, LANES), F32)] * 3,
        compiler_params=pltpu.CompilerParams(dimension_semantics=("arbitrary", "arbitrary"),
                                             vmem_limit_bytes=VMEM_LIMIT),
        name="plan",
    )(experts_all)


def _sc_workers():
    info = plsc.get_sparse_core_info()
    return info.num_cores, info.num_cores * info.num_subcores


def _sc_scatter_rows(sources, pos_t, n_out, after=()):
    w = sources[0].shape[1]
    s = pos_t.shape[0]
    n_cores, n_workers = _sc_workers()
    bounds = np.cumsum([0] + [src.shape[0] // SC_CHUNK for src in sources])
    n_chunks = int(bounds[-1])
    iters = -(-n_chunks // n_workers)
    mesh = plsc.VectorSubcoreMesh(core_axis_name="c", subcore_axis_name="s")

    @functools.partial(
        pl.kernel, mesh=mesh, out_type=jax.ShapeDtypeStruct((n_out, w), sources[0].dtype),
        scratch_types=[pltpu.VMEM((SC_CHUNK, w), sources[0].dtype), pltpu.VMEM((s, SC_CHUNK), I32),
                       pltpu.SemaphoreType.DMA],
        name="dispatch")
    def k(*refs):
        src_hbm, pos_hbm = refs[:len(sources)], refs[len(sources)]
        out_hbm, rows_v, idx_v, sem = refs[len(sources) + 1 + len(after):]
        wid = lax.axis_index("s") * n_cores + lax.axis_index("c")

        @pl.loop(0, iters)
        def _(it):
            c = it * n_workers + wid
            for src, lo, hi in zip(src_hbm, bounds[:-1], bounds[1:]):
                @pl.when((c >= int(lo)) & (c < int(hi)))
                def _():
                    base = pl.multiple_of((c - int(lo)) * SC_CHUNK, SC_CHUNK)
                    pltpu.sync_copy(src.at[pl.ds(base, SC_CHUNK)], rows_v)

            @pl.when(c < n_chunks)
            def _():
                base = pl.multiple_of(c * SC_CHUNK, SC_CHUNK)
                pltpu.sync_copy(pos_hbm.at[:, pl.ds(base, SC_CHUNK)], idx_v)
                copies = [pltpu.async_copy(rows_v, out_hbm.at[idx_v.at[j]], sem) for j in range(s)]
                for cp in copies:
                    cp.wait()

    return k(*sources, pos_t, *after)


def _sc_gather_rows(table, pos_t):
    _, w = table.shape
    s, t = pos_t.shape
    n_cores, n_workers = _sc_workers()
    n_chunks = t // SC_CHUNK
    iters = -(-n_chunks // n_workers)
    mesh = plsc.VectorSubcoreMesh(core_axis_name="c", subcore_axis_name="s")

    @functools.partial(
        pl.kernel, mesh=mesh, out_type=jax.ShapeDtypeStruct((s, t, w), table.dtype),
        scratch_types=[pltpu.VMEM((SC_CHUNK, w), table.dtype), pltpu.VMEM((s, SC_CHUNK), I32),
                       pltpu.SemaphoreType.DMA],
        name="combine")
    def k(table_hbm, pos_hbm, out_hbm, rows_v, idx_v, sem):
        wid = lax.axis_index("s") * n_cores + lax.axis_index("c")

        @pl.loop(0, iters)
        def _(it):
            c = it * n_workers + wid

            @pl.when(c < n_chunks)
            def _():
                base = pl.multiple_of(c * SC_CHUNK, SC_CHUNK)
                pltpu.sync_copy(pos_hbm.at[:, pl.ds(base, SC_CHUNK)], idx_v)
                for j in range(s):
                    pltpu.async_copy(table_hbm.at[idx_v.at[j]], rows_v, sem).wait()
                    pltpu.sync_copy(rows_v, out_hbm.at[j, pl.ds(base, SC_CHUNK)])

    return k(table, pos_t)


def _sc_pack_weights(w, rows_per_item, after=()):
    e, r, c = w.shape
    half = r // 2
    rb = rows_per_item
    per_expert = half // rb
    n_cores, n_workers = _sc_workers()
    per_worker = e * per_expert // n_workers
    assert per_worker * n_workers == e * per_expert and per_worker % 2 == 0 and c % (SC_LANES * SC_UNROLL) == 0
    mesh = plsc.VectorSubcoreMesh(core_axis_name="c", subcore_axis_name="s")

    @functools.partial(
        pl.kernel, mesh=mesh, out_type=jax.ShapeDtypeStruct((e * half, c), U32),
        scratch_types=[pltpu.VMEM((2, rb, c), F32), pltpu.VMEM((2, rb, c), F32), pltpu.VMEM((2, rb, c), U32),
                       pltpu.SemaphoreType.DMA((2,)), pltpu.SemaphoreType.DMA((2,))],
        compiler_params=pltpu.CompilerParams(needs_layout_passes=False),
        cost_estimate=pl.CostEstimate(flops=e * r * c, transcendentals=0, bytes_accessed=6 * e * r * c),
        name="pack_weights")
    def k(w_hbm, *refs):
        out_hbm, lo_v, hi_v, out_v, in_sem, out_sem = refs[len(after):]
        wid = lax.axis_index("s") * n_cores + lax.axis_index("c")
        first = wid * per_worker

        def rows(item):
            ex = item // per_expert
            j = item - ex * per_expert
            return (pl.multiple_of(ex * r + j * rb, rb), pl.multiple_of(ex * r + half + j * rb, rb),
                    pl.multiple_of(ex * half + j * rb, rb))

        def loads(item, b):
            lo_row, hi_row, _ = rows(item)
            return (pltpu.make_async_copy(w_hbm.at[pl.ds(lo_row, rb)], lo_v.at[b], in_sem.at[b]),
                    pltpu.make_async_copy(w_hbm.at[pl.ds(hi_row, rb)], hi_v.at[b], in_sem.at[b]))

        def store(item, b):
            return pltpu.make_async_copy(out_v.at[b], out_hbm.at[pl.ds(rows(item)[2], rb)], out_sem.at[b])

        for cp in loads(first, 0):
            cp.start()

        @pl.loop(0, per_worker // 2)
        def _(pair):
            for b in range(2):
                item = first + pair * 2 + b
                for cp in loads(item, b):
                    cp.wait()

                @pl.when(item + 1 < first + per_worker)
                def _():
                    for cp in loads(item + 1, 1 - b):
                        cp.start()

                @pl.when(pair > 0)
                def _():
                    store(item - 2, b).wait()

                @pl.loop(0, rb)
                def _(i):
                    @pl.loop(0, c // (SC_LANES * SC_UNROLL))
                    def _(vb):
                        for u in range(SC_UNROLL):
                            sl = pl.ds(pl.multiple_of((vb * SC_UNROLL + u) * SC_LANES, SC_LANES), SC_LANES)
                            packed = plsc.pack(lo_v[b, i, sl], hi_v[b, i, sl], format=plsc.PackFormat.INTERLEAVED)
                            out_v[b, i, sl] = plsc.bitcast(packed, U32)

                store(item, b).start()

        for b in range(2):
            store(first + per_worker - 2 + b, b).wait()

    return k(w.reshape(e * r, c), *after).reshape(e, half, c)


def _experts_kernel(first_ref, ntile_ref, cnt_ref, xs_hbm, wg_ref, wu_ref, wd_ref, ys_hbm,
                    wg16_ref, wu16_ref, wd16_ref, x_buf, y_buf, in_sem, out_sem):
    e = pl.program_id(0)
    n_used = first_ref[N_EXPERTS - 1] + ntile_ref[N_EXPERTS - 1]
    first, n_mine, count = first_ref[e], ntile_ref[e], cnt_ref[e]

    def tile_rows(g):
        return pl.ds(pl.multiple_of(g * ROW_TILE, ROW_TILE), ROW_TILE)

    def load(g):
        slot = lax.rem(g, STREAM_DEPTH)
        return pltpu.make_async_copy(xs_hbm.at[tile_rows(g)], x_buf.at[slot], in_sem.at[slot])

    def store(g):
        slot = lax.rem(g, STREAM_DEPTH)
        return pltpu.make_async_copy(y_buf.at[slot], ys_hbm.at[tile_rows(g)], out_sem.at[slot])

    @pl.when(e == 0)
    def _():
        for g0 in range(STREAM_DEPTH - 1):
            @pl.when(g0 < n_used)
            def _():
                load(g0).start()

    for packed_ref, w16_ref in ((wg_ref, wg16_ref), (wu_ref, wu16_ref), (wd_ref, wd16_ref)):
        rows = packed_ref.shape[1]
        lo, hi = _unpack_rows(packed_ref[0])
        w16_ref[0:rows, :] = lo.astype(BF16)
        w16_ref[rows:, :] = hi.astype(BF16)

    def tile(j, carry):
        g = first + j
        slot = lax.rem(g, STREAM_DEPTH)
        load(g).wait()

        @pl.when(g + (STREAM_DEPTH - 1) < n_used)
        def _():
            load(g + (STREAM_DEPTH - 1)).start()

        @pl.when(g >= STREAM_DEPTH)
        def _():
            store(g - STREAM_DEPTH).wait()

        words = x_buf[slot]
        row = lax.broadcasted_iota(I32, words.shape, 0)
        words = jnp.where(row < count - j * ROW_TILE, words, jnp.uint32(0))
        lo, hi = _unpack_rows(words)
        lo, hi = lo.astype(BF16), hi.astype(BF16)
        hg = _dot(lo, wg16_ref[0:HALF, :]) + _dot(hi, wg16_ref[HALF:, :])
        hu = _dot(lo, wu16_ref[0:HALF, :]) + _dot(hi, wu16_ref[HALF:, :])
        a = (_silu(hg) * hu).astype(BF16)
        y_buf[slot] = _pack_rows(_dot(a, wd16_ref[...]))
        store(g).start()
        return carry

    lax.fori_loop(0, n_mine, tile, 0)

    @pl.when(e == N_EXPERTS - 1)
    def _():
        for back in range(STREAM_DEPTH, 0, -1):
            @pl.when(n_used >= back)
            def _():
                store(n_used - back).wait()


def _experts(xs, first_tile, n_tile, count, w_eg, w_eu, w_ed):
    d = D_MODEL
    by_expert = lambda e, *_: (e, 0, 0)
    grid_spec = pltpu.PrefetchScalarGridSpec(
        num_scalar_prefetch=3,
        grid=(N_EXPERTS,),
        in_specs=[pl.BlockSpec(memory_space=pl.ANY),
                  pl.BlockSpec((1, d // 2, EXPERT_DIM), by_expert),
                  pl.BlockSpec((1, d // 2, EXPERT_DIM), by_expert),
                  pl.BlockSpec((1, EXPERT_DIM // 2, d), by_expert)],
        out_specs=pl.BlockSpec(memory_space=pl.ANY),
        scratch_shapes=[pltpu.VMEM((d, EXPERT_DIM), BF16), pltpu.VMEM((d, EXPERT_DIM), BF16),
                        pltpu.VMEM((EXPERT_DIM, d), BF16),
                        pltpu.VMEM((STREAM_DEPTH, ROW_TILE, HALF), U32),
                        pltpu.VMEM((STREAM_DEPTH, ROW_TILE, HALF), U32),
                        pltpu.SemaphoreType.DMA((STREAM_DEPTH,)), pltpu.SemaphoreType.DMA((STREAM_DEPTH,))])
    return pl.pallas_call(
        _experts_kernel,
        grid_spec=grid_spec,
        out_shape=jax.ShapeDtypeStruct(xs.shape, U32),
        compiler_params=pltpu.CompilerParams(dimension_semantics=("arbitrary",),
                                             vmem_limit_bytes=VMEM_LIMIT),
        name="experts",
    )(first_tile, n_tile, count, xs, w_eg, w_eu, w_ed)


def _final_kernel(z_ref, gatew_ref, h2_ref, x1_ref, mod_ref, normf_ref, wsg_ref, wsu_ref, wsd_ref, *rest):
    y_ref, wsg16_ref, wsu16_ref, wsd16_ref = rest[-4:]

    @pl.when(pl.program_id(0) == 0)
    def _():
        wsg16_ref[...] = wsg_ref[...].astype(BF16)
        wsu16_ref[...] = wsu_ref[...].astype(BF16)
        wsd16_ref[...] = wsd_ref[...].astype(BF16)

    lo, hi = _unpack_rows(h2_ref[...])
    h = jnp.concatenate([lo, hi], axis=-1).astype(BF16)
    a = _silu(_dot(h, wsg16_ref[...])) * _dot(h, wsu16_ref[...])
    acc = _dot(a.astype(BF16), wsd16_ref[...])
    for s in range(TOP_K):
        lo, hi = _unpack_rows(z_ref[s])
        acc = acc + gatew_ref[:, s:s + 1] * jnp.concatenate([lo, hi], axis=-1)
    x2 = x1_ref[...] + _mod(mod_ref, 5) * acc
    y_ref[...] = _rms(x2) * normf_ref[...]


def _final(z, gate_w, h2, x1, mod, norm_f, w_sg, w_su, w_sd, block_t, first_block, per_seq,
           seq0=0, out_rows=None, y_prev=None):
    t, d = x1.shape
    out_rows = t if out_rows is None else out_rows
    out_first = seq0 * per_seq
    tok = lambda i: (i, 0)
    if per_seq:
        mod_spec = pl.BlockSpec((1, 6, d), lambda i: (seq0 + i // per_seq, 0, 0))
    else:
        mod_spec = pl.BlockSpec((block_t, 6 * d), tok)
    operands = [z, gate_w, h2, x1, mod, norm_f, w_sg, w_su, w_sd]
    in_specs = [pl.BlockSpec((TOP_K, block_t, HALF), lambda i: (0, first_block + i, 0)),
                pl.BlockSpec((block_t, LANES), tok),
                pl.BlockSpec((block_t, HALF), tok),
                pl.BlockSpec((block_t, d), tok),
                mod_spec,
                _full((1, d)),
                _full((d, EXPERT_DIM)), _full((d, EXPERT_DIM)), _full((EXPERT_DIM, d))]
    aliases = {}
    if y_prev is not None:
        aliases = {len(operands): 0}
        operands.append(y_prev)
        in_specs.append(pl.BlockSpec(memory_space=pl.ANY))
    return pl.pallas_call(
        _final_kernel,
        grid=(t // block_t,),
        in_specs=in_specs,
        out_specs=pl.BlockSpec((block_t, d), lambda i: (out_first + i, 0)),
        out_shape=jax.ShapeDtypeStruct((out_rows, d), F32),
        scratch_shapes=[pltpu.VMEM((d, EXPERT_DIM), BF16), pltpu.VMEM((d, EXPERT_DIM), BF16),
                        pltpu.VMEM((EXPERT_DIM, d), BF16)],
        input_output_aliases=aliases,
        compiler_params=pltpu.CompilerParams(dimension_semantics=("arbitrary",),
                                             vmem_limit_bytes=VMEM_LIMIT),
        name="final",
    )(*operands)


def kernel(x_prompt, x_sample, c_prompt, c_sample, state_ret, state_pool, norm1, norm2, norm_f,
           w_ada, b_ada, w_in, w_out, w_pool, pool_scale, w_router, router_bias, w_exp_gate,
           w_exp_up, w_exp_down, w_sh_gate, w_sh_up, w_sh_down):
    b, l, d = x_prompt.shape
    n = x_sample.shape[0]
    past_len = 16384

    mod_p, mod_s = _ada(c_prompt, c_sample, w_ada[0], b_ada[0])
    mod_p = mod_p.reshape(b, 6, d)

    w_in16 = w_in[0].astype(BF16)
    w_out16 = w_out[0].astype(BF16)
    wr_t = w_router[0].T
    bias_t = jnp.broadcast_to(router_bias[0][:, None], (N_EXPERTS, LANES))
    n1, n2, nf = norm1[0].reshape(1, d), norm2[0].reshape(1, d), norm_f.reshape(1, d)
    ps = pool_scale[0].reshape(1, POOL_WIDTH)
    shared = (w_sh_gate[0], w_sh_up[0], w_sh_down[0])

    def routed(sources, experts):
        n_tiles = experts.shape[1] * TOP_K // ROW_TILE + N_EXPERTS
        pos_t, meta = _plan(experts)
        xs = _sc_scatter_rows(sources, pos_t, n_tiles * ROW_TILE, after=expert_w)
        ys = _experts(xs, meta[:, 0], meta[:, 1], meta[:, 2], *expert_w)
        return _sc_gather_rows(ys, pos_t)

    mix_args = (n1, w_in16, w_pool[0], ps, w_out16, n2, wr_t, bias_t)
    x1_s, h2_s, experts_s, gatew_s, ret_s, pool_s = _mix_sample(
        x_sample.reshape(n, d), mod_s, state_ret[0], state_pool[0], float(past_len), *mix_args)

    expert_w = (_sc_pack_weights(w_exp_gate[0], 64, after=(x1_s,)),
                _sc_pack_weights(w_exp_up[0], 64, after=(x1_s,)),
                _sc_pack_weights(w_exp_down[0], 16, after=(x1_s,)))

    ba = b // 2
    bb = b - ba
    x1_a, h2_a, experts_a, gatew_a, ret_a, pool_a = _mix_prompt(x_prompt, mod_p, 0, ba, *mix_args)
    z_a = routed((h2_a,), experts_a)
    x1_b, h2_b, experts_b, gatew_b, ret_b, pool_b = _mix_prompt(x_prompt, mod_p, ba, bb, *mix_args)
    z_b = routed((h2_b, h2_s), jnp.concatenate([experts_b, experts_s], axis=1))

    block_t = 512
    per_seq = l // block_t
    y_s = _final(z_b, gatew_s, h2_s, x1_s, mod_s, nf, *shared,
                 block_t=n, first_block=bb * l // n, per_seq=0)
    y_p = _final(z_b, gatew_b, h2_b, x1_b.reshape(bb * l, d), mod_p, nf, *shared,
                 block_t=block_t, first_block=0, per_seq=per_seq, seq0=ba, out_rows=b * l)
    y_p = _final(z_a, gatew_a, h2_a, x1_a.reshape(ba * l, d), mod_p, nf, *shared,
                 block_t=block_t, first_block=0, per_seq=per_seq, seq0=0, out_rows=b * l, y_prev=y_p)

    ret_p = jnp.concatenate([ret_a, ret_b], axis=0)
    pool_p = jnp.concatenate([pool_a, pool_b], axis=0)
    return (y_p.reshape(b, l, d), y_s.reshape(n, 1, d), ret_p[None], pool_p[None],
            ret_s[None], pool_s[None])
```

```python
import functools

import jax
import jax.numpy as jnp
import numpy as np
from jax import lax
from jax.experimental import pallas as pl
from jax.experimental.pallas import tpu as pltpu
from jax.experimental.pallas import tpu_sc as plsc

D_MODEL = 1024
RET_HEADS = 4
RET_QK_DIM = 64
RET_V_DIM = 128
RET_WIDTH = RET_HEADS * RET_V_DIM
QK_WIDTH = RET_HEADS * RET_QK_DIM
ROPE_BASE = 10000.0
POOL_WINDOWS = (2, 4, 8, 16)
POOL_WIDTH = 512
POOL_GROUP_DIM = 128
POOL_BUF = 15
IN_WIDTH = 2 * QK_WIDTH + 2 * RET_WIDTH + POOL_WIDTH
N_EXPERTS = 64
TOP_K = 8
N_EXPERT_GROUPS = 8
GROUP_SIZE = N_EXPERTS // N_EXPERT_GROUPS
TOP_GROUPS = 4
EXPERT_DIM = 256
ROUTE_SCALE = 2.5
EPS = 1e-6

LANES = 128
SUBLANES = 8
POOL_CARRY = 24
VMEM_LIMIT = 56 * 1024 * 1024
HALF = D_MODEL // 2
ROW_TILE = 512
SC_CHUNK = 128
SC_LANES = 16
SC_UNROLL = 16
STREAM_DEPTH = 8

BF16 = jnp.bfloat16
F32 = jnp.float32
U32 = jnp.uint32
I32 = jnp.int32


def _silu(x):
    return x * jax.nn.sigmoid(x)


def _dot(a, b):
    return jnp.dot(a, b, preferred_element_type=F32)


def _rms(x):
    return x * lax.rsqrt(jnp.mean(x * x, axis=-1, keepdims=True) + EPS)


def _mod(mod_ref, i, seq=0):
    if len(mod_ref.shape) == 3:
        return mod_ref[seq, i:i + 1, :]
    return mod_ref[:, i * D_MODEL:(i + 1) * D_MODEL]


def _split_bf16(x):
    hi = x.astype(BF16)
    lo = (x - hi.astype(F32)).astype(BF16)
    return hi, lo


def _pack_rows(x):
    lo = lax.bitcast_convert_type(x[:, :HALF].astype(BF16).astype(F32), U32)
    hi = lax.bitcast_convert_type(x[:, HALF:].astype(BF16).astype(F32), U32)
    return (hi & jnp.uint32(0xFFFF0000)) | (lo >> jnp.uint32(16))


def _unpack_rows(w):
    lo = lax.bitcast_convert_type(w << jnp.uint32(16), F32)
    hi = lax.bitcast_convert_type(w & jnp.uint32(0xFFFF0000), F32)
    return lo, hi


def _first_max_onehot(work, idx, n):
    m = jnp.max(work, axis=0, keepdims=True)
    first = jnp.min(jnp.where(work == m, idx, float(n)), axis=0, keepdims=True)
    return idx == first


def _route(h2, wr_t_ref, bias_t_ref):
    n = h2.shape[0]
    h_hi, h_lo = _split_bf16(h2)
    w_hi, w_lo = _split_bf16(wr_t_ref[...])
    nt = (((1,), (1,)), ((), ()))
    logits = (lax.dot_general(w_hi, h_hi, nt, preferred_element_type=F32)
              + lax.dot_general(w_hi, h_lo, nt, preferred_element_type=F32)
              + lax.dot_general(w_lo, h_hi, nt, preferred_element_type=F32))
    scores = jax.nn.sigmoid(logits)
    biased = scores + bias_t_ref[:, 0:1]
    b3 = biased.reshape(N_EXPERT_GROUPS, GROUP_SIZE, n)
    i3 = lax.broadcasted_iota(I32, b3.shape, 1).astype(F32)
    m1 = jnp.max(b3, axis=1, keepdims=True)
    first = jnp.min(jnp.where(b3 == m1, i3, float(GROUP_SIZE)), axis=1, keepdims=True)
    m2 = jnp.max(jnp.where(i3 == first, -jnp.inf, b3), axis=1, keepdims=True)
    gscore = (m1 + m2).reshape(N_EXPERT_GROUPS, n)
    gidx = lax.broadcasted_iota(I32, gscore.shape, 0).astype(F32)
    gsel = jnp.zeros(gscore.shape, F32)
    work = gscore
    for _ in range(TOP_GROUPS):
        hit = _first_max_onehot(work, gidx, N_EXPERT_GROUPS)
        gsel = jnp.where(hit, 1.0, gsel)
        work = jnp.where(hit, -jnp.inf, work)
    gsel3 = jnp.broadcast_to(gsel.reshape(N_EXPERT_GROUPS, 1, n), b3.shape)
    work = jnp.where(gsel3 > 0.0, b3, -jnp.inf).reshape(N_EXPERTS, n)
    eidx = lax.broadcasted_iota(I32, work.shape, 0).astype(F32)
    sel = jnp.zeros(work.shape, F32)
    for _ in range(TOP_K):
        hit = _first_max_onehot(work, eidx, N_EXPERTS)
        sel = jnp.where(hit, 1.0, sel)
        work = jnp.where(hit, -jnp.inf, work)
    picked = jnp.where(sel > 0.0, scores, 0.0)
    gates = picked / jnp.sum(picked, axis=0, keepdims=True) * ROUTE_SCALE
    below = (lax.broadcasted_iota(I32, (N_EXPERTS, N_EXPERTS), 1)
             < lax.broadcasted_iota(I32, (N_EXPERTS, N_EXPERTS), 0))
    slot = _dot(jnp.where(below, 1.0, 0.0).astype(BF16), sel.astype(BF16))
    e_rows, w_rows = [], []
    for s in range(TOP_K):
        here = jnp.where(slot == float(s), sel, 0.0)
        e_rows.append(jnp.sum(here * eidx, axis=0, keepdims=True))
        w_rows.append(jnp.sum(here * gates, axis=0, keepdims=True))
    experts = jnp.concatenate(e_rows, axis=0).astype(I32)
    w_t = jnp.concatenate(w_rows + [jnp.zeros((LANES - TOP_K, n), F32)], axis=0)
    return experts, w_t.T


def _group_norm_gate(o, g):
    parts = []
    for h in range(RET_HEADS):
        oh = o[:, h * RET_V_DIM:(h + 1) * RET_V_DIM]
        mu = jnp.mean(oh, axis=-1, keepdims=True)
        ctr = oh - mu
        var = jnp.mean(ctr * ctr, axis=-1, keepdims=True)
        parts.append(ctr * lax.rsqrt(var + EPS))
    return _silu(g) * jnp.concatenate(parts, axis=-1)


def _pool_project(pooled, w_pool_ref, pool_scale_ref):
    parts = [_dot(p.astype(BF16), w_pool_ref[gi].astype(BF16)) for gi, p in enumerate(pooled)]
    return jnp.concatenate(parts, axis=-1) * pool_scale_ref[...]


def _out_residual(x, o_gated, p, mod_ref, seq, w_out_ref):
    mix = jnp.concatenate([o_gated, p], axis=-1).astype(BF16)
    return x + _mod(mod_ref, 2, seq) * _dot(mix, w_out_ref[...])


def _norm_route(x1, mod_ref, seq, norm2_ref, wr_t_ref, bias_t_ref, h2_ref, experts_ref, gatew_ref):
    h2 = _rms(x1) * norm2_ref[...] * (1.0 + _mod(mod_ref, 4, seq)) + _mod(mod_ref, 3, seq)
    h2_ref[...] = _pack_rows(h2)
    experts, gate_w = _route(h2, wr_t_ref, bias_t_ref)
    experts_ref[...] = experts
    gatew_ref[...] = gate_w


def _ada_kernel(cp_ref, cs_ref, w_ref, b_ref, op_ref, os_ref):
    w16 = w_ref[...].astype(BF16)
    for c_ref, o_ref in ((cp_ref, op_ref), (cs_ref, os_ref)):
        o_ref[...] = _dot(_silu(c_ref[...]).astype(BF16), w16) + b_ref[...]


def _ada(c_prompt, c_sample, w_ada, b_ada, block_n=1536):
    d, width = w_ada.shape
    rows = lambda c: pl.BlockSpec((c.shape[0], d), lambda j: (0, 0))
    cols = lambda c: pl.BlockSpec((c.shape[0], block_n), lambda j: (0, j))
    return pl.pallas_call(
        _ada_kernel,
        grid=(width // block_n,),
        in_specs=[rows(c_prompt), rows(c_sample),
                  pl.BlockSpec((d, block_n), lambda j: (0, j)),
                  pl.BlockSpec((1, block_n), lambda j: (0, j))],
        out_specs=[cols(c_prompt), cols(c_sample)],
        out_shape=[jax.ShapeDtypeStruct((c.shape[0], width), F32) for c in (c_prompt, c_sample)],
        compiler_params=pltpu.CompilerParams(vmem_limit_bytes=VMEM_LIMIT),
        name="ada",
    )(c_prompt, c_sample, w_ada, b_ada.reshape(1, width))


def _mix_prompt_kernel(x_ref, mod_ref, norm1_ref, w_in_ref, cos_ref, sin_ref, dmat_ref, cross_ref,
                       tail_ref, cdec_ref, w_pool_ref, pool_scale_ref, w_out_ref, norm2_ref,
                       wr_t_ref, bias_t_ref,
                       x1_ref, h2_ref, experts_ref, gatew_ref, ret_ref, pool_ref,
                       state_ref, ext_ref, win_ref, o_ref, *, block_l, chunk, seqs):
    li = pl.program_id(1)

    @pl.when(li == 0)
    def _():
        state_ref[...] = jnp.zeros_like(state_ref)
        ext_ref[:, 0:POOL_CARRY, :] = jnp.zeros((seqs, POOL_CARRY, POOL_WIDTH), F32)
        win_ref[:, 0:SUBLANES, :] = jnp.zeros((seqs, SUBLANES, POOL_WIDTH), F32)

    for seq in range(seqs):
        _mix_prompt_seq(seq, li, x_ref, mod_ref, norm1_ref, w_in_ref, cos_ref, sin_ref, dmat_ref, cross_ref,
                        tail_ref, cdec_ref, w_pool_ref, pool_scale_ref, w_out_ref, norm2_ref,
                        wr_t_ref, bias_t_ref, x1_ref, h2_ref, experts_ref, gatew_ref,
                        state_ref.at[seq], ext_ref.at[seq], win_ref.at[seq], o_ref.at[seq],
                        block_l=block_l, chunk=chunk)

    @pl.when(li == pl.num_programs(1) - 1)
    def _():
        ret_ref[...] = state_ref[...].reshape(ret_ref.shape)
        pool_ref[...] = ext_ref[:, POOL_CARRY - POOL_BUF:POOL_CARRY, :]


def _window_sums(ext_ref, win_ref, block_l):
    g = POOL_GROUP_DIM
    top = POOL_CARRY + block_l
    new = slice(POOL_CARRY - SUBLANES, None)
    s2 = ext_ref[SUBLANES:top, :] + ext_ref[SUBLANES - 1:top - 1, :]
    win_ref[SUBLANES:top, g:] = s2[:, g:]
    s4 = s2[:, g:] + win_ref[SUBLANES - 2:top - 2, g:]
    win_ref[SUBLANES:top, 2 * g:] = s4[:, g:]
    s8 = s4[:, g:] + win_ref[SUBLANES - 4:top - 4, 2 * g:]
    win_ref[SUBLANES:top, 3 * g:] = s8[:, g:]
    s16 = s8[:, g:] + win_ref[0:top - SUBLANES, 3 * g:]
    return [s2[new, 0:g], s4[new, 0:g], s8[new, 0:g], s16[new, :]]


def _mix_prompt_seq(seq, li, x_ref, mod_ref, norm1_ref, w_in_ref, cos_ref, sin_ref, dmat_ref, cross_ref,
                    tail_ref, cdec_ref, w_pool_ref, pool_scale_ref, w_out_ref, norm2_ref,
                    wr_t_ref, bias_t_ref, x1_ref, h2_ref, experts_ref, gatew_ref,
                    state_ref, ext_ref, win_ref, o_ref, *, block_l, chunk):
    x = x_ref[seq]
    h16 = (_rms(x) * norm1_ref[...] * (1.0 + _mod(mod_ref, 1, seq)) + _mod(mod_ref, 0, seq)).astype(BF16)
    proj = _dot(h16, w_in_ref[...])
    q = proj[:, 0:QK_WIDTH]
    k = proj[:, QK_WIDTH:2 * QK_WIDTH]
    v = proj[:, 2 * QK_WIDTH:2 * QK_WIDTH + RET_WIDTH]
    g = proj[:, 2 * QK_WIDTH + RET_WIDTH:2 * QK_WIDTH + 2 * RET_WIDTH]
    u = proj[:, 2 * QK_WIDTH + 2 * RET_WIDTH:]

    lane = lax.broadcasted_iota(I32, q.shape, 1)
    first_half = (lane % RET_QK_DIM) < (RET_QK_DIM // 2)
    cos_t = cos_ref[...]
    sin_t = sin_ref[...]

    def rot(t):
        partner = jnp.where(first_half, pltpu.roll(t, QK_WIDTH - RET_QK_DIM // 2, axis=1),
                            pltpu.roll(t, RET_QK_DIM // 2, axis=1))
        return t * cos_t + partner * sin_t

    q = rot(q)
    k = rot(k) * (RET_QK_DIM ** -0.5)
    k_t = k.T
    v16 = v.astype(BF16)
    head_of_lane = lax.broadcasted_iota(I32, (chunk, QK_WIDTH), 1) // RET_QK_DIM

    for c in range(block_l // chunk):
        rows = slice(c * chunk, (c + 1) * chunk)
        q_c = q[rows]
        kt_c = k_t[:, rows]
        kt16 = kt_c.astype(BF16)
        state16 = state_ref[...].astype(BF16)
        for hd in range(RET_HEADS):
            in_head = head_of_lane == hd
            q_h = jnp.where(in_head, q_c, 0.0).astype(BF16)
            v_h = v16[rows, hd * RET_V_DIM:(hd + 1) * RET_V_DIM]
            scores = _dot(q_h, kt16) * dmat_ref[hd]
            inner = _dot(scores.astype(BF16), v_h)
            cross = _dot(q_h, state16) * cross_ref[hd]
            o_ref[rows, hd * RET_V_DIM:(hd + 1) * RET_V_DIM] = inner + cross
            hrows = slice(hd * RET_QK_DIM, (hd + 1) * RET_QK_DIM)
            k_dec = (kt_c[hrows] * tail_ref[hd:hd + 1, :]).astype(BF16)
            state_ref[hrows, :] = state_ref[hrows, :] * cdec_ref[hd] + _dot(k_dec, v_h)

    o_gated = _group_norm_gate(o_ref[...], g)

    ext_ref[POOL_CARRY:POOL_CARRY + block_l, :] = u
    pos = (li * block_l + lax.broadcasted_iota(I32, (block_l, 1), 0)).astype(F32)
    pooled = []
    for gi, (w, acc) in enumerate(zip(POOL_WINDOWS, _window_sums(ext_ref, win_ref, block_l))):
        cnt = jnp.minimum(pos + 1.0, float(w))
        pooled.append(acc / cnt - u[:, gi * POOL_GROUP_DIM:(gi + 1) * POOL_GROUP_DIM])
    p = _pool_project(pooled, w_pool_ref, pool_scale_ref)
    ext_ref[0:POOL_CARRY, :] = ext_ref[block_l:block_l + POOL_CARRY, :]

    x1 = _out_residual(x, o_gated, p, mod_ref, seq, w_out_ref)
    x1_ref[seq] = x1
    _norm_route(x1, mod_ref, seq, norm2_ref, wr_t_ref, bias_t_ref,
                h2_ref.at[seq], experts_ref.at[seq], gatew_ref.at[seq])


def _decay_tables(chunk):
    f32 = np.float32
    lg = np.log(f32(1.0) - f32(2.0) ** (f32(-5.0) - np.arange(RET_HEADS, dtype=f32))).astype(f32)
    idx = np.arange(chunk, dtype=f32)
    diff = idx[:, None] - idx[None, :]
    causal = diff >= 0
    dmat = np.where(causal[None], np.exp(lg[:, None, None] * np.where(causal, diff, f32(0.0))[None]), f32(0.0))
    cross = np.exp(lg[:, None] * (idx[None, :] + f32(1.0)))
    cross = np.broadcast_to(cross[:, :, None], (RET_HEADS, chunk, RET_V_DIM))
    tail = np.exp(lg[:, None] * (f32(chunk - 1.0) - idx)[None, :])
    cdec = np.broadcast_to(np.exp(lg * f32(chunk))[:, None, None], (RET_HEADS, RET_QK_DIM, RET_V_DIM))
    return tuple(jnp.asarray(np.ascontiguousarray(t, dtype=f32)) for t in (dmat, cross, tail, cdec))


def _rotary_angles(pos):
    half = RET_QK_DIM // 2
    freqs = (np.float32(ROPE_BASE) ** (-np.arange(half, dtype=np.float32) / np.float32(half))).astype(np.float32)
    return np.asarray(pos, np.float32)[:, None] * freqs[None, :]


def _rotary_tables(pos):
    ang = _rotary_angles(pos)
    cos, sin = np.cos(ang), np.sin(ang)
    cos_t = np.tile(np.concatenate([cos, cos], axis=-1), (1, RET_HEADS))
    sin_t = np.tile(np.concatenate([-sin, sin], axis=-1), (1, RET_HEADS))
    return jnp.asarray(cos_t, F32), jnp.asarray(sin_t, F32)


def _full(shape):
    return pl.BlockSpec(shape, lambda *_: (0,) * len(shape))


def _mix_prompt(x, mod, b0, b, norm1, w_in16, w_pool, pool_scale, w_out16, norm2, wr_t, bias_t,
                block_l=512, chunk=256, seqs=2):
    _, l, d = x.shape
    nl = l // block_l
    s0 = b0 // seqs
    cos_t, sin_t = _rotary_tables(np.arange(l))
    dmat, cross, tail, cdec = _decay_tables(chunk)
    kernel = functools.partial(_mix_prompt_kernel, block_l=block_l, chunk=chunk, seqs=seqs)
    tok = lambda bi, li: (bi, li, 0)
    per_seq = lambda bi, li: (bi, 0, 0)
    x1, h2, experts, gate_w, ret, pool = pl.pallas_call(
        kernel,
        grid=(b // seqs, nl),
        in_specs=[pl.BlockSpec((seqs, block_l, d), lambda bi, li: (s0 + bi, li, 0)),
                  pl.BlockSpec((seqs, 6, d), lambda bi, li: (s0 + bi, 0, 0)),
                  _full((1, d)),
                  _full((d, IN_WIDTH)),
                  pl.BlockSpec((block_l, QK_WIDTH), lambda bi, li: (li, 0)),
                  pl.BlockSpec((block_l, QK_WIDTH), lambda bi, li: (li, 0)),
                  _full(dmat.shape), _full(cross.shape), _full(tail.shape), _full(cdec.shape),
                  _full(w_pool.shape), _full((1, POOL_WIDTH)), _full((d, d)), _full((1, d)),
                  _full(wr_t.shape), _full(bias_t.shape)],
        out_specs=[pl.BlockSpec((seqs, block_l, d), tok),
                   pl.BlockSpec((seqs, block_l, HALF), tok),
                   pl.BlockSpec((seqs, TOP_K, block_l), lambda bi, li: (bi, 0, li)),
                   pl.BlockSpec((seqs, block_l, LANES), tok),
                   pl.BlockSpec((seqs, RET_HEADS, RET_QK_DIM, RET_V_DIM), lambda bi, li: (bi, 0, 0, 0)),
                   pl.BlockSpec((seqs, POOL_BUF, POOL_WIDTH), per_seq)],
        out_shape=[jax.ShapeDtypeStruct((b, l, d), F32),
                   jax.ShapeDtypeStruct((b, l, HALF), U32),
                   jax.ShapeDtypeStruct((b, TOP_K, l), I32),
                   jax.ShapeDtypeStruct((b, l, LANES), F32),
                   jax.ShapeDtypeStruct((b, RET_HEADS, RET_QK_DIM, RET_V_DIM), F32),
                   jax.ShapeDtypeStruct((b, POOL_BUF, POOL_WIDTH), F32)],
        scratch_shapes=[pltpu.VMEM((seqs, QK_WIDTH, RET_V_DIM), F32),
                        pltpu.VMEM((seqs, POOL_CARRY + block_l, POOL_WIDTH), F32),
                        pltpu.VMEM((seqs, POOL_CARRY + block_l, POOL_WIDTH), F32),
                        pltpu.VMEM((seqs, block_l, RET_WIDTH), F32)],
        compiler_params=pltpu.CompilerParams(dimension_semantics=("arbitrary", "arbitrary"),
                                             vmem_limit_bytes=VMEM_LIMIT),
        name="mix_prompt",
    )(x, mod, norm1, w_in16, cos_t, sin_t, dmat, cross, tail, cdec, w_pool, pool_scale,
      w_out16, norm2, wr_t, bias_t)
    experts = jnp.transpose(experts, (1, 0, 2)).reshape(TOP_K, b * l)
    return x1, h2.reshape(b * l, HALF), experts, gate_w.reshape(b * l, LANES), ret, pool


def _mix_sample_front_kernel(x_ref, mod_ref, norm1_ref, w_in_ref, cos_ref, sin_ref,
                             qt_ref, kt_ref, v_ref, g_ref, u_ref):
    x = x_ref[...]
    h = _rms(x) * norm1_ref[...] * (1.0 + _mod(mod_ref, 1)) + _mod(mod_ref, 0)
    proj = _dot(h.astype(BF16), w_in_ref[...])
    half = RET_QK_DIM // 2
    cos_c = cos_ref[...]
    sin_c = sin_ref[...]

    def rot_t(t):
        parts = []
        for hd in range(RET_HEADS):
            t1 = t[hd * RET_QK_DIM:hd * RET_QK_DIM + half]
            t2 = t[hd * RET_QK_DIM + half:(hd + 1) * RET_QK_DIM]
            parts += [t1 * cos_c - t2 * sin_c, t1 * sin_c + t2 * cos_c]
        return jnp.concatenate(parts, axis=0)

    qt_ref[...] = rot_t(proj[:, 0:QK_WIDTH].T)
    kt_ref[...] = rot_t(proj[:, QK_WIDTH:2 * QK_WIDTH].T) * (RET_QK_DIM ** -0.5)
    v_ref[...] = proj[:, 2 * QK_WIDTH:2 * QK_WIDTH + RET_WIDTH]
    g_ref[...] = proj[:, 2 * QK_WIDTH + RET_WIDTH:2 * QK_WIDTH + 2 * RET_WIDTH]
    u_ref[...] = proj[:, 2 * QK_WIDTH + 2 * RET_WIDTH:]


def _ret_step_kernel(qt_ref, kt_ref, v_ref, s0_ref, o_ref, s1_ref, *, block_b, decays):
    i = pl.program_id(0)
    lane = lax.broadcasted_iota(I32, qt_ref.shape, 1)
    for j in range(block_b):
        bi = i * block_b + j
        here = lane == bi
        q_col = jnp.sum(jnp.where(here, qt_ref[...], 0.0), axis=1, keepdims=True)
        k_col = jnp.sum(jnp.where(here, kt_ref[...], 0.0), axis=1, keepdims=True)
        v_row = v_ref[pl.ds(bi, 1), :]
        outs = []
        for hd in range(RET_HEADS):
            hrows = slice(hd * RET_QK_DIM, (hd + 1) * RET_QK_DIM)
            s1 = decays[hd] * s0_ref[j, hd] + k_col[hrows] * v_row[:, hd * RET_V_DIM:(hd + 1) * RET_V_DIM]
            s1_ref[j, hd] = s1
            outs.append(jnp.sum(q_col[hrows] * s1, axis=0, keepdims=True))
        o_ref[pl.ds(bi, 1), :] = jnp.concatenate(outs, axis=-1)


def _mix_sample_back_kernel(x_ref, mod_ref, o_ref, g_ref, u_ref, buf_ref, w_pool_ref, pool_scale_ref,
                            w_out_ref, norm2_ref, wr_t_ref, bias_t_ref,
                            x1_ref, h2_ref, experts_ref, gatew_ref, pool_ref):
    o_gated = _group_norm_gate(o_ref[...], g_ref[...])
    u = u_ref[...]
    pooled = []
    for gi, w in enumerate(POOL_WINDOWS):
        lanes = slice(gi * POOL_GROUP_DIM, (gi + 1) * POOL_GROUP_DIM)
        acc = u[:, lanes]
        for j in range(1, w):
            acc = acc + buf_ref[:, POOL_BUF - j, lanes]
        pooled.append(acc / float(w) - u[:, lanes])
    p = _pool_project(pooled, w_pool_ref, pool_scale_ref)
    pool_ref[:, 0:POOL_BUF - 1, :] = buf_ref[:, 1:POOL_BUF, :]
    pool_ref[:, POOL_BUF - 1, :] = u
    x1 = _out_residual(x_ref[...], o_gated, p, mod_ref, 0, w_out_ref)
    x1_ref[...] = x1
    _norm_route(x1, mod_ref, 0, norm2_ref, wr_t_ref, bias_t_ref, h2_ref, experts_ref, gatew_ref)


def _mix_sample(x, mod, state_ret, state_pool, start, norm1, w_in16, w_pool,
                pool_scale, w_out16, norm2, wr_t, bias_t, block_b=32):
    n, d = x.shape
    half = RET_QK_DIM // 2
    ang = _rotary_angles([start])
    cos_c = jnp.asarray(np.broadcast_to(np.cos(ang).T, (half, n)), F32)
    sin_c = jnp.asarray(np.broadcast_to(np.sin(ang).T, (half, n)), F32)
    params = pltpu.CompilerParams(vmem_limit_bytes=VMEM_LIMIT)
    qt, kt, v, g, u = pl.pallas_call(
        _mix_sample_front_kernel,
        out_shape=[jax.ShapeDtypeStruct((QK_WIDTH, n), F32), jax.ShapeDtypeStruct((QK_WIDTH, n), F32),
                   jax.ShapeDtypeStruct((n, RET_WIDTH), F32), jax.ShapeDtypeStruct((n, RET_WIDTH), F32),
                   jax.ShapeDtypeStruct((n, POOL_WIDTH), F32)],
        compiler_params=params,
        name="mix_sample_front",
    )(x, mod, norm1, w_in16, cos_c, sin_c)

    lg = np.log(1.0 - 2.0 ** (-5.0 - np.arange(RET_HEADS, dtype=np.float32)), dtype=np.float32)
    decays = tuple(float(np.exp(lg[h])) for h in range(RET_HEADS))
    state_block = (block_b, RET_HEADS, RET_QK_DIM, RET_V_DIM)
    o, s1 = pl.pallas_call(
        functools.partial(_ret_step_kernel, block_b=block_b, decays=decays),
        grid=(n // block_b,),
        in_specs=[_full((QK_WIDTH, n)), _full((QK_WIDTH, n)), _full((n, RET_WIDTH)),
                  pl.BlockSpec(state_block, lambda i: (i, 0, 0, 0))],
        out_specs=[_full((n, RET_WIDTH)), pl.BlockSpec(state_block, lambda i: (i, 0, 0, 0))],
        out_shape=[jax.ShapeDtypeStruct((n, RET_WIDTH), F32),
                   jax.ShapeDtypeStruct(state_ret.shape, F32)],
        compiler_params=pltpu.CompilerParams(dimension_semantics=("arbitrary",),
                                             vmem_limit_bytes=VMEM_LIMIT),
        name="ret_step",
    )(qt, kt, v, state_ret)

    x1, h2, experts, gate_w, pool = pl.pallas_call(
        _mix_sample_back_kernel,
        out_shape=[jax.ShapeDtypeStruct((n, d), F32),
                   jax.ShapeDtypeStruct((n, HALF), U32),
                   jax.ShapeDtypeStruct((TOP_K, n), I32),
                   jax.ShapeDtypeStruct((n, LANES), F32),
                   jax.ShapeDtypeStruct(state_pool.shape, F32)],
        compiler_params=params,
        name="mix_sample_back",
    )(x, mod, o, g, u, state_pool, w_pool, pool_scale, w_out16, norm2, wr_t, bias_t)
    return x1, h2, experts, gate_w, s1, pool


def _plan_kernel(experts_ref, pos_ref, meta_ref, cnt_ref, carry_ref, off_ref, *, block_t):
    phase = pl.program_id(0)
    j = pl.program_id(1)
    e_blk = experts_ref[...]
    eidx = lax.broadcasted_iota(I32, (N_EXPERTS, block_t), 0)
    member = jnp.zeros((N_EXPERTS, block_t), F32)
    for s in range(TOP_K):
        member = member + jnp.where(eidx == e_blk[s:s + 1, :], 1.0, 0.0)
    per_expert = jnp.broadcast_to(jnp.sum(member, axis=1, keepdims=True), (N_EXPERTS, LANES))

    @pl.when((phase == 0) & (j == 0))
    def _():
        cnt_ref[...] = jnp.zeros_like(cnt_ref)

    @pl.when(phase == 0)
    def _():
        cnt_ref[...] += per_expert

    @pl.when((phase == 0) & (j == pl.num_programs(1) - 1))
    def _():
        cnt = cnt_ref[...]
        n_tile = jnp.floor((cnt + (ROW_TILE - 1.0)) * (1.0 / ROW_TILE))
        upto = (lax.broadcasted_iota(I32, (N_EXPERTS, N_EXPERTS), 1)
                <= lax.broadcasted_iota(I32, (N_EXPERTS, N_EXPERTS), 0))
        tile_end = _dot(jnp.where(upto, 1.0, 0.0).astype(BF16), n_tile.astype(BF16))
        tile_start = tile_end - n_tile
        off_ref[...] = tile_start * ROW_TILE
        carry_ref[...] = jnp.zeros_like(carry_ref)
        lane = lax.broadcasted_iota(I32, cnt.shape, 1)
        meta_ref[...] = jnp.where(lane == 0, tile_start, jnp.where(lane == 1, n_tile, cnt)).astype(I32)

    @pl.when(phase == 1)
    def _():
        before = (lax.broadcasted_iota(I32, (block_t, block_t), 0)
                  < lax.broadcasted_iota(I32, (block_t, block_t), 1))
        rank = _dot(member.astype(BF16), jnp.where(before, 1.0, 0.0).astype(BF16))
        row = off_ref[:, 0:1] + carry_ref[:, 0:1] + rank
        carry_ref[...] += per_expert
        out = [jnp.sum(jnp.where(eidx == e_blk[s:s + 1, :], row, 0.0), axis=0, keepdims=True)
               for s in range(TOP_K)]
        pos_ref[...] = jnp.concatenate(out, axis=0).astype(I32)


def _plan(experts_all, max_block=1024):
    n_tokens = experts_all.shape[1]
    block_t = max(k for k in range(LANES, max_block + 1, LANES) if n_tokens % k == 0)
    nb = n_tokens // block_t
    return pl.pallas_call(
        functools.partial(_plan_kernel, block_t=block_t),
        grid=(2, nb),
        in_specs=[pl.BlockSpec((TOP_K, block_t), lambda ph, j: (0, j))],
        out_specs=[pl.BlockSpec((TOP_K, block_t), lambda ph, j: (0, j * ph)),
                   _full((N_EXPERTS, LANES))],
        out_shape=[jax.ShapeDtypeStruct((TOP_K, n_tokens), I32),
                   jax.ShapeDtypeStruct((N_EXPERTS, LANES), I32)],
        scratch_shapes=[pltpu.VMEM((N_EXPERTS, LANES), F32)] * 3,
        compiler_params=pltpu.CompilerParams(dimension_semantics=("arbitrary", "arbitrary"),
                                             vmem_limit_bytes=VMEM_LIMIT),
        name="plan",
    )(experts_all)


def _sc_workers():
    info = plsc.get_sparse_core_info()
    return info.num_cores, info.num_cores * info.num_subcores


def _sc_scatter_rows(sources, pos_t, n_out, after=()):
    w = sources[0].shape[1]
    s = pos_t.shape[0]
    n_cores, n_workers = _sc_workers()
    bounds = np.cumsum([0] + [src.shape[0] // SC_CHUNK for src in sources])
    n_chunks = int(bounds[-1])
    iters = -(-n_chunks // n_workers)
    mesh = plsc.VectorSubcoreMesh(core_axis_name="c", subcore_axis_name="s")

    @functools.partial(
        pl.kernel, mesh=mesh, out_type=jax.ShapeDtypeStruct((n_out, w), sources[0].dtype),
        scratch_types=[pltpu.VMEM((SC_CHUNK, w), sources[0].dtype), pltpu.VMEM((s, SC_CHUNK), I32),
                       pltpu.SemaphoreType.DMA],
        name="dispatch")
    def k(*refs):
        src_hbm, pos_hbm = refs[:len(sources)], refs[len(sources)]
        out_hbm, rows_v, idx_v, sem = refs[len(sources) + 1 + len(after):]
        wid = lax.axis_index("s") * n_cores + lax.axis_index("c")

        @pl.loop(0, iters)
        def _(it):
            c = it * n_workers + wid
            for src, lo, hi in zip(src_hbm, bounds[:-1], bounds[1:]):
                @pl.when((c >= int(lo)) & (c < int(hi)))
                def _():
                    base = pl.multiple_of((c - int(lo)) * SC_CHUNK, SC_CHUNK)
                    pltpu.sync_copy(src.at[pl.ds(base, SC_CHUNK)], rows_v)

            @pl.when(c < n_chunks)
            def _():
                base = pl.multiple_of(c * SC_CHUNK, SC_CHUNK)
                pltpu.sync_copy(pos_hbm.at[:, pl.ds(base, SC_CHUNK)], idx_v)
                copies = [pltpu.async_copy(rows_v, out_hbm.at[idx_v.at[j]], sem) for j in range(s)]
                for cp in copies:
                    cp.wait()

    return k(*sources, pos_t, *after)


def _sc_gather_rows(table, pos_t):
    _, w = table.shape
    s, t = pos_t.shape
    n_cores, n_workers = _sc_workers()
    n_chunks = t // SC_CHUNK
    iters = -(-n_chunks // n_workers)
    mesh = plsc.VectorSubcoreMesh(core_axis_name="c", subcore_axis_name="s")

    @functools.partial(
        pl.kernel, mesh=mesh, out_type=jax.ShapeDtypeStruct((s, t, w), table.dtype),
        scratch_types=[pltpu.VMEM((SC_CHUNK, w), table.dtype), pltpu.VMEM((s, SC_CHUNK), I32),
                       pltpu.SemaphoreType.DMA],
        name="combine")
    def k(table_hbm, pos_hbm, out_hbm, rows_v, idx_v, sem):
        wid = lax.axis_index("s") * n_cores + lax.axis_index("c")

        @pl.loop(0, iters)
        def _(it):
            c = it * n_workers + wid

            @pl.when(c < n_chunks)
            def _():
                base = pl.multiple_of(c * SC_CHUNK, SC_CHUNK)
                pltpu.sync_copy(pos_hbm.at[:, pl.ds(base, SC_CHUNK)], idx_v)
                for j in range(s):
                    pltpu.async_copy(table_hbm.at[idx_v.at[j]], rows_v, sem).wait()
                    pltpu.sync_copy(rows_v, out_hbm.at[j, pl.ds(base, SC_CHUNK)])

    return k(table, pos_t)


def _sc_pack_weights(w, rows_per_item, after=()):
    e, r, c_full = w.shape
    c = SC_LANES * SC_UNROLL
    seg = c_full // c
    rb = rows_per_item
    n_out = e * (r // 2) * seg
    n_cores, n_workers = _sc_workers()
    per_worker = n_out // (rb * n_workers)
    assert per_worker * rb * n_workers == n_out and per_worker % 2 == 0 and seg * c == c_full and rb % seg == 0
    mesh = plsc.VectorSubcoreMesh(core_axis_name="c", subcore_axis_name="s")

    @functools.partial(
        pl.kernel, mesh=mesh, out_type=jax.ShapeDtypeStruct((n_out, c), U32),
        scratch_types=[pltpu.VMEM((2, 2 * rb, c + SC_LANES), F32), pltpu.VMEM((2, rb, c), U32),
                       pltpu.SemaphoreType.DMA((2,)), pltpu.SemaphoreType.DMA((2,))],
        compiler_params=pltpu.CompilerParams(needs_layout_passes=False),
        cost_estimate=pl.CostEstimate(flops=e * r * c_full, transcendentals=0, bytes_accessed=6 * e * r * c_full),
        name="pack_weights")
    def k(w_hbm, *refs):
        out_hbm, in_v, out_v, in_sem, out_sem = refs[len(after):]
        wid = lax.axis_index("s") * n_cores + lax.axis_index("c")
        first = wid * per_worker

        def loads(item, b):
            row = pl.multiple_of(item * 2 * rb, 2 * rb)
            return [pltpu.make_async_copy(w_hbm.at[pl.ds(row + part * rb, rb)],
                                          in_v.at[b, pl.ds(part * rb, rb), pl.ds(0, c)], in_sem.at[b])
                    for part in range(2)]

        def store(item, b):
            return pltpu.make_async_copy(out_v.at[b], out_hbm.at[pl.ds(pl.multiple_of(item * rb, rb), rb)],
                                         out_sem.at[b])

        for cp in loads(first, 0):
            cp.start()

        @pl.loop(0, per_worker // 2)
        def _(pair):
            for b in range(2):
                item = first + pair * 2 + b
                for cp in loads(item, b):
                    cp.wait()

                @pl.when(item + 1 < first + per_worker)
                def _():
                    for cp in loads(item + 1, 1 - b):
                        cp.start()

                @pl.when(pair > 0)
                def _():
                    store(item - 2, b).wait()

                @pl.loop(0, rb // seg)
                def _(i):
                    for q in range(seg):
                        for u in range(SC_UNROLL):
                            sl = pl.ds(u * SC_LANES, SC_LANES)
                            packed = plsc.pack(in_v[b, 2 * i * seg + q, sl], in_v[b, (2 * i + 1) * seg + q, sl],
                                               format=plsc.PackFormat.INTERLEAVED)
                            out_v[b, i * seg + q, sl] = plsc.bitcast(packed, U32)

                store(item, b).start()

        for b in range(2):
            store(first + per_worker - 2 + b, b).wait()

    return k(w.reshape(e * r * seg, c), *after).reshape(e, r // 2, c_full)


def _experts_kernel(first_ref, ntile_ref, cnt_ref, xs_hbm, wg_ref, wu_ref, wd_ref, ys_hbm,
                    x_buf, y_buf, in_sem, out_sem):
    e = pl.program_id(0)
    n_used = first_ref[N_EXPERTS - 1] + ntile_ref[N_EXPERTS - 1]
    first, n_mine, count = first_ref[e], ntile_ref[e], cnt_ref[e]

    def tile_rows(g):
        return pl.ds(pl.multiple_of(g * ROW_TILE, ROW_TILE), ROW_TILE)

    def load(g):
        slot = lax.rem(g, STREAM_DEPTH)
        return pltpu.make_async_copy(xs_hbm.at[tile_rows(g)], x_buf.at[slot], in_sem.at[slot])

    def store(g):
        slot = lax.rem(g, STREAM_DEPTH)
        return pltpu.make_async_copy(y_buf.at[slot], ys_hbm.at[tile_rows(g)], out_sem.at[slot])

    @pl.when(e == 0)
    def _():
        for g0 in range(STREAM_DEPTH - 1):
            @pl.when(g0 < n_used)
            def _():
                load(g0).start()

    def weight(packed_ref):
        return pltpu.bitcast(packed_ref[0], BF16)

    def tile(j, carry):
        g = first + j
        slot = lax.rem(g, STREAM_DEPTH)
        load(g).wait()

        @pl.when(g + (STREAM_DEPTH - 1) < n_used)
        def _():
            load(g + (STREAM_DEPTH - 1)).start()

        @pl.when(g >= STREAM_DEPTH)
        def _():
            store(g - STREAM_DEPTH).wait()

        words = x_buf[slot]
        row = lax.broadcasted_iota(I32, words.shape, 0)
        words = jnp.where(row < count - j * ROW_TILE, words, jnp.uint32(0))
        lo, hi = _unpack_rows(words)
        lo, hi = lo.astype(BF16), hi.astype(BF16)
        wg, wu = weight(wg_ref), weight(wu_ref)
        hg = _dot(lo, wg[0:HALF]) + _dot(hi, wg[HALF:])
        hu = _dot(lo, wu[0:HALF]) + _dot(hi, wu[HALF:])
        a = (_silu(hg) * hu).astype(BF16)
        y_buf[slot] = _pack_rows(_dot(a, weight(wd_ref)))
        store(g).start()
        return carry

    lax.fori_loop(0, n_mine, tile, 0)

    @pl.when(e == N_EXPERTS - 1)
    def _():
        for back in range(STREAM_DEPTH, 0, -1):
            @pl.when(n_used >= back)
            def _():
                store(n_used - back).wait()


def _experts(xs, first_tile, n_tile, count, w_eg, w_eu, w_ed):
    d = D_MODEL
    by_expert = lambda e, *_: (e, 0, 0)
    grid_spec = pltpu.PrefetchScalarGridSpec(
        num_scalar_prefetch=3,
        grid=(N_EXPERTS,),
        in_specs=[pl.BlockSpec(memory_space=pl.ANY),
                  pl.BlockSpec((1, d // 2, EXPERT_DIM), by_expert),
                  pl.BlockSpec((1, d // 2, EXPERT_DIM), by_expert),
                  pl.BlockSpec((1, EXPERT_DIM // 2, d), by_expert)],
        out_specs=pl.BlockSpec(memory_space=pl.ANY),
        scratch_shapes=[pltpu.VMEM((STREAM_DEPTH, ROW_TILE, HALF), U32),
                        pltpu.VMEM((STREAM_DEPTH, ROW_TILE, HALF), U32),
                        pltpu.SemaphoreType.DMA((STREAM_DEPTH,)), pltpu.SemaphoreType.DMA((STREAM_DEPTH,))])
    return pl.pallas_call(
        _experts_kernel,
        grid_spec=grid_spec,
        out_shape=jax.ShapeDtypeStruct(xs.shape, U32),
        compiler_params=pltpu.CompilerParams(dimension_semantics=("arbitrary",),
                                             vmem_limit_bytes=VMEM_LIMIT),
        name="experts",
    )(first_tile, n_tile, count, xs, w_eg, w_eu, w_ed)


def _final_kernel(z_ref, gatew_ref, h2_ref, x1_ref, mod_ref, normf_ref, wsg_ref, wsu_ref, wsd_ref, *rest):
    y_ref, wsg16_ref, wsu16_ref, wsd16_ref = rest[-4:]

    @pl.when(pl.program_id(0) == 0)
    def _():
        wsg16_ref[...] = wsg_ref[...].astype(BF16)
        wsu16_ref[...] = wsu_ref[...].astype(BF16)
        wsd16_ref[...] = wsd_ref[...].astype(BF16)

    lo, hi = _unpack_rows(h2_ref[...])
    h = jnp.concatenate([lo, hi], axis=-1).astype(BF16)
    a = _silu(_dot(h, wsg16_ref[...])) * _dot(h, wsu16_ref[...])
    acc = _dot(a.astype(BF16), wsd16_ref[...])
    for s in range(TOP_K):
        lo, hi = _unpack_rows(z_ref[s])
        acc = acc + gatew_ref[:, s:s + 1] * jnp.concatenate([lo, hi], axis=-1)
    x2 = x1_ref[...] + _mod(mod_ref, 5) * acc
    y_ref[...] = _rms(x2) * normf_ref[...]


def _final(z, gate_w, h2, x1, mod, norm_f, w_sg, w_su, w_sd, block_t, first_block, per_seq,
           seq0=0, out_rows=None, y_prev=None):
    t, d = x1.shape
    out_rows = t if out_rows is None else out_rows
    out_first = seq0 * per_seq
    tok = lambda i: (i, 0)
    if per_seq:
        mod_spec = pl.BlockSpec((1, 6, d), lambda i: (seq0 + i // per_seq, 0, 0))
    else:
        mod_spec = pl.BlockSpec((block_t, 6 * d), tok)
    operands = [z, gate_w, h2, x1, mod, norm_f, w_sg, w_su, w_sd]
    in_specs = [pl.BlockSpec((TOP_K, block_t, HALF), lambda i: (0, first_block + i, 0)),
                pl.BlockSpec((block_t, LANES), tok),
                pl.BlockSpec((block_t, HALF), tok),
                pl.BlockSpec((block_t, d), tok),
                mod_spec,
                _full((1, d)),
                _full((d, EXPERT_DIM)), _full((d, EXPERT_DIM)), _full((EXPERT_DIM, d))]
    aliases = {}
    if y_prev is not None:
        aliases = {len(operands): 0}
        operands.append(y_prev)
        in_specs.append(pl.BlockSpec(memory_space=pl.ANY))
    return pl.pallas_call(
        _final_kernel,
        grid=(t // block_t,),
        in_specs=in_specs,
        out_specs=pl.BlockSpec((block_t, d), lambda i: (out_first + i, 0)),
        out_shape=jax.ShapeDtypeStruct((out_rows, d), F32),
        scratch_shapes=[pltpu.VMEM((d, EXPERT_DIM), BF16), pltpu.VMEM((d, EXPERT_DIM), BF16),
                        pltpu.VMEM((EXPERT_DIM, d), BF16)],
        input_output_aliases=aliases,
        compiler_params=pltpu.CompilerParams(dimension_semantics=("arbitrary",),
                                             vmem_limit_bytes=VMEM_LIMIT),
        name="final",
    )(*operands)


def kernel(x_prompt, x_sample, c_prompt, c_sample, state_ret, state_pool, norm1, norm2, norm_f,
           w_ada, b_ada, w_in, w_out, w_pool, pool_scale, w_router, router_bias, w_exp_gate,
           w_exp_up, w_exp_down, w_sh_gate, w_sh_up, w_sh_down):
    b, l, d = x_prompt.shape
    n = x_sample.shape[0]
    past_len = 16384

    mod_p, mod_s = _ada(c_prompt, c_sample, w_ada[0], b_ada[0])
    mod_p = mod_p.reshape(b, 6, d)

    w_in16 = w_in[0].astype(BF16)
    w_out16 = w_out[0].astype(BF16)
    wr_t = w_router[0].T
    bias_t = jnp.broadcast_to(router_bias[0][:, None], (N_EXPERTS, LANES))
    n1, n2, nf = norm1[0].reshape(1, d), norm2[0].reshape(1, d), norm_f.reshape(1, d)
    ps = pool_scale[0].reshape(1, POOL_WIDTH)
    shared = (w_sh_gate[0], w_sh_up[0], w_sh_down[0])

    def routed(sources, experts):
        n_tiles = experts.shape[1] * TOP_K // ROW_TILE + N_EXPERTS
        pos_t, meta = _plan(experts)
        xs = _sc_scatter_rows(sources, pos_t, n_tiles * ROW_TILE, after=expert_w)
        ys = _experts(xs, meta[:, 0], meta[:, 1], meta[:, 2], *expert_w)
        return _sc_gather_rows(ys, pos_t)

    mix_args = (n1, w_in16, w_pool[0], ps, w_out16, n2, wr_t, bias_t)
    x1_s, h2_s, experts_s, gatew_s, ret_s, pool_s = _mix_sample(
        x_sample.reshape(n, d), mod_s, state_ret[0], state_pool[0], float(past_len), *mix_args)

    expert_w = (_sc_pack_weights(w_exp_gate[0], 64, after=(x1_s,)),
                _sc_pack_weights(w_exp_up[0], 64, after=(x1_s,)),
                _sc_pack_weights(w_exp_down[0], 64, after=(x1_s,)))

    ba = b // 2
    bb = b - ba
    x1_a, h2_a, experts_a, gatew_a, ret_a, pool_a = _mix_prompt(x_prompt, mod_p, 0, ba, *mix_args)
    z_a = routed((h2_a,), experts_a)
    x1_b, h2_b, experts_b, gatew_b, ret_b, pool_b = _mix_prompt(x_prompt, mod_p, ba, bb, *mix_args)
    z_b = routed((h2_b, h2_s), jnp.concatenate([experts_b, experts_s], axis=1))

    block_t = 512
    per_seq = l // block_t
    y_s = _final(z_b, gatew_s, h2_s, x1_s, mod_s, nf, *shared,
                 block_t=n, first_block=bb * l // n, per_seq=0)
    y_p = _final(z_b, gatew_b, h2_b, x1_b.reshape(bb * l, d), mod_p, nf, *shared,
                 block_t=block_t, first_block=0, per_seq=per_seq, seq0=ba, out_rows=b * l)
    y_p = _final(z_a, gatew_a, h2_a, x1_a.reshape(ba * l, d), mod_p, nf, *shared,
                 block_t=block_t, first_block=0, per_seq=per_seq, seq0=0, out_rows=b * l, y_prev=y_p)

    ret_p = jnp.concatenate([ret_a, ret_b], axis=0)
    pool_p = jnp.concatenate([pool_a, pool_b], axis=0)
    return (y_p.reshape(b, l, d), y_s.reshape(n, 1, d), ret_p[None], pool_p[None],
            ret_s[None], pool_s[None])
```

```python
import functools

import jax
import jax.numpy as jnp
import numpy as np
from jax import lax
from jax.experimental import pallas as pl
from jax.experimental.pallas import tpu as pltpu
from jax.experimental.pallas import tpu_sc as plsc

D_MODEL = 1024
RET_HEADS = 4
RET_QK_DIM = 64
RET_V_DIM = 128
RET_WIDTH = RET_HEADS * RET_V_DIM
QK_WIDTH = RET_HEADS * RET_QK_DIM
ROPE_BASE = 10000.0
POOL_WINDOWS = (2, 4, 8, 16)
POOL_WIDTH = 512
POOL_GROUP_DIM = 128
POOL_BUF = 15
IN_WIDTH = 2 * QK_WIDTH + 2 * RET_WIDTH + POOL_WIDTH
N_EXPERTS = 64
TOP_K = 8
N_EXPERT_GROUPS = 8
GROUP_SIZE = N_EXPERTS // N_EXPERT_GROUPS
TOP_GROUPS = 4
EXPERT_DIM = 256
ROUTE_SCALE = 2.5
EPS = 1e-6

LANES = 128
SUBLANES = 8
POOL_CARRY = 24
VMEM_LIMIT = 56 * 1024 * 1024
HALF = D_MODEL // 2
ROW_TILE = 128
MAX_WIDTH = 4
SC_CHUNK = 128
SC_LANES = 16
SC_UNROLL = 16
STREAM_DEPTH = 32

BF16 = jnp.bfloat16
F32 = jnp.float32
U32 = jnp.uint32
I32 = jnp.int32


def _silu(x):
    return x * jax.nn.sigmoid(x)


def _dot(a, b):
    return jnp.dot(a, b, preferred_element_type=F32)


def _rms(x):
    return x * lax.rsqrt(jnp.mean(x * x, axis=-1, keepdims=True) + EPS)


def _mod(mod_ref, i, seq=0):
    if len(mod_ref.shape) == 3:
        return mod_ref[seq, i:i + 1, :]
    return mod_ref[:, i * D_MODEL:(i + 1) * D_MODEL]


def _split_bf16(x):
    hi = x.astype(BF16)
    lo = (x - hi.astype(F32)).astype(BF16)
    return hi, lo


def _pack_rows(x):
    lo = lax.bitcast_convert_type(x[:, :HALF].astype(BF16).astype(F32), U32)
    hi = lax.bitcast_convert_type(x[:, HALF:].astype(BF16).astype(F32), U32)
    return (hi & jnp.uint32(0xFFFF0000)) | (lo >> jnp.uint32(16))


def _unpack_rows(w):
    lo = lax.bitcast_convert_type(w << jnp.uint32(16), F32)
    hi = lax.bitcast_convert_type(w & jnp.uint32(0xFFFF0000), F32)
    return lo, hi


def _first_max_onehot(work, idx, n):
    m = jnp.max(work, axis=0, keepdims=True)
    first = jnp.min(jnp.where(work == m, idx, float(n)), axis=0, keepdims=True)
    return idx == first


def _route(h2, wr_t_ref, bias_t_ref):
    n = h2.shape[0]
    h_hi, h_lo = _split_bf16(h2)
    w_hi, w_lo = _split_bf16(wr_t_ref[...])
    nt = (((1,), (1,)), ((), ()))
    logits = (lax.dot_general(w_hi, h_hi, nt, preferred_element_type=F32)
              + lax.dot_general(w_hi, h_lo, nt, preferred_element_type=F32)
              + lax.dot_general(w_lo, h_hi, nt, preferred_element_type=F32))
    scores = jax.nn.sigmoid(logits)
    biased = scores + bias_t_ref[:, 0:1]
    b3 = biased.reshape(N_EXPERT_GROUPS, GROUP_SIZE, n)
    i3 = lax.broadcasted_iota(I32, b3.shape, 1).astype(F32)
    m1 = jnp.max(b3, axis=1, keepdims=True)
    first = jnp.min(jnp.where(b3 == m1, i3, float(GROUP_SIZE)), axis=1, keepdims=True)
    m2 = jnp.max(jnp.where(i3 == first, -jnp.inf, b3), axis=1, keepdims=True)
    gscore = (m1 + m2).reshape(N_EXPERT_GROUPS, n)
    gidx = lax.broadcasted_iota(I32, gscore.shape, 0).astype(F32)
    gsel = jnp.zeros(gscore.shape, F32)
    work = gscore
    for _ in range(TOP_GROUPS):
        hit = _first_max_onehot(work, gidx, N_EXPERT_GROUPS)
        gsel = jnp.where(hit, 1.0, gsel)
        work = jnp.where(hit, -jnp.inf, work)
    gsel3 = jnp.broadcast_to(gsel.reshape(N_EXPERT_GROUPS, 1, n), b3.shape)
    work = jnp.where(gsel3 > 0.0, b3, -jnp.inf).reshape(N_EXPERTS, n)
    eidx = lax.broadcasted_iota(I32, work.shape, 0).astype(F32)
    sel = jnp.zeros(work.shape, F32)
    for _ in range(TOP_K):
        hit = _first_max_onehot(work, eidx, N_EXPERTS)
        sel = jnp.where(hit, 1.0, sel)
        work = jnp.where(hit, -jnp.inf, work)
    picked = jnp.where(sel > 0.0, scores, 0.0)
    gates = picked / jnp.sum(picked, axis=0, keepdims=True) * ROUTE_SCALE
    below = (lax.broadcasted_iota(I32, (N_EXPERTS, N_EXPERTS), 1)
             < lax.broadcasted_iota(I32, (N_EXPERTS, N_EXPERTS), 0))
    slot = _dot(jnp.where(below, 1.0, 0.0).astype(BF16), sel.astype(BF16))
    e_rows, w_rows = [], []
    for s in range(TOP_K):
        here = jnp.where(slot == float(s), sel, 0.0)
        e_rows.append(jnp.sum(here * eidx, axis=0, keepdims=True))
        w_rows.append(jnp.sum(here * gates, axis=0, keepdims=True))
    experts = jnp.concatenate(e_rows, axis=0).astype(I32)
    w_t = jnp.concatenate(w_rows + [jnp.zeros((LANES - TOP_K, n), F32)], axis=0)
    return experts, w_t.T


def _group_norm_gate(o, g):
    parts = []
    for h in range(RET_HEADS):
        oh = o[:, h * RET_V_DIM:(h + 1) * RET_V_DIM]
        mu = jnp.mean(oh, axis=-1, keepdims=True)
        ctr = oh - mu
        var = jnp.mean(ctr * ctr, axis=-1, keepdims=True)
        parts.append(ctr * lax.rsqrt(var + EPS))
    return _silu(g) * jnp.concatenate(parts, axis=-1)


def _pool_project(pooled, w_pool_ref, pool_scale_ref):
    parts = [_dot(p.astype(BF16), w_pool_ref[gi].astype(BF16)) for gi, p in enumerate(pooled)]
    return jnp.concatenate(parts, axis=-1) * pool_scale_ref[...]


def _out_residual(x, o_gated, p, mod_ref, seq, w_out_ref):
    mix = jnp.concatenate([o_gated, p], axis=-1).astype(BF16)
    return x + _mod(mod_ref, 2, seq) * _dot(mix, w_out_ref[...])


def _norm_route(x1, mod_ref, seq, norm2_ref, wr_t_ref, bias_t_ref, h2_ref, experts_ref, gatew_ref):
    h2 = _rms(x1) * norm2_ref[...] * (1.0 + _mod(mod_ref, 4, seq)) + _mod(mod_ref, 3, seq)
    h2_ref[...] = _pack_rows(h2)
    experts, gate_w = _route(h2, wr_t_ref, bias_t_ref)
    experts_ref[...] = experts
    gatew_ref[...] = gate_w


def _ada_kernel(cp_ref, cs_ref, w_ref, b_ref, op_ref, os_ref):
    w16 = w_ref[...].astype(BF16)
    for c_ref, o_ref in ((cp_ref, op_ref), (cs_ref, os_ref)):
        o_ref[...] = _dot(_silu(c_ref[...]).astype(BF16), w16) + b_ref[...]


def _ada(c_prompt, c_sample, w_ada, b_ada, block_n=1536):
    d, width = w_ada.shape
    rows = lambda c: pl.BlockSpec((c.shape[0], d), lambda j: (0, 0))
    cols = lambda c: pl.BlockSpec((c.shape[0], block_n), lambda j: (0, j))
    return pl.pallas_call(
        _ada_kernel,
        grid=(width // block_n,),
        in_specs=[rows(c_prompt), rows(c_sample),
                  pl.BlockSpec((d, block_n), lambda j: (0, j)),
                  pl.BlockSpec((1, block_n), lambda j: (0, j))],
        out_specs=[cols(c_prompt), cols(c_sample)],
        out_shape=[jax.ShapeDtypeStruct((c.shape[0], width), F32) for c in (c_prompt, c_sample)],
        compiler_params=pltpu.CompilerParams(vmem_limit_bytes=VMEM_LIMIT),
        name="ada",
    )(c_prompt, c_sample, w_ada, b_ada.reshape(1, width))


def _mix_prompt_kernel(x_ref, mod_ref, norm1_ref, w_in_ref, cos_ref, sin_ref, dmat_ref, cross_ref,
                       tail_ref, cdec_ref, w_pool_ref, pool_scale_ref, w_out_ref, norm2_ref,
                       wr_t_ref, bias_t_ref,
                       x1_ref, h2_ref, experts_ref, gatew_ref, ret_ref, pool_ref,
                       state_ref, ext_ref, win_ref, o_ref, *, block_l, chunk, seqs):
    li = pl.program_id(1)

    @pl.when(li == 0)
    def _():
        state_ref[...] = jnp.zeros_like(state_ref)
        ext_ref[:, 0:POOL_CARRY, :] = jnp.zeros((seqs, POOL_CARRY, POOL_WIDTH), F32)
        win_ref[:, 0:SUBLANES, :] = jnp.zeros((seqs, SUBLANES, POOL_WIDTH), F32)

    for seq in range(seqs):
        _mix_prompt_seq(seq, li, x_ref, mod_ref, norm1_ref, w_in_ref, cos_ref, sin_ref, dmat_ref, cross_ref,
                        tail_ref, cdec_ref, w_pool_ref, pool_scale_ref, w_out_ref, norm2_ref,
                        wr_t_ref, bias_t_ref, x1_ref, h2_ref, experts_ref, gatew_ref,
                        state_ref.at[seq], ext_ref.at[seq], win_ref.at[seq], o_ref.at[seq],
                        block_l=block_l, chunk=chunk)

    @pl.when(li == pl.num_programs(1) - 1)
    def _():
        ret_ref[...] = state_ref[...].reshape(ret_ref.shape)
        pool_ref[...] = ext_ref[:, POOL_CARRY - POOL_BUF:POOL_CARRY, :]


def _window_sums(ext_ref, win_ref, block_l):
    g = POOL_GROUP_DIM
    top = POOL_CARRY + block_l
    new = slice(POOL_CARRY - SUBLANES, None)
    s2 = ext_ref[SUBLANES:top, :] + ext_ref[SUBLANES - 1:top - 1, :]
    win_ref[SUBLANES:top, g:] = s2[:, g:]
    s4 = s2[:, g:] + win_ref[SUBLANES - 2:top - 2, g:]
    win_ref[SUBLANES:top, 2 * g:] = s4[:, g:]
    s8 = s4[:, g:] + win_ref[SUBLANES - 4:top - 4, 2 * g:]
    win_ref[SUBLANES:top, 3 * g:] = s8[:, g:]
    s16 = s8[:, g:] + win_ref[0:top - SUBLANES, 3 * g:]
    return [s2[new, 0:g], s4[new, 0:g], s8[new, 0:g], s16[new, :]]


def _mix_prompt_seq(seq, li, x_ref, mod_ref, norm1_ref, w_in_ref, cos_ref, sin_ref, dmat_ref, cross_ref,
                    tail_ref, cdec_ref, w_pool_ref, pool_scale_ref, w_out_ref, norm2_ref,
                    wr_t_ref, bias_t_ref, x1_ref, h2_ref, experts_ref, gatew_ref,
                    state_ref, ext_ref, win_ref, o_ref, *, block_l, chunk):
    x = x_ref[seq]
    h16 = (_rms(x) * norm1_ref[...] * (1.0 + _mod(mod_ref, 1, seq)) + _mod(mod_ref, 0, seq)).astype(BF16)
    proj = _dot(h16, w_in_ref[...])
    q = proj[:, 0:QK_WIDTH]
    k = proj[:, QK_WIDTH:2 * QK_WIDTH]
    v = proj[:, 2 * QK_WIDTH:2 * QK_WIDTH + RET_WIDTH]
    g = proj[:, 2 * QK_WIDTH + RET_WIDTH:2 * QK_WIDTH + 2 * RET_WIDTH]
    u = proj[:, 2 * QK_WIDTH + 2 * RET_WIDTH:]

    lane = lax.broadcasted_iota(I32, q.shape, 1)
    first_half = (lane % RET_QK_DIM) < (RET_QK_DIM // 2)
    cos_t = cos_ref[...]
    sin_t = sin_ref[...]

    def rot(t):
        partner = jnp.where(first_half, pltpu.roll(t, QK_WIDTH - RET_QK_DIM // 2, axis=1),
                            pltpu.roll(t, RET_QK_DIM // 2, axis=1))
        return t * cos_t + partner * sin_t

    q = rot(q)
    k = rot(k) * (RET_QK_DIM ** -0.5)
    k_t = k.T
    v16 = v.astype(BF16)
    head_of_lane = lax.broadcasted_iota(I32, (chunk, QK_WIDTH), 1) // RET_QK_DIM

    for c in range(block_l // chunk):
        rows = slice(c * chunk, (c + 1) * chunk)
        q_c = q[rows]
        kt_c = k_t[:, rows]
        kt16 = kt_c.astype(BF16)
        state16 = state_ref[...].astype(BF16)
        for hd in range(RET_HEADS):
            in_head = head_of_lane == hd
            q_h = jnp.where(in_head, q_c, 0.0).astype(BF16)
            v_h = v16[rows, hd * RET_V_DIM:(hd + 1) * RET_V_DIM]
            scores = _dot(q_h, kt16) * dmat_ref[hd]
            inner = _dot(scores.astype(BF16), v_h)
            cross = _dot(q_h, state16) * cross_ref[hd]
            o_ref[rows, hd * RET_V_DIM:(hd + 1) * RET_V_DIM] = inner + cross
            hrows = slice(hd * RET_QK_DIM, (hd + 1) * RET_QK_DIM)
            k_dec = (kt_c[hrows] * tail_ref[hd:hd + 1, :]).astype(BF16)
            state_ref[hrows, :] = state_ref[hrows, :] * cdec_ref[hd] + _dot(k_dec, v_h)

    o_gated = _group_norm_gate(o_ref[...], g)

    ext_ref[POOL_CARRY:POOL_CARRY + block_l, :] = u
    pos = (li * block_l + lax.broadcasted_iota(I32, (block_l, 1), 0)).astype(F32)
    pooled = []
    for gi, (w, acc) in enumerate(zip(POOL_WINDOWS, _window_sums(ext_ref, win_ref, block_l))):
        cnt = jnp.minimum(pos + 1.0, float(w))
        pooled.append(acc / cnt - u[:, gi * POOL_GROUP_DIM:(gi + 1) * POOL_GROUP_DIM])
    p = _pool_project(pooled, w_pool_ref, pool_scale_ref)
    ext_ref[0:POOL_CARRY, :] = ext_ref[block_l:block_l + POOL_CARRY, :]

    x1 = _out_residual(x, o_gated, p, mod_ref, seq, w_out_ref)
    x1_ref[seq] = x1
    _norm_route(x1, mod_ref, seq, norm2_ref, wr_t_ref, bias_t_ref,
                h2_ref.at[seq], experts_ref.at[seq], gatew_ref.at[seq])


def _decay_tables(chunk):
    f32 = np.float32
    lg = np.log(f32(1.0) - f32(2.0) ** (f32(-5.0) - np.arange(RET_HEADS, dtype=f32))).astype(f32)
    idx = np.arange(chunk, dtype=f32)
    diff = idx[:, None] - idx[None, :]
    causal = diff >= 0
    dmat = np.where(causal[None], np.exp(lg[:, None, None] * np.where(causal, diff, f32(0.0))[None]), f32(0.0))
    cross = np.exp(lg[:, None] * (idx[None, :] + f32(1.0)))
    cross = np.broadcast_to(cross[:, :, None], (RET_HEADS, chunk, RET_V_DIM))
    tail = np.exp(lg[:, None] * (f32(chunk - 1.0) - idx)[None, :])
    cdec = np.broadcast_to(np.exp(lg * f32(chunk))[:, None, None], (RET_HEADS, RET_QK_DIM, RET_V_DIM))
    return tuple(jnp.asarray(np.ascontiguousarray(t, dtype=f32)) for t in (dmat, cross, tail, cdec))


def _rotary_angles(pos):
    half = RET_QK_DIM // 2
    freqs = (np.float32(ROPE_BASE) ** (-np.arange(half, dtype=np.float32) / np.float32(half))).astype(np.float32)
    return np.asarray(pos, np.float32)[:, None] * freqs[None, :]


def _rotary_tables(pos):
    ang = _rotary_angles(pos)
    cos, sin = np.cos(ang), np.sin(ang)
    cos_t = np.tile(np.concatenate([cos, cos], axis=-1), (1, RET_HEADS))
    sin_t = np.tile(np.concatenate([-sin, sin], axis=-1), (1, RET_HEADS))
    return jnp.asarray(cos_t, F32), jnp.asarray(sin_t, F32)


def _full(shape):
    return pl.BlockSpec(shape, lambda *_: (0,) * len(shape))


def _mix_prompt(x, mod, b0, b, norm1, w_in16, w_pool, pool_scale, w_out16, norm2, wr_t, bias_t,
                block_l=512, chunk=256, seqs=2):
    _, l, d = x.shape
    nl = l // block_l
    s0 = b0 // seqs
    cos_t, sin_t = _rotary_tables(np.arange(l))
    dmat, cross, tail, cdec = _decay_tables(chunk)
    kernel = functools.partial(_mix_prompt_kernel, block_l=block_l, chunk=chunk, seqs=seqs)
    tok = lambda bi, li: (bi, li, 0)
    per_seq = lambda bi, li: (bi, 0, 0)
    x1, h2, experts, gate_w, ret, pool = pl.pallas_call(
        kernel,
        grid=(b // seqs, nl),
        in_specs=[pl.BlockSpec((seqs, block_l, d), lambda bi, li: (s0 + bi, li, 0)),
                  pl.BlockSpec((seqs, 6, d), lambda bi, li: (s0 + bi, 0, 0)),
                  _full((1, d)),
                  _full((d, IN_WIDTH)),
                  pl.BlockSpec((block_l, QK_WIDTH), lambda bi, li: (li, 0)),
                  pl.BlockSpec((block_l, QK_WIDTH), lambda bi, li: (li, 0)),
                  _full(dmat.shape), _full(cross.shape), _full(tail.shape), _full(cdec.shape),
                  _full(w_pool.shape), _full((1, POOL_WIDTH)), _full((d, d)), _full((1, d)),
                  _full(wr_t.shape), _full(bias_t.shape)],
        out_specs=[pl.BlockSpec((seqs, block_l, d), tok),
                   pl.BlockSpec((seqs, block_l, HALF), tok),
                   pl.BlockSpec((seqs, TOP_K, block_l), lambda bi, li: (bi, 0, li)),
                   pl.BlockSpec((seqs, block_l, LANES), tok),
                   pl.BlockSpec((seqs, RET_HEADS, RET_QK_DIM, RET_V_DIM), lambda bi, li: (bi, 0, 0, 0)),
                   pl.BlockSpec((seqs, POOL_BUF, POOL_WIDTH), per_seq)],
        out_shape=[jax.ShapeDtypeStruct((b, l, d), F32),
                   jax.ShapeDtypeStruct((b, l, HALF), U32),
                   jax.ShapeDtypeStruct((b, TOP_K, l), I32),
                   jax.ShapeDtypeStruct((b, l, LANES), F32),
                   jax.ShapeDtypeStruct((b, RET_HEADS, RET_QK_DIM, RET_V_DIM), F32),
                   jax.ShapeDtypeStruct((b, POOL_BUF, POOL_WIDTH), F32)],
        scratch_shapes=[pltpu.VMEM((seqs, QK_WIDTH, RET_V_DIM), F32),
                        pltpu.VMEM((seqs, POOL_CARRY + block_l, POOL_WIDTH), F32),
                        pltpu.VMEM((seqs, POOL_CARRY + block_l, POOL_WIDTH), F32),
                        pltpu.VMEM((seqs, block_l, RET_WIDTH), F32)],
        compiler_params=pltpu.CompilerParams(dimension_semantics=("arbitrary", "arbitrary"),
                                             vmem_limit_bytes=VMEM_LIMIT),
        name="mix_prompt",
    )(x, mod, norm1, w_in16, cos_t, sin_t, dmat, cross, tail, cdec, w_pool, pool_scale,
      w_out16, norm2, wr_t, bias_t)
    experts = jnp.transpose(experts, (1, 0, 2)).reshape(TOP_K, b * l)
    return x1, h2.reshape(b * l, HALF), experts, gate_w.reshape(b * l, LANES), ret, pool


def _mix_sample_front_kernel(x_ref, mod_ref, norm1_ref, w_in_ref, cos_ref, sin_ref,
                             qt_ref, kt_ref, v_ref, g_ref, u_ref):
    x = x_ref[...]
    h = _rms(x) * norm1_ref[...] * (1.0 + _mod(mod_ref, 1)) + _mod(mod_ref, 0)
    proj = _dot(h.astype(BF16), w_in_ref[...])
    half = RET_QK_DIM // 2
    cos_c = cos_ref[...]
    sin_c = sin_ref[...]

    def rot_t(t):
        parts = []
        for hd in range(RET_HEADS):
            t1 = t[hd * RET_QK_DIM:hd * RET_QK_DIM + half]
            t2 = t[hd * RET_QK_DIM + half:(hd + 1) * RET_QK_DIM]
            parts += [t1 * cos_c - t2 * sin_c, t1 * sin_c + t2 * cos_c]
        return jnp.concatenate(parts, axis=0)

    qt_ref[...] = rot_t(proj[:, 0:QK_WIDTH].T)
    kt_ref[...] = rot_t(proj[:, QK_WIDTH:2 * QK_WIDTH].T) * (RET_QK_DIM ** -0.5)
    v_ref[...] = proj[:, 2 * QK_WIDTH:2 * QK_WIDTH + RET_WIDTH]
    g_ref[...] = proj[:, 2 * QK_WIDTH + RET_WIDTH:2 * QK_WIDTH + 2 * RET_WIDTH]
    u_ref[...] = proj[:, 2 * QK_WIDTH + 2 * RET_WIDTH:]


def _ret_step_kernel(qt_ref, kt_ref, v_ref, s0_ref, o_ref, s1_ref, *, block_b, decays):
    i = pl.program_id(0)
    lane = lax.broadcasted_iota(I32, qt_ref.shape, 1)
    for j in range(block_b):
        bi = i * block_b + j
        here = lane == bi
        q_col = jnp.sum(jnp.where(here, qt_ref[...], 0.0), axis=1, keepdims=True)
        k_col = jnp.sum(jnp.where(here, kt_ref[...], 0.0), axis=1, keepdims=True)
        v_row = v_ref[pl.ds(bi, 1), :]
        outs = []
        for hd in range(RET_HEADS):
            hrows = slice(hd * RET_QK_DIM, (hd + 1) * RET_QK_DIM)
            s1 = decays[hd] * s0_ref[j, hd] + k_col[hrows] * v_row[:, hd * RET_V_DIM:(hd + 1) * RET_V_DIM]
            s1_ref[j, hd] = s1
            outs.append(jnp.sum(q_col[hrows] * s1, axis=0, keepdims=True))
        o_ref[pl.ds(bi, 1), :] = jnp.concatenate(outs, axis=-1)


def _mix_sample_back_kernel(x_ref, mod_ref, o_ref, g_ref, u_ref, buf_ref, w_pool_ref, pool_scale_ref,
                            w_out_ref, norm2_ref, wr_t_ref, bias_t_ref,
                            x1_ref, h2_ref, experts_ref, gatew_ref, pool_ref):
    o_gated = _group_norm_gate(o_ref[...], g_ref[...])
    u = u_ref[...]
    pooled = []
    for gi, w in enumerate(POOL_WINDOWS):
        lanes = slice(gi * POOL_GROUP_DIM, (gi + 1) * POOL_GROUP_DIM)
        acc = u[:, lanes]
        for j in range(1, w):
            acc = acc + buf_ref[:, POOL_BUF - j, lanes]
        pooled.append(acc / float(w) - u[:, lanes])
    p = _pool_project(pooled, w_pool_ref, pool_scale_ref)
    pool_ref[:, 0:POOL_BUF - 1, :] = buf_ref[:, 1:POOL_BUF, :]
    pool_ref[:, POOL_BUF - 1, :] = u
    x1 = _out_residual(x_ref[...], o_gated, p, mod_ref, 0, w_out_ref)
    x1_ref[...] = x1
    _norm_route(x1, mod_ref, 0, norm2_ref, wr_t_ref, bias_t_ref, h2_ref, experts_ref, gatew_ref)


def _mix_sample(x, mod, state_ret, state_pool, start, norm1, w_in16, w_pool,
                pool_scale, w_out16, norm2, wr_t, bias_t, block_b=32):
    n, d = x.shape
    half = RET_QK_DIM // 2
    ang = _rotary_angles([start])
    cos_c = jnp.asarray(np.broadcast_to(np.cos(ang).T, (half, n)), F32)
    sin_c = jnp.asarray(np.broadcast_to(np.sin(ang).T, (half, n)), F32)
    params = pltpu.CompilerParams(vmem_limit_bytes=VMEM_LIMIT)
    qt, kt, v, g, u = pl.pallas_call(
        _mix_sample_front_kernel,
        out_shape=[jax.ShapeDtypeStruct((QK_WIDTH, n), F32), jax.ShapeDtypeStruct((QK_WIDTH, n), F32),
                   jax.ShapeDtypeStruct((n, RET_WIDTH), F32), jax.ShapeDtypeStruct((n, RET_WIDTH), F32),
                   jax.ShapeDtypeStruct((n, POOL_WIDTH), F32)],
        compiler_params=params,
        name="mix_sample_front",
    )(x, mod, norm1, w_in16, cos_c, sin_c)

    lg = np.log(1.0 - 2.0 ** (-5.0 - np.arange(RET_HEADS, dtype=np.float32)), dtype=np.float32)
    decays = tuple(float(np.exp(lg[h])) for h in range(RET_HEADS))
    state_block = (block_b, RET_HEADS, RET_QK_DIM, RET_V_DIM)
    o, s1 = pl.pallas_call(
        functools.partial(_ret_step_kernel, block_b=block_b, decays=decays),
        grid=(n // block_b,),
        in_specs=[_full((QK_WIDTH, n)), _full((QK_WIDTH, n)), _full((n, RET_WIDTH)),
                  pl.BlockSpec(state_block, lambda i: (i, 0, 0, 0))],
        out_specs=[_full((n, RET_WIDTH)), pl.BlockSpec(state_block, lambda i: (i, 0, 0, 0))],
        out_shape=[jax.ShapeDtypeStruct((n, RET_WIDTH), F32),
                   jax.ShapeDtypeStruct(state_ret.shape, F32)],
        compiler_params=pltpu.CompilerParams(dimension_semantics=("arbitrary",),
                                             vmem_limit_bytes=VMEM_LIMIT),
        name="ret_step",
    )(qt, kt, v, state_ret)

    x1, h2, experts, gate_w, pool = pl.pallas_call(
        _mix_sample_back_kernel,
        out_shape=[jax.ShapeDtypeStruct((n, d), F32),
                   jax.ShapeDtypeStruct((n, HALF), U32),
                   jax.ShapeDtypeStruct((TOP_K, n), I32),
                   jax.ShapeDtypeStruct((n, LANES), F32),
                   jax.ShapeDtypeStruct(state_pool.shape, F32)],
        compiler_params=params,
        name="mix_sample_back",
    )(x, mod, o, g, u, state_pool, w_pool, pool_scale, w_out16, norm2, wr_t, bias_t)
    return x1, h2, experts, gate_w, s1, pool


def _plan_kernel(experts_ref, pos_ref, meta_ref, cnt_ref, carry_ref, off_ref, *, block_t):
    phase = pl.program_id(0)
    j = pl.program_id(1)
    e_blk = experts_ref[...]
    eidx = lax.broadcasted_iota(I32, (N_EXPERTS, block_t), 0)
    member = jnp.zeros((N_EXPERTS, block_t), F32)
    for s in range(TOP_K):
        member = member + jnp.where(eidx == e_blk[s:s + 1, :], 1.0, 0.0)
    per_expert = jnp.broadcast_to(jnp.sum(member, axis=1, keepdims=True), (N_EXPERTS, LANES))

    @pl.when((phase == 0) & (j == 0))
    def _():
        cnt_ref[...] = jnp.zeros_like(cnt_ref)

    @pl.when(phase == 0)
    def _():
        cnt_ref[...] += per_expert

    @pl.when((phase == 0) & (j == pl.num_programs(1) - 1))
    def _():
        cnt = cnt_ref[...]
        n_tile = jnp.floor((cnt + (ROW_TILE - 1.0)) * (1.0 / ROW_TILE))
        upto = (lax.broadcasted_iota(I32, (N_EXPERTS, N_EXPERTS), 1)
                <= lax.broadcasted_iota(I32, (N_EXPERTS, N_EXPERTS), 0))
        tile_end = _dot(jnp.where(upto, 1.0, 0.0).astype(BF16), n_tile.astype(BF16))
        tile_start = tile_end - n_tile
        off_ref[...] = tile_start * ROW_TILE
        carry_ref[...] = jnp.zeros_like(carry_ref)
        lane = lax.broadcasted_iota(I32, cnt.shape, 1)
        meta_ref[...] = jnp.where(lane == 0, tile_start, jnp.where(lane == 1, n_tile, cnt)).astype(I32)

    @pl.when(phase == 1)
    def _():
        before = (lax.broadcasted_iota(I32, (block_t, block_t), 0)
                  < lax.broadcasted_iota(I32, (block_t, block_t), 1))
        rank = _dot(member.astype(BF16), jnp.where(before, 1.0, 0.0).astype(BF16))
        row = off_ref[:, 0:1] + carry_ref[:, 0:1] + rank
        carry_ref[...] += per_expert
        out = [jnp.sum(jnp.where(eidx == e_blk[s:s + 1, :], row, 0.0), axis=0, keepdims=True)
               for s in range(TOP_K)]
        pos_ref[...] = jnp.concatenate(out, axis=0).astype(I32)


def _plan(experts_all, max_block=1024):
    n_tokens = experts_all.shape[1]
    block_t = max(k for k in range(LANES, max_block + 1, LANES) if n_tokens % k == 0)
    nb = n_tokens // block_t
    return pl.pallas_call(
        functools.partial(_plan_kernel, block_t=block_t),
        grid=(2, nb),
        in_specs=[pl.BlockSpec((TOP_K, block_t), lambda ph, j: (0, j))],
        out_specs=[pl.BlockSpec((TOP_K, block_t), lambda ph, j: (0, j * ph)),
                   _full((N_EXPERTS, LANES))],
        out_shape=[jax.ShapeDtypeStruct((TOP_K, n_tokens), I32),
                   jax.ShapeDtypeStruct((N_EXPERTS, LANES), I32)],
        scratch_shapes=[pltpu.VMEM((N_EXPERTS, LANES), F32)] * 3,
        compiler_params=pltpu.CompilerParams(dimension_semantics=("arbitrary", "arbitrary"),
                                             vmem_limit_bytes=VMEM_LIMIT),
        name="plan",
    )(experts_all)


def _sc_workers():
    info = plsc.get_sparse_core_info()
    return info.num_cores, info.num_cores * info.num_subcores


def _sc_scatter_rows(sources, pos_t, n_out, after=()):
    w = sources[0].shape[1]
    s = pos_t.shape[0]
    n_cores, n_workers = _sc_workers()
    bounds = np.cumsum([0] + [src.shape[0] // SC_CHUNK for src in sources])
    n_chunks = int(bounds[-1])
    iters = -(-n_chunks // n_workers)
    mesh = plsc.VectorSubcoreMesh(core_axis_name="c", subcore_axis_name="s")

    @functools.partial(
        pl.kernel, mesh=mesh, out_type=jax.ShapeDtypeStruct((n_out, w), sources[0].dtype),
        scratch_types=[pltpu.VMEM((SC_CHUNK, w), sources[0].dtype), pltpu.VMEM((s, SC_CHUNK), I32),
                       pltpu.SemaphoreType.DMA],
        name="dispatch")
    def k(*refs):
        src_hbm, pos_hbm = refs[:len(sources)], refs[len(sources)]
        out_hbm, rows_v, idx_v, sem = refs[len(sources) + 1 + len(after):]
        wid = lax.axis_index("s") * n_cores + lax.axis_index("c")

        @pl.loop(0, iters)
        def _(it):
            c = it * n_workers + wid
            for src, lo, hi in zip(src_hbm, bounds[:-1], bounds[1:]):
                @pl.when((c >= int(lo)) & (c < int(hi)))
                def _():
                    base = pl.multiple_of((c - int(lo)) * SC_CHUNK, SC_CHUNK)
                    pltpu.sync_copy(src.at[pl.ds(base, SC_CHUNK)], rows_v)

            @pl.when(c < n_chunks)
            def _():
                base = pl.multiple_of(c * SC_CHUNK, SC_CHUNK)
                pltpu.sync_copy(pos_hbm.at[:, pl.ds(base, SC_CHUNK)], idx_v)
                copies = [pltpu.async_copy(rows_v, out_hbm.at[idx_v.at[j]], sem) for j in range(s)]
                for cp in copies:
                    cp.wait()

    return k(*sources, pos_t, *after)


def _sc_gather_rows(table, pos_t):
    _, w = table.shape
    s, t = pos_t.shape
    n_cores, n_workers = _sc_workers()
    n_chunks = t // SC_CHUNK
    iters = -(-n_chunks // n_workers)
    mesh = plsc.VectorSubcoreMesh(core_axis_name="c", subcore_axis_name="s")

    @functools.partial(
        pl.kernel, mesh=mesh, out_type=jax.ShapeDtypeStruct((s, t, w), table.dtype),
        scratch_types=[pltpu.VMEM((SC_CHUNK, w), table.dtype), pltpu.VMEM((s, SC_CHUNK), I32),
                       pltpu.SemaphoreType.DMA],
        name="combine")
    def k(table_hbm, pos_hbm, out_hbm, rows_v, idx_v, sem):
        wid = lax.axis_index("s") * n_cores + lax.axis_index("c")

        @pl.loop(0, iters)
        def _(it):
            c = it * n_workers + wid

            @pl.when(c < n_chunks)
            def _():
                base = pl.multiple_of(c * SC_CHUNK, SC_CHUNK)
                pltpu.sync_copy(pos_hbm.at[:, pl.ds(base, SC_CHUNK)], idx_v)
                for j in range(s):
                    pltpu.async_copy(table_hbm.at[idx_v.at[j]], rows_v, sem).wait()
                    pltpu.sync_copy(rows_v, out_hbm.at[j, pl.ds(base, SC_CHUNK)])

    return k(table, pos_t)


def _sc_pack_weights(w, rows_per_item, after=()):
    e, r, c = w.shape
    half = r // 2
    rb = rows_per_item
    per_expert = half // rb
    n_cores, n_workers = _sc_workers()
    per_worker = e * per_expert // n_workers
    assert per_worker * n_workers == e * per_expert and per_worker % 2 == 0 and c % (SC_LANES * SC_UNROLL) == 0
    mesh = plsc.VectorSubcoreMesh(core_axis_name="c", subcore_axis_name="s")

    @functools.partial(
        pl.kernel, mesh=mesh, out_type=jax.ShapeDtypeStruct((e * half, c), U32),
        scratch_types=[pltpu.VMEM((2, rb, c), F32), pltpu.VMEM((2, rb, c), F32), pltpu.VMEM((2, rb, c), U32),
                       pltpu.SemaphoreType.DMA((2,)), pltpu.SemaphoreType.DMA((2,))],
        compiler_params=pltpu.CompilerParams(needs_layout_passes=False),
        cost_estimate=pl.CostEstimate(flops=e * r * c, transcendentals=0, bytes_accessed=6 * e * r * c),
        name="pack_weights")
    def k(w_hbm, *refs):
        out_hbm, lo_v, hi_v, out_v, in_sem, out_sem = refs[len(after):]
        wid = lax.axis_index("s") * n_cores + lax.axis_index("c")
        first = wid * per_worker

        def rows(item):
            ex = item // per_expert
            j = item - ex * per_expert
            return (pl.multiple_of(ex * r + j * rb, rb), pl.multiple_of(ex * r + half + j * rb, rb),
                    pl.multiple_of(ex * half + j * rb, rb))

        def loads(item, b):
            lo_row, hi_row, _ = rows(item)
            return (pltpu.make_async_copy(w_hbm.at[pl.ds(lo_row, rb)], lo_v.at[b], in_sem.at[b]),
                    pltpu.make_async_copy(w_hbm.at[pl.ds(hi_row, rb)], hi_v.at[b], in_sem.at[b]))

        def store(item, b):
            return pltpu.make_async_copy(out_v.at[b], out_hbm.at[pl.ds(rows(item)[2], rb)], out_sem.at[b])

        for cp in loads(first, 0):
            cp.start()

        @pl.loop(0, per_worker // 2)
        def _(pair):
            for b in range(2):
                item = first + pair * 2 + b
                for cp in loads(item, b):
                    cp.wait()

                @pl.when(item + 1 < first + per_worker)
                def _():
                    for cp in loads(item + 1, 1 - b):
                        cp.start()

                @pl.when(pair > 0)
                def _():
                    store(item - 2, b).wait()

                @pl.loop(0, rb)
                def _(i):
                    @pl.loop(0, c // (SC_LANES * SC_UNROLL))
                    def _(vb):
                        for u in range(SC_UNROLL):
                            sl = pl.ds(pl.multiple_of((vb * SC_UNROLL + u) * SC_LANES, SC_LANES), SC_LANES)
                            packed = plsc.pack(lo_v[b, i, sl], hi_v[b, i, sl], format=plsc.PackFormat.INTERLEAVED)
                            out_v[b, i, sl] = plsc.bitcast(packed, U32)

                store(item, b).start()

        for b in range(2):
            store(first + per_worker - 2 + b, b).wait()

    return k(w.reshape(e * r, c), *after).reshape(e, half, c)


def _experts_kernel(first_ref, ntile_ref, cnt_ref, xs_hbm, wg_ref, wu_ref, wd_ref, ys_hbm,
                    wg16_ref, wu16_ref, wd16_ref, x_buf, y_buf, in_sem, out_sem):
    ahead = STREAM_DEPTH - MAX_WIDTH
    e = pl.program_id(0)
    n_used = first_ref[N_EXPERTS - 1] + ntile_ref[N_EXPERTS - 1]
    first, n_mine, count = first_ref[e], ntile_ref[e], cnt_ref[e]

    def tile_rows(g):
        return pl.ds(pl.multiple_of(g * ROW_TILE, ROW_TILE), ROW_TILE)

    def load(g):
        slot = lax.rem(g, STREAM_DEPTH)
        return pltpu.make_async_copy(xs_hbm.at[tile_rows(g)], x_buf.at[slot], in_sem.at[slot])

    def store(g):
        slot = lax.rem(g, STREAM_DEPTH)
        return pltpu.make_async_copy(y_buf.at[slot], ys_hbm.at[tile_rows(g)], out_sem.at[slot])

    @pl.when(e == 0)
    def _():
        for g0 in range(ahead):
            @pl.when(g0 < n_used)
            def _():
                load(g0).start()

    for packed_ref, w16_ref in ((wg_ref, wg16_ref), (wu_ref, wu16_ref), (wd_ref, wd16_ref)):
        rows = packed_ref.shape[1]
        lo, hi = _unpack_rows(packed_ref[0])
        w16_ref[0:rows, :] = lo.astype(BF16)
        w16_ref[rows:, :] = hi.astype(BF16)

    def run(j, width):
        tiles = [first + j + t for t in range(width)]
        for g in tiles:
            load(g).wait()

            @pl.when(g + ahead < n_used)
            def _():
                load(g + ahead).start()

            @pl.when(g >= STREAM_DEPTH)
            def _():
                store(g - STREAM_DEPTH).wait()

        words = jnp.concatenate([x_buf[lax.rem(g, STREAM_DEPTH)] for g in tiles], axis=0)
        row = lax.broadcasted_iota(I32, words.shape, 0)
        words = jnp.where(row < count - j * ROW_TILE, words, jnp.uint32(0))
        lo, hi = _unpack_rows(words)
        lo, hi = lo.astype(BF16), hi.astype(BF16)
        hg = _dot(lo, wg16_ref[0:HALF, :]) + _dot(hi, wg16_ref[HALF:, :])
        hu = _dot(lo, wu16_ref[0:HALF, :]) + _dot(hi, wu16_ref[HALF:, :])
        a = (_silu(hg) * hu).astype(BF16)
        y = _pack_rows(_dot(a, wd16_ref[...]))
        for t, g in enumerate(tiles):
            y_buf[lax.rem(g, STREAM_DEPTH)] = y[t * ROW_TILE:(t + 1) * ROW_TILE]
            store(g).start()

    def widest(p, carry):
        run(MAX_WIDTH * p, MAX_WIDTH)
        return carry

    lax.fori_loop(0, n_mine // MAX_WIDTH, widest, 0)
    done = n_mine - lax.rem(n_mine, MAX_WIDTH)
    width = MAX_WIDTH // 2
    while width:
        has = lax.rem(n_mine // width, 2) == 1

        @pl.when(has)
        def _(width=width, done=done):
            run(done, width)

        done = done + jnp.where(has, width, 0)
        width //= 2

    @pl.when(e == N_EXPERTS - 1)
    def _():
        for back in range(STREAM_DEPTH, 0, -1):
            @pl.when(n_used >= back)
            def _():
                store(n_used - back).wait()


def _experts(xs, first_tile, n_tile, count, w_eg, w_eu, w_ed):
    d = D_MODEL
    by_expert = lambda e, *_: (e, 0, 0)
    grid_spec = pltpu.PrefetchScalarGridSpec(
        num_scalar_prefetch=3,
        grid=(N_EXPERTS,),
        in_specs=[pl.BlockSpec(memory_space=pl.ANY),
                  pl.BlockSpec((1, d // 2, EXPERT_DIM), by_expert),
                  pl.BlockSpec((1, d // 2, EXPERT_DIM), by_expert),
                  pl.BlockSpec((1, EXPERT_DIM // 2, d), by_expert)],
        out_specs=pl.BlockSpec(memory_space=pl.ANY),
        scratch_shapes=[pltpu.VMEM((d, EXPERT_DIM), BF16), pltpu.VMEM((d, EXPERT_DIM), BF16),
                        pltpu.VMEM((EXPERT_DIM, d), BF16),
                        pltpu.VMEM((STREAM_DEPTH, ROW_TILE, HALF), U32),
                        pltpu.VMEM((STREAM_DEPTH, ROW_TILE, HALF), U32),
                        pltpu.SemaphoreType.DMA((STREAM_DEPTH,)), pltpu.SemaphoreType.DMA((STREAM_DEPTH,))])
    return pl.pallas_call(
        _experts_kernel,
        grid_spec=grid_spec,
        out_shape=jax.ShapeDtypeStruct(xs.shape, U32),
        compiler_params=pltpu.CompilerParams(dimension_semantics=("arbitrary",),
                                             vmem_limit_bytes=VMEM_LIMIT),
        name="experts",
    )(first_tile, n_tile, count, xs, w_eg, w_eu, w_ed)


def _final_kernel(z_ref, gatew_ref, h2_ref, x1_ref, mod_ref, normf_ref, wsg_ref, wsu_ref, wsd_ref, *rest):
    y_ref, wsg16_ref, wsu16_ref, wsd16_ref = rest[-4:]

    @pl.when(pl.program_id(0) == 0)
    def _():
        wsg16_ref[...] = wsg_ref[...].astype(BF16)
        wsu16_ref[...] = wsu_ref[...].astype(BF16)
        wsd16_ref[...] = wsd_ref[...].astype(BF16)

    lo, hi = _unpack_rows(h2_ref[...])
    h = jnp.concatenate([lo, hi], axis=-1).astype(BF16)
    a = _silu(_dot(h, wsg16_ref[...])) * _dot(h, wsu16_ref[...])
    acc = _dot(a.astype(BF16), wsd16_ref[...])
    for s in range(TOP_K):
        lo, hi = _unpack_rows(z_ref[s])
        acc = acc + gatew_ref[:, s:s + 1] * jnp.concatenate([lo, hi], axis=-1)
    x2 = x1_ref[...] + _mod(mod_ref, 5) * acc
    y_ref[...] = _rms(x2) * normf_ref[...]


def _final(z, gate_w, h2, x1, mod, norm_f, w_sg, w_su, w_sd, block_t, first_block, per_seq,
           seq0=0, out_rows=None, y_prev=None):
    t, d = x1.shape
    out_rows = t if out_rows is None else out_rows
    out_first = seq0 * per_seq
    tok = lambda i: (i, 0)
    if per_seq:
        mod_spec = pl.BlockSpec((1, 6, d), lambda i: (seq0 + i // per_seq, 0, 0))
    else:
        mod_spec = pl.BlockSpec((block_t, 6 * d), tok)
    operands = [z, gate_w, h2, x1, mod, norm_f, w_sg, w_su, w_sd]
    in_specs = [pl.BlockSpec((TOP_K, block_t, HALF), lambda i: (0, first_block + i, 0)),
                pl.BlockSpec((block_t, LANES), tok),
                pl.BlockSpec((block_t, HALF), tok),
                pl.BlockSpec((block_t, d), tok),
                mod_spec,
                _full((1, d)),
                _full((d, EXPERT_DIM)), _full((d, EXPERT_DIM)), _full((EXPERT_DIM, d))]
    aliases = {}
    if y_prev is not None:
        aliases = {len(operands): 0}
        operands.append(y_prev)
        in_specs.append(pl.BlockSpec(memory_space=pl.ANY))
    return pl.pallas_call(
        _final_kernel,
        grid=(t // block_t,),
        in_specs=in_specs,
        out_specs=pl.BlockSpec((block_t, d), lambda i: (out_first + i, 0)),
        out_shape=jax.ShapeDtypeStruct((out_rows, d), F32),
        scratch_shapes=[pltpu.VMEM((d, EXPERT_DIM), BF16), pltpu.VMEM((d, EXPERT_DIM), BF16),
                        pltpu.VMEM((EXPERT_DIM, d), BF16)],
        input_output_aliases=aliases,
        compiler_params=pltpu.CompilerParams(dimension_semantics=("arbitrary",),
                                             vmem_limit_bytes=VMEM_LIMIT),
        name="final",
    )(*operands)


def kernel(x_prompt, x_sample, c_prompt, c_sample, state_ret, state_pool, norm1, norm2, norm_f,
           w_ada, b_ada, w_in, w_out, w_pool, pool_scale, w_router, router_bias, w_exp_gate,
           w_exp_up, w_exp_down, w_sh_gate, w_sh_up, w_sh_down):
    b, l, d = x_prompt.shape
    n = x_sample.shape[0]
    past_len = 16384

    mod_p, mod_s = _ada(c_prompt, c_sample, w_ada[0], b_ada[0])
    mod_p = mod_p.reshape(b, 6, d)

    w_in16 = w_in[0].astype(BF16)
    w_out16 = w_out[0].astype(BF16)
    wr_t = w_router[0].T
    bias_t = jnp.broadcast_to(router_bias[0][:, None], (N_EXPERTS, LANES))
    n1, n2, nf = norm1[0].reshape(1, d), norm2[0].reshape(1, d), norm_f.reshape(1, d)
    ps = pool_scale[0].reshape(1, POOL_WIDTH)
    shared = (w_sh_gate[0], w_sh_up[0], w_sh_down[0])

    def routed(sources, experts):
        n_tiles = experts.shape[1] * TOP_K // ROW_TILE + N_EXPERTS
        pos_t, meta = _plan(experts)
        xs = _sc_scatter_rows(sources, pos_t, n_tiles * ROW_TILE, after=expert_w)
        ys = _experts(xs, meta[:, 0], meta[:, 1], meta[:, 2], *expert_w)
        return _sc_gather_rows(ys, pos_t)

    mix_args = (n1, w_in16, w_pool[0], ps, w_out16, n2, wr_t, bias_t)
    x1_s, h2_s, experts_s, gatew_s, ret_s, pool_s = _mix_sample(
        x_sample.reshape(n, d), mod_s, state_ret[0], state_pool[0], float(past_len), *mix_args)

    expert_w = (_sc_pack_weights(w_exp_gate[0], 64, after=(x1_s,)),
                _sc_pack_weights(w_exp_up[0], 64, after=(x1_s,)),
                _sc_pack_weights(w_exp_down[0], 16, after=(x1_s,)))

    ba = b // 2
    bb = b - ba
    x1_a, h2_a, experts_a, gatew_a, ret_a, pool_a = _mix_prompt(x_prompt, mod_p, 0, ba, *mix_args)
    z_a = routed((h2_a,), experts_a)
    x1_b, h2_b, experts_b, gatew_b, ret_b, pool_b = _mix_prompt(x_prompt, mod_p, ba, bb, *mix_args)
    z_b = routed((h2_b, h2_s), jnp.concatenate([experts_b, experts_s], axis=1))

    block_t = 512
    per_seq = l // block_t
    y_s = _final(z_b, gatew_s, h2_s, x1_s, mod_s, nf, *shared,
                 block_t=n, first_block=bb * l // n, per_seq=0)
    y_p = _final(z_b, gatew_b, h2_b, x1_b.reshape(bb * l, d), mod_p, nf, *shared,
                 block_t=block_t, first_block=0, per_seq=per_seq, seq0=ba, out_rows=b * l)
    y_p = _final(z_a, gatew_a, h2_a, x1_a.reshape(ba * l, d), mod_p, nf, *shared,
                 block_t=block_t, first_block=0, per_seq=per_seq, seq0=0, out_rows=b * l, y_prev=y_p)

    ret_p = jnp.concatenate([ret_a, ret_b], axis=0)
    pool_p = jnp.concatenate([pool_a, pool_b], axis=0)
    return (y_p.reshape(b, l, d), y_s.reshape(n, 1, d), ret_p[None], pool_p[None],
            ret_s[None], pool_s[None])
```

```python
import functools

import jax
import jax.numpy as jnp
import numpy as np
from jax import lax
from jax.experimental import pallas as pl
from jax.experimental.pallas import tpu as pltpu
from jax.experimental.pallas import tpu_sc as plsc

D_MODEL = 1024
RET_HEADS = 4
RET_QK_DIM = 64
RET_V_DIM = 128
RET_WIDTH = RET_HEADS * RET_V_DIM
QK_WIDTH = RET_HEADS * RET_QK_DIM
ROPE_BASE = 10000.0
POOL_WINDOWS = (2, 4, 8, 16)
POOL_WIDTH = 512
POOL_GROUP_DIM = 128
POOL_BUF = 15
IN_WIDTH = 2 * QK_WIDTH + 2 * RET_WIDTH + POOL_WIDTH
N_EXPERTS = 64
TOP_K = 8
N_EXPERT_GROUPS = 8
GROUP_SIZE = N_EXPERTS // N_EXPERT_GROUPS
TOP_GROUPS = 4
EXPERT_DIM = 256
ROUTE_SCALE = 2.5
EPS = 1e-6
PAST_LEN = 16384

LANES = 128
SUBLANES = 8
POOL_CARRY = 24
VMEM_LIMIT = 56 * 1024 * 1024
HALF = D_MODEL // 2
ROW_TILE = 256
MAX_WIDTH = 2
SC_CHUNK = 128
SC_LANES = 16
SC_UNROLL = 16
STREAM_DEPTH = 16

BF16 = jnp.bfloat16
F32 = jnp.float32
U32 = jnp.uint32
I32 = jnp.int32


def _silu(x):
    return x * jax.nn.sigmoid(x)


def _dot(a, b):
    return jnp.dot(a, b, preferred_element_type=F32)


def _rms(x):
    return x * lax.rsqrt(jnp.mean(x * x, axis=-1, keepdims=True) + EPS)


def _ada_norm(x, gain_ref, mod_ref, shift_i, scale_i, seq=0):
    return _rms(x) * (gain_ref[...] * (1.0 + _mod(mod_ref, scale_i, seq))) + _mod(mod_ref, shift_i, seq)


def _mod(mod_ref, i, seq=0):
    if len(mod_ref.shape) == 3:
        return mod_ref[seq, i:i + 1, :]
    return mod_ref[:, i * D_MODEL:(i + 1) * D_MODEL]


def _split_bf16(x):
    hi = x.astype(BF16)
    lo = (x - hi.astype(F32)).astype(BF16)
    return hi, lo


def _pack_rows(x):
    lo = lax.bitcast_convert_type(x[:, :HALF].astype(BF16).astype(F32), U32)
    hi = lax.bitcast_convert_type(x[:, HALF:].astype(BF16).astype(F32), U32)
    return (hi & jnp.uint32(0xFFFF0000)) | (lo >> jnp.uint32(16))


def _unpack_rows(w):
    lo = lax.bitcast_convert_type(w << jnp.uint32(16), F32)
    hi = lax.bitcast_convert_type(w & jnp.uint32(0xFFFF0000), F32)
    return lo, hi


def _first_max_onehot(work, idx, n):
    m = jnp.max(work, axis=0, keepdims=True)
    first = jnp.min(jnp.where(work == m, idx, float(n)), axis=0, keepdims=True)
    return idx == first


def _route(h2, wr_t_ref, bias_t_ref):
    n = h2.shape[0]
    h_hi, h_lo = _split_bf16(h2)
    w_hi, w_lo = _split_bf16(wr_t_ref[...])
    nt = (((1,), (1,)), ((), ()))
    logits = (lax.dot_general(w_hi, h_hi, nt, preferred_element_type=F32)
              + lax.dot_general(w_hi, h_lo, nt, preferred_element_type=F32)
              + lax.dot_general(w_lo, h_hi, nt, preferred_element_type=F32))
    scores = jax.nn.sigmoid(logits)
    biased = scores + bias_t_ref[:, 0:1]
    b3 = biased.reshape(N_EXPERT_GROUPS, GROUP_SIZE, n)
    i3 = lax.broadcasted_iota(I32, b3.shape, 1).astype(F32)
    m1 = jnp.max(b3, axis=1, keepdims=True)
    first = jnp.min(jnp.where(b3 == m1, i3, float(GROUP_SIZE)), axis=1, keepdims=True)
    m2 = jnp.max(jnp.where(i3 == first, -jnp.inf, b3), axis=1, keepdims=True)
    gscore = (m1 + m2).reshape(N_EXPERT_GROUPS, n)
    gidx = lax.broadcasted_iota(I32, gscore.shape, 0).astype(F32)
    gsel = jnp.zeros(gscore.shape, F32)
    work = gscore
    for _ in range(TOP_GROUPS):
        hit = _first_max_onehot(work, gidx, N_EXPERT_GROUPS)
        gsel = jnp.where(hit, 1.0, gsel)
        work = jnp.where(hit, -jnp.inf, work)
    gsel3 = jnp.broadcast_to(gsel.reshape(N_EXPERT_GROUPS, 1, n), b3.shape)
    work = jnp.where(gsel3 > 0.0, b3, -jnp.inf).reshape(N_EXPERTS, n)
    eidx = lax.broadcasted_iota(I32, work.shape, 0).astype(F32)
    sel = jnp.zeros(work.shape, F32)
    for _ in range(TOP_K):
        hit = _first_max_onehot(work, eidx, N_EXPERTS)
        sel = jnp.where(hit, 1.0, sel)
        work = jnp.where(hit, -jnp.inf, work)
    picked = jnp.where(sel > 0.0, scores, 0.0)
    gates = picked / jnp.sum(picked, axis=0, keepdims=True) * ROUTE_SCALE
    below = (lax.broadcasted_iota(I32, (N_EXPERTS, N_EXPERTS), 1)
             < lax.broadcasted_iota(I32, (N_EXPERTS, N_EXPERTS), 0))
    slot = _dot(jnp.where(below, 1.0, 0.0).astype(BF16), sel.astype(BF16))
    e_rows, w_rows = [], []
    for s in range(TOP_K):
        here = jnp.where(slot == float(s), sel, 0.0)
        e_rows.append(jnp.sum(here * eidx, axis=0, keepdims=True))
        w_rows.append(jnp.sum(here * gates, axis=0, keepdims=True))
    experts = jnp.concatenate(e_rows, axis=0).astype(I32)
    w_t = jnp.concatenate(w_rows + [jnp.zeros((LANES - TOP_K, n), F32)], axis=0)
    return experts, w_t.T


def _group_norm_gate(o, g):
    parts = []
    for h in range(RET_HEADS):
        oh = o[:, h * RET_V_DIM:(h + 1) * RET_V_DIM]
        mu = jnp.mean(oh, axis=-1, keepdims=True)
        ctr = oh - mu
        var = jnp.mean(ctr * ctr, axis=-1, keepdims=True)
        parts.append(ctr * lax.rsqrt(var + EPS))
    return _silu(g) * jnp.concatenate(parts, axis=-1)


def _pool_project(pooled, w_pool_ref, pool_scale_ref):
    parts = [_dot(p.astype(BF16), w_pool_ref[gi].astype(BF16)) for gi, p in enumerate(pooled)]
    return jnp.concatenate(parts, axis=-1) * pool_scale_ref[...]


def _out_residual(x, o_gated, p, mod_ref, seq, w_out_ref):
    mix = jnp.concatenate([o_gated, p], axis=-1).astype(BF16)
    return x + _mod(mod_ref, 2, seq) * _dot(mix, w_out_ref[...])


def _norm_route(x1, mod_ref, seq, norm2_ref, wr_t_ref, bias_t_ref, h2_ref, experts_ref, gatew_ref):
    h2 = _ada_norm(x1, norm2_ref, mod_ref, 3, 4, seq)
    h2_ref[...] = _pack_rows(h2)
    experts, gate_w = _route(h2, wr_t_ref, bias_t_ref)
    experts_ref[...] = experts
    gatew_ref[...] = gate_w


def _ada_kernel(cp_ref, cs_ref, w_ref, b_ref, op_ref, os_ref):
    w16 = w_ref[...].astype(BF16)
    for c_ref, o_ref in ((cp_ref, op_ref), (cs_ref, os_ref)):
        o_ref[...] = _dot(_silu(c_ref[...]).astype(BF16), w16) + b_ref[...]


def _ada(c_prompt, c_sample, w_ada, b_ada, block_n=1536):
    d, width = w_ada.shape
    rows = lambda c: pl.BlockSpec((c.shape[0], d), lambda j: (0, 0))
    cols = lambda c: pl.BlockSpec((c.shape[0], block_n), lambda j: (0, j))
    return pl.pallas_call(
        _ada_kernel,
        grid=(width // block_n,),
        in_specs=[rows(c_prompt), rows(c_sample),
                  pl.BlockSpec((d, block_n), lambda j: (0, j)),
                  pl.BlockSpec((1, block_n), lambda j: (0, j))],
        out_specs=[cols(c_prompt), cols(c_sample)],
        out_shape=[jax.ShapeDtypeStruct((c.shape[0], width), F32) for c in (c_prompt, c_sample)],
        compiler_params=pltpu.CompilerParams(vmem_limit_bytes=VMEM_LIMIT),
        name="ada",
    )(c_prompt, c_sample, w_ada, b_ada.reshape(1, width))


def _mix_prompt_kernel(x_ref, mod_ref, norm1_ref, w_in_ref, cos_ref, sin_ref, dmat_ref, cross_ref,
                       tail_ref, cdec_ref, w_pool_ref, pool_scale_ref, w_out_ref, norm2_ref,
                       wr_t_ref, bias_t_ref,
                       x1_ref, h2_ref, experts_ref, gatew_ref, ret_ref, pool_ref,
                       state_ref, ext_ref, win_ref, o_ref, *, block_l, chunk, seqs):
    li = pl.program_id(1)

    @pl.when(li == 0)
    def _():
        state_ref[...] = jnp.zeros_like(state_ref)
        ext_ref[:, 0:POOL_CARRY, :] = jnp.zeros((seqs, POOL_CARRY, POOL_WIDTH), F32)
        win_ref[:, 0:SUBLANES, :] = jnp.zeros((seqs, SUBLANES, POOL_WIDTH), F32)

    for seq in range(seqs):
        _mix_prompt_seq(seq, li, x_ref, mod_ref, norm1_ref, w_in_ref, cos_ref, sin_ref, dmat_ref, cross_ref,
                        tail_ref, cdec_ref, w_pool_ref, pool_scale_ref, w_out_ref, norm2_ref,
                        wr_t_ref, bias_t_ref, x1_ref, h2_ref, experts_ref, gatew_ref,
                        state_ref.at[seq], ext_ref.at[seq], win_ref.at[seq], o_ref.at[seq],
                        block_l=block_l, chunk=chunk)

    @pl.when(li == pl.num_programs(1) - 1)
    def _():
        ret_ref[...] = state_ref[...].reshape(ret_ref.shape)
        pool_ref[...] = ext_ref[:, POOL_CARRY - POOL_BUF:POOL_CARRY, :]


def _window_sums(ext_ref, win_ref, block_l):
    g = POOL_GROUP_DIM
    top = POOL_CARRY + block_l
    new = slice(POOL_CARRY - SUBLANES, None)
    s2 = ext_ref[SUBLANES:top, :] + ext_ref[SUBLANES - 1:top - 1, :]
    win_ref[SUBLANES:top, g:] = s2[:, g:]
    s4 = s2[:, g:] + win_ref[SUBLANES - 2:top - 2, g:]
    win_ref[SUBLANES:top, 2 * g:] = s4[:, g:]
    s8 = s4[:, g:] + win_ref[SUBLANES - 4:top - 4, 2 * g:]
    win_ref[SUBLANES:top, 3 * g:] = s8[:, g:]
    s16 = s8[:, g:] + win_ref[0:top - SUBLANES, 3 * g:]
    return [s2[new, 0:g], s4[new, 0:g], s8[new, 0:g], s16[new, :]]


def _mix_prompt_seq(seq, li, x_ref, mod_ref, norm1_ref, w_in_ref, cos_ref, sin_ref, dmat_ref, cross_ref,
                    tail_ref, cdec_ref, w_pool_ref, pool_scale_ref, w_out_ref, norm2_ref,
                    wr_t_ref, bias_t_ref, x1_ref, h2_ref, experts_ref, gatew_ref,
                    state_ref, ext_ref, win_ref, o_ref, *, block_l, chunk):
    x = x_ref[seq]
    h16 = _ada_norm(x, norm1_ref, mod_ref, 0, 1, seq).astype(BF16)
    proj = _dot(h16, w_in_ref[...])
    q = proj[:, 0:QK_WIDTH]
    k = proj[:, QK_WIDTH:2 * QK_WIDTH]
    v = proj[:, 2 * QK_WIDTH:2 * QK_WIDTH + RET_WIDTH]
    g = proj[:, 2 * QK_WIDTH + RET_WIDTH:2 * QK_WIDTH + 2 * RET_WIDTH]
    u = proj[:, 2 * QK_WIDTH + 2 * RET_WIDTH:]

    lane = lax.broadcasted_iota(I32, q.shape, 1)
    first_half = (lane % RET_QK_DIM) < (RET_QK_DIM // 2)
    cos_t = cos_ref[...]
    sin_t = sin_ref[...]

    def rot(t):
        partner = jnp.where(first_half, pltpu.roll(t, QK_WIDTH - RET_QK_DIM // 2, axis=1),
                            pltpu.roll(t, RET_QK_DIM // 2, axis=1))
        return t * cos_t + partner * sin_t

    q = rot(q)
    k = rot(k) * (RET_QK_DIM ** -0.5)
    k_t = k.T
    v16 = v.astype(BF16)
    head_of_lane = lax.broadcasted_iota(I32, (chunk, QK_WIDTH), 1) // RET_QK_DIM

    for c in range(block_l // chunk):
        rows = slice(c * chunk, (c + 1) * chunk)
        q_c = q[rows]
        kt_c = k_t[:, rows]
        kt16 = kt_c.astype(BF16)
        state16 = state_ref[...].astype(BF16)
        for hd in range(RET_HEADS):
            in_head = head_of_lane == hd
            q_h = jnp.where(in_head, q_c, 0.0).astype(BF16)
            v_h = v16[rows, hd * RET_V_DIM:(hd + 1) * RET_V_DIM]
            scores = _dot(q_h, kt16) * dmat_ref[hd]
            inner = _dot(scores.astype(BF16), v_h)
            cross = _dot(q_h, state16) * cross_ref[hd]
            o_ref[rows, hd * RET_V_DIM:(hd + 1) * RET_V_DIM] = inner + cross
            hrows = slice(hd * RET_QK_DIM, (hd + 1) * RET_QK_DIM)
            k_dec = (kt_c[hrows] * tail_ref[hd:hd + 1, :]).astype(BF16)
            state_ref[hrows, :] = state_ref[hrows, :] * cdec_ref[hd] + _dot(k_dec, v_h)

    o_gated = _group_norm_gate(o_ref[...], g)

    ext_ref[POOL_CARRY:POOL_CARRY + block_l, :] = u
    pos = (li * block_l + lax.broadcasted_iota(I32, (block_l, 1), 0)).astype(F32)
    pooled = []
    for gi, (w, acc) in enumerate(zip(POOL_WINDOWS, _window_sums(ext_ref, win_ref, block_l))):
        cnt = jnp.minimum(pos + 1.0, float(w))
        pooled.append(acc / cnt - u[:, gi * POOL_GROUP_DIM:(gi + 1) * POOL_GROUP_DIM])
    p = _pool_project(pooled, w_pool_ref, pool_scale_ref)
    ext_ref[0:POOL_CARRY, :] = ext_ref[block_l:block_l + POOL_CARRY, :]

    x1 = _out_residual(x, o_gated, p, mod_ref, seq, w_out_ref)
    x1_ref[seq] = x1
    _norm_route(x1, mod_ref, seq, norm2_ref, wr_t_ref, bias_t_ref,
                h2_ref.at[seq], experts_ref.at[seq], gatew_ref.at[seq])


def _decay_tables(chunk):
    f32 = np.float32
    lg = np.log(f32(1.0) - f32(2.0) ** (f32(-5.0) - np.arange(RET_HEADS, dtype=f32))).astype(f32)
    idx = np.arange(chunk, dtype=f32)
    diff = idx[:, None] - idx[None, :]
    causal = diff >= 0
    dmat = np.where(causal[None], np.exp(lg[:, None, None] * np.where(causal, diff, f32(0.0))[None]), f32(0.0))
    cross = np.exp(lg[:, None] * (idx[None, :] + f32(1.0)))
    cross = np.broadcast_to(cross[:, :, None], (RET_HEADS, chunk, RET_V_DIM))
    tail = np.exp(lg[:, None] * (f32(chunk - 1.0) - idx)[None, :])
    cdec = np.broadcast_to(np.exp(lg * f32(chunk))[:, None, None], (RET_HEADS, RET_QK_DIM, RET_V_DIM))
    return tuple(jnp.asarray(np.ascontiguousarray(t, dtype=f32)) for t in (dmat, cross, tail, cdec))


def _rotary_angles(pos):
    half = RET_QK_DIM // 2
    freqs = (np.float32(ROPE_BASE) ** (-np.arange(half, dtype=np.float32) / np.float32(half))).astype(np.float32)
    return np.asarray(pos, np.float32)[:, None] * freqs[None, :]


def _rotary_tables(pos):
    ang = _rotary_angles(pos)
    cos, sin = np.cos(ang), np.sin(ang)
    cos_t = np.tile(np.concatenate([cos, cos], axis=-1), (1, RET_HEADS))
    sin_t = np.tile(np.concatenate([-sin, sin], axis=-1), (1, RET_HEADS))
    return jnp.asarray(cos_t, F32), jnp.asarray(sin_t, F32)


def _full(shape):
    return pl.BlockSpec(shape, lambda *_: (0,) * len(shape))


def _mix_prompt(x, mod, b0, b, norm1, w_in16, w_pool, pool_scale, w_out16, norm2, wr_t, bias_t,
                block_l=512, chunk=256, seqs=2):
    _, l, d = x.shape
    nl = l // block_l
    s0 = b0 // seqs
    cos_t, sin_t = _rotary_tables(np.arange(l))
    dmat, cross, tail, cdec = _decay_tables(chunk)
    kernel = functools.partial(_mix_prompt_kernel, block_l=block_l, chunk=chunk, seqs=seqs)
    tok = lambda bi, li: (bi, li, 0)
    per_seq = lambda bi, li: (bi, 0, 0)
    x1, h2, experts, gate_w, ret, pool = pl.pallas_call(
        kernel,
        grid=(b // seqs, nl),
        in_specs=[pl.BlockSpec((seqs, block_l, d), lambda bi, li: (s0 + bi, li, 0)),
                  pl.BlockSpec((seqs, 6, d), lambda bi, li: (s0 + bi, 0, 0)),
                  _full((1, d)),
                  _full((d, IN_WIDTH)),
                  pl.BlockSpec((block_l, QK_WIDTH), lambda bi, li: (li, 0)),
                  pl.BlockSpec((block_l, QK_WIDTH), lambda bi, li: (li, 0)),
                  _full(dmat.shape), _full(cross.shape), _full(tail.shape), _full(cdec.shape),
                  _full(w_pool.shape), _full((1, POOL_WIDTH)), _full((d, d)), _full((1, d)),
                  _full(wr_t.shape), _full(bias_t.shape)],
        out_specs=[pl.BlockSpec((seqs, block_l, d), tok),
                   pl.BlockSpec((seqs, block_l, HALF), tok),
                   pl.BlockSpec((seqs, TOP_K, block_l), lambda bi, li: (bi, 0, li)),
                   pl.BlockSpec((seqs, block_l, LANES), tok),
                   pl.BlockSpec((seqs, RET_HEADS, RET_QK_DIM, RET_V_DIM), lambda bi, li: (bi, 0, 0, 0)),
                   pl.BlockSpec((seqs, POOL_BUF, POOL_WIDTH), per_seq)],
        out_shape=[jax.ShapeDtypeStruct((b, l, d), F32),
                   jax.ShapeDtypeStruct((b, l, HALF), U32),
                   jax.ShapeDtypeStruct((b, TOP_K, l), I32),
                   jax.ShapeDtypeStruct((b, l, LANES), F32),
                   jax.ShapeDtypeStruct((b, RET_HEADS, RET_QK_DIM, RET_V_DIM), F32),
                   jax.ShapeDtypeStruct((b, POOL_BUF, POOL_WIDTH), F32)],
        scratch_shapes=[pltpu.VMEM((seqs, QK_WIDTH, RET_V_DIM), F32),
                        pltpu.VMEM((seqs, POOL_CARRY + block_l, POOL_WIDTH), F32),
                        pltpu.VMEM((seqs, POOL_CARRY + block_l, POOL_WIDTH), F32),
                        pltpu.VMEM((seqs, block_l, RET_WIDTH), F32)],
        compiler_params=pltpu.CompilerParams(dimension_semantics=("arbitrary", "arbitrary"),
                                             vmem_limit_bytes=VMEM_LIMIT),
        name="mix_prompt",
    )(x, mod, norm1, w_in16, cos_t, sin_t, dmat, cross, tail, cdec, w_pool, pool_scale,
      w_out16, norm2, wr_t, bias_t)
    experts = jnp.transpose(experts, (1, 0, 2)).reshape(TOP_K, b * l)
    return x1, h2.reshape(b * l, HALF), experts, gate_w.reshape(b * l, LANES), ret, pool


def _mix_sample_front_kernel(x_ref, mod_ref, norm1_ref, w_in_ref, cos_ref, sin_ref,
                             qt_ref, kt_ref, v_ref, g_ref, u_ref):
    x = x_ref[...]
    h = _ada_norm(x, norm1_ref, mod_ref, 0, 1)
    proj = _dot(h.astype(BF16), w_in_ref[...])
    half = RET_QK_DIM // 2
    cos_c = cos_ref[...]
    sin_c = sin_ref[...]

    def rot_t(t):
        parts = []
        for hd in range(RET_HEADS):
            t1 = t[hd * RET_QK_DIM:hd * RET_QK_DIM + half]
            t2 = t[hd * RET_QK_DIM + half:(hd + 1) * RET_QK_DIM]
            parts += [t1 * cos_c - t2 * sin_c, t1 * sin_c + t2 * cos_c]
        return jnp.concatenate(parts, axis=0)

    qt_ref[...] = rot_t(proj[:, 0:QK_WIDTH].T)
    kt_ref[...] = rot_t(proj[:, QK_WIDTH:2 * QK_WIDTH].T) * (RET_QK_DIM ** -0.5)
    v_ref[...] = proj[:, 2 * QK_WIDTH:2 * QK_WIDTH + RET_WIDTH]
    g_ref[...] = proj[:, 2 * QK_WIDTH + RET_WIDTH:2 * QK_WIDTH + 2 * RET_WIDTH]
    u_ref[...] = proj[:, 2 * QK_WIDTH + 2 * RET_WIDTH:]


def _ret_step_kernel(qt_ref, kt_ref, v_ref, s0_ref, o_ref, s1_ref, *, block_b, decays):
    i = pl.program_id(0)
    lane = lax.broadcasted_iota(I32, qt_ref.shape, 1)
    for j in range(block_b):
        bi = i * block_b + j
        here = lane == bi
        q_col = jnp.sum(jnp.where(here, qt_ref[...], 0.0), axis=1, keepdims=True)
        k_col = jnp.sum(jnp.where(here, kt_ref[...], 0.0), axis=1, keepdims=True)
        v_row = v_ref[pl.ds(bi, 1), :]
        outs = []
        for hd in range(RET_HEADS):
            hrows = slice(hd * RET_QK_DIM, (hd + 1) * RET_QK_DIM)
            s1 = decays[hd] * s0_ref[j, hd] + k_col[hrows] * v_row[:, hd * RET_V_DIM:(hd + 1) * RET_V_DIM]
            s1_ref[j, hd] = s1
            outs.append(jnp.sum(q_col[hrows] * s1, axis=0, keepdims=True))
        o_ref[pl.ds(bi, 1), :] = jnp.concatenate(outs, axis=-1)


def _mix_sample_back_kernel(x_ref, mod_ref, o_ref, g_ref, u_ref, buf_ref, w_pool_ref, pool_scale_ref,
                            w_out_ref, norm2_ref, wr_t_ref, bias_t_ref,
                            x1_ref, h2_ref, experts_ref, gatew_ref, pool_ref):
    o_gated = _group_norm_gate(o_ref[...], g_ref[...])
    u = u_ref[...]
    pooled = []
    for gi, w in enumerate(POOL_WINDOWS):
        lanes = slice(gi * POOL_GROUP_DIM, (gi + 1) * POOL_GROUP_DIM)
        acc = u[:, lanes]
        for j in range(1, w):
            acc = acc + buf_ref[:, POOL_BUF - j, lanes]
        pooled.append(acc / float(w) - u[:, lanes])
    p = _pool_project(pooled, w_pool_ref, pool_scale_ref)
    pool_ref[:, 0:POOL_BUF - 1, :] = buf_ref[:, 1:POOL_BUF, :]
    pool_ref[:, POOL_BUF - 1, :] = u
    x1 = _out_residual(x_ref[...], o_gated, p, mod_ref, 0, w_out_ref)
    x1_ref[...] = x1
    _norm_route(x1, mod_ref, 0, norm2_ref, wr_t_ref, bias_t_ref, h2_ref, experts_ref, gatew_ref)


def _mix_sample(x, mod, state_ret, state_pool, start, norm1, w_in16, w_pool,
                pool_scale, w_out16, norm2, wr_t, bias_t, block_b=32):
    n, d = x.shape
    half = RET_QK_DIM // 2
    ang = _rotary_angles([start])
    cos_c = jnp.asarray(np.broadcast_to(np.cos(ang).T, (half, n)), F32)
    sin_c = jnp.asarray(np.broadcast_to(np.sin(ang).T, (half, n)), F32)
    params = pltpu.CompilerParams(vmem_limit_bytes=VMEM_LIMIT)
    qt, kt, v, g, u = pl.pallas_call(
        _mix_sample_front_kernel,
        out_shape=[jax.ShapeDtypeStruct((QK_WIDTH, n), F32), jax.ShapeDtypeStruct((QK_WIDTH, n), F32),
                   jax.ShapeDtypeStruct((n, RET_WIDTH), F32), jax.ShapeDtypeStruct((n, RET_WIDTH), F32),
                   jax.ShapeDtypeStruct((n, POOL_WIDTH), F32)],
        compiler_params=params,
        name="mix_sample_front",
    )(x, mod, norm1, w_in16, cos_c, sin_c)

    lg = np.log(1.0 - 2.0 ** (-5.0 - np.arange(RET_HEADS, dtype=np.float32)), dtype=np.float32)
    decays = tuple(float(np.exp(lg[h])) for h in range(RET_HEADS))
    state_block = (block_b, RET_HEADS, RET_QK_DIM, RET_V_DIM)
    o, s1 = pl.pallas_call(
        functools.partial(_ret_step_kernel, block_b=block_b, decays=decays),
        grid=(n // block_b,),
        in_specs=[_full((QK_WIDTH, n)), _full((QK_WIDTH, n)), _full((n, RET_WIDTH)),
                  pl.BlockSpec(state_block, lambda i: (i, 0, 0, 0))],
        out_specs=[_full((n, RET_WIDTH)), pl.BlockSpec(state_block, lambda i: (i, 0, 0, 0))],
        out_shape=[jax.ShapeDtypeStruct((n, RET_WIDTH), F32),
                   jax.ShapeDtypeStruct(state_ret.shape, F32)],
        compiler_params=pltpu.CompilerParams(dimension_semantics=("arbitrary",),
                                             vmem_limit_bytes=VMEM_LIMIT),
        name="ret_step",
    )(qt, kt, v, state_ret)

    x1, h2, experts, gate_w, pool = pl.pallas_call(
        _mix_sample_back_kernel,
        out_shape=[jax.ShapeDtypeStruct((n, d), F32),
                   jax.ShapeDtypeStruct((n, HALF), U32),
                   jax.ShapeDtypeStruct((TOP_K, n), I32),
                   jax.ShapeDtypeStruct((n, LANES), F32),
                   jax.ShapeDtypeStruct(state_pool.shape, F32)],
        compiler_params=params,
        name="mix_sample_back",
    )(x, mod, o, g, u, state_pool, w_pool, pool_scale, w_out16, norm2, wr_t, bias_t)
    return x1, h2, experts, gate_w, s1, pool


def _plan_kernel(experts_ref, pos_ref, meta_ref, cnt_ref, carry_ref, off_ref, *, block_t):
    phase = pl.program_id(0)
    j = pl.program_id(1)
    e_blk = experts_ref[...]
    eidx = lax.broadcasted_iota(I32, (N_EXPERTS, block_t), 0)
    member = jnp.zeros((N_EXPERTS, block_t), F32)
    for s in range(TOP_K):
        member = member + jnp.where(eidx == e_blk[s:s + 1, :], 1.0, 0.0)
    per_expert = jnp.broadcast_to(jnp.sum(member, axis=1, keepdims=True), (N_EXPERTS, LANES))

    @pl.when((phase == 0) & (j == 0))
    def _():
        cnt_ref[...] = jnp.zeros_like(cnt_ref)

    @pl.when(phase == 0)
    def _():
        cnt_ref[...] += per_expert

    @pl.when((phase == 0) & (j == pl.num_programs(1) - 1))
    def _():
        cnt = cnt_ref[...]
        n_tile = jnp.floor((cnt + (ROW_TILE - 1.0)) * (1.0 / ROW_TILE))
        upto = (lax.broadcasted_iota(I32, (N_EXPERTS, N_EXPERTS), 1)
                <= lax.broadcasted_iota(I32, (N_EXPERTS, N_EXPERTS), 0))
        tile_end = _dot(jnp.where(upto, 1.0, 0.0).astype(BF16), n_tile.astype(BF16))
        tile_start = tile_end - n_tile
        off_ref[...] = tile_start * ROW_TILE
        carry_ref[...] = jnp.zeros_like(carry_ref)
        lane = lax.broadcasted_iota(I32, cnt.shape, 1)
        meta_ref[...] = jnp.where(lane == 0, tile_start, jnp.where(lane == 1, n_tile, cnt)).astype(I32)

    @pl.when(phase == 1)
    def _():
        before = (lax.broadcasted_iota(I32, (block_t, block_t), 0)
                  < lax.broadcasted_iota(I32, (block_t, block_t), 1))
        rank = _dot(member.astype(BF16), jnp.where(before, 1.0, 0.0).astype(BF16))
        row = off_ref[:, 0:1] + carry_ref[:, 0:1] + rank
        carry_ref[...] += per_expert
        out = [jnp.sum(jnp.where(eidx == e_blk[s:s + 1, :], row, 0.0), axis=0, keepdims=True)
               for s in range(TOP_K)]
        pos_ref[...] = jnp.concatenate(out, axis=0).astype(I32)


def _plan(experts_all, max_block=1024):
    n_tokens = experts_all.shape[1]
    block_t = max(k for k in range(LANES, max_block + 1, LANES) if n_tokens % k == 0)
    nb = n_tokens // block_t
    return pl.pallas_call(
        functools.partial(_plan_kernel, block_t=block_t),
        grid=(2, nb),
        in_specs=[pl.BlockSpec((TOP_K, block_t), lambda ph, j: (0, j))],
        out_specs=[pl.BlockSpec((TOP_K, block_t), lambda ph, j: (0, j * ph)),
                   _full((N_EXPERTS, LANES))],
        out_shape=[jax.ShapeDtypeStruct((TOP_K, n_tokens), I32),
                   jax.ShapeDtypeStruct((N_EXPERTS, LANES), I32)],
        scratch_shapes=[pltpu.VMEM((N_EXPERTS, LANES), F32)] * 3,
        compiler_params=pltpu.CompilerParams(dimension_semantics=("arbitrary", "arbitrary"),
                                             vmem_limit_bytes=VMEM_LIMIT),
        name="plan",
    )(experts_all)


def _sc_workers():
    info = plsc.get_sparse_core_info()
    return info.num_cores, info.num_cores * info.num_subcores


def _sc_scatter_rows(sources, pos_t, n_out, after=()):
    w = sources[0].shape[1]
    s = pos_t.shape[0]
    n_cores, n_workers = _sc_workers()
    bounds = np.cumsum([0] + [src.shape[0] // SC_CHUNK for src in sources])
    n_chunks = int(bounds[-1])
    iters = -(-n_chunks // n_workers)
    mesh = plsc.VectorSubcoreMesh(core_axis_name="c", subcore_axis_name="s")

    @functools.partial(
        pl.kernel, mesh=mesh, out_type=jax.ShapeDtypeStruct((n_out, w), sources[0].dtype),
        scratch_types=[pltpu.VMEM((SC_CHUNK, w), sources[0].dtype), pltpu.VMEM((s, SC_CHUNK), I32),
                       pltpu.SemaphoreType.DMA],
        name="dispatch")
    def k(*refs):
        src_hbm, pos_hbm = refs[:len(sources)], refs[len(sources)]
        out_hbm, rows_v, idx_v, sem = refs[len(sources) + 1 + len(after):]
        wid = lax.axis_index("s") * n_cores + lax.axis_index("c")

        @pl.loop(0, iters)
        def _(it):
            c = it * n_workers + wid
            for src, lo, hi in zip(src_hbm, bounds[:-1], bounds[1:]):
                @pl.when((c >= int(lo)) & (c < int(hi)))
                def _():
                    base = pl.multiple_of((c - int(lo)) * SC_CHUNK, SC_CHUNK)
                    pltpu.sync_copy(src.at[pl.ds(base, SC_CHUNK)], rows_v)

            @pl.when(c < n_chunks)
            def _():
                base = pl.multiple_of(c * SC_CHUNK, SC_CHUNK)
                pltpu.sync_copy(pos_hbm.at[:, pl.ds(base, SC_CHUNK)], idx_v)
                copies = [pltpu.async_copy(rows_v, out_hbm.at[idx_v.at[j]], sem) for j in range(s)]
                for cp in copies:
                    cp.wait()

    return k(*sources, pos_t, *after)


def _sc_gather_rows(table, pos_t):
    _, w = table.shape
    s, t = pos_t.shape
    n_cores, n_workers = _sc_workers()
    n_chunks = t // SC_CHUNK
    iters = -(-n_chunks // n_workers)
    mesh = plsc.VectorSubcoreMesh(core_axis_name="c", subcore_axis_name="s")

    @functools.partial(
        pl.kernel, mesh=mesh, out_type=jax.ShapeDtypeStruct((s, t, w), table.dtype),
        scratch_types=[pltpu.VMEM((SC_CHUNK, w), table.dtype), pltpu.VMEM((s, SC_CHUNK), I32),
                       pltpu.SemaphoreType.DMA],
        name="combine")
    def k(table_hbm, pos_hbm, out_hbm, rows_v, idx_v, sem):
        wid = lax.axis_index("s") * n_cores + lax.axis_index("c")

        @pl.loop(0, iters)
        def _(it):
            c = it * n_workers + wid

            @pl.when(c < n_chunks)
            def _():
                base = pl.multiple_of(c * SC_CHUNK, SC_CHUNK)
                pltpu.sync_copy(pos_hbm.at[:, pl.ds(base, SC_CHUNK)], idx_v)
                for j in range(s):
                    pltpu.async_copy(table_hbm.at[idx_v.at[j]], rows_v, sem).wait()
                    pltpu.sync_copy(rows_v, out_hbm.at[j, pl.ds(base, SC_CHUNK)])

    return k(table, pos_t)


def _sc_pack_weights(w, rows_per_item, after=()):
    e, r, c = w.shape
    half = r // 2
    rb = rows_per_item
    per_expert = half // rb
    n_cores, n_workers = _sc_workers()
    per_worker = e * per_expert // n_workers
    assert per_worker * n_workers == e * per_expert and per_worker % 2 == 0 and c % (SC_LANES * SC_UNROLL) == 0
    mesh = plsc.VectorSubcoreMesh(core_axis_name="c", subcore_axis_name="s")

    @functools.partial(
        pl.kernel, mesh=mesh, out_type=jax.ShapeDtypeStruct((e * half, c), U32),
        scratch_types=[pltpu.VMEM((2, rb, c), F32), pltpu.VMEM((2, rb, c), F32), pltpu.VMEM((2, rb, c), U32),
                       pltpu.SemaphoreType.DMA((2,)), pltpu.SemaphoreType.DMA((2,))],
        compiler_params=pltpu.CompilerParams(needs_layout_passes=False),
        cost_estimate=pl.CostEstimate(flops=e * r * c, transcendentals=0, bytes_accessed=6 * e * r * c),
        name="pack_weights")
    def k(w_hbm, *refs):
        out_hbm, lo_v, hi_v, out_v, in_sem, out_sem = refs[len(after):]
        wid = lax.axis_index("s") * n_cores + lax.axis_index("c")
        first = wid * per_worker

        def rows(item):
            ex = item // per_expert
            j = item - ex * per_expert
            return (pl.multiple_of(ex * r + j * rb, rb), pl.multiple_of(ex * r + half + j * rb, rb),
                    pl.multiple_of(ex * half + j * rb, rb))

        def loads(item, b):
            lo_row, hi_row, _ = rows(item)
            return (pltpu.make_async_copy(w_hbm.at[pl.ds(lo_row, rb)], lo_v.at[b], in_sem.at[b]),
                    pltpu.make_async_copy(w_hbm.at[pl.ds(hi_row, rb)], hi_v.at[b], in_sem.at[b]))

        def store(item, b):
            return pltpu.make_async_copy(out_v.at[b], out_hbm.at[pl.ds(rows(item)[2], rb)], out_sem.at[b])

        for cp in loads(first, 0):
            cp.start()

        @pl.loop(0, per_worker // 2)
        def _(pair):
            for b in range(2):
                item = first + pair * 2 + b
                for cp in loads(item, b):
                    cp.wait()

                @pl.when(item + 1 < first + per_worker)
                def _():
                    for cp in loads(item + 1, 1 - b):
                        cp.start()

                @pl.when(pair > 0)
                def _():
                    store(item - 2, b).wait()

                @pl.loop(0, rb)
                def _(i):
                    @pl.loop(0, c // (SC_LANES * SC_UNROLL))
                    def _(vb):
                        for u in range(SC_UNROLL):
                            sl = pl.ds(pl.multiple_of((vb * SC_UNROLL + u) * SC_LANES, SC_LANES), SC_LANES)
                            packed = plsc.pack(lo_v[b, i, sl], hi_v[b, i, sl], format=plsc.PackFormat.INTERLEAVED)
                            out_v[b, i, sl] = plsc.bitcast(packed, U32)

                store(item, b).start()

        for b in range(2):
            store(first + per_worker - 2 + b, b).wait()

    return k(w.reshape(e * r, c), *after).reshape(e, half, c)


def _experts_kernel(first_ref, ntile_ref, cnt_ref, xs_hbm, wg_ref, wu_ref, wd_ref, ys_hbm,
                    wg16_ref, wu16_ref, wd16_ref, x_buf, y_buf, in_sem, out_sem):
    ahead = STREAM_DEPTH - MAX_WIDTH
    e = pl.program_id(0)
    n_used = first_ref[N_EXPERTS - 1] + ntile_ref[N_EXPERTS - 1]
    first, n_mine, count = first_ref[e], ntile_ref[e], cnt_ref[e]

    def tile_rows(g):
        return pl.ds(pl.multiple_of(g * ROW_TILE, ROW_TILE), ROW_TILE)

    def load(g):
        slot = lax.rem(g, STREAM_DEPTH)
        return pltpu.make_async_copy(xs_hbm.at[tile_rows(g)], x_buf.at[slot], in_sem.at[slot])

    def store(g):
        slot = lax.rem(g, STREAM_DEPTH)
        return pltpu.make_async_copy(y_buf.at[slot], ys_hbm.at[tile_rows(g)], out_sem.at[slot])

    @pl.when(e == 0)
    def _():
        for g0 in range(ahead):
            @pl.when(g0 < n_used)
            def _():
                load(g0).start()

    for packed_ref, w16_ref in ((wg_ref, wg16_ref), (wu_ref, wu16_ref), (wd_ref, wd16_ref)):
        rows = packed_ref.shape[1]
        lo, hi = _unpack_rows(packed_ref[0])
        w16_ref[0:rows, :] = lo.astype(BF16)
        w16_ref[rows:, :] = hi.astype(BF16)

    def run(j, width):
        tiles = [first + j + t for t in range(width)]
        for g in tiles:
            load(g).wait()

            @pl.when(g + ahead < n_used)
            def _():
                load(g + ahead).start()

            @pl.when(g >= STREAM_DEPTH)
            def _():
                store(g - STREAM_DEPTH).wait()

        words = jnp.concatenate([x_buf[lax.rem(g, STREAM_DEPTH)] for g in tiles], axis=0)
        row = lax.broadcasted_iota(I32, words.shape, 0)
        words = jnp.where(row < count - j * ROW_TILE, words, jnp.uint32(0))
        lo, hi = _unpack_rows(words)
        lo, hi = lo.astype(BF16), hi.astype(BF16)
        hg = _dot(lo, wg16_ref[0:HALF, :]) + _dot(hi, wg16_ref[HALF:, :])
        hu = _dot(lo, wu16_ref[0:HALF, :]) + _dot(hi, wu16_ref[HALF:, :])
        a = (_silu(hg) * hu).astype(BF16)
        y = _pack_rows(_dot(a, wd16_ref[...]))
        for t, g in enumerate(tiles):
            y_buf[lax.rem(g, STREAM_DEPTH)] = y[t * ROW_TILE:(t + 1) * ROW_TILE]
            store(g).start()

    def widest(p, carry):
        run(MAX_WIDTH * p, MAX_WIDTH)
        return carry

    lax.fori_loop(0, n_mine // MAX_WIDTH, widest, 0)
    done = n_mine - lax.rem(n_mine, MAX_WIDTH)
    width = MAX_WIDTH // 2
    while width:
        has = lax.rem(n_mine // width, 2) == 1

        @pl.when(has)
        def _(width=width, done=done):
            run(done, width)

        done = done + jnp.where(has, width, 0)
        width //= 2

    @pl.when(e == N_EXPERTS - 1)
    def _():
        for back in range(STREAM_DEPTH, 0, -1):
            @pl.when(n_used >= back)
            def _():
                store(n_used - back).wait()


def _experts(xs, first_tile, n_tile, count, w_eg, w_eu, w_ed):
    d = D_MODEL
    by_expert = lambda e, *_: (e, 0, 0)
    grid_spec = pltpu.PrefetchScalarGridSpec(
        num_scalar_prefetch=3,
        grid=(N_EXPERTS,),
        in_specs=[pl.BlockSpec(memory_space=pl.ANY),
                  pl.BlockSpec((1, d // 2, EXPERT_DIM), by_expert),
                  pl.BlockSpec((1, d // 2, EXPERT_DIM), by_expert),
                  pl.BlockSpec((1, EXPERT_DIM // 2, d), by_expert)],
        out_specs=pl.BlockSpec(memory_space=pl.ANY),
        scratch_shapes=[pltpu.VMEM((d, EXPERT_DIM), BF16), pltpu.VMEM((d, EXPERT_DIM), BF16),
                        pltpu.VMEM((EXPERT_DIM, d), BF16),
                        pltpu.VMEM((STREAM_DEPTH, ROW_TILE, HALF), U32),
                        pltpu.VMEM((STREAM_DEPTH, ROW_TILE, HALF), U32),
                        pltpu.SemaphoreType.DMA((STREAM_DEPTH,)), pltpu.SemaphoreType.DMA((STREAM_DEPTH,))])
    return pl.pallas_call(
        _experts_kernel,
        grid_spec=grid_spec,
        out_shape=jax.ShapeDtypeStruct(xs.shape, U32),
        compiler_params=pltpu.CompilerParams(dimension_semantics=("arbitrary",),
                                             vmem_limit_bytes=VMEM_LIMIT),
        name="experts",
    )(first_tile, n_tile, count, xs, w_eg, w_eu, w_ed)


def _final_kernel(z_ref, gatew_ref, h2_ref, x1_ref, mod_ref, normf_ref, wsg_ref, wsu_ref, wsd_ref, *rest):
    y_ref, wsg16_ref, wsu16_ref, wsd16_ref = rest[-4:]

    @pl.when(pl.program_id(0) == 0)
    def _():
        wsg16_ref[...] = wsg_ref[...].astype(BF16)
        wsu16_ref[...] = wsu_ref[...].astype(BF16)
        wsd16_ref[...] = wsd_ref[...].astype(BF16)

    lo, hi = _unpack_rows(h2_ref[...])
    h = jnp.concatenate([lo, hi], axis=-1).astype(BF16)
    a = _silu(_dot(h, wsg16_ref[...])) * _dot(h, wsu16_ref[...])
    acc = _dot(a.astype(BF16), wsd16_ref[...])
    for s in range(TOP_K):
        lo, hi = _unpack_rows(z_ref[s])
        acc = acc + gatew_ref[:, s:s + 1] * jnp.concatenate([lo, hi], axis=-1)
    x2 = x1_ref[...] + _mod(mod_ref, 5) * acc
    y_ref[...] = _rms(x2) * normf_ref[...]


def _final(z, gate_w, h2, x1, mod, norm_f, w_sg, w_su, w_sd, block_t, first_block, per_seq,
           seq0=0, out_rows=None, y_prev=None, after=None):
    t, d = x1.shape
    out_rows = t if out_rows is None else out_rows
    out_first = seq0 * per_seq
    tok = lambda i: (i, 0)
    if per_seq:
        mod_spec = pl.BlockSpec((1, 6, d), lambda i: (seq0 + i // per_seq, 0, 0))
    else:
        mod_spec = pl.BlockSpec((block_t, 6 * d), tok)
    operands = [z, gate_w, h2, x1, mod, norm_f, w_sg, w_su, w_sd]
    in_specs = [pl.BlockSpec((TOP_K, block_t, HALF), lambda i: (0, first_block + i, 0)),
                pl.BlockSpec((block_t, LANES), tok),
                pl.BlockSpec((block_t, HALF), tok),
                pl.BlockSpec((block_t, d), tok),
                mod_spec,
                _full((1, d)),
                _full((d, EXPERT_DIM)), _full((d, EXPERT_DIM)), _full((EXPERT_DIM, d))]
    aliases = {}
    if y_prev is not None:
        aliases = {len(operands): 0}
        operands.append(y_prev)
        in_specs.append(pl.BlockSpec(memory_space=pl.ANY))
    if after is not None:
        operands.append(after)
        in_specs.append(pl.BlockSpec(memory_space=pl.ANY))
    return pl.pallas_call(
        _final_kernel,
        grid=(t // block_t,),
        in_specs=in_specs,
        out_specs=pl.BlockSpec((block_t, d), lambda i: (out_first + i, 0)),
        out_shape=jax.ShapeDtypeStruct((out_rows, d), F32),
        scratch_shapes=[pltpu.VMEM((d, EXPERT_DIM), BF16), pltpu.VMEM((d, EXPERT_DIM), BF16),
                        pltpu.VMEM((EXPERT_DIM, d), BF16)],
        input_output_aliases=aliases,
        compiler_params=pltpu.CompilerParams(dimension_semantics=("arbitrary",),
                                             vmem_limit_bytes=VMEM_LIMIT),
        name="final",
    )(*operands)


def kernel(x_prompt, x_sample, c_prompt, c_sample, state_ret, state_pool, norm1, norm2, norm_f,
           w_ada, b_ada, w_in, w_out, w_pool, pool_scale, w_router, router_bias, w_exp_gate,
           w_exp_up, w_exp_down, w_sh_gate, w_sh_up, w_sh_down):
    b, l, d = x_prompt.shape
    n = x_sample.shape[0]

    mod_p, mod_s = _ada(c_prompt, c_sample, w_ada[0], b_ada[0])
    mod_p = mod_p.reshape(b, 6, d)

    w_in16 = w_in[0].astype(BF16)
    w_out16 = w_out[0].astype(BF16)
    wr_t = w_router[0].T
    bias_t = jnp.broadcast_to(router_bias[0][:, None], (N_EXPERTS, LANES))
    n1, n2, nf = norm1[0].reshape(1, d), norm2[0].reshape(1, d), norm_f.reshape(1, d)
    ps = pool_scale[0].reshape(1, POOL_WIDTH)
    shared = (w_sh_gate[0], w_sh_up[0], w_sh_down[0])

    def routed(sources, experts):
        n_tiles = experts.shape[1] * TOP_K // ROW_TILE + N_EXPERTS
        pos_t, meta = _plan(experts)
        xs = _sc_scatter_rows(sources, pos_t, n_tiles * ROW_TILE, after=expert_w)
        ys = _experts(xs, meta[:, 0], meta[:, 1], meta[:, 2], *expert_w)
        return _sc_gather_rows(ys, pos_t)

    mix_args = (n1, w_in16, w_pool[0], ps, w_out16, n2, wr_t, bias_t)
    x1_s, h2_s, experts_s, gatew_s, ret_s, pool_s = _mix_sample(
        x_sample.reshape(n, d), mod_s, state_ret[0], state_pool[0], float(PAST_LEN), *mix_args)

    expert_w = (_sc_pack_weights(w_exp_gate[0], 64, after=(x1_s,)),
                _sc_pack_weights(w_exp_up[0], 64, after=(x1_s,)),
                _sc_pack_weights(w_exp_down[0], 16, after=(x1_s,)))

    ba = b // 2
    bb = b - ba
    x1_a, h2_a, experts_a, gatew_a, ret_a, pool_a = _mix_prompt(x_prompt, mod_p, 0, ba, *mix_args)
    z_a = routed((h2_a,), experts_a)
    x1_b, h2_b, experts_b, gatew_b, ret_b, pool_b = _mix_prompt(x_prompt, mod_p, ba, bb, *mix_args)
    z_b = routed((h2_b, h2_s), jnp.concatenate([experts_b, experts_s], axis=1))

    block_t = 512
    per_seq = l // block_t
    y_p = _final(z_b, gatew_b, h2_b, x1_b.reshape(bb * l, d), mod_p, nf, *shared,
                 block_t=block_t, first_block=0, per_seq=per_seq, seq0=ba, out_rows=b * l)
    y_s = _final(z_b, gatew_s, h2_s, x1_s, mod_s, nf, *shared,
                 block_t=n, first_block=bb * l // n, per_seq=0, after=y_p)
    y_p = _final(z_a, gatew_a, h2_a, x1_a.reshape(ba * l, d), mod_p, nf, *shared,
                 block_t=block_t, first_block=0, per_seq=per_seq, seq0=0, out_rows=b * l, y_prev=y_p,
                 after=y_s)

    ret_p = jnp.concatenate([ret_a, ret_b], axis=0)
    pool_p = jnp.concatenate([pool_a, pool_b], axis=0)
    return (y_p.reshape(b, l, d), y_s.reshape(n, 1, d), ret_p[None], pool_p[None],
            ret_s[None], pool_s[None])
```

```python
import functools

import jax
import jax.numpy as jnp
import numpy as np
from jax import lax
from jax.experimental import pallas as pl
from jax.experimental.pallas import tpu as pltpu
from jax.experimental.pallas import tpu_sc as plsc

D_MODEL = 1024
RET_HEADS = 4
RET_QK_DIM = 64
RET_V_DIM = 128
RET_WIDTH = RET_HEADS * RET_V_DIM
QK_WIDTH = RET_HEADS * RET_QK_DIM
ROPE_BASE = 10000.0
POOL_WINDOWS = (2, 4, 8, 16)
POOL_WIDTH = 512
POOL_GROUP_DIM = 128
POOL_BUF = 15
IN_WIDTH = 2 * QK_WIDTH + 2 * RET_WIDTH + POOL_WIDTH
N_EXPERTS = 64
TOP_K = 8
N_EXPERT_GROUPS = 8
GROUP_SIZE = N_EXPERTS // N_EXPERT_GROUPS
TOP_GROUPS = 4
EXPERT_DIM = 256
ROUTE_SCALE = 2.5
EPS = 1e-6
PAST_LEN = 16384

LANES = 128
SUBLANES = 8
POOL_CARRY = 24
VMEM_LIMIT = 56 * 1024 * 1024
HALF = D_MODEL // 2
ROW_TILE = 256
MAX_WIDTH = 2
SC_CHUNK = 128
SC_LANES = 16
SC_UNROLL = 16
STREAM_DEPTH = 16

BF16 = jnp.bfloat16
F32 = jnp.float32
U32 = jnp.uint32
I32 = jnp.int32


def _silu(x):
    return x * jax.nn.sigmoid(x)


def _dot(a, b):
    return jnp.dot(a, b, preferred_element_type=F32)


def _rms(x):
    return x * lax.rsqrt(jnp.mean(x * x, axis=-1, keepdims=True) + EPS)


def _mod(mod_ref, i, seq=0):
    if len(mod_ref.shape) == 3:
        return mod_ref[seq, i:i + 1, :]
    return mod_ref[:, i * D_MODEL:(i + 1) * D_MODEL]


def _split_bf16(x):
    hi = x.astype(BF16)
    lo = (x - hi.astype(F32)).astype(BF16)
    return hi, lo


def _pack_rows(x):
    lo = lax.bitcast_convert_type(x[:, :HALF].astype(BF16).astype(F32), U32)
    hi = lax.bitcast_convert_type(x[:, HALF:].astype(BF16).astype(F32), U32)
    return (hi & jnp.uint32(0xFFFF0000)) | (lo >> jnp.uint32(16))


def _unpack_rows(w):
    lo = lax.bitcast_convert_type(w << jnp.uint32(16), F32)
    hi = lax.bitcast_convert_type(w & jnp.uint32(0xFFFF0000), F32)
    return lo, hi


def _first_max_onehot(work, idx, n):
    m = jnp.max(work, axis=0, keepdims=True)
    first = jnp.min(jnp.where(work == m, idx, float(n)), axis=0, keepdims=True)
    return idx == first


def _route(h2, wr_t_ref, bias_t_ref):
    n = h2.shape[0]
    h_hi, h_lo = _split_bf16(h2)
    w_hi, w_lo = _split_bf16(wr_t_ref[...])
    nt = (((1,), (1,)), ((), ()))
    logits = (lax.dot_general(w_hi, h_hi, nt, preferred_element_type=F32)
              + lax.dot_general(w_hi, h_lo, nt, preferred_element_type=F32)
              + lax.dot_general(w_lo, h_hi, nt, preferred_element_type=F32))
    scores = jax.nn.sigmoid(logits)
    biased = scores + bias_t_ref[:, 0:1]
    b3 = biased.reshape(N_EXPERT_GROUPS, GROUP_SIZE, n)
    i3 = lax.broadcasted_iota(I32, b3.shape, 1).astype(F32)
    m1 = jnp.max(b3, axis=1, keepdims=True)
    first = jnp.min(jnp.where(b3 == m1, i3, float(GROUP_SIZE)), axis=1, keepdims=True)
    m2 = jnp.max(jnp.where(i3 == first, -jnp.inf, b3), axis=1, keepdims=True)
    gscore = (m1 + m2).reshape(N_EXPERT_GROUPS, n)
    gidx = lax.broadcasted_iota(I32, gscore.shape, 0).astype(F32)
    gsel = jnp.zeros(gscore.shape, F32)
    work = gscore
    for _ in range(TOP_GROUPS):
        hit = _first_max_onehot(work, gidx, N_EXPERT_GROUPS)
        gsel = jnp.where(hit, 1.0, gsel)
        work = jnp.where(hit, -jnp.inf, work)
    gsel3 = jnp.broadcast_to(gsel.reshape(N_EXPERT_GROUPS, 1, n), b3.shape)
    work = jnp.where(gsel3 > 0.0, b3, -jnp.inf).reshape(N_EXPERTS, n)
    eidx = lax.broadcasted_iota(I32, work.shape, 0).astype(F32)
    sel = jnp.zeros(work.shape, F32)
    for _ in range(TOP_K):
        hit = _first_max_onehot(work, eidx, N_EXPERTS)
        sel = jnp.where(hit, 1.0, sel)
        work = jnp.where(hit, -jnp.inf, work)
    picked = jnp.where(sel > 0.0, scores, 0.0)
    gates = picked / jnp.sum(picked, axis=0, keepdims=True) * ROUTE_SCALE
    below = (lax.broadcasted_iota(I32, (N_EXPERTS, N_EXPERTS), 1)
             < lax.broadcasted_iota(I32, (N_EXPERTS, N_EXPERTS), 0))
    slot = _dot(jnp.where(below, 1.0, 0.0).astype(BF16), sel.astype(BF16))
    e_rows, w_rows = [], []
    for s in range(TOP_K):
        here = jnp.where(slot == float(s), sel, 0.0)
        e_rows.append(jnp.sum(here * eidx, axis=0, keepdims=True))
        w_rows.append(jnp.sum(here * gates, axis=0, keepdims=True))
    experts = jnp.concatenate(e_rows, axis=0).astype(I32)
    w_t = jnp.concatenate(w_rows + [jnp.zeros((LANES - TOP_K, n), F32)], axis=0)
    counts = jnp.broadcast_to(jnp.sum(sel, axis=1, keepdims=True), (N_EXPERTS, LANES))
    return experts, w_t.T, counts


def _group_norm_gate(o, g):
    parts = []
    for h in range(RET_HEADS):
        oh = o[:, h * RET_V_DIM:(h + 1) * RET_V_DIM]
        mu = jnp.mean(oh, axis=-1, keepdims=True)
        ctr = oh - mu
        var = jnp.mean(ctr * ctr, axis=-1, keepdims=True)
        parts.append(ctr * lax.rsqrt(var + EPS))
    return _silu(g) * jnp.concatenate(parts, axis=-1)


def _pool_project(pooled, w_pool_ref, pool_scale_ref):
    parts = [_dot(p.astype(BF16), w_pool_ref[gi].astype(BF16)) for gi, p in enumerate(pooled)]
    return jnp.concatenate(parts, axis=-1) * pool_scale_ref[...]


def _out_residual(x, o_gated, p, mod_ref, seq, w_out_ref):
    mix = jnp.concatenate([o_gated, p], axis=-1).astype(BF16)
    return x + _mod(mod_ref, 2, seq) * _dot(mix, w_out_ref[...])


def _norm_route(x1, mod_ref, seq, norm2_ref, wr_t_ref, bias_t_ref, h2_ref, experts_ref, gatew_ref, cnt_ref):
    h2 = _rms(x1) * norm2_ref[...] * (1.0 + _mod(mod_ref, 4, seq)) + _mod(mod_ref, 3, seq)
    h2_ref[...] = _pack_rows(h2)
    experts, gate_w, counts = _route(h2, wr_t_ref, bias_t_ref)
    experts_ref[...] = experts
    gatew_ref[...] = gate_w
    cnt_ref[...] += counts


def _ada_kernel(cp_ref, cs_ref, w_ref, b_ref, op_ref, os_ref):
    w16 = w_ref[...].astype(BF16)
    for c_ref, o_ref in ((cp_ref, op_ref), (cs_ref, os_ref)):
        o_ref[...] = _dot(_silu(c_ref[...]).astype(BF16), w16) + b_ref[...]


def _ada(c_prompt, c_sample, w_ada, b_ada, block_n=1536):
    d, width = w_ada.shape
    rows = lambda c: pl.BlockSpec((c.shape[0], d), lambda j: (0, 0))
    cols = lambda c: pl.BlockSpec((c.shape[0], block_n), lambda j: (0, j))
    return pl.pallas_call(
        _ada_kernel,
        grid=(width // block_n,),
        in_specs=[rows(c_prompt), rows(c_sample),
                  pl.BlockSpec((d, block_n), lambda j: (0, j)),
                  pl.BlockSpec((1, block_n), lambda j: (0, j))],
        out_specs=[cols(c_prompt), cols(c_sample)],
        out_shape=[jax.ShapeDtypeStruct((c.shape[0], width), F32) for c in (c_prompt, c_sample)],
        compiler_params=pltpu.CompilerParams(vmem_limit_bytes=VMEM_LIMIT),
        name="ada",
    )(c_prompt, c_sample, w_ada, b_ada.reshape(1, width))


def _mix_prompt_kernel(x_ref, mod_ref, norm1_ref, w_in_ref, cos_ref, sin_ref, dmat_ref, cross_ref,
                       tail_ref, cdec_ref, w_pool_ref, pool_scale_ref, w_out_ref, norm2_ref,
                       wr_t_ref, bias_t_ref,
                       x1_ref, h2_ref, experts_ref, gatew_ref, cnt_ref, ret_ref, pool_ref,
                       state_ref, ext_ref, win_ref, o_ref, *, block_l, chunk, seqs):
    li = pl.program_id(1)

    @pl.when((pl.program_id(0) == 0) & (li == 0))
    def _():
        cnt_ref[...] = jnp.zeros_like(cnt_ref)

    @pl.when(li == 0)
    def _():
        state_ref[...] = jnp.zeros_like(state_ref)
        ext_ref[:, 0:POOL_CARRY, :] = jnp.zeros((seqs, POOL_CARRY, POOL_WIDTH), F32)
        win_ref[:, 0:SUBLANES, :] = jnp.zeros((seqs, SUBLANES, POOL_WIDTH), F32)

    for seq in range(seqs):
        _mix_prompt_seq(seq, li, x_ref, mod_ref, norm1_ref, w_in_ref, cos_ref, sin_ref, dmat_ref, cross_ref,
                        tail_ref, cdec_ref, w_pool_ref, pool_scale_ref, w_out_ref, norm2_ref,
                        wr_t_ref, bias_t_ref, x1_ref, h2_ref, experts_ref, gatew_ref, cnt_ref,
                        state_ref.at[seq], ext_ref.at[seq], win_ref.at[seq], o_ref.at[seq],
                        block_l=block_l, chunk=chunk)

    @pl.when(li == pl.num_programs(1) - 1)
    def _():
        ret_ref[...] = state_ref[...].reshape(ret_ref.shape)
        pool_ref[...] = ext_ref[:, POOL_CARRY - POOL_BUF:POOL_CARRY, :]


def _window_sums(ext_ref, win_ref, block_l):
    g = POOL_GROUP_DIM
    top = POOL_CARRY + block_l
    new = slice(POOL_CARRY - SUBLANES, None)
    s2 = ext_ref[SUBLANES:top, :] + ext_ref[SUBLANES - 1:top - 1, :]
    win_ref[SUBLANES:top, g:] = s2[:, g:]
    s4 = s2[:, g:] + win_ref[SUBLANES - 2:top - 2, g:]
    win_ref[SUBLANES:top, 2 * g:] = s4[:, g:]
    s8 = s4[:, g:] + win_ref[SUBLANES - 4:top - 4, 2 * g:]
    win_ref[SUBLANES:top, 3 * g:] = s8[:, g:]
    s16 = s8[:, g:] + win_ref[0:top - SUBLANES, 3 * g:]
    return [s2[new, 0:g], s4[new, 0:g], s8[new, 0:g], s16[new, :]]


def _mix_prompt_seq(seq, li, x_ref, mod_ref, norm1_ref, w_in_ref, cos_ref, sin_ref, dmat_ref, cross_ref,
                    tail_ref, cdec_ref, w_pool_ref, pool_scale_ref, w_out_ref, norm2_ref,
                    wr_t_ref, bias_t_ref, x1_ref, h2_ref, experts_ref, gatew_ref, cnt_ref,
                    state_ref, ext_ref, win_ref, o_ref, *, block_l, chunk):
    x = x_ref[seq]
    h16 = (_rms(x) * norm1_ref[...] * (1.0 + _mod(mod_ref, 1, seq)) + _mod(mod_ref, 0, seq)).astype(BF16)
    proj = _dot(h16, w_in_ref[...])
    q = proj[:, 0:QK_WIDTH]
    k = proj[:, QK_WIDTH:2 * QK_WIDTH]
    v = proj[:, 2 * QK_WIDTH:2 * QK_WIDTH + RET_WIDTH]
    g = proj[:, 2 * QK_WIDTH + RET_WIDTH:2 * QK_WIDTH + 2 * RET_WIDTH]
    u = proj[:, 2 * QK_WIDTH + 2 * RET_WIDTH:]

    lane = lax.broadcasted_iota(I32, q.shape, 1)
    first_half = (lane % RET_QK_DIM) < (RET_QK_DIM // 2)
    cos_t = cos_ref[...]
    sin_t = sin_ref[...]

    def rot(t):
        partner = jnp.where(first_half, pltpu.roll(t, QK_WIDTH - RET_QK_DIM // 2, axis=1),
                            pltpu.roll(t, RET_QK_DIM // 2, axis=1))
        return t * cos_t + partner * sin_t

    q = rot(q)
    k = rot(k) * (RET_QK_DIM ** -0.5)
    k_t = k.T
    v16 = v.astype(BF16)
    head_of_lane = lax.broadcasted_iota(I32, (chunk, QK_WIDTH), 1) // RET_QK_DIM

    for c in range(block_l // chunk):
        rows = slice(c * chunk, (c + 1) * chunk)
        q_c = q[rows]
        kt_c = k_t[:, rows]
        kt16 = kt_c.astype(BF16)
        state16 = state_ref[...].astype(BF16)
        for hd in range(RET_HEADS):
            in_head = head_of_lane == hd
            q_h = jnp.where(in_head, q_c, 0.0).astype(BF16)
            v_h = v16[rows, hd * RET_V_DIM:(hd + 1) * RET_V_DIM]
            scores = _dot(q_h, kt16) * dmat_ref[hd]
            inner = _dot(scores.astype(BF16), v_h)
            cross = _dot(q_h, state16) * cross_ref[hd]
            o_ref[rows, hd * RET_V_DIM:(hd + 1) * RET_V_DIM] = inner + cross
            hrows = slice(hd * RET_QK_DIM, (hd + 1) * RET_QK_DIM)
            k_dec = (kt_c[hrows] * tail_ref[hd:hd + 1, :]).astype(BF16)
            state_ref[hrows, :] = state_ref[hrows, :] * cdec_ref[hd] + _dot(k_dec, v_h)

    o_gated = _group_norm_gate(o_ref[...], g)

    ext_ref[POOL_CARRY:POOL_CARRY + block_l, :] = u
    pos = (li * block_l + lax.broadcasted_iota(I32, (block_l, 1), 0)).astype(F32)
    pooled = []
    for gi, (w, acc) in enumerate(zip(POOL_WINDOWS, _window_sums(ext_ref, win_ref, block_l))):
        cnt = jnp.minimum(pos + 1.0, float(w))
        pooled.append(acc / cnt - u[:, gi * POOL_GROUP_DIM:(gi + 1) * POOL_GROUP_DIM])
    p = _pool_project(pooled, w_pool_ref, pool_scale_ref)
    ext_ref[0:POOL_CARRY, :] = ext_ref[block_l:block_l + POOL_CARRY, :]

    x1 = _out_residual(x, o_gated, p, mod_ref, seq, w_out_ref)
    x1_ref[seq] = x1
    _norm_route(x1, mod_ref, seq, norm2_ref, wr_t_ref, bias_t_ref,
                h2_ref.at[seq], experts_ref.at[seq], gatew_ref.at[seq], cnt_ref)


def _decay_tables(chunk):
    f32 = np.float32
    lg = np.log(f32(1.0) - f32(2.0) ** (f32(-5.0) - np.arange(RET_HEADS, dtype=f32))).astype(f32)
    idx = np.arange(chunk, dtype=f32)
    diff = idx[:, None] - idx[None, :]
    causal = diff >= 0
    dmat = np.where(causal[None], np.exp(lg[:, None, None] * np.where(causal, diff, f32(0.0))[None]), f32(0.0))
    cross = np.exp(lg[:, None] * (idx[None, :] + f32(1.0)))
    cross = np.broadcast_to(cross[:, :, None], (RET_HEADS, chunk, RET_V_DIM))
    tail = np.exp(lg[:, None] * (f32(chunk - 1.0) - idx)[None, :])
    cdec = np.broadcast_to(np.exp(lg * f32(chunk))[:, None, None], (RET_HEADS, RET_QK_DIM, RET_V_DIM))
    return tuple(jnp.asarray(np.ascontiguousarray(t, dtype=f32)) for t in (dmat, cross, tail, cdec))


def _rotary_angles(pos):
    half = RET_QK_DIM // 2
    freqs = (np.float32(ROPE_BASE) ** (-np.arange(half, dtype=np.float32) / np.float32(half))).astype(np.float32)
    return np.asarray(pos, np.float32)[:, None] * freqs[None, :]


def _rotary_tables(pos):
    ang = _rotary_angles(pos)
    cos, sin = np.cos(ang), np.sin(ang)
    cos_t = np.tile(np.concatenate([cos, cos], axis=-1), (1, RET_HEADS))
    sin_t = np.tile(np.concatenate([-sin, sin], axis=-1), (1, RET_HEADS))
    return jnp.asarray(cos_t, F32), jnp.asarray(sin_t, F32)


def _full(shape):
    return pl.BlockSpec(shape, lambda *_: (0,) * len(shape))


def _mix_prompt(x, mod, b0, b, norm1, w_in16, w_pool, pool_scale, w_out16, norm2, wr_t, bias_t,
                block_l=512, chunk=256, seqs=2):
    _, l, d = x.shape
    nl = l // block_l
    s0 = b0 // seqs
    cos_t, sin_t = _rotary_tables(np.arange(l))
    dmat, cross, tail, cdec = _decay_tables(chunk)
    kernel = functools.partial(_mix_prompt_kernel, block_l=block_l, chunk=chunk, seqs=seqs)
    tok = lambda bi, li: (bi, li, 0)
    per_seq = lambda bi, li: (bi, 0, 0)
    x1, h2, experts, gate_w, counts, ret, pool = pl.pallas_call(
        kernel,
        grid=(b // seqs, nl),
        in_specs=[pl.BlockSpec((seqs, block_l, d), lambda bi, li: (s0 + bi, li, 0)),
                  pl.BlockSpec((seqs, 6, d), lambda bi, li: (s0 + bi, 0, 0)),
                  _full((1, d)),
                  _full((d, IN_WIDTH)),
                  pl.BlockSpec((block_l, QK_WIDTH), lambda bi, li: (li, 0)),
                  pl.BlockSpec((block_l, QK_WIDTH), lambda bi, li: (li, 0)),
                  _full(dmat.shape), _full(cross.shape), _full(tail.shape), _full(cdec.shape),
                  _full(w_pool.shape), _full((1, POOL_WIDTH)), _full((d, d)), _full((1, d)),
                  _full(wr_t.shape), _full(bias_t.shape)],
        out_specs=[pl.BlockSpec((seqs, block_l, d), tok),
                   pl.BlockSpec((seqs, block_l, HALF), tok),
                   pl.BlockSpec((seqs, TOP_K, block_l), lambda bi, li: (bi, 0, li)),
                   pl.BlockSpec((seqs, block_l, LANES), tok),
                   _full((N_EXPERTS, LANES)),
                   pl.BlockSpec((seqs, RET_HEADS, RET_QK_DIM, RET_V_DIM), lambda bi, li: (bi, 0, 0, 0)),
                   pl.BlockSpec((seqs, POOL_BUF, POOL_WIDTH), per_seq)],
        out_shape=[jax.ShapeDtypeStruct((b, l, d), F32),
                   jax.ShapeDtypeStruct((b, l, HALF), U32),
                   jax.ShapeDtypeStruct((b, TOP_K, l), I32),
                   jax.ShapeDtypeStruct((b, l, LANES), F32),
                   jax.ShapeDtypeStruct((N_EXPERTS, LANES), F32),
                   jax.ShapeDtypeStruct((b, RET_HEADS, RET_QK_DIM, RET_V_DIM), F32),
                   jax.ShapeDtypeStruct((b, POOL_BUF, POOL_WIDTH), F32)],
        scratch_shapes=[pltpu.VMEM((seqs, QK_WIDTH, RET_V_DIM), F32),
                        pltpu.VMEM((seqs, POOL_CARRY + block_l, POOL_WIDTH), F32),
                        pltpu.VMEM((seqs, POOL_CARRY + block_l, POOL_WIDTH), F32),
                        pltpu.VMEM((seqs, block_l, RET_WIDTH), F32)],
        compiler_params=pltpu.CompilerParams(dimension_semantics=("arbitrary", "arbitrary"),
                                             vmem_limit_bytes=VMEM_LIMIT),
        name="mix_prompt",
    )(x, mod, norm1, w_in16, cos_t, sin_t, dmat, cross, tail, cdec, w_pool, pool_scale,
      w_out16, norm2, wr_t, bias_t)
    experts = jnp.transpose(experts, (1, 0, 2)).reshape(TOP_K, b * l)
    return x1, h2.reshape(b * l, HALF), experts, gate_w.reshape(b * l, LANES), counts, ret, pool


def _mix_sample_front_kernel(x_ref, mod_ref, norm1_ref, w_in_ref, cos_ref, sin_ref,
                             qt_ref, kt_ref, v_ref, g_ref, u_ref):
    x = x_ref[...]
    h = _rms(x) * norm1_ref[...] * (1.0 + _mod(mod_ref, 1)) + _mod(mod_ref, 0)
    proj = _dot(h.astype(BF16), w_in_ref[...])
    half = RET_QK_DIM // 2
    cos_c = cos_ref[...]
    sin_c = sin_ref[...]

    def rot_t(t):
        parts = []
        for hd in range(RET_HEADS):
            t1 = t[hd * RET_QK_DIM:hd * RET_QK_DIM + half]
            t2 = t[hd * RET_QK_DIM + half:(hd + 1) * RET_QK_DIM]
            parts += [t1 * cos_c - t2 * sin_c, t1 * sin_c + t2 * cos_c]
        return jnp.concatenate(parts, axis=0)

    qt_ref[...] = rot_t(proj[:, 0:QK_WIDTH].T)
    kt_ref[...] = rot_t(proj[:, QK_WIDTH:2 * QK_WIDTH].T) * (RET_QK_DIM ** -0.5)
    v_ref[...] = proj[:, 2 * QK_WIDTH:2 * QK_WIDTH + RET_WIDTH]
    g_ref[...] = proj[:, 2 * QK_WIDTH + RET_WIDTH:2 * QK_WIDTH + 2 * RET_WIDTH]
    u_ref[...] = proj[:, 2 * QK_WIDTH + 2 * RET_WIDTH:]


def _ret_step_kernel(qt_ref, kt_ref, v_ref, s0_ref, o_ref, s1_ref, *, block_b, decays):
    i = pl.program_id(0)
    lane = lax.broadcasted_iota(I32, qt_ref.shape, 1)
    for j in range(block_b):
        bi = i * block_b + j
        here = lane == bi
        q_col = jnp.sum(jnp.where(here, qt_ref[...], 0.0), axis=1, keepdims=True)
        k_col = jnp.sum(jnp.where(here, kt_ref[...], 0.0), axis=1, keepdims=True)
        v_row = v_ref[pl.ds(bi, 1), :]
        outs = []
        for hd in range(RET_HEADS):
            hrows = slice(hd * RET_QK_DIM, (hd + 1) * RET_QK_DIM)
            s1 = decays[hd] * s0_ref[j, hd] + k_col[hrows] * v_row[:, hd * RET_V_DIM:(hd + 1) * RET_V_DIM]
            s1_ref[j, hd] = s1
            outs.append(jnp.sum(q_col[hrows] * s1, axis=0, keepdims=True))
        o_ref[pl.ds(bi, 1), :] = jnp.concatenate(outs, axis=-1)


def _mix_sample_back_kernel(x_ref, mod_ref, o_ref, g_ref, u_ref, buf_ref, w_pool_ref, pool_scale_ref,
                            w_out_ref, norm2_ref, wr_t_ref, bias_t_ref,
                            x1_ref, h2_ref, experts_ref, gatew_ref, cnt_ref, pool_ref):
    cnt_ref[...] = jnp.zeros_like(cnt_ref)
    o_gated = _group_norm_gate(o_ref[...], g_ref[...])
    u = u_ref[...]
    pooled = []
    for gi, w in enumerate(POOL_WINDOWS):
        lanes = slice(gi * POOL_GROUP_DIM, (gi + 1) * POOL_GROUP_DIM)
        acc = u[:, lanes]
        for j in range(1, w):
            acc = acc + buf_ref[:, POOL_BUF - j, lanes]
        pooled.append(acc / float(w) - u[:, lanes])
    p = _pool_project(pooled, w_pool_ref, pool_scale_ref)
    pool_ref[:, 0:POOL_BUF - 1, :] = buf_ref[:, 1:POOL_BUF, :]
    pool_ref[:, POOL_BUF - 1, :] = u
    x1 = _out_residual(x_ref[...], o_gated, p, mod_ref, 0, w_out_ref)
    x1_ref[...] = x1
    _norm_route(x1, mod_ref, 0, norm2_ref, wr_t_ref, bias_t_ref, h2_ref, experts_ref, gatew_ref, cnt_ref)


def _mix_sample(x, mod, state_ret, state_pool, start, norm1, w_in16, w_pool,
                pool_scale, w_out16, norm2, wr_t, bias_t, block_b=32):
    n, d = x.shape
    half = RET_QK_DIM // 2
    ang = _rotary_angles([start])
    cos_c = jnp.asarray(np.broadcast_to(np.cos(ang).T, (half, n)), F32)
    sin_c = jnp.asarray(np.broadcast_to(np.sin(ang).T, (half, n)), F32)
    params = pltpu.CompilerParams(vmem_limit_bytes=VMEM_LIMIT)
    qt, kt, v, g, u = pl.pallas_call(
        _mix_sample_front_kernel,
        out_shape=[jax.ShapeDtypeStruct((QK_WIDTH, n), F32), jax.ShapeDtypeStruct((QK_WIDTH, n), F32),
                   jax.ShapeDtypeStruct((n, RET_WIDTH), F32), jax.ShapeDtypeStruct((n, RET_WIDTH), F32),
                   jax.ShapeDtypeStruct((n, POOL_WIDTH), F32)],
        compiler_params=params,
        name="mix_sample_front",
    )(x, mod, norm1, w_in16, cos_c, sin_c)

    lg = np.log(1.0 - 2.0 ** (-5.0 - np.arange(RET_HEADS, dtype=np.float32)), dtype=np.float32)
    decays = tuple(float(np.exp(lg[h])) for h in range(RET_HEADS))
    state_block = (block_b, RET_HEADS, RET_QK_DIM, RET_V_DIM)
    o, s1 = pl.pallas_call(
        functools.partial(_ret_step_kernel, block_b=block_b, decays=decays),
        grid=(n // block_b,),
        in_specs=[_full((QK_WIDTH, n)), _full((QK_WIDTH, n)), _full((n, RET_WIDTH)),
                  pl.BlockSpec(state_block, lambda i: (i, 0, 0, 0))],
        out_specs=[_full((n, RET_WIDTH)), pl.BlockSpec(state_block, lambda i: (i, 0, 0, 0))],
        out_shape=[jax.ShapeDtypeStruct((n, RET_WIDTH), F32),
                   jax.ShapeDtypeStruct(state_ret.shape, F32)],
        compiler_params=pltpu.CompilerParams(dimension_semantics=("arbitrary",),
                                             vmem_limit_bytes=VMEM_LIMIT),
        name="ret_step",
    )(qt, kt, v, state_ret)

    x1, h2, experts, gate_w, counts, pool = pl.pallas_call(
        _mix_sample_back_kernel,
        out_shape=[jax.ShapeDtypeStruct((n, d), F32),
                   jax.ShapeDtypeStruct((n, HALF), U32),
                   jax.ShapeDtypeStruct((TOP_K, n), I32),
                   jax.ShapeDtypeStruct((n, LANES), F32),
                   jax.ShapeDtypeStruct((N_EXPERTS, LANES), F32),
                   jax.ShapeDtypeStruct(state_pool.shape, F32)],
        compiler_params=params,
        name="mix_sample_back",
    )(x, mod, o, g, u, state_pool, w_pool, pool_scale, w_out16, norm2, wr_t, bias_t)
    return x1, h2, experts, gate_w, counts, s1, pool


def _plan_kernel(experts_ref, cnt_ref, pos_ref, meta_ref, carry_ref, off_ref, *, block_t):
    @pl.when(pl.program_id(0) == 0)
    def _():
        cnt = cnt_ref[...]
        n_tile = jnp.floor((cnt + (ROW_TILE - 1.0)) * (1.0 / ROW_TILE))
        upto = (lax.broadcasted_iota(I32, (N_EXPERTS, N_EXPERTS), 1)
                <= lax.broadcasted_iota(I32, (N_EXPERTS, N_EXPERTS), 0))
        tile_end = _dot(jnp.where(upto, 1.0, 0.0).astype(BF16), n_tile.astype(BF16))
        tile_start = tile_end - n_tile
        off_ref[...] = tile_start * ROW_TILE
        carry_ref[...] = jnp.zeros_like(carry_ref)
        lane = lax.broadcasted_iota(I32, cnt.shape, 1)
        meta_ref[...] = jnp.where(lane == 0, tile_start, jnp.where(lane == 1, n_tile, cnt)).astype(I32)

    e_blk = experts_ref[...]
    eidx = lax.broadcasted_iota(I32, (N_EXPERTS, block_t), 0)
    member = jnp.zeros((N_EXPERTS, block_t), F32)
    for s in range(TOP_K):
        member = member + jnp.where(eidx == e_blk[s:s + 1, :], 1.0, 0.0)
    before = (lax.broadcasted_iota(I32, (block_t, block_t), 0)
              < lax.broadcasted_iota(I32, (block_t, block_t), 1))
    rank = _dot(member.astype(BF16), jnp.where(before, 1.0, 0.0).astype(BF16))
    row = off_ref[:, 0:1] + carry_ref[:, 0:1] + rank
    carry_ref[...] += jnp.broadcast_to(jnp.sum(member, axis=1, keepdims=True), (N_EXPERTS, LANES))
    out = [jnp.sum(jnp.where(eidx == e_blk[s:s + 1, :], row, 0.0), axis=0, keepdims=True)
           for s in range(TOP_K)]
    pos_ref[...] = jnp.concatenate(out, axis=0).astype(I32)


def _plan(experts_all, counts, max_block=1024):
    n_tokens = experts_all.shape[1]
    block_t = max(k for k in range(LANES, max_block + 1, LANES) if n_tokens % k == 0)
    return pl.pallas_call(
        functools.partial(_plan_kernel, block_t=block_t),
        grid=(n_tokens // block_t,),
        in_specs=[pl.BlockSpec((TOP_K, block_t), lambda j: (0, j)), _full((N_EXPERTS, LANES))],
        out_specs=[pl.BlockSpec((TOP_K, block_t), lambda j: (0, j)), _full((N_EXPERTS, LANES))],
        out_shape=[jax.ShapeDtypeStruct((TOP_K, n_tokens), I32),
                   jax.ShapeDtypeStruct((N_EXPERTS, LANES), I32)],
        scratch_shapes=[pltpu.VMEM((N_EXPERTS, LANES), F32)] * 2,
        compiler_params=pltpu.CompilerParams(dimension_semantics=("arbitrary",),
                                             vmem_limit_bytes=VMEM_LIMIT),
        name="plan",
    )(experts_all, counts)


def _sc_workers():
    info = plsc.get_sparse_core_info()
    return info.num_cores, info.num_cores * info.num_subcores


def _sc_scatter_rows(sources, pos_t, n_out, after=()):
    w = sources[0].shape[1]
    s = pos_t.shape[0]
    n_cores, n_workers = _sc_workers()
    bounds = np.cumsum([0] + [src.shape[0] // SC_CHUNK for src in sources])
    n_chunks = int(bounds[-1])
    iters = -(-n_chunks // n_workers)
    mesh = plsc.VectorSubcoreMesh(core_axis_name="c", subcore_axis_name="s")

    @functools.partial(
        pl.kernel, mesh=mesh, out_type=jax.ShapeDtypeStruct((n_out, w), sources[0].dtype),
        scratch_types=[pltpu.VMEM((SC_CHUNK, w), sources[0].dtype), pltpu.VMEM((s, SC_CHUNK), I32),
                       pltpu.SemaphoreType.DMA],
        name="dispatch")
    def k(*refs):
        src_hbm, pos_hbm = refs[:len(sources)], refs[len(sources)]
        out_hbm, rows_v, idx_v, sem = refs[len(sources) + 1 + len(after):]
        wid = lax.axis_index("s") * n_cores + lax.axis_index("c")

        @pl.loop(0, iters)
        def _(it):
            c = it * n_workers + wid
            for src, lo, hi in zip(src_hbm, bounds[:-1], bounds[1:]):
                @pl.when((c >= int(lo)) & (c < int(hi)))
                def _():
                    base = pl.multiple_of((c - int(lo)) * SC_CHUNK, SC_CHUNK)
                    pltpu.sync_copy(src.at[pl.ds(base, SC_CHUNK)], rows_v)

            @pl.when(c < n_chunks)
            def _():
                base = pl.multiple_of(c * SC_CHUNK, SC_CHUNK)
                pltpu.sync_copy(pos_hbm.at[:, pl.ds(base, SC_CHUNK)], idx_v)
                copies = [pltpu.async_copy(rows_v, out_hbm.at[idx_v.at[j]], sem) for j in range(s)]
                for cp in copies:
                    cp.wait()

    return k(*sources, pos_t, *after)


def _sc_gather_rows(table, pos_t):
    _, w = table.shape
    s, t = pos_t.shape
    n_cores, n_workers = _sc_workers()
    n_chunks = t // SC_CHUNK
    iters = -(-n_chunks // n_workers)
    mesh = plsc.VectorSubcoreMesh(core_axis_name="c", subcore_axis_name="s")

    @functools.partial(
        pl.kernel, mesh=mesh, out_type=jax.ShapeDtypeStruct((s, t, w), table.dtype),
        scratch_types=[pltpu.VMEM((SC_CHUNK, w), table.dtype), pltpu.VMEM((s, SC_CHUNK), I32),
                       pltpu.SemaphoreType.DMA],
        name="combine")
    def k(table_hbm, pos_hbm, out_hbm, rows_v, idx_v, sem):
        wid = lax.axis_index("s") * n_cores + lax.axis_index("c")

        @pl.loop(0, iters)
        def _(it):
            c = it * n_workers + wid

            @pl.when(c < n_chunks)
            def _():
                base = pl.multiple_of(c * SC_CHUNK, SC_CHUNK)
                pltpu.sync_copy(pos_hbm.at[:, pl.ds(base, SC_CHUNK)], idx_v)
                for j in range(s):
                    pltpu.async_copy(table_hbm.at[idx_v.at[j]], rows_v, sem).wait()
                    pltpu.sync_copy(rows_v, out_hbm.at[j, pl.ds(base, SC_CHUNK)])

    return k(table, pos_t)


def _sc_pack_weights(w, rows_per_item, after=()):
    e, r, c = w.shape
    half = r // 2
    rb = rows_per_item
    per_expert = half // rb
    n_cores, n_workers = _sc_workers()
    per_worker = e * per_expert // n_workers
    assert per_worker * n_workers == e * per_expert and per_worker % 2 == 0 and c % (SC_LANES * SC_UNROLL) == 0
    mesh = plsc.VectorSubcoreMesh(core_axis_name="c", subcore_axis_name="s")

    @functools.partial(
        pl.kernel, mesh=mesh, out_type=jax.ShapeDtypeStruct((e * half, c), U32),
        scratch_types=[pltpu.VMEM((2, rb, c), F32), pltpu.VMEM((2, rb, c), F32), pltpu.VMEM((2, rb, c), U32),
                       pltpu.SemaphoreType.DMA((2,)), pltpu.SemaphoreType.DMA((2,))],
        compiler_params=pltpu.CompilerParams(needs_layout_passes=False),
        cost_estimate=pl.CostEstimate(flops=e * r * c, transcendentals=0, bytes_accessed=6 * e * r * c),
        name="pack_weights")
    def k(w_hbm, *refs):
        out_hbm, lo_v, hi_v, out_v, in_sem, out_sem = refs[len(after):]
        wid = lax.axis_index("s") * n_cores + lax.axis_index("c")
        first = wid * per_worker

        def rows(item):
            ex = item // per_expert
            j = item - ex * per_expert
            return (pl.multiple_of(ex * r + j * rb, rb), pl.multiple_of(ex * r + half + j * rb, rb),
                    pl.multiple_of(ex * half + j * rb, rb))

        def loads(item, b):
            lo_row, hi_row, _ = rows(item)
            return (pltpu.make_async_copy(w_hbm.at[pl.ds(lo_row, rb)], lo_v.at[b], in_sem.at[b]),
                    pltpu.make_async_copy(w_hbm.at[pl.ds(hi_row, rb)], hi_v.at[b], in_sem.at[b]))

        def store(item, b):
            return pltpu.make_async_copy(out_v.at[b], out_hbm.at[pl.ds(rows(item)[2], rb)], out_sem.at[b])

        for cp in loads(first, 0):
            cp.start()

        @pl.loop(0, per_worker // 2)
        def _(pair):
            for b in range(2):
                item = first + pair * 2 + b
                for cp in loads(item, b):
                    cp.wait()

                @pl.when(item + 1 < first + per_worker)
                def _():
                    for cp in loads(item + 1, 1 - b):
                        cp.start()

                @pl.when(pair > 0)
                def _():
                    store(item - 2, b).wait()

                @pl.loop(0, rb)
                def _(i):
                    @pl.loop(0, c // (SC_LANES * SC_UNROLL))
                    def _(vb):
                        for u in range(SC_UNROLL):
                            sl = pl.ds(pl.multiple_of((vb * SC_UNROLL + u) * SC_LANES, SC_LANES), SC_LANES)
                            packed = plsc.pack(lo_v[b, i, sl], hi_v[b, i, sl], format=plsc.PackFormat.INTERLEAVED)
                            out_v[b, i, sl] = plsc.bitcast(packed, U32)

                store(item, b).start()

        for b in range(2):
            store(first + per_worker - 2 + b, b).wait()

    return k(w.reshape(e * r, c), *after).reshape(e, half, c)


def _experts_kernel(first_ref, ntile_ref, cnt_ref, xs_hbm, wg_ref, wu_ref, wd_ref, ys_hbm,
                    wg16_ref, wu16_ref, wd16_ref, x_buf, y_buf, in_sem, out_sem):
    ahead = STREAM_DEPTH - MAX_WIDTH
    e = pl.program_id(0)
    n_used = first_ref[N_EXPERTS - 1] + ntile_ref[N_EXPERTS - 1]
    first, n_mine, count = first_ref[e], ntile_ref[e], cnt_ref[e]

    def tile_rows(g):
        return pl.ds(pl.multiple_of(g * ROW_TILE, ROW_TILE), ROW_TILE)

    def load(g):
        slot = lax.rem(g, STREAM_DEPTH)
        return pltpu.make_async_copy(xs_hbm.at[tile_rows(g)], x_buf.at[slot], in_sem.at[slot])

    def store(g):
        slot = lax.rem(g, STREAM_DEPTH)
        return pltpu.make_async_copy(y_buf.at[slot], ys_hbm.at[tile_rows(g)], out_sem.at[slot])

    @pl.when(e == 0)
    def _():
        for g0 in range(ahead):
            @pl.when(g0 < n_used)
            def _():
                load(g0).start()

    for packed_ref, w16_ref in ((wg_ref, wg16_ref), (wu_ref, wu16_ref), (wd_ref, wd16_ref)):
        rows = packed_ref.shape[1]
        lo, hi = _unpack_rows(packed_ref[0])
        w16_ref[0:rows, :] = lo.astype(BF16)
        w16_ref[rows:, :] = hi.astype(BF16)

    def run(j, width):
        tiles = [first + j + t for t in range(width)]
        for g in tiles:
            load(g).wait()

            @pl.when(g + ahead < n_used)
            def _():
                load(g + ahead).start()

            @pl.when(g >= STREAM_DEPTH)
            def _():
                store(g - STREAM_DEPTH).wait()

        words = jnp.concatenate([x_buf[lax.rem(g, STREAM_DEPTH)] for g in tiles], axis=0)
        row = lax.broadcasted_iota(I32, words.shape, 0)
        words = jnp.where(row < count - j * ROW_TILE, words, jnp.uint32(0))
        lo, hi = _unpack_rows(words)
        lo, hi = lo.astype(BF16), hi.astype(BF16)
        hg = _dot(lo, wg16_ref[0:HALF, :]) + _dot(hi, wg16_ref[HALF:, :])
        hu = _dot(lo, wu16_ref[0:HALF, :]) + _dot(hi, wu16_ref[HALF:, :])
        a = (_silu(hg) * hu).astype(BF16)
        y = _pack_rows(_dot(a, wd16_ref[...]))
        for t, g in enumerate(tiles):
            y_buf[lax.rem(g, STREAM_DEPTH)] = y[t * ROW_TILE:(t + 1) * ROW_TILE]
            store(g).start()

    def widest(p, carry):
        run(MAX_WIDTH * p, MAX_WIDTH)
        return carry

    lax.fori_loop(0, n_mine // MAX_WIDTH, widest, 0)
    done = n_mine - lax.rem(n_mine, MAX_WIDTH)
    width = MAX_WIDTH // 2
    while width:
        has = lax.rem(n_mine // width, 2) == 1

        @pl.when(has)
        def _(width=width, done=done):
            run(done, width)

        done = done + jnp.where(has, width, 0)
        width //= 2

    @pl.when(e == N_EXPERTS - 1)
    def _():
        for back in range(STREAM_DEPTH, 0, -1):
            @pl.when(n_used >= back)
            def _():
                store(n_used - back).wait()


def _experts(xs, first_tile, n_tile, count, w_eg, w_eu, w_ed):
    d = D_MODEL
    by_expert = lambda e, *_: (e, 0, 0)
    grid_spec = pltpu.PrefetchScalarGridSpec(
        num_scalar_prefetch=3,
        grid=(N_EXPERTS,),
        in_specs=[pl.BlockSpec(memory_space=pl.ANY),
                  pl.BlockSpec((1, d // 2, EXPERT_DIM), by_expert),
                  pl.BlockSpec((1, d // 2, EXPERT_DIM), by_expert),
                  pl.BlockSpec((1, EXPERT_DIM // 2, d), by_expert)],
        out_specs=pl.BlockSpec(memory_space=pl.ANY),
        scratch_shapes=[pltpu.VMEM((d, EXPERT_DIM), BF16), pltpu.VMEM((d, EXPERT_DIM), BF16),
                        pltpu.VMEM((EXPERT_DIM, d), BF16),
                        pltpu.VMEM((STREAM_DEPTH, ROW_TILE, HALF), U32),
                        pltpu.VMEM((STREAM_DEPTH, ROW_TILE, HALF), U32),
                        pltpu.SemaphoreType.DMA((STREAM_DEPTH,)), pltpu.SemaphoreType.DMA((STREAM_DEPTH,))])
    return pl.pallas_call(
        _experts_kernel,
        grid_spec=grid_spec,
        out_shape=jax.ShapeDtypeStruct(xs.shape, U32),
        compiler_params=pltpu.CompilerParams(dimension_semantics=("arbitrary",),
                                             vmem_limit_bytes=VMEM_LIMIT),
        name="experts",
    )(first_tile, n_tile, count, xs, w_eg, w_eu, w_ed)


def _final_kernel(z_ref, gatew_ref, h2_ref, x1_ref, mod_ref, normf_ref, wsg_ref, wsu_ref, wsd_ref, *rest):
    y_ref, wsg16_ref, wsu16_ref, wsd16_ref = rest[-4:]

    @pl.when(pl.program_id(0) == 0)
    def _():
        wsg16_ref[...] = wsg_ref[...].astype(BF16)
        wsu16_ref[...] = wsu_ref[...].astype(BF16)
        wsd16_ref[...] = wsd_ref[...].astype(BF16)

    lo, hi = _unpack_rows(h2_ref[...])
    h = jnp.concatenate([lo, hi], axis=-1).astype(BF16)
    a = _silu(_dot(h, wsg16_ref[...])) * _dot(h, wsu16_ref[...])
    acc = _dot(a.astype(BF16), wsd16_ref[...])
    for s in range(TOP_K):
        lo, hi = _unpack_rows(z_ref[s])
        acc = acc + gatew_ref[:, s:s + 1] * jnp.concatenate([lo, hi], axis=-1)
    x2 = x1_ref[...] + _mod(mod_ref, 5) * acc
    y_ref[...] = _rms(x2) * normf_ref[...]


def _final(z, gate_w, h2, x1, mod, norm_f, w_sg, w_su, w_sd, block_t, first_block, per_seq,
           seq0=0, out_rows=None, y_prev=None):
    t, d = x1.shape
    out_rows = t if out_rows is None else out_rows
    out_first = seq0 * per_seq
    tok = lambda i: (i, 0)
    if per_seq:
        mod_spec = pl.BlockSpec((1, 6, d), lambda i: (seq0 + i // per_seq, 0, 0))
    else:
        mod_spec = pl.BlockSpec((block_t, 6 * d), tok)
    operands = [z, gate_w, h2, x1, mod, norm_f, w_sg, w_su, w_sd]
    in_specs = [pl.BlockSpec((TOP_K, block_t, HALF), lambda i: (0, first_block + i, 0)),
                pl.BlockSpec((block_t, LANES), tok),
                pl.BlockSpec((block_t, HALF), tok),
                pl.BlockSpec((block_t, d), tok),
                mod_spec,
                _full((1, d)),
                _full((d, EXPERT_DIM)), _full((d, EXPERT_DIM)), _full((EXPERT_DIM, d))]
    aliases = {}
    if y_prev is not None:
        aliases = {len(operands): 0}
        operands.append(y_prev)
        in_specs.append(pl.BlockSpec(memory_space=pl.ANY))
    return pl.pallas_call(
        _final_kernel,
        grid=(t // block_t,),
        in_specs=in_specs,
        out_specs=pl.BlockSpec((block_t, d), lambda i: (out_first + i, 0)),
        out_shape=jax.ShapeDtypeStruct((out_rows, d), F32),
        scratch_shapes=[pltpu.VMEM((d, EXPERT_DIM), BF16), pltpu.VMEM((d, EXPERT_DIM), BF16),
                        pltpu.VMEM((EXPERT_DIM, d), BF16)],
        input_output_aliases=aliases,
        compiler_params=pltpu.CompilerParams(dimension_semantics=("arbitrary",),
                                             vmem_limit_bytes=VMEM_LIMIT),
        name="final",
    )(*operands)


def kernel(x_prompt, x_sample, c_prompt, c_sample, state_ret, state_pool, norm1, norm2, norm_f,
           w_ada, b_ada, w_in, w_out, w_pool, pool_scale, w_router, router_bias, w_exp_gate,
           w_exp_up, w_exp_down, w_sh_gate, w_sh_up, w_sh_down):
    b, l, d = x_prompt.shape
    n = x_sample.shape[0]

    mod_p, mod_s = _ada(c_prompt, c_sample, w_ada[0], b_ada[0])
    mod_p = mod_p.reshape(b, 6, d)

    w_in16 = w_in[0].astype(BF16)
    w_out16 = w_out[0].astype(BF16)
    wr_t = w_router[0].T
    bias_t = jnp.broadcast_to(router_bias[0][:, None], (N_EXPERTS, LANES))
    n1, n2, nf = norm1[0].reshape(1, d), norm2[0].reshape(1, d), norm_f.reshape(1, d)
    ps = pool_scale[0].reshape(1, POOL_WIDTH)
    shared = (w_sh_gate[0], w_sh_up[0], w_sh_down[0])

    def routed(sources, experts, counts):
        n_tiles = experts.shape[1] * TOP_K // ROW_TILE + N_EXPERTS
        pos_t, meta = _plan(experts, counts)
        xs = _sc_scatter_rows(sources, pos_t, n_tiles * ROW_TILE, after=expert_w)
        ys = _experts(xs, meta[:, 0], meta[:, 1], meta[:, 2], *expert_w)
        return _sc_gather_rows(ys, pos_t)

    mix_args = (n1, w_in16, w_pool[0], ps, w_out16, n2, wr_t, bias_t)
    x1_s, h2_s, experts_s, gatew_s, counts_s, ret_s, pool_s = _mix_sample(
        x_sample.reshape(n, d), mod_s, state_ret[0], state_pool[0], float(PAST_LEN), *mix_args)

    expert_w = (_sc_pack_weights(w_exp_gate[0], 64, after=(x1_s,)),
                _sc_pack_weights(w_exp_up[0], 64, after=(x1_s,)),
                _sc_pack_weights(w_exp_down[0], 16, after=(x1_s,)))

    ba = b // 2
    bb = b - ba
    x1_a, h2_a, experts_a, gatew_a, counts_a, ret_a, pool_a = _mix_prompt(x_prompt, mod_p, 0, ba, *mix_args)
    z_a = routed((h2_a,), experts_a, counts_a)
    x1_b, h2_b, experts_b, gatew_b, counts_b, ret_b, pool_b = _mix_prompt(x_prompt, mod_p, ba, bb, *mix_args)
    z_b = routed((h2_b, h2_s), jnp.concatenate([experts_b, experts_s], axis=1), counts_b + counts_s)

    block_t = 512
    per_seq = l // block_t
    y_s = _final(z_b, gatew_s, h2_s, x1_s, mod_s, nf, *shared,
                 block_t=n, first_block=bb * l // n, per_seq=0)
    y_p = _final(z_b, gatew_b, h2_b, x1_b.reshape(bb * l, d), mod_p, nf, *shared,
                 block_t=block_t, first_block=0, per_seq=per_seq, seq0=ba, out_rows=b * l)
    y_p = _final(z_a, gatew_a, h2_a, x1_a.reshape(ba * l, d), mod_p, nf, *shared,
                 block_t=block_t, first_block=0, per_seq=per_seq, seq0=0, out_rows=b * l, y_prev=y_p)

    ret_p = jnp.concatenate([ret_a, ret_b], axis=0)
    pool_p = jnp.concatenate([pool_a, pool_b], axis=0)
    return (y_p.reshape(b, l, d), y_s.reshape(n, 1, d), ret_p[None], pool_p[None],
            ret_s[None], pool_s[None])
```

```python
import functools

import jax
import jax.numpy as jnp
import numpy as np
from jax import lax
from jax.experimental import pallas as pl
from jax.experimental.pallas import tpu as pltpu
from jax.experimental.pallas import tpu_sc as plsc

D_MODEL = 1024
RET_HEADS = 4
RET_QK_DIM = 64
RET_V_DIM = 128
RET_WIDTH = RET_HEADS * RET_V_DIM
QK_WIDTH = RET_HEADS * RET_QK_DIM
ROPE_BASE = 10000.0
POOL_WINDOWS = (2, 4, 8, 16)
POOL_WIDTH = 512
POOL_GROUP_DIM = 128
POOL_BUF = 15
IN_WIDTH = 2 * QK_WIDTH + 2 * RET_WIDTH + POOL_WIDTH
N_EXPERTS = 64
TOP_K = 8
N_EXPERT_GROUPS = 8
GROUP_SIZE = N_EXPERTS // N_EXPERT_GROUPS
TOP_GROUPS = 4
EXPERT_DIM = 256
ROUTE_SCALE = 2.5
EPS = 1e-6
PAST_LEN = 16384

LANES = 128
SUBLANES = 8
POOL_CARRY = 24
VMEM_LIMIT = 56 * 1024 * 1024
HALF = D_MODEL // 2
ROW_TILE = 256
MAX_WIDTH = 2
SC_CHUNK = 128
SC_LANES = 16
SC_UNROLL = 16
STREAM_DEPTH = 16

BF16 = jnp.bfloat16
F32 = jnp.float32
U32 = jnp.uint32
I32 = jnp.int32


def _silu(x):
    return x * jax.nn.sigmoid(x)


def _dot(a, b):
    return jnp.dot(a, b, preferred_element_type=F32)


def _rms(x):
    return x * lax.rsqrt(jnp.mean(x * x, axis=-1, keepdims=True) + EPS)


def _mod(mod_ref, i, seq=0):
    if len(mod_ref.shape) == 3:
        return mod_ref[seq, i:i + 1, :]
    return mod_ref[:, i * D_MODEL:(i + 1) * D_MODEL]


def _split_bf16(x):
    hi = x.astype(BF16)
    lo = (x - hi.astype(F32)).astype(BF16)
    return hi, lo


def _pack_rows(x):
    lo = lax.bitcast_convert_type(x[:, :HALF].astype(BF16).astype(F32), U32)
    hi = lax.bitcast_convert_type(x[:, HALF:].astype(BF16).astype(F32), U32)
    return (hi & jnp.uint32(0xFFFF0000)) | (lo >> jnp.uint32(16))


def _unpack_rows(w):
    lo = lax.bitcast_convert_type(w << jnp.uint32(16), F32)
    hi = lax.bitcast_convert_type(w & jnp.uint32(0xFFFF0000), F32)
    return lo, hi


def _first_max_onehot(work, idx, n):
    m = jnp.max(work, axis=0, keepdims=True)
    first = jnp.min(jnp.where(work == m, idx, float(n)), axis=0, keepdims=True)
    return idx == first


def _route(h2, wr_t_ref, bias_t_ref):
    n = h2.shape[0]
    h_hi, h_lo = _split_bf16(h2)
    w_hi, w_lo = _split_bf16(wr_t_ref[...])
    nt = (((1,), (1,)), ((), ()))
    logits = (lax.dot_general(w_hi, h_hi, nt, preferred_element_type=F32)
              + lax.dot_general(w_hi, h_lo, nt, preferred_element_type=F32)
              + lax.dot_general(w_lo, h_hi, nt, preferred_element_type=F32))
    scores = jax.nn.sigmoid(logits)
    biased = scores + bias_t_ref[:, 0:1]
    b3 = biased.reshape(N_EXPERT_GROUPS, GROUP_SIZE, n)
    i3 = lax.broadcasted_iota(I32, b3.shape, 1).astype(F32)
    m1 = jnp.max(b3, axis=1, keepdims=True)
    first = jnp.min(jnp.where(b3 == m1, i3, float(GROUP_SIZE)), axis=1, keepdims=True)
    m2 = jnp.max(jnp.where(i3 == first, -jnp.inf, b3), axis=1, keepdims=True)
    gscore = (m1 + m2).reshape(N_EXPERT_GROUPS, n)
    gidx = lax.broadcasted_iota(I32, gscore.shape, 0).astype(F32)
    gsel = jnp.zeros(gscore.shape, F32)
    work = gscore
    for _ in range(TOP_GROUPS):
        hit = _first_max_onehot(work, gidx, N_EXPERT_GROUPS)
        gsel = jnp.where(hit, 1.0, gsel)
        work = jnp.where(hit, -jnp.inf, work)
    gsel3 = jnp.broadcast_to(gsel.reshape(N_EXPERT_GROUPS, 1, n), b3.shape)
    work = jnp.where(gsel3 > 0.0, b3, -jnp.inf).reshape(N_EXPERTS, n)
    eidx = lax.broadcasted_iota(I32, work.shape, 0).astype(F32)
    sel = jnp.zeros(work.shape, F32)
    for _ in range(TOP_K):
        hit = _first_max_onehot(work, eidx, N_EXPERTS)
        sel = jnp.where(hit, 1.0, sel)
        work = jnp.where(hit, -jnp.inf, work)
    picked = jnp.where(sel > 0.0, scores, 0.0)
    gates = picked / jnp.sum(picked, axis=0, keepdims=True) * ROUTE_SCALE
    below = (lax.broadcasted_iota(I32, (N_EXPERTS, N_EXPERTS), 1)
             < lax.broadcasted_iota(I32, (N_EXPERTS, N_EXPERTS), 0))
    slot = _dot(jnp.where(below, 1.0, 0.0).astype(BF16), sel.astype(BF16))
    e_rows, w_rows = [], []
    for s in range(TOP_K):
        here = jnp.where(slot == float(s), sel, 0.0)
        e_rows.append(jnp.sum(here * eidx, axis=0, keepdims=True))
        w_rows.append(jnp.sum(here * gates, axis=0, keepdims=True))
    experts = jnp.concatenate(e_rows, axis=0).astype(I32)
    w_t = jnp.concatenate(w_rows + [jnp.zeros((LANES - TOP_K, n), F32)], axis=0)
    counts = jnp.broadcast_to(jnp.sum(sel, axis=1, keepdims=True), (N_EXPERTS, LANES))
    return experts, w_t.T, counts


def _group_norm_gate(o, g):
    parts = []
    for h in range(RET_HEADS):
        oh = o[:, h * RET_V_DIM:(h + 1) * RET_V_DIM]
        mu = jnp.mean(oh, axis=-1, keepdims=True)
        ctr = oh - mu
        var = jnp.mean(ctr * ctr, axis=-1, keepdims=True)
        parts.append(ctr * lax.rsqrt(var + EPS))
    return _silu(g) * jnp.concatenate(parts, axis=-1)


def _pool_project(pooled, w_pool_ref, pool_scale_ref):
    parts = [_dot(p.astype(BF16), w_pool_ref[gi].astype(BF16)) for gi, p in enumerate(pooled)]
    return jnp.concatenate(parts, axis=-1) * pool_scale_ref[...]


def _out_residual(x, o_gated, p, mod_ref, seq, w_out_ref):
    mix = jnp.concatenate([o_gated, p], axis=-1).astype(BF16)
    return x + _mod(mod_ref, 2, seq) * _dot(mix, w_out_ref[...])


def _norm_route(x1, mod_ref, seq, norm2_ref, wr_t_ref, bias_t_ref, h2_ref, experts_ref, gatew_ref, cnt_ref):
    h2 = _rms(x1) * norm2_ref[...] * (1.0 + _mod(mod_ref, 4, seq)) + _mod(mod_ref, 3, seq)
    h2_ref[...] = _pack_rows(h2)
    experts, gate_w, counts = _route(h2, wr_t_ref, bias_t_ref)
    experts_ref[...] = experts
    gatew_ref[...] = gate_w
    cnt_ref[...] += counts


def _ada_kernel(cp_ref, cs_ref, w_ref, b_ref, op_ref, os_ref):
    w16 = w_ref[...].astype(BF16)
    for c_ref, o_ref in ((cp_ref, op_ref), (cs_ref, os_ref)):
        o_ref[...] = _dot(_silu(c_ref[...]).astype(BF16), w16) + b_ref[...]


def _ada(c_prompt, c_sample, w_ada, b_ada, block_n=1536):
    d, width = w_ada.shape
    rows = lambda c: pl.BlockSpec((c.shape[0], d), lambda j: (0, 0))
    cols = lambda c: pl.BlockSpec((c.shape[0], block_n), lambda j: (0, j))
    return pl.pallas_call(
        _ada_kernel,
        grid=(width // block_n,),
        in_specs=[rows(c_prompt), rows(c_sample),
                  pl.BlockSpec((d, block_n), lambda j: (0, j)),
                  pl.BlockSpec((1, block_n), lambda j: (0, j))],
        out_specs=[cols(c_prompt), cols(c_sample)],
        out_shape=[jax.ShapeDtypeStruct((c.shape[0], width), F32) for c in (c_prompt, c_sample)],
        compiler_params=pltpu.CompilerParams(vmem_limit_bytes=VMEM_LIMIT),
        name="ada",
    )(c_prompt, c_sample, w_ada, b_ada.reshape(1, width))


def _mix_prompt_kernel(x_ref, mod_ref, norm1_ref, w_in_ref, cos_ref, sin_ref, dmat_ref, cross_ref,
                       tail_ref, cdec_ref, w_pool_ref, pool_scale_ref, w_out_ref, norm2_ref,
                       wr_t_ref, bias_t_ref,
                       x1_ref, h2_ref, experts_ref, gatew_ref, cnt_ref, ret_ref, pool_ref,
                       state_ref, ext_ref, win_ref, o_ref, *, block_l, chunk, seqs):
    li = pl.program_id(1)

    @pl.when((pl.program_id(0) == 0) & (li == 0))
    def _():
        cnt_ref[...] = jnp.zeros_like(cnt_ref)

    @pl.when(li == 0)
    def _():
        state_ref[...] = jnp.zeros_like(state_ref)
        ext_ref[:, 0:POOL_CARRY, :] = jnp.zeros((seqs, POOL_CARRY, POOL_WIDTH), F32)
        win_ref[:, 0:SUBLANES, :] = jnp.zeros((seqs, SUBLANES, POOL_WIDTH), F32)

    for seq in range(seqs):
        _mix_prompt_seq(seq, li, x_ref, mod_ref, norm1_ref, w_in_ref, cos_ref, sin_ref, dmat_ref, cross_ref,
                        tail_ref, cdec_ref, w_pool_ref, pool_scale_ref, w_out_ref, norm2_ref,
                        wr_t_ref, bias_t_ref, x1_ref, h2_ref, experts_ref, gatew_ref, cnt_ref,
                        state_ref.at[seq], ext_ref.at[seq], win_ref.at[seq], o_ref.at[seq],
                        block_l=block_l, chunk=chunk)

    @pl.when(li == pl.num_programs(1) - 1)
    def _():
        ret_ref[...] = state_ref[...].reshape(ret_ref.shape)
        pool_ref[...] = ext_ref[:, POOL_CARRY - POOL_BUF:POOL_CARRY, :]


def _window_sums(ext_ref, win_ref, block_l):
    g = POOL_GROUP_DIM
    top = POOL_CARRY + block_l
    new = slice(POOL_CARRY - SUBLANES, None)
    s2 = ext_ref[SUBLANES:top, :] + ext_ref[SUBLANES - 1:top - 1, :]
    win_ref[SUBLANES:top, g:] = s2[:, g:]
    s4 = s2[:, g:] + win_ref[SUBLANES - 2:top - 2, g:]
    win_ref[SUBLANES:top, 2 * g:] = s4[:, g:]
    s8 = s4[:, g:] + win_ref[SUBLANES - 4:top - 4, 2 * g:]
    win_ref[SUBLANES:top, 3 * g:] = s8[:, g:]
    s16 = s8[:, g:] + win_ref[0:top - SUBLANES, 3 * g:]
    return [s2[new, 0:g], s4[new, 0:g], s8[new, 0:g], s16[new, :]]


def _mix_prompt_seq(seq, li, x_ref, mod_ref, norm1_ref, w_in_ref, cos_ref, sin_ref, dmat_ref, cross_ref,
                    tail_ref, cdec_ref, w_pool_ref, pool_scale_ref, w_out_ref, norm2_ref,
                    wr_t_ref, bias_t_ref, x1_ref, h2_ref, experts_ref, gatew_ref, cnt_ref,
                    state_ref, ext_ref, win_ref, o_ref, *, block_l, chunk):
    x = x_ref[seq]
    h16 = (_rms(x) * norm1_ref[...] * (1.0 + _mod(mod_ref, 1, seq)) + _mod(mod_ref, 0, seq)).astype(BF16)
    proj = _dot(h16, w_in_ref[...])
    q = proj[:, 0:QK_WIDTH]
    k = proj[:, QK_WIDTH:2 * QK_WIDTH]
    v = proj[:, 2 * QK_WIDTH:2 * QK_WIDTH + RET_WIDTH]
    g = proj[:, 2 * QK_WIDTH + RET_WIDTH:2 * QK_WIDTH + 2 * RET_WIDTH]
    u = proj[:, 2 * QK_WIDTH + 2 * RET_WIDTH:]

    lane = lax.broadcasted_iota(I32, q.shape, 1)
    first_half = (lane % RET_QK_DIM) < (RET_QK_DIM // 2)
    cos_t = cos_ref[...]
    sin_t = sin_ref[...]

    def rot(t):
        partner = jnp.where(first_half, pltpu.roll(t, QK_WIDTH - RET_QK_DIM // 2, axis=1),
                            pltpu.roll(t, RET_QK_DIM // 2, axis=1))
        return t * cos_t + partner * sin_t

    q = rot(q)
    k = rot(k) * (RET_QK_DIM ** -0.5)
    k_t = k.T
    v16 = v.astype(BF16)
    head_of_lane = lax.broadcasted_iota(I32, (chunk, QK_WIDTH), 1) // RET_QK_DIM

    for c in range(block_l // chunk):
        rows = slice(c * chunk, (c + 1) * chunk)
        q_c = q[rows]
        kt_c = k_t[:, rows]
        kt16 = kt_c.astype(BF16)
        state16 = state_ref[...].astype(BF16)
        for hd in range(RET_HEADS):
            in_head = head_of_lane == hd
            q_h = jnp.where(in_head, q_c, 0.0).astype(BF16)
            v_h = v16[rows, hd * RET_V_DIM:(hd + 1) * RET_V_DIM]
            scores = _dot(q_h, kt16) * dmat_ref[hd]
            inner = _dot(scores.astype(BF16), v_h)
            cross = _dot(q_h, state16) * cross_ref[hd]
            o_ref[rows, hd * RET_V_DIM:(hd + 1) * RET_V_DIM] = inner + cross
            hrows = slice(hd * RET_QK_DIM, (hd + 1) * RET_QK_DIM)
            k_dec = (kt_c[hrows] * tail_ref[hd:hd + 1, :]).astype(BF16)
            state_ref[hrows, :] = state_ref[hrows, :] * cdec_ref[hd] + _dot(k_dec, v_h)

    o_gated = _group_norm_gate(o_ref[...], g)

    ext_ref[POOL_CARRY:POOL_CARRY + block_l, :] = u
    pos = (li * block_l + lax.broadcasted_iota(I32, (block_l, 1), 0)).astype(F32)
    pooled = []
    for gi, (w, acc) in enumerate(zip(POOL_WINDOWS, _window_sums(ext_ref, win_ref, block_l))):
        cnt = jnp.minimum(pos + 1.0, float(w))
        pooled.append(acc / cnt - u[:, gi * POOL_GROUP_DIM:(gi + 1) * POOL_GROUP_DIM])
    p = _pool_project(pooled, w_pool_ref, pool_scale_ref)
    ext_ref[0:POOL_CARRY, :] = ext_ref[block_l:block_l + POOL_CARRY, :]

    x1 = _out_residual(x, o_gated, p, mod_ref, seq, w_out_ref)
    x1_ref[seq] = x1
    _norm_route(x1, mod_ref, seq, norm2_ref, wr_t_ref, bias_t_ref,
                h2_ref.at[seq], experts_ref.at[seq], gatew_ref.at[seq], cnt_ref)


def _decay_tables(chunk):
    f32 = np.float32
    lg = np.log(f32(1.0) - f32(2.0) ** (f32(-5.0) - np.arange(RET_HEADS, dtype=f32))).astype(f32)
    idx = np.arange(chunk, dtype=f32)
    diff = idx[:, None] - idx[None, :]
    causal = diff >= 0
    dmat = np.where(causal[None], np.exp(lg[:, None, None] * np.where(causal, diff, f32(0.0))[None]), f32(0.0))
    cross = np.exp(lg[:, None] * (idx[None, :] + f32(1.0)))
    cross = np.broadcast_to(cross[:, :, None], (RET_HEADS, chunk, RET_V_DIM))
    tail = np.exp(lg[:, None] * (f32(chunk - 1.0) - idx)[None, :])
    cdec = np.broadcast_to(np.exp(lg * f32(chunk))[:, None, None], (RET_HEADS, RET_QK_DIM, RET_V_DIM))
    return tuple(jnp.asarray(np.ascontiguousarray(t, dtype=f32)) for t in (dmat, cross, tail, cdec))


def _rotary_angles(pos):
    half = RET_QK_DIM // 2
    freqs = (np.float32(ROPE_BASE) ** (-np.arange(half, dtype=np.float32) / np.float32(half))).astype(np.float32)
    return np.asarray(pos, np.float32)[:, None] * freqs[None, :]


def _rotary_tables(pos):
    ang = _rotary_angles(pos)
    cos, sin = np.cos(ang), np.sin(ang)
    cos_t = np.tile(np.concatenate([cos, cos], axis=-1), (1, RET_HEADS))
    sin_t = np.tile(np.concatenate([-sin, sin], axis=-1), (1, RET_HEADS))
    return jnp.asarray(cos_t, F32), jnp.asarray(sin_t, F32)


def _full(shape):
    return pl.BlockSpec(shape, lambda *_: (0,) * len(shape))


def _mix_prompt(x, mod, b0, b, norm1, w_in16, w_pool, pool_scale, w_out16, norm2, wr_t, bias_t,
                block_l=512, chunk=256, seqs=2):
    _, l, d = x.shape
    nl = l // block_l
    s0 = b0 // seqs
    cos_t, sin_t = _rotary_tables(np.arange(l))
    dmat, cross, tail, cdec = _decay_tables(chunk)
    kernel = functools.partial(_mix_prompt_kernel, block_l=block_l, chunk=chunk, seqs=seqs)
    tok = lambda bi, li: (bi, li, 0)
    per_seq = lambda bi, li: (bi, 0, 0)
    x1, h2, experts, gate_w, counts, ret, pool = pl.pallas_call(
        kernel,
        grid=(b // seqs, nl),
        in_specs=[pl.BlockSpec((seqs, block_l, d), lambda bi, li: (s0 + bi, li, 0)),
                  pl.BlockSpec((seqs, 6, d), lambda bi, li: (s0 + bi, 0, 0)),
                  _full((1, d)),
                  _full((d, IN_WIDTH)),
                  pl.BlockSpec((block_l, QK_WIDTH), lambda bi, li: (li, 0)),
                  pl.BlockSpec((block_l, QK_WIDTH), lambda bi, li: (li, 0)),
                  _full(dmat.shape), _full(cross.shape), _full(tail.shape), _full(cdec.shape),
                  _full(w_pool.shape), _full((1, POOL_WIDTH)), _full((d, d)), _full((1, d)),
                  _full(wr_t.shape), _full(bias_t.shape)],
        out_specs=[pl.BlockSpec((seqs, block_l, d), tok),
                   pl.BlockSpec((seqs, block_l, HALF), tok),
                   pl.BlockSpec((seqs, TOP_K, block_l), lambda bi, li: (bi, 0, li)),
                   pl.BlockSpec((seqs, block_l, LANES), tok),
                   _full((N_EXPERTS, LANES)),
                   pl.BlockSpec((seqs, RET_HEADS, RET_QK_DIM, RET_V_DIM), lambda bi, li: (bi, 0, 0, 0)),
                   pl.BlockSpec((seqs, POOL_BUF, POOL_WIDTH), per_seq)],
        out_shape=[jax.ShapeDtypeStruct((b, l, d), F32),
                   jax.ShapeDtypeStruct((b, l, HALF), U32),
                   jax.ShapeDtypeStruct((b, TOP_K, l), I32),
                   jax.ShapeDtypeStruct((b, l, LANES), F32),
                   jax.ShapeDtypeStruct((N_EXPERTS, LANES), F32),
                   jax.ShapeDtypeStruct((b, RET_HEADS, RET_QK_DIM, RET_V_DIM), F32),
                   jax.ShapeDtypeStruct((b, POOL_BUF, POOL_WIDTH), F32)],
        scratch_shapes=[pltpu.VMEM((seqs, QK_WIDTH, RET_V_DIM), F32),
                        pltpu.VMEM((seqs, POOL_CARRY + block_l, POOL_WIDTH), F32),
                        pltpu.VMEM((seqs, POOL_CARRY + block_l, POOL_WIDTH), F32),
                        pltpu.VMEM((seqs, block_l, RET_WIDTH), F32)],
        compiler_params=pltpu.CompilerParams(dimension_semantics=("arbitrary", "arbitrary"),
                                             vmem_limit_bytes=VMEM_LIMIT),
        name="mix_prompt",
    )(x, mod, norm1, w_in16, cos_t, sin_t, dmat, cross, tail, cdec, w_pool, pool_scale,
      w_out16, norm2, wr_t, bias_t)
    experts = jnp.transpose(experts, (1, 0, 2)).reshape(TOP_K, b * l)
    return x1, h2.reshape(b * l, HALF), experts, gate_w.reshape(b * l, LANES), counts, ret, pool


def _mix_sample_front_kernel(x_ref, mod_ref, norm1_ref, w_in_ref, cos_ref, sin_ref,
                             qt_ref, kt_ref, v_ref, g_ref, u_ref):
    x = x_ref[...]
    h = _rms(x) * norm1_ref[...] * (1.0 + _mod(mod_ref, 1)) + _mod(mod_ref, 0)
    proj = _dot(h.astype(BF16), w_in_ref[...])
    half = RET_QK_DIM // 2
    cos_c = cos_ref[...]
    sin_c = sin_ref[...]

    def rot_t(t):
        parts = []
        for hd in range(RET_HEADS):
            t1 = t[hd * RET_QK_DIM:hd * RET_QK_DIM + half]
            t2 = t[hd * RET_QK_DIM + half:(hd + 1) * RET_QK_DIM]
            parts += [t1 * cos_c - t2 * sin_c, t1 * sin_c + t2 * cos_c]
        return jnp.concatenate(parts, axis=0)

    qt_ref[...] = rot_t(proj[:, 0:QK_WIDTH].T)
    kt_ref[...] = rot_t(proj[:, QK_WIDTH:2 * QK_WIDTH].T) * (RET_QK_DIM ** -0.5)
    v_ref[...] = proj[:, 2 * QK_WIDTH:2 * QK_WIDTH + RET_WIDTH]
    g_ref[...] = proj[:, 2 * QK_WIDTH + RET_WIDTH:2 * QK_WIDTH + 2 * RET_WIDTH]
    u_ref[...] = proj[:, 2 * QK_WIDTH + 2 * RET_WIDTH:]


def _ret_step_kernel(qt_ref, kt_ref, v_ref, s0_ref, o_ref, s1_ref, *, block_b, decays):
    i = pl.program_id(0)
    lane = lax.broadcasted_iota(I32, qt_ref.shape, 1)
    for j in range(block_b):
        bi = i * block_b + j
        here = lane == bi
        q_col = jnp.sum(jnp.where(here, qt_ref[...], 0.0), axis=1, keepdims=True)
        k_col = jnp.sum(jnp.where(here, kt_ref[...], 0.0), axis=1, keepdims=True)
        v_row = v_ref[pl.ds(bi, 1), :]
        outs = []
        for hd in range(RET_HEADS):
            hrows = slice(hd * RET_QK_DIM, (hd + 1) * RET_QK_DIM)
            s1 = decays[hd] * s0_ref[j, hd] + k_col[hrows] * v_row[:, hd * RET_V_DIM:(hd + 1) * RET_V_DIM]
            s1_ref[j, hd] = s1
            outs.append(jnp.sum(q_col[hrows] * s1, axis=0, keepdims=True))
        o_ref[pl.ds(bi, 1), :] = jnp.concatenate(outs, axis=-1)


def _mix_sample_back_kernel(x_ref, mod_ref, o_ref, g_ref, u_ref, buf_ref, w_pool_ref, pool_scale_ref,
                            w_out_ref, norm2_ref, wr_t_ref, bias_t_ref,
                            x1_ref, h2_ref, experts_ref, gatew_ref, cnt_ref, pool_ref):
    cnt_ref[...] = jnp.zeros_like(cnt_ref)
    o_gated = _group_norm_gate(o_ref[...], g_ref[...])
    u = u_ref[...]
    pooled = []
    for gi, w in enumerate(POOL_WINDOWS):
        lanes = slice(gi * POOL_GROUP_DIM, (gi + 1) * POOL_GROUP_DIM)
        acc = u[:, lanes]
        for j in range(1, w):
            acc = acc + buf_ref[:, POOL_BUF - j, lanes]
        pooled.append(acc / float(w) - u[:, lanes])
    p = _pool_project(pooled, w_pool_ref, pool_scale_ref)
    pool_ref[:, 0:POOL_BUF - 1, :] = buf_ref[:, 1:POOL_BUF, :]
    pool_ref[:, POOL_BUF - 1, :] = u
    x1 = _out_residual(x_ref[...], o_gated, p, mod_ref, 0, w_out_ref)
    x1_ref[...] = x1
    _norm_route(x1, mod_ref, 0, norm2_ref, wr_t_ref, bias_t_ref, h2_ref, experts_ref, gatew_ref, cnt_ref)


def _mix_sample(x, mod, state_ret, state_pool, start, norm1, w_in16, w_pool,
                pool_scale, w_out16, norm2, wr_t, bias_t, block_b=32):
    n, d = x.shape
    half = RET_QK_DIM // 2
    ang = _rotary_angles([start])
    cos_c = jnp.asarray(np.broadcast_to(np.cos(ang).T, (half, n)), F32)
    sin_c = jnp.asarray(np.broadcast_to(np.sin(ang).T, (half, n)), F32)
    params = pltpu.CompilerParams(vmem_limit_bytes=VMEM_LIMIT)
    qt, kt, v, g, u = pl.pallas_call(
        _mix_sample_front_kernel,
        out_shape=[jax.ShapeDtypeStruct((QK_WIDTH, n), F32), jax.ShapeDtypeStruct((QK_WIDTH, n), F32),
                   jax.ShapeDtypeStruct((n, RET_WIDTH), F32), jax.ShapeDtypeStruct((n, RET_WIDTH), F32),
                   jax.ShapeDtypeStruct((n, POOL_WIDTH), F32)],
        compiler_params=params,
        name="mix_sample_front",
    )(x, mod, norm1, w_in16, cos_c, sin_c)

    lg = np.log(1.0 - 2.0 ** (-5.0 - np.arange(RET_HEADS, dtype=np.float32)), dtype=np.float32)
    decays = tuple(float(np.exp(lg[h])) for h in range(RET_HEADS))
    state_block = (block_b, RET_HEADS, RET_QK_DIM, RET_V_DIM)
    o, s1 = pl.pallas_call(
        functools.partial(_ret_step_kernel, block_b=block_b, decays=decays),
        grid=(n // block_b,),
        in_specs=[_full((QK_WIDTH, n)), _full((QK_WIDTH, n)), _full((n, RET_WIDTH)),
                  pl.BlockSpec(state_block, lambda i: (i, 0, 0, 0))],
        out_specs=[_full((n, RET_WIDTH)), pl.BlockSpec(state_block, lambda i: (i, 0, 0, 0))],
        out_shape=[jax.ShapeDtypeStruct((n, RET_WIDTH), F32),
                   jax.ShapeDtypeStruct(state_ret.shape, F32)],
        compiler_params=pltpu.CompilerParams(dimension_semantics=("arbitrary",),
                                             vmem_limit_bytes=VMEM_LIMIT),
        name="ret_step",
    )(qt, kt, v, state_ret)

    x1, h2, experts, gate_w, counts, pool = pl.pallas_call(
        _mix_sample_back_kernel,
        out_shape=[jax.ShapeDtypeStruct((n, d), F32),
                   jax.ShapeDtypeStruct((n, HALF), U32),
                   jax.ShapeDtypeStruct((TOP_K, n), I32),
                   jax.ShapeDtypeStruct((n, LANES), F32),
                   jax.ShapeDtypeStruct((N_EXPERTS, LANES), F32),
                   jax.ShapeDtypeStruct(state_pool.shape, F32)],
        compiler_params=params,
        name="mix_sample_back",
    )(x, mod, o, g, u, state_pool, w_pool, pool_scale, w_out16, norm2, wr_t, bias_t)
    return x1, h2, experts, gate_w, counts, s1, pool


def _plan_kernel(experts_ref, cnt_ref, pos_ref, meta_ref, carry_ref, off_ref, *, block_t):
    @pl.when(pl.program_id(0) == 0)
    def _():
        cnt = cnt_ref[...]
        n_tile = jnp.floor((cnt + (ROW_TILE - 1.0)) * (1.0 / ROW_TILE))
        upto = (lax.broadcasted_iota(I32, (N_EXPERTS, N_EXPERTS), 1)
                <= lax.broadcasted_iota(I32, (N_EXPERTS, N_EXPERTS), 0))
        tile_end = _dot(jnp.where(upto, 1.0, 0.0).astype(BF16), n_tile.astype(BF16))
        tile_start = tile_end - n_tile
        off_ref[...] = tile_start * ROW_TILE
        carry_ref[...] = jnp.zeros_like(carry_ref)
        lane = lax.broadcasted_iota(I32, cnt.shape, 1)
        meta_ref[...] = jnp.where(lane == 0, tile_start, jnp.where(lane == 1, n_tile, cnt)).astype(I32)

    e_blk = experts_ref[...]
    eidx = lax.broadcasted_iota(I32, (N_EXPERTS, block_t), 0)
    member = jnp.zeros((N_EXPERTS, block_t), F32)
    for s in range(TOP_K):
        member = member + jnp.where(eidx == e_blk[s:s + 1, :], 1.0, 0.0)
    before = (lax.broadcasted_iota(I32, (block_t, block_t), 0)
              < lax.broadcasted_iota(I32, (block_t, block_t), 1))
    rank = _dot(member.astype(BF16), jnp.where(before, 1.0, 0.0).astype(BF16))
    row = off_ref[:, 0:1] + carry_ref[:, 0:1] + rank
    carry_ref[...] += jnp.broadcast_to(jnp.sum(member, axis=1, keepdims=True), (N_EXPERTS, LANES))
    out = [jnp.sum(jnp.where(eidx == e_blk[s:s + 1, :], row, 0.0), axis=0, keepdims=True)
           for s in range(TOP_K)]
    pos_ref[...] = jnp.concatenate(out, axis=0).astype(I32)


def _plan(experts_all, counts, max_block=1024):
    n_tokens = experts_all.shape[1]
    block_t = max(k for k in range(LANES, max_block + 1, LANES) if n_tokens % k == 0)
    return pl.pallas_call(
        functools.partial(_plan_kernel, block_t=block_t),
        grid=(n_tokens // block_t,),
        in_specs=[pl.BlockSpec((TOP_K, block_t), lambda j: (0, j)), _full((N_EXPERTS, LANES))],
        out_specs=[pl.BlockSpec((TOP_K, block_t), lambda j: (0, j)), _full((N_EXPERTS, LANES))],
        out_shape=[jax.ShapeDtypeStruct((TOP_K, n_tokens), I32),
                   jax.ShapeDtypeStruct((N_EXPERTS, LANES), I32)],
        scratch_shapes=[pltpu.VMEM((N_EXPERTS, LANES), F32)] * 2,
        compiler_params=pltpu.CompilerParams(dimension_semantics=("arbitrary",),
                                             vmem_limit_bytes=VMEM_LIMIT),
        name="plan",
    )(experts_all, counts)


def _sc_workers():
    info = plsc.get_sparse_core_info()
    return info.num_cores, info.num_cores * info.num_subcores


def _sc_scatter_rows(sources, pos_t, n_out, after=()):
    w = sources[0].shape[1]
    s = pos_t.shape[0]
    n_cores, n_workers = _sc_workers()
    bounds = np.cumsum([0] + [src.shape[0] // SC_CHUNK for src in sources])
    n_chunks = int(bounds[-1])
    iters = -(-n_chunks // n_workers)
    mesh = plsc.VectorSubcoreMesh(core_axis_name="c", subcore_axis_name="s")

    @functools.partial(
        pl.kernel, mesh=mesh, out_type=jax.ShapeDtypeStruct((n_out, w), sources[0].dtype),
        scratch_types=[pltpu.VMEM((SC_CHUNK, w), sources[0].dtype), pltpu.VMEM((s, SC_CHUNK), I32),
                       pltpu.SemaphoreType.DMA],
        name="dispatch")
    def k(*refs):
        src_hbm, pos_hbm = refs[:len(sources)], refs[len(sources)]
        out_hbm, rows_v, idx_v, sem = refs[len(sources) + 1 + len(after):]
        wid = lax.axis_index("s") * n_cores + lax.axis_index("c")

        @pl.loop(0, iters)
        def _(it):
            c = it * n_workers + wid
            for src, lo, hi in zip(src_hbm, bounds[:-1], bounds[1:]):
                @pl.when((c >= int(lo)) & (c < int(hi)))
                def _():
                    base = pl.multiple_of((c - int(lo)) * SC_CHUNK, SC_CHUNK)
                    pltpu.sync_copy(src.at[pl.ds(base, SC_CHUNK)], rows_v)

            @pl.when(c < n_chunks)
            def _():
                base = pl.multiple_of(c * SC_CHUNK, SC_CHUNK)
                pltpu.sync_copy(pos_hbm.at[:, pl.ds(base, SC_CHUNK)], idx_v)
                copies = [pltpu.async_copy(rows_v, out_hbm.at[idx_v.at[j]], sem) for j in range(s)]
                for cp in copies:
                    cp.wait()

    return k(*sources, pos_t, *after)


def _sc_gather_rows(table, pos_t):
    _, w = table.shape
    s, t = pos_t.shape
    n_cores, n_workers = _sc_workers()
    n_chunks = t // SC_CHUNK
    iters = -(-n_chunks // n_workers)
    mesh = plsc.VectorSubcoreMesh(core_axis_name="c", subcore_axis_name="s")

    @functools.partial(
        pl.kernel, mesh=mesh, out_type=jax.ShapeDtypeStruct((s, t, w), table.dtype),
        scratch_types=[pltpu.VMEM((SC_CHUNK, w), table.dtype), pltpu.VMEM((s, SC_CHUNK), I32),
                       pltpu.SemaphoreType.DMA],
        name="combine")
    def k(table_hbm, pos_hbm, out_hbm, rows_v, idx_v, sem):
        wid = lax.axis_index("s") * n_cores + lax.axis_index("c")

        @pl.loop(0, iters)
        def _(it):
            c = it * n_workers + wid

            @pl.when(c < n_chunks)
            def _():
                base = pl.multiple_of(c * SC_CHUNK, SC_CHUNK)
                pltpu.sync_copy(pos_hbm.at[:, pl.ds(base, SC_CHUNK)], idx_v)
                for j in range(s):
                    pltpu.async_copy(table_hbm.at[idx_v.at[j]], rows_v, sem).wait()
                    pltpu.sync_copy(rows_v, out_hbm.at[j, pl.ds(base, SC_CHUNK)])

    return k(table, pos_t)


def _sc_pack_weights(w, rows_per_item, after=()):
    e, r, c = w.shape
    half = r // 2
    rb = rows_per_item
    per_expert = half // rb
    n_cores, n_workers = _sc_workers()
    per_worker = e * per_expert // n_workers
    assert per_worker * n_workers == e * per_expert and per_worker % 2 == 0 and c % (SC_LANES * SC_UNROLL) == 0
    mesh = plsc.VectorSubcoreMesh(core_axis_name="c", subcore_axis_name="s")

    @functools.partial(
        pl.kernel, mesh=mesh, out_type=jax.ShapeDtypeStruct((e * half, c), U32),
        scratch_types=[pltpu.VMEM((2, rb, c), F32), pltpu.VMEM((2, rb, c), F32), pltpu.VMEM((2, rb, c), U32),
                       pltpu.SemaphoreType.DMA((2,)), pltpu.SemaphoreType.DMA((2,))],
        compiler_params=pltpu.CompilerParams(needs_layout_passes=False),
        cost_estimate=pl.CostEstimate(flops=e * r * c, transcendentals=0, bytes_accessed=6 * e * r * c),
        name="pack_weights")
    def k(w_hbm, *refs):
        out_hbm, lo_v, hi_v, out_v, in_sem, out_sem = refs[len(after):]
        wid = lax.axis_index("s") * n_cores + lax.axis_index("c")
        first = wid * per_worker

        def rows(item):
            ex = item // per_expert
            j = item - ex * per_expert
            return (pl.multiple_of(ex * r + j * rb, rb), pl.multiple_of(ex * r + half + j * rb, rb),
                    pl.multiple_of(ex * half + j * rb, rb))

        def loads(item, b):
            lo_row, hi_row, _ = rows(item)
            return (pltpu.make_async_copy(w_hbm.at[pl.ds(lo_row, rb)], lo_v.at[b], in_sem.at[b]),
                    pltpu.make_async_copy(w_hbm.at[pl.ds(hi_row, rb)], hi_v.at[b], in_sem.at[b]))

        def store(item, b):
            return pltpu.make_async_copy(out_v.at[b], out_hbm.at[pl.ds(rows(item)[2], rb)], out_sem.at[b])

        for cp in loads(first, 0):
            cp.start()

        @pl.loop(0, per_worker // 2)
        def _(pair):
            for b in range(2):
                item = first + pair * 2 + b
                for cp in loads(item, b):
                    cp.wait()

                @pl.when(item + 1 < first + per_worker)
                def _():
                    for cp in loads(item + 1, 1 - b):
                        cp.start()

                @pl.when(pair > 0)
                def _():
                    store(item - 2, b).wait()

                @pl.loop(0, rb)
                def _(i):
                    @pl.loop(0, c // (SC_LANES * SC_UNROLL))
                    def _(vb):
                        for u in range(SC_UNROLL):
                            sl = pl.ds(pl.multiple_of((vb * SC_UNROLL + u) * SC_LANES, SC_LANES), SC_LANES)
                            packed = plsc.pack(lo_v[b, i, sl], hi_v[b, i, sl], format=plsc.PackFormat.INTERLEAVED)
                            out_v[b, i, sl] = plsc.bitcast(packed, U32)

                store(item, b).start()

        for b in range(2):
            store(first + per_worker - 2 + b, b).wait()

    return k(w.reshape(e * r, c), *after).reshape(e, half, c)


def _experts_kernel(first_ref, ntile_ref, cnt_ref, xs_hbm, wg_ref, wu_ref, wd_ref, ys_hbm,
                    wg16_ref, wu16_ref, wd16_ref, x_buf, y_buf, in_sem, out_sem):
    ahead = STREAM_DEPTH - MAX_WIDTH
    e = pl.program_id(0)
    n_used = first_ref[N_EXPERTS - 1] + ntile_ref[N_EXPERTS - 1]
    first, n_mine, count = first_ref[e], ntile_ref[e], cnt_ref[e]

    def tile_rows(g):
        return pl.ds(pl.multiple_of(g * ROW_TILE, ROW_TILE), ROW_TILE)

    def load(g):
        slot = lax.rem(g, STREAM_DEPTH)
        return pltpu.make_async_copy(xs_hbm.at[tile_rows(g)], x_buf.at[slot], in_sem.at[slot])

    def store(g):
        slot = lax.rem(g, STREAM_DEPTH)
        return pltpu.make_async_copy(y_buf.at[slot], ys_hbm.at[tile_rows(g)], out_sem.at[slot])

    @pl.when(e == 0)
    def _():
        for g0 in range(ahead):
            @pl.when(g0 < n_used)
            def _():
                load(g0).start()

    for packed_ref, w16_ref in ((wg_ref, wg16_ref), (wu_ref, wu16_ref), (wd_ref, wd16_ref)):
        rows = packed_ref.shape[1]
        lo, hi = _unpack_rows(packed_ref[0])
        w16_ref[0:rows, :] = lo.astype(BF16)
        w16_ref[rows:, :] = hi.astype(BF16)

    def run(j, width):
        tiles = [first + j + t for t in range(width)]
        for g in tiles:
            load(g).wait()

            @pl.when(g + ahead < n_used)
            def _():
                load(g + ahead).start()

            @pl.when(g >= STREAM_DEPTH)
            def _():
                store(g - STREAM_DEPTH).wait()

        words = jnp.concatenate([x_buf[lax.rem(g, STREAM_DEPTH)] for g in tiles], axis=0)
        row = lax.broadcasted_iota(I32, words.shape, 0)
        words = jnp.where(row < count - j * ROW_TILE, words, jnp.uint32(0))
        lo, hi = _unpack_rows(words)
        lo, hi = lo.astype(BF16), hi.astype(BF16)
        hg = _dot(lo, wg16_ref[0:HALF, :]) + _dot(hi, wg16_ref[HALF:, :])
        hu = _dot(lo, wu16_ref[0:HALF, :]) + _dot(hi, wu16_ref[HALF:, :])
        a = (_silu(hg) * hu).astype(BF16)
        y = _pack_rows(_dot(a, wd16_ref[...]))
        for t, g in enumerate(tiles):
            y_buf[lax.rem(g, STREAM_DEPTH)] = y[t * ROW_TILE:(t + 1) * ROW_TILE]
            store(g).start()

    def widest(p, carry):
        run(MAX_WIDTH * p, MAX_WIDTH)
        return carry

    lax.fori_loop(0, n_mine // MAX_WIDTH, widest, 0)
    done = n_mine - lax.rem(n_mine, MAX_WIDTH)
    width = MAX_WIDTH // 2
    while width:
        has = lax.rem(n_mine // width, 2) == 1

        @pl.when(has)
        def _(width=width, done=done):
            run(done, width)

        done = done + jnp.where(has, width, 0)
        width //= 2

    @pl.when(e == N_EXPERTS - 1)
    def _():
        for back in range(STREAM_DEPTH, 0, -1):
            @pl.when(n_used >= back)
            def _():
                store(n_used - back).wait()


def _experts(xs, first_tile, n_tile, count, w_eg, w_eu, w_ed):
    d = D_MODEL
    by_expert = lambda e, *_: (e, 0, 0)
    grid_spec = pltpu.PrefetchScalarGridSpec(
        num_scalar_prefetch=3,
        grid=(N_EXPERTS,),
        in_specs=[pl.BlockSpec(memory_space=pl.ANY),
                  pl.BlockSpec((1, d // 2, EXPERT_DIM), by_expert),
                  pl.BlockSpec((1, d // 2, EXPERT_DIM), by_expert),
                  pl.BlockSpec((1, EXPERT_DIM // 2, d), by_expert)],
        out_specs=pl.BlockSpec(memory_space=pl.ANY),
        scratch_shapes=[pltpu.VMEM((d, EXPERT_DIM), BF16), pltpu.VMEM((d, EXPERT_DIM), BF16),
                        pltpu.VMEM((EXPERT_DIM, d), BF16),
                        pltpu.VMEM((STREAM_DEPTH, ROW_TILE, HALF), U32),
                        pltpu.VMEM((STREAM_DEPTH, ROW_TILE, HALF), U32),
                        pltpu.SemaphoreType.DMA((STREAM_DEPTH,)), pltpu.SemaphoreType.DMA((STREAM_DEPTH,))])
    return pl.pallas_call(
        _experts_kernel,
        grid_spec=grid_spec,
        out_shape=jax.ShapeDtypeStruct(xs.shape, U32),
        compiler_params=pltpu.CompilerParams(dimension_semantics=("arbitrary",),
                                             vmem_limit_bytes=VMEM_LIMIT),
        name="experts",
    )(first_tile, n_tile, count, xs, w_eg, w_eu, w_ed)


def _final_kernel(z_ref, gatew_ref, h2_ref, x1_ref, mod_ref, normf_ref, wsg_ref, wsu_ref, wsd_ref, *rest):
    y_ref, wsg16_ref, wsu16_ref, wsd16_ref = rest[-4:]

    @pl.when(pl.program_id(0) == 0)
    def _():
        wsg16_ref[...] = wsg_ref[...].astype(BF16)
        wsu16_ref[...] = wsu_ref[...].astype(BF16)
        wsd16_ref[...] = wsd_ref[...].astype(BF16)

    lo, hi = _unpack_rows(h2_ref[...])
    h = jnp.concatenate([lo, hi], axis=-1).astype(BF16)
    a = _silu(_dot(h, wsg16_ref[...])) * _dot(h, wsu16_ref[...])
    acc = _dot(a.astype(BF16), wsd16_ref[...])
    for s in range(TOP_K):
        lo, hi = _unpack_rows(z_ref[s])
        acc = acc + gatew_ref[:, s:s + 1] * jnp.concatenate([lo, hi], axis=-1)
    x2 = x1_ref[...] + _mod(mod_ref, 5) * acc
    y_ref[...] = _rms(x2) * normf_ref[...]


def _final(z, gate_w, h2, x1, mod, norm_f, w_sg, w_su, w_sd, block_t, first_block, per_seq,
           seq0=0, out_rows=None, y_prev=None):
    t, d = x1.shape
    out_rows = t if out_rows is None else out_rows
    out_first = seq0 * per_seq
    tok = lambda i: (i, 0)
    if per_seq:
        mod_spec = pl.BlockSpec((1, 6, d), lambda i: (seq0 + i // per_seq, 0, 0))
    else:
        mod_spec = pl.BlockSpec((block_t, 6 * d), tok)
    operands = [z, gate_w, h2, x1, mod, norm_f, w_sg, w_su, w_sd]
    in_specs = [pl.BlockSpec((TOP_K, block_t, HALF), lambda i: (0, first_block + i, 0)),
                pl.BlockSpec((block_t, LANES), tok),
                pl.BlockSpec((block_t, HALF), tok),
                pl.BlockSpec((block_t, d), tok),
                mod_spec,
                _full((1, d)),
                _full((d, EXPERT_DIM)), _full((d, EXPERT_DIM)), _full((EXPERT_DIM, d))]
    aliases = {}
    if y_prev is not None:
        aliases = {len(operands): 0}
        operands.append(y_prev)
        in_specs.append(pl.BlockSpec(memory_space=pl.ANY))
    return pl.pallas_call(
        _final_kernel,
        grid=(t // block_t,),
        in_specs=in_specs,
        out_specs=pl.BlockSpec((block_t, d), lambda i: (out_first + i, 0)),
        out_shape=jax.ShapeDtypeStruct((out_rows, d), F32),
        scratch_shapes=[pltpu.VMEM((d, EXPERT_DIM), BF16), pltpu.VMEM((d, EXPERT_DIM), BF16),
                        pltpu.VMEM((EXPERT_DIM, d), BF16)],
        input_output_aliases=aliases,
        compiler_params=pltpu.CompilerParams(dimension_semantics=("arbitrary",),
                                             vmem_limit_bytes=VMEM_LIMIT),
        name="final",
    )(*operands)


def kernel(x_prompt, x_sample, c_prompt, c_sample, state_ret, state_pool, norm1, norm2, norm_f,
           w_ada, b_ada, w_in, w_out, w_pool, pool_scale, w_router, router_bias, w_exp_gate,
           w_exp_up, w_exp_down, w_sh_gate, w_sh_up, w_sh_down):
    b, l, d = x_prompt.shape
    n = x_sample.shape[0]

    mod_p, mod_s = _ada(c_prompt, c_sample, w_ada[0], b_ada[0])
    mod_p = mod_p.reshape(b, 6, d)

    w_in16 = w_in[0].astype(BF16)
    w_out16 = w_out[0].astype(BF16)
    wr_t = w_router[0].T
    bias_t = jnp.broadcast_to(router_bias[0][:, None], (N_EXPERTS, LANES))
    n1, n2, nf = norm1[0].reshape(1, d), norm2[0].reshape(1, d), norm_f.reshape(1, d)
    ps = pool_scale[0].reshape(1, POOL_WIDTH)
    shared = (w_sh_gate[0], w_sh_up[0], w_sh_down[0])

    def routed(sources, experts, counts):
        n_tiles = experts.shape[1] * TOP_K // ROW_TILE + N_EXPERTS
        pos_t, meta = _plan(experts, counts)
        xs = _sc_scatter_rows(sources, pos_t, n_tiles * ROW_TILE, after=expert_w)
        ys = _experts(xs, meta[:, 0], meta[:, 1], meta[:, 2], *expert_w)
        return _sc_gather_rows(ys, pos_t)

    mix_args = (n1, w_in16, w_pool[0], ps, w_out16, n2, wr_t, bias_t)
    x1_s, h2_s, experts_s, gatew_s, counts_s, ret_s, pool_s = _mix_sample(
        x_sample.reshape(n, d), mod_s, state_ret[0], state_pool[0], float(PAST_LEN), *mix_args)

    expert_w = (_sc_pack_weights(w_exp_gate[0], 64, after=(mod_s,)),
                _sc_pack_weights(w_exp_up[0], 64, after=(mod_s,)),
                _sc_pack_weights(w_exp_down[0], 16, after=(mod_s,)))

    ba = b // 2
    bb = b - ba
    x1_a, h2_a, experts_a, gatew_a, counts_a, ret_a, pool_a = _mix_prompt(x_prompt, mod_p, 0, ba, *mix_args)
    z_a = routed((h2_a,), experts_a, counts_a)
    x1_b, h2_b, experts_b, gatew_b, counts_b, ret_b, pool_b = _mix_prompt(x_prompt, mod_p, ba, bb, *mix_args)
    z_b = routed((h2_b, h2_s), jnp.concatenate([experts_b, experts_s], axis=1), counts_b + counts_s)

    block_t = 512
    per_seq = l // block_t
    y_s = _final(z_b, gatew_s, h2_s, x1_s, mod_s, nf, *shared,
                 block_t=n, first_block=bb * l // n, per_seq=0)
    y_p = _final(z_b, gatew_b, h2_b, x1_b.reshape(bb * l, d), mod_p, nf, *shared,
                 block_t=block_t, first_block=0, per_seq=per_seq, seq0=ba, out_rows=b * l)
    y_p = _final(z_a, gatew_a, h2_a, x1_a.reshape(ba * l, d), mod_p, nf, *shared,
                 block_t=block_t, first_block=0, per_seq=per_seq, seq0=0, out_rows=b * l, y_prev=y_p)

    ret_p = jnp.concatenate([ret_a, ret_b], axis=0)
    pool_p = jnp.concatenate([pool_a, pool_b], axis=0)
    return (y_p.reshape(b, l, d), y_s.reshape(n, 1, d), ret_p[None], pool_p[None],
            ret_s[None], pool_s[None])
```

```python
import functools

import jax
import jax.numpy as jnp
import numpy as np
from jax import lax
from jax.experimental import pallas as pl
from jax.experimental.pallas import tpu as pltpu
from jax.experimental.pallas import tpu_sc as plsc

D_MODEL = 1024
RET_HEADS = 4
RET_QK_DIM = 64
RET_V_DIM = 128
RET_WIDTH = RET_HEADS * RET_V_DIM
QK_WIDTH = RET_HEADS * RET_QK_DIM
ROPE_BASE = 10000.0
POOL_WINDOWS = (2, 4, 8, 16)
POOL_WIDTH = 512
POOL_GROUP_DIM = 128
POOL_BUF = 15
IN_WIDTH = 2 * QK_WIDTH + 2 * RET_WIDTH + POOL_WIDTH
N_EXPERTS = 64
TOP_K = 8
N_EXPERT_GROUPS = 8
GROUP_SIZE = N_EXPERTS // N_EXPERT_GROUPS
TOP_GROUPS = 4
EXPERT_DIM = 256
ROUTE_SCALE = 2.5
EPS = 1e-6
PAST_LEN = 16384

LANES = 128
SUBLANES = 8
POOL_CARRY = 24
VMEM_LIMIT = 56 * 1024 * 1024
HALF = D_MODEL // 2
ROW_TILE = 256
MAX_WIDTH = 2
SC_CHUNK = 128
SC_LANES = 16
SC_UNROLL = 16
STREAM_DEPTH = 24

BF16 = jnp.bfloat16
F32 = jnp.float32
U32 = jnp.uint32
I32 = jnp.int32


def _silu(x):
    return x * jax.nn.sigmoid(x)


def _dot(a, b):
    return jnp.dot(a, b, preferred_element_type=F32)


def _rms(x):
    return x * lax.rsqrt(jnp.mean(x * x, axis=-1, keepdims=True) + EPS)


def _mod(mod_ref, i, seq=0):
    if len(mod_ref.shape) == 3:
        return mod_ref[seq, i:i + 1, :]
    return mod_ref[:, i * D_MODEL:(i + 1) * D_MODEL]


def _split_bf16(x):
    hi = x.astype(BF16)
    lo = (x - hi.astype(F32)).astype(BF16)
    return hi, lo


def _pack_rows(x):
    lo = lax.bitcast_convert_type(x[:, :HALF].astype(BF16).astype(F32), U32)
    hi = lax.bitcast_convert_type(x[:, HALF:].astype(BF16).astype(F32), U32)
    return (hi & jnp.uint32(0xFFFF0000)) | (lo >> jnp.uint32(16))


def _unpack_rows(w):
    lo = lax.bitcast_convert_type(w << jnp.uint32(16), F32)
    hi = lax.bitcast_convert_type(w & jnp.uint32(0xFFFF0000), F32)
    return lo, hi


def _first_max_onehot(work, idx, n):
    m = jnp.max(work, axis=0, keepdims=True)
    first = jnp.min(jnp.where(work == m, idx, float(n)), axis=0, keepdims=True)
    return idx == first


def _route(h2, wr_t_ref, bias_t_ref):
    n = h2.shape[0]
    h_hi, h_lo = _split_bf16(h2)
    w_hi, w_lo = _split_bf16(wr_t_ref[...])
    nt = (((1,), (1,)), ((), ()))
    logits = (lax.dot_general(w_hi, h_hi, nt, preferred_element_type=F32)
              + lax.dot_general(w_hi, h_lo, nt, preferred_element_type=F32)
              + lax.dot_general(w_lo, h_hi, nt, preferred_element_type=F32))
    scores = jax.nn.sigmoid(logits)
    biased = scores + bias_t_ref[:, 0:1]
    b3 = biased.reshape(N_EXPERT_GROUPS, GROUP_SIZE, n)
    i3 = lax.broadcasted_iota(I32, b3.shape, 1).astype(F32)
    m1 = jnp.max(b3, axis=1, keepdims=True)
    first = jnp.min(jnp.where(b3 == m1, i3, float(GROUP_SIZE)), axis=1, keepdims=True)
    m2 = jnp.max(jnp.where(i3 == first, -jnp.inf, b3), axis=1, keepdims=True)
    gscore = (m1 + m2).reshape(N_EXPERT_GROUPS, n)
    gidx = lax.broadcasted_iota(I32, gscore.shape, 0).astype(F32)
    gsel = jnp.zeros(gscore.shape, F32)
    work = gscore
    for _ in range(TOP_GROUPS):
        hit = _first_max_onehot(work, gidx, N_EXPERT_GROUPS)
        gsel = jnp.where(hit, 1.0, gsel)
        work = jnp.where(hit, -jnp.inf, work)
    gsel3 = jnp.broadcast_to(gsel.reshape(N_EXPERT_GROUPS, 1, n), b3.shape)
    work = jnp.where(gsel3 > 0.0, b3, -jnp.inf).reshape(N_EXPERTS, n)
    eidx = lax.broadcasted_iota(I32, work.shape, 0).astype(F32)
    sel = jnp.zeros(work.shape, F32)
    for _ in range(TOP_K):
        hit = _first_max_onehot(work, eidx, N_EXPERTS)
        sel = jnp.where(hit, 1.0, sel)
        work = jnp.where(hit, -jnp.inf, work)
    picked = jnp.where(sel > 0.0, scores, 0.0)
    gates = picked / jnp.sum(picked, axis=0, keepdims=True) * ROUTE_SCALE
    below = (lax.broadcasted_iota(I32, (N_EXPERTS, N_EXPERTS), 1)
             < lax.broadcasted_iota(I32, (N_EXPERTS, N_EXPERTS), 0))
    slot = _dot(jnp.where(below, 1.0, 0.0).astype(BF16), sel.astype(BF16))
    e_rows, w_rows = [], []
    for s in range(TOP_K):
        here = jnp.where(slot == float(s), sel, 0.0)
        e_rows.append(jnp.sum(here * eidx, axis=0, keepdims=True))
        w_rows.append(jnp.sum(here * gates, axis=0, keepdims=True))
    experts = jnp.concatenate(e_rows, axis=0).astype(I32)
    w_t = jnp.concatenate(w_rows + [jnp.zeros((LANES - TOP_K, n), F32)], axis=0)
    counts = jnp.broadcast_to(jnp.sum(sel, axis=1, keepdims=True), (N_EXPERTS, LANES))
    return experts, w_t.T, counts


def _group_norm_gate(o, g):
    parts = []
    for h in range(RET_HEADS):
        oh = o[:, h * RET_V_DIM:(h + 1) * RET_V_DIM]
        mu = jnp.mean(oh, axis=-1, keepdims=True)
        ctr = oh - mu
        var = jnp.mean(ctr * ctr, axis=-1, keepdims=True)
        parts.append(ctr * lax.rsqrt(var + EPS))
    return _silu(g) * jnp.concatenate(parts, axis=-1)


def _pool_project(pooled, w_pool_ref, pool_scale_ref):
    parts = [_dot(p.astype(BF16), w_pool_ref[gi].astype(BF16)) for gi, p in enumerate(pooled)]
    return jnp.concatenate(parts, axis=-1) * pool_scale_ref[...]


def _out_residual(x, o_gated, p, mod_ref, seq, w_out_ref):
    mix = jnp.concatenate([o_gated, p], axis=-1).astype(BF16)
    return x + _mod(mod_ref, 2, seq) * _dot(mix, w_out_ref[...])


def _norm_route(x1, mod_ref, seq, norm2_ref, wr_t_ref, bias_t_ref, h2_ref, experts_ref, gatew_ref, cnt_ref):
    h2 = _rms(x1) * norm2_ref[...] * (1.0 + _mod(mod_ref, 4, seq)) + _mod(mod_ref, 3, seq)
    h2_ref[...] = _pack_rows(h2)
    experts, gate_w, counts = _route(h2, wr_t_ref, bias_t_ref)
    experts_ref[...] = experts
    gatew_ref[...] = gate_w
    cnt_ref[...] += counts


def _ada_kernel(cp_ref, cs_ref, w_ref, b_ref, op_ref, os_ref):
    w16 = w_ref[...].astype(BF16)
    for c_ref, o_ref in ((cp_ref, op_ref), (cs_ref, os_ref)):
        o_ref[...] = _dot(_silu(c_ref[...]).astype(BF16), w16) + b_ref[...]


def _ada(c_prompt, c_sample, w_ada, b_ada, block_n=1536):
    d, width = w_ada.shape
    rows = lambda c: pl.BlockSpec((c.shape[0], d), lambda j: (0, 0))
    cols = lambda c: pl.BlockSpec((c.shape[0], block_n), lambda j: (0, j))
    return pl.pallas_call(
        _ada_kernel,
        grid=(width // block_n,),
        in_specs=[rows(c_prompt), rows(c_sample),
                  pl.BlockSpec((d, block_n), lambda j: (0, j)),
                  pl.BlockSpec((1, block_n), lambda j: (0, j))],
        out_specs=[cols(c_prompt), cols(c_sample)],
        out_shape=[jax.ShapeDtypeStruct((c.shape[0], width), F32) for c in (c_prompt, c_sample)],
        compiler_params=pltpu.CompilerParams(vmem_limit_bytes=VMEM_LIMIT),
        name="ada",
    )(c_prompt, c_sample, w_ada, b_ada.reshape(1, width))


def _mix_prompt_kernel(x_ref, mod_ref, norm1_ref, w_in_ref, cos_ref, sin_ref, dmat_ref, cross_ref,
                       tail_ref, cdec_ref, w_pool_ref, pool_scale_ref, w_out_ref, norm2_ref,
                       wr_t_ref, bias_t_ref,
                       x1_ref, h2_ref, experts_ref, gatew_ref, cnt_ref, ret_ref, pool_ref,
                       state_ref, ext_ref, win_ref, o_ref, *, block_l, chunk, seqs):
    li = pl.program_id(1)

    @pl.when((pl.program_id(0) == 0) & (li == 0))
    def _():
        cnt_ref[...] = jnp.zeros_like(cnt_ref)

    @pl.when(li == 0)
    def _():
        state_ref[...] = jnp.zeros_like(state_ref)
        ext_ref[:, 0:POOL_CARRY, :] = jnp.zeros((seqs, POOL_CARRY, POOL_WIDTH), F32)
        win_ref[:, 0:SUBLANES, :] = jnp.zeros((seqs, SUBLANES, POOL_WIDTH), F32)

    for seq in range(seqs):
        _mix_prompt_seq(seq, li, x_ref, mod_ref, norm1_ref, w_in_ref, cos_ref, sin_ref, dmat_ref, cross_ref,
                        tail_ref, cdec_ref, w_pool_ref, pool_scale_ref, w_out_ref, norm2_ref,
                        wr_t_ref, bias_t_ref, x1_ref, h2_ref, experts_ref, gatew_ref, cnt_ref,
                        state_ref.at[seq], ext_ref.at[seq], win_ref.at[seq], o_ref.at[seq],
                        block_l=block_l, chunk=chunk)

    @pl.when(li == pl.num_programs(1) - 1)
    def _():
        ret_ref[...] = state_ref[...].reshape(ret_ref.shape)
        pool_ref[...] = ext_ref[:, POOL_CARRY - POOL_BUF:POOL_CARRY, :]


def _window_sums(ext_ref, win_ref, block_l):
    g = POOL_GROUP_DIM
    top = POOL_CARRY + block_l
    new = slice(POOL_CARRY - SUBLANES, None)
    s2 = ext_ref[SUBLANES:top, :] + ext_ref[SUBLANES - 1:top - 1, :]
    win_ref[SUBLANES:top, g:] = s2[:, g:]
    s4 = s2[:, g:] + win_ref[SUBLANES - 2:top - 2, g:]
    win_ref[SUBLANES:top, 2 * g:] = s4[:, g:]
    s8 = s4[:, g:] + win_ref[SUBLANES - 4:top - 4, 2 * g:]
    win_ref[SUBLANES:top, 3 * g:] = s8[:, g:]
    s16 = s8[:, g:] + win_ref[0:top - SUBLANES, 3 * g:]
    return [s2[new, 0:g], s4[new, 0:g], s8[new, 0:g], s16[new, :]]


def _mix_prompt_seq(seq, li, x_ref, mod_ref, norm1_ref, w_in_ref, cos_ref, sin_ref, dmat_ref, cross_ref,
                    tail_ref, cdec_ref, w_pool_ref, pool_scale_ref, w_out_ref, norm2_ref,
                    wr_t_ref, bias_t_ref, x1_ref, h2_ref, experts_ref, gatew_ref, cnt_ref,
                    state_ref, ext_ref, win_ref, o_ref, *, block_l, chunk):
    x = x_ref[seq]
    h16 = (_rms(x) * norm1_ref[...] * (1.0 + _mod(mod_ref, 1, seq)) + _mod(mod_ref, 0, seq)).astype(BF16)
    proj = _dot(h16, w_in_ref[...])
    q = proj[:, 0:QK_WIDTH]
    k = proj[:, QK_WIDTH:2 * QK_WIDTH]
    v = proj[:, 2 * QK_WIDTH:2 * QK_WIDTH + RET_WIDTH]
    g = proj[:, 2 * QK_WIDTH + RET_WIDTH:2 * QK_WIDTH + 2 * RET_WIDTH]
    u = proj[:, 2 * QK_WIDTH + 2 * RET_WIDTH:]

    lane = lax.broadcasted_iota(I32, q.shape, 1)
    first_half = (lane % RET_QK_DIM) < (RET_QK_DIM // 2)
    cos_t = cos_ref[...]
    sin_t = sin_ref[...]

    def rot(t):
        partner = jnp.where(first_half, pltpu.roll(t, QK_WIDTH - RET_QK_DIM // 2, axis=1),
                            pltpu.roll(t, RET_QK_DIM // 2, axis=1))
        return t * cos_t + partner * sin_t

    q = rot(q)
    k = rot(k) * (RET_QK_DIM ** -0.5)
    k_t = k.T
    v16 = v.astype(BF16)
    head_of_lane = lax.broadcasted_iota(I32, (chunk, QK_WIDTH), 1) // RET_QK_DIM

    for c in range(block_l // chunk):
        rows = slice(c * chunk, (c + 1) * chunk)
        q_c = q[rows]
        kt_c = k_t[:, rows]
        kt16 = kt_c.astype(BF16)
        state16 = state_ref[...].astype(BF16)
        for hd in range(RET_HEADS):
            in_head = head_of_lane == hd
            q_h = jnp.where(in_head, q_c, 0.0).astype(BF16)
            v_h = v16[rows, hd * RET_V_DIM:(hd + 1) * RET_V_DIM]
            scores = _dot(q_h, kt16) * dmat_ref[hd]
            inner = _dot(scores.astype(BF16), v_h)
            cross = _dot(q_h, state16) * cross_ref[hd]
            o_ref[rows, hd * RET_V_DIM:(hd + 1) * RET_V_DIM] = inner + cross
            hrows = slice(hd * RET_QK_DIM, (hd + 1) * RET_QK_DIM)
            k_dec = (kt_c[hrows] * tail_ref[hd:hd + 1, :]).astype(BF16)
            state_ref[hrows, :] = state_ref[hrows, :] * cdec_ref[hd] + _dot(k_dec, v_h)

    o_gated = _group_norm_gate(o_ref[...], g)

    ext_ref[POOL_CARRY:POOL_CARRY + block_l, :] = u
    pos = (li * block_l + lax.broadcasted_iota(I32, (block_l, 1), 0)).astype(F32)
    pooled = []
    for gi, (w, acc) in enumerate(zip(POOL_WINDOWS, _window_sums(ext_ref, win_ref, block_l))):
        cnt = jnp.minimum(pos + 1.0, float(w))
        pooled.append(acc / cnt - u[:, gi * POOL_GROUP_DIM:(gi + 1) * POOL_GROUP_DIM])
    p = _pool_project(pooled, w_pool_ref, pool_scale_ref)
    ext_ref[0:POOL_CARRY, :] = ext_ref[block_l:block_l + POOL_CARRY, :]

    x1 = _out_residual(x, o_gated, p, mod_ref, seq, w_out_ref)
    x1_ref[seq] = x1
    _norm_route(x1, mod_ref, seq, norm2_ref, wr_t_ref, bias_t_ref,
                h2_ref.at[seq], experts_ref.at[seq], gatew_ref.at[seq], cnt_ref)


def _decay_tables(chunk):
    f32 = np.float32
    lg = np.log(f32(1.0) - f32(2.0) ** (f32(-5.0) - np.arange(RET_HEADS, dtype=f32))).astype(f32)
    idx = np.arange(chunk, dtype=f32)
    diff = idx[:, None] - idx[None, :]
    causal = diff >= 0
    dmat = np.where(causal[None], np.exp(lg[:, None, None] * np.where(causal, diff, f32(0.0))[None]), f32(0.0))
    cross = np.exp(lg[:, None] * (idx[None, :] + f32(1.0)))
    cross = np.broadcast_to(cross[:, :, None], (RET_HEADS, chunk, RET_V_DIM))
    tail = np.exp(lg[:, None] * (f32(chunk - 1.0) - idx)[None, :])
    cdec = np.broadcast_to(np.exp(lg * f32(chunk))[:, None, None], (RET_HEADS, RET_QK_DIM, RET_V_DIM))
    return tuple(jnp.asarray(np.ascontiguousarray(t, dtype=f32)) for t in (dmat, cross, tail, cdec))


def _rotary_angles(pos):
    half = RET_QK_DIM // 2
    freqs = (np.float32(ROPE_BASE) ** (-np.arange(half, dtype=np.float32) / np.float32(half))).astype(np.float32)
    return np.asarray(pos, np.float32)[:, None] * freqs[None, :]


def _rotary_tables(pos):
    ang = _rotary_angles(pos)
    cos, sin = np.cos(ang), np.sin(ang)
    cos_t = np.tile(np.concatenate([cos, cos], axis=-1), (1, RET_HEADS))
    sin_t = np.tile(np.concatenate([-sin, sin], axis=-1), (1, RET_HEADS))
    return jnp.asarray(cos_t, F32), jnp.asarray(sin_t, F32)


def _full(shape):
    return pl.BlockSpec(shape, lambda *_: (0,) * len(shape))


def _mix_prompt(x, mod, b0, b, norm1, w_in16, w_pool, pool_scale, w_out16, norm2, wr_t, bias_t,
                block_l=512, chunk=256, seqs=2):
    _, l, d = x.shape
    nl = l // block_l
    s0 = b0 // seqs
    cos_t, sin_t = _rotary_tables(np.arange(l))
    dmat, cross, tail, cdec = _decay_tables(chunk)
    kernel = functools.partial(_mix_prompt_kernel, block_l=block_l, chunk=chunk, seqs=seqs)
    tok = lambda bi, li: (bi, li, 0)
    per_seq = lambda bi, li: (bi, 0, 0)
    x1, h2, experts, gate_w, counts, ret, pool = pl.pallas_call(
        kernel,
        grid=(b // seqs, nl),
        in_specs=[pl.BlockSpec((seqs, block_l, d), lambda bi, li: (s0 + bi, li, 0)),
                  pl.BlockSpec((seqs, 6, d), lambda bi, li: (s0 + bi, 0, 0)),
                  _full((1, d)),
                  _full((d, IN_WIDTH)),
                  pl.BlockSpec((block_l, QK_WIDTH), lambda bi, li: (li, 0)),
                  pl.BlockSpec((block_l, QK_WIDTH), lambda bi, li: (li, 0)),
                  _full(dmat.shape), _full(cross.shape), _full(tail.shape), _full(cdec.shape),
                  _full(w_pool.shape), _full((1, POOL_WIDTH)), _full((d, d)), _full((1, d)),
                  _full(wr_t.shape), _full(bias_t.shape)],
        out_specs=[pl.BlockSpec((seqs, block_l, d), tok),
                   pl.BlockSpec((seqs, block_l, HALF), tok),
                   pl.BlockSpec((seqs, TOP_K, block_l), lambda bi, li: (bi, 0, li)),
                   pl.BlockSpec((seqs, block_l, LANES), tok),
                   _full((N_EXPERTS, LANES)),
                   pl.BlockSpec((seqs, RET_HEADS, RET_QK_DIM, RET_V_DIM), lambda bi, li: (bi, 0, 0, 0)),
                   pl.BlockSpec((seqs, POOL_BUF, POOL_WIDTH), per_seq)],
        out_shape=[jax.ShapeDtypeStruct((b, l, d), F32),
                   jax.ShapeDtypeStruct((b, l, HALF), U32),
                   jax.ShapeDtypeStruct((b, TOP_K, l), I32),
                   jax.ShapeDtypeStruct((b, l, LANES), F32),
                   jax.ShapeDtypeStruct((N_EXPERTS, LANES), F32),
                   jax.ShapeDtypeStruct((b, RET_HEADS, RET_QK_DIM, RET_V_DIM), F32),
                   jax.ShapeDtypeStruct((b, POOL_BUF, POOL_WIDTH), F32)],
        scratch_shapes=[pltpu.VMEM((seqs, QK_WIDTH, RET_V_DIM), F32),
                        pltpu.VMEM((seqs, POOL_CARRY + block_l, POOL_WIDTH), F32),
                        pltpu.VMEM((seqs, POOL_CARRY + block_l, POOL_WIDTH), F32),
                        pltpu.VMEM((seqs, block_l, RET_WIDTH), F32)],
        compiler_params=pltpu.CompilerParams(dimension_semantics=("arbitrary", "arbitrary"),
                                             vmem_limit_bytes=VMEM_LIMIT),
        name="mix_prompt",
    )(x, mod, norm1, w_in16, cos_t, sin_t, dmat, cross, tail, cdec, w_pool, pool_scale,
      w_out16, norm2, wr_t, bias_t)
    experts = jnp.transpose(experts, (1, 0, 2)).reshape(TOP_K, b * l)
    return x1, h2.reshape(b * l, HALF), experts, gate_w.reshape(b * l, LANES), counts, ret, pool


def _mix_sample_front_kernel(x_ref, mod_ref, norm1_ref, w_in_ref, cos_ref, sin_ref,
                             qt_ref, kt_ref, v_ref, g_ref, u_ref):
    x = x_ref[...]
    h = _rms(x) * norm1_ref[...] * (1.0 + _mod(mod_ref, 1)) + _mod(mod_ref, 0)
    proj = _dot(h.astype(BF16), w_in_ref[...])
    half = RET_QK_DIM // 2
    cos_c = cos_ref[...]
    sin_c = sin_ref[...]

    def rot_t(t):
        parts = []
        for hd in range(RET_HEADS):
            t1 = t[hd * RET_QK_DIM:hd * RET_QK_DIM + half]
            t2 = t[hd * RET_QK_DIM + half:(hd + 1) * RET_QK_DIM]
            parts += [t1 * cos_c - t2 * sin_c, t1 * sin_c + t2 * cos_c]
        return jnp.concatenate(parts, axis=0)

    qt_ref[...] = rot_t(proj[:, 0:QK_WIDTH].T)
    kt_ref[...] = rot_t(proj[:, QK_WIDTH:2 * QK_WIDTH].T) * (RET_QK_DIM ** -0.5)
    v_ref[...] = proj[:, 2 * QK_WIDTH:2 * QK_WIDTH + RET_WIDTH]
    g_ref[...] = proj[:, 2 * QK_WIDTH + RET_WIDTH:2 * QK_WIDTH + 2 * RET_WIDTH]
    u_ref[...] = proj[:, 2 * QK_WIDTH + 2 * RET_WIDTH:]


def _ret_step_kernel(qt_ref, kt_ref, v_ref, s0_ref, o_ref, s1_ref, *, block_b, decays):
    i = pl.program_id(0)
    lane = lax.broadcasted_iota(I32, qt_ref.shape, 1)
    for j in range(block_b):
        bi = i * block_b + j
        here = lane == bi
        q_col = jnp.sum(jnp.where(here, qt_ref[...], 0.0), axis=1, keepdims=True)
        k_col = jnp.sum(jnp.where(here, kt_ref[...], 0.0), axis=1, keepdims=True)
        v_row = v_ref[pl.ds(bi, 1), :]
        outs = []
        for hd in range(RET_HEADS):
            hrows = slice(hd * RET_QK_DIM, (hd + 1) * RET_QK_DIM)
            s1 = decays[hd] * s0_ref[j, hd] + k_col[hrows] * v_row[:, hd * RET_V_DIM:(hd + 1) * RET_V_DIM]
            s1_ref[j, hd] = s1
            outs.append(jnp.sum(q_col[hrows] * s1, axis=0, keepdims=True))
        o_ref[pl.ds(bi, 1), :] = jnp.concatenate(outs, axis=-1)


def _mix_sample_back_kernel(x_ref, mod_ref, o_ref, g_ref, u_ref, buf_ref, w_pool_ref, pool_scale_ref,
                            w_out_ref, norm2_ref, wr_t_ref, bias_t_ref,
                            x1_ref, h2_ref, experts_ref, gatew_ref, cnt_ref, pool_ref):
    cnt_ref[...] = jnp.zeros_like(cnt_ref)
    o_gated = _group_norm_gate(o_ref[...], g_ref[...])
    u = u_ref[...]
    pooled = []
    for gi, w in enumerate(POOL_WINDOWS):
        lanes = slice(gi * POOL_GROUP_DIM, (gi + 1) * POOL_GROUP_DIM)
        acc = u[:, lanes]
        for j in range(1, w):
            acc = acc + buf_ref[:, POOL_BUF - j, lanes]
        pooled.append(acc / float(w) - u[:, lanes])
    p = _pool_project(pooled, w_pool_ref, pool_scale_ref)
    pool_ref[:, 0:POOL_BUF - 1, :] = buf_ref[:, 1:POOL_BUF, :]
    pool_ref[:, POOL_BUF - 1, :] = u
    x1 = _out_residual(x_ref[...], o_gated, p, mod_ref, 0, w_out_ref)
    x1_ref[...] = x1
    _norm_route(x1, mod_ref, 0, norm2_ref, wr_t_ref, bias_t_ref, h2_ref, experts_ref, gatew_ref, cnt_ref)


def _mix_sample(x, mod, state_ret, state_pool, start, norm1, w_in16, w_pool,
                pool_scale, w_out16, norm2, wr_t, bias_t, block_b=64):
    n, d = x.shape
    half = RET_QK_DIM // 2
    ang = _rotary_angles([start])
    cos_c = jnp.asarray(np.broadcast_to(np.cos(ang).T, (half, n)), F32)
    sin_c = jnp.asarray(np.broadcast_to(np.sin(ang).T, (half, n)), F32)
    params = pltpu.CompilerParams(vmem_limit_bytes=VMEM_LIMIT)
    qt, kt, v, g, u = pl.pallas_call(
        _mix_sample_front_kernel,
        out_shape=[jax.ShapeDtypeStruct((QK_WIDTH, n), F32), jax.ShapeDtypeStruct((QK_WIDTH, n), F32),
                   jax.ShapeDtypeStruct((n, RET_WIDTH), F32), jax.ShapeDtypeStruct((n, RET_WIDTH), F32),
                   jax.ShapeDtypeStruct((n, POOL_WIDTH), F32)],
        compiler_params=params,
        name="mix_sample_front",
    )(x, mod, norm1, w_in16, cos_c, sin_c)

    lg = np.log(1.0 - 2.0 ** (-5.0 - np.arange(RET_HEADS, dtype=np.float32)), dtype=np.float32)
    decays = tuple(float(np.exp(lg[h])) for h in range(RET_HEADS))
    state_block = (block_b, RET_HEADS, RET_QK_DIM, RET_V_DIM)
    o, s1 = pl.pallas_call(
        functools.partial(_ret_step_kernel, block_b=block_b, decays=decays),
        grid=(n // block_b,),
        in_specs=[_full((QK_WIDTH, n)), _full((QK_WIDTH, n)), _full((n, RET_WIDTH)),
                  pl.BlockSpec(state_block, lambda i: (i, 0, 0, 0))],
        out_specs=[_full((n, RET_WIDTH)), pl.BlockSpec(state_block, lambda i: (i, 0, 0, 0))],
        out_shape=[jax.ShapeDtypeStruct((n, RET_WIDTH), F32),
                   jax.ShapeDtypeStruct(state_ret.shape, F32)],
        compiler_params=pltpu.CompilerParams(dimension_semantics=("arbitrary",),
                                             vmem_limit_bytes=VMEM_LIMIT),
        name="ret_step",
    )(qt, kt, v, state_ret)

    x1, h2, experts, gate_w, counts, pool = pl.pallas_call(
        _mix_sample_back_kernel,
        out_shape=[jax.ShapeDtypeStruct((n, d), F32),
                   jax.ShapeDtypeStruct((n, HALF), U32),
                   jax.ShapeDtypeStruct((TOP_K, n), I32),
                   jax.ShapeDtypeStruct((n, LANES), F32),
                   jax.ShapeDtypeStruct((N_EXPERTS, LANES), F32),
                   jax.ShapeDtypeStruct(state_pool.shape, F32)],
        compiler_params=params,
        name="mix_sample_back",
    )(x, mod, o, g, u, state_pool, w_pool, pool_scale, w_out16, norm2, wr_t, bias_t)
    return x1, h2, experts, gate_w, counts, s1, pool


def _plan_kernel(experts_ref, cnt_ref, pos_ref, meta_ref, carry_ref, off_ref, *, block_t):
    @pl.when(pl.program_id(0) == 0)
    def _():
        cnt = cnt_ref[...]
        n_tile = jnp.floor((cnt + (ROW_TILE - 1.0)) * (1.0 / ROW_TILE))
        upto = (lax.broadcasted_iota(I32, (N_EXPERTS, N_EXPERTS), 1)
                <= lax.broadcasted_iota(I32, (N_EXPERTS, N_EXPERTS), 0))
        tile_end = _dot(jnp.where(upto, 1.0, 0.0).astype(BF16), n_tile.astype(BF16))
        tile_start = tile_end - n_tile
        off_ref[...] = tile_start * ROW_TILE
        carry_ref[...] = jnp.zeros_like(carry_ref)
        lane = lax.broadcasted_iota(I32, cnt.shape, 1)
        meta_ref[...] = jnp.where(lane == 0, tile_start, jnp.where(lane == 1, n_tile, cnt)).astype(I32)

    e_blk = experts_ref[...]
    eidx = lax.broadcasted_iota(I32, (N_EXPERTS, block_t), 0)
    member = jnp.zeros((N_EXPERTS, block_t), F32)
    for s in range(TOP_K):
        member = member + jnp.where(eidx == e_blk[s:s + 1, :], 1.0, 0.0)
    before = (lax.broadcasted_iota(I32, (block_t, block_t), 0)
              < lax.broadcasted_iota(I32, (block_t, block_t), 1))
    rank = _dot(member.astype(BF16), jnp.where(before, 1.0, 0.0).astype(BF16))
    row = off_ref[:, 0:1] + carry_ref[:, 0:1] + rank
    carry_ref[...] += jnp.broadcast_to(jnp.sum(member, axis=1, keepdims=True), (N_EXPERTS, LANES))
    out = [jnp.sum(jnp.where(eidx == e_blk[s:s + 1, :], row, 0.0), axis=0, keepdims=True)
           for s in range(TOP_K)]
    pos_ref[...] = jnp.concatenate(out, axis=0).astype(I32)


def _plan(experts_all, counts, max_block=1024):
    n_tokens = experts_all.shape[1]
    block_t = max(k for k in range(LANES, max_block + 1, LANES) if n_tokens % k == 0)
    return pl.pallas_call(
        functools.partial(_plan_kernel, block_t=block_t),
        grid=(n_tokens // block_t,),
        in_specs=[pl.BlockSpec((TOP_K, block_t), lambda j: (0, j)), _full((N_EXPERTS, LANES))],
        out_specs=[pl.BlockSpec((TOP_K, block_t), lambda j: (0, j)), _full((N_EXPERTS, LANES))],
        out_shape=[jax.ShapeDtypeStruct((TOP_K, n_tokens), I32),
                   jax.ShapeDtypeStruct((N_EXPERTS, LANES), I32)],
        scratch_shapes=[pltpu.VMEM((N_EXPERTS, LANES), F32)] * 2,
        compiler_params=pltpu.CompilerParams(dimension_semantics=("arbitrary",),
                                             vmem_limit_bytes=VMEM_LIMIT),
        name="plan",
    )(experts_all, counts)


def _sc_workers():
    info = plsc.get_sparse_core_info()
    return info.num_cores, info.num_cores * info.num_subcores


def _sc_scatter_rows(sources, pos_t, n_out, after=()):
    w = sources[0].shape[1]
    s = pos_t.shape[0]
    n_cores, n_workers = _sc_workers()
    bounds = np.cumsum([0] + [src.shape[0] // SC_CHUNK for src in sources])
    n_chunks = int(bounds[-1])
    iters = -(-n_chunks // n_workers)
    mesh = plsc.VectorSubcoreMesh(core_axis_name="c", subcore_axis_name="s")

    @functools.partial(
        pl.kernel, mesh=mesh, out_type=jax.ShapeDtypeStruct((n_out, w), sources[0].dtype),
        scratch_types=[pltpu.VMEM((SC_CHUNK, w), sources[0].dtype), pltpu.VMEM((s, SC_CHUNK), I32),
                       pltpu.SemaphoreType.DMA],
        name="dispatch")
    def k(*refs):
        src_hbm, pos_hbm = refs[:len(sources)], refs[len(sources)]
        out_hbm, rows_v, idx_v, sem = refs[len(sources) + 1 + len(after):]
        wid = lax.axis_index("s") * n_cores + lax.axis_index("c")

        @pl.loop(0, iters)
        def _(it):
            c = it * n_workers + wid
            for src, lo, hi in zip(src_hbm, bounds[:-1], bounds[1:]):
                @pl.when((c >= int(lo)) & (c < int(hi)))
                def _():
                    base = pl.multiple_of((c - int(lo)) * SC_CHUNK, SC_CHUNK)
                    pltpu.sync_copy(src.at[pl.ds(base, SC_CHUNK)], rows_v)

            @pl.when(c < n_chunks)
            def _():
                base = pl.multiple_of(c * SC_CHUNK, SC_CHUNK)
                pltpu.sync_copy(pos_hbm.at[:, pl.ds(base, SC_CHUNK)], idx_v)
                copies = [pltpu.async_copy(rows_v, out_hbm.at[idx_v.at[j]], sem) for j in range(s)]
                for cp in copies:
                    cp.wait()

    return k(*sources, pos_t, *after)


def _sc_gather_rows(table, pos_t):
    _, w = table.shape
    s, t = pos_t.shape
    n_cores, n_workers = _sc_workers()
    n_chunks = t // SC_CHUNK
    iters = -(-n_chunks // n_workers)
    mesh = plsc.VectorSubcoreMesh(core_axis_name="c", subcore_axis_name="s")

    @functools.partial(
        pl.kernel, mesh=mesh, out_type=jax.ShapeDtypeStruct((s, t, w), table.dtype),
        scratch_types=[pltpu.VMEM((SC_CHUNK, w), table.dtype), pltpu.VMEM((s, SC_CHUNK), I32),
                       pltpu.SemaphoreType.DMA],
        name="combine")
    def k(table_hbm, pos_hbm, out_hbm, rows_v, idx_v, sem):
        wid = lax.axis_index("s") * n_cores + lax.axis_index("c")

        @pl.loop(0, iters)
        def _(it):
            c = it * n_workers + wid

            @pl.when(c < n_chunks)
            def _():
                base = pl.multiple_of(c * SC_CHUNK, SC_CHUNK)
                pltpu.sync_copy(pos_hbm.at[:, pl.ds(base, SC_CHUNK)], idx_v)
                for j in range(s):
                    pltpu.async_copy(table_hbm.at[idx_v.at[j]], rows_v, sem).wait()
                    pltpu.sync_copy(rows_v, out_hbm.at[j, pl.ds(base, SC_CHUNK)])

    return k(table, pos_t)


def _sc_pack_weights(w, rows_per_item, after=()):
    e, r, c = w.shape
    half = r // 2
    rb = rows_per_item
    per_expert = half // rb
    n_cores, n_workers = _sc_workers()
    per_worker = e * per_expert // n_workers
    assert per_worker * n_workers == e * per_expert and per_worker % 2 == 0 and c % (SC_LANES * SC_UNROLL) == 0
    mesh = plsc.VectorSubcoreMesh(core_axis_name="c", subcore_axis_name="s")

    @functools.partial(
        pl.kernel, mesh=mesh, out_type=jax.ShapeDtypeStruct((e * half, c), U32),
        scratch_types=[pltpu.VMEM((2, rb, c), F32), pltpu.VMEM((2, rb, c), F32), pltpu.VMEM((2, rb, c), U32),
                       pltpu.SemaphoreType.DMA((2,)), pltpu.SemaphoreType.DMA((2,))],
        compiler_params=pltpu.CompilerParams(needs_layout_passes=False),
        name="pack_weights")
    def k(w_hbm, *refs):
        out_hbm, lo_v, hi_v, out_v, in_sem, out_sem = refs[len(after):]
        wid = lax.axis_index("s") * n_cores + lax.axis_index("c")
        first = wid * per_worker

        def rows(item):
            ex = item // per_expert
            j = item - ex * per_expert
            return (pl.multiple_of(ex * r + j * rb, rb), pl.multiple_of(ex * r + half + j * rb, rb),
                    pl.multiple_of(ex * half + j * rb, rb))

        def loads(item, b):
            lo_row, hi_row, _ = rows(item)
            return (pltpu.make_async_copy(w_hbm.at[pl.ds(lo_row, rb)], lo_v.at[b], in_sem.at[b]),
                    pltpu.make_async_copy(w_hbm.at[pl.ds(hi_row, rb)], hi_v.at[b], in_sem.at[b]))

        def store(item, b):
            return pltpu.make_async_copy(out_v.at[b], out_hbm.at[pl.ds(rows(item)[2], rb)], out_sem.at[b])

        for cp in loads(first, 0):
            cp.start()

        @pl.loop(0, per_worker // 2)
        def _(pair):
            for b in range(2):
                item = first + pair * 2 + b
                for cp in loads(item, b):
                    cp.wait()

                @pl.when(item + 1 < first + per_worker)
                def _():
                    for cp in loads(item + 1, 1 - b):
                        cp.start()

                @pl.when(pair > 0)
                def _():
                    store(item - 2, b).wait()

                @pl.loop(0, rb)
                def _(i):
                    @pl.loop(0, c // (SC_LANES * SC_UNROLL))
                    def _(vb):
                        for u in range(SC_UNROLL):
                            sl = pl.ds(pl.multiple_of((vb * SC_UNROLL + u) * SC_LANES, SC_LANES), SC_LANES)
                            packed = plsc.pack(lo_v[b, i, sl], hi_v[b, i, sl], format=plsc.PackFormat.INTERLEAVED)
                            out_v[b, i, sl] = plsc.bitcast(packed, U32)

                store(item, b).start()

        for b in range(2):
            store(first + per_worker - 2 + b, b).wait()

    return k(w.reshape(e * r, c), *after).reshape(e, half, c)


def _experts_kernel(first_ref, ntile_ref, cnt_ref, xs_hbm, wg_ref, wu_ref, wd_ref, ys_hbm,
                    wg16_ref, wu16_ref, wd16_ref, x_buf, y_buf, in_sem, out_sem):
    ahead = STREAM_DEPTH - MAX_WIDTH
    e = pl.program_id(0)
    n_used = first_ref[N_EXPERTS - 1] + ntile_ref[N_EXPERTS - 1]
    first, n_mine, count = first_ref[e], ntile_ref[e], cnt_ref[e]

    def tile_rows(g):
        return pl.ds(pl.multiple_of(g * ROW_TILE, ROW_TILE), ROW_TILE)

    def load(g):
        slot = lax.rem(g, STREAM_DEPTH)
        return pltpu.make_async_copy(xs_hbm.at[tile_rows(g)], x_buf.at[slot], in_sem.at[slot])

    def store(g):
        slot = lax.rem(g, STREAM_DEPTH)
        return pltpu.make_async_copy(y_buf.at[slot], ys_hbm.at[tile_rows(g)], out_sem.at[slot])

    @pl.when(e == 0)
    def _():
        for g0 in range(ahead):
            @pl.when(g0 < n_used)
            def _():
                load(g0).start()

    for packed_ref, w16_ref in ((wg_ref, wg16_ref), (wu_ref, wu16_ref), (wd_ref, wd16_ref)):
        rows = packed_ref.shape[1]
        lo, hi = _unpack_rows(packed_ref[0])
        w16_ref[0:rows, :] = lo.astype(BF16)
        w16_ref[rows:, :] = hi.astype(BF16)

    def run(j, width):
        tiles = [first + j + t for t in range(width)]
        for g in tiles:
            load(g).wait()

            @pl.when(g + ahead < n_used)
            def _():
                load(g + ahead).start()

            @pl.when(g >= STREAM_DEPTH)
            def _():
                store(g - STREAM_DEPTH).wait()

        words = jnp.concatenate([x_buf[lax.rem(g, STREAM_DEPTH)] for g in tiles], axis=0)
        row = lax.broadcasted_iota(I32, words.shape, 0)
        words = jnp.where(row < count - j * ROW_TILE, words, jnp.uint32(0))
        lo, hi = _unpack_rows(words)
        lo, hi = lo.astype(BF16), hi.astype(BF16)
        hg = _dot(lo, wg16_ref[0:HALF, :]) + _dot(hi, wg16_ref[HALF:, :])
        hu = _dot(lo, wu16_ref[0:HALF, :]) + _dot(hi, wu16_ref[HALF:, :])
        a = (_silu(hg) * hu).astype(BF16)
        y = _pack_rows(_dot(a, wd16_ref[...]))
        for t, g in enumerate(tiles):
            y_buf[lax.rem(g, STREAM_DEPTH)] = y[t * ROW_TILE:(t + 1) * ROW_TILE]
            store(g).start()

    def widest(p, carry):
        run(MAX_WIDTH * p, MAX_WIDTH)
        return carry

    lax.fori_loop(0, n_mine // MAX_WIDTH, widest, 0)
    done = n_mine - lax.rem(n_mine, MAX_WIDTH)
    width = MAX_WIDTH // 2
    while width:
        has = lax.rem(n_mine // width, 2) == 1

        @pl.when(has)
        def _(width=width, done=done):
            run(done, width)

        done = done + jnp.where(has, width, 0)
        width //= 2

    @pl.when(e == N_EXPERTS - 1)
    def _():
        for back in range(STREAM_DEPTH, 0, -1):
            @pl.when(n_used >= back)
            def _():
                store(n_used - back).wait()


def _experts(xs, first_tile, n_tile, count, w_eg, w_eu, w_ed):
    d = D_MODEL
    by_expert = lambda e, *_: (e, 0, 0)
    grid_spec = pltpu.PrefetchScalarGridSpec(
        num_scalar_prefetch=3,
        grid=(N_EXPERTS,),
        in_specs=[pl.BlockSpec(memory_space=pl.ANY),
                  pl.BlockSpec((1, d // 2, EXPERT_DIM), by_expert),
                  pl.BlockSpec((1, d // 2, EXPERT_DIM), by_expert),
                  pl.BlockSpec((1, EXPERT_DIM // 2, d), by_expert)],
        out_specs=pl.BlockSpec(memory_space=pl.ANY),
        scratch_shapes=[pltpu.VMEM((d, EXPERT_DIM), BF16), pltpu.VMEM((d, EXPERT_DIM), BF16),
                        pltpu.VMEM((EXPERT_DIM, d), BF16),
                        pltpu.VMEM((STREAM_DEPTH, ROW_TILE, HALF), U32),
                        pltpu.VMEM((STREAM_DEPTH, ROW_TILE, HALF), U32),
                        pltpu.SemaphoreType.DMA((STREAM_DEPTH,)), pltpu.SemaphoreType.DMA((STREAM_DEPTH,))])
    return pl.pallas_call(
        _experts_kernel,
        grid_spec=grid_spec,
        out_shape=jax.ShapeDtypeStruct(xs.shape, U32),
        compiler_params=pltpu.CompilerParams(dimension_semantics=("arbitrary",),
                                             vmem_limit_bytes=VMEM_LIMIT),
        name="experts",
    )(first_tile, n_tile, count, xs, w_eg, w_eu, w_ed)


def _final_kernel(z_ref, gatew_ref, h2_ref, x1_ref, mod_ref, normf_ref, wsg_ref, wsu_ref, wsd_ref, *rest):
    y_ref, wsg16_ref, wsu16_ref, wsd16_ref = rest[-4:]

    @pl.when(pl.program_id(0) == 0)
    def _():
        wsg16_ref[...] = wsg_ref[...].astype(BF16)
        wsu16_ref[...] = wsu_ref[...].astype(BF16)
        wsd16_ref[...] = wsd_ref[...].astype(BF16)

    lo, hi = _unpack_rows(h2_ref[...])
    h = jnp.concatenate([lo, hi], axis=-1).astype(BF16)
    a = _silu(_dot(h, wsg16_ref[...])) * _dot(h, wsu16_ref[...])
    acc = _dot(a.astype(BF16), wsd16_ref[...])
    for s in range(TOP_K):
        lo, hi = _unpack_rows(z_ref[s])
        acc = acc + gatew_ref[:, s:s + 1] * jnp.concatenate([lo, hi], axis=-1)
    x2 = x1_ref[...] + _mod(mod_ref, 5) * acc
    y_ref[...] = _rms(x2) * normf_ref[...]


def _final(z, gate_w, h2, x1, mod, norm_f, w_sg, w_su, w_sd, block_t, first_block, per_seq,
           seq0=0, out_rows=None, y_prev=None):
    t, d = x1.shape
    out_rows = t if out_rows is None else out_rows
    out_first = seq0 * per_seq
    tok = lambda i: (i, 0)
    if per_seq:
        mod_spec = pl.BlockSpec((1, 6, d), lambda i: (seq0 + i // per_seq, 0, 0))
    else:
        mod_spec = pl.BlockSpec((block_t, 6 * d), tok)
    operands = [z, gate_w, h2, x1, mod, norm_f, w_sg, w_su, w_sd]
    in_specs = [pl.BlockSpec((TOP_K, block_t, HALF), lambda i: (0, first_block + i, 0)),
                pl.BlockSpec((block_t, LANES), tok),
                pl.BlockSpec((block_t, HALF), tok),
                pl.BlockSpec((block_t, d), tok),
                mod_spec,
                _full((1, d)),
                _full((d, EXPERT_DIM)), _full((d, EXPERT_DIM)), _full((EXPERT_DIM, d))]
    aliases = {}
    if y_prev is not None:
        aliases = {len(operands): 0}
        operands.append(y_prev)
        in_specs.append(pl.BlockSpec(memory_space=pl.ANY))
    return pl.pallas_call(
        _final_kernel,
        grid=(t // block_t,),
        in_specs=in_specs,
        out_specs=pl.BlockSpec((block_t, d), lambda i: (out_first + i, 0)),
        out_shape=jax.ShapeDtypeStruct((out_rows, d), F32),
        scratch_shapes=[pltpu.VMEM((d, EXPERT_DIM), BF16), pltpu.VMEM((d, EXPERT_DIM), BF16),
                        pltpu.VMEM((EXPERT_DIM, d), BF16)],
        input_output_aliases=aliases,
        compiler_params=pltpu.CompilerParams(dimension_semantics=("arbitrary",),
                                             vmem_limit_bytes=VMEM_LIMIT),
        name="final",
    )(*operands)


def kernel(x_prompt, x_sample, c_prompt, c_sample, state_ret, state_pool, norm1, norm2, norm_f,
           w_ada, b_ada, w_in, w_out, w_pool, pool_scale, w_router, router_bias, w_exp_gate,
           w_exp_up, w_exp_down, w_sh_gate, w_sh_up, w_sh_down):
    b, l, d = x_prompt.shape
    n = x_sample.shape[0]

    mod_p, mod_s = _ada(c_prompt, c_sample, w_ada[0], b_ada[0])
    mod_p = mod_p.reshape(b, 6, d)

    w_in16 = w_in[0].astype(BF16)
    w_out16 = w_out[0].astype(BF16)
    wr_t = w_router[0].T
    bias_t = jnp.broadcast_to(router_bias[0][:, None], (N_EXPERTS, LANES))
    n1, n2, nf = norm1[0].reshape(1, d), norm2[0].reshape(1, d), norm_f.reshape(1, d)
    ps = pool_scale[0].reshape(1, POOL_WIDTH)
    shared = (w_sh_gate[0], w_sh_up[0], w_sh_down[0])

    def routed(sources, experts, counts, repacked):
        n_tiles = experts.shape[1] * TOP_K // ROW_TILE + N_EXPERTS
        pos_t, meta = _plan(experts, counts)
        xs = _sc_scatter_rows(sources, pos_t, n_tiles * ROW_TILE, after=repacked)
        ys = _experts(xs, meta[:, 0], meta[:, 1], meta[:, 2], *expert_w)
        return _sc_gather_rows(ys, pos_t)

    mix_args = (n1, w_in16, w_pool[0], ps, w_out16, n2, wr_t, bias_t)
    x1_s, h2_s, experts_s, gatew_s, counts_s, ret_s, pool_s = _mix_sample(
        x_sample.reshape(n, d), mod_s, state_ret[0], state_pool[0], float(PAST_LEN), *mix_args)

    expert_w = (_sc_pack_weights(w_exp_gate[0], 64, after=(x1_s,)),
                _sc_pack_weights(w_exp_up[0], 64, after=(x1_s,)),
                _sc_pack_weights(w_exp_down[0], 16, after=(x1_s,)))

    ba = b // 2
    bb = b - ba
    x1_a, h2_a, experts_a, gatew_a, counts_a, ret_a, pool_a = _mix_prompt(x_prompt, mod_p, 0, ba, *mix_args)
    z_a = routed((h2_a,), experts_a, counts_a, expert_w)
    x1_b, h2_b, experts_b, gatew_b, counts_b, ret_b, pool_b = _mix_prompt(x_prompt, mod_p, ba, bb, *mix_args)
    z_b = routed((h2_b, h2_s), jnp.concatenate([experts_b, experts_s], axis=1), counts_b + counts_s,
                 expert_w[:2])

    block_t = 512
    per_seq = l // block_t
    y_s = _final(z_b, gatew_s, h2_s, x1_s, mod_s, nf, *shared,
                 block_t=n, first_block=bb * l // n, per_seq=0)
    y_p = _final(z_b, gatew_b, h2_b, x1_b.reshape(bb * l, d), mod_p, nf, *shared,
                 block_t=block_t, first_block=0, per_seq=per_seq, seq0=ba, out_rows=b * l)
    y_p = _final(z_a, gatew_a, h2_a, x1_a.reshape(ba * l, d), mod_p, nf, *shared,
                 block_t=block_t, first_block=0, per_seq=per_seq, seq0=0, out_rows=b * l, y_prev=y_p)

    ret_p = jnp.concatenate([ret_a, ret_b], axis=0)
    pool_p = jnp.concatenate([pool_a, pool_b], axis=0)
    return (y_p.reshape(b, l, d), y_s.reshape(n, 1, d), ret_p[None], pool_p[None],
            ret_s[None], pool_s[None])
```

```python
import functools

import jax
import jax.numpy as jnp
import numpy as np
from jax import lax
from jax.experimental import pallas as pl
from jax.experimental.pallas import tpu as pltpu
from jax.experimental.pallas import tpu_sc as plsc

D_MODEL = 1024
RET_HEADS = 4
RET_QK_DIM = 64
RET_V_DIM = 128
RET_WIDTH = RET_HEADS * RET_V_DIM
QK_WIDTH = RET_HEADS * RET_QK_DIM
ROPE_BASE = 10000.0
POOL_WINDOWS = (2, 4, 8, 16)
POOL_WIDTH = 512
POOL_GROUP_DIM = 128
POOL_BUF = 15
IN_WIDTH = 2 * QK_WIDTH + 2 * RET_WIDTH + POOL_WIDTH
N_EXPERTS = 64
TOP_K = 8
N_EXPERT_GROUPS = 8
GROUP_SIZE = N_EXPERTS // N_EXPERT_GROUPS
TOP_GROUPS = 4
EXPERT_DIM = 256
ROUTE_SCALE = 2.5
EPS = 1e-6
PAST_LEN = 16384

LANES = 128
SUBLANES = 8
POOL_CARRY = 24
VMEM_LIMIT = 56 * 1024 * 1024
HALF = D_MODEL // 2
ROW_TILE = 256
MAX_WIDTH = 4
SC_CHUNK = 128
SC_LANES = 16
SC_UNROLL = 16
STREAM_DEPTH = 16

BF16 = jnp.bfloat16
F32 = jnp.float32
U32 = jnp.uint32
I32 = jnp.int32


def _silu(x):
    return x * jax.nn.sigmoid(x)


def _dot(a, b):
    return jnp.dot(a, b, preferred_element_type=F32)


def _rms(x):
    return x * lax.rsqrt(jnp.mean(x * x, axis=-1, keepdims=True) + EPS)


def _mod(mod_ref, i, seq=0):
    if len(mod_ref.shape) == 3:
        return mod_ref[seq, i:i + 1, :]
    return mod_ref[:, i * D_MODEL:(i + 1) * D_MODEL]


def _split_bf16(x):
    hi = x.astype(BF16)
    lo = (x - hi.astype(F32)).astype(BF16)
    return hi, lo


def _pack_rows(x):
    lo = lax.bitcast_convert_type(x[:, :HALF].astype(BF16).astype(F32), U32)
    hi = lax.bitcast_convert_type(x[:, HALF:].astype(BF16).astype(F32), U32)
    return (hi & jnp.uint32(0xFFFF0000)) | (lo >> jnp.uint32(16))


def _unpack_rows(w):
    lo = lax.bitcast_convert_type(w << jnp.uint32(16), F32)
    hi = lax.bitcast_convert_type(w & jnp.uint32(0xFFFF0000), F32)
    return lo, hi


def _first_max_onehot(work, idx, n):
    m = jnp.max(work, axis=0, keepdims=True)
    first = jnp.min(jnp.where(work == m, idx, float(n)), axis=0, keepdims=True)
    return idx == first


def _route(h2, wr_t_ref, bias_t_ref):
    n = h2.shape[0]
    h_hi, h_lo = _split_bf16(h2)
    w_hi, w_lo = _split_bf16(wr_t_ref[...])
    nt = (((1,), (1,)), ((), ()))
    logits = (lax.dot_general(w_hi, h_hi, nt, preferred_element_type=F32)
              + lax.dot_general(w_hi, h_lo, nt, preferred_element_type=F32)
              + lax.dot_general(w_lo, h_hi, nt, preferred_element_type=F32))
    scores = jax.nn.sigmoid(logits)
    biased = scores + bias_t_ref[:, 0:1]
    b3 = biased.reshape(N_EXPERT_GROUPS, GROUP_SIZE, n)
    i3 = lax.broadcasted_iota(I32, b3.shape, 1).astype(F32)
    m1 = jnp.max(b3, axis=1, keepdims=True)
    first = jnp.min(jnp.where(b3 == m1, i3, float(GROUP_SIZE)), axis=1, keepdims=True)
    m2 = jnp.max(jnp.where(i3 == first, -jnp.inf, b3), axis=1, keepdims=True)
    gscore = (m1 + m2).reshape(N_EXPERT_GROUPS, n)
    gidx = lax.broadcasted_iota(I32, gscore.shape, 0).astype(F32)
    gsel = jnp.zeros(gscore.shape, F32)
    work = gscore
    for _ in range(TOP_GROUPS):
        hit = _first_max_onehot(work, gidx, N_EXPERT_GROUPS)
        gsel = jnp.where(hit, 1.0, gsel)
        work = jnp.where(hit, -jnp.inf, work)
    gsel3 = jnp.broadcast_to(gsel.reshape(N_EXPERT_GROUPS, 1, n), b3.shape)
    work = jnp.where(gsel3 > 0.0, b3, -jnp.inf).reshape(N_EXPERTS, n)
    eidx = lax.broadcasted_iota(I32, work.shape, 0).astype(F32)
    sel = jnp.zeros(work.shape, F32)
    for _ in range(TOP_K):
        hit = _first_max_onehot(work, eidx, N_EXPERTS)
        sel = jnp.where(hit, 1.0, sel)
        work = jnp.where(hit, -jnp.inf, work)
    picked = jnp.where(sel > 0.0, scores, 0.0)
    gates = picked / jnp.sum(picked, axis=0, keepdims=True) * ROUTE_SCALE
    below = (lax.broadcasted_iota(I32, (N_EXPERTS, N_EXPERTS), 1)
             < lax.broadcasted_iota(I32, (N_EXPERTS, N_EXPERTS), 0))
    slot = _dot(jnp.where(below, 1.0, 0.0).astype(BF16), sel.astype(BF16))
    e_rows, w_rows = [], []
    for s in range(TOP_K):
        here = jnp.where(slot == float(s), sel, 0.0)
        e_rows.append(jnp.sum(here * eidx, axis=0, keepdims=True))
        w_rows.append(jnp.sum(here * gates, axis=0, keepdims=True))
    experts = jnp.concatenate(e_rows, axis=0).astype(I32)
    w_t = jnp.concatenate(w_rows + [jnp.zeros((LANES - TOP_K, n), F32)], axis=0)
    counts = jnp.broadcast_to(jnp.sum(sel, axis=1, keepdims=True), (N_EXPERTS, LANES))
    return experts, w_t.T, counts


def _group_norm_gate(o, g):
    parts = []
    for h in range(RET_HEADS):
        oh = o[:, h * RET_V_DIM:(h + 1) * RET_V_DIM]
        mu = jnp.mean(oh, axis=-1, keepdims=True)
        ctr = oh - mu
        var = jnp.mean(ctr * ctr, axis=-1, keepdims=True)
        parts.append(ctr * lax.rsqrt(var + EPS))
    return _silu(g) * jnp.concatenate(parts, axis=-1)


def _pool_project(pooled, w_pool_ref, pool_scale_ref):
    parts = [_dot(p.astype(BF16), w_pool_ref[gi].astype(BF16)) for gi, p in enumerate(pooled)]
    return jnp.concatenate(parts, axis=-1) * pool_scale_ref[...]


def _out_residual(x, o_gated, p, mod_ref, seq, w_out_ref):
    mix = jnp.concatenate([o_gated, p], axis=-1).astype(BF16)
    return x + _mod(mod_ref, 2, seq) * _dot(mix, w_out_ref[...])


def _norm_route(x1, mod_ref, seq, norm2_ref, wr_t_ref, bias_t_ref, h2_ref, experts_ref, gatew_ref, cnt_ref):
    h2 = _rms(x1) * norm2_ref[...] * (1.0 + _mod(mod_ref, 4, seq)) + _mod(mod_ref, 3, seq)
    h2_ref[...] = _pack_rows(h2)
    experts, gate_w, counts = _route(h2, wr_t_ref, bias_t_ref)
    experts_ref[...] = experts
    gatew_ref[...] = gate_w
    cnt_ref[...] += counts


def _ada_kernel(cp_ref, cs_ref, w_ref, b_ref, op_ref, os_ref):
    w16 = w_ref[...].astype(BF16)
    for c_ref, o_ref in ((cp_ref, op_ref), (cs_ref, os_ref)):
        o_ref[...] = _dot(_silu(c_ref[...]).astype(BF16), w16) + b_ref[...]


def _ada(c_prompt, c_sample, w_ada, b_ada, block_n=1536):
    d, width = w_ada.shape
    rows = lambda c: pl.BlockSpec((c.shape[0], d), lambda j: (0, 0))
    cols = lambda c: pl.BlockSpec((c.shape[0], block_n), lambda j: (0, j))
    return pl.pallas_call(
        _ada_kernel,
        grid=(width // block_n,),
        in_specs=[rows(c_prompt), rows(c_sample),
                  pl.BlockSpec((d, block_n), lambda j: (0, j)),
                  pl.BlockSpec((1, block_n), lambda j: (0, j))],
        out_specs=[cols(c_prompt), cols(c_sample)],
        out_shape=[jax.ShapeDtypeStruct((c.shape[0], width), F32) for c in (c_prompt, c_sample)],
        compiler_params=pltpu.CompilerParams(vmem_limit_bytes=VMEM_LIMIT),
        name="ada",
    )(c_prompt, c_sample, w_ada, b_ada.reshape(1, width))


def _mix_prompt_kernel(x_ref, mod_ref, norm1_ref, w_in_ref, cos_ref, sin_ref, dmat_ref, cross_ref,
                       tail_ref, cdec_ref, w_pool_ref, pool_scale_ref, w_out_ref, norm2_ref,
                       wr_t_ref, bias_t_ref,
                       x1_ref, h2_ref, experts_ref, gatew_ref, cnt_ref, ret_ref, pool_ref,
                       state_ref, ext_ref, win_ref, o_ref, *, block_l, chunk, seqs):
    li = pl.program_id(1)

    @pl.when((pl.program_id(0) == 0) & (li == 0))
    def _():
        cnt_ref[...] = jnp.zeros_like(cnt_ref)

    @pl.when(li == 0)
    def _():
        state_ref[...] = jnp.zeros_like(state_ref)
        ext_ref[:, 0:POOL_CARRY, :] = jnp.zeros((seqs, POOL_CARRY, POOL_WIDTH), F32)
        win_ref[:, 0:SUBLANES, :] = jnp.zeros((seqs, SUBLANES, POOL_WIDTH), F32)

    for seq in range(seqs):
        _mix_prompt_seq(seq, li, x_ref, mod_ref, norm1_ref, w_in_ref, cos_ref, sin_ref, dmat_ref, cross_ref,
                        tail_ref, cdec_ref, w_pool_ref, pool_scale_ref, w_out_ref, norm2_ref,
                        wr_t_ref, bias_t_ref, x1_ref, h2_ref, experts_ref, gatew_ref, cnt_ref,
                        state_ref.at[seq], ext_ref.at[seq], win_ref.at[seq], o_ref.at[seq],
                        block_l=block_l, chunk=chunk)

    @pl.when(li == pl.num_programs(1) - 1)
    def _():
        ret_ref[...] = state_ref[...].reshape(ret_ref.shape)
        pool_ref[...] = ext_ref[:, POOL_CARRY - POOL_BUF:POOL_CARRY, :]


def _window_sums(ext_ref, win_ref, block_l):
    g = POOL_GROUP_DIM
    top = POOL_CARRY + block_l
    new = slice(POOL_CARRY - SUBLANES, None)
    s2 = ext_ref[SUBLANES:top, :] + ext_ref[SUBLANES - 1:top - 1, :]
    win_ref[SUBLANES:top, g:] = s2[:, g:]
    s4 = s2[:, g:] + win_ref[SUBLANES - 2:top - 2, g:]
    win_ref[SUBLANES:top, 2 * g:] = s4[:, g:]
    s8 = s4[:, g:] + win_ref[SUBLANES - 4:top - 4, 2 * g:]
    win_ref[SUBLANES:top, 3 * g:] = s8[:, g:]
    s16 = s8[:, g:] + win_ref[0:top - SUBLANES, 3 * g:]
    return [s2[new, 0:g], s4[new, 0:g], s8[new, 0:g], s16[new, :]]


def _mix_prompt_seq(seq, li, x_ref, mod_ref, norm1_ref, w_in_ref, cos_ref, sin_ref, dmat_ref, cross_ref,
                    tail_ref, cdec_ref, w_pool_ref, pool_scale_ref, w_out_ref, norm2_ref,
                    wr_t_ref, bias_t_ref, x1_ref, h2_ref, experts_ref, gatew_ref, cnt_ref,
                    state_ref, ext_ref, win_ref, o_ref, *, block_l, chunk):
    x = x_ref[seq]
    h16 = (_rms(x) * norm1_ref[...] * (1.0 + _mod(mod_ref, 1, seq)) + _mod(mod_ref, 0, seq)).astype(BF16)
    proj = _dot(h16, w_in_ref[...])
    q = proj[:, 0:QK_WIDTH]
    k = proj[:, QK_WIDTH:2 * QK_WIDTH]
    v = proj[:, 2 * QK_WIDTH:2 * QK_WIDTH + RET_WIDTH]
    g = proj[:, 2 * QK_WIDTH + RET_WIDTH:2 * QK_WIDTH + 2 * RET_WIDTH]
    u = proj[:, 2 * QK_WIDTH + 2 * RET_WIDTH:]

    lane = lax.broadcasted_iota(I32, q.shape, 1)
    first_half = (lane % RET_QK_DIM) < (RET_QK_DIM // 2)
    cos_t = cos_ref[...]
    sin_t = sin_ref[...]

    def rot(t):
        partner = jnp.where(first_half, pltpu.roll(t, QK_WIDTH - RET_QK_DIM // 2, axis=1),
                            pltpu.roll(t, RET_QK_DIM // 2, axis=1))
        return t * cos_t + partner * sin_t

    q = rot(q)
    k = rot(k) * (RET_QK_DIM ** -0.5)
    k_t = k.T
    v16 = v.astype(BF16)
    head_of_lane = lax.broadcasted_iota(I32, (chunk, QK_WIDTH), 1) // RET_QK_DIM

    for c in range(block_l // chunk):
        rows = slice(c * chunk, (c + 1) * chunk)
        q_c = q[rows]
        kt_c = k_t[:, rows]
        kt16 = kt_c.astype(BF16)
        state16 = state_ref[...].astype(BF16)
        for hd in range(RET_HEADS):
            in_head = head_of_lane == hd
            q_h = jnp.where(in_head, q_c, 0.0).astype(BF16)
            v_h = v16[rows, hd * RET_V_DIM:(hd + 1) * RET_V_DIM]
            scores = _dot(q_h, kt16) * dmat_ref[hd]
            inner = _dot(scores.astype(BF16), v_h)
            cross = _dot(q_h, state16) * cross_ref[hd]
            o_ref[rows, hd * RET_V_DIM:(hd + 1) * RET_V_DIM] = inner + cross
            hrows = slice(hd * RET_QK_DIM, (hd + 1) * RET_QK_DIM)
            k_dec = (kt_c[hrows] * tail_ref[hd:hd + 1, :]).astype(BF16)
            state_ref[hrows, :] = state_ref[hrows, :] * cdec_ref[hd] + _dot(k_dec, v_h)

    o_gated = _group_norm_gate(o_ref[...], g)

    ext_ref[POOL_CARRY:POOL_CARRY + block_l, :] = u
    pos = (li * block_l + lax.broadcasted_iota(I32, (block_l, 1), 0)).astype(F32)
    pooled = []
    for gi, (w, acc) in enumerate(zip(POOL_WINDOWS, _window_sums(ext_ref, win_ref, block_l))):
        cnt = jnp.minimum(pos + 1.0, float(w))
        pooled.append(acc / cnt - u[:, gi * POOL_GROUP_DIM:(gi + 1) * POOL_GROUP_DIM])
    p = _pool_project(pooled, w_pool_ref, pool_scale_ref)
    ext_ref[0:POOL_CARRY, :] = ext_ref[block_l:block_l + POOL_CARRY, :]

    x1 = _out_residual(x, o_gated, p, mod_ref, seq, w_out_ref)
    x1_ref[seq] = x1
    _norm_route(x1, mod_ref, seq, norm2_ref, wr_t_ref, bias_t_ref,
                h2_ref.at[seq], experts_ref.at[seq], gatew_ref.at[seq], cnt_ref)


def _decay_tables(chunk):
    f32 = np.float32
    lg = np.log(f32(1.0) - f32(2.0) ** (f32(-5.0) - np.arange(RET_HEADS, dtype=f32))).astype(f32)
    idx = np.arange(chunk, dtype=f32)
    diff = idx[:, None] - idx[None, :]
    causal = diff >= 0
    dmat = np.where(causal[None], np.exp(lg[:, None, None] * np.where(causal, diff, f32(0.0))[None]), f32(0.0))
    cross = np.exp(lg[:, None] * (idx[None, :] + f32(1.0)))
    cross = np.broadcast_to(cross[:, :, None], (RET_HEADS, chunk, RET_V_DIM))
    tail = np.exp(lg[:, None] * (f32(chunk - 1.0) - idx)[None, :])
    cdec = np.broadcast_to(np.exp(lg * f32(chunk))[:, None, None], (RET_HEADS, RET_QK_DIM, RET_V_DIM))
    return tuple(jnp.asarray(np.ascontiguousarray(t, dtype=f32)) for t in (dmat, cross, tail, cdec))


def _rotary_angles(pos):
    half = RET_QK_DIM // 2
    freqs = (np.float32(ROPE_BASE) ** (-np.arange(half, dtype=np.float32) / np.float32(half))).astype(np.float32)
    return np.asarray(pos, np.float32)[:, None] * freqs[None, :]


def _rotary_tables(pos):
    ang = _rotary_angles(pos)
    cos, sin = np.cos(ang), np.sin(ang)
    cos_t = np.tile(np.concatenate([cos, cos], axis=-1), (1, RET_HEADS))
    sin_t = np.tile(np.concatenate([-sin, sin], axis=-1), (1, RET_HEADS))
    return jnp.asarray(cos_t, F32), jnp.asarray(sin_t, F32)


def _full(shape):
    return pl.BlockSpec(shape, lambda *_: (0,) * len(shape))


def _mix_prompt(x, mod, b0, b, norm1, w_in16, w_pool, pool_scale, w_out16, norm2, wr_t, bias_t,
                block_l=512, chunk=256, seqs=2):
    _, l, d = x.shape
    nl = l // block_l
    s0 = b0 // seqs
    cos_t, sin_t = _rotary_tables(np.arange(l))
    dmat, cross, tail, cdec = _decay_tables(chunk)
    kernel = functools.partial(_mix_prompt_kernel, block_l=block_l, chunk=chunk, seqs=seqs)
    tok = lambda bi, li: (bi, li, 0)
    per_seq = lambda bi, li: (bi, 0, 0)
    x1, h2, experts, gate_w, counts, ret, pool = pl.pallas_call(
        kernel,
        grid=(b // seqs, nl),
        in_specs=[pl.BlockSpec((seqs, block_l, d), lambda bi, li: (s0 + bi, li, 0)),
                  pl.BlockSpec((seqs, 6, d), lambda bi, li: (s0 + bi, 0, 0)),
                  _full((1, d)),
                  _full((d, IN_WIDTH)),
                  pl.BlockSpec((block_l, QK_WIDTH), lambda bi, li: (li, 0)),
                  pl.BlockSpec((block_l, QK_WIDTH), lambda bi, li: (li, 0)),
                  _full(dmat.shape), _full(cross.shape), _full(tail.shape), _full(cdec.shape),
                  _full(w_pool.shape), _full((1, POOL_WIDTH)), _full((d, d)), _full((1, d)),
                  _full(wr_t.shape), _full(bias_t.shape)],
        out_specs=[pl.BlockSpec((seqs, block_l, d), tok),
                   pl.BlockSpec((seqs, block_l, HALF), tok),
                   pl.BlockSpec((seqs, TOP_K, block_l), lambda bi, li: (bi, 0, li)),
                   pl.BlockSpec((seqs, block_l, LANES), tok),
                   _full((N_EXPERTS, LANES)),
                   pl.BlockSpec((seqs, RET_HEADS, RET_QK_DIM, RET_V_DIM), lambda bi, li: (bi, 0, 0, 0)),
                   pl.BlockSpec((seqs, POOL_BUF, POOL_WIDTH), per_seq)],
        out_shape=[jax.ShapeDtypeStruct((b, l, d), F32),
                   jax.ShapeDtypeStruct((b, l, HALF), U32),
                   jax.ShapeDtypeStruct((b, TOP_K, l), I32),
                   jax.ShapeDtypeStruct((b, l, LANES), F32),
                   jax.ShapeDtypeStruct((N_EXPERTS, LANES), F32),
                   jax.ShapeDtypeStruct((b, RET_HEADS, RET_QK_DIM, RET_V_DIM), F32),
                   jax.ShapeDtypeStruct((b, POOL_BUF, POOL_WIDTH), F32)],
        scratch_shapes=[pltpu.VMEM((seqs, QK_WIDTH, RET_V_DIM), F32),
                        pltpu.VMEM((seqs, POOL_CARRY + block_l, POOL_WIDTH), F32),
                        pltpu.VMEM((seqs, POOL_CARRY + block_l, POOL_WIDTH), F32),
                        pltpu.VMEM((seqs, block_l, RET_WIDTH), F32)],
        compiler_params=pltpu.CompilerParams(dimension_semantics=("arbitrary", "arbitrary"),
                                             vmem_limit_bytes=VMEM_LIMIT),
        name="mix_prompt",
    )(x, mod, norm1, w_in16, cos_t, sin_t, dmat, cross, tail, cdec, w_pool, pool_scale,
      w_out16, norm2, wr_t, bias_t)
    experts = jnp.transpose(experts, (1, 0, 2)).reshape(TOP_K, b * l)
    return x1, h2.reshape(b * l, HALF), experts, gate_w.reshape(b * l, LANES), counts, ret, pool


def _mix_sample_front_kernel(x_ref, mod_ref, norm1_ref, w_in_ref, cos_ref, sin_ref,
                             qt_ref, kt_ref, v_ref, g_ref, u_ref):
    x = x_ref[...]
    h = _rms(x) * norm1_ref[...] * (1.0 + _mod(mod_ref, 1)) + _mod(mod_ref, 0)
    proj = _dot(h.astype(BF16), w_in_ref[...])
    half = RET_QK_DIM // 2
    cos_c = cos_ref[...]
    sin_c = sin_ref[...]

    def rot_t(t):
        parts = []
        for hd in range(RET_HEADS):
            t1 = t[hd * RET_QK_DIM:hd * RET_QK_DIM + half]
            t2 = t[hd * RET_QK_DIM + half:(hd + 1) * RET_QK_DIM]
            parts += [t1 * cos_c - t2 * sin_c, t1 * sin_c + t2 * cos_c]
        return jnp.concatenate(parts, axis=0)

    qt_ref[...] = rot_t(proj[:, 0:QK_WIDTH].T)
    kt_ref[...] = rot_t(proj[:, QK_WIDTH:2 * QK_WIDTH].T) * (RET_QK_DIM ** -0.5)
    v_ref[...] = proj[:, 2 * QK_WIDTH:2 * QK_WIDTH + RET_WIDTH]
    g_ref[...] = proj[:, 2 * QK_WIDTH + RET_WIDTH:2 * QK_WIDTH + 2 * RET_WIDTH]
    u_ref[...] = proj[:, 2 * QK_WIDTH + 2 * RET_WIDTH:]


def _ret_step_kernel(qt_ref, kt_ref, v_ref, s0_ref, o_ref, s1_ref, *, block_b, decays):
    i = pl.program_id(0)
    lane = lax.broadcasted_iota(I32, qt_ref.shape, 1)
    for j in range(block_b):
        bi = i * block_b + j
        here = lane == bi
        q_col = jnp.sum(jnp.where(here, qt_ref[...], 0.0), axis=1, keepdims=True)
        k_col = jnp.sum(jnp.where(here, kt_ref[...], 0.0), axis=1, keepdims=True)
        v_row = v_ref[pl.ds(bi, 1), :]
        outs = []
        for hd in range(RET_HEADS):
            hrows = slice(hd * RET_QK_DIM, (hd + 1) * RET_QK_DIM)
            s1 = decays[hd] * s0_ref[j, hd] + k_col[hrows] * v_row[:, hd * RET_V_DIM:(hd + 1) * RET_V_DIM]
            s1_ref[j, hd] = s1
            outs.append(jnp.sum(q_col[hrows] * s1, axis=0, keepdims=True))
        o_ref[pl.ds(bi, 1), :] = jnp.concatenate(outs, axis=-1)


def _mix_sample_back_kernel(x_ref, mod_ref, o_ref, g_ref, u_ref, buf_ref, w_pool_ref, pool_scale_ref,
                            w_out_ref, norm2_ref, wr_t_ref, bias_t_ref,
                            x1_ref, h2_ref, experts_ref, gatew_ref, cnt_ref, pool_ref):
    cnt_ref[...] = jnp.zeros_like(cnt_ref)
    o_gated = _group_norm_gate(o_ref[...], g_ref[...])
    u = u_ref[...]
    pooled = []
    for gi, w in enumerate(POOL_WINDOWS):
        lanes = slice(gi * POOL_GROUP_DIM, (gi + 1) * POOL_GROUP_DIM)
        acc = u[:, lanes]
        for j in range(1, w):
            acc = acc + buf_ref[:, POOL_BUF - j, lanes]
        pooled.append(acc / float(w) - u[:, lanes])
    p = _pool_project(pooled, w_pool_ref, pool_scale_ref)
    pool_ref[:, 0:POOL_BUF - 1, :] = buf_ref[:, 1:POOL_BUF, :]
    pool_ref[:, POOL_BUF - 1, :] = u
    x1 = _out_residual(x_ref[...], o_gated, p, mod_ref, 0, w_out_ref)
    x1_ref[...] = x1
    _norm_route(x1, mod_ref, 0, norm2_ref, wr_t_ref, bias_t_ref, h2_ref, experts_ref, gatew_ref, cnt_ref)


def _mix_sample(x, mod, state_ret, state_pool, start, norm1, w_in16, w_pool,
                pool_scale, w_out16, norm2, wr_t, bias_t, block_b=32):
    n, d = x.shape
    half = RET_QK_DIM // 2
    ang = _rotary_angles([start])
    cos_c = jnp.asarray(np.broadcast_to(np.cos(ang).T, (half, n)), F32)
    sin_c = jnp.asarray(np.broadcast_to(np.sin(ang).T, (half, n)), F32)
    params = pltpu.CompilerParams(vmem_limit_bytes=VMEM_LIMIT)
    qt, kt, v, g, u = pl.pallas_call(
        _mix_sample_front_kernel,
        out_shape=[jax.ShapeDtypeStruct((QK_WIDTH, n), F32), jax.ShapeDtypeStruct((QK_WIDTH, n), F32),
                   jax.ShapeDtypeStruct((n, RET_WIDTH), F32), jax.ShapeDtypeStruct((n, RET_WIDTH), F32),
                   jax.ShapeDtypeStruct((n, POOL_WIDTH), F32)],
        compiler_params=params,
        name="mix_sample_front",
    )(x, mod, norm1, w_in16, cos_c, sin_c)

    lg = np.log(1.0 - 2.0 ** (-5.0 - np.arange(RET_HEADS, dtype=np.float32)), dtype=np.float32)
    decays = tuple(float(np.exp(lg[h])) for h in range(RET_HEADS))
    state_block = (block_b, RET_HEADS, RET_QK_DIM, RET_V_DIM)
    o, s1 = pl.pallas_call(
        functools.partial(_ret_step_kernel, block_b=block_b, decays=decays),
        grid=(n // block_b,),
        in_specs=[_full((QK_WIDTH, n)), _full((QK_WIDTH, n)), _full((n, RET_WIDTH)),
                  pl.BlockSpec(state_block, lambda i: (i, 0, 0, 0))],
        out_specs=[_full((n, RET_WIDTH)), pl.BlockSpec(state_block, lambda i: (i, 0, 0, 0))],
        out_shape=[jax.ShapeDtypeStruct((n, RET_WIDTH), F32),
                   jax.ShapeDtypeStruct(state_ret.shape, F32)],
        compiler_params=pltpu.CompilerParams(dimension_semantics=("arbitrary",),
                                             vmem_limit_bytes=VMEM_LIMIT),
        name="ret_step",
    )(qt, kt, v, state_ret)

    x1, h2, experts, gate_w, counts, pool = pl.pallas_call(
        _mix_sample_back_kernel,
        out_shape=[jax.ShapeDtypeStruct((n, d), F32),
                   jax.ShapeDtypeStruct((n, HALF), U32),
                   jax.ShapeDtypeStruct((TOP_K, n), I32),
                   jax.ShapeDtypeStruct((n, LANES), F32),
                   jax.ShapeDtypeStruct((N_EXPERTS, LANES), F32),
                   jax.ShapeDtypeStruct(state_pool.shape, F32)],
        compiler_params=params,
        name="mix_sample_back",
    )(x, mod, o, g, u, state_pool, w_pool, pool_scale, w_out16, norm2, wr_t, bias_t)
    return x1, h2, experts, gate_w, counts, s1, pool


def _plan_kernel(experts_ref, cnt_ref, pos_ref, meta_ref, carry_ref, off_ref, *, block_t):
    @pl.when(pl.program_id(0) == 0)
    def _():
        cnt = cnt_ref[...]
        n_tile = jnp.floor((cnt + (ROW_TILE - 1.0)) * (1.0 / ROW_TILE))
        upto = (lax.broadcasted_iota(I32, (N_EXPERTS, N_EXPERTS), 1)
                <= lax.broadcasted_iota(I32, (N_EXPERTS, N_EXPERTS), 0))
        tile_end = _dot(jnp.where(upto, 1.0, 0.0).astype(BF16), n_tile.astype(BF16))
        tile_start = tile_end - n_tile
        off_ref[...] = tile_start * ROW_TILE
        carry_ref[...] = jnp.zeros_like(carry_ref)
        lane = lax.broadcasted_iota(I32, cnt.shape, 1)
        meta_ref[...] = jnp.where(lane == 0, tile_start, jnp.where(lane == 1, n_tile, cnt)).astype(I32)

    e_blk = experts_ref[...]
    eidx = lax.broadcasted_iota(I32, (N_EXPERTS, block_t), 0)
    member = jnp.zeros((N_EXPERTS, block_t), F32)
    for s in range(TOP_K):
        member = member + jnp.where(eidx == e_blk[s:s + 1, :], 1.0, 0.0)
    before = (lax.broadcasted_iota(I32, (block_t, block_t), 0)
              < lax.broadcasted_iota(I32, (block_t, block_t), 1))
    rank = _dot(member.astype(BF16), jnp.where(before, 1.0, 0.0).astype(BF16))
    row = off_ref[:, 0:1] + carry_ref[:, 0:1] + rank
    carry_ref[...] += jnp.broadcast_to(jnp.sum(member, axis=1, keepdims=True), (N_EXPERTS, LANES))
    out = [jnp.sum(jnp.where(eidx == e_blk[s:s + 1, :], row, 0.0), axis=0, keepdims=True)
           for s in range(TOP_K)]
    pos_ref[...] = jnp.concatenate(out, axis=0).astype(I32)


def _plan(experts_all, counts, max_block=1024):
    n_tokens = experts_all.shape[1]
    block_t = max(k for k in range(LANES, max_block + 1, LANES) if n_tokens % k == 0)
    return pl.pallas_call(
        functools.partial(_plan_kernel, block_t=block_t),
        grid=(n_tokens // block_t,),
        in_specs=[pl.BlockSpec((TOP_K, block_t), lambda j: (0, j)), _full((N_EXPERTS, LANES))],
        out_specs=[pl.BlockSpec((TOP_K, block_t), lambda j: (0, j)), _full((N_EXPERTS, LANES))],
        out_shape=[jax.ShapeDtypeStruct((TOP_K, n_tokens), I32),
                   jax.ShapeDtypeStruct((N_EXPERTS, LANES), I32)],
        scratch_shapes=[pltpu.VMEM((N_EXPERTS, LANES), F32)] * 2,
        compiler_params=pltpu.CompilerParams(dimension_semantics=("arbitrary",),
                                             vmem_limit_bytes=VMEM_LIMIT),
        name="plan",
    )(experts_all, counts)


def _sc_workers():
    info = plsc.get_sparse_core_info()
    return info.num_cores, info.num_cores * info.num_subcores


def _sc_scatter_rows(sources, pos_t, n_out, after=()):
    w = sources[0].shape[1]
    s = pos_t.shape[0]
    n_cores, n_workers = _sc_workers()
    bounds = np.cumsum([0] + [src.shape[0] // SC_CHUNK for src in sources])
    n_chunks = int(bounds[-1])
    iters = -(-n_chunks // n_workers)
    mesh = plsc.VectorSubcoreMesh(core_axis_name="c", subcore_axis_name="s")

    @functools.partial(
        pl.kernel, mesh=mesh, out_type=jax.ShapeDtypeStruct((n_out, w), sources[0].dtype),
        scratch_types=[pltpu.VMEM((SC_CHUNK, w), sources[0].dtype), pltpu.VMEM((s, SC_CHUNK), I32),
                       pltpu.SemaphoreType.DMA],
        name="dispatch")
    def k(*refs):
        src_hbm, pos_hbm = refs[:len(sources)], refs[len(sources)]
        out_hbm, rows_v, idx_v, sem = refs[len(sources) + 1 + len(after):]
        wid = lax.axis_index("s") * n_cores + lax.axis_index("c")

        @pl.loop(0, iters)
        def _(it):
            c = it * n_workers + wid
            for src, lo, hi in zip(src_hbm, bounds[:-1], bounds[1:]):
                @pl.when((c >= int(lo)) & (c < int(hi)))
                def _():
                    base = pl.multiple_of((c - int(lo)) * SC_CHUNK, SC_CHUNK)
                    pltpu.sync_copy(src.at[pl.ds(base, SC_CHUNK)], rows_v)

            @pl.when(c < n_chunks)
            def _():
                base = pl.multiple_of(c * SC_CHUNK, SC_CHUNK)
                pltpu.sync_copy(pos_hbm.at[:, pl.ds(base, SC_CHUNK)], idx_v)
                copies = [pltpu.async_copy(rows_v, out_hbm.at[idx_v.at[j]], sem) for j in range(s)]
                for cp in copies:
                    cp.wait()

    return k(*sources, pos_t, *after)


def _sc_gather_rows(table, pos_t):
    _, w = table.shape
    s, t = pos_t.shape
    n_cores, n_workers = _sc_workers()
    n_chunks = t // SC_CHUNK
    iters = -(-n_chunks // n_workers)
    mesh = plsc.VectorSubcoreMesh(core_axis_name="c", subcore_axis_name="s")

    @functools.partial(
        pl.kernel, mesh=mesh, out_type=jax.ShapeDtypeStruct((s, t, w), table.dtype),
        scratch_types=[pltpu.VMEM((SC_CHUNK, w), table.dtype), pltpu.VMEM((s, SC_CHUNK), I32),
                       pltpu.SemaphoreType.DMA],
        name="combine")
    def k(table_hbm, pos_hbm, out_hbm, rows_v, idx_v, sem):
        wid = lax.axis_index("s") * n_cores + lax.axis_index("c")

        @pl.loop(0, iters)
        def _(it):
            c = it * n_workers + wid

            @pl.when(c < n_chunks)
            def _():
                base = pl.multiple_of(c * SC_CHUNK, SC_CHUNK)
                pltpu.sync_copy(pos_hbm.at[:, pl.ds(base, SC_CHUNK)], idx_v)
                for j in range(s):
                    pltpu.async_copy(table_hbm.at[idx_v.at[j]], rows_v, sem).wait()
                    pltpu.sync_copy(rows_v, out_hbm.at[j, pl.ds(base, SC_CHUNK)])

    return k(table, pos_t)


def _sc_pack_weights(w, rows_per_item, after=()):
    e, r, c = w.shape
    half = r // 2
    rb = rows_per_item
    per_expert = half // rb
    n_cores, n_workers = _sc_workers()
    per_worker = e * per_expert // n_workers
    assert per_worker * n_workers == e * per_expert and per_worker % 2 == 0 and c % (SC_LANES * SC_UNROLL) == 0
    mesh = plsc.VectorSubcoreMesh(core_axis_name="c", subcore_axis_name="s")

    @functools.partial(
        pl.kernel, mesh=mesh, out_type=jax.ShapeDtypeStruct((e * half, c), U32),
        scratch_types=[pltpu.VMEM((2, rb, c), F32), pltpu.VMEM((2, rb, c), F32), pltpu.VMEM((2, rb, c), U32),
                       pltpu.SemaphoreType.DMA((2,)), pltpu.SemaphoreType.DMA((2,))],
        compiler_params=pltpu.CompilerParams(needs_layout_passes=False),
        name="pack_weights")
    def k(w_hbm, *refs):
        out_hbm, lo_v, hi_v, out_v, in_sem, out_sem = refs[len(after):]
        wid = lax.axis_index("s") * n_cores + lax.axis_index("c")
        first = wid * per_worker

        def rows(item):
            ex = item // per_expert
            j = item - ex * per_expert
            return (pl.multiple_of(ex * r + j * rb, rb), pl.multiple_of(ex * r + half + j * rb, rb),
                    pl.multiple_of(ex * half + j * rb, rb))

        def loads(item, b):
            lo_row, hi_row, _ = rows(item)
            return (pltpu.make_async_copy(w_hbm.at[pl.ds(lo_row, rb)], lo_v.at[b], in_sem.at[b]),
                    pltpu.make_async_copy(w_hbm.at[pl.ds(hi_row, rb)], hi_v.at[b], in_sem.at[b]))

        def store(item, b):
            return pltpu.make_async_copy(out_v.at[b], out_hbm.at[pl.ds(rows(item)[2], rb)], out_sem.at[b])

        for cp in loads(first, 0):
            cp.start()

        @pl.loop(0, per_worker // 2)
        def _(pair):
            for b in range(2):
                item = first + pair * 2 + b
                for cp in loads(item, b):
                    cp.wait()

                @pl.when(item + 1 < first + per_worker)
                def _():
                    for cp in loads(item + 1, 1 - b):
                        cp.start()

                @pl.when(pair > 0)
                def _():
                    store(item - 2, b).wait()

                @pl.loop(0, rb)
                def _(i):
                    @pl.loop(0, c // (SC_LANES * SC_UNROLL))
                    def _(vb):
                        for u in range(SC_UNROLL):
                            sl = pl.ds(pl.multiple_of((vb * SC_UNROLL + u) * SC_LANES, SC_LANES), SC_LANES)
                            packed = plsc.pack(lo_v[b, i, sl], hi_v[b, i, sl], format=plsc.PackFormat.INTERLEAVED)
                            out_v[b, i, sl] = plsc.bitcast(packed, U32)

                store(item, b).start()

        for b in range(2):
            store(first + per_worker - 2 + b, b).wait()

    return k(w.reshape(e * r, c), *after).reshape(e, half, c)


def _experts_kernel(first_ref, ntile_ref, cnt_ref, xs_hbm, wg_ref, wu_ref, wd_ref, ys_hbm,
                    wg16_ref, wu16_ref, wd16_ref, x_buf, y_buf, in_sem, out_sem):
    ahead = STREAM_DEPTH - MAX_WIDTH
    e = pl.program_id(0)
    n_used = first_ref[N_EXPERTS - 1] + ntile_ref[N_EXPERTS - 1]
    first, n_mine, count = first_ref[e], ntile_ref[e], cnt_ref[e]

    def tile_rows(g):
        return pl.ds(pl.multiple_of(g * ROW_TILE, ROW_TILE), ROW_TILE)

    def load(g):
        slot = lax.rem(g, STREAM_DEPTH)
        return pltpu.make_async_copy(xs_hbm.at[tile_rows(g)], x_buf.at[slot], in_sem.at[slot])

    def store(g):
        slot = lax.rem(g, STREAM_DEPTH)
        return pltpu.make_async_copy(y_buf.at[slot], ys_hbm.at[tile_rows(g)], out_sem.at[slot])

    @pl.when(e == 0)
    def _():
        for g0 in range(ahead):
            @pl.when(g0 < n_used)
            def _():
                load(g0).start()

    for packed_ref, w16_ref in ((wg_ref, wg16_ref), (wu_ref, wu16_ref), (wd_ref, wd16_ref)):
        rows = packed_ref.shape[1]
        lo, hi = _unpack_rows(packed_ref[0])
        w16_ref[0:rows, :] = lo.astype(BF16)
        w16_ref[rows:, :] = hi.astype(BF16)

    def run(j, width):
        tiles = [first + j + t for t in range(width)]
        for g in tiles:
            load(g).wait()

            @pl.when(g + ahead < n_used)
            def _():
                load(g + ahead).start()

            @pl.when(g >= STREAM_DEPTH)
            def _():
                store(g - STREAM_DEPTH).wait()

        words = jnp.concatenate([x_buf[lax.rem(g, STREAM_DEPTH)] for g in tiles], axis=0)
        row = lax.broadcasted_iota(I32, words.shape, 0)
        words = jnp.where(row < count - j * ROW_TILE, words, jnp.uint32(0))
        lo, hi = _unpack_rows(words)
        lo, hi = lo.astype(BF16), hi.astype(BF16)
        hg = _dot(lo, wg16_ref[0:HALF, :]) + _dot(hi, wg16_ref[HALF:, :])
        hu = _dot(lo, wu16_ref[0:HALF, :]) + _dot(hi, wu16_ref[HALF:, :])
        a = (_silu(hg) * hu).astype(BF16)
        y = _pack_rows(_dot(a, wd16_ref[...]))
        for t, g in enumerate(tiles):
            y_buf[lax.rem(g, STREAM_DEPTH)] = y[t * ROW_TILE:(t + 1) * ROW_TILE]
            store(g).start()

    def widest(p, carry):
        run(MAX_WIDTH * p, MAX_WIDTH)
        return carry

    lax.fori_loop(0, n_mine // MAX_WIDTH, widest, 0)
    done = n_mine - lax.rem(n_mine, MAX_WIDTH)
    width = MAX_WIDTH // 2
    while width:
        has = lax.rem(n_mine // width, 2) == 1

        @pl.when(has)
        def _(width=width, done=done):
            run(done, width)

        done = done + jnp.where(has, width, 0)
        width //= 2

    @pl.when(e == N_EXPERTS - 1)
    def _():
        for back in range(STREAM_DEPTH, 0, -1):
            @pl.when(n_used >= back)
            def _():
                store(n_used - back).wait()


def _experts(xs, first_tile, n_tile, count, w_eg, w_eu, w_ed):
    d = D_MODEL
    by_expert = lambda e, *_: (e, 0, 0)
    grid_spec = pltpu.PrefetchScalarGridSpec(
        num_scalar_prefetch=3,
        grid=(N_EXPERTS,),
        in_specs=[pl.BlockSpec(memory_space=pl.ANY),
                  pl.BlockSpec((1, d // 2, EXPERT_DIM), by_expert),
                  pl.BlockSpec((1, d // 2, EXPERT_DIM), by_expert),
                  pl.BlockSpec((1, EXPERT_DIM // 2, d), by_expert)],
        out_specs=pl.BlockSpec(memory_space=pl.ANY),
        scratch_shapes=[pltpu.VMEM((d, EXPERT_DIM), BF16), pltpu.VMEM((d, EXPERT_DIM), BF16),
                        pltpu.VMEM((EXPERT_DIM, d), BF16),
                        pltpu.VMEM((STREAM_DEPTH, ROW_TILE, HALF), U32),
                        pltpu.VMEM((STREAM_DEPTH, ROW_TILE, HALF), U32),
                        pltpu.SemaphoreType.DMA((STREAM_DEPTH,)), pltpu.SemaphoreType.DMA((STREAM_DEPTH,))])
    return pl.pallas_call(
        _experts_kernel,
        grid_spec=grid_spec,
        out_shape=jax.ShapeDtypeStruct(xs.shape, U32),
        compiler_params=pltpu.CompilerParams(dimension_semantics=("arbitrary",),
                                             vmem_limit_bytes=VMEM_LIMIT),
        name="experts",
    )(first_tile, n_tile, count, xs, w_eg, w_eu, w_ed)


def _final_kernel(z_ref, gatew_ref, h2_ref, x1_ref, mod_ref, normf_ref, wsg_ref, wsu_ref, wsd_ref, *rest):
    y_ref, wsg16_ref, wsu16_ref, wsd16_ref = rest[-4:]

    @pl.when(pl.program_id(0) == 0)
    def _():
        wsg16_ref[...] = wsg_ref[...].astype(BF16)
        wsu16_ref[...] = wsu_ref[...].astype(BF16)
        wsd16_ref[...] = wsd_ref[...].astype(BF16)

    lo, hi = _unpack_rows(h2_ref[...])
    h = jnp.concatenate([lo, hi], axis=-1).astype(BF16)
    a = _silu(_dot(h, wsg16_ref[...])) * _dot(h, wsu16_ref[...])
    acc = _dot(a.astype(BF16), wsd16_ref[...])
    for s in range(TOP_K):
        lo, hi = _unpack_rows(z_ref[s])
        acc = acc + gatew_ref[:, s:s + 1] * jnp.concatenate([lo, hi], axis=-1)
    x2 = x1_ref[...] + _mod(mod_ref, 5) * acc
    y_ref[...] = _rms(x2) * normf_ref[...]


def _final(z, gate_w, h2, x1, mod, norm_f, w_sg, w_su, w_sd, block_t, first_block, per_seq,
           seq0=0, out_rows=None, y_prev=None):
    t, d = x1.shape
    out_rows = t if out_rows is None else out_rows
    out_first = seq0 * per_seq
    tok = lambda i: (i, 0)
    if per_seq:
        mod_spec = pl.BlockSpec((1, 6, d), lambda i: (seq0 + i // per_seq, 0, 0))
    else:
        mod_spec = pl.BlockSpec((block_t, 6 * d), tok)
    operands = [z, gate_w, h2, x1, mod, norm_f, w_sg, w_su, w_sd]
    in_specs = [pl.BlockSpec((TOP_K, block_t, HALF), lambda i: (0, first_block + i, 0)),
                pl.BlockSpec((block_t, LANES), tok),
                pl.BlockSpec((block_t, HALF), tok),
                pl.BlockSpec((block_t, d), tok),
                mod_spec,
                _full((1, d)),
                _full((d, EXPERT_DIM)), _full((d, EXPERT_DIM)), _full((EXPERT_DIM, d))]
    aliases = {}
    if y_prev is not None:
        aliases = {len(operands): 0}
        operands.append(y_prev)
        in_specs.append(pl.BlockSpec(memory_space=pl.ANY))
    return pl.pallas_call(
        _final_kernel,
        grid=(t // block_t,),
        in_specs=in_specs,
        out_specs=pl.BlockSpec((block_t, d), lambda i: (out_first + i, 0)),
        out_shape=jax.ShapeDtypeStruct((out_rows, d), F32),
        scratch_shapes=[pltpu.VMEM((d, EXPERT_DIM), BF16), pltpu.VMEM((d, EXPERT_DIM), BF16),
                        pltpu.VMEM((EXPERT_DIM, d), BF16)],
        input_output_aliases=aliases,
        compiler_params=pltpu.CompilerParams(dimension_semantics=("arbitrary",),
                                             vmem_limit_bytes=VMEM_LIMIT),
        name="final",
    )(*operands)


def kernel(x_prompt, x_sample, c_prompt, c_sample, state_ret, state_pool, norm1, norm2, norm_f,
           w_ada, b_ada, w_in, w_out, w_pool, pool_scale, w_router, router_bias, w_exp_gate,
           w_exp_up, w_exp_down, w_sh_gate, w_sh_up, w_sh_down):
    b, l, d = x_prompt.shape
    n = x_sample.shape[0]

    mod_p, mod_s = _ada(c_prompt, c_sample, w_ada[0], b_ada[0])
    mod_p = mod_p.reshape(b, 6, d)

    w_in16 = w_in[0].astype(BF16)
    w_out16 = w_out[0].astype(BF16)
    wr_t = w_router[0].T
    bias_t = jnp.broadcast_to(router_bias[0][:, None], (N_EXPERTS, LANES))
    n1, n2, nf = norm1[0].reshape(1, d), norm2[0].reshape(1, d), norm_f.reshape(1, d)
    ps = pool_scale[0].reshape(1, POOL_WIDTH)
    shared = (w_sh_gate[0], w_sh_up[0], w_sh_down[0])

    def routed(sources, experts, counts, repacked):
        n_tiles = experts.shape[1] * TOP_K // ROW_TILE + N_EXPERTS
        pos_t, meta = _plan(experts, counts)
        xs = _sc_scatter_rows(sources, pos_t, n_tiles * ROW_TILE, after=repacked)
        ys = _experts(xs, meta[:, 0], meta[:, 1], meta[:, 2], *expert_w)
        return _sc_gather_rows(ys, pos_t)

    mix_args = (n1, w_in16, w_pool[0], ps, w_out16, n2, wr_t, bias_t)
    x1_s, h2_s, experts_s, gatew_s, counts_s, ret_s, pool_s = _mix_sample(
        x_sample.reshape(n, d), mod_s, state_ret[0], state_pool[0], float(PAST_LEN), *mix_args)

    expert_w = (_sc_pack_weights(w_exp_gate[0], 64, after=(x1_s,)),
                _sc_pack_weights(w_exp_up[0], 64, after=(x1_s,)),
                _sc_pack_weights(w_exp_down[0], 16, after=(x1_s,)))

    ba = b // 2
    bb = b - ba
    x1_a, h2_a, experts_a, gatew_a, counts_a, ret_a, pool_a = _mix_prompt(x_prompt, mod_p, 0, ba, *mix_args)
    z_a = routed((h2_a,), experts_a, counts_a, expert_w)
    x1_b, h2_b, experts_b, gatew_b, counts_b, ret_b, pool_b = _mix_prompt(x_prompt, mod_p, ba, bb, *mix_args)
    z_b = routed((h2_b, h2_s), jnp.concatenate([experts_b, experts_s], axis=1), counts_b + counts_s,
                 expert_w[:2])

    block_t = 512
    per_seq = l // block_t
    y_s = _final(z_b, gatew_s, h2_s, x1_s, mod_s, nf, *shared,
                 block_t=n, first_block=bb * l // n, per_seq=0)
    y_p = _final(z_b, gatew_b, h2_b, x1_b.reshape(bb * l, d), mod_p, nf, *shared,
                 block_t=block_t, first_block=0, per_seq=per_seq, seq0=ba, out_rows=b * l)
    y_p = _final(z_a, gatew_a, h2_a, x1_a.reshape(ba * l, d), mod_p, nf, *shared,
                 block_t=block_t, first_block=0, per_seq=per_seq, seq0=0, out_rows=b * l, y_prev=y_p)

    ret_p = jnp.concatenate([ret_a, ret_b], axis=0)
    pool_p = jnp.concatenate([pool_a, pool_b], axis=0)
    return (y_p.reshape(b, l, d), y_s.reshape(n, 1, d), ret_p[None], pool_p[None],
            ret_s[None], pool_s[None])
```

```python
import functools

import jax
import jax.numpy as jnp
import numpy as np
from jax import lax
from jax.experimental import pallas as pl
from jax.experimental.pallas import tpu as pltpu
from jax.experimental.pallas import tpu_sc as plsc

D_MODEL = 1024
RET_HEADS = 4
RET_QK_DIM = 64
RET_V_DIM = 128
RET_WIDTH = RET_HEADS * RET_V_DIM
QK_WIDTH = RET_HEADS * RET_QK_DIM
ROPE_BASE = 10000.0
POOL_WINDOWS = (2, 4, 8, 16)
POOL_WIDTH = 512
POOL_GROUP_DIM = 128
POOL_BUF = 15
IN_WIDTH = 2 * QK_WIDTH + 2 * RET_WIDTH + POOL_WIDTH
N_EXPERTS = 64
TOP_K = 8
N_EXPERT_GROUPS = 8
GROUP_SIZE = N_EXPERTS // N_EXPERT_GROUPS
TOP_GROUPS = 4
EXPERT_DIM = 256
ROUTE_SCALE = 2.5
EPS = 1e-6
PAST_LEN = 16384

LANES = 128
SUBLANES = 8
POOL_CARRY = 24
VMEM_LIMIT = 56 * 1024 * 1024
HALF = D_MODEL // 2
ROW_TILE = 128
MAX_WIDTH = 8
SC_CHUNK = 128
SC_LANES = 16
SC_UNROLL = 16
STREAM_DEPTH = 32

BF16 = jnp.bfloat16
F32 = jnp.float32
U32 = jnp.uint32
I32 = jnp.int32


def _silu(x):
    return x * jax.nn.sigmoid(x)


def _dot(a, b):
    return jnp.dot(a, b, preferred_element_type=F32)


def _rms(x):
    return x * lax.rsqrt(jnp.mean(x * x, axis=-1, keepdims=True) + EPS)


def _mod(mod_ref, i, seq=0):
    if len(mod_ref.shape) == 3:
        return mod_ref[seq, i:i + 1, :]
    return mod_ref[:, i * D_MODEL:(i + 1) * D_MODEL]


def _split_bf16(x):
    hi = x.astype(BF16)
    lo = (x - hi.astype(F32)).astype(BF16)
    return hi, lo


def _pack_rows(x):
    lo = lax.bitcast_convert_type(x[:, :HALF].astype(BF16).astype(F32), U32)
    hi = lax.bitcast_convert_type(x[:, HALF:].astype(BF16).astype(F32), U32)
    return (hi & jnp.uint32(0xFFFF0000)) | (lo >> jnp.uint32(16))


def _unpack_rows(w):
    lo = lax.bitcast_convert_type(w << jnp.uint32(16), F32)
    hi = lax.bitcast_convert_type(w & jnp.uint32(0xFFFF0000), F32)
    return lo, hi


def _first_max_onehot(work, idx, n):
    m = jnp.max(work, axis=0, keepdims=True)
    first = jnp.min(jnp.where(work == m, idx, float(n)), axis=0, keepdims=True)
    return idx == first


def _route(h2, wr_t_ref, bias_t_ref):
    n = h2.shape[0]
    h_hi, h_lo = _split_bf16(h2)
    w_hi, w_lo = _split_bf16(wr_t_ref[...])
    nt = (((1,), (1,)), ((), ()))
    logits = (lax.dot_general(w_hi, h_hi, nt, preferred_element_type=F32)
              + lax.dot_general(w_hi, h_lo, nt, preferred_element_type=F32)
              + lax.dot_general(w_lo, h_hi, nt, preferred_element_type=F32))
    scores = jax.nn.sigmoid(logits)
    biased = scores + bias_t_ref[:, 0:1]
    b3 = biased.reshape(N_EXPERT_GROUPS, GROUP_SIZE, n)
    i3 = lax.broadcasted_iota(I32, b3.shape, 1).astype(F32)
    m1 = jnp.max(b3, axis=1, keepdims=True)
    first = jnp.min(jnp.where(b3 == m1, i3, float(GROUP_SIZE)), axis=1, keepdims=True)
    m2 = jnp.max(jnp.where(i3 == first, -jnp.inf, b3), axis=1, keepdims=True)
    gscore = (m1 + m2).reshape(N_EXPERT_GROUPS, n)
    gidx = lax.broadcasted_iota(I32, gscore.shape, 0).astype(F32)
    gsel = jnp.zeros(gscore.shape, F32)
    work = gscore
    for _ in range(TOP_GROUPS):
        hit = _first_max_onehot(work, gidx, N_EXPERT_GROUPS)
        gsel = jnp.where(hit, 1.0, gsel)
        work = jnp.where(hit, -jnp.inf, work)
    gsel3 = jnp.broadcast_to(gsel.reshape(N_EXPERT_GROUPS, 1, n), b3.shape)
    work = jnp.where(gsel3 > 0.0, b3, -jnp.inf).reshape(N_EXPERTS, n)
    eidx = lax.broadcasted_iota(I32, work.shape, 0).astype(F32)
    sel = jnp.zeros(work.shape, F32)
    for _ in range(TOP_K):
        hit = _first_max_onehot(work, eidx, N_EXPERTS)
        sel = jnp.where(hit, 1.0, sel)
        work = jnp.where(hit, -jnp.inf, work)
    picked = jnp.where(sel > 0.0, scores, 0.0)
    gates = picked / jnp.sum(picked, axis=0, keepdims=True) * ROUTE_SCALE
    below = (lax.broadcasted_iota(I32, (N_EXPERTS, N_EXPERTS), 1)
             < lax.broadcasted_iota(I32, (N_EXPERTS, N_EXPERTS), 0))
    slot = _dot(jnp.where(below, 1.0, 0.0).astype(BF16), sel.astype(BF16))
    e_rows, w_rows = [], []
    for s in range(TOP_K):
        here = jnp.where(slot == float(s), sel, 0.0)
        e_rows.append(jnp.sum(here * eidx, axis=0, keepdims=True))
        w_rows.append(jnp.sum(here * gates, axis=0, keepdims=True))
    experts = jnp.concatenate(e_rows, axis=0).astype(I32)
    w_t = jnp.concatenate(w_rows + [jnp.zeros((LANES - TOP_K, n), F32)], axis=0)
    counts = jnp.broadcast_to(jnp.sum(sel, axis=1, keepdims=True), (N_EXPERTS, LANES))
    return experts, w_t.T, counts


def _group_norm_gate(o, g):
    parts = []
    for h in range(RET_HEADS):
        oh = o[:, h * RET_V_DIM:(h + 1) * RET_V_DIM]
        mu = jnp.mean(oh, axis=-1, keepdims=True)
        ctr = oh - mu
        var = jnp.mean(ctr * ctr, axis=-1, keepdims=True)
        parts.append(ctr * lax.rsqrt(var + EPS))
    return _silu(g) * jnp.concatenate(parts, axis=-1)


def _pool_project(pooled, w_pool_ref, pool_scale_ref):
    parts = [_dot(p.astype(BF16), w_pool_ref[gi].astype(BF16)) for gi, p in enumerate(pooled)]
    return jnp.concatenate(parts, axis=-1) * pool_scale_ref[...]


def _out_residual(x, o_gated, p, mod_ref, seq, w_out_ref):
    mix = jnp.concatenate([o_gated, p], axis=-1).astype(BF16)
    return x + _mod(mod_ref, 2, seq) * _dot(mix, w_out_ref[...])


def _norm_route(x1, mod_ref, seq, norm2_ref, wr_t_ref, bias_t_ref, h2_ref, experts_ref, gatew_ref, cnt_ref):
    h2 = _rms(x1) * norm2_ref[...] * (1.0 + _mod(mod_ref, 4, seq)) + _mod(mod_ref, 3, seq)
    h2_ref[...] = _pack_rows(h2)
    experts, gate_w, counts = _route(h2, wr_t_ref, bias_t_ref)
    experts_ref[...] = experts
    gatew_ref[...] = gate_w
    cnt_ref[...] += counts


def _ada_kernel(cp_ref, cs_ref, w_ref, b_ref, op_ref, os_ref):
    w16 = w_ref[...].astype(BF16)
    for c_ref, o_ref in ((cp_ref, op_ref), (cs_ref, os_ref)):
        o_ref[...] = _dot(_silu(c_ref[...]).astype(BF16), w16) + b_ref[...]


def _ada(c_prompt, c_sample, w_ada, b_ada, block_n=1536):
    d, width = w_ada.shape
    rows = lambda c: pl.BlockSpec((c.shape[0], d), lambda j: (0, 0))
    cols = lambda c: pl.BlockSpec((c.shape[0], block_n), lambda j: (0, j))
    return pl.pallas_call(
        _ada_kernel,
        grid=(width // block_n,),
        in_specs=[rows(c_prompt), rows(c_sample),
                  pl.BlockSpec((d, block_n), lambda j: (0, j)),
                  pl.BlockSpec((1, block_n), lambda j: (0, j))],
        out_specs=[cols(c_prompt), cols(c_sample)],
        out_shape=[jax.ShapeDtypeStruct((c.shape[0], width), F32) for c in (c_prompt, c_sample)],
        compiler_params=pltpu.CompilerParams(vmem_limit_bytes=VMEM_LIMIT),
        name="ada",
    )(c_prompt, c_sample, w_ada, b_ada.reshape(1, width))


def _mix_prompt_kernel(x_ref, mod_ref, norm1_ref, w_in_ref, cos_ref, sin_ref, dmat_ref, cross_ref,
                       tail_ref, cdec_ref, w_pool_ref, pool_scale_ref, w_out_ref, norm2_ref,
                       wr_t_ref, bias_t_ref,
                       x1_ref, h2_ref, experts_ref, gatew_ref, cnt_ref, ret_ref, pool_ref,
                       state_ref, ext_ref, win_ref, o_ref, *, block_l, chunk, seqs):
    li = pl.program_id(1)

    @pl.when((pl.program_id(0) == 0) & (li == 0))
    def _():
        cnt_ref[...] = jnp.zeros_like(cnt_ref)

    @pl.when(li == 0)
    def _():
        state_ref[...] = jnp.zeros_like(state_ref)
        ext_ref[:, 0:POOL_CARRY, :] = jnp.zeros((seqs, POOL_CARRY, POOL_WIDTH), F32)
        win_ref[:, 0:SUBLANES, :] = jnp.zeros((seqs, SUBLANES, POOL_WIDTH), F32)

    for seq in range(seqs):
        _mix_prompt_seq(seq, li, x_ref, mod_ref, norm1_ref, w_in_ref, cos_ref, sin_ref, dmat_ref, cross_ref,
                        tail_ref, cdec_ref, w_pool_ref, pool_scale_ref, w_out_ref, norm2_ref,
                        wr_t_ref, bias_t_ref, x1_ref, h2_ref, experts_ref, gatew_ref, cnt_ref,
                        state_ref.at[seq], ext_ref.at[seq], win_ref.at[seq], o_ref.at[seq],
                        block_l=block_l, chunk=chunk)

    @pl.when(li == pl.num_programs(1) - 1)
    def _():
        ret_ref[...] = state_ref[...].reshape(ret_ref.shape)
        pool_ref[...] = ext_ref[:, POOL_CARRY - POOL_BUF:POOL_CARRY, :]


def _window_sums(ext_ref, win_ref, block_l):
    g = POOL_GROUP_DIM
    top = POOL_CARRY + block_l
    new = slice(POOL_CARRY - SUBLANES, None)
    s2 = ext_ref[SUBLANES:top, :] + ext_ref[SUBLANES - 1:top - 1, :]
    win_ref[SUBLANES:top, g:] = s2[:, g:]
    s4 = s2[:, g:] + win_ref[SUBLANES - 2:top - 2, g:]
    win_ref[SUBLANES:top, 2 * g:] = s4[:, g:]
    s8 = s4[:, g:] + win_ref[SUBLANES - 4:top - 4, 2 * g:]
    win_ref[SUBLANES:top, 3 * g:] = s8[:, g:]
    s16 = s8[:, g:] + win_ref[0:top - SUBLANES, 3 * g:]
    return [s2[new, 0:g], s4[new, 0:g], s8[new, 0:g], s16[new, :]]


def _mix_prompt_seq(seq, li, x_ref, mod_ref, norm1_ref, w_in_ref, cos_ref, sin_ref, dmat_ref, cross_ref,
                    tail_ref, cdec_ref, w_pool_ref, pool_scale_ref, w_out_ref, norm2_ref,
                    wr_t_ref, bias_t_ref, x1_ref, h2_ref, experts_ref, gatew_ref, cnt_ref,
                    state_ref, ext_ref, win_ref, o_ref, *, block_l, chunk):
    x = x_ref[seq]
    h16 = (_rms(x) * norm1_ref[...] * (1.0 + _mod(mod_ref, 1, seq)) + _mod(mod_ref, 0, seq)).astype(BF16)
    proj = _dot(h16, w_in_ref[...])
    q = proj[:, 0:QK_WIDTH]
    k = proj[:, QK_WIDTH:2 * QK_WIDTH]
    v = proj[:, 2 * QK_WIDTH:2 * QK_WIDTH + RET_WIDTH]
    g = proj[:, 2 * QK_WIDTH + RET_WIDTH:2 * QK_WIDTH + 2 * RET_WIDTH]
    u = proj[:, 2 * QK_WIDTH + 2 * RET_WIDTH:]

    lane = lax.broadcasted_iota(I32, q.shape, 1)
    first_half = (lane % RET_QK_DIM) < (RET_QK_DIM // 2)
    cos_t = cos_ref[...]
    sin_t = sin_ref[...]

    def rot(t):
        partner = jnp.where(first_half, pltpu.roll(t, QK_WIDTH - RET_QK_DIM // 2, axis=1),
                            pltpu.roll(t, RET_QK_DIM // 2, axis=1))
        return t * cos_t + partner * sin_t

    q = rot(q)
    k = rot(k) * (RET_QK_DIM ** -0.5)
    k_t = k.T
    v16 = v.astype(BF16)
    head_of_lane = lax.broadcasted_iota(I32, (chunk, QK_WIDTH), 1) // RET_QK_DIM

    for c in range(block_l // chunk):
        rows = slice(c * chunk, (c + 1) * chunk)
        q_c = q[rows]
        kt_c = k_t[:, rows]
        kt16 = kt_c.astype(BF16)
        state16 = state_ref[...].astype(BF16)
        for hd in range(RET_HEADS):
            in_head = head_of_lane == hd
            q_h = jnp.where(in_head, q_c, 0.0).astype(BF16)
            v_h = v16[rows, hd * RET_V_DIM:(hd + 1) * RET_V_DIM]
            scores = _dot(q_h, kt16) * dmat_ref[hd]
            inner = _dot(scores.astype(BF16), v_h)
            cross = _dot(q_h, state16) * cross_ref[hd]
            o_ref[rows, hd * RET_V_DIM:(hd + 1) * RET_V_DIM] = inner + cross
            hrows = slice(hd * RET_QK_DIM, (hd + 1) * RET_QK_DIM)
            k_dec = (kt_c[hrows] * tail_ref[hd:hd + 1, :]).astype(BF16)
            state_ref[hrows, :] = state_ref[hrows, :] * cdec_ref[hd] + _dot(k_dec, v_h)

    o_gated = _group_norm_gate(o_ref[...], g)

    ext_ref[POOL_CARRY:POOL_CARRY + block_l, :] = u
    pos = (li * block_l + lax.broadcasted_iota(I32, (block_l, 1), 0)).astype(F32)
    pooled = []
    for gi, (w, acc) in enumerate(zip(POOL_WINDOWS, _window_sums(ext_ref, win_ref, block_l))):
        cnt = jnp.minimum(pos + 1.0, float(w))
        pooled.append(acc / cnt - u[:, gi * POOL_GROUP_DIM:(gi + 1) * POOL_GROUP_DIM])
    p = _pool_project(pooled, w_pool_ref, pool_scale_ref)
    ext_ref[0:POOL_CARRY, :] = ext_ref[block_l:block_l + POOL_CARRY, :]

    x1 = _out_residual(x, o_gated, p, mod_ref, seq, w_out_ref)
    x1_ref[seq] = x1
    _norm_route(x1, mod_ref, seq, norm2_ref, wr_t_ref, bias_t_ref,
                h2_ref.at[seq], experts_ref.at[seq], gatew_ref.at[seq], cnt_ref)


def _decay_tables(chunk):
    f32 = np.float32
    lg = np.log(f32(1.0) - f32(2.0) ** (f32(-5.0) - np.arange(RET_HEADS, dtype=f32))).astype(f32)
    idx = np.arange(chunk, dtype=f32)
    diff = idx[:, None] - idx[None, :]
    causal = diff >= 0
    dmat = np.where(causal[None], np.exp(lg[:, None, None] * np.where(causal, diff, f32(0.0))[None]), f32(0.0))
    cross = np.exp(lg[:, None] * (idx[None, :] + f32(1.0)))
    cross = np.broadcast_to(cross[:, :, None], (RET_HEADS, chunk, RET_V_DIM))
    tail = np.exp(lg[:, None] * (f32(chunk - 1.0) - idx)[None, :])
    cdec = np.broadcast_to(np.exp(lg * f32(chunk))[:, None, None], (RET_HEADS, RET_QK_DIM, RET_V_DIM))
    return tuple(jnp.asarray(np.ascontiguousarray(t, dtype=f32)) for t in (dmat, cross, tail, cdec))


def _rotary_angles(pos):
    half = RET_QK_DIM // 2
    freqs = (np.float32(ROPE_BASE) ** (-np.arange(half, dtype=np.float32) / np.float32(half))).astype(np.float32)
    return np.asarray(pos, np.float32)[:, None] * freqs[None, :]


def _rotary_tables(pos):
    ang = _rotary_angles(pos)
    cos, sin = np.cos(ang), np.sin(ang)
    cos_t = np.tile(np.concatenate([cos, cos], axis=-1), (1, RET_HEADS))
    sin_t = np.tile(np.concatenate([-sin, sin], axis=-1), (1, RET_HEADS))
    return jnp.asarray(cos_t, F32), jnp.asarray(sin_t, F32)


def _full(shape):
    return pl.BlockSpec(shape, lambda *_: (0,) * len(shape))


def _mix_prompt(x, mod, b0, b, norm1, w_in16, w_pool, pool_scale, w_out16, norm2, wr_t, bias_t,
                block_l=512, chunk=256, seqs=2):
    _, l, d = x.shape
    nl = l // block_l
    s0 = b0 // seqs
    cos_t, sin_t = _rotary_tables(np.arange(l))
    dmat, cross, tail, cdec = _decay_tables(chunk)
    kernel = functools.partial(_mix_prompt_kernel, block_l=block_l, chunk=chunk, seqs=seqs)
    tok = lambda bi, li: (bi, li, 0)
    per_seq = lambda bi, li: (bi, 0, 0)
    x1, h2, experts, gate_w, counts, ret, pool = pl.pallas_call(
        kernel,
        grid=(b // seqs, nl),
        in_specs=[pl.BlockSpec((seqs, block_l, d), lambda bi, li: (s0 + bi, li, 0)),
                  pl.BlockSpec((seqs, 6, d), lambda bi, li: (s0 + bi, 0, 0)),
                  _full((1, d)),
                  _full((d, IN_WIDTH)),
                  pl.BlockSpec((block_l, QK_WIDTH), lambda bi, li: (li, 0)),
                  pl.BlockSpec((block_l, QK_WIDTH), lambda bi, li: (li, 0)),
                  _full(dmat.shape), _full(cross.shape), _full(tail.shape), _full(cdec.shape),
                  _full(w_pool.shape), _full((1, POOL_WIDTH)), _full((d, d)), _full((1, d)),
                  _full(wr_t.shape), _full(bias_t.shape)],
        out_specs=[pl.BlockSpec((seqs, block_l, d), tok),
                   pl.BlockSpec((seqs, block_l, HALF), tok),
                   pl.BlockSpec((seqs, TOP_K, block_l), lambda bi, li: (bi, 0, li)),
                   pl.BlockSpec((seqs, block_l, LANES), tok),
                   _full((N_EXPERTS, LANES)),
                   pl.BlockSpec((seqs, RET_HEADS, RET_QK_DIM, RET_V_DIM), lambda bi, li: (bi, 0, 0, 0)),
                   pl.BlockSpec((seqs, POOL_BUF, POOL_WIDTH), per_seq)],
        out_shape=[jax.ShapeDtypeStruct((b, l, d), F32),
                   jax.ShapeDtypeStruct((b, l, HALF), U32),
                   jax.ShapeDtypeStruct((b, TOP_K, l), I32),
                   jax.ShapeDtypeStruct((b, l, LANES), F32),
                   jax.ShapeDtypeStruct((N_EXPERTS, LANES), F32),
                   jax.ShapeDtypeStruct((b, RET_HEADS, RET_QK_DIM, RET_V_DIM), F32),
                   jax.ShapeDtypeStruct((b, POOL_BUF, POOL_WIDTH), F32)],
        scratch_shapes=[pltpu.VMEM((seqs, QK_WIDTH, RET_V_DIM), F32),
                        pltpu.VMEM((seqs, POOL_CARRY + block_l, POOL_WIDTH), F32),
                        pltpu.VMEM((seqs, POOL_CARRY + block_l, POOL_WIDTH), F32),
                        pltpu.VMEM((seqs, block_l, RET_WIDTH), F32)],
        compiler_params=pltpu.CompilerParams(dimension_semantics=("arbitrary", "arbitrary"),
                                             vmem_limit_bytes=VMEM_LIMIT),
        name="mix_prompt",
    )(x, mod, norm1, w_in16, cos_t, sin_t, dmat, cross, tail, cdec, w_pool, pool_scale,
      w_out16, norm2, wr_t, bias_t)
    experts = jnp.transpose(experts, (1, 0, 2)).reshape(TOP_K, b * l)
    return x1, h2.reshape(b * l, HALF), experts, gate_w.reshape(b * l, LANES), counts, ret, pool


def _mix_sample_front_kernel(x_ref, mod_ref, norm1_ref, w_in_ref, cos_ref, sin_ref,
                             qt_ref, kt_ref, v_ref, g_ref, u_ref):
    x = x_ref[...]
    h = _rms(x) * norm1_ref[...] * (1.0 + _mod(mod_ref, 1)) + _mod(mod_ref, 0)
    proj = _dot(h.astype(BF16), w_in_ref[...])
    half = RET_QK_DIM // 2
    cos_c = cos_ref[...]
    sin_c = sin_ref[...]

    def rot_t(t):
        parts = []
        for hd in range(RET_HEADS):
            t1 = t[hd * RET_QK_DIM:hd * RET_QK_DIM + half]
            t2 = t[hd * RET_QK_DIM + half:(hd + 1) * RET_QK_DIM]
            parts += [t1 * cos_c - t2 * sin_c, t1 * sin_c + t2 * cos_c]
        return jnp.concatenate(parts, axis=0)

    qt_ref[...] = rot_t(proj[:, 0:QK_WIDTH].T)
    kt_ref[...] = rot_t(proj[:, QK_WIDTH:2 * QK_WIDTH].T) * (RET_QK_DIM ** -0.5)
    v_ref[...] = proj[:, 2 * QK_WIDTH:2 * QK_WIDTH + RET_WIDTH]
    g_ref[...] = proj[:, 2 * QK_WIDTH + RET_WIDTH:2 * QK_WIDTH + 2 * RET_WIDTH]
    u_ref[...] = proj[:, 2 * QK_WIDTH + 2 * RET_WIDTH:]


def _ret_step_kernel(qt_ref, kt_ref, v_ref, s0_ref, o_ref, s1_ref, *, block_b, decays):
    i = pl.program_id(0)
    lane = lax.broadcasted_iota(I32, qt_ref.shape, 1)
    for j in range(block_b):
        bi = i * block_b + j
        here = lane == bi
        q_col = jnp.sum(jnp.where(here, qt_ref[...], 0.0), axis=1, keepdims=True)
        k_col = jnp.sum(jnp.where(here, kt_ref[...], 0.0), axis=1, keepdims=True)
        v_row = v_ref[pl.ds(bi, 1), :]
        outs = []
        for hd in range(RET_HEADS):
            hrows = slice(hd * RET_QK_DIM, (hd + 1) * RET_QK_DIM)
            s1 = decays[hd] * s0_ref[j, hd] + k_col[hrows] * v_row[:, hd * RET_V_DIM:(hd + 1) * RET_V_DIM]
            s1_ref[j, hd] = s1
            outs.append(jnp.sum(q_col[hrows] * s1, axis=0, keepdims=True))
        o_ref[pl.ds(bi, 1), :] = jnp.concatenate(outs, axis=-1)


def _mix_sample_back_kernel(x_ref, mod_ref, o_ref, g_ref, u_ref, buf_ref, w_pool_ref, pool_scale_ref,
                            w_out_ref, norm2_ref, wr_t_ref, bias_t_ref,
                            x1_ref, h2_ref, experts_ref, gatew_ref, cnt_ref, pool_ref):
    cnt_ref[...] = jnp.zeros_like(cnt_ref)
    o_gated = _group_norm_gate(o_ref[...], g_ref[...])
    u = u_ref[...]
    pooled = []
    for gi, w in enumerate(POOL_WINDOWS):
        lanes = slice(gi * POOL_GROUP_DIM, (gi + 1) * POOL_GROUP_DIM)
        acc = u[:, lanes]
        for j in range(1, w):
            acc = acc + buf_ref[:, POOL_BUF - j, lanes]
        pooled.append(acc / float(w) - u[:, lanes])
    p = _pool_project(pooled, w_pool_ref, pool_scale_ref)
    pool_ref[:, 0:POOL_BUF - 1, :] = buf_ref[:, 1:POOL_BUF, :]
    pool_ref[:, POOL_BUF - 1, :] = u
    x1 = _out_residual(x_ref[...], o_gated, p, mod_ref, 0, w_out_ref)
    x1_ref[...] = x1
    _norm_route(x1, mod_ref, 0, norm2_ref, wr_t_ref, bias_t_ref, h2_ref, experts_ref, gatew_ref, cnt_ref)


def _mix_sample(x, mod, state_ret, state_pool, start, norm1, w_in16, w_pool,
                pool_scale, w_out16, norm2, wr_t, bias_t, block_b=32):
    n, d = x.shape
    half = RET_QK_DIM // 2
    ang = _rotary_angles([start])
    cos_c = jnp.asarray(np.broadcast_to(np.cos(ang).T, (half, n)), F32)
    sin_c = jnp.asarray(np.broadcast_to(np.sin(ang).T, (half, n)), F32)
    params = pltpu.CompilerParams(vmem_limit_bytes=VMEM_LIMIT)
    qt, kt, v, g, u = pl.pallas_call(
        _mix_sample_front_kernel,
        out_shape=[jax.ShapeDtypeStruct((QK_WIDTH, n), F32), jax.ShapeDtypeStruct((QK_WIDTH, n), F32),
                   jax.ShapeDtypeStruct((n, RET_WIDTH), F32), jax.ShapeDtypeStruct((n, RET_WIDTH), F32),
                   jax.ShapeDtypeStruct((n, POOL_WIDTH), F32)],
        compiler_params=params,
        name="mix_sample_front",
    )(x, mod, norm1, w_in16, cos_c, sin_c)

    lg = np.log(1.0 - 2.0 ** (-5.0 - np.arange(RET_HEADS, dtype=np.float32)), dtype=np.float32)
    decays = tuple(float(np.exp(lg[h])) for h in range(RET_HEADS))
    state_block = (block_b, RET_HEADS, RET_QK_DIM, RET_V_DIM)
    o, s1 = pl.pallas_call(
        functools.partial(_ret_step_kernel, block_b=block_b, decays=decays),
        grid=(n // block_b,),
        in_specs=[_full((QK_WIDTH, n)), _full((QK_WIDTH, n)), _full((n, RET_WIDTH)),
                  pl.BlockSpec(state_block, lambda i: (i, 0, 0, 0))],
        out_specs=[_full((n, RET_WIDTH)), pl.BlockSpec(state_block, lambda i: (i, 0, 0, 0))],
        out_shape=[jax.ShapeDtypeStruct((n, RET_WIDTH), F32),
                   jax.ShapeDtypeStruct(state_ret.shape, F32)],
        compiler_params=pltpu.CompilerParams(dimension_semantics=("arbitrary",),
                                             vmem_limit_bytes=VMEM_LIMIT),
        name="ret_step",
    )(qt, kt, v, state_ret)

    x1, h2, experts, gate_w, counts, pool = pl.pallas_call(
        _mix_sample_back_kernel,
        out_shape=[jax.ShapeDtypeStruct((n, d), F32),
                   jax.ShapeDtypeStruct((n, HALF), U32),
                   jax.ShapeDtypeStruct((TOP_K, n), I32),
                   jax.ShapeDtypeStruct((n, LANES), F32),
                   jax.ShapeDtypeStruct((N_EXPERTS, LANES), F32),
                   jax.ShapeDtypeStruct(state_pool.shape, F32)],
        compiler_params=params,
        name="mix_sample_back",
    )(x, mod, o, g, u, state_pool, w_pool, pool_scale, w_out16, norm2, wr_t, bias_t)
    return x1, h2, experts, gate_w, counts, s1, pool


def _plan_kernel(experts_ref, cnt_ref, pos_ref, meta_ref, carry_ref, off_ref, *, block_t):
    @pl.when(pl.program_id(0) == 0)
    def _():
        cnt = cnt_ref[...]
        n_tile = jnp.floor((cnt + (ROW_TILE - 1.0)) * (1.0 / ROW_TILE))
        upto = (lax.broadcasted_iota(I32, (N_EXPERTS, N_EXPERTS), 1)
                <= lax.broadcasted_iota(I32, (N_EXPERTS, N_EXPERTS), 0))
        tile_end = _dot(jnp.where(upto, 1.0, 0.0).astype(BF16), n_tile.astype(BF16))
        tile_start = tile_end - n_tile
        off_ref[...] = tile_start * ROW_TILE
        carry_ref[...] = jnp.zeros_like(carry_ref)
        lane = lax.broadcasted_iota(I32, cnt.shape, 1)
        meta_ref[...] = jnp.where(lane == 0, tile_start, jnp.where(lane == 1, n_tile, cnt)).astype(I32)

    e_blk = experts_ref[...]
    eidx = lax.broadcasted_iota(I32, (N_EXPERTS, block_t), 0)
    member = jnp.zeros((N_EXPERTS, block_t), F32)
    for s in range(TOP_K):
        member = member + jnp.where(eidx == e_blk[s:s + 1, :], 1.0, 0.0)
    before = (lax.broadcasted_iota(I32, (block_t, block_t), 0)
              < lax.broadcasted_iota(I32, (block_t, block_t), 1))
    rank = _dot(member.astype(BF16), jnp.where(before, 1.0, 0.0).astype(BF16))
    row = off_ref[:, 0:1] + carry_ref[:, 0:1] + rank
    carry_ref[...] += jnp.broadcast_to(jnp.sum(member, axis=1, keepdims=True), (N_EXPERTS, LANES))
    out = [jnp.sum(jnp.where(eidx == e_blk[s:s + 1, :], row, 0.0), axis=0, keepdims=True)
           for s in range(TOP_K)]
    pos_ref[...] = jnp.concatenate(out, axis=0).astype(I32)


def _plan(experts_all, counts, max_block=1024):
    n_tokens = experts_all.shape[1]
    block_t = max(k for k in range(LANES, max_block + 1, LANES) if n_tokens % k == 0)
    return pl.pallas_call(
        functools.partial(_plan_kernel, block_t=block_t),
        grid=(n_tokens // block_t,),
        in_specs=[pl.BlockSpec((TOP_K, block_t), lambda j: (0, j)), _full((N_EXPERTS, LANES))],
        out_specs=[pl.BlockSpec((TOP_K, block_t), lambda j: (0, j)), _full((N_EXPERTS, LANES))],
        out_shape=[jax.ShapeDtypeStruct((TOP_K, n_tokens), I32),
                   jax.ShapeDtypeStruct((N_EXPERTS, LANES), I32)],
        scratch_shapes=[pltpu.VMEM((N_EXPERTS, LANES), F32)] * 2,
        compiler_params=pltpu.CompilerParams(dimension_semantics=("arbitrary",),
                                             vmem_limit_bytes=VMEM_LIMIT),
        name="plan",
    )(experts_all, counts)


def _sc_workers():
    info = plsc.get_sparse_core_info()
    return info.num_cores, info.num_cores * info.num_subcores


def _sc_scatter_rows(sources, pos_t, n_out, after=()):
    w = sources[0].shape[1]
    s = pos_t.shape[0]
    n_cores, n_workers = _sc_workers()
    bounds = np.cumsum([0] + [src.shape[0] // SC_CHUNK for src in sources])
    n_chunks = int(bounds[-1])
    iters = -(-n_chunks // n_workers)
    mesh = plsc.VectorSubcoreMesh(core_axis_name="c", subcore_axis_name="s")

    @functools.partial(
        pl.kernel, mesh=mesh, out_type=jax.ShapeDtypeStruct((n_out, w), sources[0].dtype),
        scratch_types=[pltpu.VMEM((SC_CHUNK, w), sources[0].dtype), pltpu.VMEM((s, SC_CHUNK), I32),
                       pltpu.SemaphoreType.DMA],
        name="dispatch")
    def k(*refs):
        src_hbm, pos_hbm = refs[:len(sources)], refs[len(sources)]
        out_hbm, rows_v, idx_v, sem = refs[len(sources) + 1 + len(after):]
        wid = lax.axis_index("s") * n_cores + lax.axis_index("c")

        @pl.loop(0, iters)
        def _(it):
            c = it * n_workers + wid
            for src, lo, hi in zip(src_hbm, bounds[:-1], bounds[1:]):
                @pl.when((c >= int(lo)) & (c < int(hi)))
                def _():
                    base = pl.multiple_of((c - int(lo)) * SC_CHUNK, SC_CHUNK)
                    pltpu.sync_copy(src.at[pl.ds(base, SC_CHUNK)], rows_v)

            @pl.when(c < n_chunks)
            def _():
                base = pl.multiple_of(c * SC_CHUNK, SC_CHUNK)
                pltpu.sync_copy(pos_hbm.at[:, pl.ds(base, SC_CHUNK)], idx_v)
                copies = [pltpu.async_copy(rows_v, out_hbm.at[idx_v.at[j]], sem) for j in range(s)]
                for cp in copies:
                    cp.wait()

    return k(*sources, pos_t, *after)


def _sc_gather_rows(table, pos_t):
    _, w = table.shape
    s, t = pos_t.shape
    n_cores, n_workers = _sc_workers()
    n_chunks = t // SC_CHUNK
    iters = -(-n_chunks // n_workers)
    mesh = plsc.VectorSubcoreMesh(core_axis_name="c", subcore_axis_name="s")

    @functools.partial(
        pl.kernel, mesh=mesh, out_type=jax.ShapeDtypeStruct((s, t, w), table.dtype),
        scratch_types=[pltpu.VMEM((SC_CHUNK, w), table.dtype), pltpu.VMEM((s, SC_CHUNK), I32),
                       pltpu.SemaphoreType.DMA],
        name="combine")
    def k(table_hbm, pos_hbm, out_hbm, rows_v, idx_v, sem):
        wid = lax.axis_index("s") * n_cores + lax.axis_index("c")

        @pl.loop(0, iters)
        def _(it):
            c = it * n_workers + wid

            @pl.when(c < n_chunks)
            def _():
                base = pl.multiple_of(c * SC_CHUNK, SC_CHUNK)
                pltpu.sync_copy(pos_hbm.at[:, pl.ds(base, SC_CHUNK)], idx_v)
                for j in range(s):
                    pltpu.async_copy(table_hbm.at[idx_v.at[j]], rows_v, sem).wait()
                    pltpu.sync_copy(rows_v, out_hbm.at[j, pl.ds(base, SC_CHUNK)])

    return k(table, pos_t)


def _sc_pack_weights(w, rows_per_item, after=()):
    e, r, c = w.shape
    half = r // 2
    rb = rows_per_item
    per_expert = half // rb
    n_cores, n_workers = _sc_workers()
    per_worker = e * per_expert // n_workers
    assert per_worker * n_workers == e * per_expert and per_worker % 2 == 0 and c % (SC_LANES * SC_UNROLL) == 0
    mesh = plsc.VectorSubcoreMesh(core_axis_name="c", subcore_axis_name="s")

    @functools.partial(
        pl.kernel, mesh=mesh, out_type=jax.ShapeDtypeStruct((e * half, c), U32),
        scratch_types=[pltpu.VMEM((2, rb, c), F32), pltpu.VMEM((2, rb, c), F32), pltpu.VMEM((2, rb, c), U32),
                       pltpu.SemaphoreType.DMA((2,)), pltpu.SemaphoreType.DMA((2,))],
        compiler_params=pltpu.CompilerParams(needs_layout_passes=False),
        name="pack_weights")
    def k(w_hbm, *refs):
        out_hbm, lo_v, hi_v, out_v, in_sem, out_sem = refs[len(after):]
        wid = lax.axis_index("s") * n_cores + lax.axis_index("c")
        first = wid * per_worker

        def rows(item):
            ex = item // per_expert
            j = item - ex * per_expert
            return (pl.multiple_of(ex * r + j * rb, rb), pl.multiple_of(ex * r + half + j * rb, rb),
                    pl.multiple_of(ex * half + j * rb, rb))

        def loads(item, b):
            lo_row, hi_row, _ = rows(item)
            return (pltpu.make_async_copy(w_hbm.at[pl.ds(lo_row, rb)], lo_v.at[b], in_sem.at[b]),
                    pltpu.make_async_copy(w_hbm.at[pl.ds(hi_row, rb)], hi_v.at[b], in_sem.at[b]))

        def store(item, b):
            return pltpu.make_async_copy(out_v.at[b], out_hbm.at[pl.ds(rows(item)[2], rb)], out_sem.at[b])

        for cp in loads(first, 0):
            cp.start()

        @pl.loop(0, per_worker // 2)
        def _(pair):
            for b in range(2):
                item = first + pair * 2 + b
                for cp in loads(item, b):
                    cp.wait()

                @pl.when(item + 1 < first + per_worker)
                def _():
                    for cp in loads(item + 1, 1 - b):
                        cp.start()

                @pl.when(pair > 0)
                def _():
                    store(item - 2, b).wait()

                @pl.loop(0, rb)
                def _(i):
                    @pl.loop(0, c // (SC_LANES * SC_UNROLL))
                    def _(vb):
                        for u in range(SC_UNROLL):
                            sl = pl.ds(pl.multiple_of((vb * SC_UNROLL + u) * SC_LANES, SC_LANES), SC_LANES)
                            packed = plsc.pack(lo_v[b, i, sl], hi_v[b, i, sl], format=plsc.PackFormat.INTERLEAVED)
                            out_v[b, i, sl] = plsc.bitcast(packed, U32)

                store(item, b).start()

        for b in range(2):
            store(first + per_worker - 2 + b, b).wait()

    return k(w.reshape(e * r, c), *after).reshape(e, half, c)


def _experts_kernel(first_ref, ntile_ref, cnt_ref, xs_hbm, wg_ref, wu_ref, wd_ref, ys_hbm,
                    wg16_ref, wu16_ref, wd16_ref, x_buf, y_buf, in_sem, out_sem):
    ahead = STREAM_DEPTH - MAX_WIDTH
    e = pl.program_id(0)
    n_used = first_ref[N_EXPERTS - 1] + ntile_ref[N_EXPERTS - 1]
    first, n_mine, count = first_ref[e], ntile_ref[e], cnt_ref[e]

    def tile_rows(g):
        return pl.ds(pl.multiple_of(g * ROW_TILE, ROW_TILE), ROW_TILE)

    def load(g):
        slot = lax.rem(g, STREAM_DEPTH)
        return pltpu.make_async_copy(xs_hbm.at[tile_rows(g)], x_buf.at[slot], in_sem.at[slot])

    def store(g):
        slot = lax.rem(g, STREAM_DEPTH)
        return pltpu.make_async_copy(y_buf.at[slot], ys_hbm.at[tile_rows(g)], out_sem.at[slot])

    @pl.when(e == 0)
    def _():
        for g0 in range(ahead):
            @pl.when(g0 < n_used)
            def _():
                load(g0).start()

    for packed_ref, w16_ref in ((wg_ref, wg16_ref), (wu_ref, wu16_ref), (wd_ref, wd16_ref)):
        rows = packed_ref.shape[1]
        lo, hi = _unpack_rows(packed_ref[0])
        w16_ref[0:rows, :] = lo.astype(BF16)
        w16_ref[rows:, :] = hi.astype(BF16)

    def run(j, width):
        tiles = [first + j + t for t in range(width)]
        for g in tiles:
            load(g).wait()

            @pl.when(g + ahead < n_used)
            def _():
                load(g + ahead).start()

            @pl.when(g >= STREAM_DEPTH)
            def _():
                store(g - STREAM_DEPTH).wait()

        words = jnp.concatenate([x_buf[lax.rem(g, STREAM_DEPTH)] for g in tiles], axis=0)
        row = lax.broadcasted_iota(I32, words.shape, 0)
        words = jnp.where(row < count - j * ROW_TILE, words, jnp.uint32(0))
        lo, hi = _unpack_rows(words)
        lo, hi = lo.astype(BF16), hi.astype(BF16)
        hg = _dot(lo, wg16_ref[0:HALF, :]) + _dot(hi, wg16_ref[HALF:, :])
        hu = _dot(lo, wu16_ref[0:HALF, :]) + _dot(hi, wu16_ref[HALF:, :])
        a = (_silu(hg) * hu).astype(BF16)
        y = _pack_rows(_dot(a, wd16_ref[...]))
        for t, g in enumerate(tiles):
            y_buf[lax.rem(g, STREAM_DEPTH)] = y[t * ROW_TILE:(t + 1) * ROW_TILE]
            store(g).start()

    def widest(p, carry):
        run(MAX_WIDTH * p, MAX_WIDTH)
        return carry

    lax.fori_loop(0, n_mine // MAX_WIDTH, widest, 0)
    done = n_mine - lax.rem(n_mine, MAX_WIDTH)
    width = MAX_WIDTH // 2
    while width:
        has = lax.rem(n_mine // width, 2) == 1

        @pl.when(has)
        def _(width=width, done=done):
            run(done, width)

        done = done + jnp.where(has, width, 0)
        width //= 2

    @pl.when(e == N_EXPERTS - 1)
    def _():
        for back in range(STREAM_DEPTH, 0, -1):
            @pl.when(n_used >= back)
            def _():
                store(n_used - back).wait()


def _experts(xs, first_tile, n_tile, count, w_eg, w_eu, w_ed):
    d = D_MODEL
    by_expert = lambda e, *_: (e, 0, 0)
    grid_spec = pltpu.PrefetchScalarGridSpec(
        num_scalar_prefetch=3,
        grid=(N_EXPERTS,),
        in_specs=[pl.BlockSpec(memory_space=pl.ANY),
                  pl.BlockSpec((1, d // 2, EXPERT_DIM), by_expert),
                  pl.BlockSpec((1, d // 2, EXPERT_DIM), by_expert),
                  pl.BlockSpec((1, EXPERT_DIM // 2, d), by_expert)],
        out_specs=pl.BlockSpec(memory_space=pl.ANY),
        scratch_shapes=[pltpu.VMEM((d, EXPERT_DIM), BF16), pltpu.VMEM((d, EXPERT_DIM), BF16),
                        pltpu.VMEM((EXPERT_DIM, d), BF16),
                        pltpu.VMEM((STREAM_DEPTH, ROW_TILE, HALF), U32),
                        pltpu.VMEM((STREAM_DEPTH, ROW_TILE, HALF), U32),
                        pltpu.SemaphoreType.DMA((STREAM_DEPTH,)), pltpu.SemaphoreType.DMA((STREAM_DEPTH,))])
    return pl.pallas_call(
        _experts_kernel,
        grid_spec=grid_spec,
        out_shape=jax.ShapeDtypeStruct(xs.shape, U32),
        compiler_params=pltpu.CompilerParams(dimension_semantics=("arbitrary",),
                                             vmem_limit_bytes=VMEM_LIMIT),
        name="experts",
    )(first_tile, n_tile, count, xs, w_eg, w_eu, w_ed)


def _final_kernel(z_ref, gatew_ref, h2_ref, x1_ref, mod_ref, normf_ref, wsg_ref, wsu_ref, wsd_ref, *rest):
    y_ref, wsg16_ref, wsu16_ref, wsd16_ref = rest[-4:]

    @pl.when(pl.program_id(0) == 0)
    def _():
        wsg16_ref[...] = wsg_ref[...].astype(BF16)
        wsu16_ref[...] = wsu_ref[...].astype(BF16)
        wsd16_ref[...] = wsd_ref[...].astype(BF16)

    lo, hi = _unpack_rows(h2_ref[...])
    h = jnp.concatenate([lo, hi], axis=-1).astype(BF16)
    a = _silu(_dot(h, wsg16_ref[...])) * _dot(h, wsu16_ref[...])
    acc = _dot(a.astype(BF16), wsd16_ref[...])
    for s in range(TOP_K):
        lo, hi = _unpack_rows(z_ref[s])
        acc = acc + gatew_ref[:, s:s + 1] * jnp.concatenate([lo, hi], axis=-1)
    x2 = x1_ref[...] + _mod(mod_ref, 5) * acc
    y_ref[...] = _rms(x2) * normf_ref[...]


def _final(z, gate_w, h2, x1, mod, norm_f, w_sg, w_su, w_sd, block_t, first_block, per_seq,
           seq0=0, out_rows=None, y_prev=None):
    t, d = x1.shape
    out_rows = t if out_rows is None else out_rows
    out_first = seq0 * per_seq
    tok = lambda i: (i, 0)
    if per_seq:
        mod_spec = pl.BlockSpec((1, 6, d), lambda i: (seq0 + i // per_seq, 0, 0))
    else:
        mod_spec = pl.BlockSpec((block_t, 6 * d), tok)
    operands = [z, gate_w, h2, x1, mod, norm_f, w_sg, w_su, w_sd]
    in_specs = [pl.BlockSpec((TOP_K, block_t, HALF), lambda i: (0, first_block + i, 0)),
                pl.BlockSpec((block_t, LANES), tok),
                pl.BlockSpec((block_t, HALF), tok),
                pl.BlockSpec((block_t, d), tok),
                mod_spec,
                _full((1, d)),
                _full((d, EXPERT_DIM)), _full((d, EXPERT_DIM)), _full((EXPERT_DIM, d))]
    aliases = {}
    if y_prev is not None:
        aliases = {len(operands): 0}
        operands.append(y_prev)
        in_specs.append(pl.BlockSpec(memory_space=pl.ANY))
    return pl.pallas_call(
        _final_kernel,
        grid=(t // block_t,),
        in_specs=in_specs,
        out_specs=pl.BlockSpec((block_t, d), lambda i: (out_first + i, 0)),
        out_shape=jax.ShapeDtypeStruct((out_rows, d), F32),
        scratch_shapes=[pltpu.VMEM((d, EXPERT_DIM), BF16), pltpu.VMEM((d, EXPERT_DIM), BF16),
                        pltpu.VMEM((EXPERT_DIM, d), BF16)],
        input_output_aliases=aliases,
        compiler_params=pltpu.CompilerParams(dimension_semantics=("arbitrary",),
                                             vmem_limit_bytes=VMEM_LIMIT),
        name="final",
    )(*operands)


def kernel(x_prompt, x_sample, c_prompt, c_sample, state_ret, state_pool, norm1, norm2, norm_f,
           w_ada, b_ada, w_in, w_out, w_pool, pool_scale, w_router, router_bias, w_exp_gate,
           w_exp_up, w_exp_down, w_sh_gate, w_sh_up, w_sh_down):
    b, l, d = x_prompt.shape
    n = x_sample.shape[0]

    mod_p, mod_s = _ada(c_prompt, c_sample, w_ada[0], b_ada[0])
    mod_p = mod_p.reshape(b, 6, d)

    w_in16 = w_in[0].astype(BF16)
    w_out16 = w_out[0].astype(BF16)
    wr_t = w_router[0].T
    bias_t = jnp.broadcast_to(router_bias[0][:, None], (N_EXPERTS, LANES))
    n1, n2, nf = norm1[0].reshape(1, d), norm2[0].reshape(1, d), norm_f.reshape(1, d)
    ps = pool_scale[0].reshape(1, POOL_WIDTH)
    shared = (w_sh_gate[0], w_sh_up[0], w_sh_down[0])

    def routed(sources, experts, counts, repacked):
        n_tiles = experts.shape[1] * TOP_K // ROW_TILE + N_EXPERTS
        pos_t, meta = _plan(experts, counts)
        xs = _sc_scatter_rows(sources, pos_t, n_tiles * ROW_TILE, after=repacked)
        ys = _experts(xs, meta[:, 0], meta[:, 1], meta[:, 2], *expert_w)
        return _sc_gather_rows(ys, pos_t)

    mix_args = (n1, w_in16, w_pool[0], ps, w_out16, n2, wr_t, bias_t)
    x1_s, h2_s, experts_s, gatew_s, counts_s, ret_s, pool_s = _mix_sample(
        x_sample.reshape(n, d), mod_s, state_ret[0], state_pool[0], float(PAST_LEN), *mix_args)

    expert_w = (_sc_pack_weights(w_exp_gate[0], 64, after=(x1_s,)),
                _sc_pack_weights(w_exp_up[0], 64, after=(x1_s,)),
                _sc_pack_weights(w_exp_down[0], 16, after=(x1_s,)))

    ba = b // 2
    bb = b - ba
    x1_a, h2_a, experts_a, gatew_a, counts_a, ret_a, pool_a = _mix_prompt(x_prompt, mod_p, 0, ba, *mix_args)
    z_a = routed((h2_a,), experts_a, counts_a, expert_w)
    x1_b, h2_b, experts_b, gatew_b, counts_b, ret_b, pool_b = _mix_prompt(x_prompt, mod_p, ba, bb, *mix_args)
    z_b = routed((h2_b, h2_s), jnp.concatenate([experts_b, experts_s], axis=1), counts_b + counts_s,
                 expert_w[:2])

    block_t = 512
    per_seq = l // block_t
    y_s = _final(z_b, gatew_s, h2_s, x1_s, mod_s, nf, *shared,
                 block_t=n, first_block=bb * l // n, per_seq=0)
    y_p = _final(z_b, gatew_b, h2_b, x1_b.reshape(bb * l, d), mod_p, nf, *shared,
                 block_t=block_t, first_block=0, per_seq=per_seq, seq0=ba, out_rows=b * l)
    y_p = _final(z_a, gatew_a, h2_a, x1_a.reshape(ba * l, d), mod_p, nf, *shared,
                 block_t=block_t, first_block=0, per_seq=per_seq, seq0=0, out_rows=b * l, y_prev=y_p)

    ret_p = jnp.concatenate([ret_a, ret_b], axis=0)
    pool_p = jnp.concatenate([pool_a, pool_b], axis=0)
    return (y_p.reshape(b, l, d), y_s.reshape(n, 1, d), ret_p[None], pool_p[None],
            ret_s[None], pool_s[None])
```

```python
import functools

import jax
import jax.numpy as jnp
import numpy as np
from jax import lax
from jax.experimental import pallas as pl
from jax.experimental.pallas import tpu as pltpu
from jax.experimental.pallas import tpu_sc as plsc

D_MODEL = 1024
RET_HEADS = 4
RET_QK_DIM = 64
RET_V_DIM = 128
RET_WIDTH = RET_HEADS * RET_V_DIM
QK_WIDTH = RET_HEADS * RET_QK_DIM
ROPE_BASE = 10000.0
POOL_WINDOWS = (2, 4, 8, 16)
POOL_WIDTH = 512
POOL_GROUP_DIM = 128
POOL_BUF = 15
IN_WIDTH = 2 * QK_WIDTH + 2 * RET_WIDTH + POOL_WIDTH
N_EXPERTS = 64
TOP_K = 8
N_EXPERT_GROUPS = 8
GROUP_SIZE = N_EXPERTS // N_EXPERT_GROUPS
TOP_GROUPS = 4
EXPERT_DIM = 256
ROUTE_SCALE = 2.5
EPS = 1e-6
PAST_LEN = 16384

LANES = 128
SUBLANES = 8
POOL_CARRY = 24
VMEM_LIMIT = 56 * 1024 * 1024
HALF = D_MODEL // 2
ROW_TILE = 256
MAX_WIDTH = 4
SC_CHUNK = 128
SC_LANES = 16
SC_UNROLL = 16
STREAM_DEPTH = 16

BF16 = jnp.bfloat16
F32 = jnp.float32
U32 = jnp.uint32
I32 = jnp.int32


def _silu(x):
    return x * jax.nn.sigmoid(x)


def _dot(a, b):
    return jnp.dot(a, b, preferred_element_type=F32)


def _rms(x):
    return x * lax.rsqrt(jnp.mean(x * x, axis=-1, keepdims=True) + EPS)


def _mod(mod_ref, i, seq=0):
    if len(mod_ref.shape) == 3:
        return mod_ref[seq, i:i + 1, :]
    return mod_ref[:, i * D_MODEL:(i + 1) * D_MODEL]


def _split_bf16(x):
    hi = x.astype(BF16)
    lo = (x - hi.astype(F32)).astype(BF16)
    return hi, lo


def _pack_rows(x):
    lo = lax.bitcast_convert_type(x[:, :HALF].astype(BF16).astype(F32), U32)
    hi = lax.bitcast_convert_type(x[:, HALF:].astype(BF16).astype(F32), U32)
    return (hi & jnp.uint32(0xFFFF0000)) | (lo >> jnp.uint32(16))


def _unpack_rows(w):
    lo = lax.bitcast_convert_type(w << jnp.uint32(16), F32)
    hi = lax.bitcast_convert_type(w & jnp.uint32(0xFFFF0000), F32)
    return lo, hi


def _first_max_onehot(work, idx, n):
    m = jnp.max(work, axis=0, keepdims=True)
    first = jnp.min(jnp.where(work == m, idx, float(n)), axis=0, keepdims=True)
    return idx == first


def _route(h2, wr_t_ref, bias_t_ref):
    n = h2.shape[0]
    h_hi, h_lo = _split_bf16(h2)
    w_hi, w_lo = _split_bf16(wr_t_ref[...])
    nt = (((1,), (1,)), ((), ()))
    logits = (lax.dot_general(w_hi, h_hi, nt, preferred_element_type=F32)
              + lax.dot_general(w_hi, h_lo, nt, preferred_element_type=F32)
              + lax.dot_general(w_lo, h_hi, nt, preferred_element_type=F32))
    scores = jax.nn.sigmoid(logits)
    biased = scores + bias_t_ref[:, 0:1]
    b3 = biased.reshape(N_EXPERT_GROUPS, GROUP_SIZE, n)
    i3 = lax.broadcasted_iota(I32, b3.shape, 1).astype(F32)
    m1 = jnp.max(b3, axis=1, keepdims=True)
    first = jnp.min(jnp.where(b3 == m1, i3, float(GROUP_SIZE)), axis=1, keepdims=True)
    m2 = jnp.max(jnp.where(i3 == first, -jnp.inf, b3), axis=1, keepdims=True)
    gscore = (m1 + m2).reshape(N_EXPERT_GROUPS, n)
    gidx = lax.broadcasted_iota(I32, gscore.shape, 0).astype(F32)
    gsel = jnp.zeros(gscore.shape, F32)
    work = gscore
    for _ in range(TOP_GROUPS):
        hit = _first_max_onehot(work, gidx, N_EXPERT_GROUPS)
        gsel = jnp.where(hit, 1.0, gsel)
        work = jnp.where(hit, -jnp.inf, work)
    gsel3 = jnp.broadcast_to(gsel.reshape(N_EXPERT_GROUPS, 1, n), b3.shape)
    work = jnp.where(gsel3 > 0.0, b3, -jnp.inf).reshape(N_EXPERTS, n)
    eidx = lax.broadcasted_iota(I32, work.shape, 0).astype(F32)
    sel = jnp.zeros(work.shape, F32)
    for _ in range(TOP_K):
        hit = _first_max_onehot(work, eidx, N_EXPERTS)
        sel = jnp.where(hit, 1.0, sel)
        work = jnp.where(hit, -jnp.inf, work)
    picked = jnp.where(sel > 0.0, scores, 0.0)
    gates = picked / jnp.sum(picked, axis=0, keepdims=True) * ROUTE_SCALE
    below = (lax.broadcasted_iota(I32, (N_EXPERTS, N_EXPERTS), 1)
             < lax.broadcasted_iota(I32, (N_EXPERTS, N_EXPERTS), 0))
    slot = _dot(jnp.where(below, 1.0, 0.0).astype(BF16), sel.astype(BF16))
    e_rows, w_rows = [], []
    for s in range(TOP_K):
        here = jnp.where(slot == float(s), sel, 0.0)
        e_rows.append(jnp.sum(here * eidx, axis=0, keepdims=True))
        w_rows.append(jnp.sum(here * gates, axis=0, keepdims=True))
    experts = jnp.concatenate(e_rows, axis=0).astype(I32)
    w_t = jnp.concatenate(w_rows + [jnp.zeros((LANES - TOP_K, n), F32)], axis=0)
    counts = jnp.broadcast_to(jnp.sum(sel, axis=1, keepdims=True), (N_EXPERTS, LANES))
    return experts, w_t.T, counts


def _group_norm_gate(o, g):
    parts = []
    for h in range(RET_HEADS):
        oh = o[:, h * RET_V_DIM:(h + 1) * RET_V_DIM]
        mu = jnp.mean(oh, axis=-1, keepdims=True)
        ctr = oh - mu
        var = jnp.mean(ctr * ctr, axis=-1, keepdims=True)
        parts.append(ctr * lax.rsqrt(var + EPS))
    return _silu(g) * jnp.concatenate(parts, axis=-1)


def _pool_project(pooled, w_pool_ref, pool_scale_ref):
    parts = [_dot(p.astype(BF16), w_pool_ref[gi].astype(BF16)) for gi, p in enumerate(pooled)]
    return jnp.concatenate(parts, axis=-1) * pool_scale_ref[...]


def _out_residual(x, o_gated, p, mod_ref, seq, w_out_ref):
    mix = jnp.concatenate([o_gated, p], axis=-1).astype(BF16)
    return x + _mod(mod_ref, 2, seq) * _dot(mix, w_out_ref[...])


def _norm_route(x1, mod_ref, seq, norm2_ref, wr_t_ref, bias_t_ref, h2_ref, experts_ref, gatew_ref, cnt_ref):
    h2 = _rms(x1) * norm2_ref[...] * (1.0 + _mod(mod_ref, 4, seq)) + _mod(mod_ref, 3, seq)
    h2_ref[...] = _pack_rows(h2)
    experts, gate_w, counts = _route(h2, wr_t_ref, bias_t_ref)
    experts_ref[...] = experts
    gatew_ref[...] = gate_w
    cnt_ref[...] += counts


def _ada_kernel(cp_ref, cs_ref, w_ref, b_ref, op_ref, os_ref):
    w16 = w_ref[...].astype(BF16)
    for c_ref, o_ref in ((cp_ref, op_ref), (cs_ref, os_ref)):
        o_ref[...] = _dot(_silu(c_ref[...]).astype(BF16), w16) + b_ref[...]


def _ada(c_prompt, c_sample, w_ada, b_ada, block_n=1536):
    d, width = w_ada.shape
    rows = lambda c: pl.BlockSpec((c.shape[0], d), lambda j: (0, 0))
    cols = lambda c: pl.BlockSpec((c.shape[0], block_n), lambda j: (0, j))
    return pl.pallas_call(
        _ada_kernel,
        grid=(width // block_n,),
        in_specs=[rows(c_prompt), rows(c_sample),
                  pl.BlockSpec((d, block_n), lambda j: (0, j)),
                  pl.BlockSpec((1, block_n), lambda j: (0, j))],
        out_specs=[cols(c_prompt), cols(c_sample)],
        out_shape=[jax.ShapeDtypeStruct((c.shape[0], width), F32) for c in (c_prompt, c_sample)],
        compiler_params=pltpu.CompilerParams(vmem_limit_bytes=VMEM_LIMIT),
        name="ada",
    )(c_prompt, c_sample, w_ada, b_ada.reshape(1, width))


def _mix_prompt_kernel(x_ref, mod_ref, norm1_ref, w_in_ref, cos_ref, sin_ref, dmat_ref, cross_ref,
                       tail_ref, cdec_ref, w_pool_ref, pool_scale_ref, w_out_ref, norm2_ref,
                       wr_t_ref, bias_t_ref,
                       x1_ref, h2_ref, experts_ref, gatew_ref, cnt_ref, ret_ref, pool_ref,
                       state_ref, ext_ref, win_ref, o_ref, *, block_l, chunk, seqs):
    li = pl.program_id(1)

    @pl.when((pl.program_id(0) == 0) & (li == 0))
    def _():
        cnt_ref[...] = jnp.zeros_like(cnt_ref)

    @pl.when(li == 0)
    def _():
        state_ref[...] = jnp.zeros_like(state_ref)
        ext_ref[:, 0:POOL_CARRY, :] = jnp.zeros((seqs, POOL_CARRY, POOL_WIDTH), F32)
        win_ref[:, 0:SUBLANES, :] = jnp.zeros((seqs, SUBLANES, POOL_WIDTH), F32)

    for seq in range(seqs):
        _mix_prompt_seq(seq, li, x_ref, mod_ref, norm1_ref, w_in_ref, cos_ref, sin_ref, dmat_ref, cross_ref,
                        tail_ref, cdec_ref, w_pool_ref, pool_scale_ref, w_out_ref, norm2_ref,
                        wr_t_ref, bias_t_ref, x1_ref, h2_ref, experts_ref, gatew_ref, cnt_ref,
                        state_ref.at[seq], ext_ref.at[seq], win_ref.at[seq], o_ref.at[seq],
                        block_l=block_l, chunk=chunk)

    @pl.when(li == pl.num_programs(1) - 1)
    def _():
        ret_ref[...] = state_ref[...].reshape(ret_ref.shape)
        pool_ref[...] = ext_ref[:, POOL_CARRY - POOL_BUF:POOL_CARRY, :]


def _window_sums(ext_ref, win_ref, block_l):
    g = POOL_GROUP_DIM
    top = POOL_CARRY + block_l
    new = slice(POOL_CARRY - SUBLANES, None)
    s2 = ext_ref[SUBLANES:top, :] + ext_ref[SUBLANES - 1:top - 1, :]
    win_ref[SUBLANES:top, g:] = s2[:, g:]
    s4 = s2[:, g:] + win_ref[SUBLANES - 2:top - 2, g:]
    win_ref[SUBLANES:top, 2 * g:] = s4[:, g:]
    s8 = s4[:, g:] + win_ref[SUBLANES - 4:top - 4, 2 * g:]
    win_ref[SUBLANES:top, 3 * g:] = s8[:, g:]
    s16 = s8[:, g:] + win_ref[0:top - SUBLANES, 3 * g:]
    return [s2[new, 0:g], s4[new, 0:g], s8[new, 0:g], s16[new, :]]


def _mix_prompt_seq(seq, li, x_ref, mod_ref, norm1_ref, w_in_ref, cos_ref, sin_ref, dmat_ref, cross_ref,
                    tail_ref, cdec_ref, w_pool_ref, pool_scale_ref, w_out_ref, norm2_ref,
                    wr_t_ref, bias_t_ref, x1_ref, h2_ref, experts_ref, gatew_ref, cnt_ref,
                    state_ref, ext_ref, win_ref, o_ref, *, block_l, chunk):
    x = x_ref[seq]
    h16 = (_rms(x) * norm1_ref[...] * (1.0 + _mod(mod_ref, 1, seq)) + _mod(mod_ref, 0, seq)).astype(BF16)
    proj = _dot(h16, w_in_ref[...])
    q = proj[:, 0:QK_WIDTH]
    k = proj[:, QK_WIDTH:2 * QK_WIDTH]
    v = proj[:, 2 * QK_WIDTH:2 * QK_WIDTH + RET_WIDTH]
    g = proj[:, 2 * QK_WIDTH + RET_WIDTH:2 * QK_WIDTH + 2 * RET_WIDTH]
    u = proj[:, 2 * QK_WIDTH + 2 * RET_WIDTH:]

    lane = lax.broadcasted_iota(I32, q.shape, 1)
    first_half = (lane % RET_QK_DIM) < (RET_QK_DIM // 2)
    cos_t = cos_ref[...]
    sin_t = sin_ref[...]

    def rot(t):
        partner = jnp.where(first_half, pltpu.roll(t, QK_WIDTH - RET_QK_DIM // 2, axis=1),
                            pltpu.roll(t, RET_QK_DIM // 2, axis=1))
        return t * cos_t + partner * sin_t

    q = rot(q)
    k = rot(k) * (RET_QK_DIM ** -0.5)
    k_t = k.T
    v16 = v.astype(BF16)
    head_of_lane = lax.broadcasted_iota(I32, (chunk, QK_WIDTH), 1) // RET_QK_DIM

    for c in range(block_l // chunk):
        rows = slice(c * chunk, (c + 1) * chunk)
        q_c = q[rows]
        kt_c = k_t[:, rows]
        kt16 = kt_c.astype(BF16)
        state16 = state_ref[...].astype(BF16)
        for hd in range(RET_HEADS):
            in_head = head_of_lane == hd
            q_h = jnp.where(in_head, q_c, 0.0).astype(BF16)
            v_h = v16[rows, hd * RET_V_DIM:(hd + 1) * RET_V_DIM]
            scores = _dot(q_h, kt16) * dmat_ref[hd]
            inner = _dot(scores.astype(BF16), v_h)
            cross = _dot(q_h, state16) * cross_ref[hd]
            o_ref[rows, hd * RET_V_DIM:(hd + 1) * RET_V_DIM] = inner + cross
            hrows = slice(hd * RET_QK_DIM, (hd + 1) * RET_QK_DIM)
            k_dec = (kt_c[hrows] * tail_ref[hd:hd + 1, :]).astype(BF16)
            state_ref[hrows, :] = state_ref[hrows, :] * cdec_ref[hd] + _dot(k_dec, v_h)

    o_gated = _group_norm_gate(o_ref[...], g)

    ext_ref[POOL_CARRY:POOL_CARRY + block_l, :] = u
    pos = (li * block_l + lax.broadcasted_iota(I32, (block_l, 1), 0)).astype(F32)
    pooled = []
    for gi, (w, acc) in enumerate(zip(POOL_WINDOWS, _window_sums(ext_ref, win_ref, block_l))):
        cnt = jnp.minimum(pos + 1.0, float(w))
        pooled.append(acc / cnt - u[:, gi * POOL_GROUP_DIM:(gi + 1) * POOL_GROUP_DIM])
    p = _pool_project(pooled, w_pool_ref, pool_scale_ref)
    ext_ref[0:POOL_CARRY, :] = ext_ref[block_l:block_l + POOL_CARRY, :]

    x1 = _out_residual(x, o_gated, p, mod_ref, seq, w_out_ref)
    x1_ref[seq] = x1
    _norm_route(x1, mod_ref, seq, norm2_ref, wr_t_ref, bias_t_ref,
                h2_ref.at[seq], experts_ref.at[seq], gatew_ref.at[seq], cnt_ref)


def _decay_tables(chunk):
    f32 = np.float32
    lg = np.log(f32(1.0) - f32(2.0) ** (f32(-5.0) - np.arange(RET_HEADS, dtype=f32))).astype(f32)
    idx = np.arange(chunk, dtype=f32)
    diff = idx[:, None] - idx[None, :]
    causal = diff >= 0
    dmat = np.where(causal[None], np.exp(lg[:, None, None] * np.where(causal, diff, f32(0.0))[None]), f32(0.0))
    cross = np.exp(lg[:, None] * (idx[None, :] + f32(1.0)))
    cross = np.broadcast_to(cross[:, :, None], (RET_HEADS, chunk, RET_V_DIM))
    tail = np.exp(lg[:, None] * (f32(chunk - 1.0) - idx)[None, :])
    cdec = np.broadcast_to(np.exp(lg * f32(chunk))[:, None, None], (RET_HEADS, RET_QK_DIM, RET_V_DIM))
    return tuple(jnp.asarray(np.ascontiguousarray(t, dtype=f32)) for t in (dmat, cross, tail, cdec))


def _rotary_angles(pos):
    half = RET_QK_DIM // 2
    freqs = (np.float32(ROPE_BASE) ** (-np.arange(half, dtype=np.float32) / np.float32(half))).astype(np.float32)
    return np.asarray(pos, np.float32)[:, None] * freqs[None, :]


def _rotary_tables(pos):
    ang = _rotary_angles(pos)
    cos, sin = np.cos(ang), np.sin(ang)
    cos_t = np.tile(np.concatenate([cos, cos], axis=-1), (1, RET_HEADS))
    sin_t = np.tile(np.concatenate([-sin, sin], axis=-1), (1, RET_HEADS))
    return jnp.asarray(cos_t, F32), jnp.asarray(sin_t, F32)


def _full(shape):
    return pl.BlockSpec(shape, lambda *_: (0,) * len(shape))


def _mix_prompt(x, mod, b0, b, norm1, w_in16, w_pool, pool_scale, w_out16, norm2, wr_t, bias_t,
                block_l=512, chunk=256, seqs=2):
    _, l, d = x.shape
    nl = l // block_l
    s0 = b0 // seqs
    cos_t, sin_t = _rotary_tables(np.arange(l))
    dmat, cross, tail, cdec = _decay_tables(chunk)
    kernel = functools.partial(_mix_prompt_kernel, block_l=block_l, chunk=chunk, seqs=seqs)
    tok = lambda bi, li: (bi, li, 0)
    per_seq = lambda bi, li: (bi, 0, 0)
    x1, h2, experts, gate_w, counts, ret, pool = pl.pallas_call(
        kernel,
        grid=(b // seqs, nl),
        in_specs=[pl.BlockSpec((seqs, block_l, d), lambda bi, li: (s0 + bi, li, 0)),
                  pl.BlockSpec((seqs, 6, d), lambda bi, li: (s0 + bi, 0, 0)),
                  _full((1, d)),
                  _full((d, IN_WIDTH)),
                  pl.BlockSpec((block_l, QK_WIDTH), lambda bi, li: (li, 0)),
                  pl.BlockSpec((block_l, QK_WIDTH), lambda bi, li: (li, 0)),
                  _full(dmat.shape), _full(cross.shape), _full(tail.shape), _full(cdec.shape),
                  _full(w_pool.shape), _full((1, POOL_WIDTH)), _full((d, d)), _full((1, d)),
                  _full(wr_t.shape), _full(bias_t.shape)],
        out_specs=[pl.BlockSpec((seqs, block_l, d), tok),
                   pl.BlockSpec((seqs, block_l, HALF), tok),
                   pl.BlockSpec((seqs, TOP_K, block_l), lambda bi, li: (bi, 0, li)),
                   pl.BlockSpec((seqs, block_l, LANES), tok),
                   _full((N_EXPERTS, LANES)),
                   pl.BlockSpec((seqs, RET_HEADS, RET_QK_DIM, RET_V_DIM), lambda bi, li: (bi, 0, 0, 0)),
                   pl.BlockSpec((seqs, POOL_BUF, POOL_WIDTH), per_seq)],
        out_shape=[jax.ShapeDtypeStruct((b, l, d), F32),
                   jax.ShapeDtypeStruct((b, l, HALF), U32),
                   jax.ShapeDtypeStruct((b, TOP_K, l), I32),
                   jax.ShapeDtypeStruct((b, l, LANES), F32),
                   jax.ShapeDtypeStruct((N_EXPERTS, LANES), F32),
                   jax.ShapeDtypeStruct((b, RET_HEADS, RET_QK_DIM, RET_V_DIM), F32),
                   jax.ShapeDtypeStruct((b, POOL_BUF, POOL_WIDTH), F32)],
        scratch_shapes=[pltpu.VMEM((seqs, QK_WIDTH, RET_V_DIM), F32),
                        pltpu.VMEM((seqs, POOL_CARRY + block_l, POOL_WIDTH), F32),
                        pltpu.VMEM((seqs, POOL_CARRY + block_l, POOL_WIDTH), F32),
                        pltpu.VMEM((seqs, block_l, RET_WIDTH), F32)],
        compiler_params=pltpu.CompilerParams(dimension_semantics=("arbitrary", "arbitrary"),
                                             vmem_limit_bytes=VMEM_LIMIT),
        name="mix_prompt",
    )(x, mod, norm1, w_in16, cos_t, sin_t, dmat, cross, tail, cdec, w_pool, pool_scale,
      w_out16, norm2, wr_t, bias_t)
    experts = jnp.transpose(experts, (1, 0, 2)).reshape(TOP_K, b * l)
    return x1, h2.reshape(b * l, HALF), experts, gate_w.reshape(b * l, LANES), counts, ret, pool


def _mix_sample_front_kernel(x_ref, mod_ref, norm1_ref, w_in_ref, cos_ref, sin_ref,
                             qt_ref, kt_ref, v_ref, g_ref, u_ref):
    x = x_ref[...]
    h = _rms(x) * norm1_ref[...] * (1.0 + _mod(mod_ref, 1)) + _mod(mod_ref, 0)
    proj = _dot(h.astype(BF16), w_in_ref[...])
    half = RET_QK_DIM // 2
    cos_c = cos_ref[...]
    sin_c = sin_ref[...]

    def rot_t(t):
        parts = []
        for hd in range(RET_HEADS):
            t1 = t[hd * RET_QK_DIM:hd * RET_QK_DIM + half]
            t2 = t[hd * RET_QK_DIM + half:(hd + 1) * RET_QK_DIM]
            parts += [t1 * cos_c - t2 * sin_c, t1 * sin_c + t2 * cos_c]
        return jnp.concatenate(parts, axis=0)

    qt_ref[...] = rot_t(proj[:, 0:QK_WIDTH].T)
    kt_ref[...] = rot_t(proj[:, QK_WIDTH:2 * QK_WIDTH].T) * (RET_QK_DIM ** -0.5)
    v_ref[...] = proj[:, 2 * QK_WIDTH:2 * QK_WIDTH + RET_WIDTH]
    g_ref[...] = proj[:, 2 * QK_WIDTH + RET_WIDTH:2 * QK_WIDTH + 2 * RET_WIDTH]
    u_ref[...] = proj[:, 2 * QK_WIDTH + 2 * RET_WIDTH:]


def _ret_step_kernel(qt_ref, kt_ref, v_ref, s0_ref, o_ref, s1_ref, *, block_b, decays):
    i = pl.program_id(0)
    lane = lax.broadcasted_iota(I32, qt_ref.shape, 1)
    for j in range(block_b):
        bi = i * block_b + j
        here = lane == bi
        q_col = jnp.sum(jnp.where(here, qt_ref[...], 0.0), axis=1, keepdims=True)
        k_col = jnp.sum(jnp.where(here, kt_ref[...], 0.0), axis=1, keepdims=True)
        v_row = v_ref[pl.ds(bi, 1), :]
        outs = []
        for hd in range(RET_HEADS):
            hrows = slice(hd * RET_QK_DIM, (hd + 1) * RET_QK_DIM)
            s1 = decays[hd] * s0_ref[j, hd] + k_col[hrows] * v_row[:, hd * RET_V_DIM:(hd + 1) * RET_V_DIM]
            s1_ref[j, hd] = s1
            outs.append(jnp.sum(q_col[hrows] * s1, axis=0, keepdims=True))
        o_ref[pl.ds(bi, 1), :] = jnp.concatenate(outs, axis=-1)


def _mix_sample_back_kernel(x_ref, mod_ref, o_ref, g_ref, u_ref, buf_ref, w_pool_ref, pool_scale_ref,
                            w_out_ref, norm2_ref, wr_t_ref, bias_t_ref,
                            x1_ref, h2_ref, experts_ref, gatew_ref, cnt_ref, pool_ref):
    cnt_ref[...] = jnp.zeros_like(cnt_ref)
    o_gated = _group_norm_gate(o_ref[...], g_ref[...])
    u = u_ref[...]
    pooled = []
    for gi, w in enumerate(POOL_WINDOWS):
        lanes = slice(gi * POOL_GROUP_DIM, (gi + 1) * POOL_GROUP_DIM)
        acc = u[:, lanes]
        for j in range(1, w):
            acc = acc + buf_ref[:, POOL_BUF - j, lanes]
        pooled.append(acc / float(w) - u[:, lanes])
    p = _pool_project(pooled, w_pool_ref, pool_scale_ref)
    pool_ref[:, 0:POOL_BUF - 1, :] = buf_ref[:, 1:POOL_BUF, :]
    pool_ref[:, POOL_BUF - 1, :] = u
    x1 = _out_residual(x_ref[...], o_gated, p, mod_ref, 0, w_out_ref)
    x1_ref[...] = x1
    _norm_route(x1, mod_ref, 0, norm2_ref, wr_t_ref, bias_t_ref, h2_ref, experts_ref, gatew_ref, cnt_ref)


def _mix_sample(x, mod, state_ret, state_pool, start, norm1, w_in16, w_pool,
                pool_scale, w_out16, norm2, wr_t, bias_t, block_b=32):
    n, d = x.shape
    half = RET_QK_DIM // 2
    ang = _rotary_angles([start])
    cos_c = jnp.asarray(np.broadcast_to(np.cos(ang).T, (half, n)), F32)
    sin_c = jnp.asarray(np.broadcast_to(np.sin(ang).T, (half, n)), F32)
    params = pltpu.CompilerParams(vmem_limit_bytes=VMEM_LIMIT)
    qt, kt, v, g, u = pl.pallas_call(
        _mix_sample_front_kernel,
        out_shape=[jax.ShapeDtypeStruct((QK_WIDTH, n), F32), jax.ShapeDtypeStruct((QK_WIDTH, n), F32),
                   jax.ShapeDtypeStruct((n, RET_WIDTH), F32), jax.ShapeDtypeStruct((n, RET_WIDTH), F32),
                   jax.ShapeDtypeStruct((n, POOL_WIDTH), F32)],
        compiler_params=params,
        name="mix_sample_front",
    )(x, mod, norm1, w_in16, cos_c, sin_c)

    lg = np.log(1.0 - 2.0 ** (-5.0 - np.arange(RET_HEADS, dtype=np.float32)), dtype=np.float32)
    decays = tuple(float(np.exp(lg[h])) for h in range(RET_HEADS))
    state_block = (block_b, RET_HEADS, RET_QK_DIM, RET_V_DIM)
    o, s1 = pl.pallas_call(
        functools.partial(_ret_step_kernel, block_b=block_b, decays=decays),
        grid=(n // block_b,),
        in_specs=[_full((QK_WIDTH, n)), _full((QK_WIDTH, n)), _full((n, RET_WIDTH)),
                  pl.BlockSpec(state_block, lambda i: (i, 0, 0, 0))],
        out_specs=[_full((n, RET_WIDTH)), pl.BlockSpec(state_block, lambda i: (i, 0, 0, 0))],
        out_shape=[jax.ShapeDtypeStruct((n, RET_WIDTH), F32),
                   jax.ShapeDtypeStruct(state_ret.shape, F32)],
        compiler_params=pltpu.CompilerParams(dimension_semantics=("arbitrary",),
                                             vmem_limit_bytes=VMEM_LIMIT),
        name="ret_step",
    )(qt, kt, v, state_ret)

    x1, h2, experts, gate_w, counts, pool = pl.pallas_call(
        _mix_sample_back_kernel,
        out_shape=[jax.ShapeDtypeStruct((n, d), F32),
                   jax.ShapeDtypeStruct((n, HALF), U32),
                   jax.ShapeDtypeStruct((TOP_K, n), I32),
                   jax.ShapeDtypeStruct((n, LANES), F32),
                   jax.ShapeDtypeStruct((N_EXPERTS, LANES), F32),
                   jax.ShapeDtypeStruct(state_pool.shape, F32)],
        compiler_params=params,
        name="mix_sample_back",
    )(x, mod, o, g, u, state_pool, w_pool, pool_scale, w_out16, norm2, wr_t, bias_t)
    return x1, h2, experts, gate_w, counts, s1, pool


def _plan_kernel(experts_ref, cnt_ref, pos_ref, meta_ref, carry_ref, off_ref, *, block_t):
    @pl.when(pl.program_id(0) == 0)
    def _():
        cnt = cnt_ref[...]
        n_tile = jnp.floor((cnt + (ROW_TILE - 1.0)) * (1.0 / ROW_TILE))
        upto = (lax.broadcasted_iota(I32, (N_EXPERTS, N_EXPERTS), 1)
                <= lax.broadcasted_iota(I32, (N_EXPERTS, N_EXPERTS), 0))
        tile_end = _dot(jnp.where(upto, 1.0, 0.0).astype(BF16), n_tile.astype(BF16))
        tile_start = tile_end - n_tile
        off_ref[...] = tile_start * ROW_TILE
        carry_ref[...] = jnp.zeros_like(carry_ref)
        lane = lax.broadcasted_iota(I32, cnt.shape, 1)
        meta_ref[...] = jnp.where(lane == 0, tile_start, jnp.where(lane == 1, n_tile, cnt)).astype(I32)

    e_blk = experts_ref[...]
    eidx = lax.broadcasted_iota(I32, (N_EXPERTS, block_t), 0)
    member = jnp.zeros((N_EXPERTS, block_t), F32)
    for s in range(TOP_K):
        member = member + jnp.where(eidx == e_blk[s:s + 1, :], 1.0, 0.0)
    before = (lax.broadcasted_iota(I32, (block_t, block_t), 0)
              < lax.broadcasted_iota(I32, (block_t, block_t), 1))
    rank = _dot(member.astype(BF16), jnp.where(before, 1.0, 0.0).astype(BF16))
    row = off_ref[:, 0:1] + carry_ref[:, 0:1] + rank
    carry_ref[...] += jnp.broadcast_to(jnp.sum(member, axis=1, keepdims=True), (N_EXPERTS, LANES))
    out = [jnp.sum(jnp.where(eidx == e_blk[s:s + 1, :], row, 0.0), axis=0, keepdims=True)
           for s in range(TOP_K)]
    pos_ref[...] = jnp.concatenate(out, axis=0).astype(I32)


def _plan(experts_all, counts, max_block=1024):
    n_tokens = experts_all.shape[1]
    block_t = max(k for k in range(LANES, max_block + 1, LANES) if n_tokens % k == 0)
    return pl.pallas_call(
        functools.partial(_plan_kernel, block_t=block_t),
        grid=(n_tokens // block_t,),
        in_specs=[pl.BlockSpec((TOP_K, block_t), lambda j: (0, j)), _full((N_EXPERTS, LANES))],
        out_specs=[pl.BlockSpec((TOP_K, block_t), lambda j: (0, j)), _full((N_EXPERTS, LANES))],
        out_shape=[jax.ShapeDtypeStruct((TOP_K, n_tokens), I32),
                   jax.ShapeDtypeStruct((N_EXPERTS, LANES), I32)],
        scratch_shapes=[pltpu.VMEM((N_EXPERTS, LANES), F32)] * 2,
        compiler_params=pltpu.CompilerParams(dimension_semantics=("arbitrary",),
                                             vmem_limit_bytes=VMEM_LIMIT),
        name="plan",
    )(experts_all, counts)


def _sc_workers():
    info = plsc.get_sparse_core_info()
    return info.num_cores, info.num_cores * info.num_subcores


def _sc_scatter_rows(sources, pos_t, n_out, after=()):
    w = sources[0].shape[1]
    s = pos_t.shape[0]
    n_cores, n_workers = _sc_workers()
    bounds = np.cumsum([0] + [src.shape[0] // SC_CHUNK for src in sources])
    n_chunks = int(bounds[-1])
    iters = -(-n_chunks // n_workers)
    mesh = plsc.VectorSubcoreMesh(core_axis_name="c", subcore_axis_name="s")

    @functools.partial(
        pl.kernel, mesh=mesh, out_type=jax.ShapeDtypeStruct((n_out, w), sources[0].dtype),
        scratch_types=[pltpu.VMEM((SC_CHUNK, w), sources[0].dtype), pltpu.VMEM((s, SC_CHUNK), I32),
                       pltpu.SemaphoreType.DMA],
        name="dispatch")
    def k(*refs):
        src_hbm, pos_hbm = refs[:len(sources)], refs[len(sources)]
        out_hbm, rows_v, idx_v, sem = refs[len(sources) + 1 + len(after):]
        wid = lax.axis_index("s") * n_cores + lax.axis_index("c")

        @pl.loop(0, iters)
        def _(it):
            c = it * n_workers + wid
            for src, lo, hi in zip(src_hbm, bounds[:-1], bounds[1:]):
                @pl.when((c >= int(lo)) & (c < int(hi)))
                def _():
                    base = pl.multiple_of((c - int(lo)) * SC_CHUNK, SC_CHUNK)
                    pltpu.sync_copy(src.at[pl.ds(base, SC_CHUNK)], rows_v)

            @pl.when(c < n_chunks)
            def _():
                base = pl.multiple_of(c * SC_CHUNK, SC_CHUNK)
                pltpu.sync_copy(pos_hbm.at[:, pl.ds(base, SC_CHUNK)], idx_v)
                copies = [pltpu.async_copy(rows_v, out_hbm.at[idx_v.at[j]], sem) for j in range(s)]
                for cp in copies:
                    cp.wait()

    return k(*sources, pos_t, *after)


def _sc_gather_rows(table, pos_t):
    _, w = table.shape
    s, t = pos_t.shape
    n_cores, n_workers = _sc_workers()
    n_chunks = t // SC_CHUNK
    iters = -(-n_chunks // n_workers)
    mesh = plsc.VectorSubcoreMesh(core_axis_name="c", subcore_axis_name="s")

    @functools.partial(
        pl.kernel, mesh=mesh, out_type=jax.ShapeDtypeStruct((s, t, w), table.dtype),
        scratch_types=[pltpu.VMEM((SC_CHUNK, w), table.dtype), pltpu.VMEM((s, SC_CHUNK), I32),
                       pltpu.SemaphoreType.DMA],
        name="combine")
    def k(table_hbm, pos_hbm, out_hbm, rows_v, idx_v, sem):
        wid = lax.axis_index("s") * n_cores + lax.axis_index("c")

        @pl.loop(0, iters)
        def _(it):
            c = it * n_workers + wid

            @pl.when(c < n_chunks)
            def _():
                base = pl.multiple_of(c * SC_CHUNK, SC_CHUNK)
                pltpu.sync_copy(pos_hbm.at[:, pl.ds(base, SC_CHUNK)], idx_v)
                for j in range(s):
                    pltpu.async_copy(table_hbm.at[idx_v.at[j]], rows_v, sem).wait()
                    pltpu.sync_copy(rows_v, out_hbm.at[j, pl.ds(base, SC_CHUNK)])

    return k(table, pos_t)


def _sc_pack_weights(w, rows_per_item, after=()):
    e, r, c = w.shape
    half = r // 2
    rb = rows_per_item
    per_expert = half // rb
    n_cores, n_workers = _sc_workers()
    per_worker = e * per_expert // n_workers
    assert per_worker * n_workers == e * per_expert and per_worker % 2 == 0 and c % (SC_LANES * SC_UNROLL) == 0
    mesh = plsc.VectorSubcoreMesh(core_axis_name="c", subcore_axis_name="s")

    @functools.partial(
        pl.kernel, mesh=mesh, out_type=jax.ShapeDtypeStruct((e * half, c), U32),
        scratch_types=[pltpu.VMEM((2, rb, c), F32), pltpu.VMEM((2, rb, c), F32), pltpu.VMEM((2, rb, c), U32),
                       pltpu.SemaphoreType.DMA((2,)), pltpu.SemaphoreType.DMA((2,))],
        compiler_params=pltpu.CompilerParams(needs_layout_passes=False),
        name="pack_weights")
    def k(w_hbm, *refs):
        out_hbm, lo_v, hi_v, out_v, in_sem, out_sem = refs[len(after):]
        wid = lax.axis_index("s") * n_cores + lax.axis_index("c")
        first = wid * per_worker

        def rows(item):
            ex = item // per_expert
            j = item - ex * per_expert
            return (pl.multiple_of(ex * r + j * rb, rb), pl.multiple_of(ex * r + half + j * rb, rb),
                    pl.multiple_of(ex * half + j * rb, rb))

        def loads(item, b):
            lo_row, hi_row, _ = rows(item)
            return (pltpu.make_async_copy(w_hbm.at[pl.ds(lo_row, rb)], lo_v.at[b], in_sem.at[b]),
                    pltpu.make_async_copy(w_hbm.at[pl.ds(hi_row, rb)], hi_v.at[b], in_sem.at[b]))

        def store(item, b):
            return pltpu.make_async_copy(out_v.at[b], out_hbm.at[pl.ds(rows(item)[2], rb)], out_sem.at[b])

        for cp in loads(first, 0):
            cp.start()

        @pl.loop(0, per_worker // 2)
        def _(pair):
            for b in range(2):
                item = first + pair * 2 + b
                for cp in loads(item, b):
                    cp.wait()

                @pl.when(item + 1 < first + per_worker)
                def _():
                    for cp in loads(item + 1, 1 - b):
                        cp.start()

                @pl.when(pair > 0)
                def _():
                    store(item - 2, b).wait()

                @pl.loop(0, rb)
                def _(i):
                    @pl.loop(0, c // (SC_LANES * SC_UNROLL))
                    def _(vb):
                        for u in range(SC_UNROLL):
                            sl = pl.ds(pl.multiple_of((vb * SC_UNROLL + u) * SC_LANES, SC_LANES), SC_LANES)
                            packed = plsc.pack(lo_v[b, i, sl], hi_v[b, i, sl], format=plsc.PackFormat.INTERLEAVED)
                            out_v[b, i, sl] = plsc.bitcast(packed, U32)

                store(item, b).start()

        for b in range(2):
            store(first + per_worker - 2 + b, b).wait()

    return k(w.reshape(e * r, c), *after).reshape(e, half, c)


def _experts_kernel(first_ref, ntile_ref, cnt_ref, xs_hbm, wg_ref, wu_ref, wd_ref, ys_hbm,
                    wg16_ref, wu16_ref, wd16_ref, x_buf, y_buf, in_sem, out_sem):
    ahead = STREAM_DEPTH - MAX_WIDTH
    e = pl.program_id(0)
    n_used = first_ref[N_EXPERTS - 1] + ntile_ref[N_EXPERTS - 1]
    first, n_mine, count = first_ref[e], ntile_ref[e], cnt_ref[e]

    def tile_rows(g):
        return pl.ds(pl.multiple_of(g * ROW_TILE, ROW_TILE), ROW_TILE)

    def load(g):
        slot = lax.rem(g, STREAM_DEPTH)
        return pltpu.make_async_copy(xs_hbm.at[tile_rows(g)], x_buf.at[slot], in_sem.at[slot])

    def store(g):
        slot = lax.rem(g, STREAM_DEPTH)
        return pltpu.make_async_copy(y_buf.at[slot], ys_hbm.at[tile_rows(g)], out_sem.at[slot])

    @pl.when(e == 0)
    def _():
        for g0 in range(ahead):
            @pl.when(g0 < n_used)
            def _():
                load(g0).start()

    for packed_ref, w16_ref in ((wg_ref, wg16_ref), (wu_ref, wu16_ref), (wd_ref, wd16_ref)):
        rows = packed_ref.shape[1]
        lo, hi = _unpack_rows(packed_ref[0])
        w16_ref[0:rows, :] = lo.astype(BF16)
        w16_ref[rows:, :] = hi.astype(BF16)

    def run(j, width):
        tiles = [first + j + t for t in range(width)]
        for g in tiles:
            load(g).wait()

            @pl.when(g + ahead < n_used)
            def _():
                load(g + ahead).start()

            @pl.when(g >= STREAM_DEPTH)
            def _():
                store(g - STREAM_DEPTH).wait()

        words = jnp.concatenate([x_buf[lax.rem(g, STREAM_DEPTH)] for g in tiles], axis=0)
        row = lax.broadcasted_iota(I32, words.shape, 0)
        words = jnp.where(row < count - j * ROW_TILE, words, jnp.uint32(0))
        lo, hi = _unpack_rows(words)
        lo, hi = lo.astype(BF16), hi.astype(BF16)
        hg = _dot(lo, wg16_ref[0:HALF, :]) + _dot(hi, wg16_ref[HALF:, :])
        hu = _dot(lo, wu16_ref[0:HALF, :]) + _dot(hi, wu16_ref[HALF:, :])
        a = (_silu(hg) * hu).astype(BF16)
        y = _pack_rows(_dot(a, wd16_ref[...]))
        for t, g in enumerate(tiles):
            y_buf[lax.rem(g, STREAM_DEPTH)] = y[t * ROW_TILE:(t + 1) * ROW_TILE]
            store(g).start(priority=1)

    def widest(p, carry):
        run(MAX_WIDTH * p, MAX_WIDTH)
        return carry

    lax.fori_loop(0, n_mine // MAX_WIDTH, widest, 0)
    done = n_mine - lax.rem(n_mine, MAX_WIDTH)
    width = MAX_WIDTH // 2
    while width:
        has = lax.rem(n_mine // width, 2) == 1

        @pl.when(has)
        def _(width=width, done=done):
            run(done, width)

        done = done + jnp.where(has, width, 0)
        width //= 2

    @pl.when(e == N_EXPERTS - 1)
    def _():
        for back in range(STREAM_DEPTH, 0, -1):
            @pl.when(n_used >= back)
            def _():
                store(n_used - back).wait()


def _experts(xs, first_tile, n_tile, count, w_eg, w_eu, w_ed):
    d = D_MODEL
    by_expert = lambda e, *_: (e, 0, 0)
    grid_spec = pltpu.PrefetchScalarGridSpec(
        num_scalar_prefetch=3,
        grid=(N_EXPERTS,),
        in_specs=[pl.BlockSpec(memory_space=pl.ANY),
                  pl.BlockSpec((1, d // 2, EXPERT_DIM), by_expert),
                  pl.BlockSpec((1, d // 2, EXPERT_DIM), by_expert),
                  pl.BlockSpec((1, EXPERT_DIM // 2, d), by_expert)],
        out_specs=pl.BlockSpec(memory_space=pl.ANY),
        scratch_shapes=[pltpu.VMEM((d, EXPERT_DIM), BF16), pltpu.VMEM((d, EXPERT_DIM), BF16),
                        pltpu.VMEM((EXPERT_DIM, d), BF16),
                        pltpu.VMEM((STREAM_DEPTH, ROW_TILE, HALF), U32),
                        pltpu.VMEM((STREAM_DEPTH, ROW_TILE, HALF), U32),
                        pltpu.SemaphoreType.DMA((STREAM_DEPTH,)), pltpu.SemaphoreType.DMA((STREAM_DEPTH,))])
    return pl.pallas_call(
        _experts_kernel,
        grid_spec=grid_spec,
        out_shape=jax.ShapeDtypeStruct(xs.shape, U32),
        compiler_params=pltpu.CompilerParams(dimension_semantics=("arbitrary",),
                                             vmem_limit_bytes=VMEM_LIMIT),
        name="experts",
    )(first_tile, n_tile, count, xs, w_eg, w_eu, w_ed)


def _final_kernel(z_ref, gatew_ref, h2_ref, x1_ref, mod_ref, normf_ref, wsg_ref, wsu_ref, wsd_ref, *rest):
    y_ref, wsg16_ref, wsu16_ref, wsd16_ref = rest[-4:]

    @pl.when(pl.program_id(0) == 0)
    def _():
        wsg16_ref[...] = wsg_ref[...].astype(BF16)
        wsu16_ref[...] = wsu_ref[...].astype(BF16)
        wsd16_ref[...] = wsd_ref[...].astype(BF16)

    lo, hi = _unpack_rows(h2_ref[...])
    h = jnp.concatenate([lo, hi], axis=-1).astype(BF16)
    a = _silu(_dot(h, wsg16_ref[...])) * _dot(h, wsu16_ref[...])
    acc = _dot(a.astype(BF16), wsd16_ref[...])
    for s in range(TOP_K):
        lo, hi = _unpack_rows(z_ref[s])
        acc = acc + gatew_ref[:, s:s + 1] * jnp.concatenate([lo, hi], axis=-1)
    x2 = x1_ref[...] + _mod(mod_ref, 5) * acc
    y_ref[...] = _rms(x2) * normf_ref[...]


def _final(z, gate_w, h2, x1, mod, norm_f, w_sg, w_su, w_sd, block_t, first_block, per_seq,
           seq0=0, out_rows=None, y_prev=None):
    t, d = x1.shape
    out_rows = t if out_rows is None else out_rows
    out_first = seq0 * per_seq
    tok = lambda i: (i, 0)
    if per_seq:
        mod_spec = pl.BlockSpec((1, 6, d), lambda i: (seq0 + i // per_seq, 0, 0))
    else:
        mod_spec = pl.BlockSpec((block_t, 6 * d), tok)
    operands = [z, gate_w, h2, x1, mod, norm_f, w_sg, w_su, w_sd]
    in_specs = [pl.BlockSpec((TOP_K, block_t, HALF), lambda i: (0, first_block + i, 0)),
                pl.BlockSpec((block_t, LANES), tok),
                pl.BlockSpec((block_t, HALF), tok),
                pl.BlockSpec((block_t, d), tok),
                mod_spec,
                _full((1, d)),
                _full((d, EXPERT_DIM)), _full((d, EXPERT_DIM)), _full((EXPERT_DIM, d))]
    aliases = {}
    if y_prev is not None:
        aliases = {len(operands): 0}
        operands.append(y_prev)
        in_specs.append(pl.BlockSpec(memory_space=pl.ANY))
    return pl.pallas_call(
        _final_kernel,
        grid=(t // block_t,),
        in_specs=in_specs,
        out_specs=pl.BlockSpec((block_t, d), lambda i: (out_first + i, 0)),
        out_shape=jax.ShapeDtypeStruct((out_rows, d), F32),
        scratch_shapes=[pltpu.VMEM((d, EXPERT_DIM), BF16), pltpu.VMEM((d, EXPERT_DIM), BF16),
                        pltpu.VMEM((EXPERT_DIM, d), BF16)],
        input_output_aliases=aliases,
        compiler_params=pltpu.CompilerParams(dimension_semantics=("arbitrary",),
                                             vmem_limit_bytes=VMEM_LIMIT),
        name="final",
    )(*operands)


def kernel(x_prompt, x_sample, c_prompt, c_sample, state_ret, state_pool, norm1, norm2, norm_f,
           w_ada, b_ada, w_in, w_out, w_pool, pool_scale, w_router, router_bias, w_exp_gate,
           w_exp_up, w_exp_down, w_sh_gate, w_sh_up, w_sh_down):
    b, l, d = x_prompt.shape
    n = x_sample.shape[0]

    mod_p, mod_s = _ada(c_prompt, c_sample, w_ada[0], b_ada[0])
    mod_p = mod_p.reshape(b, 6, d)

    w_in16 = w_in[0].astype(BF16)
    w_out16 = w_out[0].astype(BF16)
    wr_t = w_router[0].T
    bias_t = jnp.broadcast_to(router_bias[0][:, None], (N_EXPERTS, LANES))
    n1, n2, nf = norm1[0].reshape(1, d), norm2[0].reshape(1, d), norm_f.reshape(1, d)
    ps = pool_scale[0].reshape(1, POOL_WIDTH)
    shared = (w_sh_gate[0], w_sh_up[0], w_sh_down[0])

    def routed(sources, experts, counts, repacked):
        n_tiles = experts.shape[1] * TOP_K // ROW_TILE + N_EXPERTS
        pos_t, meta = _plan(experts, counts)
        xs = _sc_scatter_rows(sources, pos_t, n_tiles * ROW_TILE, after=repacked)
        ys = _experts(xs, meta[:, 0], meta[:, 1], meta[:, 2], *expert_w)
        return _sc_gather_rows(ys, pos_t)

    mix_args = (n1, w_in16, w_pool[0], ps, w_out16, n2, wr_t, bias_t)
    x1_s, h2_s, experts_s, gatew_s, counts_s, ret_s, pool_s = _mix_sample(
        x_sample.reshape(n, d), mod_s, state_ret[0], state_pool[0], float(PAST_LEN), *mix_args)

    expert_w = (_sc_pack_weights(w_exp_gate[0], 64, after=(x1_s,)),
                _sc_pack_weights(w_exp_up[0], 64, after=(x1_s,)),
                _sc_pack_weights(w_exp_down[0], 16, after=(x1_s,)))

    ba = b // 2
    bb = b - ba
    x1_a, h2_a, experts_a, gatew_a, counts_a, ret_a, pool_a = _mix_prompt(x_prompt, mod_p, 0, ba, *mix_args)
    z_a = routed((h2_a,), experts_a, counts_a, expert_w)
    x1_b, h2_b, experts_b, gatew_b, counts_b, ret_b, pool_b = _mix_prompt(x_prompt, mod_p, ba, bb, *mix_args)
    z_b = routed((h2_b, h2_s), jnp.concatenate([experts_b, experts_s], axis=1), counts_b + counts_s,
                 expert_w[:2])

    block_t = 512
    per_seq = l // block_t
    y_s = _final(z_b, gatew_s, h2_s, x1_s, mod_s, nf, *shared,
                 block_t=n, first_block=bb * l // n, per_seq=0)
    y_p = _final(z_b, gatew_b, h2_b, x1_b.reshape(bb * l, d), mod_p, nf, *shared,
                 block_t=block_t, first_block=0, per_seq=per_seq, seq0=ba, out_rows=b * l)
    y_p = _final(z_a, gatew_a, h2_a, x1_a.reshape(ba * l, d), mod_p, nf, *shared,
                 block_t=block_t, first_block=0, per_seq=per_seq, seq0=0, out_rows=b * l, y_prev=y_p)

    ret_p = jnp.concatenate([ret_a, ret_b], axis=0)
    pool_p = jnp.concatenate([pool_a, pool_b], axis=0)
    return (y_p.reshape(b, l, d), y_s.reshape(n, 1, d), ret_p[None], pool_p[None],
            ret_s[None], pool_s[None])
```

```python
import functools

import jax
import jax.numpy as jnp
import numpy as np
from jax import lax
from jax.experimental import pallas as pl
from jax.experimental.pallas import tpu as pltpu
from jax.experimental.pallas import tpu_sc as plsc

D_MODEL = 1024
RET_HEADS = 4
RET_QK_DIM = 64
RET_V_DIM = 128
RET_WIDTH = RET_HEADS * RET_V_DIM
QK_WIDTH = RET_HEADS * RET_QK_DIM
ROPE_BASE = 10000.0
POOL_WINDOWS = (2, 4, 8, 16)
POOL_WIDTH = 512
POOL_GROUP_DIM = 128
POOL_BUF = 15
IN_WIDTH = 2 * QK_WIDTH + 2 * RET_WIDTH + POOL_WIDTH
N_EXPERTS = 64
TOP_K = 8
N_EXPERT_GROUPS = 8
GROUP_SIZE = N_EXPERTS // N_EXPERT_GROUPS
TOP_GROUPS = 4
EXPERT_DIM = 256
ROUTE_SCALE = 2.5
EPS = 1e-6
PAST_LEN = 16384

LANES = 128
SUBLANES = 8
POOL_CARRY = 24
VMEM_LIMIT = 56 * 1024 * 1024
HALF = D_MODEL // 2
ROW_TILE = 256
MAX_WIDTH = 4
SC_CHUNK = 128
SC_LANES = 16
SC_UNROLL = 16
STREAM_DEPTH = 16

BF16 = jnp.bfloat16
F32 = jnp.float32
U32 = jnp.uint32
I32 = jnp.int32


def _silu(x):
    return x * jax.nn.sigmoid(x)


def _dot(a, b):
    return jnp.dot(a, b, preferred_element_type=F32)


def _rms(x):
    return x * lax.rsqrt(jnp.mean(x * x, axis=-1, keepdims=True) + EPS)


def _mod(mod_ref, i, seq=0):
    if len(mod_ref.shape) == 3:
        return mod_ref[seq, i:i + 1, :]
    return mod_ref[:, i * D_MODEL:(i + 1) * D_MODEL]


def _split_bf16(x):
    hi = x.astype(BF16)
    lo = (x - hi.astype(F32)).astype(BF16)
    return hi, lo


def _pack_rows(x):
    return _pack_pair(x[:, :HALF], x[:, HALF:])


def _pack_pair(lo, hi):
    lo = lax.bitcast_convert_type(lo.astype(BF16).astype(F32), U32)
    hi = lax.bitcast_convert_type(hi.astype(BF16).astype(F32), U32)
    return (hi & jnp.uint32(0xFFFF0000)) | (lo >> jnp.uint32(16))


def _unpack_rows(w):
    lo = lax.bitcast_convert_type(w << jnp.uint32(16), F32)
    hi = lax.bitcast_convert_type(w & jnp.uint32(0xFFFF0000), F32)
    return lo, hi


def _first_max_onehot(work, idx, n):
    m = jnp.max(work, axis=0, keepdims=True)
    first = jnp.min(jnp.where(work == m, idx, float(n)), axis=0, keepdims=True)
    return idx == first


def _route(h2, wr_t_ref, bias_t_ref):
    n = h2.shape[0]
    h_hi, h_lo = _split_bf16(h2)
    w_hi, w_lo = _split_bf16(wr_t_ref[...])
    nt = (((1,), (1,)), ((), ()))
    logits = (lax.dot_general(w_hi, h_hi, nt, preferred_element_type=F32)
              + lax.dot_general(w_hi, h_lo, nt, preferred_element_type=F32)
              + lax.dot_general(w_lo, h_hi, nt, preferred_element_type=F32))
    scores = jax.nn.sigmoid(logits)
    biased = scores + bias_t_ref[:, 0:1]
    b3 = biased.reshape(N_EXPERT_GROUPS, GROUP_SIZE, n)
    i3 = lax.broadcasted_iota(I32, b3.shape, 1).astype(F32)
    m1 = jnp.max(b3, axis=1, keepdims=True)
    first = jnp.min(jnp.where(b3 == m1, i3, float(GROUP_SIZE)), axis=1, keepdims=True)
    m2 = jnp.max(jnp.where(i3 == first, -jnp.inf, b3), axis=1, keepdims=True)
    gscore = (m1 + m2).reshape(N_EXPERT_GROUPS, n)
    gidx = lax.broadcasted_iota(I32, gscore.shape, 0).astype(F32)
    gsel = jnp.zeros(gscore.shape, F32)
    work = gscore
    for _ in range(TOP_GROUPS):
        hit = _first_max_onehot(work, gidx, N_EXPERT_GROUPS)
        gsel = jnp.where(hit, 1.0, gsel)
        work = jnp.where(hit, -jnp.inf, work)
    gsel3 = jnp.broadcast_to(gsel.reshape(N_EXPERT_GROUPS, 1, n), b3.shape)
    work = jnp.where(gsel3 > 0.0, b3, -jnp.inf).reshape(N_EXPERTS, n)
    eidx = lax.broadcasted_iota(I32, work.shape, 0).astype(F32)
    sel = jnp.zeros(work.shape, F32)
    for _ in range(TOP_K):
        hit = _first_max_onehot(work, eidx, N_EXPERTS)
        sel = jnp.where(hit, 1.0, sel)
        work = jnp.where(hit, -jnp.inf, work)
    picked = jnp.where(sel > 0.0, scores, 0.0)
    gates = picked / jnp.sum(picked, axis=0, keepdims=True) * ROUTE_SCALE
    below = (lax.broadcasted_iota(I32, (N_EXPERTS, N_EXPERTS), 1)
             < lax.broadcasted_iota(I32, (N_EXPERTS, N_EXPERTS), 0))
    slot = _dot(jnp.where(below, 1.0, 0.0).astype(BF16), sel.astype(BF16))
    e_rows, w_rows = [], []
    for s in range(TOP_K):
        here = jnp.where(slot == float(s), sel, 0.0)
        e_rows.append(jnp.sum(here * eidx, axis=0, keepdims=True))
        w_rows.append(jnp.sum(here * gates, axis=0, keepdims=True))
    experts = jnp.concatenate(e_rows, axis=0).astype(I32)
    w_t = jnp.concatenate(w_rows + [jnp.zeros((LANES - TOP_K, n), F32)], axis=0)
    counts = jnp.broadcast_to(jnp.sum(sel, axis=1, keepdims=True), (N_EXPERTS, LANES))
    return experts, w_t.T, counts


def _group_norm_gate(o, g):
    parts = []
    for h in range(RET_HEADS):
        oh = o[:, h * RET_V_DIM:(h + 1) * RET_V_DIM]
        mu = jnp.mean(oh, axis=-1, keepdims=True)
        ctr = oh - mu
        var = jnp.mean(ctr * ctr, axis=-1, keepdims=True)
        parts.append(ctr * lax.rsqrt(var + EPS))
    return _silu(g) * jnp.concatenate(parts, axis=-1)


def _pool_project(pooled, w_pool_ref, pool_scale_ref):
    parts = [_dot(p.astype(BF16), w_pool_ref[gi].astype(BF16)) for gi, p in enumerate(pooled)]
    return jnp.concatenate(parts, axis=-1) * pool_scale_ref[...]


def _out_residual(x, o_gated, p, mod_ref, seq, w_out_ref):
    mix = jnp.concatenate([o_gated, p], axis=-1).astype(BF16)
    return x + _mod(mod_ref, 2, seq) * _dot(mix, w_out_ref[...])


def _norm_route(x1, mod_ref, seq, norm2_ref, wr_t_ref, bias_t_ref, h2_ref, experts_ref, gatew_ref, cnt_ref):
    h2 = _rms(x1) * norm2_ref[...] * (1.0 + _mod(mod_ref, 4, seq)) + _mod(mod_ref, 3, seq)
    h2_ref[...] = _pack_rows(h2)
    experts, gate_w, counts = _route(h2, wr_t_ref, bias_t_ref)
    experts_ref[...] = experts
    gatew_ref[...] = gate_w
    cnt_ref[...] += counts


def _ada_kernel(cp_ref, cs_ref, w_ref, b_ref, op_ref, os_ref):
    w16 = w_ref[...].astype(BF16)
    for c_ref, o_ref in ((cp_ref, op_ref), (cs_ref, os_ref)):
        o_ref[...] = _dot(_silu(c_ref[...]).astype(BF16), w16) + b_ref[...]


def _ada(c_prompt, c_sample, w_ada, b_ada, block_n=1536):
    d, width = w_ada.shape
    rows = lambda c: pl.BlockSpec((c.shape[0], d), lambda j: (0, 0))
    cols = lambda c: pl.BlockSpec((c.shape[0], block_n), lambda j: (0, j))
    return pl.pallas_call(
        _ada_kernel,
        grid=(width // block_n,),
        in_specs=[rows(c_prompt), rows(c_sample),
                  pl.BlockSpec((d, block_n), lambda j: (0, j)),
                  pl.BlockSpec((1, block_n), lambda j: (0, j))],
        out_specs=[cols(c_prompt), cols(c_sample)],
        out_shape=[jax.ShapeDtypeStruct((c.shape[0], width), F32) for c in (c_prompt, c_sample)],
        compiler_params=pltpu.CompilerParams(vmem_limit_bytes=VMEM_LIMIT),
        name="ada",
    )(c_prompt, c_sample, w_ada, b_ada.reshape(1, width))


def _mix_prompt_kernel(x_ref, mod_ref, norm1_ref, w_in_ref, cos_ref, sin_ref, dmat_ref, cross_ref,
                       tail_ref, cdec_ref, w_pool_ref, pool_scale_ref, w_out_ref, norm2_ref,
                       wr_t_ref, bias_t_ref,
                       x1_ref, h2_ref, experts_ref, gatew_ref, cnt_ref, ret_ref, pool_ref,
                       state_ref, ext_ref, win_ref, o_ref, *, block_l, chunk, seqs):
    li = pl.program_id(1)

    @pl.when((pl.program_id(0) == 0) & (li == 0))
    def _():
        cnt_ref[...] = jnp.zeros_like(cnt_ref)

    @pl.when(li == 0)
    def _():
        state_ref[...] = jnp.zeros_like(state_ref)
        ext_ref[:, 0:POOL_CARRY, :] = jnp.zeros((seqs, POOL_CARRY, POOL_WIDTH), F32)
        win_ref[:, 0:SUBLANES, :] = jnp.zeros((seqs, SUBLANES, POOL_WIDTH), F32)

    for seq in range(seqs):
        _mix_prompt_seq(seq, li, x_ref, mod_ref, norm1_ref, w_in_ref, cos_ref, sin_ref, dmat_ref, cross_ref,
                        tail_ref, cdec_ref, w_pool_ref, pool_scale_ref, w_out_ref, norm2_ref,
                        wr_t_ref, bias_t_ref, x1_ref, h2_ref, experts_ref, gatew_ref, cnt_ref,
                        state_ref.at[seq], ext_ref.at[seq], win_ref.at[seq], o_ref.at[seq],
                        block_l=block_l, chunk=chunk)

    @pl.when(li == pl.num_programs(1) - 1)
    def _():
        ret_ref[...] = state_ref[...].reshape(ret_ref.shape)
        pool_ref[...] = ext_ref[:, POOL_CARRY - POOL_BUF:POOL_CARRY, :]


def _window_sums(ext_ref, win_ref, block_l):
    g = POOL_GROUP_DIM
    top = POOL_CARRY + block_l
    new = slice(POOL_CARRY - SUBLANES, None)
    s2 = ext_ref[SUBLANES:top, :] + ext_ref[SUBLANES - 1:top - 1, :]
    win_ref[SUBLANES:top, g:] = s2[:, g:]
    s4 = s2[:, g:] + win_ref[SUBLANES - 2:top - 2, g:]
    win_ref[SUBLANES:top, 2 * g:] = s4[:, g:]
    s8 = s4[:, g:] + win_ref[SUBLANES - 4:top - 4, 2 * g:]
    win_ref[SUBLANES:top, 3 * g:] = s8[:, g:]
    s16 = s8[:, g:] + win_ref[0:top - SUBLANES, 3 * g:]
    return [s2[new, 0:g], s4[new, 0:g], s8[new, 0:g], s16[new, :]]


def _mix_prompt_seq(seq, li, x_ref, mod_ref, norm1_ref, w_in_ref, cos_ref, sin_ref, dmat_ref, cross_ref,
                    tail_ref, cdec_ref, w_pool_ref, pool_scale_ref, w_out_ref, norm2_ref,
                    wr_t_ref, bias_t_ref, x1_ref, h2_ref, experts_ref, gatew_ref, cnt_ref,
                    state_ref, ext_ref, win_ref, o_ref, *, block_l, chunk):
    x = x_ref[seq]
    h16 = (_rms(x) * norm1_ref[...] * (1.0 + _mod(mod_ref, 1, seq)) + _mod(mod_ref, 0, seq)).astype(BF16)
    proj = _dot(h16, w_in_ref[...])
    q = proj[:, 0:QK_WIDTH]
    k = proj[:, QK_WIDTH:2 * QK_WIDTH]
    v = proj[:, 2 * QK_WIDTH:2 * QK_WIDTH + RET_WIDTH]
    g = proj[:, 2 * QK_WIDTH + RET_WIDTH:2 * QK_WIDTH + 2 * RET_WIDTH]
    u = proj[:, 2 * QK_WIDTH + 2 * RET_WIDTH:]

    lane = lax.broadcasted_iota(I32, q.shape, 1)
    first_half = (lane % RET_QK_DIM) < (RET_QK_DIM // 2)
    cos_t = cos_ref[...]
    sin_t = sin_ref[...]

    def rot(t):
        partner = jnp.where(first_half, pltpu.roll(t, QK_WIDTH - RET_QK_DIM // 2, axis=1),
                            pltpu.roll(t, RET_QK_DIM // 2, axis=1))
        return t * cos_t + partner * sin_t

    q = rot(q)
    k = rot(k) * (RET_QK_DIM ** -0.5)
    k_t = k.T
    v16 = v.astype(BF16)
    head_of_lane = lax.broadcasted_iota(I32, (chunk, QK_WIDTH), 1) // RET_QK_DIM

    for c in range(block_l // chunk):
        rows = slice(c * chunk, (c + 1) * chunk)
        q_c = q[rows]
        kt_c = k_t[:, rows]
        kt16 = kt_c.astype(BF16)
        state16 = state_ref[...].astype(BF16)
        for hd in range(RET_HEADS):
            in_head = head_of_lane == hd
            q_h = jnp.where(in_head, q_c, 0.0).astype(BF16)
            v_h = v16[rows, hd * RET_V_DIM:(hd + 1) * RET_V_DIM]
            scores = _dot(q_h, kt16) * dmat_ref[hd]
            inner = _dot(scores.astype(BF16), v_h)
            cross = _dot(q_h, state16) * cross_ref[hd]
            o_ref[rows, hd * RET_V_DIM:(hd + 1) * RET_V_DIM] = inner + cross
            hrows = slice(hd * RET_QK_DIM, (hd + 1) * RET_QK_DIM)
            k_dec = (kt_c[hrows] * tail_ref[hd:hd + 1, :]).astype(BF16)
            state_ref[hrows, :] = state_ref[hrows, :] * cdec_ref[hd] + _dot(k_dec, v_h)

    o_gated = _group_norm_gate(o_ref[...], g)

    ext_ref[POOL_CARRY:POOL_CARRY + block_l, :] = u
    pos = (li * block_l + lax.broadcasted_iota(I32, (block_l, 1), 0)).astype(F32)
    pooled = []
    for gi, (w, acc) in enumerate(zip(POOL_WINDOWS, _window_sums(ext_ref, win_ref, block_l))):
        cnt = jnp.minimum(pos + 1.0, float(w))
        pooled.append(acc / cnt - u[:, gi * POOL_GROUP_DIM:(gi + 1) * POOL_GROUP_DIM])
    p = _pool_project(pooled, w_pool_ref, pool_scale_ref)
    ext_ref[0:POOL_CARRY, :] = ext_ref[block_l:block_l + POOL_CARRY, :]

    x1 = _out_residual(x, o_gated, p, mod_ref, seq, w_out_ref)
    x1_ref[seq] = x1
    _norm_route(x1, mod_ref, seq, norm2_ref, wr_t_ref, bias_t_ref,
                h2_ref.at[seq], experts_ref.at[seq], gatew_ref.at[seq], cnt_ref)


def _decay_tables(chunk):
    f32 = np.float32
    lg = np.log(f32(1.0) - f32(2.0) ** (f32(-5.0) - np.arange(RET_HEADS, dtype=f32))).astype(f32)
    idx = np.arange(chunk, dtype=f32)
    diff = idx[:, None] - idx[None, :]
    causal = diff >= 0
    dmat = np.where(causal[None], np.exp(lg[:, None, None] * np.where(causal, diff, f32(0.0))[None]), f32(0.0))
    cross = np.exp(lg[:, None] * (idx[None, :] + f32(1.0)))
    cross = np.broadcast_to(cross[:, :, None], (RET_HEADS, chunk, RET_V_DIM))
    tail = np.exp(lg[:, None] * (f32(chunk - 1.0) - idx)[None, :])
    cdec = np.broadcast_to(np.exp(lg * f32(chunk))[:, None, None], (RET_HEADS, RET_QK_DIM, RET_V_DIM))
    return tuple(jnp.asarray(np.ascontiguousarray(t, dtype=f32)) for t in (dmat, cross, tail, cdec))


def _rotary_angles(pos):
    half = RET_QK_DIM // 2
    freqs = (np.float32(ROPE_BASE) ** (-np.arange(half, dtype=np.float32) / np.float32(half))).astype(np.float32)
    return np.asarray(pos, np.float32)[:, None] * freqs[None, :]


def _rotary_tables(pos):
    ang = _rotary_angles(pos)
    cos, sin = np.cos(ang), np.sin(ang)
    cos_t = np.tile(np.concatenate([cos, cos], axis=-1), (1, RET_HEADS))
    sin_t = np.tile(np.concatenate([-sin, sin], axis=-1), (1, RET_HEADS))
    return jnp.asarray(cos_t, F32), jnp.asarray(sin_t, F32)


def _full(shape):
    return pl.BlockSpec(shape, lambda *_: (0,) * len(shape))


def _mix_prompt(x, mod, b0, b, norm1, w_in16, w_pool, pool_scale, w_out16, norm2, wr_t, bias_t,
                block_l=512, chunk=256, seqs=2):
    _, l, d = x.shape
    nl = l // block_l
    s0 = b0 // seqs
    cos_t, sin_t = _rotary_tables(np.arange(l))
    dmat, cross, tail, cdec = _decay_tables(chunk)
    kernel = functools.partial(_mix_prompt_kernel, block_l=block_l, chunk=chunk, seqs=seqs)
    tok = lambda bi, li: (bi, li, 0)
    per_seq = lambda bi, li: (bi, 0, 0)
    x1, h2, experts, gate_w, counts, ret, pool = pl.pallas_call(
        kernel,
        grid=(b // seqs, nl),
        in_specs=[pl.BlockSpec((seqs, block_l, d), lambda bi, li: (s0 + bi, li, 0)),
                  pl.BlockSpec((seqs, 6, d), lambda bi, li: (s0 + bi, 0, 0)),
                  _full((1, d)),
                  _full((d, IN_WIDTH)),
                  pl.BlockSpec((block_l, QK_WIDTH), lambda bi, li: (li, 0)),
                  pl.BlockSpec((block_l, QK_WIDTH), lambda bi, li: (li, 0)),
                  _full(dmat.shape), _full(cross.shape), _full(tail.shape), _full(cdec.shape),
                  _full(w_pool.shape), _full((1, POOL_WIDTH)), _full((d, d)), _full((1, d)),
                  _full(wr_t.shape), _full(bias_t.shape)],
        out_specs=[pl.BlockSpec((seqs, block_l, d), tok),
                   pl.BlockSpec((seqs, block_l, HALF), tok),
                   pl.BlockSpec((seqs, TOP_K, block_l), lambda bi, li: (bi, 0, li)),
                   pl.BlockSpec((seqs, block_l, LANES), tok),
                   _full((N_EXPERTS, LANES)),
                   pl.BlockSpec((seqs, RET_HEADS, RET_QK_DIM, RET_V_DIM), lambda bi, li: (bi, 0, 0, 0)),
                   pl.BlockSpec((seqs, POOL_BUF, POOL_WIDTH), per_seq)],
        out_shape=[jax.ShapeDtypeStruct((b, l, d), F32),
                   jax.ShapeDtypeStruct((b, l, HALF), U32),
                   jax.ShapeDtypeStruct((b, TOP_K, l), I32),
                   jax.ShapeDtypeStruct((b, l, LANES), F32),
                   jax.ShapeDtypeStruct((N_EXPERTS, LANES), F32),
                   jax.ShapeDtypeStruct((b, RET_HEADS, RET_QK_DIM, RET_V_DIM), F32),
                   jax.ShapeDtypeStruct((b, POOL_BUF, POOL_WIDTH), F32)],
        scratch_shapes=[pltpu.VMEM((seqs, QK_WIDTH, RET_V_DIM), F32),
                        pltpu.VMEM((seqs, POOL_CARRY + block_l, POOL_WIDTH), F32),
                        pltpu.VMEM((seqs, POOL_CARRY + block_l, POOL_WIDTH), F32),
                        pltpu.VMEM((seqs, block_l, RET_WIDTH), F32)],
        compiler_params=pltpu.CompilerParams(dimension_semantics=("arbitrary", "arbitrary"),
                                             vmem_limit_bytes=VMEM_LIMIT),
        name="mix_prompt",
    )(x, mod, norm1, w_in16, cos_t, sin_t, dmat, cross, tail, cdec, w_pool, pool_scale,
      w_out16, norm2, wr_t, bias_t)
    experts = jnp.transpose(experts, (1, 0, 2)).reshape(TOP_K, b * l)
    return x1, h2.reshape(b * l, HALF), experts, gate_w.reshape(b * l, LANES), counts, ret, pool


def _mix_sample_front_kernel(x_ref, mod_ref, norm1_ref, w_in_ref, cos_ref, sin_ref,
                             qt_ref, kt_ref, v_ref, g_ref, u_ref):
    x = x_ref[...]
    h = _rms(x) * norm1_ref[...] * (1.0 + _mod(mod_ref, 1)) + _mod(mod_ref, 0)
    proj = _dot(h.astype(BF16), w_in_ref[...])
    half = RET_QK_DIM // 2
    cos_c = cos_ref[...]
    sin_c = sin_ref[...]

    def rot_t(t):
        parts = []
        for hd in range(RET_HEADS):
            t1 = t[hd * RET_QK_DIM:hd * RET_QK_DIM + half]
            t2 = t[hd * RET_QK_DIM + half:(hd + 1) * RET_QK_DIM]
            parts += [t1 * cos_c - t2 * sin_c, t1 * sin_c + t2 * cos_c]
        return jnp.concatenate(parts, axis=0)

    qt_ref[...] = rot_t(proj[:, 0:QK_WIDTH].T)
    kt_ref[...] = rot_t(proj[:, QK_WIDTH:2 * QK_WIDTH].T) * (RET_QK_DIM ** -0.5)
    v_ref[...] = proj[:, 2 * QK_WIDTH:2 * QK_WIDTH + RET_WIDTH]
    g_ref[...] = proj[:, 2 * QK_WIDTH + RET_WIDTH:2 * QK_WIDTH + 2 * RET_WIDTH]
    u_ref[...] = proj[:, 2 * QK_WIDTH + 2 * RET_WIDTH:]


def _ret_step_kernel(qt_ref, kt_ref, v_ref, s0_ref, o_ref, s1_ref, *, block_b, decays):
    i = pl.program_id(0)
    lane = lax.broadcasted_iota(I32, qt_ref.shape, 1)
    for j in range(block_b):
        bi = i * block_b + j
        here = lane == bi
        q_col = jnp.sum(jnp.where(here, qt_ref[...], 0.0), axis=1, keepdims=True)
        k_col = jnp.sum(jnp.where(here, kt_ref[...], 0.0), axis=1, keepdims=True)
        v_row = v_ref[pl.ds(bi, 1), :]
        outs = []
        for hd in range(RET_HEADS):
            hrows = slice(hd * RET_QK_DIM, (hd + 1) * RET_QK_DIM)
            s1 = decays[hd] * s0_ref[j, hd] + k_col[hrows] * v_row[:, hd * RET_V_DIM:(hd + 1) * RET_V_DIM]
            s1_ref[j, hd] = s1
            outs.append(jnp.sum(q_col[hrows] * s1, axis=0, keepdims=True))
        o_ref[pl.ds(bi, 1), :] = jnp.concatenate(outs, axis=-1)


def _mix_sample_back_kernel(x_ref, mod_ref, o_ref, g_ref, u_ref, buf_ref, w_pool_ref, pool_scale_ref,
                            w_out_ref, norm2_ref, wr_t_ref, bias_t_ref,
                            x1_ref, h2_ref, experts_ref, gatew_ref, cnt_ref, pool_ref):
    cnt_ref[...] = jnp.zeros_like(cnt_ref)
    o_gated = _group_norm_gate(o_ref[...], g_ref[...])
    u = u_ref[...]
    pooled = []
    for gi, w in enumerate(POOL_WINDOWS):
        lanes = slice(gi * POOL_GROUP_DIM, (gi + 1) * POOL_GROUP_DIM)
        acc = u[:, lanes]
        for j in range(1, w):
            acc = acc + buf_ref[:, POOL_BUF - j, lanes]
        pooled.append(acc / float(w) - u[:, lanes])
    p = _pool_project(pooled, w_pool_ref, pool_scale_ref)
    pool_ref[:, 0:POOL_BUF - 1, :] = buf_ref[:, 1:POOL_BUF, :]
    pool_ref[:, POOL_BUF - 1, :] = u
    x1 = _out_residual(x_ref[...], o_gated, p, mod_ref, 0, w_out_ref)
    x1_ref[...] = x1
    _norm_route(x1, mod_ref, 0, norm2_ref, wr_t_ref, bias_t_ref, h2_ref, experts_ref, gatew_ref, cnt_ref)


def _mix_sample(x, mod, state_ret, state_pool, start, norm1, w_in16, w_pool,
                pool_scale, w_out16, norm2, wr_t, bias_t, block_b=32):
    n, d = x.shape
    half = RET_QK_DIM // 2
    ang = _rotary_angles([start])
    cos_c = jnp.asarray(np.broadcast_to(np.cos(ang).T, (half, n)), F32)
    sin_c = jnp.asarray(np.broadcast_to(np.sin(ang).T, (half, n)), F32)
    params = pltpu.CompilerParams(vmem_limit_bytes=VMEM_LIMIT)
    qt, kt, v, g, u = pl.pallas_call(
        _mix_sample_front_kernel,
        out_shape=[jax.ShapeDtypeStruct((QK_WIDTH, n), F32), jax.ShapeDtypeStruct((QK_WIDTH, n), F32),
                   jax.ShapeDtypeStruct((n, RET_WIDTH), F32), jax.ShapeDtypeStruct((n, RET_WIDTH), F32),
                   jax.ShapeDtypeStruct((n, POOL_WIDTH), F32)],
        compiler_params=params,
        name="mix_sample_front",
    )(x, mod, norm1, w_in16, cos_c, sin_c)

    lg = np.log(1.0 - 2.0 ** (-5.0 - np.arange(RET_HEADS, dtype=np.float32)), dtype=np.float32)
    decays = tuple(float(np.exp(lg[h])) for h in range(RET_HEADS))
    state_block = (block_b, RET_HEADS, RET_QK_DIM, RET_V_DIM)
    o, s1 = pl.pallas_call(
        functools.partial(_ret_step_kernel, block_b=block_b, decays=decays),
        grid=(n // block_b,),
        in_specs=[_full((QK_WIDTH, n)), _full((QK_WIDTH, n)), _full((n, RET_WIDTH)),
                  pl.BlockSpec(state_block, lambda i: (i, 0, 0, 0))],
        out_specs=[_full((n, RET_WIDTH)), pl.BlockSpec(state_block, lambda i: (i, 0, 0, 0))],
        out_shape=[jax.ShapeDtypeStruct((n, RET_WIDTH), F32),
                   jax.ShapeDtypeStruct(state_ret.shape, F32)],
        compiler_params=pltpu.CompilerParams(dimension_semantics=("arbitrary",),
                                             vmem_limit_bytes=VMEM_LIMIT),
        name="ret_step",
    )(qt, kt, v, state_ret)

    x1, h2, experts, gate_w, counts, pool = pl.pallas_call(
        _mix_sample_back_kernel,
        out_shape=[jax.ShapeDtypeStruct((n, d), F32),
                   jax.ShapeDtypeStruct((n, HALF), U32),
                   jax.ShapeDtypeStruct((TOP_K, n), I32),
                   jax.ShapeDtypeStruct((n, LANES), F32),
                   jax.ShapeDtypeStruct((N_EXPERTS, LANES), F32),
                   jax.ShapeDtypeStruct(state_pool.shape, F32)],
        compiler_params=params,
        name="mix_sample_back",
    )(x, mod, o, g, u, state_pool, w_pool, pool_scale, w_out16, norm2, wr_t, bias_t)
    return x1, h2, experts, gate_w, counts, s1, pool


def _plan_kernel(experts_ref, cnt_ref, pos_ref, meta_ref, carry_ref, off_ref, *, block_t):
    @pl.when(pl.program_id(0) == 0)
    def _():
        cnt = cnt_ref[...]
        n_tile = jnp.floor((cnt + (ROW_TILE - 1.0)) * (1.0 / ROW_TILE))
        upto = (lax.broadcasted_iota(I32, (N_EXPERTS, N_EXPERTS), 1)
                <= lax.broadcasted_iota(I32, (N_EXPERTS, N_EXPERTS), 0))
        tile_end = _dot(jnp.where(upto, 1.0, 0.0).astype(BF16), n_tile.astype(BF16))
        tile_start = tile_end - n_tile
        off_ref[...] = tile_start * ROW_TILE
        carry_ref[...] = jnp.zeros_like(carry_ref)
        lane = lax.broadcasted_iota(I32, cnt.shape, 1)
        meta_ref[...] = jnp.where(lane == 0, tile_start, jnp.where(lane == 1, n_tile, cnt)).astype(I32)

    e_blk = experts_ref[...]
    eidx = lax.broadcasted_iota(I32, (N_EXPERTS, block_t), 0)
    member = jnp.zeros((N_EXPERTS, block_t), F32)
    for s in range(TOP_K):
        member = member + jnp.where(eidx == e_blk[s:s + 1, :], 1.0, 0.0)
    before = (lax.broadcasted_iota(I32, (block_t, block_t), 0)
              < lax.broadcasted_iota(I32, (block_t, block_t), 1))
    rank = _dot(member.astype(BF16), jnp.where(before, 1.0, 0.0).astype(BF16))
    row = off_ref[:, 0:1] + carry_ref[:, 0:1] + rank
    carry_ref[...] += jnp.broadcast_to(jnp.sum(member, axis=1, keepdims=True), (N_EXPERTS, LANES))
    out = [jnp.sum(jnp.where(eidx == e_blk[s:s + 1, :], row, 0.0), axis=0, keepdims=True)
           for s in range(TOP_K)]
    pos_ref[...] = jnp.concatenate(out, axis=0).astype(I32)


def _plan(experts_all, counts, max_block=1024):
    n_tokens = experts_all.shape[1]
    block_t = max(k for k in range(LANES, max_block + 1, LANES) if n_tokens % k == 0)
    return pl.pallas_call(
        functools.partial(_plan_kernel, block_t=block_t),
        grid=(n_tokens // block_t,),
        in_specs=[pl.BlockSpec((TOP_K, block_t), lambda j: (0, j)), _full((N_EXPERTS, LANES))],
        out_specs=[pl.BlockSpec((TOP_K, block_t), lambda j: (0, j)), _full((N_EXPERTS, LANES))],
        out_shape=[jax.ShapeDtypeStruct((TOP_K, n_tokens), I32),
                   jax.ShapeDtypeStruct((N_EXPERTS, LANES), I32)],
        scratch_shapes=[pltpu.VMEM((N_EXPERTS, LANES), F32)] * 2,
        compiler_params=pltpu.CompilerParams(dimension_semantics=("arbitrary",),
                                             vmem_limit_bytes=VMEM_LIMIT),
        name="plan",
    )(experts_all, counts)


def _sc_workers():
    info = plsc.get_sparse_core_info()
    return info.num_cores, info.num_cores * info.num_subcores


def _sc_scatter_rows(sources, pos_t, n_out, after=()):
    w = sources[0].shape[1]
    s = pos_t.shape[0]
    n_cores, n_workers = _sc_workers()
    bounds = np.cumsum([0] + [src.shape[0] // SC_CHUNK for src in sources])
    n_chunks = int(bounds[-1])
    iters = -(-n_chunks // n_workers)
    mesh = plsc.VectorSubcoreMesh(core_axis_name="c", subcore_axis_name="s")

    @functools.partial(
        pl.kernel, mesh=mesh, out_type=jax.ShapeDtypeStruct((n_out, w), sources[0].dtype),
        scratch_types=[pltpu.VMEM((SC_CHUNK, w), sources[0].dtype), pltpu.VMEM((s, SC_CHUNK), I32),
                       pltpu.SemaphoreType.DMA],
        name="dispatch")
    def k(*refs):
        src_hbm, pos_hbm = refs[:len(sources)], refs[len(sources)]
        out_hbm, rows_v, idx_v, sem = refs[len(sources) + 1 + len(after):]
        wid = lax.axis_index("s") * n_cores + lax.axis_index("c")

        @pl.loop(0, iters)
        def _(it):
            c = it * n_workers + wid
            for src, lo, hi in zip(src_hbm, bounds[:-1], bounds[1:]):
                @pl.when((c >= int(lo)) & (c < int(hi)))
                def _():
                    base = pl.multiple_of((c - int(lo)) * SC_CHUNK, SC_CHUNK)
                    pltpu.sync_copy(src.at[pl.ds(base, SC_CHUNK)], rows_v)

            @pl.when(c < n_chunks)
            def _():
                base = pl.multiple_of(c * SC_CHUNK, SC_CHUNK)
                pltpu.sync_copy(pos_hbm.at[:, pl.ds(base, SC_CHUNK)], idx_v)
                copies = [pltpu.async_copy(rows_v, out_hbm.at[idx_v.at[j]], sem) for j in range(s)]
                for cp in copies:
                    cp.wait()

    return k(*sources, pos_t, *after)


def _sc_gather_rows(table, pos_t):
    _, w = table.shape
    s, t = pos_t.shape
    n_cores, n_workers = _sc_workers()
    n_chunks = t // SC_CHUNK
    iters = -(-n_chunks // n_workers)
    mesh = plsc.VectorSubcoreMesh(core_axis_name="c", subcore_axis_name="s")

    @functools.partial(
        pl.kernel, mesh=mesh, out_type=jax.ShapeDtypeStruct((s, t, w), table.dtype),
        scratch_types=[pltpu.VMEM((SC_CHUNK, w), table.dtype), pltpu.VMEM((s, SC_CHUNK), I32),
                       pltpu.SemaphoreType.DMA],
        name="combine")
    def k(table_hbm, pos_hbm, out_hbm, rows_v, idx_v, sem):
        wid = lax.axis_index("s") * n_cores + lax.axis_index("c")

        @pl.loop(0, iters)
        def _(it):
            c = it * n_workers + wid

            @pl.when(c < n_chunks)
            def _():
                base = pl.multiple_of(c * SC_CHUNK, SC_CHUNK)
                pltpu.sync_copy(pos_hbm.at[:, pl.ds(base, SC_CHUNK)], idx_v)
                for j in range(s):
                    pltpu.async_copy(table_hbm.at[idx_v.at[j]], rows_v, sem).wait()
                    pltpu.sync_copy(rows_v, out_hbm.at[j, pl.ds(base, SC_CHUNK)])

    return k(table, pos_t)


def _sc_pack_weights(w, rows_per_item, after=()):
    e, r, c = w.shape
    half = r // 2
    rb = rows_per_item
    per_expert = half // rb
    n_cores, n_workers = _sc_workers()
    per_worker = e * per_expert // n_workers
    assert per_worker * n_workers == e * per_expert and per_worker % 2 == 0 and c % (SC_LANES * SC_UNROLL) == 0
    mesh = plsc.VectorSubcoreMesh(core_axis_name="c", subcore_axis_name="s")

    @functools.partial(
        pl.kernel, mesh=mesh, out_type=jax.ShapeDtypeStruct((e * half, c), U32),
        scratch_types=[pltpu.VMEM((2, rb, c), F32), pltpu.VMEM((2, rb, c), F32), pltpu.VMEM((2, rb, c), U32),
                       pltpu.SemaphoreType.DMA((2,)), pltpu.SemaphoreType.DMA((2,))],
        compiler_params=pltpu.CompilerParams(needs_layout_passes=False),
        name="pack_weights")
    def k(w_hbm, *refs):
        out_hbm, lo_v, hi_v, out_v, in_sem, out_sem = refs[len(after):]
        wid = lax.axis_index("s") * n_cores + lax.axis_index("c")
        first = wid * per_worker

        def rows(item):
            ex = item // per_expert
            j = item - ex * per_expert
            return (pl.multiple_of(ex * r + j * rb, rb), pl.multiple_of(ex * r + half + j * rb, rb),
                    pl.multiple_of(ex * half + j * rb, rb))

        def loads(item, b):
            lo_row, hi_row, _ = rows(item)
            return (pltpu.make_async_copy(w_hbm.at[pl.ds(lo_row, rb)], lo_v.at[b], in_sem.at[b]),
                    pltpu.make_async_copy(w_hbm.at[pl.ds(hi_row, rb)], hi_v.at[b], in_sem.at[b]))

        def store(item, b):
            return pltpu.make_async_copy(out_v.at[b], out_hbm.at[pl.ds(rows(item)[2], rb)], out_sem.at[b])

        for cp in loads(first, 0):
            cp.start()

        @pl.loop(0, per_worker // 2)
        def _(pair):
            for b in range(2):
                item = first + pair * 2 + b
                for cp in loads(item, b):
                    cp.wait()

                @pl.when(item + 1 < first + per_worker)
                def _():
                    for cp in loads(item + 1, 1 - b):
                        cp.start()

                @pl.when(pair > 0)
                def _():
                    store(item - 2, b).wait()

                @pl.loop(0, rb)
                def _(i):
                    @pl.loop(0, c // (SC_LANES * SC_UNROLL))
                    def _(vb):
                        for u in range(SC_UNROLL):
                            sl = pl.ds(pl.multiple_of((vb * SC_UNROLL + u) * SC_LANES, SC_LANES), SC_LANES)
                            packed = plsc.pack(lo_v[b, i, sl], hi_v[b, i, sl], format=plsc.PackFormat.INTERLEAVED)
                            out_v[b, i, sl] = plsc.bitcast(packed, U32)

                store(item, b).start()

        for b in range(2):
            store(first + per_worker - 2 + b, b).wait()

    return k(w.reshape(e * r, c), *after).reshape(e, half, c)


def _experts_kernel(first_ref, ntile_ref, cnt_ref, xs_hbm, wg_ref, wu_ref, wd_ref, ys_hbm,
                    wg16_ref, wu16_ref, wd16_ref, x_buf, y_buf, in_sem, out_sem):
    ahead = STREAM_DEPTH - MAX_WIDTH
    e = pl.program_id(0)
    n_used = first_ref[N_EXPERTS - 1] + ntile_ref[N_EXPERTS - 1]
    first, n_mine, count = first_ref[e], ntile_ref[e], cnt_ref[e]

    def tile_rows(g):
        return pl.ds(pl.multiple_of(g * ROW_TILE, ROW_TILE), ROW_TILE)

    def load(g):
        slot = lax.rem(g, STREAM_DEPTH)
        return pltpu.make_async_copy(xs_hbm.at[tile_rows(g)], x_buf.at[slot], in_sem.at[slot])

    def store(g):
        slot = lax.rem(g, STREAM_DEPTH)
        return pltpu.make_async_copy(y_buf.at[slot], ys_hbm.at[tile_rows(g)], out_sem.at[slot])

    @pl.when(e == 0)
    def _():
        for g0 in range(ahead):
            @pl.when(g0 < n_used)
            def _():
                load(g0).start()

    for packed_ref, w16_ref in ((wg_ref, wg16_ref), (wu_ref, wu16_ref), (wd_ref, wd16_ref)):
        rows = packed_ref.shape[1]
        lo, hi = _unpack_rows(packed_ref[0])
        w16_ref[0:rows, :] = lo.astype(BF16)
        w16_ref[rows:, :] = hi.astype(BF16)

    def run(j, width):
        tiles = [first + j + t for t in range(width)]
        for g in tiles:
            load(g).wait()

            @pl.when(g + ahead < n_used)
            def _():
                load(g + ahead).start()

            @pl.when(g >= STREAM_DEPTH)
            def _():
                store(g - STREAM_DEPTH).wait()

        words = jnp.concatenate([x_buf[lax.rem(g, STREAM_DEPTH)] for g in tiles], axis=0)
        row = lax.broadcasted_iota(I32, words.shape, 0)
        words = jnp.where(row < count - j * ROW_TILE, words, jnp.uint32(0))
        lo, hi = _unpack_rows(words)
        lo, hi = lo.astype(BF16), hi.astype(BF16)
        hg = _dot(lo, wg16_ref[0:HALF, :]) + _dot(hi, wg16_ref[HALF:, :])
        hu = _dot(lo, wu16_ref[0:HALF, :]) + _dot(hi, wu16_ref[HALF:, :])
        a = (_silu(hg) * hu).astype(BF16)
        part = HALF // 2
        for c in range(2):
            y = _pack_pair(_dot(a, wd16_ref[:, c * part:(c + 1) * part]),
                           _dot(a, wd16_ref[:, HALF + c * part:HALF + (c + 1) * part]))
            for t, g in enumerate(tiles):
                y_buf[lax.rem(g, STREAM_DEPTH), :, c * part:(c + 1) * part] = y[t * ROW_TILE:(t + 1) * ROW_TILE]
        for g in tiles:
            store(g).start()

    def widest(p, carry):
        run(MAX_WIDTH * p, MAX_WIDTH)
        return carry

    lax.fori_loop(0, n_mine // MAX_WIDTH, widest, 0)
    done = n_mine - lax.rem(n_mine, MAX_WIDTH)
    width = MAX_WIDTH // 2
    while width:
        has = lax.rem(n_mine // width, 2) == 1

        @pl.when(has)
        def _(width=width, done=done):
            run(done, width)

        done = done + jnp.where(has, width, 0)
        width //= 2

    @pl.when(e == N_EXPERTS - 1)
    def _():
        for back in range(STREAM_DEPTH, 0, -1):
            @pl.when(n_used >= back)
            def _():
                store(n_used - back).wait()


def _experts(xs, first_tile, n_tile, count, w_eg, w_eu, w_ed):
    d = D_MODEL
    by_expert = lambda e, *_: (e, 0, 0)
    grid_spec = pltpu.PrefetchScalarGridSpec(
        num_scalar_prefetch=3,
        grid=(N_EXPERTS,),
        in_specs=[pl.BlockSpec(memory_space=pl.ANY),
                  pl.BlockSpec((1, d // 2, EXPERT_DIM), by_expert),
                  pl.BlockSpec((1, d // 2, EXPERT_DIM), by_expert),
                  pl.BlockSpec((1, EXPERT_DIM // 2, d), by_expert)],
        out_specs=pl.BlockSpec(memory_space=pl.ANY),
        scratch_shapes=[pltpu.VMEM((d, EXPERT_DIM), BF16), pltpu.VMEM((d, EXPERT_DIM), BF16),
                        pltpu.VMEM((EXPERT_DIM, d), BF16),
                        pltpu.VMEM((STREAM_DEPTH, ROW_TILE, HALF), U32),
                        pltpu.VMEM((STREAM_DEPTH, ROW_TILE, HALF), U32),
                        pltpu.SemaphoreType.DMA((STREAM_DEPTH,)), pltpu.SemaphoreType.DMA((STREAM_DEPTH,))])
    return pl.pallas_call(
        _experts_kernel,
        grid_spec=grid_spec,
        out_shape=jax.ShapeDtypeStruct(xs.shape, U32),
        compiler_params=pltpu.CompilerParams(dimension_semantics=("arbitrary",),
                                             vmem_limit_bytes=VMEM_LIMIT),
        name="experts",
    )(first_tile, n_tile, count, xs, w_eg, w_eu, w_ed)


def _final_kernel(z_ref, gatew_ref, h2_ref, x1_ref, mod_ref, normf_ref, wsg_ref, wsu_ref, wsd_ref, *rest):
    y_ref, wsg16_ref, wsu16_ref, wsd16_ref = rest[-4:]

    @pl.when(pl.program_id(0) == 0)
    def _():
        wsg16_ref[...] = wsg_ref[...].astype(BF16)
        wsu16_ref[...] = wsu_ref[...].astype(BF16)
        wsd16_ref[...] = wsd_ref[...].astype(BF16)

    lo, hi = _unpack_rows(h2_ref[...])
    h = jnp.concatenate([lo, hi], axis=-1).astype(BF16)
    a = _silu(_dot(h, wsg16_ref[...])) * _dot(h, wsu16_ref[...])
    acc = _dot(a.astype(BF16), wsd16_ref[...])
    for s in range(TOP_K):
        lo, hi = _unpack_rows(z_ref[s])
        acc = acc + gatew_ref[:, s:s + 1] * jnp.concatenate([lo, hi], axis=-1)
    x2 = x1_ref[...] + _mod(mod_ref, 5) * acc
    y_ref[...] = _rms(x2) * normf_ref[...]


def _final(z, gate_w, h2, x1, mod, norm_f, w_sg, w_su, w_sd, block_t, first_block, per_seq,
           seq0=0, out_rows=None, y_prev=None):
    t, d = x1.shape
    out_rows = t if out_rows is None else out_rows
    out_first = seq0 * per_seq
    tok = lambda i: (i, 0)
    if per_seq:
        mod_spec = pl.BlockSpec((1, 6, d), lambda i: (seq0 + i // per_seq, 0, 0))
    else:
        mod_spec = pl.BlockSpec((block_t, 6 * d), tok)
    operands = [z, gate_w, h2, x1, mod, norm_f, w_sg, w_su, w_sd]
    in_specs = [pl.BlockSpec((TOP_K, block_t, HALF), lambda i: (0, first_block + i, 0)),
                pl.BlockSpec((block_t, LANES), tok),
                pl.BlockSpec((block_t, HALF), tok),
                pl.BlockSpec((block_t, d), tok),
                mod_spec,
                _full((1, d)),
                _full((d, EXPERT_DIM)), _full((d, EXPERT_DIM)), _full((EXPERT_DIM, d))]
    aliases = {}
    if y_prev is not None:
        aliases = {len(operands): 0}
        operands.append(y_prev)
        in_specs.append(pl.BlockSpec(memory_space=pl.ANY))
    return pl.pallas_call(
        _final_kernel,
        grid=(t // block_t,),
        in_specs=in_specs,
        out_specs=pl.BlockSpec((block_t, d), lambda i: (out_first + i, 0)),
        out_shape=jax.ShapeDtypeStruct((out_rows, d), F32),
        scratch_shapes=[pltpu.VMEM((d, EXPERT_DIM), BF16), pltpu.VMEM((d, EXPERT_DIM), BF16),
                        pltpu.VMEM((EXPERT_DIM, d), BF16)],
        input_output_aliases=aliases,
        compiler_params=pltpu.CompilerParams(dimension_semantics=("arbitrary",),
                                             vmem_limit_bytes=VMEM_LIMIT),
        name="final",
    )(*operands)


def kernel(x_prompt, x_sample, c_prompt, c_sample, state_ret, state_pool, norm1, norm2, norm_f,
           w_ada, b_ada, w_in, w_out, w_pool, pool_scale, w_router, router_bias, w_exp_gate,
           w_exp_up, w_exp_down, w_sh_gate, w_sh_up, w_sh_down):
    b, l, d = x_prompt.shape
    n = x_sample.shape[0]

    mod_p, mod_s = _ada(c_prompt, c_sample, w_ada[0], b_ada[0])
    mod_p = mod_p.reshape(b, 6, d)

    w_in16 = w_in[0].astype(BF16)
    w_out16 = w_out[0].astype(BF16)
    wr_t = w_router[0].T
    bias_t = jnp.broadcast_to(router_bias[0][:, None], (N_EXPERTS, LANES))
    n1, n2, nf = norm1[0].reshape(1, d), norm2[0].reshape(1, d), norm_f.reshape(1, d)
    ps = pool_scale[0].reshape(1, POOL_WIDTH)
    shared = (w_sh_gate[0], w_sh_up[0], w_sh_down[0])

    def routed(sources, experts, counts, repacked):
        n_tiles = experts.shape[1] * TOP_K // ROW_TILE + N_EXPERTS
        pos_t, meta = _plan(experts, counts)
        xs = _sc_scatter_rows(sources, pos_t, n_tiles * ROW_TILE, after=repacked)
        ys = _experts(xs, meta[:, 0], meta[:, 1], meta[:, 2], *expert_w)
        return _sc_gather_rows(ys, pos_t)

    mix_args = (n1, w_in16, w_pool[0], ps, w_out16, n2, wr_t, bias_t)
    x1_s, h2_s, experts_s, gatew_s, counts_s, ret_s, pool_s = _mix_sample(
        x_sample.reshape(n, d), mod_s, state_ret[0], state_pool[0], float(PAST_LEN), *mix_args)

    expert_w = (_sc_pack_weights(w_exp_gate[0], 64, after=(x1_s,)),
                _sc_pack_weights(w_exp_up[0], 64, after=(x1_s,)),
                _sc_pack_weights(w_exp_down[0], 16, after=(x1_s,)))

    ba = b // 2
    bb = b - ba
    x1_a, h2_a, experts_a, gatew_a, counts_a, ret_a, pool_a = _mix_prompt(x_prompt, mod_p, 0, ba, *mix_args)
    z_a = routed((h2_a,), experts_a, counts_a, expert_w)
    x1_b, h2_b, experts_b, gatew_b, counts_b, ret_b, pool_b = _mix_prompt(x_prompt, mod_p, ba, bb, *mix_args)
    z_b = routed((h2_b, h2_s), jnp.concatenate([experts_b, experts_s], axis=1), counts_b + counts_s,
                 expert_w[:2])

    block_t = 512
    per_seq = l // block_t
    y_s = _final(z_b, gatew_s, h2_s, x1_s, mod_s, nf, *shared,
                 block_t=n, first_block=bb * l // n, per_seq=0)
    y_p = _final(z_b, gatew_b, h2_b, x1_b.reshape(bb * l, d), mod_p, nf, *shared,
                 block_t=block_t, first_block=0, per_seq=per_seq, seq0=ba, out_rows=b * l)
    y_p = _final(z_a, gatew_a, h2_a, x1_a.reshape(ba * l, d), mod_p, nf, *shared,
                 block_t=block_t, first_block=0, per_seq=per_seq, seq0=0, out_rows=b * l, y_prev=y_p)

    ret_p = jnp.concatenate([ret_a, ret_b], axis=0)
    pool_p = jnp.concatenate([pool_a, pool_b], axis=0)
    return (y_p.reshape(b, l, d), y_s.reshape(n, 1, d), ret_p[None], pool_p[None],
            ret_s[None], pool_s[None])
```
